```python
import math
import jax, jax.numpy as jnp
from jax import lax
import numpy as np

D_MODEL = 1024
BATCH = 8
SEQ = 2048
DEPTH = 4

N_MIXERS = 2
N_HEADS = 16
HEAD_DIM = D_MODEL // N_HEADS
CONV_WIDTH = 3
BRANCHES = ((128, 1), (512, 4), (2048, 16))
NUM_BUCKETS = 32
MAX_DISTANCE = 2048
D_FF = ((8 * D_MODEL // 3 + 255) // 256) * 256
EPS = 1e-6
NEG_INF = -1e30

kernel_name = "hybrid_shortconv_dilated_attn_swiglu"


def rms_norm(x, g):
    xf = x.astype(jnp.float32)
    y = xf * lax.rsqrt(jnp.mean(xf * xf, axis=-1, keepdims=True) + EPS)
    return (y * g.astype(jnp.float32)).astype(x.dtype)


def t5_bucket(dist):
    exact = NUM_BUCKETS // 2
    df = jnp.maximum(dist, 1).astype(jnp.float32)
    large = exact + (jnp.log(df / exact) / math.log(MAX_DISTANCE / exact)
                     * (NUM_BUCKETS - exact)).astype(jnp.int32)
    large = jnp.minimum(large, NUM_BUCKETS - 1)
    return jnp.where(dist < exact, dist, large)


def short_conv_mixer(h, w_in, conv_k, w_out):
    b, c, u = jnp.split(h @ w_in, 3, axis=-1)
    y = lax.conv_general_dilated(
        c * u, conv_k[:, None, :].astype(u.dtype),
        window_strides=(1,), padding=[(CONV_WIDTH - 1, 0)],
        dimension_numbers=("NWC", "WIO", "NWC"),
        feature_group_count=D_MODEL)
    return (b * y) @ w_out


def dilated_branch(q, k, v, rel_bias, window, dilation):
    bsz, seq, nh, dh = q.shape
    blk = window // dilation
    L = seq // dilation
    nb = -(-L // blk)
    Lp = nb * blk

    def sub(t):
        t = t.reshape(bsz, L, dilation, nh, dh).transpose(0, 2, 1, 3, 4)
        t = jnp.pad(t, ((0, 0), (0, 0), (0, Lp - L), (0, 0), (0, 0)))
        return t.reshape(bsz, dilation, nb, blk, nh, dh)

    def with_prev(t):
        prev = jnp.pad(t, ((0, 0), (0, 0), (1, 0), (0, 0), (0, 0), (0, 0)))[:, :, :-1]
        return jnp.concatenate([prev, t], axis=3)

    qb = sub(q)
    kk = with_prev(sub(k))
    vv = with_prev(sub(v))

    s = jnp.einsum("brnqhd,brnkhd->brnhqk", qb, kk,
                   preferred_element_type=jnp.float32) * (HEAD_DIM ** -0.5)
    qi = jnp.arange(blk)[:, None]
    ki = jnp.arange(2 * blk)[None, :]
    rel = qi + blk - ki
    band = (rel >= 0) & (rel <= blk)
    valid_start = (jnp.arange(nb)[:, None, None] > 0) | (ki >= blk)[None]
    mask = band[None] & valid_start
    bias = rel_bias[t5_bucket(jnp.clip(rel, 0) * dilation)]
    s = s + bias.transpose(2, 0, 1).astype(jnp.float32)
    s = jnp.where(mask[:, None], s, NEG_INF)
    lse = jax.nn.logsumexp(s, axis=-1)
    p = jnp.exp(s - lse[..., None])
    o = jnp.einsum("brnhqk,brnkhd->brnqhd", p.astype(v.dtype), vv,
                   preferred_element_type=jnp.float32)

    o = o.reshape(bsz, dilation, Lp, nh, dh)[:, :, :L]
    o = o.transpose(0, 2, 1, 3, 4).reshape(bsz, seq, nh, dh)
    lse = lse.transpose(0, 1, 2, 4, 3).reshape(bsz, dilation, Lp, nh)[:, :, :L]
    lse = lse.transpose(0, 2, 1, 3).reshape(bsz, seq, nh)
    return o, lse


def dilated_attention_mixer(h, w_qkv, w_out, rel_bias):
    bsz, seq, _ = h.shape
    qkv = (h @ w_qkv).reshape(bsz, seq, 3, N_HEADS, HEAD_DIM)
    q, k, v = qkv[:, :, 0], qkv[:, :, 1], qkv[:, :, 2]
    outs, lses = [], []
    for window, dilation in BRANCHES:
        o, l = dilated_branch(q, k, v, rel_bias, window, dilation)
        outs.append(o)
        lses.append(l)
    alpha = jax.nn.softmax(jnp.stack(lses, axis=0), axis=0)
    o = jnp.sum(alpha[..., None] * jnp.stack(outs, axis=0), axis=0)
    return o.reshape(bsz, seq, D_MODEL).astype(h.dtype) @ w_out


def swiglu(h, w_gate, w_up, w_down):
    return (jax.nn.silu(h @ w_gate) * (h @ w_up)) @ w_down


def _fwd_setup_inputs(seed: int = 0) -> dict:
    key = jax.random.key(seed)
    ks = jax.random.split(key, 14)
    n_conv = (DEPTH + 1) // 2
    n_attn = DEPTH // 2
    f32 = jnp.float32
    nrm = lambda k, shape, scale: jax.random.normal(k, shape, f32) * scale
    return {
        "x": nrm(ks[0], (BATCH, SEQ, D_MODEL), 1.0),
        "mix_norm": 1.0 + nrm(ks[1], (DEPTH, D_MODEL), 0.05),
        "ffn_norm": 1.0 + nrm(ks[2], (DEPTH, D_MODEL), 0.05),
        "final_norm": 1.0 + nrm(ks[3], (D_MODEL,), 0.05),
        "conv_w_in": nrm(ks[4], (n_conv, D_MODEL, 3 * D_MODEL), D_MODEL ** -0.5),
        "conv_kernel": nrm(ks[5], (n_conv, CONV_WIDTH, D_MODEL), CONV_WIDTH ** -0.5),
        "conv_w_out": nrm(ks[6], (n_conv, D_MODEL, D_MODEL), D_MODEL ** -0.5),
        "attn_w_qkv": nrm(ks[7], (n_attn, D_MODEL, 3 * D_MODEL), D_MODEL ** -0.5),
        "attn_w_out": nrm(ks[8], (n_attn, D_MODEL, D_MODEL), D_MODEL ** -0.5),
        "rel_bias": nrm(ks[9], (NUM_BUCKETS, N_HEADS), 0.2),
        "ffn_w_gate": nrm(ks[10], (DEPTH, D_MODEL, D_FF), D_MODEL ** -0.5),
        "ffn_w_up": nrm(ks[11], (DEPTH, D_MODEL, D_FF), D_MODEL ** -0.5),
        "ffn_w_down": nrm(ks[12], (DEPTH, D_FF, D_MODEL), D_FF ** -0.5),
    }


def _fwd_reference(x, mix_norm, ffn_norm, final_norm, conv_w_in, conv_kernel, conv_w_out,
              attn_w_qkv, attn_w_out, rel_bias, ffn_w_gate, ffn_w_up, ffn_w_down):
    for i in range(DEPTH):
        h = rms_norm(x, mix_norm[i])
        j = i // N_MIXERS
        if i % N_MIXERS == 0:
            x = x + short_conv_mixer(h, conv_w_in[j], conv_kernel[j], conv_w_out[j])
        else:
            x = x + dilated_attention_mixer(h, attn_w_qkv[j], attn_w_out[j], rel_bias)
        h = rms_norm(x, ffn_norm[i])
        x = x + swiglu(h, ffn_w_gate[i], ffn_w_up[i], ffn_w_down[i])
    return rms_norm(x, final_norm)


import jax as _jax
import jax.numpy as _jnp

TWIN_FORMAT = 'train_step'
FWD_PARAMS = ['x', 'mix_norm', 'ffn_norm', 'final_norm', 'conv_w_in', 'conv_kernel', 'conv_w_out', 'attn_w_qkv', 'attn_w_out', 'rel_bias', 'ffn_w_gate', 'ffn_w_up', 'ffn_w_down']
TWIN_WEIGHTS = ['mix_norm', 'ffn_norm', 'final_norm', 'conv_w_in', 'conv_kernel', 'conv_w_out', 'attn_w_qkv', 'attn_w_out', 'rel_bias', 'ffn_w_gate', 'ffn_w_up', 'ffn_w_down']
TWIN_DIFF_INPUT = 'x'
TWIN_INPUTS = ['x', 'mix_norm', 'ffn_norm', 'final_norm', 'conv_w_in', 'conv_kernel', 'conv_w_out', 'attn_w_qkv', 'attn_w_out', 'rel_bias', 'ffn_w_gate', 'ffn_w_up', 'ffn_w_down', 'loss_target', 'm_mix_norm', 'm_ffn_norm', 'm_final_norm', 'm_conv_w_in', 'm_conv_kernel', 'm_conv_w_out', 'm_attn_w_qkv', 'm_attn_w_out', 'm_rel_bias', 'm_ffn_w_gate', 'm_ffn_w_up', 'm_ffn_w_down', 'v_mix_norm', 'v_ffn_norm', 'v_final_norm', 'v_conv_w_in', 'v_conv_kernel', 'v_conv_w_out', 'v_attn_w_qkv', 'v_attn_w_out', 'v_rel_bias', 'v_ffn_w_gate', 'v_ffn_w_up', 'v_ffn_w_down']
TWIN_OUTPUTS = ['loss', 'grad_x', 'grad_mix_norm', 'grad_ffn_norm', 'grad_final_norm', 'grad_conv_w_in', 'grad_conv_kernel', 'grad_conv_w_out', 'grad_attn_w_qkv', 'grad_attn_w_out', 'grad_rel_bias', 'grad_ffn_w_gate', 'grad_ffn_w_up', 'grad_ffn_w_down', 'delta_mix_norm', 'delta_ffn_norm', 'delta_final_norm', 'delta_conv_w_in', 'delta_conv_kernel', 'delta_conv_w_out', 'delta_attn_w_qkv', 'delta_attn_w_out', 'delta_rel_bias', 'delta_ffn_w_gate', 'delta_ffn_w_up', 'delta_ffn_w_down', 'new_m_mix_norm', 'new_m_ffn_norm', 'new_m_final_norm', 'new_m_conv_w_in', 'new_m_conv_kernel', 'new_m_conv_w_out', 'new_m_attn_w_qkv', 'new_m_attn_w_out', 'new_m_rel_bias', 'new_m_ffn_w_gate', 'new_m_ffn_w_up', 'new_m_ffn_w_down', 'new_v_mix_norm', 'new_v_ffn_norm', 'new_v_final_norm', 'new_v_conv_w_in', 'new_v_conv_kernel', 'new_v_conv_w_out', 'new_v_attn_w_qkv', 'new_v_attn_w_out', 'new_v_rel_bias', 'new_v_ffn_w_gate', 'new_v_ffn_w_up', 'new_v_ffn_w_down']
TWIN_LEAF_KINDS = {'loss': 'loss', 'grad_x': 'grad_x', 'grad_mix_norm': 'grad_w', 'grad_ffn_norm': 'grad_w', 'grad_final_norm': 'grad_w', 'grad_conv_w_in': 'grad_w', 'grad_conv_kernel': 'grad_w', 'grad_conv_w_out': 'grad_w', 'grad_attn_w_qkv': 'grad_w', 'grad_attn_w_out': 'grad_w', 'grad_rel_bias': 'grad_w', 'grad_ffn_w_gate': 'grad_w', 'grad_ffn_w_up': 'grad_w', 'grad_ffn_w_down': 'grad_w', 'delta_mix_norm': 'delta_w', 'delta_ffn_norm': 'delta_w', 'delta_final_norm': 'delta_w', 'delta_conv_w_in': 'delta_w', 'delta_conv_kernel': 'delta_w', 'delta_conv_w_out': 'delta_w', 'delta_attn_w_qkv': 'delta_w', 'delta_attn_w_out': 'delta_w', 'delta_rel_bias': 'delta_w', 'delta_ffn_w_gate': 'delta_w', 'delta_ffn_w_up': 'delta_w', 'delta_ffn_w_down': 'delta_w', 'new_m_mix_norm': 'new_m', 'new_m_ffn_norm': 'new_m', 'new_m_final_norm': 'new_m', 'new_m_conv_w_in': 'new_m', 'new_m_conv_kernel': 'new_m', 'new_m_conv_w_out': 'new_m', 'new_m_attn_w_qkv': 'new_m', 'new_m_attn_w_out': 'new_m', 'new_m_rel_bias': 'new_m', 'new_m_ffn_w_gate': 'new_m', 'new_m_ffn_w_up': 'new_m', 'new_m_ffn_w_down': 'new_m', 'new_v_mix_norm': 'new_v', 'new_v_ffn_norm': 'new_v', 'new_v_final_norm': 'new_v', 'new_v_conv_w_in': 'new_v', 'new_v_conv_kernel': 'new_v', 'new_v_conv_w_out': 'new_v', 'new_v_attn_w_qkv': 'new_v', 'new_v_attn_w_out': 'new_v', 'new_v_rel_bias': 'new_v', 'new_v_ffn_w_gate': 'new_v', 'new_v_ffn_w_up': 'new_v', 'new_v_ffn_w_down': 'new_v'}


def _forward(args):
    return _fwd_reference(*[args[k] for k in FWD_PARAMS])


def _output_shape():
    out = _jax.eval_shape(lambda: _forward(_fwd_setup_inputs(0)))
    return out.shape, out.dtype

N_MICROBATCH = 1
ADAM_LR = 0.001
ADAM_B1 = 0.9
ADAM_B2 = 0.999
ADAM_EPS = 1e-08
ADAM_WD = 0.01
ADAM_STEP = 10
PER_EXAMPLE_BATCH_AXIS = {'x': 0, 'loss_target': 0}
SHARED_INPUTS = []
_WEIGHT_DTYPES = {'mix_norm': _jnp.float32, 'ffn_norm': _jnp.float32, 'final_norm': _jnp.float32, 'conv_w_in': _jnp.float32, 'conv_kernel': _jnp.float32, 'conv_w_out': _jnp.float32, 'attn_w_qkv': _jnp.float32, 'attn_w_out': _jnp.float32, 'rel_bias': _jnp.float32, 'ffn_w_gate': _jnp.float32, 'ffn_w_up': _jnp.float32, 'ffn_w_down': _jnp.float32}
MOMENT_SCALE = {'mix_norm': 1.399577e-01, 'ffn_norm': 8.003758e-02, 'final_norm': 1.601231e+01, 'conv_w_in': 1.119996e-01, 'conv_kernel': 1.143345e-01, 'conv_w_out': 1.119411e-01, 'attn_w_qkv': 1.989067e-02, 'attn_w_out': 2.304321e-02, 'rel_bias': 3.534704e-02, 'ffn_w_gate': 3.402887e-02, 'ffn_w_up': 3.295444e-02, 'ffn_w_down': 5.457494e-02}


def _to_microbatches(a, axis):
    t = _jnp.moveaxis(a, axis, 0)
    t = t.reshape((N_MICROBATCH, t.shape[0] // N_MICROBATCH) + t.shape[1:])
    return _jnp.moveaxis(t, 1, axis + 1)


def setup_inputs(seed: int = 0) -> dict:
    inp = _fwd_setup_inputs(seed)
    key = _jax.random.fold_in(_jax.random.key(seed), 7919)
    shape, _ = _output_shape()
    out = dict(inp)
    out["loss_target"] = _jax.random.normal(_jax.random.fold_in(key, 0), shape, _jnp.float32)
    for i, name in enumerate(TWIN_WEIGHTS):
        w = inp[name].astype(_jnp.float32)
        if MOMENT_SCALE is None:
            s = _jnp.sqrt(_jnp.mean(_jnp.square(w)) + 1e-30)
        else:
            s = MOMENT_SCALE[name]
        km, kv = _jax.random.split(_jax.random.fold_in(key, i + 1))
        out[name] = w
        out["m_" + name] = s * _jax.random.normal(km, w.shape, _jnp.float32)
        out["v_" + name] = (s * s) * _jax.random.uniform(kv, w.shape, _jnp.float32, 0.5, 1.5)
    if N_MICROBATCH > 1:
        for name, axis in PER_EXAMPLE_BATCH_AXIS.items():
            out[name] = _to_microbatches(out[name], axis)
    return {'x': out['x'], 'mix_norm': out['mix_norm'], 'ffn_norm': out['ffn_norm'], 'final_norm': out['final_norm'], 'conv_w_in': out['conv_w_in'], 'conv_kernel': out['conv_kernel'], 'conv_w_out': out['conv_w_out'], 'attn_w_qkv': out['attn_w_qkv'], 'attn_w_out': out['attn_w_out'], 'rel_bias': out['rel_bias'], 'ffn_w_gate': out['ffn_w_gate'], 'ffn_w_up': out['ffn_w_up'], 'ffn_w_down': out['ffn_w_down'], 'loss_target': out['loss_target'], 'm_mix_norm': out['m_mix_norm'], 'm_ffn_norm': out['m_ffn_norm'], 'm_final_norm': out['m_final_norm'], 'm_conv_w_in': out['m_conv_w_in'], 'm_conv_kernel': out['m_conv_kernel'], 'm_conv_w_out': out['m_conv_w_out'], 'm_attn_w_qkv': out['m_attn_w_qkv'], 'm_attn_w_out': out['m_attn_w_out'], 'm_rel_bias': out['m_rel_bias'], 'm_ffn_w_gate': out['m_ffn_w_gate'], 'm_ffn_w_up': out['m_ffn_w_up'], 'm_ffn_w_down': out['m_ffn_w_down'], 'v_mix_norm': out['v_mix_norm'], 'v_ffn_norm': out['v_ffn_norm'], 'v_final_norm': out['v_final_norm'], 'v_conv_w_in': out['v_conv_w_in'], 'v_conv_kernel': out['v_conv_kernel'], 'v_conv_w_out': out['v_conv_w_out'], 'v_attn_w_qkv': out['v_attn_w_qkv'], 'v_attn_w_out': out['v_attn_w_out'], 'v_rel_bias': out['v_rel_bias'], 'v_ffn_w_gate': out['v_ffn_w_gate'], 'v_ffn_w_up': out['v_ffn_w_up'], 'v_ffn_w_down': out['v_ffn_w_down']}


def _loss(weights, diff, rest, loss_target):
    with _jax.named_scope("forward"):
        args = {**rest, TWIN_DIFF_INPUT: diff, **{k: w.astype(_WEIGHT_DTYPES[k]) for k, w in weights.items()}}
        y = _forward(args)
    with _jax.named_scope("loss_head"):
        err = _jnp.square(y.astype(_jnp.float32) - loss_target)
        return 0.5 * _jnp.sum(_jnp.mean(err, axis=-1)) if err.ndim else 0.5 * err


def _adamw(w, g, m, v):
    m = ADAM_B1 * m + (1.0 - ADAM_B1) * g
    v = ADAM_B2 * v + (1.0 - ADAM_B2) * _jnp.square(g)
    m_hat = m / (1.0 - ADAM_B1 ** ADAM_STEP)
    v_hat = v / (1.0 - ADAM_B2 ** ADAM_STEP)
    delta = -ADAM_LR * (m_hat / (_jnp.sqrt(v_hat) + ADAM_EPS) + ADAM_WD * w)
    return delta, m, v


def reference(x, mix_norm, ffn_norm, final_norm, conv_w_in, conv_kernel, conv_w_out, attn_w_qkv, attn_w_out, rel_bias, ffn_w_gate, ffn_w_up, ffn_w_down, loss_target, m_mix_norm, m_ffn_norm, m_final_norm, m_conv_w_in, m_conv_kernel, m_conv_w_out, m_attn_w_qkv, m_attn_w_out, m_rel_bias, m_ffn_w_gate, m_ffn_w_up, m_ffn_w_down, v_mix_norm, v_ffn_norm, v_final_norm, v_conv_w_in, v_conv_kernel, v_conv_w_out, v_attn_w_qkv, v_attn_w_out, v_rel_bias, v_ffn_w_gate, v_ffn_w_up, v_ffn_w_down):
    given = dict(x=x, mix_norm=mix_norm, ffn_norm=ffn_norm, final_norm=final_norm, conv_w_in=conv_w_in, conv_kernel=conv_kernel, conv_w_out=conv_w_out, attn_w_qkv=attn_w_qkv, attn_w_out=attn_w_out, rel_bias=rel_bias, ffn_w_gate=ffn_w_gate, ffn_w_up=ffn_w_up, ffn_w_down=ffn_w_down, loss_target=loss_target, m_mix_norm=m_mix_norm, m_ffn_norm=m_ffn_norm, m_final_norm=m_final_norm, m_conv_w_in=m_conv_w_in, m_conv_kernel=m_conv_kernel, m_conv_w_out=m_conv_w_out, m_attn_w_qkv=m_attn_w_qkv, m_attn_w_out=m_attn_w_out, m_rel_bias=m_rel_bias, m_ffn_w_gate=m_ffn_w_gate, m_ffn_w_up=m_ffn_w_up, m_ffn_w_down=m_ffn_w_down, v_mix_norm=v_mix_norm, v_ffn_norm=v_ffn_norm, v_final_norm=v_final_norm, v_conv_w_in=v_conv_w_in, v_conv_kernel=v_conv_kernel, v_conv_w_out=v_conv_w_out, v_attn_w_qkv=v_attn_w_qkv, v_attn_w_out=v_attn_w_out, v_rel_bias=v_rel_bias, v_ffn_w_gate=v_ffn_w_gate, v_ffn_w_up=v_ffn_w_up, v_ffn_w_down=v_ffn_w_down)
    weights = {n: given[n] for n in TWIN_WEIGHTS}
    shared = {n: given[n] for n in SHARED_INPUTS}
    per_example = {n: given[n] for n in ['x']}
    grad_fn = _jax.value_and_grad(_loss, argnums=(0, 1))

    def one_microbatch(ex, loss_target):
        ex = dict(ex)
        diff = ex.pop(TWIN_DIFF_INPUT)
        return grad_fn(weights, diff, {**shared, **ex}, loss_target)

    if N_MICROBATCH == 1:
        loss, (grad_w, grad_x) = one_microbatch(per_example, given["loss_target"])
    else:
        def body(carry, xs):
            loss_sum, grad_sum = carry
            l_k, (gw_k, gx_k) = one_microbatch(xs[0], xs[1])
            with _jax.named_scope("update"):
                return (loss_sum + l_k, _jax.tree.map(_jnp.add, grad_sum, gw_k)), gx_k

        init = (_jnp.zeros((), _jnp.float32), _jax.tree.map(_jnp.zeros_like, weights))
        (loss, grad_w), grad_x = _jax.lax.scan(body, init, (per_example, given["loss_target"]))
    with _jax.named_scope("update"):
        delta_w, new_m, new_v = {}, {}, {}
        for n in TWIN_WEIGHTS:
            delta_w[n], new_m[n], new_v[n] = _adamw(weights[n], grad_w[n], given["m_" + n], given["v_" + n])
    return (loss, grad_x, *[grad_w[n] for n in TWIN_WEIGHTS], *[delta_w[n] for n in TWIN_WEIGHTS],
            *[new_m[n] for n in TWIN_WEIGHTS], *[new_v[n] for n in TWIN_WEIGHTS])
```

```python
import functools
import math

import jax
import jax.numpy as jnp
from jax import lax
from jax.experimental import pallas as pl
from jax.experimental.pallas import tpu as pltpu

S = 2048
D = 1024
H = 16
DH = 64
DFF = 2816
NDEV = 8
DEPTH = 4
FF_SHARD = DFF // NDEV
FF_SHARD_PAD = 384
DFF_PAD = FF_SHARD_PAD * NDEV
BLK = 128
BRANCH_DILATIONS = (1, 4, 16)
NUM_BUCKETS = 32
MAX_DISTANCE = 2048
EPS = 1e-6
NEG_INF = -1e30
SCALE = DH ** -0.5

ADAM_LR = 0.001
ADAM_B1 = 0.9
ADAM_B2 = 0.999
ADAM_EPS = 1e-08
ADAM_WD = 0.01
ADAM_STEP = 10

BF = jnp.bfloat16
F32 = jnp.float32
VMEM_LIMIT_BYTES = 56 * 1024 * 1024
MESH = pl.DeviceIdType.MESH
ANY = pl.BlockSpec(memory_space=pl.ANY)

_NT = (((1,), (1,)), ((), ()))
_TN = (((0,), (0,)), ((), ()))


def _cparams(sem=None):
    return pltpu.CompilerParams(dimension_semantics=sem, vmem_limit_bytes=VMEM_LIMIT_BYTES)


def _rms(x):
    return lax.rsqrt(jnp.mean(x * x, axis=-1, keepdims=True) + EPS)


def norm_matmul3(x, gain, w, name, tm=512, tn=512):
    per = D // tn

    def body(x_ref, g_ref, w_ref, z_ref, h_ref, hs_ref):
        @pl.when(pl.program_id(1) == 0)
        def _():
            xv = x_ref[...]
            hv = (xv * _rms(xv) * g_ref[...]).astype(BF)
            hs_ref[...] = hv
            h_ref[...] = hv
        z_ref[...] = jnp.dot(hs_ref[...], w_ref[...], preferred_element_type=F32).astype(BF)

    return pl.pallas_call(
        body, name=name,
        grid=(S // tm, 3 * D // tn),
        in_specs=[pl.BlockSpec((tm, D), lambda i, j: (i, 0)),
                  pl.BlockSpec((1, D), lambda i, j: (0, 0)),
                  pl.BlockSpec((D, tn), lambda i, j: (0, j))],
        out_specs=[pl.BlockSpec((None, tm, tn), lambda i, j: (j // per, i, j % per)),
                   pl.BlockSpec((tm, D), lambda i, j: (i, 0))],
        out_shape=[jax.ShapeDtypeStruct((3, S, D), BF), jax.ShapeDtypeStruct((S, D), BF)],
        scratch_shapes=[pltpu.VMEM((tm, D), BF)],
        compiler_params=_cparams(("parallel", "arbitrary")),
    )(x, gain, w)


def norm_swiglu_up(x, gain, wg, wu, name, tm=512, tn=512):
    def body(x_ref, g_ref, wg_ref, wu_ref, go_ref, uo_ref, ao_ref, h_ref, hs_ref):
        @pl.when(pl.program_id(1) == 0)
        def _():
            xv = x_ref[...]
            hv = (xv * _rms(xv) * g_ref[...]).astype(BF)
            hs_ref[...] = hv
            h_ref[...] = hv
        hv = hs_ref[...]
        g = jnp.dot(hv, wg_ref[...], preferred_element_type=F32)
        u = jnp.dot(hv, wu_ref[...], preferred_element_type=F32)
        go_ref[...] = g.astype(BF)
        uo_ref[...] = u.astype(BF)
        ao_ref[...] = (g * jax.nn.sigmoid(g) * u).astype(BF)

    act = jax.ShapeDtypeStruct((S, DFF_PAD), BF)
    blk = pl.BlockSpec((tm, tn), lambda i, j: (i, j))
    return pl.pallas_call(
        body, name=name,
        grid=(S // tm, DFF_PAD // tn),
        in_specs=[pl.BlockSpec((tm, D), lambda i, j: (i, 0)),
                  pl.BlockSpec((1, D), lambda i, j: (0, 0)),
                  pl.BlockSpec((D, tn), lambda i, j: (0, j)),
                  pl.BlockSpec((D, tn), lambda i, j: (0, j))],
        out_specs=[blk, blk, blk, pl.BlockSpec((tm, D), lambda i, j: (i, 0))],
        out_shape=[act, act, act, jax.ShapeDtypeStruct((S, D), BF)],
        scratch_shapes=[pltpu.VMEM((tm, D), BF)],
        compiler_params=_cparams(("parallel", "arbitrary")),
    )(x, gain, wg, wu)


def matmul_residual(a, w, x, name, tm=512, tn=512):
    K = a.shape[1]

    def body(a_ref, w_ref, x_ref, o_ref):
        o_ref[...] = x_ref[...] + jnp.dot(a_ref[...], w_ref[...], preferred_element_type=F32)

    return pl.pallas_call(
        body, name=name,
        grid=(S // tm, D // tn),
        in_specs=[pl.BlockSpec((tm, K), lambda i, j: (i, 0)),
                  pl.BlockSpec((K, tn), lambda i, j: (0, j)),
                  pl.BlockSpec((tm, tn), lambda i, j: (i, j))],
        out_specs=pl.BlockSpec((tm, tn), lambda i, j: (i, j)),
        out_shape=jax.ShapeDtypeStruct((S, D), F32),
        compiler_params=_cparams(("parallel", "parallel")),
    )(a, w, x)


def matmul_nt(a, w, name, out_dtype=BF, tm=512, tn=512):
    K = a.shape[1]
    N = w.shape[0]

    def body(a_ref, w_ref, o_ref):
        o_ref[...] = lax.dot_general(a_ref[...], w_ref[...], _NT,
                                     preferred_element_type=F32).astype(o_ref.dtype)

    return pl.pallas_call(
        body, name=name,
        grid=(S // tm, N // tn),
        in_specs=[pl.BlockSpec((tm, K), lambda i, j: (i, 0)),
                  pl.BlockSpec((tn, K), lambda i, j: (j, 0))],
        out_specs=pl.BlockSpec((tm, tn), lambda i, j: (i, j)),
        out_shape=jax.ShapeDtypeStruct((S, N), out_dtype),
        compiler_params=_cparams(("parallel", "parallel")),
    )(a, w)


def swiglu_bwd_da(dxb, wd, g, u, name, tm=512, tn=512):
    def body(dx_ref, w_ref, g_ref, u_ref, dg_ref, du_ref):
        da = lax.dot_general(dx_ref[...], w_ref[...], _NT, preferred_element_type=F32)
        gv = g_ref[...].astype(F32)
        uv = u_ref[...].astype(F32)
        sig = jax.nn.sigmoid(gv)
        dg_ref[...] = (da * uv * (sig * (1.0 + gv * (1.0 - sig)))).astype(BF)
        du_ref[...] = (da * (gv * sig)).astype(BF)

    act = jax.ShapeDtypeStruct((S, DFF_PAD), BF)
    blk = pl.BlockSpec((tm, tn), lambda i, j: (i, j))
    return pl.pallas_call(
        body, name=name,
        grid=(S // tm, DFF_PAD // tn),
        in_specs=[pl.BlockSpec((tm, D), lambda i, j: (i, 0)),
                  pl.BlockSpec((tn, D), lambda i, j: (j, 0)),
                  blk, blk],
        out_specs=[blk, blk],
        out_shape=[act, act],
        compiler_params=_cparams(("parallel", "parallel")),
    )(dxb, wd, g, u)


def matmul_tn(a, b, name, tm=512, tn=512):
    M = a.shape[1]
    if b.ndim == 3:
        per = D // tn
        N = 3 * D
        b_spec = pl.BlockSpec((None, S, tn), lambda i, j: (j // per, 0, j % per))
    else:
        N = b.shape[1]
        b_spec = pl.BlockSpec((S, tn), lambda i, j: (0, j))

    def body(a_ref, b_ref, o_ref):
        o_ref[...] = lax.dot_general(a_ref[...], b_ref[...], _TN,
                                     preferred_element_type=F32).astype(BF)

    return pl.pallas_call(
        body, name=name,
        grid=(M // tm, N // tn),
        in_specs=[pl.BlockSpec((S, tm), lambda i, j: (0, i)), b_spec],
        out_specs=pl.BlockSpec((tm, tn), lambda i, j: (i, j)),
        out_shape=jax.ShapeDtypeStruct((M, N), BF),
        compiler_params=_cparams(("parallel", "parallel")),
    )(a, b)


def matmul_nt_normbwd(terms, x_in, gain, dx, name, tk, tm=512):
    specs, operands, ranges = [], [], []
    start = 0
    for (a, w, stacked) in terms:
        K = w.shape[1]
        n = K // tk
        lo = start

        def rel(k, lo=lo, n=n):
            return jnp.clip(k - lo, 0, n - 1)

        if stacked:
            per = D // tk
            specs.append(pl.BlockSpec((None, tm, tk),
                                      lambda i, k, rel=rel, per=per: (rel(k) // per, i, rel(k) % per)))
        else:
            specs.append(pl.BlockSpec((tm, tk), lambda i, k, rel=rel: (i, rel(k))))
        specs.append(pl.BlockSpec((D, tk), lambda i, k, rel=rel: (0, rel(k))))
        operands += [a, w]
        ranges.append((lo, lo + n))
        start += n
    nk = start
    nt = len(terms)

    def body(*refs):
        aw = refs[:2 * nt]
        x_ref, g_ref, dx_ref, dxo_ref, dxb_ref, dg_ref, acc_ref = refs[2 * nt:]
        i = pl.program_id(0)
        k = pl.program_id(1)

        @pl.when(k == 0)
        def _():
            acc_ref[...] = jnp.zeros_like(acc_ref)

        @pl.when((i == 0) & (k == 0))
        def _():
            dg_ref[...] = jnp.zeros_like(dg_ref)

        for t in range(nt):
            lo, hi = ranges[t]

            @pl.when((k >= lo) & (k < hi))
            def _(t=t):
                acc_ref[...] += lax.dot_general(aw[2 * t][...], aw[2 * t + 1][...], _NT,
                                                preferred_element_type=F32)

        @pl.when(k == nk - 1)
        def _():
            xv = x_ref[...]
            r = _rms(xv)
            xhat = xv * r
            dh = acc_ref[...]
            dg_ref[0:1, :] += jnp.sum(dh * xhat, axis=0, keepdims=True)
            dxh = dh * g_ref[...]
            dxn = r * (dxh - xhat * jnp.mean(dxh * xhat, axis=-1, keepdims=True))
            out = dx_ref[...] + dxn
            dxo_ref[...] = out
            dxb_ref[...] = out.astype(BF)

    row = pl.BlockSpec((tm, D), lambda i, k: (i, 0))
    return pl.pallas_call(
        body, name=name,
        grid=(S // tm, nk),
        in_specs=specs + [row, pl.BlockSpec((1, D), lambda i, k: (0, 0)), row],
        out_specs=[row, row, pl.BlockSpec((8, D), lambda i, k: (0, 0))],
        out_shape=[jax.ShapeDtypeStruct((S, D), F32), jax.ShapeDtypeStruct((S, D), BF),
                   jax.ShapeDtypeStruct((8, D), F32)],
        scratch_shapes=[pltpu.VMEM((tm, D), F32)],
        compiler_params=_cparams(("arbitrary", "arbitrary")),
    )(*operands, x_in, gain, dx)


def loss_head(x, gain, target, name, tm=512):
    def body(x_ref, g_ref, t_ref, dxo_ref, dxb_ref, dg_ref, sq_ref):
        @pl.when(pl.program_id(0) == 0)
        def _():
            dg_ref[...] = jnp.zeros_like(dg_ref)
            sq_ref[...] = jnp.zeros_like(sq_ref)
        xv = x_ref[...]
        r = _rms(xv)
        xhat = xv * r
        err = xhat * g_ref[...] - t_ref[...]
        sq_ref[0:1, :] += jnp.sum(err * err, axis=0, keepdims=True)
        dy = err * (1.0 / D)
        dg_ref[0:1, :] += jnp.sum(dy * xhat, axis=0, keepdims=True)
        dxh = dy * g_ref[...]
        out = r * (dxh - xhat * jnp.mean(dxh * xhat, axis=-1, keepdims=True))
        dxo_ref[...] = out
        dxb_ref[...] = out.astype(BF)

    row = pl.BlockSpec((tm, D), lambda i: (i, 0))
    acc = pl.BlockSpec((8, D), lambda i: (0, 0))
    return pl.pallas_call(
        body, name=name,
        grid=(S // tm,),
        in_specs=[row, pl.BlockSpec((1, D), lambda i: (0, 0)), row],
        out_specs=[row, row, acc, acc],
        out_shape=[jax.ShapeDtypeStruct((S, D), F32), jax.ShapeDtypeStruct((S, D), BF),
                   jax.ShapeDtypeStruct((8, D), F32), jax.ShapeDtypeStruct((8, D), F32)],
        compiler_params=_cparams(("arbitrary",)),
    )(x, gain, target)


def _shift_down(p, n, row):
    return jnp.where(row >= n, pltpu.roll(p, n, axis=0), 0.0)


def _shift_up(p, n, row):
    return jnp.where(row < S - n, pltpu.roll(p, S - n, axis=0), 0.0)


def conv_fwd(z3, taps, name, tn=128):
    def body(z_ref, k_ref, m_ref):
        b = z_ref[0].astype(F32)
        p = z_ref[1].astype(F32) * z_ref[2].astype(F32)
        row = lax.broadcasted_iota(jnp.int32, p.shape, 0)
        y = (k_ref[2:3, :] * p + k_ref[1:2, :] * _shift_down(p, 1, row)
             + k_ref[0:1, :] * _shift_down(p, 2, row))
        m_ref[...] = (b * y).astype(BF)

    return pl.pallas_call(
        body, name=name,
        grid=(D // tn,),
        in_specs=[pl.BlockSpec((3, S, tn), lambda j: (0, 0, j)),
                  pl.BlockSpec((8, tn), lambda j: (0, j))],
        out_specs=pl.BlockSpec((S, tn), lambda j: (0, j)),
        out_shape=jax.ShapeDtypeStruct((S, D), BF),
        compiler_params=_cparams(("parallel",)),
    )(z3, taps)


def conv_bwd(dm, z3, taps, name, tn=128):
    def body(dm_ref, z_ref, k_ref, dz_ref, dk_ref):
        dmv = dm_ref[...]
        b = z_ref[0].astype(F32)
        c = z_ref[1].astype(F32)
        u = z_ref[2].astype(F32)
        p = c * u
        row = lax.broadcasted_iota(jnp.int32, p.shape, 0)
        p1 = _shift_down(p, 1, row)
        p2 = _shift_down(p, 2, row)
        y = k_ref[2:3, :] * p + k_ref[1:2, :] * p1 + k_ref[0:1, :] * p2
        dy = dmv * b
        dz_ref[0] = (dmv * y).astype(BF)
        dp = (k_ref[2:3, :] * dy + k_ref[1:2, :] * _shift_up(dy, 1, row)
              + k_ref[0:1, :] * _shift_up(dy, 2, row))
        dz_ref[1] = (dp * u).astype(BF)
        dz_ref[2] = (dp * c).astype(BF)
        dk_ref[...] = jnp.zeros_like(dk_ref)
        dk_ref[0:1, :] = jnp.sum(dy * p2, axis=0, keepdims=True)
        dk_ref[1:2, :] = jnp.sum(dy * p1, axis=0, keepdims=True)
        dk_ref[2:3, :] = jnp.sum(dy * p, axis=0, keepdims=True)

    return pl.pallas_call(
        body, name=name,
        grid=(D // tn,),
        in_specs=[pl.BlockSpec((S, tn), lambda j: (0, j)),
                  pl.BlockSpec((3, S, tn), lambda j: (0, 0, j)),
                  pl.BlockSpec((8, tn), lambda j: (0, j))],
        out_specs=[pl.BlockSpec((3, S, tn), lambda j: (0, 0, j)),
                   pl.BlockSpec((8, tn), lambda j: (0, j))],
        out_shape=[jax.ShapeDtypeStruct((3, S, D), BF), jax.ShapeDtypeStruct((8, D), F32)],
        compiler_params=_cparams(("parallel",)),
    )(dm, z3, taps)


def _t5_bucket(dist):
    exact = NUM_BUCKETS // 2
    df = jnp.maximum(dist, 1).astype(jnp.float32)
    large = exact + (jnp.log(df / exact) / math.log(MAX_DISTANCE / exact)
                     * (NUM_BUCKETS - exact)).astype(jnp.int32)
    large = jnp.minimum(large, NUM_BUCKETS - 1)
    return jnp.where(dist < exact, dist, large)


def _bucket_onehot_t():
    qi = jnp.arange(BLK)[:, None]
    ki = jnp.arange(2 * BLK)[None, :]
    rel = qi + BLK - ki
    band = ((rel >= 0) & (rel <= BLK)).reshape(1, -1).astype(F32)
    hots = []
    for d in BRANCH_DILATIONS:
        bucket = _t5_bucket(jnp.clip(rel, 0) * d).reshape(1, -1)
        hots.append((jnp.arange(NUM_BUCKETS)[:, None] == bucket).astype(F32))
    return jnp.stack(hots), band


def bias_tables(rel_bias_t, onehot_t, band, name):
    def body(rb_ref, oh_ref, band_ref, o_ref):
        b = jnp.dot(rb_ref[...], oh_ref[...], preferred_element_type=F32,
                    precision=lax.Precision.HIGHEST)
        o_ref[...] = jnp.where(band_ref[...] > 0.5, b, NEG_INF)

    n = BLK * 2 * BLK
    return pl.pallas_call(
        body, name=name,
        grid=(3,),
        in_specs=[pl.BlockSpec((H, NUM_BUCKETS), lambda g: (0, 0)),
                  pl.BlockSpec((None, NUM_BUCKETS, n), lambda g: (g, 0, 0)),
                  pl.BlockSpec((1, n), lambda g: (0, 0))],
        out_specs=pl.BlockSpec((None, H, n), lambda g: (g, 0, 0)),
        out_shape=jax.ShapeDtypeStruct((3, H, n), F32),
        compiler_params=_cparams(("parallel",)),
    )(rel_bias_t, onehot_t, band)


def bias_grad(dbias, onehot_t, name):
    def body(db_ref, oh_ref, o_ref):
        @pl.when(pl.program_id(0) == 0)
        def _():
            o_ref[...] = jnp.zeros_like(o_ref)
        o_ref[...] += lax.dot_general(db_ref[...], oh_ref[...], _NT, preferred_element_type=F32,
                                      precision=lax.Precision.HIGHEST)

    n = BLK * 2 * BLK
    return pl.pallas_call(
        body, name=name,
        grid=(dbias.shape[0],),
        in_specs=[pl.BlockSpec((None, H, n), lambda g: (g, 0, 0)),
                  pl.BlockSpec((None, NUM_BUCKETS, n), lambda g: (g % 3, 0, 0))],
        out_specs=pl.BlockSpec((H, NUM_BUCKETS), lambda g: (0, 0)),
        out_shape=jax.ShapeDtypeStruct((H, NUM_BUCKETS), F32),
        compiler_params=_cparams(("arbitrary",)),
    )(dbias, onehot_t)


def _head_masks():
    lane = lax.broadcasted_iota(jnp.int32, (1, 2 * DH), 1)
    return (lane < DH, lane >= DH)


def attn_branch_fwd(z3d, bias, d, name):
    L = S // d
    nb = L // BLK

    def body(q_ref, k_ref, v_ref, b_ref, o_ref, lse_ref):
        masks = _head_masks()
        lane2 = lax.broadcasted_iota(jnp.int32, (BLK, 2), 1)

        def block(n, first):
            if first:
                q0 = 0
                kk = k_ref[0:BLK, :]
                vv = v_ref[0:BLK, :]
            else:
                q0 = pl.multiple_of(n * BLK, BLK)
                k0 = pl.multiple_of((n - 1) * BLK, BLK)
                kk = k_ref[pl.ds(k0, 2 * BLK), :]
                vv = v_ref[pl.ds(k0, 2 * BLK), :]
            q = q_ref[pl.ds(q0, BLK), :]
            outs, lses = [], []
            for hh in range(2):
                qh = jnp.where(masks[hh], q, jnp.zeros_like(q))
                bias_h = b_ref[hh][:, BLK:] if first else b_ref[hh]
                s = lax.dot_general(qh, kk, _NT, preferred_element_type=F32) * SCALE + bias_h
                mx = jnp.max(s, axis=1, keepdims=True)
                p = jnp.exp(s - mx)
                l = jnp.sum(p, axis=1, keepdims=True)
                outs.append(jnp.dot(p.astype(BF), vv, preferred_element_type=F32) / l)
                lses.append(mx + jnp.log(l))
            o_ref[pl.ds(q0, BLK), :] = jnp.where(masks[0], outs[0], outs[1])
            lse_ref[pl.ds(q0, BLK), :] = jnp.where(lane2 == 0, lses[0], lses[1])

        block(0, True)
        if nb > 1:
            def loop(n, carry):
                block(n, False)
                return carry
            lax.fori_loop(1, nb, loop, 0)

    col = lambda hp, r: (0, r * 8 + hp)
    return pl.pallas_call(
        body, name=name,
        grid=(8, d),
        in_specs=[pl.BlockSpec((None, L, 2 * DH), lambda hp, r: (0, 0, r * 8 + hp)),
                  pl.BlockSpec((None, L, 2 * DH), lambda hp, r: (1, 0, r * 8 + hp)),
                  pl.BlockSpec((None, L, 2 * DH), lambda hp, r: (2, 0, r * 8 + hp)),
                  pl.BlockSpec((2, BLK, 2 * BLK), lambda hp, r: (hp, 0, 0))],
        out_specs=[pl.BlockSpec((L, 2 * DH), col),
                   pl.BlockSpec((None, L, 2), lambda hp, r: (r * 8 + hp, 0, 0))],
        out_shape=[jax.ShapeDtypeStruct((L, d * D), F32),
                   jax.ShapeDtypeStruct((d * 8, L, 2), F32)],
        compiler_params=_cparams(("parallel", "parallel")),
    )(z3d, z3d, z3d, bias)


def attn_branch_bwd(z3d, dod, od, lse, bias, d, name):
    L = S // d
    nb = L // BLK

    def body(q_ref, k_ref, v_ref, do_ref, o_ref, lse_ref, b_ref, dz_ref, db_ref):
        masks = _head_masks()

        @pl.when(pl.program_id(1) == 0)
        def _():
            db_ref[...] = jnp.zeros_like(db_ref)

        dz_ref[1] = jnp.zeros((L, 2 * DH), F32)
        dz_ref[2] = jnp.zeros((L, 2 * DH), F32)

        def block(n, first):
            if first:
                q0 = 0
                k0 = 0
                nk = BLK
            else:
                q0 = pl.multiple_of(n * BLK, BLK)
                k0 = pl.multiple_of((n - 1) * BLK, BLK)
                nk = 2 * BLK
            kk = k_ref[pl.ds(k0, nk), :]
            vv = v_ref[pl.ds(k0, nk), :]
            q = q_ref[pl.ds(q0, BLK), :]
            do = do_ref[pl.ds(q0, BLK), :]
            of = o_ref[pl.ds(q0, BLK), :].astype(F32)
            lse_blk = lse_ref[pl.ds(q0, BLK), :]
            dqs = []
            dk = jnp.zeros((nk, 2 * DH), F32)
            dv = jnp.zeros((nk, 2 * DH), F32)
            for hh in range(2):
                qh = jnp.where(masks[hh], q, jnp.zeros_like(q))
                doh = jnp.where(masks[hh], do, jnp.zeros_like(do))
                bias_h = b_ref[hh][:, BLK:] if first else b_ref[hh]
                s = lax.dot_general(qh, kk, _NT, preferred_element_type=F32) * SCALE + bias_h
                p = jnp.exp(s - lse_blk[:, hh:hh + 1])
                dp = lax.dot_general(doh, vv, _NT, preferred_element_type=F32)
                delta = jnp.sum(doh.astype(F32) * of, axis=1, keepdims=True)
                ds = p * (dp - delta)
                if first:
                    db_ref[hh, :, BLK:] += ds
                else:
                    db_ref[hh] += ds
                dsb = ds.astype(BF)
                dqs.append(jnp.dot(dsb, kk, preferred_element_type=F32) * SCALE)
                dk += lax.dot_general(dsb, qh, _TN, preferred_element_type=F32) * SCALE
                dv += lax.dot_general(p.astype(BF), doh, _TN, preferred_element_type=F32)
            dz_ref[0, pl.ds(q0, BLK), :] = jnp.where(masks[0], dqs[0], dqs[1])
            dz_ref[1, pl.ds(k0, nk), :] += dk
            dz_ref[2, pl.ds(k0, nk), :] += dv

        block(0, True)
        if nb > 1:
            def loop(n, carry):
                block(n, False)
                return carry
            lax.fori_loop(1, nb, loop, 0)

    col = lambda hp, r: (0, r * 8 + hp)
    return pl.pallas_call(
        body, name=name,
        grid=(8, d),
        in_specs=[pl.BlockSpec((None, L, 2 * DH), lambda hp, r: (0, 0, r * 8 + hp)),
                  pl.BlockSpec((None, L, 2 * DH), lambda hp, r: (1, 0, r * 8 + hp)),
                  pl.BlockSpec((None, L, 2 * DH), lambda hp, r: (2, 0, r * 8 + hp)),
                  pl.BlockSpec((L, 2 * DH), col),
                  pl.BlockSpec((L, 2 * DH), col),
                  pl.BlockSpec((None, L, 2), lambda hp, r: (r * 8 + hp, 0, 0)),
                  pl.BlockSpec((2, BLK, 2 * BLK), lambda hp, r: (hp, 0, 0))],
        out_specs=[pl.BlockSpec((3, L, 2 * DH), lambda hp, r: (0, 0, r * 8 + hp)),
                   pl.BlockSpec((2, BLK, 2 * BLK), lambda hp, r: (hp, 0, 0))],
        out_shape=[jax.ShapeDtypeStruct((3, L, d * D), F32),
                   jax.ShapeDtypeStruct((H, BLK, 2 * BLK), F32)],
        compiler_params=_cparams(("parallel", "arbitrary")),
    )(z3d, z3d, z3d, dod, od, lse, bias)


def attn_combine(o_parts, lse_parts, name, tm=512):
    def body(o1, o2, o3, l1, l2, l3, o_ref, lse_ref):
        a, b, c = l1[...], l2[...], l3[...]
        mx = jnp.maximum(jnp.maximum(a, b), c)
        ea, eb, ec = jnp.exp(a - mx), jnp.exp(b - mx), jnp.exp(c - mx)
        tot = ea + eb + ec
        o_ref[...] = ((ea * o1[...] + eb * o2[...] + ec * o3[...]) / tot).astype(BF)
        lse_ref[...] = mx + jnp.log(tot)

    row = pl.BlockSpec((tm, D), lambda i: (i, 0))
    return pl.pallas_call(
        body, name=name,
        grid=(S // tm,),
        in_specs=[row] * 6,
        out_specs=[row, row],
        out_shape=[jax.ShapeDtypeStruct((S, D), BF), jax.ShapeDtypeStruct((S, D), F32)],
        compiler_params=_cparams(("parallel",)),
    )(*o_parts, *lse_parts)


def sum3_cast(a, b, c, name, tm=512):
    def body(a_ref, b_ref, c_ref, o_ref):
        o_ref[...] = (a_ref[...] + b_ref[...] + c_ref[...]).astype(BF)

    blk = pl.BlockSpec((None, tm, D), lambda k, i: (k, i, 0))
    return pl.pallas_call(
        body, name=name,
        grid=(3, S // tm),
        in_specs=[blk] * 3,
        out_specs=blk,
        out_shape=jax.ShapeDtypeStruct((3, S, D), BF),
        compiler_params=_cparams(("parallel", "parallel")),
    )(a, b, c)


def _lse_compact(lse_b, d):
    L = S // d
    v = lse_b[:, ::DH].reshape(L, d, 8, 2)
    return v.transpose(1, 2, 0, 3).reshape(d * 8, L, 2)


def _lse_broadcast(lse_c, d):
    L = S // d
    v = lse_c.reshape(d, 8, L, 2).transpose(2, 0, 1, 3).reshape(S, H)
    return jnp.repeat(v, DH, axis=1)


def attention_fwd(z3, bias3):
    o_parts, lse_parts = [], []
    for g, d in enumerate(BRANCH_DILATIONS):
        L = S // d
        o_d, lse_c = attn_branch_fwd(z3.reshape(3, L, d * D), bias3[g], d, f"attn_fwd_d{d}")
        o_parts.append(o_d.reshape(S, D))
        lse_parts.append(_lse_broadcast(lse_c, d))
    return attn_combine(o_parts, lse_parts, "attn_combine")


def attention_bwd(z3, dob, ob, lse_b, bias3):
    dzs, dbs = [], []
    for g, d in enumerate(BRANCH_DILATIONS):
        L = S // d
        dz_d, db = attn_branch_bwd(z3.reshape(3, L, d * D), dob.reshape(L, d * D),
                                   ob.reshape(L, d * D), _lse_compact(lse_b, d), bias3[g], d,
                                   f"attn_bwd_d{d}")
        dzs.append(dz_d.reshape(3, S, D))
        dbs.append(db.reshape(H, BLK * 2 * BLK))
    return sum3_cast(*dzs, "attn_dz_sum"), jnp.stack(dbs)


def _me():
    return lax.axis_index("x"), lax.axis_index("y"), lax.axis_index("c")


def _other_chips(x, y):
    return [(1 - x, y), (x, 1 - y), (1 - x, 1 - y)]


def _shard_window(ref, axis, t, shape):
    R, C = shape
    if axis == 0:
        return ref.at[pl.ds(pl.multiple_of(t * R, 128), R), :]
    return ref.at[:, pl.ds(pl.multiple_of(t * C, 128), C)]


def all_gather_weights(shards, axes, name):
    n = len(shards)
    shapes = [s.shape for s in shards]
    outs_shape = [jax.ShapeDtypeStruct((8 * R, C) if ax == 0 else (R, 8 * C), BF)
                  for (R, C), ax in zip(shapes, axes)]

    def body(*refs):
        ins, outs = refs[:n], refs[n:2 * n]
        send_sems, recv_sems, local_sems = refs[2 * n:]
        x, y, c = _me()
        sibling = (x, y, 1 - c)
        chips = _other_chips(x, y)

        def win(i, px, py, pc):
            return _shard_window(outs[i], axes[i], 4 * px + 2 * py + pc, shapes[i])

        def copy(i, k, block, to, src=None):
            return pltpu.make_async_remote_copy(
                src_ref=win(i, *block) if src is None else src, dst_ref=win(i, *block),
                send_sem=send_sems.at[i * 7 + k], recv_sem=recv_sems.at[i * 7 + k],
                device_id=to, device_id_type=MESH)

        mine = [pltpu.make_async_copy(ins[i], win(i, x, y, c), local_sems.at[i]) for i in range(n)]
        for cp in mine:
            cp.start()
        first = []
        for i in range(n):
            first.append(copy(i, 0, (x, y, c), sibling, src=ins[i]))
            for j, chip in enumerate(chips):
                first.append(copy(i, 1 + j, (x, y, c), (*chip, c), src=ins[i]))
        for cp in first:
            cp.start()
        passed = []
        for j, chip in enumerate(chips):
            for i in range(n):
                copy(i, 1 + j, (*chip, c), (x, y, c)).wait_recv()
                cp = copy(i, 4 + j, (*chip, c), sibling)
                cp.start()
                passed.append(cp)
        for i in range(n):
            copy(i, 0, sibling, (x, y, c)).wait_recv()
        for j, chip in enumerate(chips):
            for i in range(n):
                copy(i, 4 + j, (*chip, 1 - c), (x, y, c)).wait_recv()
        for cp in first + passed:
            cp.wait_send()
        for cp in mine:
            cp.wait()

    return pl.pallas_call(
        body, name=name,
        in_specs=[ANY] * n,
        out_specs=[ANY] * n,
        out_shape=outs_shape,
        scratch_shapes=[pltpu.SemaphoreType.DMA((7 * n,)), pltpu.SemaphoreType.DMA((7 * n,)),
                        pltpu.SemaphoreType.DMA((n,))],
        compiler_params=pltpu.CompilerParams(has_side_effects=True),
    )(*shards)


def pair_exchange_grads(grads, axes, shapes, name):
    n = len(grads)

    def body(*refs):
        ins, outs = refs[:n], refs[n:2 * n]
        send_sems, recv_sems = refs[2 * n:]
        x, y, c = _me()
        sibling = (x, y, 1 - c)
        copies = []
        for i in range(n):
            for q in range(4):
                t = 2 * q + (1 - c)
                copies.append(pltpu.make_async_remote_copy(
                    src_ref=_shard_window(ins[i], axes[i], t, shapes[i]), dst_ref=outs[i].at[q],
                    send_sem=send_sems.at[i * 4 + q], recv_sem=recv_sems.at[i * 4 + q],
                    device_id=sibling, device_id_type=MESH))
        for cp in copies:
            cp.start()
        for cp in copies:
            cp.wait_recv()
        for cp in copies:
            cp.wait_send()

    return pl.pallas_call(
        body, name=name,
        in_specs=[ANY] * n,
        out_specs=[ANY] * n,
        out_shape=[jax.ShapeDtypeStruct((4,) + tuple(sh), BF) for sh in shapes],
        scratch_shapes=[pltpu.SemaphoreType.DMA((4 * n,)), pltpu.SemaphoreType.DMA((4 * n,))],
        compiler_params=pltpu.CompilerParams(has_side_effects=True),
    )(*grads)


def pair_add(grad, landed, axis, shape, c_idx, name):
    R, C = shape

    def body(c_ref, g_ref, l_ref, o_ref):
        o_ref[...] = (g_ref[...].astype(F32) + l_ref[...].astype(F32)).astype(BF)

    if axis == 0:
        g_spec = pl.BlockSpec((R, C), lambda q, c_ref: (2 * q + c_ref[0], 0))
    else:
        g_spec = pl.BlockSpec((R, C), lambda q, c_ref: (0, 2 * q + c_ref[0]))
    blk = pl.BlockSpec((None, R, C), lambda q, c_ref: (q, 0, 0))
    return pl.pallas_call(
        body, name=name,
        grid_spec=pltpu.PrefetchScalarGridSpec(
            num_scalar_prefetch=1, grid=(4,), in_specs=[g_spec, blk], out_specs=blk),
        out_shape=jax.ShapeDtypeStruct((4, R, C), BF),
        compiler_params=_cparams(("parallel",)),
    )(c_idx, grad, landed)


def chip_exchange_grads(parts, name):
    n = len(parts)

    def body(*refs):
        ins, outs = refs[:n], refs[n:2 * n]
        send_sems, recv_sems = refs[2 * n:]
        x, y, c = _me()
        copies = []
        for i in range(n):
            for k, (px, py) in enumerate(_other_chips(x, y)):
                copies.append(pltpu.make_async_remote_copy(
                    src_ref=ins[i].at[2 * px + py], dst_ref=outs[i].at[k],
                    send_sem=send_sems.at[i * 3 + k], recv_sem=recv_sems.at[i * 3 + k],
                    device_id=(px, py, c), device_id_type=MESH))
        for cp in copies:
            cp.start()
        for cp in copies:
            cp.wait_recv()
        for cp in copies:
            cp.wait_send()

    return pl.pallas_call(
        body, name=name,
        in_specs=[ANY] * n,
        out_specs=[ANY] * n,
        out_shape=[jax.ShapeDtypeStruct((3,) + tuple(p.shape[1:]), BF) for p in parts],
        scratch_shapes=[pltpu.SemaphoreType.DMA((3 * n,)), pltpu.SemaphoreType.DMA((3 * n,))],
        compiler_params=pltpu.CompilerParams(has_side_effects=True),
    )(*parts)


def all_gather_small(v, name):
    R, C = v.shape

    def body(v_ref, out_ref, send_sems, recv_sems, local_sem):
        x, y, c = _me()
        me, sibling = (x, y, c), (x, y, 1 - c)
        chips = _other_chips(x, y)

        def slot(px, py, pc):
            return out_ref.at[4 * px + 2 * py + pc]

        def copy(k, block, to, src=None):
            return pltpu.make_async_remote_copy(
                src_ref=slot(*block) if src is None else src, dst_ref=slot(*block),
                send_sem=send_sems.at[k], recv_sem=recv_sems.at[k],
                device_id=to, device_id_type=MESH)

        mine = pltpu.make_async_copy(v_ref, slot(*me), local_sem)
        mine.start()
        first = [copy(0, me, sibling, src=v_ref)]
        first += [copy(1 + j, me, (*chip, c), src=v_ref) for j, chip in enumerate(chips)]
        for cp in first:
            cp.start()
        passed = [copy(4 + j, (*chip, c), sibling) for j, chip in enumerate(chips)]
        for j, chip in enumerate(chips):
            copy(1 + j, (*chip, c), me).wait_recv()
            passed[j].start()
        copy(0, sibling, me).wait_recv()
        for j, chip in enumerate(chips):
            copy(4 + j, (*chip, 1 - c), me).wait_recv()
        for cp in first + passed:
            cp.wait_send()
        mine.wait()

    return pl.pallas_call(
        body, name=name,
        in_specs=[pl.BlockSpec(memory_space=pltpu.VMEM)],
        out_specs=pl.BlockSpec(memory_space=pltpu.VMEM),
        out_shape=jax.ShapeDtypeStruct((NDEV, R, C), F32),
        scratch_shapes=[pltpu.SemaphoreType.DMA((7,)), pltpu.SemaphoreType.DMA((7,)),
                        pltpu.SemaphoreType.DMA],
    )(v)


def _adamw(w, g, m, v):
    m = ADAM_B1 * m + (1.0 - ADAM_B1) * g
    v = ADAM_B2 * v + (1.0 - ADAM_B2) * (g * g)
    m_hat = m / (1.0 - ADAM_B1 ** ADAM_STEP)
    v_hat = v / (1.0 - ADAM_B2 ** ADAM_STEP)
    delta = -ADAM_LR * (m_hat / (jnp.sqrt(v_hat) + ADAM_EPS) + ADAM_WD * w)
    return delta, m, v


def reduce_adamw(part, landed, w, m, v, layer, q_idx, name):
    R, C = part.shape[1:]
    r, c = w.shape[1:]
    tr = r // 2 if r % 16 == 0 and r >= 256 else r
    tR = tr if tr != r else R

    def body(q_ref, p_ref, l_ref, w_ref, m_ref, v_ref, g_out, d_out, m_out, v_out):
        g = p_ref[...].astype(F32)
        for k in range(3):
            g = g + l_ref[k].astype(F32)
        g = g[:tr, :c]
        d, mm, vv = _adamw(w_ref[...], g, m_ref[...], v_ref[...])
        g_out[...] = g
        d_out[...] = d
        m_out[...] = mm
        v_out[...] = vv

    wspec = pl.BlockSpec((None, tr, c), lambda i, q_ref: (layer, i, 0))
    ospec = pl.BlockSpec((tr, c), lambda i, q_ref: (i, 0))
    out = jax.ShapeDtypeStruct((r, c), F32)
    return pl.pallas_call(
        body, name=name,
        grid_spec=pltpu.PrefetchScalarGridSpec(
            num_scalar_prefetch=1, grid=(r // tr,),
            in_specs=[pl.BlockSpec((None, tR, C), lambda i, q_ref: (q_ref[0], i, 0)),
                      pl.BlockSpec((3, tR, C), lambda i, q_ref: (0, i, 0)),
                      wspec, wspec, wspec],
            out_specs=[ospec] * 4),
        out_shape=[out] * 4,
        compiler_params=_cparams(("parallel",)),
    )(q_idx, part, landed, w, m, v)


def small_reduce_adamw(gathered, w, m, v, name):
    R, C = w.shape

    def body(a_ref, w_ref, m_ref, v_ref, g_out, d_out, m_out, v_out):
        g = a_ref[0]
        for k in range(1, NDEV):
            g = g + a_ref[k]
        d, mm, vv = _adamw(w_ref[...], g, m_ref[...], v_ref[...])
        g_out[...] = g
        d_out[...] = d
        m_out[...] = mm
        v_out[...] = vv

    out = jax.ShapeDtypeStruct((R, C), F32)
    return pl.pallas_call(body, name=name, out_shape=[out] * 4,
                          compiler_params=_cparams())(gathered, w, m, v)


def _pad_cols(a, n):
    return jnp.pad(a, ((0, 0), (0, n - a.shape[1])))


def _pad_rows(a, n):
    return jnp.pad(a, ((0, n - a.shape[0]), (0, 0)))


SMALL_ROWS = 16


def _pack_small(mix, ffn, fin, taps_full, relb):
    return jnp.concatenate([
        mix, ffn, fin.reshape(1, D), taps_full.reshape(6, D),
        jnp.pad(relb.reshape(1, NUM_BUCKETS * H), ((0, 0), (0, D - NUM_BUCKETS * H)))], axis=0)


def kernel(x, mix_norm, ffn_norm, final_norm, conv_w_in, conv_kernel, conv_w_out, attn_w_qkv, attn_w_out, rel_bias, ffn_w_gate, ffn_w_up, ffn_w_down, loss_target, m_mix_norm, m_ffn_norm, m_final_norm, m_conv_w_in, m_conv_kernel, m_conv_w_out, m_attn_w_qkv, m_attn_w_out, m_rel_bias, m_ffn_w_gate, m_ffn_w_up, m_ffn_w_down, v_mix_norm, v_ffn_norm, v_final_norm, v_conv_w_in, v_conv_kernel, v_conv_w_out, v_attn_w_qkv, v_attn_w_out, v_rel_bias, v_ffn_w_gate, v_ffn_w_up, v_ffn_w_down):
    xi, yi, ci = _me()
    me = 4 * xi + 2 * yi + ci
    c_idx = jnp.reshape(ci, (1,)).astype(jnp.int32)
    q_idx = jnp.reshape(2 * xi + yi, (1,)).astype(jnp.int32)
    col0 = me * (D // NDEV)

    taps_local = jnp.zeros((2, 3, D), F32)
    taps_local = lax.dynamic_update_slice(taps_local, conv_kernel, (0, 0, col0))
    taps_pack = jnp.pad(taps_local.reshape(6, D), ((0, 2), (0, 0)))
    taps_all = all_gather_small(taps_pack, "ag_taps")
    taps_sum = jnp.sum(taps_all, axis=0)
    taps = [jnp.pad(taps_sum[3 * j:3 * j + 3], ((0, 5), (0, 0))) for j in range(2)]

    mixer_in = (conv_w_in, attn_w_qkv)
    mixer_out = (conv_w_out, attn_w_out)
    wts = []
    for i in range(DEPTH):
        j = i // 2
        shards = [mixer_in[i % 2][j].astype(BF), mixer_out[i % 2][j].astype(BF),
                  _pad_cols(ffn_w_gate[i].astype(BF), FF_SHARD_PAD),
                  _pad_cols(ffn_w_up[i].astype(BF), FF_SHARD_PAD),
                  _pad_rows(ffn_w_down[i].astype(BF), FF_SHARD_PAD)]
        wts.append(all_gather_weights(shards, (1, 0, 1, 1, 0), f"ag_weights_l{i}"))

    onehot_t, band = _bucket_onehot_t()
    bias3 = bias_tables(rel_bias.T, onehot_t, band, "bias_tables").reshape(3, H, BLK, 2 * BLK)

    saved = []
    xc = x[0]
    for i in range(DEPTH):
        w_in, w_out, w_g, w_u, w_d = wts[i]
        j = i // 2
        x_mix = xc
        z3, h_mix = norm_matmul3(xc, mix_norm[i:i + 1], w_in, f"mix_in_l{i}")
        if i % 2 == 0:
            act = conv_fwd(z3, taps[j], f"conv_fwd_l{i}")
            lse_b = None
        else:
            act, lse_b = attention_fwd(z3, bias3)
        xc = matmul_residual(act, w_out, xc, f"mix_out_l{i}")
        x_ffn = xc
        g, u, a, h_ffn = norm_swiglu_up(xc, ffn_norm[i:i + 1], w_g, w_u, f"ffn_up_l{i}")
        xc = matmul_residual(a, w_d, xc, f"ffn_down_l{i}")
        saved.append((x_mix, h_mix, z3, act, lse_b, x_ffn, h_ffn, g, u, a))

    dx, dxb, dg_final, sq = loss_head(xc, final_norm.reshape(1, D), loss_target[0], "loss_head")
    loss = lax.psum(0.5 * jnp.sum(sq[0]) / D, ("x", "y", "c"))

    w_params = ((conv_w_in, m_conv_w_in, v_conv_w_in), (attn_w_qkv, m_attn_w_qkv, v_attn_w_qkv))
    o_params = ((conv_w_out, m_conv_w_out, v_conv_w_out), (attn_w_out, m_attn_w_out, v_attn_w_out))
    results = {}
    dg_mix = [None] * DEPTH
    dg_ffn = [None] * DEPTH
    dtaps = [None, None]
    dbias_all = []
    axes = (1, 0, 1, 1, 0)
    shapes = ((D, 3 * D // NDEV), (D // NDEV, D), (D, FF_SHARD_PAD), (D, FF_SHARD_PAD), (FF_SHARD_PAD, D))
    for i in reversed(range(DEPTH)):
        w_in, w_out, w_g, w_u, w_d = wts[i]
        j = i // 2
        x_mix, h_mix, z3, act, lse_b, x_ffn, h_ffn, g, u, a = saved[i]
        dgate, dup = swiglu_bwd_da(dxb, w_d, g, u, f"ffn_da_l{i}")
        gw_d = matmul_tn(a, dxb, f"ffn_dwd_l{i}")
        gw_g = matmul_tn(h_ffn, dgate, f"ffn_dwg_l{i}")
        gw_u = matmul_tn(h_ffn, dup, f"ffn_dwu_l{i}")
        dx, dxb, dg_ffn[i] = matmul_nt_normbwd(
            [(dgate, w_g, False), (dup, w_u, False)], x_ffn, ffn_norm[i:i + 1], dx, f"ffn_dh_l{i}", tk=768)
        gw_out = matmul_tn(act, dxb, f"mix_dwout_l{i}")
        if i % 2 == 0:
            dm = matmul_nt(dxb, w_out, f"mix_dact_l{i}", out_dtype=F32)
            dz3, dtaps[j] = conv_bwd(dm, z3, taps[j], f"conv_bwd_l{i}")
        else:
            dob = matmul_nt(dxb, w_out, f"mix_dact_l{i}", out_dtype=BF)
            dz3, dbias3 = attention_bwd(z3, dob, act, lse_b, bias3)
            dbias_all.append(dbias3)
        gw_in = matmul_tn(h_mix, dz3, f"mix_dwin_l{i}")
        dx, dxb, dg_mix[i] = matmul_nt_normbwd(
            [(dz3, w_in, True)], x_mix, mix_norm[i:i + 1], dx, f"mix_dh_l{i}", tk=512)
        grads = [gw_in, gw_out, gw_g, gw_u, gw_d]
        landed1 = pair_exchange_grads(grads, axes, shapes, f"rs_pair_l{i}")
        parts = [pair_add(grads[t], landed1[t], axes[t], shapes[t], c_idx, f"rs_add_l{i}_{t}")
                 for t in range(5)]
        landed2 = chip_exchange_grads(parts, f"rs_chip_l{i}")
        plist = [w_params[i % 2] + (j,), o_params[i % 2] + (j,),
                 (ffn_w_gate, m_ffn_w_gate, v_ffn_w_gate, i), (ffn_w_up, m_ffn_w_up, v_ffn_w_up, i),
                 (ffn_w_down, m_ffn_w_down, v_ffn_w_down, i)]
        for t in range(5):
            w_, m_, v_, layer = plist[t]
            results[(i, t)] = reduce_adamw(parts[t], landed2[t], w_, m_, v_, layer, q_idx,
                                           f"adamw_l{i}_{t}")

    grad_relb_t = bias_grad(jnp.concatenate(dbias_all), onehot_t, "bias_grad")
    dtaps_full = jnp.stack([dtaps[0][:3], dtaps[1][:3]])
    g_small = _pack_small(jnp.concatenate([d[0:1] for d in dg_mix], axis=0),
                          jnp.concatenate([d[0:1] for d in dg_ffn], axis=0),
                          dg_final[0], dtaps_full, grad_relb_t.T)
    gathered = all_gather_small(g_small, "ag_small_grads")

    def taps_at_cols(k):
        return lax.dynamic_update_slice(jnp.zeros((2, 3, D), F32), k, (0, 0, col0))

    w_small = _pack_small(mix_norm, ffn_norm, final_norm, taps_at_cols(conv_kernel), rel_bias)
    m_small = _pack_small(m_mix_norm, m_ffn_norm, m_final_norm, taps_at_cols(m_conv_kernel), m_rel_bias)
    v_small = _pack_small(v_mix_norm, v_ffn_norm, v_final_norm, taps_at_cols(v_conv_kernel), v_rel_bias)
    small = small_reduce_adamw(gathered, w_small, m_small, v_small, "adamw_small")

    def unpack_small(p):
        taps_p = lax.dynamic_slice(p[9:15].reshape(2, 3, D), (0, 0, col0), (2, 3, D // NDEV))
        return {"mix_norm": p[0:4], "ffn_norm": p[4:8], "final_norm": p[8],
                "conv_kernel": taps_p, "rel_bias": p[15, :NUM_BUCKETS * H].reshape(NUM_BUCKETS, H)}

    small_out = [unpack_small(p) for p in small]

    def big(kind, which):
        if which in (0, 1):
            layers = [i for i in range(DEPTH) if (i % 2 == 0) == (kind == "conv")]
        else:
            layers = list(range(DEPTH))
        return [jnp.stack([results[(i, which)][o] for i in layers]) for o in range(4)]

    big_out = {"conv_w_in": big("conv", 0), "conv_w_out": big("conv", 1),
               "attn_w_qkv": big("attn", 0), "attn_w_out": big("attn", 1),
               "ffn_w_gate": big("ffn", 2), "ffn_w_up": big("ffn", 3), "ffn_w_down": big("ffn", 4)}

    names = ["mix_norm", "ffn_norm", "final_norm", "conv_w_in", "conv_kernel", "conv_w_out",
             "attn_w_qkv", "attn_w_out", "rel_bias", "ffn_w_gate", "ffn_w_up", "ffn_w_down"]
    outs = [loss, dx.reshape(1, S, D)]
    for o in range(4):
        for nme in names:
            outs.append(big_out[nme][o] if nme in big_out else small_out[o][nme])
    return tuple(outs)
```

```python
import functools
import math

import jax
import jax.numpy as jnp
from jax import lax
from jax.experimental import pallas as pl
from jax.experimental.pallas import tpu as pltpu
from jax.experimental.pallas import tpu_sc as plsc

S = 2048
D = 1024
H = 16
DH = 64
DFF = 2816
NDEV = 8
DEPTH = 4
FF_SHARD = DFF // NDEV
FF_SHARD_PAD = 384
DFF_PAD = FF_SHARD_PAD * NDEV
BLK = 128
BRANCH_DILATIONS = (1, 4, 16)
NUM_BUCKETS = 32
MAX_DISTANCE = 2048
EPS = 1e-6
NEG_INF = -1e30
SCALE = DH ** -0.5

ADAM_LR = 0.001
ADAM_B1 = 0.9
ADAM_B2 = 0.999
ADAM_EPS = 1e-08
ADAM_WD = 0.01
ADAM_STEP = 10

BF = jnp.bfloat16
F32 = jnp.float32
VMEM_LIMIT_BYTES = 56 * 1024 * 1024
MESH = pl.DeviceIdType.MESH
ANY = pl.BlockSpec(memory_space=pl.ANY)

_NT = (((1,), (1,)), ((), ()))
_TN = (((0,), (0,)), ((), ()))


def _cparams(sem=None):
    return pltpu.CompilerParams(dimension_semantics=sem, vmem_limit_bytes=VMEM_LIMIT_BYTES)


def _rms(x):
    return lax.rsqrt(jnp.mean(x * x, axis=-1, keepdims=True) + EPS)


def norm_matmul3(x, gain, w, name, tm=512, tn=512):
    per = D // tn

    def body(x_ref, g_ref, w_ref, z_ref, h_ref, hs_ref):
        @pl.when(pl.program_id(1) == 0)
        def _():
            xv = x_ref[...]
            hv = (xv * _rms(xv) * g_ref[...]).astype(BF)
            hs_ref[...] = hv
            h_ref[...] = hv
        z_ref[...] = jnp.dot(hs_ref[...], w_ref[...], preferred_element_type=F32).astype(BF)

    return pl.pallas_call(
        body, name=name,
        grid=(S // tm, 3 * D // tn),
        in_specs=[pl.BlockSpec((tm, D), lambda i, j: (i, 0)),
                  pl.BlockSpec((1, D), lambda i, j: (0, 0)),
                  pl.BlockSpec((D, tn), lambda i, j: (0, j))],
        out_specs=[pl.BlockSpec((None, tm, tn), lambda i, j: (j // per, i, j % per)),
                   pl.BlockSpec((tm, D), lambda i, j: (i, 0))],
        out_shape=[jax.ShapeDtypeStruct((3, S, D), BF), jax.ShapeDtypeStruct((S, D), BF)],
        scratch_shapes=[pltpu.VMEM((tm, D), BF)],
        compiler_params=_cparams(("parallel", "arbitrary")),
    )(x, gain, w)


def norm_swiglu_up(x, gain, wg, wu, name, tm=512, tn=512):
    def body(x_ref, g_ref, wg_ref, wu_ref, go_ref, uo_ref, ao_ref, h_ref, hs_ref):
        @pl.when(pl.program_id(1) == 0)
        def _():
            xv = x_ref[...]
            hv = (xv * _rms(xv) * g_ref[...]).astype(BF)
            hs_ref[...] = hv
            h_ref[...] = hv
        hv = hs_ref[...]
        g = jnp.dot(hv, wg_ref[...], preferred_element_type=F32)
        u = jnp.dot(hv, wu_ref[...], preferred_element_type=F32)
        go_ref[...] = g.astype(BF)
        uo_ref[...] = u.astype(BF)
        ao_ref[...] = (g * jax.nn.sigmoid(g) * u).astype(BF)

    act = jax.ShapeDtypeStruct((S, DFF_PAD), BF)
    blk = pl.BlockSpec((tm, tn), lambda i, j: (i, j))
    return pl.pallas_call(
        body, name=name,
        grid=(S // tm, DFF_PAD // tn),
        in_specs=[pl.BlockSpec((tm, D), lambda i, j: (i, 0)),
                  pl.BlockSpec((1, D), lambda i, j: (0, 0)),
                  pl.BlockSpec((D, tn), lambda i, j: (0, j)),
                  pl.BlockSpec((D, tn), lambda i, j: (0, j))],
        out_specs=[blk, blk, blk, pl.BlockSpec((tm, D), lambda i, j: (i, 0))],
        out_shape=[act, act, act, jax.ShapeDtypeStruct((S, D), BF)],
        scratch_shapes=[pltpu.VMEM((tm, D), BF)],
        compiler_params=_cparams(("parallel", "arbitrary")),
    )(x, gain, wg, wu)


def matmul_residual(a, w, x, name, tm=512, tn=512):
    K = a.shape[1]

    def body(a_ref, w_ref, x_ref, o_ref):
        o_ref[...] = x_ref[...] + jnp.dot(a_ref[...], w_ref[...], preferred_element_type=F32)

    return pl.pallas_call(
        body, name=name,
        grid=(S // tm, D // tn),
        in_specs=[pl.BlockSpec((tm, K), lambda i, j: (i, 0)),
                  pl.BlockSpec((K, tn), lambda i, j: (0, j)),
                  pl.BlockSpec((tm, tn), lambda i, j: (i, j))],
        out_specs=pl.BlockSpec((tm, tn), lambda i, j: (i, j)),
        out_shape=jax.ShapeDtypeStruct((S, D), F32),
        compiler_params=_cparams(("parallel", "parallel")),
    )(a, w, x)


def matmul_nt(a, w, name, out_dtype=BF, tm=512, tn=512):
    K = a.shape[1]
    N = w.shape[0]

    def body(a_ref, w_ref, o_ref):
        o_ref[...] = lax.dot_general(a_ref[...], w_ref[...], _NT,
                                     preferred_element_type=F32).astype(o_ref.dtype)

    return pl.pallas_call(
        body, name=name,
        grid=(S // tm, N // tn),
        in_specs=[pl.BlockSpec((tm, K), lambda i, j: (i, 0)),
                  pl.BlockSpec((tn, K), lambda i, j: (j, 0))],
        out_specs=pl.BlockSpec((tm, tn), lambda i, j: (i, j)),
        out_shape=jax.ShapeDtypeStruct((S, N), out_dtype),
        compiler_params=_cparams(("parallel", "parallel")),
    )(a, w)


def swiglu_bwd_da(dxb, wd, g, u, name, tm=512, tn=512):
    def body(dx_ref, w_ref, g_ref, u_ref, dg_ref, du_ref):
        da = lax.dot_general(dx_ref[...], w_ref[...], _NT, preferred_element_type=F32)
        gv = g_ref[...].astype(F32)
        uv = u_ref[...].astype(F32)
        sig = jax.nn.sigmoid(gv)
        dg_ref[...] = (da * uv * (sig * (1.0 + gv * (1.0 - sig)))).astype(BF)
        du_ref[...] = (da * (gv * sig)).astype(BF)

    act = jax.ShapeDtypeStruct((S, DFF_PAD), BF)
    blk = pl.BlockSpec((tm, tn), lambda i, j: (i, j))
    return pl.pallas_call(
        body, name=name,
        grid=(S // tm, DFF_PAD // tn),
        in_specs=[pl.BlockSpec((tm, D), lambda i, j: (i, 0)),
                  pl.BlockSpec((tn, D), lambda i, j: (j, 0)),
                  blk, blk],
        out_specs=[blk, blk],
        out_shape=[act, act],
        compiler_params=_cparams(("parallel", "parallel")),
    )(dxb, wd, g, u)


def matmul_tn(a, b, name, tm=512, tn=512):
    M = a.shape[1]
    if b.ndim == 3:
        per = D // tn
        N = 3 * D
        b_spec = pl.BlockSpec((None, S, tn), lambda i, j: (j // per, 0, j % per))
    else:
        N = b.shape[1]
        b_spec = pl.BlockSpec((S, tn), lambda i, j: (0, j))

    def body(a_ref, b_ref, o_ref):
        o_ref[...] = lax.dot_general(a_ref[...], b_ref[...], _TN,
                                     preferred_element_type=F32).astype(BF)

    return pl.pallas_call(
        body, name=name,
        grid=(M // tm, N // tn),
        in_specs=[pl.BlockSpec((S, tm), lambda i, j: (0, i)), b_spec],
        out_specs=pl.BlockSpec((tm, tn), lambda i, j: (i, j)),
        out_shape=jax.ShapeDtypeStruct((M, N), BF),
        compiler_params=_cparams(("parallel", "parallel")),
    )(a, b)


def matmul_nt_normbwd(terms, x_in, gain, dx, name, tk, tm=512):
    specs, operands, ranges = [], [], []
    start = 0
    for (a, w, stacked) in terms:
        K = w.shape[1]
        n = K // tk
        lo = start

        def rel(k, lo=lo, n=n):
            return jnp.clip(k - lo, 0, n - 1)

        if stacked:
            per = D // tk
            specs.append(pl.BlockSpec((None, tm, tk),
                                      lambda i, k, rel=rel, per=per: (rel(k) // per, i, rel(k) % per)))
        else:
            specs.append(pl.BlockSpec((tm, tk), lambda i, k, rel=rel: (i, rel(k))))
        specs.append(pl.BlockSpec((D, tk), lambda i, k, rel=rel: (0, rel(k))))
        operands += [a, w]
        ranges.append((lo, lo + n))
        start += n
    nk = start
    nt = len(terms)

    def body(*refs):
        aw = refs[:2 * nt]
        x_ref, g_ref, dx_ref, dxo_ref, dxb_ref, dg_ref, acc_ref = refs[2 * nt:]
        i = pl.program_id(0)
        k = pl.program_id(1)

        @pl.when(k == 0)
        def _():
            acc_ref[...] = jnp.zeros_like(acc_ref)

        @pl.when((i == 0) & (k == 0))
        def _():
            dg_ref[...] = jnp.zeros_like(dg_ref)

        for t in range(nt):
            lo, hi = ranges[t]

            @pl.when((k >= lo) & (k < hi))
            def _(t=t):
                acc_ref[...] += lax.dot_general(aw[2 * t][...], aw[2 * t + 1][...], _NT,
                                                preferred_element_type=F32)

        @pl.when(k == nk - 1)
        def _():
            xv = x_ref[...]
            r = _rms(xv)
            xhat = xv * r
            dh = acc_ref[...]
            dg_ref[0:1, :] += jnp.sum(dh * xhat, axis=0, keepdims=True)
            dxh = dh * g_ref[...]
            dxn = r * (dxh - xhat * jnp.mean(dxh * xhat, axis=-1, keepdims=True))
            out = dx_ref[...] + dxn
            dxo_ref[...] = out
            dxb_ref[...] = out.astype(BF)

    row = pl.BlockSpec((tm, D), lambda i, k: (i, 0))
    return pl.pallas_call(
        body, name=name,
        grid=(S // tm, nk),
        in_specs=specs + [row, pl.BlockSpec((1, D), lambda i, k: (0, 0)), row],
        out_specs=[row, row, pl.BlockSpec((8, D), lambda i, k: (0, 0))],
        out_shape=[jax.ShapeDtypeStruct((S, D), F32), jax.ShapeDtypeStruct((S, D), BF),
                   jax.ShapeDtypeStruct((8, D), F32)],
        scratch_shapes=[pltpu.VMEM((tm, D), F32)],
        compiler_params=_cparams(("arbitrary", "arbitrary")),
    )(*operands, x_in, gain, dx)


def loss_head(x, gain, target, name, tm=512):
    def body(x_ref, g_ref, t_ref, dxo_ref, dxb_ref, dg_ref, sq_ref):
        @pl.when(pl.program_id(0) == 0)
        def _():
            dg_ref[...] = jnp.zeros_like(dg_ref)
            sq_ref[...] = jnp.zeros_like(sq_ref)
        xv = x_ref[...]
        r = _rms(xv)
        xhat = xv * r
        err = xhat * g_ref[...] - t_ref[...]
        sq_ref[0:1, :] += jnp.sum(err * err, axis=0, keepdims=True)
        dy = err * (1.0 / D)
        dg_ref[0:1, :] += jnp.sum(dy * xhat, axis=0, keepdims=True)
        dxh = dy * g_ref[...]
        out = r * (dxh - xhat * jnp.mean(dxh * xhat, axis=-1, keepdims=True))
        dxo_ref[...] = out
        dxb_ref[...] = out.astype(BF)

    row = pl.BlockSpec((tm, D), lambda i: (i, 0))
    acc = pl.BlockSpec((8, D), lambda i: (0, 0))
    return pl.pallas_call(
        body, name=name,
        grid=(S // tm,),
        in_specs=[row, pl.BlockSpec((1, D), lambda i: (0, 0)), row],
        out_specs=[row, row, acc, acc],
        out_shape=[jax.ShapeDtypeStruct((S, D), F32), jax.ShapeDtypeStruct((S, D), BF),
                   jax.ShapeDtypeStruct((8, D), F32), jax.ShapeDtypeStruct((8, D), F32)],
        compiler_params=_cparams(("arbitrary",)),
    )(x, gain, target)


def _shift_down(p, n, row):
    return jnp.where(row >= n, pltpu.roll(p, n, axis=0), 0.0)


def _shift_up(p, n, row):
    return jnp.where(row < S - n, pltpu.roll(p, S - n, axis=0), 0.0)


def conv_fwd(z3, taps, name, tn=128):
    def body(z_ref, k_ref, m_ref):
        b = z_ref[0].astype(F32)
        p = z_ref[1].astype(F32) * z_ref[2].astype(F32)
        row = lax.broadcasted_iota(jnp.int32, p.shape, 0)
        y = (k_ref[2:3, :] * p + k_ref[1:2, :] * _shift_down(p, 1, row)
             + k_ref[0:1, :] * _shift_down(p, 2, row))
        m_ref[...] = (b * y).astype(BF)

    return pl.pallas_call(
        body, name=name,
        grid=(D // tn,),
        in_specs=[pl.BlockSpec((3, S, tn), lambda j: (0, 0, j)),
                  pl.BlockSpec((8, tn), lambda j: (0, j))],
        out_specs=pl.BlockSpec((S, tn), lambda j: (0, j)),
        out_shape=jax.ShapeDtypeStruct((S, D), BF),
        compiler_params=_cparams(("parallel",)),
    )(z3, taps)


def conv_bwd(dm, z3, taps, name, tn=128):
    def body(dm_ref, z_ref, k_ref, dz_ref, dk_ref):
        dmv = dm_ref[...]
        b = z_ref[0].astype(F32)
        c = z_ref[1].astype(F32)
        u = z_ref[2].astype(F32)
        p = c * u
        row = lax.broadcasted_iota(jnp.int32, p.shape, 0)
        p1 = _shift_down(p, 1, row)
        p2 = _shift_down(p, 2, row)
        y = k_ref[2:3, :] * p + k_ref[1:2, :] * p1 + k_ref[0:1, :] * p2
        dy = dmv * b
        dz_ref[0] = (dmv * y).astype(BF)
        dp = (k_ref[2:3, :] * dy + k_ref[1:2, :] * _shift_up(dy, 1, row)
              + k_ref[0:1, :] * _shift_up(dy, 2, row))
        dz_ref[1] = (dp * u).astype(BF)
        dz_ref[2] = (dp * c).astype(BF)
        dk_ref[...] = jnp.zeros_like(dk_ref)
        dk_ref[0:1, :] = jnp.sum(dy * p2, axis=0, keepdims=True)
        dk_ref[1:2, :] = jnp.sum(dy * p1, axis=0, keepdims=True)
        dk_ref[2:3, :] = jnp.sum(dy * p, axis=0, keepdims=True)

    return pl.pallas_call(
        body, name=name,
        grid=(D // tn,),
        in_specs=[pl.BlockSpec((S, tn), lambda j: (0, j)),
                  pl.BlockSpec((3, S, tn), lambda j: (0, 0, j)),
                  pl.BlockSpec((8, tn), lambda j: (0, j))],
        out_specs=[pl.BlockSpec((3, S, tn), lambda j: (0, 0, j)),
                   pl.BlockSpec((8, tn), lambda j: (0, j))],
        out_shape=[jax.ShapeDtypeStruct((3, S, D), BF), jax.ShapeDtypeStruct((8, D), F32)],
        compiler_params=_cparams(("parallel",)),
    )(dm, z3, taps)


def _t5_bucket(dist):
    exact = NUM_BUCKETS // 2
    df = jnp.maximum(dist, 1).astype(jnp.float32)
    large = exact + (jnp.log(df / exact) / math.log(MAX_DISTANCE / exact)
                     * (NUM_BUCKETS - exact)).astype(jnp.int32)
    large = jnp.minimum(large, NUM_BUCKETS - 1)
    return jnp.where(dist < exact, dist, large)


def _bucket_onehot_t():
    qi = jnp.arange(BLK)[:, None]
    ki = jnp.arange(2 * BLK)[None, :]
    rel = qi + BLK - ki
    band = ((rel >= 0) & (rel <= BLK)).reshape(1, -1).astype(F32)
    hots = []
    for d in BRANCH_DILATIONS:
        bucket = _t5_bucket(jnp.clip(rel, 0) * d).reshape(1, -1)
        hots.append((jnp.arange(NUM_BUCKETS)[:, None] == bucket).astype(F32))
    return jnp.stack(hots), band


def bias_tables(rel_bias_t, onehot_t, band, name):
    def body(rb_ref, oh_ref, band_ref, o_ref):
        b = jnp.dot(rb_ref[...], oh_ref[...], preferred_element_type=F32,
                    precision=lax.Precision.HIGHEST)
        o_ref[...] = jnp.where(band_ref[...] > 0.5, b, NEG_INF)

    n = BLK * 2 * BLK
    return pl.pallas_call(
        body, name=name,
        grid=(3,),
        in_specs=[pl.BlockSpec((H, NUM_BUCKETS), lambda g: (0, 0)),
                  pl.BlockSpec((None, NUM_BUCKETS, n), lambda g: (g, 0, 0)),
                  pl.BlockSpec((1, n), lambda g: (0, 0))],
        out_specs=pl.BlockSpec((None, H, n), lambda g: (g, 0, 0)),
        out_shape=jax.ShapeDtypeStruct((3, H, n), F32),
        compiler_params=_cparams(("parallel",)),
    )(rel_bias_t, onehot_t, band)


def bias_grad(dbias, onehot_t, name):
    def body(db_ref, oh_ref, o_ref):
        @pl.when(pl.program_id(0) == 0)
        def _():
            o_ref[...] = jnp.zeros_like(o_ref)
        o_ref[...] += lax.dot_general(db_ref[...], oh_ref[...], _NT, preferred_element_type=F32,
                                      precision=lax.Precision.HIGHEST)

    n = BLK * 2 * BLK
    return pl.pallas_call(
        body, name=name,
        grid=(dbias.shape[0],),
        in_specs=[pl.BlockSpec((None, H, n), lambda g: (g, 0, 0)),
                  pl.BlockSpec((None, NUM_BUCKETS, n), lambda g: (g % 3, 0, 0))],
        out_specs=pl.BlockSpec((H, NUM_BUCKETS), lambda g: (0, 0)),
        out_shape=jax.ShapeDtypeStruct((H, NUM_BUCKETS), F32),
        compiler_params=_cparams(("arbitrary",)),
    )(dbias, onehot_t)


def _head_masks():
    lane = lax.broadcasted_iota(jnp.int32, (1, 2 * DH), 1)
    return (lane < DH, lane >= DH)


def attn_branch_fwd(z3d, bias, d, name):
    L = S // d
    nb = L // BLK

    def body(q_ref, k_ref, v_ref, b_ref, o_ref, lse_ref):
        masks = _head_masks()
        lane2 = lax.broadcasted_iota(jnp.int32, (BLK, 2), 1)

        def block(n, first):
            if first:
                q0 = 0
                kk = k_ref[0:BLK, :]
                vv = v_ref[0:BLK, :]
            else:
                q0 = pl.multiple_of(n * BLK, BLK)
                k0 = pl.multiple_of((n - 1) * BLK, BLK)
                kk = k_ref[pl.ds(k0, 2 * BLK), :]
                vv = v_ref[pl.ds(k0, 2 * BLK), :]
            q = q_ref[pl.ds(q0, BLK), :]
            outs, lses = [], []
            for hh in range(2):
                qh = jnp.where(masks[hh], q, jnp.zeros_like(q))
                bias_h = b_ref[hh][:, BLK:] if first else b_ref[hh]
                s = lax.dot_general(qh, kk, _NT, preferred_element_type=F32) * SCALE + bias_h
                mx = jnp.max(s, axis=1, keepdims=True)
                p = jnp.exp(s - mx)
                l = jnp.sum(p, axis=1, keepdims=True)
                outs.append(jnp.dot(p.astype(BF), vv, preferred_element_type=F32) / l)
                lses.append(mx + jnp.log(l))
            o_ref[pl.ds(q0, BLK), :] = jnp.where(masks[0], outs[0], outs[1])
            lse_ref[pl.ds(q0, BLK), :] = jnp.where(lane2 == 0, lses[0], lses[1])

        block(0, True)
        if nb > 1:
            def loop(n, carry):
                block(n, False)
                return carry
            lax.fori_loop(1, nb, loop, 0)

    col = lambda hp, r: (0, r * 8 + hp)
    return pl.pallas_call(
        body, name=name,
        grid=(8, d),
        in_specs=[pl.BlockSpec((None, L, 2 * DH), lambda hp, r: (0, 0, r * 8 + hp)),
                  pl.BlockSpec((None, L, 2 * DH), lambda hp, r: (1, 0, r * 8 + hp)),
                  pl.BlockSpec((None, L, 2 * DH), lambda hp, r: (2, 0, r * 8 + hp)),
                  pl.BlockSpec((2, BLK, 2 * BLK), lambda hp, r: (hp, 0, 0))],
        out_specs=[pl.BlockSpec((L, 2 * DH), col),
                   pl.BlockSpec((None, L, 2), lambda hp, r: (r * 8 + hp, 0, 0))],
        out_shape=[jax.ShapeDtypeStruct((L, d * D), F32),
                   jax.ShapeDtypeStruct((d * 8, L, 2), F32)],
        compiler_params=_cparams(("parallel", "parallel")),
    )(z3d, z3d, z3d, bias)


def attn_branch_bwd(z3d, dod, od, lse, bias, d, name):
    L = S // d
    nb = L // BLK

    def body(q_ref, k_ref, v_ref, do_ref, o_ref, lse_ref, b_ref, dz_ref, db_ref):
        masks = _head_masks()

        @pl.when(pl.program_id(1) == 0)
        def _():
            db_ref[...] = jnp.zeros_like(db_ref)

        dz_ref[1] = jnp.zeros((L, 2 * DH), F32)
        dz_ref[2] = jnp.zeros((L, 2 * DH), F32)

        def block(n, first):
            if first:
                q0 = 0
                k0 = 0
                nk = BLK
            else:
                q0 = pl.multiple_of(n * BLK, BLK)
                k0 = pl.multiple_of((n - 1) * BLK, BLK)
                nk = 2 * BLK
            kk = k_ref[pl.ds(k0, nk), :]
            vv = v_ref[pl.ds(k0, nk), :]
            q = q_ref[pl.ds(q0, BLK), :]
            do = do_ref[pl.ds(q0, BLK), :]
            of = o_ref[pl.ds(q0, BLK), :].astype(F32)
            lse_blk = lse_ref[pl.ds(q0, BLK), :]
            dqs = []
            dk = jnp.zeros((nk, 2 * DH), F32)
            dv = jnp.zeros((nk, 2 * DH), F32)
            for hh in range(2):
                qh = jnp.where(masks[hh], q, jnp.zeros_like(q))
                doh = jnp.where(masks[hh], do, jnp.zeros_like(do))
                bias_h = b_ref[hh][:, BLK:] if first else b_ref[hh]
                s = lax.dot_general(qh, kk, _NT, preferred_element_type=F32) * SCALE + bias_h
                p = jnp.exp(s - lse_blk[:, hh:hh + 1])
                dp = lax.dot_general(doh, vv, _NT, preferred_element_type=F32)
                delta = jnp.sum(doh.astype(F32) * of, axis=1, keepdims=True)
                ds = p * (dp - delta)
                if first:
                    db_ref[hh, :, BLK:] += ds
                else:
                    db_ref[hh] += ds
                dsb = ds.astype(BF)
                dqs.append(jnp.dot(dsb, kk, preferred_element_type=F32) * SCALE)
                dk += lax.dot_general(dsb, qh, _TN, preferred_element_type=F32) * SCALE
                dv += lax.dot_general(p.astype(BF), doh, _TN, preferred_element_type=F32)
            dz_ref[0, pl.ds(q0, BLK), :] = jnp.where(masks[0], dqs[0], dqs[1])
            dz_ref[1, pl.ds(k0, nk), :] += dk
            dz_ref[2, pl.ds(k0, nk), :] += dv

        block(0, True)
        if nb > 1:
            def loop(n, carry):
                block(n, False)
                return carry
            lax.fori_loop(1, nb, loop, 0)

    col = lambda hp, r: (0, r * 8 + hp)
    return pl.pallas_call(
        body, name=name,
        grid=(8, d),
        in_specs=[pl.BlockSpec((None, L, 2 * DH), lambda hp, r: (0, 0, r * 8 + hp)),
                  pl.BlockSpec((None, L, 2 * DH), lambda hp, r: (1, 0, r * 8 + hp)),
                  pl.BlockSpec((None, L, 2 * DH), lambda hp, r: (2, 0, r * 8 + hp)),
                  pl.BlockSpec((L, 2 * DH), col),
                  pl.BlockSpec((L, 2 * DH), col),
                  pl.BlockSpec((None, L, 2), lambda hp, r: (r * 8 + hp, 0, 0)),
                  pl.BlockSpec((2, BLK, 2 * BLK), lambda hp, r: (hp, 0, 0))],
        out_specs=[pl.BlockSpec((3, L, 2 * DH), lambda hp, r: (0, 0, r * 8 + hp)),
                   pl.BlockSpec((2, BLK, 2 * BLK), lambda hp, r: (hp, 0, 0))],
        out_shape=[jax.ShapeDtypeStruct((3, L, d * D), F32),
                   jax.ShapeDtypeStruct((H, BLK, 2 * BLK), F32)],
        compiler_params=_cparams(("parallel", "arbitrary")),
    )(z3d, z3d, z3d, dod, od, lse, bias)


def attn_combine(o_parts, lse_parts, name, tm=512):
    def body(o1, o2, o3, l1, l2, l3, o_ref, lse_ref):
        a, b, c = l1[...], l2[...], l3[...]
        mx = jnp.maximum(jnp.maximum(a, b), c)
        ea, eb, ec = jnp.exp(a - mx), jnp.exp(b - mx), jnp.exp(c - mx)
        tot = ea + eb + ec
        o_ref[...] = ((ea * o1[...] + eb * o2[...] + ec * o3[...]) / tot).astype(BF)
        lse_ref[...] = mx + jnp.log(tot)

    row = pl.BlockSpec((tm, D), lambda i: (i, 0))
    return pl.pallas_call(
        body, name=name,
        grid=(S // tm,),
        in_specs=[row] * 6,
        out_specs=[row, row],
        out_shape=[jax.ShapeDtypeStruct((S, D), BF), jax.ShapeDtypeStruct((S, D), F32)],
        compiler_params=_cparams(("parallel",)),
    )(*o_parts, *lse_parts)


def sum3_cast(a, b, c, name, tm=512):
    def body(a_ref, b_ref, c_ref, o_ref):
        o_ref[...] = (a_ref[...] + b_ref[...] + c_ref[...]).astype(BF)

    blk = pl.BlockSpec((None, tm, D), lambda k, i: (k, i, 0))
    return pl.pallas_call(
        body, name=name,
        grid=(3, S // tm),
        in_specs=[blk] * 3,
        out_specs=blk,
        out_shape=jax.ShapeDtypeStruct((3, S, D), BF),
        compiler_params=_cparams(("parallel", "parallel")),
    )(a, b, c)


def _lse_compact(lse_b, d):
    L = S // d
    v = lse_b[:, ::DH].reshape(L, d, 8, 2)
    return v.transpose(1, 2, 0, 3).reshape(d * 8, L, 2)


def _lse_broadcast(lse_c, d):
    L = S // d
    v = lse_c.reshape(d, 8, L, 2).transpose(2, 0, 1, 3).reshape(S, H)
    return jnp.repeat(v, DH, axis=1)


def attention_fwd(z3, bias3):
    o_parts, lse_parts = [], []
    for g, d in enumerate(BRANCH_DILATIONS):
        L = S // d
        o_d, lse_c = attn_branch_fwd(z3.reshape(3, L, d * D), bias3[g], d, f"attn_fwd_d{d}")
        o_parts.append(o_d.reshape(S, D))
        lse_parts.append(_lse_broadcast(lse_c, d))
    return attn_combine(o_parts, lse_parts, "attn_combine")


def attention_bwd(z3, dob, ob, lse_b, bias3):
    dzs, dbs = [], []
    for g, d in enumerate(BRANCH_DILATIONS):
        L = S // d
        dz_d, db = attn_branch_bwd(z3.reshape(3, L, d * D), dob.reshape(L, d * D),
                                   ob.reshape(L, d * D), _lse_compact(lse_b, d), bias3[g], d,
                                   f"attn_bwd_d{d}")
        dzs.append(dz_d.reshape(3, S, D))
        dbs.append(db.reshape(H, BLK * 2 * BLK))
    return sum3_cast(*dzs, "attn_dz_sum"), jnp.stack(dbs)


def _me():
    return lax.axis_index("x"), lax.axis_index("y"), lax.axis_index("c")


def _other_chips(x, y):
    return [(1 - x, y), (x, 1 - y), (1 - x, 1 - y)]


def _shard_window(ref, axis, t, shape):
    R, C = shape
    if axis == 0:
        return ref.at[pl.ds(pl.multiple_of(t * R, 128), R), :]
    return ref.at[:, pl.ds(pl.multiple_of(t * C, 128), C)]


def all_gather_weights(shards, axes, name):
    n = len(shards)
    shapes = [s.shape for s in shards]
    outs_shape = [jax.ShapeDtypeStruct((8 * R, C) if ax == 0 else (R, 8 * C), BF)
                  for (R, C), ax in zip(shapes, axes)]

    def body(*refs):
        ins, outs = refs[:n], refs[n:2 * n]
        send_sems, recv_sems, local_sems = refs[2 * n:]
        x, y, c = _me()
        sibling = (x, y, 1 - c)
        chips = _other_chips(x, y)
        barrier = pltpu.get_barrier_semaphore()
        for peer in [sibling] + [(*chip, c) for chip in chips]:
            pl.semaphore_signal(barrier, inc=1, device_id=peer, device_id_type=MESH)
        pl.semaphore_wait(barrier, 4)

        def win(i, px, py, pc):
            return _shard_window(outs[i], axes[i], 4 * px + 2 * py + pc, shapes[i])

        def copy(i, k, block, to, src=None):
            return pltpu.make_async_remote_copy(
                src_ref=win(i, *block) if src is None else src, dst_ref=win(i, *block),
                send_sem=send_sems.at[i * 7 + k], recv_sem=recv_sems.at[i * 7 + k],
                device_id=to, device_id_type=MESH)

        mine = [pltpu.make_async_copy(ins[i], win(i, x, y, c), local_sems.at[i]) for i in range(n)]
        for cp in mine:
            cp.start()
        first = []
        for i in range(n):
            first.append(copy(i, 0, (x, y, c), sibling, src=ins[i]))
            for j, chip in enumerate(chips):
                first.append(copy(i, 1 + j, (x, y, c), (*chip, c), src=ins[i]))
        for cp in first:
            cp.start()
        passed = []
        for j, chip in enumerate(chips):
            for i in range(n):
                copy(i, 1 + j, (*chip, c), (x, y, c)).wait_recv()
                cp = copy(i, 4 + j, (*chip, c), sibling)
                cp.start()
                passed.append(cp)
        for i in range(n):
            copy(i, 0, sibling, (x, y, c)).wait_recv()
        for j, chip in enumerate(chips):
            for i in range(n):
                copy(i, 4 + j, (*chip, 1 - c), (x, y, c)).wait_recv()
        for cp in first + passed:
            cp.wait_send()
        for cp in mine:
            cp.wait()

    return pl.kernel(
        body, out_type=outs_shape, name=name,
        mesh=plsc.ScalarSubcoreMesh(axis_name="sequencer", num_cores=1),
        scratch_types=[pltpu.SemaphoreType.DMA((7 * n,)), pltpu.SemaphoreType.DMA((7 * n,)),
                       pltpu.SemaphoreType.DMA((n,))],
        compiler_params=pltpu.CompilerParams(collective_id=1),
    )(*shards)


def pair_exchange_grads(grads, axes, shapes, name):
    n = len(grads)

    def body(*refs):
        ins, outs = refs[:n], refs[n:2 * n]
        send_sems, recv_sems = refs[2 * n:]
        x, y, c = _me()
        sibling = (x, y, 1 - c)
        copies = []
        for i in range(n):
            for q in range(4):
                t = 2 * q + (1 - c)
                copies.append(pltpu.make_async_remote_copy(
                    src_ref=_shard_window(ins[i], axes[i], t, shapes[i]), dst_ref=outs[i].at[q],
                    send_sem=send_sems.at[i * 4 + q], recv_sem=recv_sems.at[i * 4 + q],
                    device_id=sibling, device_id_type=MESH))
        for cp in copies:
            cp.start()
        for cp in copies:
            cp.wait_recv()
        for cp in copies:
            cp.wait_send()

    return pl.pallas_call(
        body, name=name,
        in_specs=[ANY] * n,
        out_specs=[ANY] * n,
        out_shape=[jax.ShapeDtypeStruct((4,) + tuple(sh), BF) for sh in shapes],
        scratch_shapes=[pltpu.SemaphoreType.DMA((4 * n,)), pltpu.SemaphoreType.DMA((4 * n,))],
        compiler_params=pltpu.CompilerParams(has_side_effects=True),
    )(*grads)


def pair_add(grad, landed, axis, shape, c_idx, name):
    R, C = shape

    def body(c_ref, g_ref, l_ref, o_ref):
        o_ref[...] = (g_ref[...].astype(F32) + l_ref[...].astype(F32)).astype(BF)

    if axis == 0:
        g_spec = pl.BlockSpec((R, C), lambda q, c_ref: (2 * q + c_ref[0], 0))
    else:
        g_spec = pl.BlockSpec((R, C), lambda q, c_ref: (0, 2 * q + c_ref[0]))
    blk = pl.BlockSpec((None, R, C), lambda q, c_ref: (q, 0, 0))
    return pl.pallas_call(
        body, name=name,
        grid_spec=pltpu.PrefetchScalarGridSpec(
            num_scalar_prefetch=1, grid=(4,), in_specs=[g_spec, blk], out_specs=blk),
        out_shape=jax.ShapeDtypeStruct((4, R, C), BF),
        compiler_params=_cparams(("parallel",)),
    )(c_idx, grad, landed)


def chip_exchange_grads(parts, name):
    n = len(parts)

    def body(*refs):
        ins, outs = refs[:n], refs[n:2 * n]
        send_sems, recv_sems = refs[2 * n:]
        x, y, c = _me()
        copies = []
        for i in range(n):
            for k, (px, py) in enumerate(_other_chips(x, y)):
                copies.append(pltpu.make_async_remote_copy(
                    src_ref=ins[i].at[2 * px + py], dst_ref=outs[i].at[k],
                    send_sem=send_sems.at[i * 3 + k], recv_sem=recv_sems.at[i * 3 + k],
                    device_id=(px, py, c), device_id_type=MESH))
        for cp in copies:
            cp.start()
        for cp in copies:
            cp.wait_recv()
        for cp in copies:
            cp.wait_send()

    return pl.pallas_call(
        body, name=name,
        in_specs=[ANY] * n,
        out_specs=[ANY] * n,
        out_shape=[jax.ShapeDtypeStruct((3,) + tuple(p.shape[1:]), BF) for p in parts],
        scratch_shapes=[pltpu.SemaphoreType.DMA((3 * n,)), pltpu.SemaphoreType.DMA((3 * n,))],
        compiler_params=pltpu.CompilerParams(has_side_effects=True),
    )(*parts)


def all_gather_small(v, name):
    R, C = v.shape

    def body(v_ref, out_ref, send_sems, recv_sems, local_sem):
        x, y, c = _me()
        me, sibling = (x, y, c), (x, y, 1 - c)
        chips = _other_chips(x, y)

        def slot(px, py, pc):
            return out_ref.at[4 * px + 2 * py + pc]

        def copy(k, block, to, src=None):
            return pltpu.make_async_remote_copy(
                src_ref=slot(*block) if src is None else src, dst_ref=slot(*block),
                send_sem=send_sems.at[k], recv_sem=recv_sems.at[k],
                device_id=to, device_id_type=MESH)

        mine = pltpu.make_async_copy(v_ref, slot(*me), local_sem)
        mine.start()
        first = [copy(0, me, sibling, src=v_ref)]
        first += [copy(1 + j, me, (*chip, c), src=v_ref) for j, chip in enumerate(chips)]
        for cp in first:
            cp.start()
        passed = [copy(4 + j, (*chip, c), sibling) for j, chip in enumerate(chips)]
        for j, chip in enumerate(chips):
            copy(1 + j, (*chip, c), me).wait_recv()
            passed[j].start()
        copy(0, sibling, me).wait_recv()
        for j, chip in enumerate(chips):
            copy(4 + j, (*chip, 1 - c), me).wait_recv()
        for cp in first + passed:
            cp.wait_send()
        mine.wait()

    return pl.pallas_call(
        body, name=name,
        in_specs=[pl.BlockSpec(memory_space=pltpu.VMEM)],
        out_specs=pl.BlockSpec(memory_space=pltpu.VMEM),
        out_shape=jax.ShapeDtypeStruct((NDEV, R, C), F32),
        scratch_shapes=[pltpu.SemaphoreType.DMA((7,)), pltpu.SemaphoreType.DMA((7,)),
                        pltpu.SemaphoreType.DMA],
    )(v)


def _adamw(w, g, m, v):
    m = ADAM_B1 * m + (1.0 - ADAM_B1) * g
    v = ADAM_B2 * v + (1.0 - ADAM_B2) * (g * g)
    m_hat = m / (1.0 - ADAM_B1 ** ADAM_STEP)
    v_hat = v / (1.0 - ADAM_B2 ** ADAM_STEP)
    delta = -ADAM_LR * (m_hat / (jnp.sqrt(v_hat) + ADAM_EPS) + ADAM_WD * w)
    return delta, m, v


def reduce_adamw(part, landed, w, m, v, layer, q_idx, name):
    R, C = part.shape[1:]
    r, c = w.shape[1:]
    tr = r // 2 if r % 16 == 0 and r >= 256 else r
    tR = tr if tr != r else R

    def body(q_ref, p_ref, l_ref, w_ref, m_ref, v_ref, g_out, d_out, m_out, v_out):
        g = p_ref[...].astype(F32)
        for k in range(3):
            g = g + l_ref[k].astype(F32)
        g = g[:tr, :c]
        d, mm, vv = _adamw(w_ref[...], g, m_ref[...], v_ref[...])
        g_out[...] = g
        d_out[...] = d
        m_out[...] = mm
        v_out[...] = vv

    wspec = pl.BlockSpec((None, tr, c), lambda i, q_ref: (layer, i, 0))
    ospec = pl.BlockSpec((tr, c), lambda i, q_ref: (i, 0))
    out = jax.ShapeDtypeStruct((r, c), F32)
    return pl.pallas_call(
        body, name=name,
        grid_spec=pltpu.PrefetchScalarGridSpec(
            num_scalar_prefetch=1, grid=(r // tr,),
            in_specs=[pl.BlockSpec((None, tR, C), lambda i, q_ref: (q_ref[0], i, 0)),
                      pl.BlockSpec((3, tR, C), lambda i, q_ref: (0, i, 0)),
                      wspec, wspec, wspec],
            out_specs=[ospec] * 4),
        out_shape=[out] * 4,
        compiler_params=_cparams(("parallel",)),
    )(q_idx, part, landed, w, m, v)


def small_reduce_adamw(gathered, w, m, v, name):
    R, C = w.shape

    def body(a_ref, w_ref, m_ref, v_ref, g_out, d_out, m_out, v_out):
        g = a_ref[0]
        for k in range(1, NDEV):
            g = g + a_ref[k]
        d, mm, vv = _adamw(w_ref[...], g, m_ref[...], v_ref[...])
        g_out[...] = g
        d_out[...] = d
        m_out[...] = mm
        v_out[...] = vv

    out = jax.ShapeDtypeStruct((R, C), F32)
    return pl.pallas_call(body, name=name, out_shape=[out] * 4,
                          compiler_params=_cparams())(gathered, w, m, v)


def _pad_cols(a, n):
    return jnp.pad(a, ((0, 0), (0, n - a.shape[1])))


def _pad_rows(a, n):
    return jnp.pad(a, ((0, n - a.shape[0]), (0, 0)))


SMALL_ROWS = 16


def _pack_small(mix, ffn, fin, taps_full, relb):
    return jnp.concatenate([
        mix, ffn, fin.reshape(1, D), taps_full.reshape(6, D),
        jnp.pad(relb.reshape(1, NUM_BUCKETS * H), ((0, 0), (0, D - NUM_BUCKETS * H)))], axis=0)


def kernel(x, mix_norm, ffn_norm, final_norm, conv_w_in, conv_kernel, conv_w_out, attn_w_qkv, attn_w_out, rel_bias, ffn_w_gate, ffn_w_up, ffn_w_down, loss_target, m_mix_norm, m_ffn_norm, m_final_norm, m_conv_w_in, m_conv_kernel, m_conv_w_out, m_attn_w_qkv, m_attn_w_out, m_rel_bias, m_ffn_w_gate, m_ffn_w_up, m_ffn_w_down, v_mix_norm, v_ffn_norm, v_final_norm, v_conv_w_in, v_conv_kernel, v_conv_w_out, v_attn_w_qkv, v_attn_w_out, v_rel_bias, v_ffn_w_gate, v_ffn_w_up, v_ffn_w_down):
    xi, yi, ci = _me()
    me = 4 * xi + 2 * yi + ci
    c_idx = jnp.reshape(ci, (1,)).astype(jnp.int32)
    q_idx = jnp.reshape(2 * xi + yi, (1,)).astype(jnp.int32)
    col0 = me * (D // NDEV)

    taps_local = jnp.zeros((2, 3, D), F32)
    taps_local = lax.dynamic_update_slice(taps_local, conv_kernel, (0, 0, col0))
    taps_pack = jnp.pad(taps_local.reshape(6, D), ((0, 2), (0, 0)))
    taps_all = all_gather_small(taps_pack, "ag_taps")
    taps_sum = jnp.sum(taps_all, axis=0)
    taps = [jnp.pad(taps_sum[3 * j:3 * j + 3], ((0, 5), (0, 0))) for j in range(2)]

    mixer_in = (conv_w_in, attn_w_qkv)
    mixer_out = (conv_w_out, attn_w_out)
    wts = []
    for i in range(DEPTH):
        j = i // 2
        shards = [mixer_in[i % 2][j].astype(BF), mixer_out[i % 2][j].astype(BF),
                  _pad_cols(ffn_w_gate[i].astype(BF), FF_SHARD_PAD),
                  _pad_cols(ffn_w_up[i].astype(BF), FF_SHARD_PAD),
                  _pad_rows(ffn_w_down[i].astype(BF), FF_SHARD_PAD)]
        wts.append(all_gather_weights(shards, (1, 0, 1, 1, 0), f"ag_weights_l{i}"))

    onehot_t, band = _bucket_onehot_t()
    bias3 = bias_tables(rel_bias.T, onehot_t, band, "bias_tables").reshape(3, H, BLK, 2 * BLK)

    saved = []
    xc = x[0]
    for i in range(DEPTH):
        w_in, w_out, w_g, w_u, w_d = wts[i]
        j = i // 2
        x_mix = xc
        z3, h_mix = norm_matmul3(xc, mix_norm[i:i + 1], w_in, f"mix_in_l{i}")
        if i % 2 == 0:
            act = conv_fwd(z3, taps[j], f"conv_fwd_l{i}")
            lse_b = None
        else:
            act, lse_b = attention_fwd(z3, bias3)
        xc = matmul_residual(act, w_out, xc, f"mix_out_l{i}")
        x_ffn = xc
        g, u, a, h_ffn = norm_swiglu_up(xc, ffn_norm[i:i + 1], w_g, w_u, f"ffn_up_l{i}")
        xc = matmul_residual(a, w_d, xc, f"ffn_down_l{i}")
        saved.append((x_mix, h_mix, z3, act, lse_b, x_ffn, h_ffn, g, u, a))

    dx, dxb, dg_final, sq = loss_head(xc, final_norm.reshape(1, D), loss_target[0], "loss_head")
    loss = lax.psum(0.5 * jnp.sum(sq[0]) / D, ("x", "y", "c"))

    w_params = ((conv_w_in, m_conv_w_in, v_conv_w_in), (attn_w_qkv, m_attn_w_qkv, v_attn_w_qkv))
    o_params = ((conv_w_out, m_conv_w_out, v_conv_w_out), (attn_w_out, m_attn_w_out, v_attn_w_out))
    results = {}
    dg_mix = [None] * DEPTH
    dg_ffn = [None] * DEPTH
    dtaps = [None, None]
    dbias_all = []
    axes = (1, 0, 1, 1, 0)
    shapes = ((D, 3 * D // NDEV), (D // NDEV, D), (D, FF_SHARD_PAD), (D, FF_SHARD_PAD), (FF_SHARD_PAD, D))
    for i in reversed(range(DEPTH)):
        w_in, w_out, w_g, w_u, w_d = wts[i]
        j = i // 2
        x_mix, h_mix, z3, act, lse_b, x_ffn, h_ffn, g, u, a = saved[i]
        dgate, dup = swiglu_bwd_da(dxb, w_d, g, u, f"ffn_da_l{i}")
        gw_d = matmul_tn(a, dxb, f"ffn_dwd_l{i}")
        gw_g = matmul_tn(h_ffn, dgate, f"ffn_dwg_l{i}")
        gw_u = matmul_tn(h_ffn, dup, f"ffn_dwu_l{i}")
        dx, dxb, dg_ffn[i] = matmul_nt_normbwd(
            [(dgate, w_g, False), (dup, w_u, False)], x_ffn, ffn_norm[i:i + 1], dx, f"ffn_dh_l{i}", tk=768)
        gw_out = matmul_tn(act, dxb, f"mix_dwout_l{i}")
        if i % 2 == 0:
            dm = matmul_nt(dxb, w_out, f"mix_dact_l{i}", out_dtype=F32)
            dz3, dtaps[j] = conv_bwd(dm, z3, taps[j], f"conv_bwd_l{i}")
        else:
            dob = matmul_nt(dxb, w_out, f"mix_dact_l{i}", out_dtype=BF)
            dz3, dbias3 = attention_bwd(z3, dob, act, lse_b, bias3)
            dbias_all.append(dbias3)
        gw_in = matmul_tn(h_mix, dz3, f"mix_dwin_l{i}")
        dx, dxb, dg_mix[i] = matmul_nt_normbwd(
            [(dz3, w_in, True)], x_mix, mix_norm[i:i + 1], dx, f"mix_dh_l{i}", tk=512)
        grads = [gw_in, gw_out, gw_g, gw_u, gw_d]
        landed1 = pair_exchange_grads(grads, axes, shapes, f"rs_pair_l{i}")
        parts = [pair_add(grads[t], landed1[t], axes[t], shapes[t], c_idx, f"rs_add_l{i}_{t}")
                 for t in range(5)]
        landed2 = chip_exchange_grads(parts, f"rs_chip_l{i}")
        plist = [w_params[i % 2] + (j,), o_params[i % 2] + (j,),
                 (ffn_w_gate, m_ffn_w_gate, v_ffn_w_gate, i), (ffn_w_up, m_ffn_w_up, v_ffn_w_up, i),
                 (ffn_w_down, m_ffn_w_down, v_ffn_w_down, i)]
        for t in range(5):
            w_, m_, v_, layer = plist[t]
            results[(i, t)] = reduce_adamw(parts[t], landed2[t], w_, m_, v_, layer, q_idx,
                                           f"adamw_l{i}_{t}")

    grad_relb_t = bias_grad(jnp.concatenate(dbias_all), onehot_t, "bias_grad")
    dtaps_full = jnp.stack([dtaps[0][:3], dtaps[1][:3]])
    g_small = _pack_small(jnp.concatenate([d[0:1] for d in dg_mix], axis=0),
                          jnp.concatenate([d[0:1] for d in dg_ffn], axis=0),
                          dg_final[0], dtaps_full, grad_relb_t.T)
    gathered = all_gather_small(g_small, "ag_small_grads")

    def taps_at_cols(k):
        return lax.dynamic_update_slice(jnp.zeros((2, 3, D), F32), k, (0, 0, col0))

    w_small = _pack_small(mix_norm, ffn_norm, final_norm, taps_at_cols(conv_kernel), rel_bias)
    m_small = _pack_small(m_mix_norm, m_ffn_norm, m_final_norm, taps_at_cols(m_conv_kernel), m_rel_bias)
    v_small = _pack_small(v_mix_norm, v_ffn_norm, v_final_norm, taps_at_cols(v_conv_kernel), v_rel_bias)
    small = small_reduce_adamw(gathered, w_small, m_small, v_small, "adamw_small")

    def unpack_small(p):
        taps_p = lax.dynamic_slice(p[9:15].reshape(2, 3, D), (0, 0, col0), (2, 3, D // NDEV))
        return {"mix_norm": p[0:4], "ffn_norm": p[4:8], "final_norm": p[8],
                "conv_kernel": taps_p, "rel_bias": p[15, :NUM_BUCKETS * H].reshape(NUM_BUCKETS, H)}

    small_out = [unpack_small(p) for p in small]

    def big(kind, which):
        if which in (0, 1):
            layers = [i for i in range(DEPTH) if (i % 2 == 0) == (kind == "conv")]
        else:
            layers = list(range(DEPTH))
        return [jnp.stack([results[(i, which)][o] for i in layers]) for o in range(4)]

    big_out = {"conv_w_in": big("conv", 0), "conv_w_out": big("conv", 1),
               "attn_w_qkv": big("attn", 0), "attn_w_out": big("attn", 1),
               "ffn_w_gate": big("ffn", 2), "ffn_w_up": big("ffn", 3), "ffn_w_down": big("ffn", 4)}

    names = ["mix_norm", "ffn_norm", "final_norm", "conv_w_in", "conv_kernel", "conv_w_out",
             "attn_w_qkv", "attn_w_out", "rel_bias", "ffn_w_gate", "ffn_w_up", "ffn_w_down"]
    outs = [loss, dx.reshape(1, S, D)]
    for o in range(4):
        for nme in names:
            outs.append(big_out[nme][o] if nme in big_out else small_out[o][nme])
    return tuple(outs)
```

```python
import functools
import math

import jax
import jax.numpy as jnp
from jax import lax
from jax.experimental import pallas as pl
from jax.experimental.pallas import tpu as pltpu
from jax.experimental.pallas import tpu_sc as plsc

S = 2048
D = 1024
H = 16
DH = 64
DFF = 2816
NDEV = 8
DEPTH = 4
FF_SHARD = DFF // NDEV
FF_SHARD_PAD = 384
DFF_PAD = FF_SHARD_PAD * NDEV
BLK = 128
BRANCH_DILATIONS = (1, 4, 16)
NUM_BUCKETS = 32
MAX_DISTANCE = 2048
EPS = 1e-6
NEG_INF = -1e30
SCALE = DH ** -0.5

ADAM_LR = 0.001
ADAM_B1 = 0.9
ADAM_B2 = 0.999
ADAM_EPS = 1e-08
ADAM_WD = 0.01
ADAM_STEP = 10

BF = jnp.bfloat16
F32 = jnp.float32
VMEM_LIMIT_BYTES = 56 * 1024 * 1024
MESH = pl.DeviceIdType.MESH
ANY = pl.BlockSpec(memory_space=pl.ANY)

_NT = (((1,), (1,)), ((), ()))
_TN = (((0,), (0,)), ((), ()))


def _cparams(sem=None):
    return pltpu.CompilerParams(dimension_semantics=sem, vmem_limit_bytes=VMEM_LIMIT_BYTES)


def _rms(x):
    return lax.rsqrt(jnp.mean(x * x, axis=-1, keepdims=True) + EPS)


def norm_matmul3(x, gain, w, name, tm=512, tn=512):
    per = D // tn

    def body(x_ref, g_ref, w_ref, z_ref, h_ref, hs_ref):
        @pl.when(pl.program_id(1) == 0)
        def _():
            xv = x_ref[...]
            hv = (xv * _rms(xv) * g_ref[...]).astype(BF)
            hs_ref[...] = hv
            h_ref[...] = hv
        z_ref[...] = jnp.dot(hs_ref[...], w_ref[...], preferred_element_type=F32).astype(BF)

    return pl.pallas_call(
        body, name=name,
        grid=(S // tm, 3 * D // tn),
        in_specs=[pl.BlockSpec((tm, D), lambda i, j: (i, 0)),
                  pl.BlockSpec((1, D), lambda i, j: (0, 0)),
                  pl.BlockSpec((D, tn), lambda i, j: (0, j))],
        out_specs=[pl.BlockSpec((None, tm, tn), lambda i, j: (j // per, i, j % per)),
                   pl.BlockSpec((tm, D), lambda i, j: (i, 0))],
        out_shape=[jax.ShapeDtypeStruct((3, S, D), BF), jax.ShapeDtypeStruct((S, D), BF)],
        scratch_shapes=[pltpu.VMEM((tm, D), BF)],
        compiler_params=_cparams(("parallel", "arbitrary")),
    )(x, gain, w)


def norm_swiglu_up(x, gain, wg, wu, name, tm=512, tn=512):
    def body(x_ref, g_ref, wg_ref, wu_ref, go_ref, uo_ref, ao_ref, h_ref, hs_ref):
        @pl.when(pl.program_id(1) == 0)
        def _():
            xv = x_ref[...]
            hv = (xv * _rms(xv) * g_ref[...]).astype(BF)
            hs_ref[...] = hv
            h_ref[...] = hv
        hv = hs_ref[...]
        g = jnp.dot(hv, wg_ref[...], preferred_element_type=F32)
        u = jnp.dot(hv, wu_ref[...], preferred_element_type=F32)
        go_ref[...] = g.astype(BF)
        uo_ref[...] = u.astype(BF)
        ao_ref[...] = (g * jax.nn.sigmoid(g) * u).astype(BF)

    act = jax.ShapeDtypeStruct((S, DFF_PAD), BF)
    blk = pl.BlockSpec((tm, tn), lambda i, j: (i, j))
    return pl.pallas_call(
        body, name=name,
        grid=(S // tm, DFF_PAD // tn),
        in_specs=[pl.BlockSpec((tm, D), lambda i, j: (i, 0)),
                  pl.BlockSpec((1, D), lambda i, j: (0, 0)),
                  pl.BlockSpec((D, tn), lambda i, j: (0, j)),
                  pl.BlockSpec((D, tn), lambda i, j: (0, j))],
        out_specs=[blk, blk, blk, pl.BlockSpec((tm, D), lambda i, j: (i, 0))],
        out_shape=[act, act, act, jax.ShapeDtypeStruct((S, D), BF)],
        scratch_shapes=[pltpu.VMEM((tm, D), BF)],
        compiler_params=_cparams(("parallel", "arbitrary")),
    )(x, gain, wg, wu)


def matmul_residual(a, w, x, name, tm=512, tn=512):
    K = a.shape[1]

    def body(a_ref, w_ref, x_ref, o_ref):
        o_ref[...] = x_ref[...] + jnp.dot(a_ref[...], w_ref[...], preferred_element_type=F32)

    return pl.pallas_call(
        body, name=name,
        grid=(S // tm, D // tn),
        in_specs=[pl.BlockSpec((tm, K), lambda i, j: (i, 0)),
                  pl.BlockSpec((K, tn), lambda i, j: (0, j)),
                  pl.BlockSpec((tm, tn), lambda i, j: (i, j))],
        out_specs=pl.BlockSpec((tm, tn), lambda i, j: (i, j)),
        out_shape=jax.ShapeDtypeStruct((S, D), F32),
        compiler_params=_cparams(("parallel", "parallel")),
    )(a, w, x)


def matmul_nt(a, w, name, out_dtype=BF, tm=512, tn=512):
    K = a.shape[1]
    N = w.shape[0]

    def body(a_ref, w_ref, o_ref):
        o_ref[...] = lax.dot_general(a_ref[...], w_ref[...], _NT,
                                     preferred_element_type=F32).astype(o_ref.dtype)

    return pl.pallas_call(
        body, name=name,
        grid=(S // tm, N // tn),
        in_specs=[pl.BlockSpec((tm, K), lambda i, j: (i, 0)),
                  pl.BlockSpec((tn, K), lambda i, j: (j, 0))],
        out_specs=pl.BlockSpec((tm, tn), lambda i, j: (i, j)),
        out_shape=jax.ShapeDtypeStruct((S, N), out_dtype),
        compiler_params=_cparams(("parallel", "parallel")),
    )(a, w)


def swiglu_bwd_da(dxb, wd, g, u, name, tm=512, tn=512):
    def body(dx_ref, w_ref, g_ref, u_ref, dg_ref, du_ref):
        da = lax.dot_general(dx_ref[...], w_ref[...], _NT, preferred_element_type=F32)
        gv = g_ref[...].astype(F32)
        uv = u_ref[...].astype(F32)
        sig = jax.nn.sigmoid(gv)
        dg_ref[...] = (da * uv * (sig * (1.0 + gv * (1.0 - sig)))).astype(BF)
        du_ref[...] = (da * (gv * sig)).astype(BF)

    act = jax.ShapeDtypeStruct((S, DFF_PAD), BF)
    blk = pl.BlockSpec((tm, tn), lambda i, j: (i, j))
    return pl.pallas_call(
        body, name=name,
        grid=(S // tm, DFF_PAD // tn),
        in_specs=[pl.BlockSpec((tm, D), lambda i, j: (i, 0)),
                  pl.BlockSpec((tn, D), lambda i, j: (j, 0)),
                  blk, blk],
        out_specs=[blk, blk],
        out_shape=[act, act],
        compiler_params=_cparams(("parallel", "parallel")),
    )(dxb, wd, g, u)


def matmul_tn(a, b, name, tm=512, tn=512):
    M = a.shape[1]
    if b.ndim == 3:
        per = D // tn
        N = 3 * D
        b_spec = pl.BlockSpec((None, S, tn), lambda i, j: (j // per, 0, j % per))
    else:
        N = b.shape[1]
        b_spec = pl.BlockSpec((S, tn), lambda i, j: (0, j))

    def body(a_ref, b_ref, o_ref):
        o_ref[...] = lax.dot_general(a_ref[...], b_ref[...], _TN,
                                     preferred_element_type=F32).astype(BF)

    return pl.pallas_call(
        body, name=name,
        grid=(M // tm, N // tn),
        in_specs=[pl.BlockSpec((S, tm), lambda i, j: (0, i)), b_spec],
        out_specs=pl.BlockSpec((tm, tn), lambda i, j: (i, j)),
        out_shape=jax.ShapeDtypeStruct((M, N), BF),
        compiler_params=_cparams(("parallel", "parallel")),
    )(a, b)


def matmul_nt_normbwd(terms, x_in, gain, dx, name, tk, tm=512):
    specs, operands, ranges = [], [], []
    start = 0
    for (a, w, stacked) in terms:
        K = w.shape[1]
        n = K // tk
        lo = start

        def rel(k, lo=lo, n=n):
            return jnp.clip(k - lo, 0, n - 1)

        if stacked:
            per = D // tk
            specs.append(pl.BlockSpec((None, tm, tk),
                                      lambda i, k, rel=rel, per=per: (rel(k) // per, i, rel(k) % per)))
        else:
            specs.append(pl.BlockSpec((tm, tk), lambda i, k, rel=rel: (i, rel(k))))
        specs.append(pl.BlockSpec((D, tk), lambda i, k, rel=rel: (0, rel(k))))
        operands += [a, w]
        ranges.append((lo, lo + n))
        start += n
    nk = start
    nt = len(terms)

    def body(*refs):
        aw = refs[:2 * nt]
        x_ref, g_ref, dx_ref, dxo_ref, dxb_ref, dg_ref, acc_ref = refs[2 * nt:]
        i = pl.program_id(0)
        k = pl.program_id(1)

        @pl.when(k == 0)
        def _():
            acc_ref[...] = jnp.zeros_like(acc_ref)

        @pl.when((i == 0) & (k == 0))
        def _():
            dg_ref[...] = jnp.zeros_like(dg_ref)

        for t in range(nt):
            lo, hi = ranges[t]

            @pl.when((k >= lo) & (k < hi))
            def _(t=t):
                acc_ref[...] += lax.dot_general(aw[2 * t][...], aw[2 * t + 1][...], _NT,
                                                preferred_element_type=F32)

        @pl.when(k == nk - 1)
        def _():
            xv = x_ref[...]
            r = _rms(xv)
            xhat = xv * r
            dh = acc_ref[...]
            dg_ref[0:1, :] += jnp.sum(dh * xhat, axis=0, keepdims=True)
            dxh = dh * g_ref[...]
            dxn = r * (dxh - xhat * jnp.mean(dxh * xhat, axis=-1, keepdims=True))
            out = dx_ref[...] + dxn
            dxo_ref[...] = out
            dxb_ref[...] = out.astype(BF)

    row = pl.BlockSpec((tm, D), lambda i, k: (i, 0))
    return pl.pallas_call(
        body, name=name,
        grid=(S // tm, nk),
        in_specs=specs + [row, pl.BlockSpec((1, D), lambda i, k: (0, 0)), row],
        out_specs=[row, row, pl.BlockSpec((8, D), lambda i, k: (0, 0))],
        out_shape=[jax.ShapeDtypeStruct((S, D), F32), jax.ShapeDtypeStruct((S, D), BF),
                   jax.ShapeDtypeStruct((8, D), F32)],
        scratch_shapes=[pltpu.VMEM((tm, D), F32)],
        compiler_params=_cparams(("arbitrary", "arbitrary")),
    )(*operands, x_in, gain, dx)


def loss_head(x, gain, target, name, tm=512):
    def body(x_ref, g_ref, t_ref, dxo_ref, dxb_ref, dg_ref, sq_ref):
        @pl.when(pl.program_id(0) == 0)
        def _():
            dg_ref[...] = jnp.zeros_like(dg_ref)
            sq_ref[...] = jnp.zeros_like(sq_ref)
        xv = x_ref[...]
        r = _rms(xv)
        xhat = xv * r
        err = xhat * g_ref[...] - t_ref[...]
        sq_ref[0:1, :] += jnp.sum(err * err, axis=0, keepdims=True)
        dy = err * (1.0 / D)
        dg_ref[0:1, :] += jnp.sum(dy * xhat, axis=0, keepdims=True)
        dxh = dy * g_ref[...]
        out = r * (dxh - xhat * jnp.mean(dxh * xhat, axis=-1, keepdims=True))
        dxo_ref[...] = out
        dxb_ref[...] = out.astype(BF)

    row = pl.BlockSpec((tm, D), lambda i: (i, 0))
    acc = pl.BlockSpec((8, D), lambda i: (0, 0))
    return pl.pallas_call(
        body, name=name,
        grid=(S // tm,),
        in_specs=[row, pl.BlockSpec((1, D), lambda i: (0, 0)), row],
        out_specs=[row, row, acc, acc],
        out_shape=[jax.ShapeDtypeStruct((S, D), F32), jax.ShapeDtypeStruct((S, D), BF),
                   jax.ShapeDtypeStruct((8, D), F32), jax.ShapeDtypeStruct((8, D), F32)],
        compiler_params=_cparams(("arbitrary",)),
    )(x, gain, target)


def _shift_down(p, n, row):
    return jnp.where(row >= n, pltpu.roll(p, n, axis=0), 0.0)


def _shift_up(p, n, row):
    return jnp.where(row < S - n, pltpu.roll(p, S - n, axis=0), 0.0)


def conv_fwd(z3, taps, name, tn=128):
    def body(z_ref, k_ref, m_ref):
        b = z_ref[0].astype(F32)
        p = z_ref[1].astype(F32) * z_ref[2].astype(F32)
        row = lax.broadcasted_iota(jnp.int32, p.shape, 0)
        y = (k_ref[2:3, :] * p + k_ref[1:2, :] * _shift_down(p, 1, row)
             + k_ref[0:1, :] * _shift_down(p, 2, row))
        m_ref[...] = (b * y).astype(BF)

    return pl.pallas_call(
        body, name=name,
        grid=(D // tn,),
        in_specs=[pl.BlockSpec((3, S, tn), lambda j: (0, 0, j)),
                  pl.BlockSpec((8, tn), lambda j: (0, j))],
        out_specs=pl.BlockSpec((S, tn), lambda j: (0, j)),
        out_shape=jax.ShapeDtypeStruct((S, D), BF),
        compiler_params=_cparams(("parallel",)),
    )(z3, taps)


def conv_bwd(dm, z3, taps, name, tn=128):
    def body(dm_ref, z_ref, k_ref, dz_ref, dk_ref):
        dmv = dm_ref[...]
        b = z_ref[0].astype(F32)
        c = z_ref[1].astype(F32)
        u = z_ref[2].astype(F32)
        p = c * u
        row = lax.broadcasted_iota(jnp.int32, p.shape, 0)
        p1 = _shift_down(p, 1, row)
        p2 = _shift_down(p, 2, row)
        y = k_ref[2:3, :] * p + k_ref[1:2, :] * p1 + k_ref[0:1, :] * p2
        dy = dmv * b
        dz_ref[0] = (dmv * y).astype(BF)
        dp = (k_ref[2:3, :] * dy + k_ref[1:2, :] * _shift_up(dy, 1, row)
              + k_ref[0:1, :] * _shift_up(dy, 2, row))
        dz_ref[1] = (dp * u).astype(BF)
        dz_ref[2] = (dp * c).astype(BF)
        dk_ref[...] = jnp.zeros_like(dk_ref)
        dk_ref[0:1, :] = jnp.sum(dy * p2, axis=0, keepdims=True)
        dk_ref[1:2, :] = jnp.sum(dy * p1, axis=0, keepdims=True)
        dk_ref[2:3, :] = jnp.sum(dy * p, axis=0, keepdims=True)

    return pl.pallas_call(
        body, name=name,
        grid=(D // tn,),
        in_specs=[pl.BlockSpec((S, tn), lambda j: (0, j)),
                  pl.BlockSpec((3, S, tn), lambda j: (0, 0, j)),
                  pl.BlockSpec((8, tn), lambda j: (0, j))],
        out_specs=[pl.BlockSpec((3, S, tn), lambda j: (0, 0, j)),
                   pl.BlockSpec((8, tn), lambda j: (0, j))],
        out_shape=[jax.ShapeDtypeStruct((3, S, D), BF), jax.ShapeDtypeStruct((8, D), F32)],
        compiler_params=_cparams(("parallel",)),
    )(dm, z3, taps)


def _t5_bucket(dist):
    exact = NUM_BUCKETS // 2
    df = jnp.maximum(dist, 1).astype(jnp.float32)
    large = exact + (jnp.log(df / exact) / math.log(MAX_DISTANCE / exact)
                     * (NUM_BUCKETS - exact)).astype(jnp.int32)
    large = jnp.minimum(large, NUM_BUCKETS - 1)
    return jnp.where(dist < exact, dist, large)


def _bucket_onehot_t():
    qi = jnp.arange(BLK)[:, None]
    ki = jnp.arange(2 * BLK)[None, :]
    rel = qi + BLK - ki
    band = ((rel >= 0) & (rel <= BLK)).reshape(1, -1).astype(F32)
    hots = []
    for d in BRANCH_DILATIONS:
        bucket = _t5_bucket(jnp.clip(rel, 0) * d).reshape(1, -1)
        hots.append((jnp.arange(NUM_BUCKETS)[:, None] == bucket).astype(F32))
    return jnp.stack(hots), band


def bias_tables(rel_bias_t, onehot_t, band, name):
    def body(rb_ref, oh_ref, band_ref, o_ref):
        b = jnp.dot(rb_ref[...], oh_ref[...], preferred_element_type=F32,
                    precision=lax.Precision.HIGHEST)
        o_ref[...] = jnp.where(band_ref[...] > 0.5, b, NEG_INF)

    n = BLK * 2 * BLK
    return pl.pallas_call(
        body, name=name,
        grid=(3,),
        in_specs=[pl.BlockSpec((H, NUM_BUCKETS), lambda g: (0, 0)),
                  pl.BlockSpec((None, NUM_BUCKETS, n), lambda g: (g, 0, 0)),
                  pl.BlockSpec((1, n), lambda g: (0, 0))],
        out_specs=pl.BlockSpec((None, H, n), lambda g: (g, 0, 0)),
        out_shape=jax.ShapeDtypeStruct((3, H, n), F32),
        compiler_params=_cparams(("parallel",)),
    )(rel_bias_t, onehot_t, band)


def bias_grad(dbias, onehot_t, name):
    def body(db_ref, oh_ref, o_ref):
        @pl.when(pl.program_id(0) == 0)
        def _():
            o_ref[...] = jnp.zeros_like(o_ref)
        o_ref[...] += lax.dot_general(db_ref[...], oh_ref[...], _NT, preferred_element_type=F32,
                                      precision=lax.Precision.HIGHEST)

    n = BLK * 2 * BLK
    return pl.pallas_call(
        body, name=name,
        grid=(dbias.shape[0],),
        in_specs=[pl.BlockSpec((None, H, n), lambda g: (g, 0, 0)),
                  pl.BlockSpec((None, NUM_BUCKETS, n), lambda g: (g % 3, 0, 0))],
        out_specs=pl.BlockSpec((H, NUM_BUCKETS), lambda g: (0, 0)),
        out_shape=jax.ShapeDtypeStruct((H, NUM_BUCKETS), F32),
        compiler_params=_cparams(("arbitrary",)),
    )(dbias, onehot_t)


def _head_masks():
    lane = lax.broadcasted_iota(jnp.int32, (1, 2 * DH), 1)
    return (lane < DH, lane >= DH)


def attn_branch_fwd(z3d, bias, d, name):
    L = S // d
    nb = L // BLK

    def body(q_ref, k_ref, v_ref, b_ref, o_ref, lse_ref):
        masks = _head_masks()
        lane2 = lax.broadcasted_iota(jnp.int32, (BLK, 2), 1)

        def block(n, first):
            if first:
                q0 = 0
                kk = k_ref[0:BLK, :]
                vv = v_ref[0:BLK, :]
            else:
                q0 = pl.multiple_of(n * BLK, BLK)
                k0 = pl.multiple_of((n - 1) * BLK, BLK)
                kk = k_ref[pl.ds(k0, 2 * BLK), :]
                vv = v_ref[pl.ds(k0, 2 * BLK), :]
            q = q_ref[pl.ds(q0, BLK), :]
            outs, lses = [], []
            for hh in range(2):
                qh = jnp.where(masks[hh], q, jnp.zeros_like(q))
                bias_h = b_ref[hh][:, BLK:] if first else b_ref[hh]
                s = lax.dot_general(qh, kk, _NT, preferred_element_type=F32) * SCALE + bias_h
                mx = jnp.max(s, axis=1, keepdims=True)
                p = jnp.exp(s - mx)
                l = jnp.sum(p, axis=1, keepdims=True)
                outs.append(jnp.dot(p.astype(BF), vv, preferred_element_type=F32) / l)
                lses.append(mx + jnp.log(l))
            o_ref[pl.ds(q0, BLK), :] = jnp.where(masks[0], outs[0], outs[1])
            lse_ref[pl.ds(q0, BLK), :] = jnp.where(lane2 == 0, lses[0], lses[1])

        block(0, True)
        if nb > 1:
            def loop(n, carry):
                block(n, False)
                return carry
            lax.fori_loop(1, nb, loop, 0)

    col = lambda hp, r: (0, r * 8 + hp)
    return pl.pallas_call(
        body, name=name,
        grid=(8, d),
        in_specs=[pl.BlockSpec((None, L, 2 * DH), lambda hp, r: (0, 0, r * 8 + hp)),
                  pl.BlockSpec((None, L, 2 * DH), lambda hp, r: (1, 0, r * 8 + hp)),
                  pl.BlockSpec((None, L, 2 * DH), lambda hp, r: (2, 0, r * 8 + hp)),
                  pl.BlockSpec((2, BLK, 2 * BLK), lambda hp, r: (hp, 0, 0))],
        out_specs=[pl.BlockSpec((L, 2 * DH), col),
                   pl.BlockSpec((None, L, 2), lambda hp, r: (r * 8 + hp, 0, 0))],
        out_shape=[jax.ShapeDtypeStruct((L, d * D), F32),
                   jax.ShapeDtypeStruct((d * 8, L, 2), F32)],
        compiler_params=_cparams(("parallel", "parallel")),
    )(z3d, z3d, z3d, bias)


def attn_branch_bwd(z3d, dod, od, lse, bias, d, name):
    L = S // d
    nb = L // BLK

    def body(q_ref, k_ref, v_ref, do_ref, o_ref, lse_ref, b_ref, dz_ref, db_ref):
        masks = _head_masks()

        @pl.when(pl.program_id(1) == 0)
        def _():
            db_ref[...] = jnp.zeros_like(db_ref)

        dz_ref[1] = jnp.zeros((L, 2 * DH), F32)
        dz_ref[2] = jnp.zeros((L, 2 * DH), F32)

        def block(n, first):
            if first:
                q0 = 0
                k0 = 0
                nk = BLK
            else:
                q0 = pl.multiple_of(n * BLK, BLK)
                k0 = pl.multiple_of((n - 1) * BLK, BLK)
                nk = 2 * BLK
            kk = k_ref[pl.ds(k0, nk), :]
            vv = v_ref[pl.ds(k0, nk), :]
            q = q_ref[pl.ds(q0, BLK), :]
            do = do_ref[pl.ds(q0, BLK), :]
            of = o_ref[pl.ds(q0, BLK), :].astype(F32)
            lse_blk = lse_ref[pl.ds(q0, BLK), :]
            dqs = []
            dk = jnp.zeros((nk, 2 * DH), F32)
            dv = jnp.zeros((nk, 2 * DH), F32)
            for hh in range(2):
                qh = jnp.where(masks[hh], q, jnp.zeros_like(q))
                doh = jnp.where(masks[hh], do, jnp.zeros_like(do))
                bias_h = b_ref[hh][:, BLK:] if first else b_ref[hh]
                s = lax.dot_general(qh, kk, _NT, preferred_element_type=F32) * SCALE + bias_h
                p = jnp.exp(s - lse_blk[:, hh:hh + 1])
                dp = lax.dot_general(doh, vv, _NT, preferred_element_type=F32)
                delta = jnp.sum(doh.astype(F32) * of, axis=1, keepdims=True)
                ds = p * (dp - delta)
                if first:
                    db_ref[hh, :, BLK:] += ds
                else:
                    db_ref[hh] += ds
                dsb = ds.astype(BF)
                dqs.append(jnp.dot(dsb, kk, preferred_element_type=F32) * SCALE)
                dk += lax.dot_general(dsb, qh, _TN, preferred_element_type=F32) * SCALE
                dv += lax.dot_general(p.astype(BF), doh, _TN, preferred_element_type=F32)
            dz_ref[0, pl.ds(q0, BLK), :] = jnp.where(masks[0], dqs[0], dqs[1])
            dz_ref[1, pl.ds(k0, nk), :] += dk
            dz_ref[2, pl.ds(k0, nk), :] += dv

        block(0, True)
        if nb > 1:
            def loop(n, carry):
                block(n, False)
                return carry
            lax.fori_loop(1, nb, loop, 0)

    col = lambda hp, r: (0, r * 8 + hp)
    return pl.pallas_call(
        body, name=name,
        grid=(8, d),
        in_specs=[pl.BlockSpec((None, L, 2 * DH), lambda hp, r: (0, 0, r * 8 + hp)),
                  pl.BlockSpec((None, L, 2 * DH), lambda hp, r: (1, 0, r * 8 + hp)),
                  pl.BlockSpec((None, L, 2 * DH), lambda hp, r: (2, 0, r * 8 + hp)),
                  pl.BlockSpec((L, 2 * DH), col),
                  pl.BlockSpec((L, 2 * DH), col),
                  pl.BlockSpec((None, L, 2), lambda hp, r: (r * 8 + hp, 0, 0)),
                  pl.BlockSpec((2, BLK, 2 * BLK), lambda hp, r: (hp, 0, 0))],
        out_specs=[pl.BlockSpec((3, L, 2 * DH), lambda hp, r: (0, 0, r * 8 + hp)),
                   pl.BlockSpec((2, BLK, 2 * BLK), lambda hp, r: (hp, 0, 0))],
        out_shape=[jax.ShapeDtypeStruct((3, L, d * D), F32),
                   jax.ShapeDtypeStruct((H, BLK, 2 * BLK), F32)],
        compiler_params=_cparams(("parallel", "arbitrary")),
    )(z3d, z3d, z3d, dod, od, lse, bias)


def attn_combine(o_parts, lse_parts, name, tm=512):
    def body(o1, o2, o3, l1, l2, l3, o_ref, lse_ref):
        a, b, c = l1[...], l2[...], l3[...]
        mx = jnp.maximum(jnp.maximum(a, b), c)
        ea, eb, ec = jnp.exp(a - mx), jnp.exp(b - mx), jnp.exp(c - mx)
        tot = ea + eb + ec
        o_ref[...] = ((ea * o1[...] + eb * o2[...] + ec * o3[...]) / tot).astype(BF)
        lse_ref[...] = mx + jnp.log(tot)

    row = pl.BlockSpec((tm, D), lambda i: (i, 0))
    return pl.pallas_call(
        body, name=name,
        grid=(S // tm,),
        in_specs=[row] * 6,
        out_specs=[row, row],
        out_shape=[jax.ShapeDtypeStruct((S, D), BF), jax.ShapeDtypeStruct((S, D), F32)],
        compiler_params=_cparams(("parallel",)),
    )(*o_parts, *lse_parts)


def sum3_cast(a, b, c, name, tm=512):
    def body(a_ref, b_ref, c_ref, o_ref):
        o_ref[...] = (a_ref[...] + b_ref[...] + c_ref[...]).astype(BF)

    blk = pl.BlockSpec((None, tm, D), lambda k, i: (k, i, 0))
    return pl.pallas_call(
        body, name=name,
        grid=(3, S // tm),
        in_specs=[blk] * 3,
        out_specs=blk,
        out_shape=jax.ShapeDtypeStruct((3, S, D), BF),
        compiler_params=_cparams(("parallel", "parallel")),
    )(a, b, c)


def _lse_compact(lse_b, d):
    L = S // d
    v = lse_b[:, ::DH].reshape(L, d, 8, 2)
    return v.transpose(1, 2, 0, 3).reshape(d * 8, L, 2)


def _lse_broadcast(lse_c, d):
    L = S // d
    v = lse_c.reshape(d, 8, L, 2).transpose(2, 0, 1, 3).reshape(S, H)
    return jnp.repeat(v, DH, axis=1)


def attention_fwd(z3, bias3):
    o_parts, lse_parts = [], []
    for g, d in enumerate(BRANCH_DILATIONS):
        L = S // d
        o_d, lse_c = attn_branch_fwd(z3.reshape(3, L, d * D), bias3[g], d, f"attn_fwd_d{d}")
        o_parts.append(o_d.reshape(S, D))
        lse_parts.append(_lse_broadcast(lse_c, d))
    return attn_combine(o_parts, lse_parts, "attn_combine")


def attention_bwd(z3, dob, ob, lse_b, bias3):
    dzs, dbs = [], []
    for g, d in enumerate(BRANCH_DILATIONS):
        L = S // d
        dz_d, db = attn_branch_bwd(z3.reshape(3, L, d * D), dob.reshape(L, d * D),
                                   ob.reshape(L, d * D), _lse_compact(lse_b, d), bias3[g], d,
                                   f"attn_bwd_d{d}")
        dzs.append(dz_d.reshape(3, S, D))
        dbs.append(db.reshape(H, BLK * 2 * BLK))
    return sum3_cast(*dzs, "attn_dz_sum"), jnp.stack(dbs)


def _me():
    return lax.axis_index("x"), lax.axis_index("y"), lax.axis_index("c")


def _other_chips(x, y):
    return [(1 - x, y), (x, 1 - y), (1 - x, 1 - y)]


def _shard_window(ref, axis, t, shape):
    R, C = shape
    if axis == 0:
        return ref.at[pl.ds(pl.multiple_of(t * R, 128), R), :]
    return ref.at[:, pl.ds(pl.multiple_of(t * C, 128), C)]


def all_gather_weights(shards, axes, name):
    n = len(shards)
    shapes = [s.shape for s in shards]
    outs_shape = [jax.ShapeDtypeStruct((8 * R, C) if ax == 0 else (R, 8 * C), BF)
                  for (R, C), ax in zip(shapes, axes)]

    def body(*refs):
        ins, outs = refs[:n], refs[n:2 * n]
        send_sems, recv_sems, local_sems = refs[2 * n:]
        x, y, c = _me()
        sibling = (x, y, 1 - c)
        chips = _other_chips(x, y)
        barrier = pltpu.get_barrier_semaphore()
        for peer in [sibling] + [(*chip, c) for chip in chips]:
            pl.semaphore_signal(barrier, inc=1, device_id=peer, device_id_type=MESH)
        pl.semaphore_wait(barrier, 4)

        def win(i, px, py, pc):
            return _shard_window(outs[i], axes[i], 4 * px + 2 * py + pc, shapes[i])

        def copy(i, k, block, to, src=None):
            return pltpu.make_async_remote_copy(
                src_ref=win(i, *block) if src is None else src, dst_ref=win(i, *block),
                send_sem=send_sems.at[i * 7 + k], recv_sem=recv_sems.at[i * 7 + k],
                device_id=to, device_id_type=MESH)

        mine = [pltpu.make_async_copy(ins[i], win(i, x, y, c), local_sems.at[i]) for i in range(n)]
        for cp in mine:
            cp.start()
        first = []
        for i in range(n):
            first.append(copy(i, 0, (x, y, c), sibling, src=ins[i]))
            for j, chip in enumerate(chips):
                first.append(copy(i, 1 + j, (x, y, c), (*chip, c), src=ins[i]))
        for cp in first:
            cp.start()
        passed = []
        for j, chip in enumerate(chips):
            for i in range(n):
                copy(i, 1 + j, (*chip, c), (x, y, c)).wait_recv()
                cp = copy(i, 4 + j, (*chip, c), sibling)
                cp.start()
                passed.append(cp)
        for i in range(n):
            copy(i, 0, sibling, (x, y, c)).wait_recv()
        for j, chip in enumerate(chips):
            for i in range(n):
                copy(i, 4 + j, (*chip, 1 - c), (x, y, c)).wait_recv()
        for cp in first + passed:
            cp.wait_send()
        for cp in mine:
            cp.wait()

    return pl.kernel(
        body, out_type=outs_shape, name=name,
        mesh=plsc.ScalarSubcoreMesh(axis_name="sequencer", num_cores=1),
        scratch_types=[pltpu.SemaphoreType.DMA((7 * n,)), pltpu.SemaphoreType.DMA((7 * n,)),
                       pltpu.SemaphoreType.DMA((n,))],
        compiler_params=pltpu.CompilerParams(collective_id=1),
    )(*shards)


def pair_exchange_grads(grads, axes, shapes, name):
    n = len(grads)

    def body(*refs):
        ins, outs = refs[:n], refs[n:2 * n]
        send_sems, recv_sems = refs[2 * n:]
        x, y, c = _me()
        sibling = (x, y, 1 - c)
        barrier = pltpu.get_barrier_semaphore()
        pl.semaphore_signal(barrier, inc=1, device_id=sibling, device_id_type=MESH)
        pl.semaphore_wait(barrier, 1)
        copies = []
        for i in range(n):
            for q in range(4):
                t = 2 * q + (1 - c)
                copies.append(pltpu.make_async_remote_copy(
                    src_ref=_shard_window(ins[i], axes[i], t, shapes[i]), dst_ref=outs[i].at[q],
                    send_sem=send_sems.at[i * 4 + q], recv_sem=recv_sems.at[i * 4 + q],
                    device_id=sibling, device_id_type=MESH))
        for cp in copies:
            cp.start()
        for cp in copies:
            cp.wait_recv()
        for cp in copies:
            cp.wait_send()

    return pl.kernel(
        body, out_type=[jax.ShapeDtypeStruct((4,) + tuple(sh), BF) for sh in shapes], name=name,
        mesh=plsc.ScalarSubcoreMesh(axis_name="sequencer", num_cores=1),
        scratch_types=[pltpu.SemaphoreType.DMA((4 * n,)), pltpu.SemaphoreType.DMA((4 * n,))],
        compiler_params=pltpu.CompilerParams(collective_id=2),
    )(*grads)


def pair_add(grad, landed, axis, shape, c_idx, name):
    R, C = shape

    def body(c_ref, g_ref, l_ref, o_ref):
        o_ref[...] = (g_ref[...].astype(F32) + l_ref[...].astype(F32)).astype(BF)

    if axis == 0:
        g_spec = pl.BlockSpec((R, C), lambda q, c_ref: (2 * q + c_ref[0], 0))
    else:
        g_spec = pl.BlockSpec((R, C), lambda q, c_ref: (0, 2 * q + c_ref[0]))
    blk = pl.BlockSpec((None, R, C), lambda q, c_ref: (q, 0, 0))
    return pl.pallas_call(
        body, name=name,
        grid_spec=pltpu.PrefetchScalarGridSpec(
            num_scalar_prefetch=1, grid=(4,), in_specs=[g_spec, blk], out_specs=blk),
        out_shape=jax.ShapeDtypeStruct((4, R, C), BF),
        compiler_params=_cparams(("parallel",)),
    )(c_idx, grad, landed)


def chip_exchange_grads(parts, name):
    n = len(parts)

    def body(*refs):
        ins, outs = refs[:n], refs[n:2 * n]
        send_sems, recv_sems = refs[2 * n:]
        x, y, c = _me()
        barrier = pltpu.get_barrier_semaphore()
        for px, py in _other_chips(x, y):
            pl.semaphore_signal(barrier, inc=1, device_id=(px, py, c), device_id_type=MESH)
        pl.semaphore_wait(barrier, 3)
        copies = []
        for i in range(n):
            for k, (px, py) in enumerate(_other_chips(x, y)):
                copies.append(pltpu.make_async_remote_copy(
                    src_ref=ins[i].at[2 * px + py], dst_ref=outs[i].at[k],
                    send_sem=send_sems.at[i * 3 + k], recv_sem=recv_sems.at[i * 3 + k],
                    device_id=(px, py, c), device_id_type=MESH))
        for cp in copies:
            cp.start()
        for cp in copies:
            cp.wait_recv()
        for cp in copies:
            cp.wait_send()

    return pl.kernel(
        body, out_type=[jax.ShapeDtypeStruct((3,) + tuple(p.shape[1:]), BF) for p in parts], name=name,
        mesh=plsc.ScalarSubcoreMesh(axis_name="sequencer", num_cores=1),
        scratch_types=[pltpu.SemaphoreType.DMA((3 * n,)), pltpu.SemaphoreType.DMA((3 * n,))],
        compiler_params=pltpu.CompilerParams(collective_id=3),
    )(*parts)


def all_gather_small(v, name):
    R, C = v.shape

    def body(v_ref, out_ref, send_sems, recv_sems, local_sem):
        x, y, c = _me()
        me, sibling = (x, y, c), (x, y, 1 - c)
        chips = _other_chips(x, y)

        def slot(px, py, pc):
            return out_ref.at[4 * px + 2 * py + pc]

        def copy(k, block, to, src=None):
            return pltpu.make_async_remote_copy(
                src_ref=slot(*block) if src is None else src, dst_ref=slot(*block),
                send_sem=send_sems.at[k], recv_sem=recv_sems.at[k],
                device_id=to, device_id_type=MESH)

        mine = pltpu.make_async_copy(v_ref, slot(*me), local_sem)
        mine.start()
        first = [copy(0, me, sibling, src=v_ref)]
        first += [copy(1 + j, me, (*chip, c), src=v_ref) for j, chip in enumerate(chips)]
        for cp in first:
            cp.start()
        passed = [copy(4 + j, (*chip, c), sibling) for j, chip in enumerate(chips)]
        for j, chip in enumerate(chips):
            copy(1 + j, (*chip, c), me).wait_recv()
            passed[j].start()
        copy(0, sibling, me).wait_recv()
        for j, chip in enumerate(chips):
            copy(4 + j, (*chip, 1 - c), me).wait_recv()
        for cp in first + passed:
            cp.wait_send()
        mine.wait()

    return pl.pallas_call(
        body, name=name,
        in_specs=[pl.BlockSpec(memory_space=pltpu.VMEM)],
        out_specs=pl.BlockSpec(memory_space=pltpu.VMEM),
        out_shape=jax.ShapeDtypeStruct((NDEV, R, C), F32),
        scratch_shapes=[pltpu.SemaphoreType.DMA((7,)), pltpu.SemaphoreType.DMA((7,)),
                        pltpu.SemaphoreType.DMA],
    )(v)


def _adamw(w, g, m, v):
    m = ADAM_B1 * m + (1.0 - ADAM_B1) * g
    v = ADAM_B2 * v + (1.0 - ADAM_B2) * (g * g)
    m_hat = m / (1.0 - ADAM_B1 ** ADAM_STEP)
    v_hat = v / (1.0 - ADAM_B2 ** ADAM_STEP)
    delta = -ADAM_LR * (m_hat / (jnp.sqrt(v_hat) + ADAM_EPS) + ADAM_WD * w)
    return delta, m, v


def reduce_adamw(part, landed, w, m, v, layer, q_idx, name):
    R, C = part.shape[1:]
    r, c = w.shape[1:]
    tr = r // 2 if r % 16 == 0 and r >= 256 else r
    tR = tr if tr != r else R

    def body(q_ref, p_ref, l_ref, w_ref, m_ref, v_ref, g_out, d_out, m_out, v_out):
        g = p_ref[...].astype(F32)
        for k in range(3):
            g = g + l_ref[k].astype(F32)
        g = g[:tr, :c]
        d, mm, vv = _adamw(w_ref[...], g, m_ref[...], v_ref[...])
        g_out[...] = g
        d_out[...] = d
        m_out[...] = mm
        v_out[...] = vv

    wspec = pl.BlockSpec((None, tr, c), lambda i, q_ref: (layer, i, 0))
    ospec = pl.BlockSpec((tr, c), lambda i, q_ref: (i, 0))
    out = jax.ShapeDtypeStruct((r, c), F32)
    return pl.pallas_call(
        body, name=name,
        grid_spec=pltpu.PrefetchScalarGridSpec(
            num_scalar_prefetch=1, grid=(r // tr,),
            in_specs=[pl.BlockSpec((None, tR, C), lambda i, q_ref: (q_ref[0], i, 0)),
                      pl.BlockSpec((3, tR, C), lambda i, q_ref: (0, i, 0)),
                      wspec, wspec, wspec],
            out_specs=[ospec] * 4),
        out_shape=[out] * 4,
        compiler_params=_cparams(("parallel",)),
    )(q_idx, part, landed, w, m, v)


def small_reduce_adamw(gathered, w, m, v, name):
    R, C = w.shape

    def body(a_ref, w_ref, m_ref, v_ref, g_out, d_out, m_out, v_out):
        g = a_ref[0]
        for k in range(1, NDEV):
            g = g + a_ref[k]
        d, mm, vv = _adamw(w_ref[...], g, m_ref[...], v_ref[...])
        g_out[...] = g
        d_out[...] = d
        m_out[...] = mm
        v_out[...] = vv

    out = jax.ShapeDtypeStruct((R, C), F32)
    return pl.pallas_call(body, name=name, out_shape=[out] * 4,
                          compiler_params=_cparams())(gathered, w, m, v)


def _pad_cols(a, n):
    return jnp.pad(a, ((0, 0), (0, n - a.shape[1])))


def _pad_rows(a, n):
    return jnp.pad(a, ((0, n - a.shape[0]), (0, 0)))


SMALL_ROWS = 16


def _pack_small(mix, ffn, fin, taps_full, relb):
    return jnp.concatenate([
        mix, ffn, fin.reshape(1, D), taps_full.reshape(6, D),
        jnp.pad(relb.reshape(1, NUM_BUCKETS * H), ((0, 0), (0, D - NUM_BUCKETS * H)))], axis=0)


def kernel(x, mix_norm, ffn_norm, final_norm, conv_w_in, conv_kernel, conv_w_out, attn_w_qkv, attn_w_out, rel_bias, ffn_w_gate, ffn_w_up, ffn_w_down, loss_target, m_mix_norm, m_ffn_norm, m_final_norm, m_conv_w_in, m_conv_kernel, m_conv_w_out, m_attn_w_qkv, m_attn_w_out, m_rel_bias, m_ffn_w_gate, m_ffn_w_up, m_ffn_w_down, v_mix_norm, v_ffn_norm, v_final_norm, v_conv_w_in, v_conv_kernel, v_conv_w_out, v_attn_w_qkv, v_attn_w_out, v_rel_bias, v_ffn_w_gate, v_ffn_w_up, v_ffn_w_down):
    xi, yi, ci = _me()
    me = 4 * xi + 2 * yi + ci
    c_idx = jnp.reshape(ci, (1,)).astype(jnp.int32)
    q_idx = jnp.reshape(2 * xi + yi, (1,)).astype(jnp.int32)
    col0 = me * (D // NDEV)

    taps_local = jnp.zeros((2, 3, D), F32)
    taps_local = lax.dynamic_update_slice(taps_local, conv_kernel, (0, 0, col0))
    taps_pack = jnp.pad(taps_local.reshape(6, D), ((0, 2), (0, 0)))
    taps_all = all_gather_small(taps_pack, "ag_taps")
    taps_sum = jnp.sum(taps_all, axis=0)
    taps = [jnp.pad(taps_sum[3 * j:3 * j + 3], ((0, 5), (0, 0))) for j in range(2)]

    mixer_in = (conv_w_in, attn_w_qkv)
    mixer_out = (conv_w_out, attn_w_out)
    wts = []
    for i in range(DEPTH):
        j = i // 2
        shards = [mixer_in[i % 2][j].astype(BF), mixer_out[i % 2][j].astype(BF),
                  _pad_cols(ffn_w_gate[i].astype(BF), FF_SHARD_PAD),
                  _pad_cols(ffn_w_up[i].astype(BF), FF_SHARD_PAD),
                  _pad_rows(ffn_w_down[i].astype(BF), FF_SHARD_PAD)]
        wts.append(all_gather_weights(shards, (1, 0, 1, 1, 0), f"ag_weights_l{i}"))

    onehot_t, band = _bucket_onehot_t()
    bias3 = bias_tables(rel_bias.T, onehot_t, band, "bias_tables").reshape(3, H, BLK, 2 * BLK)

    saved = []
    xc = x[0]
    for i in range(DEPTH):
        w_in, w_out, w_g, w_u, w_d = wts[i]
        j = i // 2
        x_mix = xc
        z3, h_mix = norm_matmul3(xc, mix_norm[i:i + 1], w_in, f"mix_in_l{i}")
        if i % 2 == 0:
            act = conv_fwd(z3, taps[j], f"conv_fwd_l{i}")
            lse_b = None
        else:
            act, lse_b = attention_fwd(z3, bias3)
        xc = matmul_residual(act, w_out, xc, f"mix_out_l{i}")
        x_ffn = xc
        g, u, a, h_ffn = norm_swiglu_up(xc, ffn_norm[i:i + 1], w_g, w_u, f"ffn_up_l{i}")
        xc = matmul_residual(a, w_d, xc, f"ffn_down_l{i}")
        saved.append((x_mix, h_mix, z3, act, lse_b, x_ffn, h_ffn, g, u, a))

    dx, dxb, dg_final, sq = loss_head(xc, final_norm.reshape(1, D), loss_target[0], "loss_head")
    loss = lax.psum(0.5 * jnp.sum(sq[0]) / D, ("x", "y", "c"))

    w_params = ((conv_w_in, m_conv_w_in, v_conv_w_in), (attn_w_qkv, m_attn_w_qkv, v_attn_w_qkv))
    o_params = ((conv_w_out, m_conv_w_out, v_conv_w_out), (attn_w_out, m_attn_w_out, v_attn_w_out))
    results = {}
    dg_mix = [None] * DEPTH
    dg_ffn = [None] * DEPTH
    dtaps = [None, None]
    dbias_all = []
    axes = (1, 0, 1, 1, 0)
    shapes = ((D, 3 * D // NDEV), (D // NDEV, D), (D, FF_SHARD_PAD), (D, FF_SHARD_PAD), (FF_SHARD_PAD, D))
    for i in reversed(range(DEPTH)):
        w_in, w_out, w_g, w_u, w_d = wts[i]
        j = i // 2
        x_mix, h_mix, z3, act, lse_b, x_ffn, h_ffn, g, u, a = saved[i]
        dgate, dup = swiglu_bwd_da(dxb, w_d, g, u, f"ffn_da_l{i}")
        gw_d = matmul_tn(a, dxb, f"ffn_dwd_l{i}")
        gw_g = matmul_tn(h_ffn, dgate, f"ffn_dwg_l{i}")
        gw_u = matmul_tn(h_ffn, dup, f"ffn_dwu_l{i}")
        dx, dxb, dg_ffn[i] = matmul_nt_normbwd(
            [(dgate, w_g, False), (dup, w_u, False)], x_ffn, ffn_norm[i:i + 1], dx, f"ffn_dh_l{i}", tk=768)
        gw_out = matmul_tn(act, dxb, f"mix_dwout_l{i}")
        if i % 2 == 0:
            dm = matmul_nt(dxb, w_out, f"mix_dact_l{i}", out_dtype=F32)
            dz3, dtaps[j] = conv_bwd(dm, z3, taps[j], f"conv_bwd_l{i}")
        else:
            dob = matmul_nt(dxb, w_out, f"mix_dact_l{i}", out_dtype=BF)
            dz3, dbias3 = attention_bwd(z3, dob, act, lse_b, bias3)
            dbias_all.append(dbias3)
        gw_in = matmul_tn(h_mix, dz3, f"mix_dwin_l{i}")
        dx, dxb, dg_mix[i] = matmul_nt_normbwd(
            [(dz3, w_in, True)], x_mix, mix_norm[i:i + 1], dx, f"mix_dh_l{i}", tk=512)
        grads = [gw_in, gw_out, gw_g, gw_u, gw_d]
        landed1 = pair_exchange_grads(grads, axes, shapes, f"rs_pair_l{i}")
        parts = [pair_add(grads[t], landed1[t], axes[t], shapes[t], c_idx, f"rs_add_l{i}_{t}")
                 for t in range(5)]
        landed2 = chip_exchange_grads(parts, f"rs_chip_l{i}")
        plist = [w_params[i % 2] + (j,), o_params[i % 2] + (j,),
                 (ffn_w_gate, m_ffn_w_gate, v_ffn_w_gate, i), (ffn_w_up, m_ffn_w_up, v_ffn_w_up, i),
                 (ffn_w_down, m_ffn_w_down, v_ffn_w_down, i)]
        for t in range(5):
            w_, m_, v_, layer = plist[t]
            results[(i, t)] = reduce_adamw(parts[t], landed2[t], w_, m_, v_, layer, q_idx,
                                           f"adamw_l{i}_{t}")

    grad_relb_t = bias_grad(jnp.concatenate(dbias_all), onehot_t, "bias_grad")
    dtaps_full = jnp.stack([dtaps[0][:3], dtaps[1][:3]])
    g_small = _pack_small(jnp.concatenate([d[0:1] for d in dg_mix], axis=0),
                          jnp.concatenate([d[0:1] for d in dg_ffn], axis=0),
                          dg_final[0], dtaps_full, grad_relb_t.T)
    gathered = all_gather_small(g_small, "ag_small_grads")

    def taps_at_cols(k):
        return lax.dynamic_update_slice(jnp.zeros((2, 3, D), F32), k, (0, 0, col0))

    w_small = _pack_small(mix_norm, ffn_norm, final_norm, taps_at_cols(conv_kernel), rel_bias)
    m_small = _pack_small(m_mix_norm, m_ffn_norm, m_final_norm, taps_at_cols(m_conv_kernel), m_rel_bias)
    v_small = _pack_small(v_mix_norm, v_ffn_norm, v_final_norm, taps_at_cols(v_conv_kernel), v_rel_bias)
    small = small_reduce_adamw(gathered, w_small, m_small, v_small, "adamw_small")

    def unpack_small(p):
        taps_p = lax.dynamic_slice(p[9:15].reshape(2, 3, D), (0, 0, col0), (2, 3, D // NDEV))
        return {"mix_norm": p[0:4], "ffn_norm": p[4:8], "final_norm": p[8],
                "conv_kernel": taps_p, "rel_bias": p[15, :NUM_BUCKETS * H].reshape(NUM_BUCKETS, H)}

    small_out = [unpack_small(p) for p in small]

    def big(kind, which):
        if which in (0, 1):
            layers = [i for i in range(DEPTH) if (i % 2 == 0) == (kind == "conv")]
        else:
            layers = list(range(DEPTH))
        return [jnp.stack([results[(i, which)][o] for i in layers]) for o in range(4)]

    big_out = {"conv_w_in": big("conv", 0), "conv_w_out": big("conv", 1),
               "attn_w_qkv": big("attn", 0), "attn_w_out": big("attn", 1),
               "ffn_w_gate": big("ffn", 2), "ffn_w_up": big("ffn", 3), "ffn_w_down": big("ffn", 4)}

    names = ["mix_norm", "ffn_norm", "final_norm", "conv_w_in", "conv_kernel", "conv_w_out",
             "attn_w_qkv", "attn_w_out", "rel_bias", "ffn_w_gate", "ffn_w_up", "ffn_w_down"]
    outs = [loss, dx.reshape(1, S, D)]
    for o in range(4):
        for nme in names:
            outs.append(big_out[nme][o] if nme in big_out else small_out[o][nme])
    return tuple(outs)
```

```python
import functools
import math

import jax
import jax.numpy as jnp
from jax import lax
from jax.experimental import pallas as pl
from jax.experimental.pallas import tpu as pltpu
from jax.experimental.pallas import tpu_sc as plsc

S = 2048
D = 1024
H = 16
DH = 64
DFF = 2816
NDEV = 8
DEPTH = 4
FF_SHARD = DFF // NDEV
FF_SHARD_PAD = 384
DFF_PAD = FF_SHARD_PAD * NDEV
BLK = 128
BRANCH_DILATIONS = (1, 4, 16)
NUM_BUCKETS = 32
MAX_DISTANCE = 2048
EPS = 1e-6
NEG_INF = -1e30
SCALE = DH ** -0.5

ADAM_LR = 0.001
ADAM_B1 = 0.9
ADAM_B2 = 0.999
ADAM_EPS = 1e-08
ADAM_WD = 0.01
ADAM_STEP = 10

BF = jnp.bfloat16
F32 = jnp.float32
VMEM_LIMIT_BYTES = 56 * 1024 * 1024
MESH = pl.DeviceIdType.MESH
ANY = pl.BlockSpec(memory_space=pl.ANY)

_NT = (((1,), (1,)), ((), ()))
_TN = (((0,), (0,)), ((), ()))


def _cparams(sem=None):
    return pltpu.CompilerParams(dimension_semantics=sem, vmem_limit_bytes=VMEM_LIMIT_BYTES)


def _after(body, n, deps):
    nd = len(deps)
    if nd == 0:
        return body

    def ordered(*refs):
        body(*refs[:n], *refs[n + nd:])
    return ordered


def _rms(x):
    return lax.rsqrt(jnp.mean(x * x, axis=-1, keepdims=True) + EPS)


def norm_matmul3(x, gain, w, name, tm=512, tn=512):
    per = D // tn

    def body(x_ref, g_ref, w_ref, z_ref, h_ref, hs_ref):
        @pl.when(pl.program_id(1) == 0)
        def _():
            xv = x_ref[...]
            hv = (xv * _rms(xv) * g_ref[...]).astype(BF)
            hs_ref[...] = hv
            h_ref[...] = hv
        z_ref[...] = jnp.dot(hs_ref[...], w_ref[...], preferred_element_type=F32).astype(BF)

    return pl.pallas_call(
        body, name=name,
        grid=(S // tm, 3 * D // tn),
        in_specs=[pl.BlockSpec((tm, D), lambda i, j: (i, 0)),
                  pl.BlockSpec((1, D), lambda i, j: (0, 0)),
                  pl.BlockSpec((D, tn), lambda i, j: (0, j))],
        out_specs=[pl.BlockSpec((None, tm, tn), lambda i, j: (j // per, i, j % per)),
                   pl.BlockSpec((tm, D), lambda i, j: (i, 0))],
        out_shape=[jax.ShapeDtypeStruct((3, S, D), BF), jax.ShapeDtypeStruct((S, D), BF)],
        scratch_shapes=[pltpu.VMEM((tm, D), BF)],
        compiler_params=_cparams(("parallel", "arbitrary")),
    )(x, gain, w)


def norm_swiglu_up(x, gain, wg, wu, name, tm=512, tn=512):
    def body(x_ref, g_ref, wg_ref, wu_ref, go_ref, uo_ref, ao_ref, h_ref, hs_ref):
        @pl.when(pl.program_id(1) == 0)
        def _():
            xv = x_ref[...]
            hv = (xv * _rms(xv) * g_ref[...]).astype(BF)
            hs_ref[...] = hv
            h_ref[...] = hv
        hv = hs_ref[...]
        g = jnp.dot(hv, wg_ref[...], preferred_element_type=F32)
        u = jnp.dot(hv, wu_ref[...], preferred_element_type=F32)
        go_ref[...] = g.astype(BF)
        uo_ref[...] = u.astype(BF)
        ao_ref[...] = (g * jax.nn.sigmoid(g) * u).astype(BF)

    act = jax.ShapeDtypeStruct((S, DFF_PAD), BF)
    blk = pl.BlockSpec((tm, tn), lambda i, j: (i, j))
    return pl.pallas_call(
        body, name=name,
        grid=(S // tm, DFF_PAD // tn),
        in_specs=[pl.BlockSpec((tm, D), lambda i, j: (i, 0)),
                  pl.BlockSpec((1, D), lambda i, j: (0, 0)),
                  pl.BlockSpec((D, tn), lambda i, j: (0, j)),
                  pl.BlockSpec((D, tn), lambda i, j: (0, j))],
        out_specs=[blk, blk, blk, pl.BlockSpec((tm, D), lambda i, j: (i, 0))],
        out_shape=[act, act, act, jax.ShapeDtypeStruct((S, D), BF)],
        scratch_shapes=[pltpu.VMEM((tm, D), BF)],
        compiler_params=_cparams(("parallel", "arbitrary")),
    )(x, gain, wg, wu)


def matmul_residual(a, w, x, name, tm=512, tn=512):
    K = a.shape[1]

    def body(a_ref, w_ref, x_ref, o_ref):
        o_ref[...] = x_ref[...] + jnp.dot(a_ref[...], w_ref[...], preferred_element_type=F32)

    return pl.pallas_call(
        body, name=name,
        grid=(S // tm, D // tn),
        in_specs=[pl.BlockSpec((tm, K), lambda i, j: (i, 0)),
                  pl.BlockSpec((K, tn), lambda i, j: (0, j)),
                  pl.BlockSpec((tm, tn), lambda i, j: (i, j))],
        out_specs=pl.BlockSpec((tm, tn), lambda i, j: (i, j)),
        out_shape=jax.ShapeDtypeStruct((S, D), F32),
        compiler_params=_cparams(("parallel", "parallel")),
    )(a, w, x)


def matmul_nt(a, w, name, out_dtype=BF, tm=512, tn=512, deps=()):
    K = a.shape[1]
    N = w.shape[0]

    def body(a_ref, w_ref, o_ref):
        o_ref[...] = lax.dot_general(a_ref[...], w_ref[...], _NT,
                                     preferred_element_type=F32).astype(o_ref.dtype)

    return pl.pallas_call(
        _after(body, 2, deps), name=name,
        grid=(S // tm, N // tn),
        in_specs=[pl.BlockSpec((tm, K), lambda i, j: (i, 0)),
                  pl.BlockSpec((tn, K), lambda i, j: (j, 0))] + [ANY] * len(deps),
        out_specs=pl.BlockSpec((tm, tn), lambda i, j: (i, j)),
        out_shape=jax.ShapeDtypeStruct((S, N), out_dtype),
        compiler_params=_cparams(("parallel", "parallel")),
    )(a, w, *deps)


def swiglu_bwd_da(dxb, wd, g, u, name, tm=512, tn=512, deps=()):
    def body(dx_ref, w_ref, g_ref, u_ref, dg_ref, du_ref):
        da = lax.dot_general(dx_ref[...], w_ref[...], _NT, preferred_element_type=F32)
        gv = g_ref[...].astype(F32)
        uv = u_ref[...].astype(F32)
        sig = jax.nn.sigmoid(gv)
        dg_ref[...] = (da * uv * (sig * (1.0 + gv * (1.0 - sig)))).astype(BF)
        du_ref[...] = (da * (gv * sig)).astype(BF)

    act = jax.ShapeDtypeStruct((S, DFF_PAD), BF)
    blk = pl.BlockSpec((tm, tn), lambda i, j: (i, j))
    return pl.pallas_call(
        _after(body, 4, deps), name=name,
        grid=(S // tm, DFF_PAD // tn),
        in_specs=[pl.BlockSpec((tm, D), lambda i, j: (i, 0)),
                  pl.BlockSpec((tn, D), lambda i, j: (j, 0)),
                  blk, blk] + [ANY] * len(deps),
        out_specs=[blk, blk],
        out_shape=[act, act],
        compiler_params=_cparams(("parallel", "parallel")),
    )(dxb, wd, g, u, *deps)


def matmul_tn(a, b, name, tm=512, tn=512, deps=()):
    M = a.shape[1]
    if b.ndim == 3:
        per = D // tn
        N = 3 * D
        b_spec = pl.BlockSpec((None, S, tn), lambda i, j: (j // per, 0, j % per))
    else:
        N = b.shape[1]
        b_spec = pl.BlockSpec((S, tn), lambda i, j: (0, j))

    def body(a_ref, b_ref, o_ref):
        o_ref[...] = lax.dot_general(a_ref[...], b_ref[...], _TN,
                                     preferred_element_type=F32).astype(BF)

    return pl.pallas_call(
        _after(body, 2, deps), name=name,
        grid=(M // tm, N // tn),
        in_specs=[pl.BlockSpec((S, tm), lambda i, j: (0, i)), b_spec] + [ANY] * len(deps),
        out_specs=pl.BlockSpec((tm, tn), lambda i, j: (i, j)),
        out_shape=jax.ShapeDtypeStruct((M, N), BF),
        compiler_params=_cparams(("parallel", "parallel")),
    )(a, b, *deps)


def matmul_nt_normbwd(terms, x_in, gain, dx, name, tk, tm=512, deps=()):
    specs, operands, ranges = [], [], []
    start = 0
    for (a, w, stacked) in terms:
        K = w.shape[1]
        n = K // tk
        lo = start

        def rel(k, lo=lo, n=n):
            return jnp.clip(k - lo, 0, n - 1)

        if stacked:
            per = D // tk
            specs.append(pl.BlockSpec((None, tm, tk),
                                      lambda i, k, rel=rel, per=per: (rel(k) // per, i, rel(k) % per)))
        else:
            specs.append(pl.BlockSpec((tm, tk), lambda i, k, rel=rel: (i, rel(k))))
        specs.append(pl.BlockSpec((D, tk), lambda i, k, rel=rel: (0, rel(k))))
        operands += [a, w]
        ranges.append((lo, lo + n))
        start += n
    nk = start
    nt = len(terms)

    def body(*refs):
        aw = refs[:2 * nt]
        x_ref, g_ref, dx_ref, dxo_ref, dxb_ref, dg_ref, acc_ref = refs[2 * nt:]
        i = pl.program_id(0)
        k = pl.program_id(1)

        @pl.when(k == 0)
        def _():
            acc_ref[...] = jnp.zeros_like(acc_ref)

        @pl.when((i == 0) & (k == 0))
        def _():
            dg_ref[...] = jnp.zeros_like(dg_ref)

        for t in range(nt):
            lo, hi = ranges[t]

            @pl.when((k >= lo) & (k < hi))
            def _(t=t):
                acc_ref[...] += lax.dot_general(aw[2 * t][...], aw[2 * t + 1][...], _NT,
                                                preferred_element_type=F32)

        @pl.when(k == nk - 1)
        def _():
            xv = x_ref[...]
            r = _rms(xv)
            xhat = xv * r
            dh = acc_ref[...]
            dg_ref[0:1, :] += jnp.sum(dh * xhat, axis=0, keepdims=True)
            dxh = dh * g_ref[...]
            dxn = r * (dxh - xhat * jnp.mean(dxh * xhat, axis=-1, keepdims=True))
            out = dx_ref[...] + dxn
            dxo_ref[...] = out
            dxb_ref[...] = out.astype(BF)

    row = pl.BlockSpec((tm, D), lambda i, k: (i, 0))
    return pl.pallas_call(
        _after(body, 2 * nt + 3, deps), name=name,
        grid=(S // tm, nk),
        in_specs=specs + [row, pl.BlockSpec((1, D), lambda i, k: (0, 0)), row] + [ANY] * len(deps),
        out_specs=[row, row, pl.BlockSpec((8, D), lambda i, k: (0, 0))],
        out_shape=[jax.ShapeDtypeStruct((S, D), F32), jax.ShapeDtypeStruct((S, D), BF),
                   jax.ShapeDtypeStruct((8, D), F32)],
        scratch_shapes=[pltpu.VMEM((tm, D), F32)],
        compiler_params=_cparams(("arbitrary", "arbitrary")),
    )(*operands, x_in, gain, dx, *deps)


def loss_head(x, gain, target, name, tm=512):
    def body(x_ref, g_ref, t_ref, dxo_ref, dxb_ref, dg_ref, sq_ref):
        @pl.when(pl.program_id(0) == 0)
        def _():
            dg_ref[...] = jnp.zeros_like(dg_ref)
            sq_ref[...] = jnp.zeros_like(sq_ref)
        xv = x_ref[...]
        r = _rms(xv)
        xhat = xv * r
        err = xhat * g_ref[...] - t_ref[...]
        sq_ref[0:1, :] += jnp.sum(err * err, axis=0, keepdims=True)
        dy = err * (1.0 / D)
        dg_ref[0:1, :] += jnp.sum(dy * xhat, axis=0, keepdims=True)
        dxh = dy * g_ref[...]
        out = r * (dxh - xhat * jnp.mean(dxh * xhat, axis=-1, keepdims=True))
        dxo_ref[...] = out
        dxb_ref[...] = out.astype(BF)

    row = pl.BlockSpec((tm, D), lambda i: (i, 0))
    acc = pl.BlockSpec((8, D), lambda i: (0, 0))
    return pl.pallas_call(
        body, name=name,
        grid=(S // tm,),
        in_specs=[row, pl.BlockSpec((1, D), lambda i: (0, 0)), row],
        out_specs=[row, row, acc, acc],
        out_shape=[jax.ShapeDtypeStruct((S, D), F32), jax.ShapeDtypeStruct((S, D), BF),
                   jax.ShapeDtypeStruct((8, D), F32), jax.ShapeDtypeStruct((8, D), F32)],
        compiler_params=_cparams(("arbitrary",)),
    )(x, gain, target)


def _shift_down(p, n, row):
    return jnp.where(row >= n, pltpu.roll(p, n, axis=0), 0.0)


def _shift_up(p, n, row):
    return jnp.where(row < S - n, pltpu.roll(p, S - n, axis=0), 0.0)


def conv_fwd(z3, taps, name, tn=128):
    def body(z_ref, k_ref, m_ref):
        b = z_ref[0].astype(F32)
        p = z_ref[1].astype(F32) * z_ref[2].astype(F32)
        row = lax.broadcasted_iota(jnp.int32, p.shape, 0)
        y = (k_ref[2:3, :] * p + k_ref[1:2, :] * _shift_down(p, 1, row)
             + k_ref[0:1, :] * _shift_down(p, 2, row))
        m_ref[...] = (b * y).astype(BF)

    return pl.pallas_call(
        body, name=name,
        grid=(D // tn,),
        in_specs=[pl.BlockSpec((3, S, tn), lambda j: (0, 0, j)),
                  pl.BlockSpec((8, tn), lambda j: (0, j))],
        out_specs=pl.BlockSpec((S, tn), lambda j: (0, j)),
        out_shape=jax.ShapeDtypeStruct((S, D), BF),
        compiler_params=_cparams(("parallel",)),
    )(z3, taps)


def conv_bwd(dm, z3, taps, name, tn=128, deps=()):
    def body(dm_ref, z_ref, k_ref, dz_ref, dk_ref):
        dmv = dm_ref[...]
        b = z_ref[0].astype(F32)
        c = z_ref[1].astype(F32)
        u = z_ref[2].astype(F32)
        p = c * u
        row = lax.broadcasted_iota(jnp.int32, p.shape, 0)
        p1 = _shift_down(p, 1, row)
        p2 = _shift_down(p, 2, row)
        y = k_ref[2:3, :] * p + k_ref[1:2, :] * p1 + k_ref[0:1, :] * p2
        dy = dmv * b
        dz_ref[0] = (dmv * y).astype(BF)
        dp = (k_ref[2:3, :] * dy + k_ref[1:2, :] * _shift_up(dy, 1, row)
              + k_ref[0:1, :] * _shift_up(dy, 2, row))
        dz_ref[1] = (dp * u).astype(BF)
        dz_ref[2] = (dp * c).astype(BF)
        dk_ref[...] = jnp.zeros_like(dk_ref)
        dk_ref[0:1, :] = jnp.sum(dy * p2, axis=0, keepdims=True)
        dk_ref[1:2, :] = jnp.sum(dy * p1, axis=0, keepdims=True)
        dk_ref[2:3, :] = jnp.sum(dy * p, axis=0, keepdims=True)

    return pl.pallas_call(
        _after(body, 3, deps), name=name,
        grid=(D // tn,),
        in_specs=[pl.BlockSpec((S, tn), lambda j: (0, j)),
                  pl.BlockSpec((3, S, tn), lambda j: (0, 0, j)),
                  pl.BlockSpec((8, tn), lambda j: (0, j))] + [ANY] * len(deps),
        out_specs=[pl.BlockSpec((3, S, tn), lambda j: (0, 0, j)),
                   pl.BlockSpec((8, tn), lambda j: (0, j))],
        out_shape=[jax.ShapeDtypeStruct((3, S, D), BF), jax.ShapeDtypeStruct((8, D), F32)],
        compiler_params=_cparams(("parallel",)),
    )(dm, z3, taps, *deps)


def _t5_bucket(dist):
    exact = NUM_BUCKETS // 2
    df = jnp.maximum(dist, 1).astype(jnp.float32)
    large = exact + (jnp.log(df / exact) / math.log(MAX_DISTANCE / exact)
                     * (NUM_BUCKETS - exact)).astype(jnp.int32)
    large = jnp.minimum(large, NUM_BUCKETS - 1)
    return jnp.where(dist < exact, dist, large)


def _bucket_onehot_t():
    qi = jnp.arange(BLK)[:, None]
    ki = jnp.arange(2 * BLK)[None, :]
    rel = qi + BLK - ki
    band = ((rel >= 0) & (rel <= BLK)).reshape(1, -1).astype(F32)
    hots = []
    for d in BRANCH_DILATIONS:
        bucket = _t5_bucket(jnp.clip(rel, 0) * d).reshape(1, -1)
        hots.append((jnp.arange(NUM_BUCKETS)[:, None] == bucket).astype(F32))
    return jnp.stack(hots), band


def bias_tables(rel_bias_t, onehot_t, band, name):
    def body(rb_ref, oh_ref, band_ref, o_ref):
        b = jnp.dot(rb_ref[...], oh_ref[...], preferred_element_type=F32,
                    precision=lax.Precision.HIGHEST)
        o_ref[...] = jnp.where(band_ref[...] > 0.5, b, NEG_INF)

    n = BLK * 2 * BLK
    return pl.pallas_call(
        body, name=name,
        grid=(3,),
        in_specs=[pl.BlockSpec((H, NUM_BUCKETS), lambda g: (0, 0)),
                  pl.BlockSpec((None, NUM_BUCKETS, n), lambda g: (g, 0, 0)),
                  pl.BlockSpec((1, n), lambda g: (0, 0))],
        out_specs=pl.BlockSpec((None, H, n), lambda g: (g, 0, 0)),
        out_shape=jax.ShapeDtypeStruct((3, H, n), F32),
        compiler_params=_cparams(("parallel",)),
    )(rel_bias_t, onehot_t, band)


def bias_grad(dbias, onehot_t, name):
    def body(db_ref, oh_ref, o_ref):
        @pl.when(pl.program_id(0) == 0)
        def _():
            o_ref[...] = jnp.zeros_like(o_ref)
        o_ref[...] += lax.dot_general(db_ref[...], oh_ref[...], _NT, preferred_element_type=F32,
                                      precision=lax.Precision.HIGHEST)

    n = BLK * 2 * BLK
    return pl.pallas_call(
        body, name=name,
        grid=(dbias.shape[0],),
        in_specs=[pl.BlockSpec((None, H, n), lambda g: (g, 0, 0)),
                  pl.BlockSpec((None, NUM_BUCKETS, n), lambda g: (g % 3, 0, 0))],
        out_specs=pl.BlockSpec((H, NUM_BUCKETS), lambda g: (0, 0)),
        out_shape=jax.ShapeDtypeStruct((H, NUM_BUCKETS), F32),
        compiler_params=_cparams(("arbitrary",)),
    )(dbias, onehot_t)


def _head_masks():
    lane = lax.broadcasted_iota(jnp.int32, (1, 2 * DH), 1)
    return (lane < DH, lane >= DH)


def attn_branch_fwd(z3d, bias, d, name):
    L = S // d
    nb = L // BLK

    def body(q_ref, k_ref, v_ref, b_ref, o_ref, lse_ref):
        masks = _head_masks()
        lane2 = lax.broadcasted_iota(jnp.int32, (BLK, 2), 1)

        def block(n, first):
            if first:
                q0 = 0
                kk = k_ref[0:BLK, :]
                vv = v_ref[0:BLK, :]
            else:
                q0 = pl.multiple_of(n * BLK, BLK)
                k0 = pl.multiple_of((n - 1) * BLK, BLK)
                kk = k_ref[pl.ds(k0, 2 * BLK), :]
                vv = v_ref[pl.ds(k0, 2 * BLK), :]
            q = q_ref[pl.ds(q0, BLK), :]
            outs, lses = [], []
            for hh in range(2):
                qh = jnp.where(masks[hh], q, jnp.zeros_like(q))
                bias_h = b_ref[hh][:, BLK:] if first else b_ref[hh]
                s = lax.dot_general(qh, kk, _NT, preferred_element_type=F32) * SCALE + bias_h
                mx = jnp.max(s, axis=1, keepdims=True)
                p = jnp.exp(s - mx)
                l = jnp.sum(p, axis=1, keepdims=True)
                outs.append(jnp.dot(p.astype(BF), vv, preferred_element_type=F32) / l)
                lses.append(mx + jnp.log(l))
            o_ref[pl.ds(q0, BLK), :] = jnp.where(masks[0], outs[0], outs[1])
            lse_ref[pl.ds(q0, BLK), :] = jnp.where(lane2 == 0, lses[0], lses[1])

        block(0, True)
        if nb > 1:
            def loop(n, carry):
                block(n, False)
                return carry
            lax.fori_loop(1, nb, loop, 0)

    col = lambda hp, r: (0, r * 8 + hp)
    return pl.pallas_call(
        body, name=name,
        grid=(8, d),
        in_specs=[pl.BlockSpec((None, L, 2 * DH), lambda hp, r: (0, 0, r * 8 + hp)),
                  pl.BlockSpec((None, L, 2 * DH), lambda hp, r: (1, 0, r * 8 + hp)),
                  pl.BlockSpec((None, L, 2 * DH), lambda hp, r: (2, 0, r * 8 + hp)),
                  pl.BlockSpec((2, BLK, 2 * BLK), lambda hp, r: (hp, 0, 0))],
        out_specs=[pl.BlockSpec((L, 2 * DH), col),
                   pl.BlockSpec((None, L, 2), lambda hp, r: (r * 8 + hp, 0, 0))],
        out_shape=[jax.ShapeDtypeStruct((L, d * D), F32),
                   jax.ShapeDtypeStruct((d * 8, L, 2), F32)],
        compiler_params=_cparams(("parallel", "parallel")),
    )(z3d, z3d, z3d, bias)


def attn_branch_bwd(z3d, dod, od, lse, bias, d, name, deps=()):
    L = S // d
    nb = L // BLK

    def body(q_ref, k_ref, v_ref, do_ref, o_ref, lse_ref, b_ref, dz_ref, db_ref):
        masks = _head_masks()

        @pl.when(pl.program_id(1) == 0)
        def _():
            db_ref[...] = jnp.zeros_like(db_ref)

        dz_ref[1] = jnp.zeros((L, 2 * DH), F32)
        dz_ref[2] = jnp.zeros((L, 2 * DH), F32)

        def block(n, first):
            if first:
                q0 = 0
                k0 = 0
                nk = BLK
            else:
                q0 = pl.multiple_of(n * BLK, BLK)
                k0 = pl.multiple_of((n - 1) * BLK, BLK)
                nk = 2 * BLK
            kk = k_ref[pl.ds(k0, nk), :]
            vv = v_ref[pl.ds(k0, nk), :]
            q = q_ref[pl.ds(q0, BLK), :]
            do = do_ref[pl.ds(q0, BLK), :]
            of = o_ref[pl.ds(q0, BLK), :].astype(F32)
            lse_blk = lse_ref[pl.ds(q0, BLK), :]
            dqs = []
            dk = jnp.zeros((nk, 2 * DH), F32)
            dv = jnp.zeros((nk, 2 * DH), F32)
            for hh in range(2):
                qh = jnp.where(masks[hh], q, jnp.zeros_like(q))
                doh = jnp.where(masks[hh], do, jnp.zeros_like(do))
                bias_h = b_ref[hh][:, BLK:] if first else b_ref[hh]
                s = lax.dot_general(qh, kk, _NT, preferred_element_type=F32) * SCALE + bias_h
                p = jnp.exp(s - lse_blk[:, hh:hh + 1])
                dp = lax.dot_general(doh, vv, _NT, preferred_element_type=F32)
                delta = jnp.sum(doh.astype(F32) * of, axis=1, keepdims=True)
                ds = p * (dp - delta)
                if first:
                    db_ref[hh, :, BLK:] += ds
                else:
                    db_ref[hh] += ds
                dsb = ds.astype(BF)
                dqs.append(jnp.dot(dsb, kk, preferred_element_type=F32) * SCALE)
                dk += lax.dot_general(dsb, qh, _TN, preferred_element_type=F32) * SCALE
                dv += lax.dot_general(p.astype(BF), doh, _TN, preferred_element_type=F32)
            dz_ref[0, pl.ds(q0, BLK), :] = jnp.where(masks[0], dqs[0], dqs[1])
            dz_ref[1, pl.ds(k0, nk), :] += dk
            dz_ref[2, pl.ds(k0, nk), :] += dv

        block(0, True)
        if nb > 1:
            def loop(n, carry):
                block(n, False)
                return carry
            lax.fori_loop(1, nb, loop, 0)

    col = lambda hp, r: (0, r * 8 + hp)
    return pl.pallas_call(
        _after(body, 7, deps), name=name,
        grid=(8, d),
        in_specs=[pl.BlockSpec((None, L, 2 * DH), lambda hp, r: (0, 0, r * 8 + hp)),
                  pl.BlockSpec((None, L, 2 * DH), lambda hp, r: (1, 0, r * 8 + hp)),
                  pl.BlockSpec((None, L, 2 * DH), lambda hp, r: (2, 0, r * 8 + hp)),
                  pl.BlockSpec((L, 2 * DH), col),
                  pl.BlockSpec((L, 2 * DH), col),
                  pl.BlockSpec((None, L, 2), lambda hp, r: (r * 8 + hp, 0, 0)),
                  pl.BlockSpec((2, BLK, 2 * BLK), lambda hp, r: (hp, 0, 0))] + [ANY] * len(deps),
        out_specs=[pl.BlockSpec((3, L, 2 * DH), lambda hp, r: (0, 0, r * 8 + hp)),
                   pl.BlockSpec((2, BLK, 2 * BLK), lambda hp, r: (hp, 0, 0))],
        out_shape=[jax.ShapeDtypeStruct((3, L, d * D), F32),
                   jax.ShapeDtypeStruct((H, BLK, 2 * BLK), F32)],
        compiler_params=_cparams(("parallel", "arbitrary")),
    )(z3d, z3d, z3d, dod, od, lse, bias, *deps)


def attn_combine(o_parts, lse_parts, name, tm=512):
    def body(o1, o2, o3, l1, l2, l3, o_ref, lse_ref):
        a, b, c = l1[...], l2[...], l3[...]
        mx = jnp.maximum(jnp.maximum(a, b), c)
        ea, eb, ec = jnp.exp(a - mx), jnp.exp(b - mx), jnp.exp(c - mx)
        tot = ea + eb + ec
        o_ref[...] = ((ea * o1[...] + eb * o2[...] + ec * o3[...]) / tot).astype(BF)
        lse_ref[...] = mx + jnp.log(tot)

    row = pl.BlockSpec((tm, D), lambda i: (i, 0))
    return pl.pallas_call(
        body, name=name,
        grid=(S // tm,),
        in_specs=[row] * 6,
        out_specs=[row, row],
        out_shape=[jax.ShapeDtypeStruct((S, D), BF), jax.ShapeDtypeStruct((S, D), F32)],
        compiler_params=_cparams(("parallel",)),
    )(*o_parts, *lse_parts)


def sum3_cast(a, b, c, name, tm=512):
    def body(a_ref, b_ref, c_ref, o_ref):
        o_ref[...] = (a_ref[...] + b_ref[...] + c_ref[...]).astype(BF)

    blk = pl.BlockSpec((None, tm, D), lambda k, i: (k, i, 0))
    return pl.pallas_call(
        body, name=name,
        grid=(3, S // tm),
        in_specs=[blk] * 3,
        out_specs=blk,
        out_shape=jax.ShapeDtypeStruct((3, S, D), BF),
        compiler_params=_cparams(("parallel", "parallel")),
    )(a, b, c)


def _lse_compact(lse_b, d):
    L = S // d
    v = lse_b[:, ::DH].reshape(L, d, 8, 2)
    return v.transpose(1, 2, 0, 3).reshape(d * 8, L, 2)


def _lse_broadcast(lse_c, d):
    L = S // d
    v = lse_c.reshape(d, 8, L, 2).transpose(2, 0, 1, 3).reshape(S, H)
    return jnp.repeat(v, DH, axis=1)


def attention_fwd(z3, bias3):
    o_parts, lse_parts = [], []
    for g, d in enumerate(BRANCH_DILATIONS):
        L = S // d
        o_d, lse_c = attn_branch_fwd(z3.reshape(3, L, d * D), bias3[g], d, f"attn_fwd_d{d}")
        o_parts.append(o_d.reshape(S, D))
        lse_parts.append(_lse_broadcast(lse_c, d))
    return attn_combine(o_parts, lse_parts, "attn_combine")


def attention_bwd(z3, dob, ob, lse_b, bias3, deps=()):
    dzs, dbs = [], []
    for g, d in enumerate(BRANCH_DILATIONS):
        L = S // d
        dz_d, db = attn_branch_bwd(z3.reshape(3, L, d * D), dob.reshape(L, d * D),
                                   ob.reshape(L, d * D), _lse_compact(lse_b, d), bias3[g], d,
                                   f"attn_bwd_d{d}", deps=deps)
        dzs.append(dz_d.reshape(3, S, D))
        dbs.append(db.reshape(H, BLK * 2 * BLK))
    return sum3_cast(*dzs, "attn_dz_sum"), jnp.stack(dbs)


def _me():
    return lax.axis_index("x"), lax.axis_index("y"), lax.axis_index("c")


def _other_chips(x, y):
    return [(1 - x, y), (x, 1 - y), (1 - x, 1 - y)]


def _shard_window(ref, axis, t, shape):
    R, C = shape
    if axis == 0:
        return ref.at[pl.ds(pl.multiple_of(t * R, 128), R), :]
    return ref.at[:, pl.ds(pl.multiple_of(t * C, 128), C)]


def all_gather_weights(shards, axes, name):
    n = len(shards)
    shapes = [s.shape for s in shards]
    outs_shape = [jax.ShapeDtypeStruct((8 * R, C) if ax == 0 else (R, 8 * C), BF)
                  for (R, C), ax in zip(shapes, axes)]

    def body(*refs):
        ins, outs = refs[:n], refs[n:2 * n]
        send_sems, recv_sems, local_sems = refs[2 * n:]
        x, y, c = _me()
        sibling = (x, y, 1 - c)
        chips = _other_chips(x, y)
        barrier = pltpu.get_barrier_semaphore()
        for peer in [sibling] + [(*chip, c) for chip in chips]:
            pl.semaphore_signal(barrier, inc=1, device_id=peer, device_id_type=MESH)
        pl.semaphore_wait(barrier, 4)

        def win(i, px, py, pc):
            return _shard_window(outs[i], axes[i], 4 * px + 2 * py + pc, shapes[i])

        def copy(i, k, block, to, src=None):
            return pltpu.make_async_remote_copy(
                src_ref=win(i, *block) if src is None else src, dst_ref=win(i, *block),
                send_sem=send_sems.at[i * 7 + k], recv_sem=recv_sems.at[i * 7 + k],
                device_id=to, device_id_type=MESH)

        mine = [pltpu.make_async_copy(ins[i], win(i, x, y, c), local_sems.at[i]) for i in range(n)]
        for cp in mine:
            cp.start()
        first = []
        for i in range(n):
            first.append(copy(i, 0, (x, y, c), sibling, src=ins[i]))
            for j, chip in enumerate(chips):
                first.append(copy(i, 1 + j, (x, y, c), (*chip, c), src=ins[i]))
        for cp in first:
            cp.start()
        passed = []
        for j, chip in enumerate(chips):
            for i in range(n):
                copy(i, 1 + j, (*chip, c), (x, y, c)).wait_recv()
                cp = copy(i, 4 + j, (*chip, c), sibling)
                cp.start()
                passed.append(cp)
        for i in range(n):
            copy(i, 0, sibling, (x, y, c)).wait_recv()
        for j, chip in enumerate(chips):
            for i in range(n):
                copy(i, 4 + j, (*chip, 1 - c), (x, y, c)).wait_recv()
        for cp in first + passed:
            cp.wait_send()
        for cp in mine:
            cp.wait()

    return pl.kernel(
        body, out_type=outs_shape, name=name,
        mesh=plsc.ScalarSubcoreMesh(axis_name="sequencer", num_cores=1),
        scratch_types=[pltpu.SemaphoreType.DMA((7 * n,)), pltpu.SemaphoreType.DMA((7 * n,)),
                       pltpu.SemaphoreType.DMA((n,))],
        compiler_params=pltpu.CompilerParams(collective_id=1),
    )(*shards)


def pair_exchange_grads(grads, axes, shapes, name):
    n = len(grads)

    def body(*refs):
        ins, outs = refs[:n], refs[n:2 * n]
        send_sems, recv_sems = refs[2 * n:]
        x, y, c = _me()
        sibling = (x, y, 1 - c)
        barrier = pltpu.get_barrier_semaphore()
        pl.semaphore_signal(barrier, inc=1, device_id=sibling, device_id_type=MESH)
        pl.semaphore_wait(barrier, 1)
        copies = []
        for i in range(n):
            for q in range(4):
                t = 2 * q + (1 - c)
                copies.append(pltpu.make_async_remote_copy(
                    src_ref=_shard_window(ins[i], axes[i], t, shapes[i]), dst_ref=outs[i].at[q],
                    send_sem=send_sems.at[i * 4 + q], recv_sem=recv_sems.at[i * 4 + q],
                    device_id=sibling, device_id_type=MESH))
        for cp in copies:
            cp.start()
        for cp in copies:
            cp.wait_recv()
        for cp in copies:
            cp.wait_send()

    return pl.kernel(
        body, out_type=[jax.ShapeDtypeStruct((4,) + tuple(sh), BF) for sh in shapes], name=name,
        mesh=plsc.ScalarSubcoreMesh(axis_name="sequencer", num_cores=1),
        scratch_types=[pltpu.SemaphoreType.DMA((4 * n,)), pltpu.SemaphoreType.DMA((4 * n,))],
        compiler_params=pltpu.CompilerParams(collective_id=2),
    )(*grads)


def pair_add(grad, landed, axis, shape, c_idx, name, deps=()):
    R, C = shape

    def body(c_ref, g_ref, l_ref, o_ref):
        o_ref[...] = (g_ref[...].astype(F32) + l_ref[...].astype(F32)).astype(BF)

    if axis == 0:
        g_spec = pl.BlockSpec((R, C), lambda q, c_ref: (2 * q + c_ref[0], 0))
    else:
        g_spec = pl.BlockSpec((R, C), lambda q, c_ref: (0, 2 * q + c_ref[0]))
    blk = pl.BlockSpec((None, R, C), lambda q, c_ref: (q, 0, 0))
    return pl.pallas_call(
        _after(body, 3, deps), name=name,
        grid_spec=pltpu.PrefetchScalarGridSpec(
            num_scalar_prefetch=1, grid=(4,), in_specs=[g_spec, blk] + [ANY] * len(deps),
            out_specs=blk),
        out_shape=jax.ShapeDtypeStruct((4, R, C), BF),
        compiler_params=_cparams(("parallel",)),
    )(c_idx, grad, landed, *deps)


def chip_exchange_grads(parts, name):
    n = len(parts)

    def body(*refs):
        ins, outs = refs[:n], refs[n:2 * n]
        send_sems, recv_sems = refs[2 * n:]
        x, y, c = _me()
        barrier = pltpu.get_barrier_semaphore()
        for px, py in _other_chips(x, y):
            pl.semaphore_signal(barrier, inc=1, device_id=(px, py, c), device_id_type=MESH)
        pl.semaphore_wait(barrier, 3)
        copies = []
        for i in range(n):
            for k, (px, py) in enumerate(_other_chips(x, y)):
                copies.append(pltpu.make_async_remote_copy(
                    src_ref=ins[i].at[2 * px + py], dst_ref=outs[i].at[k],
                    send_sem=send_sems.at[i * 3 + k], recv_sem=recv_sems.at[i * 3 + k],
                    device_id=(px, py, c), device_id_type=MESH))
        for cp in copies:
            cp.start()
        for cp in copies:
            cp.wait_recv()
        for cp in copies:
            cp.wait_send()

    return pl.kernel(
        body, out_type=[jax.ShapeDtypeStruct((3,) + tuple(p.shape[1:]), BF) for p in parts], name=name,
        mesh=plsc.ScalarSubcoreMesh(axis_name="sequencer", num_cores=1),
        scratch_types=[pltpu.SemaphoreType.DMA((3 * n,)), pltpu.SemaphoreType.DMA((3 * n,))],
        compiler_params=pltpu.CompilerParams(collective_id=3),
    )(*parts)


def all_gather_small(v, name):
    R, C = v.shape

    def body(v_ref, out_ref, send_sems, recv_sems, local_sem):
        x, y, c = _me()
        me, sibling = (x, y, c), (x, y, 1 - c)
        chips = _other_chips(x, y)

        def slot(px, py, pc):
            return out_ref.at[4 * px + 2 * py + pc]

        def copy(k, block, to, src=None):
            return pltpu.make_async_remote_copy(
                src_ref=slot(*block) if src is None else src, dst_ref=slot(*block),
                send_sem=send_sems.at[k], recv_sem=recv_sems.at[k],
                device_id=to, device_id_type=MESH)

        mine = pltpu.make_async_copy(v_ref, slot(*me), local_sem)
        mine.start()
        first = [copy(0, me, sibling, src=v_ref)]
        first += [copy(1 + j, me, (*chip, c), src=v_ref) for j, chip in enumerate(chips)]
        for cp in first:
            cp.start()
        passed = [copy(4 + j, (*chip, c), sibling) for j, chip in enumerate(chips)]
        for j, chip in enumerate(chips):
            copy(1 + j, (*chip, c), me).wait_recv()
            passed[j].start()
        copy(0, sibling, me).wait_recv()
        for j, chip in enumerate(chips):
            copy(4 + j, (*chip, 1 - c), me).wait_recv()
        for cp in first + passed:
            cp.wait_send()
        mine.wait()

    return pl.pallas_call(
        body, name=name,
        in_specs=[pl.BlockSpec(memory_space=pltpu.VMEM)],
        out_specs=pl.BlockSpec(memory_space=pltpu.VMEM),
        out_shape=jax.ShapeDtypeStruct((NDEV, R, C), F32),
        scratch_shapes=[pltpu.SemaphoreType.DMA((7,)), pltpu.SemaphoreType.DMA((7,)),
                        pltpu.SemaphoreType.DMA],
    )(v)


def _adamw(w, g, m, v):
    m = ADAM_B1 * m + (1.0 - ADAM_B1) * g
    v = ADAM_B2 * v + (1.0 - ADAM_B2) * (g * g)
    m_hat = m / (1.0 - ADAM_B1 ** ADAM_STEP)
    v_hat = v / (1.0 - ADAM_B2 ** ADAM_STEP)
    delta = -ADAM_LR * (m_hat / (jnp.sqrt(v_hat) + ADAM_EPS) + ADAM_WD * w)
    return delta, m, v


def reduce_adamw(part, landed, w, m, v, layer, q_idx, name, prev=(), deps=()):
    R, C = part.shape[1:]
    r, c = w.shape[1:]
    tr = r // 2 if r % 16 == 0 and r >= 256 else r
    tR = tr if tr != r else R
    extra = tuple(prev) + tuple(deps)

    def body(q_ref, p_ref, l_ref, w_ref, m_ref, v_ref, g_out, d_out, m_out, v_out):
        g = p_ref[...].astype(F32)
        for k in range(3):
            g = g + l_ref[k].astype(F32)
        g = g[:tr, :c]
        d, mm, vv = _adamw(w_ref[...], g, m_ref[...], v_ref[...])
        g_out[...] = g
        d_out[...] = d
        m_out[...] = mm
        v_out[...] = vv

    wspec = pl.BlockSpec((None, tr, c), lambda i, q_ref: (layer, i, 0))
    out = jax.ShapeDtypeStruct(w.shape, F32)
    return pl.pallas_call(
        _after(body, 6, extra), name=name,
        grid_spec=pltpu.PrefetchScalarGridSpec(
            num_scalar_prefetch=1, grid=(r // tr,),
            in_specs=[pl.BlockSpec((None, tR, C), lambda i, q_ref: (q_ref[0], i, 0)),
                      pl.BlockSpec((3, tR, C), lambda i, q_ref: (0, i, 0)),
                      wspec, wspec, wspec] + [ANY] * len(extra),
            out_specs=[wspec] * 4),
        out_shape=[out] * 4,
        input_output_aliases={6 + k: k for k in range(len(prev))},
        compiler_params=_cparams(("parallel",)),
    )(q_idx, part, landed, w, m, v, *extra)


def small_reduce_adamw(gathered, w, m, v, name):
    R, C = w.shape

    def body(a_ref, w_ref, m_ref, v_ref, g_out, d_out, m_out, v_out):
        g = a_ref[0]
        for k in range(1, NDEV):
            g = g + a_ref[k]
        d, mm, vv = _adamw(w_ref[...], g, m_ref[...], v_ref[...])
        g_out[...] = g
        d_out[...] = d
        m_out[...] = mm
        v_out[...] = vv

    out = jax.ShapeDtypeStruct((R, C), F32)
    return pl.pallas_call(body, name=name, out_shape=[out] * 4,
                          compiler_params=_cparams())(gathered, w, m, v)


def _pad_cols(a, n):
    return jnp.pad(a, ((0, 0), (0, n - a.shape[1])))


def _pad_rows(a, n):
    return jnp.pad(a, ((0, n - a.shape[0]), (0, 0)))


SMALL_ROWS = 16


def _pack_small(mix, ffn, fin, taps_full, relb):
    return jnp.concatenate([
        mix, ffn, fin.reshape(1, D), taps_full.reshape(6, D),
        jnp.pad(relb.reshape(1, NUM_BUCKETS * H), ((0, 0), (0, D - NUM_BUCKETS * H)))], axis=0)


def kernel(x, mix_norm, ffn_norm, final_norm, conv_w_in, conv_kernel, conv_w_out, attn_w_qkv, attn_w_out, rel_bias, ffn_w_gate, ffn_w_up, ffn_w_down, loss_target, m_mix_norm, m_ffn_norm, m_final_norm, m_conv_w_in, m_conv_kernel, m_conv_w_out, m_attn_w_qkv, m_attn_w_out, m_rel_bias, m_ffn_w_gate, m_ffn_w_up, m_ffn_w_down, v_mix_norm, v_ffn_norm, v_final_norm, v_conv_w_in, v_conv_kernel, v_conv_w_out, v_attn_w_qkv, v_attn_w_out, v_rel_bias, v_ffn_w_gate, v_ffn_w_up, v_ffn_w_down):
    xi, yi, ci = _me()
    me = 4 * xi + 2 * yi + ci
    c_idx = jnp.reshape(ci, (1,)).astype(jnp.int32)
    q_idx = jnp.reshape(2 * xi + yi, (1,)).astype(jnp.int32)
    col0 = me * (D // NDEV)

    taps_local = jnp.zeros((2, 3, D), F32)
    taps_local = lax.dynamic_update_slice(taps_local, conv_kernel, (0, 0, col0))
    taps_pack = jnp.pad(taps_local.reshape(6, D), ((0, 2), (0, 0)))
    taps_all = all_gather_small(taps_pack, "ag_taps")
    taps_sum = jnp.sum(taps_all, axis=0)
    taps = [jnp.pad(taps_sum[3 * j:3 * j + 3], ((0, 5), (0, 0))) for j in range(2)]

    mixer_in = (conv_w_in, attn_w_qkv)
    mixer_out = (conv_w_out, attn_w_out)
    wts = []
    for i in range(DEPTH):
        j = i // 2
        shards = [mixer_in[i % 2][j].astype(BF), mixer_out[i % 2][j].astype(BF),
                  _pad_cols(ffn_w_gate[i].astype(BF), FF_SHARD_PAD),
                  _pad_cols(ffn_w_up[i].astype(BF), FF_SHARD_PAD),
                  _pad_rows(ffn_w_down[i].astype(BF), FF_SHARD_PAD)]
        wts.append(all_gather_weights(shards, (1, 0, 1, 1, 0), f"ag_weights_l{i}"))

    onehot_t, band = _bucket_onehot_t()
    bias3 = bias_tables(rel_bias.T, onehot_t, band, "bias_tables").reshape(3, H, BLK, 2 * BLK)

    saved = []
    xc = x[0]
    for i in range(DEPTH):
        w_in, w_out, w_g, w_u, w_d = wts[i]
        j = i // 2
        x_mix = xc
        z3, h_mix = norm_matmul3(xc, mix_norm[i:i + 1], w_in, f"mix_in_l{i}")
        if i % 2 == 0:
            act = conv_fwd(z3, taps[j], f"conv_fwd_l{i}")
            lse_b = None
        else:
            act, lse_b = attention_fwd(z3, bias3)
        xc = matmul_residual(act, w_out, xc, f"mix_out_l{i}")
        x_ffn = xc
        g, u, a, h_ffn = norm_swiglu_up(xc, ffn_norm[i:i + 1], w_g, w_u, f"ffn_up_l{i}")
        xc = matmul_residual(a, w_d, xc, f"ffn_down_l{i}")
        saved.append((x_mix, h_mix, z3, act, lse_b, x_ffn, h_ffn, g, u, a))

    dx, dxb, dg_final, sq = loss_head(xc, final_norm.reshape(1, D), loss_target[0], "loss_head")
    loss = lax.psum(0.5 * jnp.sum(sq[0]) / D, ("x", "y", "c"))

    w_params = ((conv_w_in, m_conv_w_in, v_conv_w_in), (attn_w_qkv, m_attn_w_qkv, v_attn_w_qkv))
    o_params = ((conv_w_out, m_conv_w_out, v_conv_w_out), (attn_w_out, m_attn_w_out, v_attn_w_out))
    dg_mix = [None] * DEPTH
    dg_ffn = [None] * DEPTH
    dtaps = [None, None]
    dbias_all = []
    shape_in, shape_out = (D, 3 * D // NDEV), (D // NDEV, D)
    shape_up, shape_down = (D, FF_SHARD_PAD), (FF_SHARD_PAD, D)
    stacked = {}

    def pair_stage(grads, landed1, axes, shapes, tag, tok):
        parts = []
        for t in range(len(grads)):
            parts.append(pair_add(grads[t], landed1[t], axes[t], shapes[t], c_idx,
                                  f"rs_add_{tag}_{t}", deps=[tok]))
            tok = parts[-1]
        return parts, chip_exchange_grads(parts, f"rs_chip_{tag}"), tok

    def adamw_stage(parts, landed2, params, tag, tok):
        for t, (pname, w_, m_, v_, layer) in enumerate(params):
            res = reduce_adamw(parts[t], landed2[t], w_, m_, v_, layer, q_idx, f"adamw_{tag}_{t}",
                               prev=stacked.get(pname, ()), deps=[tok])
            stacked[pname] = res
            tok = res[0]
        return tok

    tok = dxb
    mix_wait = None
    mix_chip = None
    ffn_chip = None
    for i in reversed(range(DEPTH)):
        w_in, w_out, w_g, w_u, w_d = wts[i]
        j = i // 2
        x_mix, h_mix, z3, act, lse_b, x_ffn, h_ffn, g, u, a = saved[i]
        ffn_params = [("ffn_w_gate", ffn_w_gate, m_ffn_w_gate, v_ffn_w_gate, i),
                      ("ffn_w_up", ffn_w_up, m_ffn_w_up, v_ffn_w_up, i),
                      ("ffn_w_down", ffn_w_down, m_ffn_w_down, v_ffn_w_down, i)]
        if i % 2 == 0:
            mix_params = [("conv_w_in", conv_w_in, m_conv_w_in, v_conv_w_in, j),
                          ("conv_w_out", conv_w_out, m_conv_w_out, v_conv_w_out, j)]
        else:
            mix_params = [("attn_w_qkv", attn_w_qkv, m_attn_w_qkv, v_attn_w_qkv, j),
                          ("attn_w_out", attn_w_out, m_attn_w_out, v_attn_w_out, j)]
        dgate, dup = swiglu_bwd_da(dxb, w_d, g, u, f"ffn_da_l{i}", deps=[tok])
        gw_d = matmul_tn(a, dxb, f"ffn_dwd_l{i}", deps=[dgate])
        tok = gw_d
        if mix_wait is not None:
            grads_m, landed1_m, params_m, tag_m = mix_wait
            parts_m, landed2_m, tok = pair_stage(grads_m, landed1_m, (1, 0), (shape_in, shape_out),
                                                 tag_m, tok)
            mix_chip = (parts_m, landed2_m, params_m, tag_m)
            mix_wait = None
        gw_g = matmul_tn(h_ffn, dgate, f"ffn_dwg_l{i}", deps=[tok])
        gw_u = matmul_tn(h_ffn, dup, f"ffn_dwu_l{i}", deps=[gw_g])
        grads_f = [gw_g, gw_u, gw_d]
        landed1_f = pair_exchange_grads(grads_f, (1, 1, 0), (shape_up, shape_up, shape_down),
                                        f"rs_pair_f{i}")
        tok = gw_u
        if ffn_chip is not None:
            tok = adamw_stage(*ffn_chip, tok)
            ffn_chip = None
        dx, dxb, dg_ffn[i] = matmul_nt_normbwd(
            [(dgate, w_g, False), (dup, w_u, False)], x_ffn, ffn_norm[i:i + 1], dx, f"ffn_dh_l{i}",
            tk=768, deps=[tok])
        gw_out = matmul_tn(act, dxb, f"mix_dwout_l{i}")
        if i % 2 == 0:
            dact = matmul_nt(dxb, w_out, f"mix_dact_l{i}", out_dtype=F32, deps=[gw_out])
        else:
            dact = matmul_nt(dxb, w_out, f"mix_dact_l{i}", out_dtype=BF, deps=[gw_out])
        parts_f, landed2_f, tok = pair_stage(grads_f, landed1_f, (1, 1, 0),
                                             (shape_up, shape_up, shape_down), f"f{i}", dact)
        ffn_chip = (parts_f, landed2_f, ffn_params, f"f{i}")
        if i % 2 == 0:
            dz3, dtaps[j] = conv_bwd(dact, z3, taps[j], f"conv_bwd_l{i}", deps=[tok])
        else:
            dz3, dbias3 = attention_bwd(z3, dact, act, lse_b, bias3, deps=[tok])
            dbias_all.append(dbias3)
        gw_in = matmul_tn(h_mix, dz3, f"mix_dwin_l{i}")
        grads_m = [gw_in, gw_out]
        landed1_m = pair_exchange_grads(grads_m, (1, 0), (shape_in, shape_out), f"rs_pair_m{i}")
        mix_wait = (grads_m, landed1_m, mix_params, f"m{i}")
        tok = gw_in
        if mix_chip is not None:
            tok = adamw_stage(*mix_chip, tok)
            mix_chip = None
        dx, dxb, dg_mix[i] = matmul_nt_normbwd(
            [(dz3, w_in, True)], x_mix, mix_norm[i:i + 1], dx, f"mix_dh_l{i}", tk=512, deps=[tok])
        tok = dxb
    grads_m, landed1_m, params_m, tag_m = mix_wait
    parts_m, landed2_m, tok = pair_stage(grads_m, landed1_m, (1, 0), (shape_in, shape_out), tag_m, tok)
    tok = adamw_stage(*ffn_chip, tok)

    grad_relb_t = bias_grad(jnp.concatenate(dbias_all), onehot_t, "bias_grad")
    dtaps_full = jnp.stack([dtaps[0][:3], dtaps[1][:3]])
    g_small = _pack_small(jnp.concatenate([d[0:1] for d in dg_mix], axis=0),
                          jnp.concatenate([d[0:1] for d in dg_ffn], axis=0),
                          dg_final[0], dtaps_full, grad_relb_t.T)
    gathered = all_gather_small(g_small, "ag_small_grads")

    def taps_at_cols(k):
        return lax.dynamic_update_slice(jnp.zeros((2, 3, D), F32), k, (0, 0, col0))

    w_small = _pack_small(mix_norm, ffn_norm, final_norm, taps_at_cols(conv_kernel), rel_bias)
    m_small = _pack_small(m_mix_norm, m_ffn_norm, m_final_norm, taps_at_cols(m_conv_kernel), m_rel_bias)
    v_small = _pack_small(v_mix_norm, v_ffn_norm, v_final_norm, taps_at_cols(v_conv_kernel), v_rel_bias)
    small = small_reduce_adamw(gathered, w_small, m_small, v_small, "adamw_small")

    def unpack_small(p):
        taps_p = lax.dynamic_slice(p[9:15].reshape(2, 3, D), (0, 0, col0), (2, 3, D // NDEV))
        return {"mix_norm": p[0:4], "ffn_norm": p[4:8], "final_norm": p[8],
                "conv_kernel": taps_p, "rel_bias": p[15, :NUM_BUCKETS * H].reshape(NUM_BUCKETS, H)}

    small_out = [unpack_small(p) for p in small]
    adamw_stage(parts_m, landed2_m, params_m, tag_m, small[0])

    names = ["mix_norm", "ffn_norm", "final_norm", "conv_w_in", "conv_kernel", "conv_w_out",
             "attn_w_qkv", "attn_w_out", "rel_bias", "ffn_w_gate", "ffn_w_up", "ffn_w_down"]
    outs = [loss, dx.reshape(1, S, D)]
    for o in range(4):
        for nme in names:
            outs.append(stacked[nme][o] if nme in stacked else small_out[o][nme])
    return tuple(outs)
```

```python
import functools
import math

import jax
import jax.numpy as jnp
from jax import lax
from jax.experimental import pallas as pl
from jax.experimental.pallas import tpu as pltpu
from jax.experimental.pallas import tpu_sc as plsc

S = 2048
D = 1024
H = 16
DH = 64
DFF = 2816
NDEV = 8
DEPTH = 4
FF_SHARD = DFF // NDEV
FF_SHARD_PAD = 384
DFF_PAD = FF_SHARD_PAD * NDEV
BLK = 128
BRANCH_DILATIONS = (1, 4, 16)
NUM_BUCKETS = 32
MAX_DISTANCE = 2048
EPS = 1e-6
NEG_INF = -1e30
SCALE = DH ** -0.5

ADAM_LR = 0.001
ADAM_B1 = 0.9
ADAM_B2 = 0.999
ADAM_EPS = 1e-08
ADAM_WD = 0.01
ADAM_STEP = 10

BF = jnp.bfloat16
F32 = jnp.float32
VMEM_LIMIT_BYTES = 56 * 1024 * 1024
MESH = pl.DeviceIdType.MESH
ANY = pl.BlockSpec(memory_space=pl.ANY)

_NT = (((1,), (1,)), ((), ()))
_TN = (((0,), (0,)), ((), ()))


def _cparams(sem=None):
    return pltpu.CompilerParams(dimension_semantics=sem, vmem_limit_bytes=VMEM_LIMIT_BYTES)


def _after(body, n, deps):
    nd = len(deps)
    if nd == 0:
        return body

    def ordered(*refs):
        body(*refs[:n], *refs[n + nd:])
    return ordered


def _rms(x):
    return lax.rsqrt(jnp.mean(x * x, axis=-1, keepdims=True) + EPS)


def norm_matmul3(x, gain, w, name, tm=512, tn=512):
    per = D // tn

    def body(x_ref, g_ref, w_ref, z_ref, h_ref, hs_ref):
        @pl.when(pl.program_id(1) == 0)
        def _():
            xv = x_ref[...]
            hv = (xv * _rms(xv) * g_ref[...]).astype(BF)
            hs_ref[...] = hv
            h_ref[...] = hv
        z_ref[...] = jnp.dot(hs_ref[...], w_ref[...], preferred_element_type=F32).astype(BF)

    return pl.pallas_call(
        body, name=name,
        grid=(S // tm, 3 * D // tn),
        in_specs=[pl.BlockSpec((tm, D), lambda i, j: (i, 0)),
                  pl.BlockSpec((1, D), lambda i, j: (0, 0)),
                  pl.BlockSpec((D, tn), lambda i, j: (0, j))],
        out_specs=[pl.BlockSpec((None, tm, tn), lambda i, j: (j // per, i, j % per)),
                   pl.BlockSpec((tm, D), lambda i, j: (i, 0))],
        out_shape=[jax.ShapeDtypeStruct((3, S, D), BF), jax.ShapeDtypeStruct((S, D), BF)],
        scratch_shapes=[pltpu.VMEM((tm, D), BF)],
        compiler_params=_cparams(("parallel", "arbitrary")),
    )(x, gain, w)


def norm_swiglu_up(x, gain, wg, wu, name, tm=512, tn=512):
    def body(x_ref, g_ref, wg_ref, wu_ref, go_ref, uo_ref, ao_ref, h_ref, hs_ref):
        @pl.when(pl.program_id(1) == 0)
        def _():
            xv = x_ref[...]
            hv = (xv * _rms(xv) * g_ref[...]).astype(BF)
            hs_ref[...] = hv
            h_ref[...] = hv
        hv = hs_ref[...]
        g = jnp.dot(hv, wg_ref[...], preferred_element_type=F32)
        u = jnp.dot(hv, wu_ref[...], preferred_element_type=F32)
        go_ref[...] = g.astype(BF)
        uo_ref[...] = u.astype(BF)
        ao_ref[...] = (g * jax.nn.sigmoid(g) * u).astype(BF)

    act = jax.ShapeDtypeStruct((S, DFF_PAD), BF)
    blk = pl.BlockSpec((tm, tn), lambda i, j: (i, j))
    return pl.pallas_call(
        body, name=name,
        grid=(S // tm, DFF_PAD // tn),
        in_specs=[pl.BlockSpec((tm, D), lambda i, j: (i, 0)),
                  pl.BlockSpec((1, D), lambda i, j: (0, 0)),
                  pl.BlockSpec((D, tn), lambda i, j: (0, j)),
                  pl.BlockSpec((D, tn), lambda i, j: (0, j))],
        out_specs=[blk, blk, blk, pl.BlockSpec((tm, D), lambda i, j: (i, 0))],
        out_shape=[act, act, act, jax.ShapeDtypeStruct((S, D), BF)],
        scratch_shapes=[pltpu.VMEM((tm, D), BF)],
        compiler_params=_cparams(("parallel", "arbitrary")),
    )(x, gain, wg, wu)


def matmul_residual(a, w, x, name, tm=512, tn=512):
    K = a.shape[1]

    def body(a_ref, w_ref, x_ref, o_ref):
        o_ref[...] = x_ref[...] + jnp.dot(a_ref[...], w_ref[...], preferred_element_type=F32)

    return pl.pallas_call(
        body, name=name,
        grid=(S // tm, D // tn),
        in_specs=[pl.BlockSpec((tm, K), lambda i, j: (i, 0)),
                  pl.BlockSpec((K, tn), lambda i, j: (0, j)),
                  pl.BlockSpec((tm, tn), lambda i, j: (i, j))],
        out_specs=pl.BlockSpec((tm, tn), lambda i, j: (i, j)),
        out_shape=jax.ShapeDtypeStruct((S, D), F32),
        compiler_params=_cparams(("parallel", "parallel")),
    )(a, w, x)


def matmul_nt(a, w, name, out_dtype=BF, tm=512, tn=512, deps=()):
    K = a.shape[1]
    N = w.shape[0]

    def body(a_ref, w_ref, o_ref):
        o_ref[...] = lax.dot_general(a_ref[...], w_ref[...], _NT,
                                     preferred_element_type=F32).astype(o_ref.dtype)

    return pl.pallas_call(
        _after(body, 2, deps), name=name,
        grid=(S // tm, N // tn),
        in_specs=[pl.BlockSpec((tm, K), lambda i, j: (i, 0)),
                  pl.BlockSpec((tn, K), lambda i, j: (j, 0))] + [ANY] * len(deps),
        out_specs=pl.BlockSpec((tm, tn), lambda i, j: (i, j)),
        out_shape=jax.ShapeDtypeStruct((S, N), out_dtype),
        compiler_params=_cparams(("parallel", "parallel")),
    )(a, w, *deps)


def swiglu_bwd_da(dxb, wd, g, u, name, tm=512, tn=512, deps=()):
    def body(dx_ref, w_ref, g_ref, u_ref, dg_ref, du_ref):
        da = lax.dot_general(dx_ref[...], w_ref[...], _NT, preferred_element_type=F32)
        gv = g_ref[...].astype(F32)
        uv = u_ref[...].astype(F32)
        sig = jax.nn.sigmoid(gv)
        dg_ref[...] = (da * uv * (sig * (1.0 + gv * (1.0 - sig)))).astype(BF)
        du_ref[...] = (da * (gv * sig)).astype(BF)

    act = jax.ShapeDtypeStruct((S, DFF_PAD), BF)
    blk = pl.BlockSpec((tm, tn), lambda i, j: (i, j))
    return pl.pallas_call(
        _after(body, 4, deps), name=name,
        grid=(S // tm, DFF_PAD // tn),
        in_specs=[pl.BlockSpec((tm, D), lambda i, j: (i, 0)),
                  pl.BlockSpec((tn, D), lambda i, j: (j, 0)),
                  blk, blk] + [ANY] * len(deps),
        out_specs=[blk, blk],
        out_shape=[act, act],
        compiler_params=_cparams(("parallel", "parallel")),
    )(dxb, wd, g, u, *deps)


def matmul_tn(a, b, name, tm=512, tn=512, deps=()):
    M = a.shape[1]
    if b.ndim == 3:
        per = D // tn
        N = 3 * D
        b_spec = pl.BlockSpec((None, S, tn), lambda i, j: (j // per, 0, j % per))
    else:
        N = b.shape[1]
        b_spec = pl.BlockSpec((S, tn), lambda i, j: (0, j))

    def body(a_ref, b_ref, o_ref):
        o_ref[...] = lax.dot_general(a_ref[...], b_ref[...], _TN,
                                     preferred_element_type=F32).astype(BF)

    return pl.pallas_call(
        _after(body, 2, deps), name=name,
        grid=(M // tm, N // tn),
        in_specs=[pl.BlockSpec((S, tm), lambda i, j: (0, i)), b_spec] + [ANY] * len(deps),
        out_specs=pl.BlockSpec((tm, tn), lambda i, j: (i, j)),
        out_shape=jax.ShapeDtypeStruct((M, N), BF),
        compiler_params=_cparams(("parallel", "parallel")),
    )(a, b, *deps)


def matmul_nt_normbwd(terms, x_in, gain, dx, name, tk, tm=512, deps=()):
    specs, operands, ranges = [], [], []
    start = 0
    for (a, w, stacked) in terms:
        K = w.shape[1]
        n = K // tk
        lo = start

        def rel(k, lo=lo, n=n):
            return jnp.clip(k - lo, 0, n - 1)

        if stacked:
            per = D // tk
            specs.append(pl.BlockSpec((None, tm, tk),
                                      lambda i, k, rel=rel, per=per: (rel(k) // per, i, rel(k) % per)))
        else:
            specs.append(pl.BlockSpec((tm, tk), lambda i, k, rel=rel: (i, rel(k))))
        specs.append(pl.BlockSpec((D, tk), lambda i, k, rel=rel: (0, rel(k))))
        operands += [a, w]
        ranges.append((lo, lo + n))
        start += n
    nk = start
    nt = len(terms)

    def body(*refs):
        aw = refs[:2 * nt]
        x_ref, g_ref, dx_ref, dxo_ref, dxb_ref, dg_ref, acc_ref = refs[2 * nt:]
        i = pl.program_id(0)
        k = pl.program_id(1)

        @pl.when(k == 0)
        def _():
            acc_ref[...] = jnp.zeros_like(acc_ref)

        @pl.when((i == 0) & (k == 0))
        def _():
            dg_ref[...] = jnp.zeros_like(dg_ref)

        for t in range(nt):
            lo, hi = ranges[t]

            @pl.when((k >= lo) & (k < hi))
            def _(t=t):
                acc_ref[...] += lax.dot_general(aw[2 * t][...], aw[2 * t + 1][...], _NT,
                                                preferred_element_type=F32)

        @pl.when(k == nk - 1)
        def _():
            xv = x_ref[...]
            r = _rms(xv)
            xhat = xv * r
            dh = acc_ref[...]
            dg_ref[0:1, :] += jnp.sum(dh * xhat, axis=0, keepdims=True)
            dxh = dh * g_ref[...]
            dxn = r * (dxh - xhat * jnp.mean(dxh * xhat, axis=-1, keepdims=True))
            out = dx_ref[...] + dxn
            dxo_ref[...] = out
            dxb_ref[...] = out.astype(BF)

    row = pl.BlockSpec((tm, D), lambda i, k: (i, 0))
    return pl.pallas_call(
        _after(body, 2 * nt + 3, deps), name=name,
        grid=(S // tm, nk),
        in_specs=specs + [row, pl.BlockSpec((1, D), lambda i, k: (0, 0)), row] + [ANY] * len(deps),
        out_specs=[row, row, pl.BlockSpec((8, D), lambda i, k: (0, 0))],
        out_shape=[jax.ShapeDtypeStruct((S, D), F32), jax.ShapeDtypeStruct((S, D), BF),
                   jax.ShapeDtypeStruct((8, D), F32)],
        scratch_shapes=[pltpu.VMEM((tm, D), F32)],
        compiler_params=_cparams(("arbitrary", "arbitrary")),
    )(*operands, x_in, gain, dx, *deps)


def loss_head(x, gain, target, name, tm=512):
    def body(x_ref, g_ref, t_ref, dxo_ref, dxb_ref, dg_ref, sq_ref):
        @pl.when(pl.program_id(0) == 0)
        def _():
            dg_ref[...] = jnp.zeros_like(dg_ref)
            sq_ref[...] = jnp.zeros_like(sq_ref)
        xv = x_ref[...]
        r = _rms(xv)
        xhat = xv * r
        err = xhat * g_ref[...] - t_ref[...]
        sq_ref[0:1, :] += jnp.sum(err * err, axis=0, keepdims=True)
        dy = err * (1.0 / D)
        dg_ref[0:1, :] += jnp.sum(dy * xhat, axis=0, keepdims=True)
        dxh = dy * g_ref[...]
        out = r * (dxh - xhat * jnp.mean(dxh * xhat, axis=-1, keepdims=True))
        dxo_ref[...] = out
        dxb_ref[...] = out.astype(BF)

    row = pl.BlockSpec((tm, D), lambda i: (i, 0))
    acc = pl.BlockSpec((8, D), lambda i: (0, 0))
    return pl.pallas_call(
        body, name=name,
        grid=(S // tm,),
        in_specs=[row, pl.BlockSpec((1, D), lambda i: (0, 0)), row],
        out_specs=[row, row, acc, acc],
        out_shape=[jax.ShapeDtypeStruct((S, D), F32), jax.ShapeDtypeStruct((S, D), BF),
                   jax.ShapeDtypeStruct((8, D), F32), jax.ShapeDtypeStruct((8, D), F32)],
        compiler_params=_cparams(("arbitrary",)),
    )(x, gain, target)


def _shift_down(p, n, row):
    return jnp.where(row >= n, pltpu.roll(p, n, axis=0), 0.0)


def _shift_up(p, n, row):
    return jnp.where(row < S - n, pltpu.roll(p, S - n, axis=0), 0.0)


def conv_fwd(z3, taps, name, tn=128):
    def body(z_ref, k_ref, m_ref):
        b = z_ref[0].astype(F32)
        p = z_ref[1].astype(F32) * z_ref[2].astype(F32)
        row = lax.broadcasted_iota(jnp.int32, p.shape, 0)
        y = (k_ref[2:3, :] * p + k_ref[1:2, :] * _shift_down(p, 1, row)
             + k_ref[0:1, :] * _shift_down(p, 2, row))
        m_ref[...] = (b * y).astype(BF)

    return pl.pallas_call(
        body, name=name,
        grid=(D // tn,),
        in_specs=[pl.BlockSpec((3, S, tn), lambda j: (0, 0, j)),
                  pl.BlockSpec((8, tn), lambda j: (0, j))],
        out_specs=pl.BlockSpec((S, tn), lambda j: (0, j)),
        out_shape=jax.ShapeDtypeStruct((S, D), BF),
        compiler_params=_cparams(("parallel",)),
    )(z3, taps)


def conv_bwd(dm, z3, taps, name, tn=128, deps=()):
    def body(dm_ref, z_ref, k_ref, dz_ref, dk_ref):
        dmv = dm_ref[...]
        b = z_ref[0].astype(F32)
        c = z_ref[1].astype(F32)
        u = z_ref[2].astype(F32)
        p = c * u
        row = lax.broadcasted_iota(jnp.int32, p.shape, 0)
        p1 = _shift_down(p, 1, row)
        p2 = _shift_down(p, 2, row)
        y = k_ref[2:3, :] * p + k_ref[1:2, :] * p1 + k_ref[0:1, :] * p2
        dy = dmv * b
        dz_ref[0] = (dmv * y).astype(BF)
        dp = (k_ref[2:3, :] * dy + k_ref[1:2, :] * _shift_up(dy, 1, row)
              + k_ref[0:1, :] * _shift_up(dy, 2, row))
        dz_ref[1] = (dp * u).astype(BF)
        dz_ref[2] = (dp * c).astype(BF)
        dk_ref[...] = jnp.zeros_like(dk_ref)
        dk_ref[0:1, :] = jnp.sum(dy * p2, axis=0, keepdims=True)
        dk_ref[1:2, :] = jnp.sum(dy * p1, axis=0, keepdims=True)
        dk_ref[2:3, :] = jnp.sum(dy * p, axis=0, keepdims=True)

    return pl.pallas_call(
        _after(body, 3, deps), name=name,
        grid=(D // tn,),
        in_specs=[pl.BlockSpec((S, tn), lambda j: (0, j)),
                  pl.BlockSpec((3, S, tn), lambda j: (0, 0, j)),
                  pl.BlockSpec((8, tn), lambda j: (0, j))] + [ANY] * len(deps),
        out_specs=[pl.BlockSpec((3, S, tn), lambda j: (0, 0, j)),
                   pl.BlockSpec((8, tn), lambda j: (0, j))],
        out_shape=[jax.ShapeDtypeStruct((3, S, D), BF), jax.ShapeDtypeStruct((8, D), F32)],
        compiler_params=_cparams(("parallel",)),
    )(dm, z3, taps, *deps)


def _t5_bucket(dist):
    exact = NUM_BUCKETS // 2
    df = jnp.maximum(dist, 1).astype(jnp.float32)
    large = exact + (jnp.log(df / exact) / math.log(MAX_DISTANCE / exact)
                     * (NUM_BUCKETS - exact)).astype(jnp.int32)
    large = jnp.minimum(large, NUM_BUCKETS - 1)
    return jnp.where(dist < exact, dist, large)


def _bucket_onehot_t():
    qi = jnp.arange(BLK)[:, None]
    ki = jnp.arange(2 * BLK)[None, :]
    rel = qi + BLK - ki
    band = ((rel >= 0) & (rel <= BLK)).reshape(1, -1).astype(F32)
    hots = []
    for d in BRANCH_DILATIONS:
        bucket = _t5_bucket(jnp.clip(rel, 0) * d).reshape(1, -1)
        hots.append((jnp.arange(NUM_BUCKETS)[:, None] == bucket).astype(F32))
    return jnp.stack(hots), band


def bias_tables(rel_bias_t, onehot_t, band, name):
    def body(rb_ref, oh_ref, band_ref, o_ref):
        b = jnp.dot(rb_ref[...], oh_ref[...], preferred_element_type=F32,
                    precision=lax.Precision.HIGHEST)
        o_ref[...] = jnp.where(band_ref[...] > 0.5, b, NEG_INF)

    n = BLK * 2 * BLK
    return pl.pallas_call(
        body, name=name,
        grid=(3,),
        in_specs=[pl.BlockSpec((H, NUM_BUCKETS), lambda g: (0, 0)),
                  pl.BlockSpec((None, NUM_BUCKETS, n), lambda g: (g, 0, 0)),
                  pl.BlockSpec((1, n), lambda g: (0, 0))],
        out_specs=pl.BlockSpec((None, H, n), lambda g: (g, 0, 0)),
        out_shape=jax.ShapeDtypeStruct((3, H, n), F32),
        compiler_params=_cparams(("parallel",)),
    )(rel_bias_t, onehot_t, band)


def bias_grad(dbias, onehot_t, name):
    def body(db_ref, oh_ref, o_ref):
        @pl.when(pl.program_id(0) == 0)
        def _():
            o_ref[...] = jnp.zeros_like(o_ref)
        o_ref[...] += lax.dot_general(db_ref[...], oh_ref[...], _NT, preferred_element_type=F32,
                                      precision=lax.Precision.HIGHEST)

    n = BLK * 2 * BLK
    return pl.pallas_call(
        body, name=name,
        grid=(dbias.shape[0],),
        in_specs=[pl.BlockSpec((None, H, n), lambda g: (g, 0, 0)),
                  pl.BlockSpec((None, NUM_BUCKETS, n), lambda g: (g % 3, 0, 0))],
        out_specs=pl.BlockSpec((H, NUM_BUCKETS), lambda g: (0, 0)),
        out_shape=jax.ShapeDtypeStruct((H, NUM_BUCKETS), F32),
        compiler_params=_cparams(("arbitrary",)),
    )(dbias, onehot_t)


def _head_masks():
    lane = lax.broadcasted_iota(jnp.int32, (1, 2 * DH), 1)
    return (lane < DH, lane >= DH)


def _deinterleave(src_ref, dst_ref, d, dtype):
    L = S // d
    for r in range(d):
        dst_ref[r * L:(r + 1) * L, :] = src_ref[pl.ds(r, L, stride=d), :].astype(dtype)


def _branch_loops(d, block):
    L = S // d
    nb = L // BLK

    def residue(r, carry):
        base = 0 if d == 1 else pl.multiple_of(r * L, BLK)
        block(base, base, BLK, True)
        if nb > 1:
            def loop(n, c):
                q0 = pl.multiple_of(base + n * BLK, BLK)
                block(q0, pl.multiple_of(q0 - BLK, BLK), 2 * BLK, False)
                return c
            lax.fori_loop(1, nb, loop, 0)
        return carry

    if d == 1:
        residue(0, 0)
    else:
        lax.fori_loop(0, d, residue, 0)


def attention_fwd(z3, bias3, name):
    W = 2 * DH
    CH = 256

    def body(q_ref, k_ref, v_ref, b_ref, o_ref, lse_ref, stage, qd, kd, vd, od, ld, on, ln):
        masks = _head_masks()
        for src, dst in ((q_ref, qd), (k_ref, kd), (v_ref, vd)):
            stage[...] = src[...].astype(F32)
            for gi, d in enumerate(BRANCH_DILATIONS[1:]):
                _deinterleave(stage, dst.at[gi], d, BF)

        for g, d in enumerate(BRANCH_DILATIONS):
            qs, ks, vs = (q_ref, k_ref, v_ref) if d == 1 else (qd.at[g - 1], kd.at[g - 1], vd.at[g - 1])
            o_dst, l_dst = (on.at[0], ln.at[0]) if d == 1 else (od, ld)

            def block(q0, k0, nk, first, g=g, qs=qs, ks=ks, vs=vs, o_dst=o_dst, l_dst=l_dst):
                q = qs[pl.ds(q0, BLK), :]
                kk = ks[pl.ds(k0, nk), :]
                vv = vs[pl.ds(k0, nk), :]
                outs, lses = [], []
                for hh in range(2):
                    qh = jnp.where(masks[hh], q, jnp.zeros_like(q))
                    bias_h = b_ref[g, hh][:, BLK:] if first else b_ref[g, hh]
                    s = lax.dot_general(qh, kk, _NT, preferred_element_type=F32) * SCALE + bias_h
                    mx = jnp.max(s, axis=1, keepdims=True)
                    p = jnp.exp(s - mx)
                    l = jnp.sum(p, axis=1, keepdims=True)
                    outs.append(jnp.dot(p.astype(BF), vv, preferred_element_type=F32) / l)
                    lses.append(mx + jnp.log(l))
                o_dst[pl.ds(q0, BLK), :] = jnp.where(masks[0], outs[0], outs[1])
                l_dst[pl.ds(q0, BLK), :] = jnp.where(masks[0], lses[0], lses[1])

            _branch_loops(d, block)
            if d > 1:
                L = S // d
                for r in range(d):
                    on[g, pl.ds(r, L, stride=d), :] = od[r * L:(r + 1) * L, :]
                    ln[g, pl.ds(r, L, stride=d), :] = ld[r * L:(r + 1) * L, :]

        def join(c, carry):
            rows = pl.ds(pl.multiple_of(c * CH, CH), CH)
            a, b, cc = ln[0, rows, :], ln[1, rows, :], ln[2, rows, :]
            mx = jnp.maximum(jnp.maximum(a, b), cc)
            ea, eb, ec = jnp.exp(a - mx), jnp.exp(b - mx), jnp.exp(cc - mx)
            tot = ea + eb + ec
            o_ref[rows, :] = ((ea * on[0, rows, :] + eb * on[1, rows, :] + ec * on[2, rows, :])
                              / tot).astype(BF)
            lse_ref[rows, :] = mx + jnp.log(tot)
            return carry
        lax.fori_loop(0, S // CH, join, 0)

    col = pl.BlockSpec((S, W), lambda hp: (0, hp))
    return pl.pallas_call(
        body, name=name,
        grid=(D // W,),
        in_specs=[pl.BlockSpec((None, S, W), lambda hp: (0, 0, hp)),
                  pl.BlockSpec((None, S, W), lambda hp: (1, 0, hp)),
                  pl.BlockSpec((None, S, W), lambda hp: (2, 0, hp)),
                  pl.BlockSpec((3, 2, BLK, 2 * BLK), lambda hp: (0, hp, 0, 0))],
        out_specs=[col, col],
        out_shape=[jax.ShapeDtypeStruct((S, D), BF), jax.ShapeDtypeStruct((S, D), F32)],
        scratch_shapes=[pltpu.VMEM((S, W), F32),
                        pltpu.VMEM((2, S, W), BF), pltpu.VMEM((2, S, W), BF), pltpu.VMEM((2, S, W), BF),
                        pltpu.VMEM((S, W), F32), pltpu.VMEM((S, W), F32),
                        pltpu.VMEM((3, S, W), F32), pltpu.VMEM((3, S, W), F32)],
        compiler_params=_cparams(("parallel",)),
    )(z3, z3, z3, bias3)


def attention_bwd(z3, dob, ob, lse_b, bias3, name, deps=()):
    W = 2 * DH
    CH = 256

    def body(q_ref, k_ref, v_ref, do_ref, o_ref, lse_ref, b_ref, dz_ref, db_ref,
             stage, delta, qd, kd, vd, dod, lsd, dld, res, acc):
        masks = _head_masks()

        def rowsum(c, carry):
            rows = pl.ds(pl.multiple_of(c * CH, CH), CH)
            prod = do_ref[rows, :].astype(F32) * o_ref[rows, :].astype(F32)
            sa = jnp.sum(jnp.where(masks[0], prod, 0.0), axis=1, keepdims=True)
            sb = jnp.sum(jnp.where(masks[1], prod, 0.0), axis=1, keepdims=True)
            delta[rows, :] = jnp.where(masks[0], sa, sb)
            return carry
        lax.fori_loop(0, S // CH, rowsum, 0)

        for src, dst in ((q_ref, qd), (k_ref, kd), (v_ref, vd), (do_ref, dod)):
            stage[...] = src[...].astype(F32)
            for gi, d in enumerate(BRANCH_DILATIONS[1:]):
                _deinterleave(stage, dst.at[gi], d, BF)
        for gi, d in enumerate(BRANCH_DILATIONS[1:]):
            _deinterleave(lse_ref, lsd.at[gi], d, F32)
            _deinterleave(delta, dld.at[gi], d, F32)

        db_ref[...] = jnp.zeros_like(db_ref)
        for g, d in enumerate(BRANCH_DILATIONS):
            if d == 1:
                qs, ks, vs, dos, ls, dl = q_ref, k_ref, v_ref, do_ref, lse_ref, delta
            else:
                qs, ks, vs, dos = qd.at[g - 1], kd.at[g - 1], vd.at[g - 1], dod.at[g - 1]
                ls, dl = lsd.at[g - 1], dld.at[g - 1]
            res[1] = jnp.zeros((S, W), F32)
            res[2] = jnp.zeros((S, W), F32)

            def block(q0, k0, nk, first, g=g, qs=qs, ks=ks, vs=vs, dos=dos, ls=ls, dl=dl):
                kk = ks[pl.ds(k0, nk), :]
                vv = vs[pl.ds(k0, nk), :]
                q = qs[pl.ds(q0, BLK), :]
                do = dos[pl.ds(q0, BLK), :]
                lse_blk = ls[pl.ds(q0, BLK), :]
                del_blk = dl[pl.ds(q0, BLK), :]
                dqs = []
                dk = jnp.zeros((nk, W), F32)
                dv = jnp.zeros((nk, W), F32)
                for hh in range(2):
                    qh = jnp.where(masks[hh], q, jnp.zeros_like(q))
                    doh = jnp.where(masks[hh], do, jnp.zeros_like(do))
                    bias_h = b_ref[g, hh][:, BLK:] if first else b_ref[g, hh]
                    s = lax.dot_general(qh, kk, _NT, preferred_element_type=F32) * SCALE + bias_h
                    p = jnp.exp(s - lse_blk[:, hh * DH:hh * DH + 1])
                    dp = lax.dot_general(doh, vv, _NT, preferred_element_type=F32)
                    ds = p * (dp - del_blk[:, hh * DH:hh * DH + 1])
                    if first:
                        db_ref[g, hh, :, BLK:] += ds
                    else:
                        db_ref[g, hh] += ds
                    dsb = ds.astype(BF)
                    dqs.append(jnp.dot(dsb, kk, preferred_element_type=F32) * SCALE)
                    dk += lax.dot_general(dsb, qh, _TN, preferred_element_type=F32) * SCALE
                    dv += lax.dot_general(p.astype(BF), doh, _TN, preferred_element_type=F32)
                res[0, pl.ds(q0, BLK), :] = jnp.where(masks[0], dqs[0], dqs[1])
                res[1, pl.ds(k0, nk), :] += dk
                res[2, pl.ds(k0, nk), :] += dv

            _branch_loops(d, block)
            L = S // d
            for t in range(3):
                if d == 1:
                    acc[t] = res[t]
                else:
                    for r in range(d):
                        acc[t, pl.ds(r, L, stride=d), :] = (acc[t, pl.ds(r, L, stride=d), :]
                                                            + res[t, r * L:(r + 1) * L, :])
        for t in range(3):
            dz_ref[t] = acc[t].astype(BF)

    col = pl.BlockSpec((S, W), lambda hp: (0, hp))
    bspec = pl.BlockSpec((3, 2, BLK, 2 * BLK), lambda hp: (0, hp, 0, 0))
    return pl.pallas_call(
        _after(body, 7, deps), name=name,
        grid=(D // W,),
        in_specs=[pl.BlockSpec((None, S, W), lambda hp: (0, 0, hp)),
                  pl.BlockSpec((None, S, W), lambda hp: (1, 0, hp)),
                  pl.BlockSpec((None, S, W), lambda hp: (2, 0, hp)),
                  col, col, col, bspec] + [ANY] * len(deps),
        out_specs=[pl.BlockSpec((3, S, W), lambda hp: (0, 0, hp)), bspec],
        out_shape=[jax.ShapeDtypeStruct((3, S, D), BF),
                   jax.ShapeDtypeStruct((3, H, BLK, 2 * BLK), F32)],
        scratch_shapes=[pltpu.VMEM((S, W), F32), pltpu.VMEM((S, W), F32),
                        pltpu.VMEM((2, S, W), BF), pltpu.VMEM((2, S, W), BF),
                        pltpu.VMEM((2, S, W), BF), pltpu.VMEM((2, S, W), BF),
                        pltpu.VMEM((2, S, W), F32), pltpu.VMEM((2, S, W), F32),
                        pltpu.VMEM((3, S, W), F32), pltpu.VMEM((3, S, W), F32)],
        compiler_params=_cparams(("parallel",)),
    )(z3, z3, z3, dob, ob, lse_b, bias3, *deps)


def _me():
    return lax.axis_index("x"), lax.axis_index("y"), lax.axis_index("c")


def _other_chips(x, y):
    return [(1 - x, y), (x, 1 - y), (1 - x, 1 - y)]


def _shard_window(ref, axis, t, shape):
    R, C = shape
    if axis == 0:
        return ref.at[pl.ds(pl.multiple_of(t * R, 128), R), :]
    return ref.at[:, pl.ds(pl.multiple_of(t * C, 128), C)]


def all_gather_weights(shards, axes, name):
    n = len(shards)
    shapes = [s.shape for s in shards]
    outs_shape = [jax.ShapeDtypeStruct((8 * R, C) if ax == 0 else (R, 8 * C), BF)
                  for (R, C), ax in zip(shapes, axes)]

    def body(*refs):
        ins, outs = refs[:n], refs[n:2 * n]
        send_sems, recv_sems, local_sems = refs[2 * n:]
        x, y, c = _me()
        sibling = (x, y, 1 - c)
        chips = _other_chips(x, y)
        barrier = pltpu.get_barrier_semaphore()
        for peer in [sibling] + [(*chip, c) for chip in chips]:
            pl.semaphore_signal(barrier, inc=1, device_id=peer, device_id_type=MESH)
        pl.semaphore_wait(barrier, 4)

        def win(i, px, py, pc):
            return _shard_window(outs[i], axes[i], 4 * px + 2 * py + pc, shapes[i])

        def copy(i, k, block, to, src=None):
            return pltpu.make_async_remote_copy(
                src_ref=win(i, *block) if src is None else src, dst_ref=win(i, *block),
                send_sem=send_sems.at[i * 7 + k], recv_sem=recv_sems.at[i * 7 + k],
                device_id=to, device_id_type=MESH)

        mine = [pltpu.make_async_copy(ins[i], win(i, x, y, c), local_sems.at[i]) for i in range(n)]
        for cp in mine:
            cp.start()
        first = []
        for i in range(n):
            first.append(copy(i, 0, (x, y, c), sibling, src=ins[i]))
            for j, chip in enumerate(chips):
                first.append(copy(i, 1 + j, (x, y, c), (*chip, c), src=ins[i]))
        for cp in first:
            cp.start()
        passed = []
        for j, chip in enumerate(chips):
            for i in range(n):
                copy(i, 1 + j, (*chip, c), (x, y, c)).wait_recv()
                cp = copy(i, 4 + j, (*chip, c), sibling)
                cp.start()
                passed.append(cp)
        for i in range(n):
            copy(i, 0, sibling, (x, y, c)).wait_recv()
        for j, chip in enumerate(chips):
            for i in range(n):
                copy(i, 4 + j, (*chip, 1 - c), (x, y, c)).wait_recv()
        for cp in first + passed:
            cp.wait_send()
        for cp in mine:
            cp.wait()

    return pl.kernel(
        body, out_type=outs_shape, name=name,
        mesh=plsc.ScalarSubcoreMesh(axis_name="sequencer", num_cores=1),
        scratch_types=[pltpu.SemaphoreType.DMA((7 * n,)), pltpu.SemaphoreType.DMA((7 * n,)),
                       pltpu.SemaphoreType.DMA((n,))],
        compiler_params=pltpu.CompilerParams(collective_id=1),
    )(*shards)


def pair_exchange_grads(grads, axes, shapes, name):
    n = len(grads)

    def body(*refs):
        ins, outs = refs[:n], refs[n:2 * n]
        send_sems, recv_sems = refs[2 * n:]
        x, y, c = _me()
        sibling = (x, y, 1 - c)
        barrier = pltpu.get_barrier_semaphore()
        pl.semaphore_signal(barrier, inc=1, device_id=sibling, device_id_type=MESH)
        pl.semaphore_wait(barrier, 1)
        copies = []
        for i in range(n):
            for q in range(4):
                t = 2 * q + (1 - c)
                copies.append(pltpu.make_async_remote_copy(
                    src_ref=_shard_window(ins[i], axes[i], t, shapes[i]), dst_ref=outs[i].at[q],
                    send_sem=send_sems.at[i * 4 + q], recv_sem=recv_sems.at[i * 4 + q],
                    device_id=sibling, device_id_type=MESH))
        for cp in copies:
            cp.start()
        for cp in copies:
            cp.wait_recv()
        for cp in copies:
            cp.wait_send()

    return pl.kernel(
        body, out_type=[jax.ShapeDtypeStruct((4,) + tuple(sh), BF) for sh in shapes], name=name,
        mesh=plsc.ScalarSubcoreMesh(axis_name="sequencer", num_cores=1),
        scratch_types=[pltpu.SemaphoreType.DMA((4 * n,)), pltpu.SemaphoreType.DMA((4 * n,))],
        compiler_params=pltpu.CompilerParams(collective_id=2),
    )(*grads)


def pair_add(grad, landed, axis, shape, c_idx, name, deps=()):
    R, C = shape

    def body(c_ref, g_ref, l_ref, o_ref):
        o_ref[...] = (g_ref[...].astype(F32) + l_ref[...].astype(F32)).astype(BF)

    if axis == 0:
        g_spec = pl.BlockSpec((R, C), lambda q, c_ref: (2 * q + c_ref[0], 0))
    else:
        g_spec = pl.BlockSpec((R, C), lambda q, c_ref: (0, 2 * q + c_ref[0]))
    blk = pl.BlockSpec((None, R, C), lambda q, c_ref: (q, 0, 0))
    return pl.pallas_call(
        _after(body, 3, deps), name=name,
        grid_spec=pltpu.PrefetchScalarGridSpec(
            num_scalar_prefetch=1, grid=(4,), in_specs=[g_spec, blk] + [ANY] * len(deps),
            out_specs=blk),
        out_shape=jax.ShapeDtypeStruct((4, R, C), BF),
        compiler_params=_cparams(("parallel",)),
    )(c_idx, grad, landed, *deps)


def chip_exchange_grads(parts, name):
    n = len(parts)

    def body(*refs):
        ins, outs = refs[:n], refs[n:2 * n]
        send_sems, recv_sems = refs[2 * n:]
        x, y, c = _me()
        barrier = pltpu.get_barrier_semaphore()
        for px, py in _other_chips(x, y):
            pl.semaphore_signal(barrier, inc=1, device_id=(px, py, c), device_id_type=MESH)
        pl.semaphore_wait(barrier, 3)
        copies = []
        for i in range(n):
            for k, (px, py) in enumerate(_other_chips(x, y)):
                copies.append(pltpu.make_async_remote_copy(
                    src_ref=ins[i].at[2 * px + py], dst_ref=outs[i].at[k],
                    send_sem=send_sems.at[i * 3 + k], recv_sem=recv_sems.at[i * 3 + k],
                    device_id=(px, py, c), device_id_type=MESH))
        for cp in copies:
            cp.start()
        for cp in copies:
            cp.wait_recv()
        for cp in copies:
            cp.wait_send()

    return pl.kernel(
        body, out_type=[jax.ShapeDtypeStruct((3,) + tuple(p.shape[1:]), BF) for p in parts], name=name,
        mesh=plsc.ScalarSubcoreMesh(axis_name="sequencer", num_cores=1),
        scratch_types=[pltpu.SemaphoreType.DMA((3 * n,)), pltpu.SemaphoreType.DMA((3 * n,))],
        compiler_params=pltpu.CompilerParams(collective_id=3),
    )(*parts)


def all_gather_small(v, name):
    R, C = v.shape

    def body(v_ref, out_ref, send_sems, recv_sems, local_sem):
        x, y, c = _me()
        me, sibling = (x, y, c), (x, y, 1 - c)
        chips = _other_chips(x, y)

        def slot(px, py, pc):
            return out_ref.at[4 * px + 2 * py + pc]

        def copy(k, block, to, src=None):
            return pltpu.make_async_remote_copy(
                src_ref=slot(*block) if src is None else src, dst_ref=slot(*block),
                send_sem=send_sems.at[k], recv_sem=recv_sems.at[k],
                device_id=to, device_id_type=MESH)

        mine = pltpu.make_async_copy(v_ref, slot(*me), local_sem)
        mine.start()
        first = [copy(0, me, sibling, src=v_ref)]
        first += [copy(1 + j, me, (*chip, c), src=v_ref) for j, chip in enumerate(chips)]
        for cp in first:
            cp.start()
        passed = [copy(4 + j, (*chip, c), sibling) for j, chip in enumerate(chips)]
        for j, chip in enumerate(chips):
            copy(1 + j, (*chip, c), me).wait_recv()
            passed[j].start()
        copy(0, sibling, me).wait_recv()
        for j, chip in enumerate(chips):
            copy(4 + j, (*chip, 1 - c), me).wait_recv()
        for cp in first + passed:
            cp.wait_send()
        mine.wait()

    return pl.pallas_call(
        body, name=name,
        in_specs=[pl.BlockSpec(memory_space=pltpu.VMEM)],
        out_specs=pl.BlockSpec(memory_space=pltpu.VMEM),
        out_shape=jax.ShapeDtypeStruct((NDEV, R, C), F32),
        scratch_shapes=[pltpu.SemaphoreType.DMA((7,)), pltpu.SemaphoreType.DMA((7,)),
                        pltpu.SemaphoreType.DMA],
    )(v)


def _adamw(w, g, m, v):
    m = ADAM_B1 * m + (1.0 - ADAM_B1) * g
    v = ADAM_B2 * v + (1.0 - ADAM_B2) * (g * g)
    m_hat = m / (1.0 - ADAM_B1 ** ADAM_STEP)
    v_hat = v / (1.0 - ADAM_B2 ** ADAM_STEP)
    delta = -ADAM_LR * (m_hat / (jnp.sqrt(v_hat) + ADAM_EPS) + ADAM_WD * w)
    return delta, m, v


def reduce_adamw(part, landed, w, m, v, layer, q_idx, name, prev=(), deps=()):
    R, C = part.shape[1:]
    r, c = w.shape[1:]
    tr = r // 2 if r % 16 == 0 and r >= 256 else r
    tR = tr if tr != r else R
    extra = tuple(prev) + tuple(deps)

    def body(q_ref, p_ref, l_ref, w_ref, m_ref, v_ref, g_out, d_out, m_out, v_out):
        g = p_ref[...].astype(F32)
        for k in range(3):
            g = g + l_ref[k].astype(F32)
        g = g[:tr, :c]
        d, mm, vv = _adamw(w_ref[...], g, m_ref[...], v_ref[...])
        g_out[...] = g
        d_out[...] = d
        m_out[...] = mm
        v_out[...] = vv

    wspec = pl.BlockSpec((None, tr, c), lambda i, q_ref: (layer, i, 0))
    out = jax.ShapeDtypeStruct(w.shape, F32)
    return pl.pallas_call(
        _after(body, 6, extra), name=name,
        grid_spec=pltpu.PrefetchScalarGridSpec(
            num_scalar_prefetch=1, grid=(r // tr,),
            in_specs=[pl.BlockSpec((None, tR, C), lambda i, q_ref: (q_ref[0], i, 0)),
                      pl.BlockSpec((3, tR, C), lambda i, q_ref: (0, i, 0)),
                      wspec, wspec, wspec] + [ANY] * len(extra),
            out_specs=[wspec] * 4),
        out_shape=[out] * 4,
        input_output_aliases={6 + k: k for k in range(len(prev))},
        compiler_params=_cparams(("parallel",)),
    )(q_idx, part, landed, w, m, v, *extra)


def small_reduce_adamw(gathered, w, m, v, name):
    R, C = w.shape

    def body(a_ref, w_ref, m_ref, v_ref, g_out, d_out, m_out, v_out):
        g = a_ref[0]
        for k in range(1, NDEV):
            g = g + a_ref[k]
        d, mm, vv = _adamw(w_ref[...], g, m_ref[...], v_ref[...])
        g_out[...] = g
        d_out[...] = d
        m_out[...] = mm
        v_out[...] = vv

    out = jax.ShapeDtypeStruct((R, C), F32)
    return pl.pallas_call(body, name=name, out_shape=[out] * 4,
                          compiler_params=_cparams())(gathered, w, m, v)


def _pad_cols(a, n):
    return jnp.pad(a, ((0, 0), (0, n - a.shape[1])))


def _pad_rows(a, n):
    return jnp.pad(a, ((0, n - a.shape[0]), (0, 0)))


SMALL_ROWS = 16


def _pack_small(mix, ffn, fin, taps_full, relb):
    return jnp.concatenate([
        mix, ffn, fin.reshape(1, D), taps_full.reshape(6, D),
        jnp.pad(relb.reshape(1, NUM_BUCKETS * H), ((0, 0), (0, D - NUM_BUCKETS * H)))], axis=0)


def kernel(x, mix_norm, ffn_norm, final_norm, conv_w_in, conv_kernel, conv_w_out, attn_w_qkv, attn_w_out, rel_bias, ffn_w_gate, ffn_w_up, ffn_w_down, loss_target, m_mix_norm, m_ffn_norm, m_final_norm, m_conv_w_in, m_conv_kernel, m_conv_w_out, m_attn_w_qkv, m_attn_w_out, m_rel_bias, m_ffn_w_gate, m_ffn_w_up, m_ffn_w_down, v_mix_norm, v_ffn_norm, v_final_norm, v_conv_w_in, v_conv_kernel, v_conv_w_out, v_attn_w_qkv, v_attn_w_out, v_rel_bias, v_ffn_w_gate, v_ffn_w_up, v_ffn_w_down):
    xi, yi, ci = _me()
    me = 4 * xi + 2 * yi + ci
    c_idx = jnp.reshape(ci, (1,)).astype(jnp.int32)
    q_idx = jnp.reshape(2 * xi + yi, (1,)).astype(jnp.int32)
    col0 = me * (D // NDEV)

    taps_local = jnp.zeros((2, 3, D), F32)
    taps_local = lax.dynamic_update_slice(taps_local, conv_kernel, (0, 0, col0))
    taps_pack = jnp.pad(taps_local.reshape(6, D), ((0, 2), (0, 0)))
    taps_all = all_gather_small(taps_pack, "ag_taps")
    taps_sum = jnp.sum(taps_all, axis=0)
    taps = [jnp.pad(taps_sum[3 * j:3 * j + 3], ((0, 5), (0, 0))) for j in range(2)]

    mixer_in = (conv_w_in, attn_w_qkv)
    mixer_out = (conv_w_out, attn_w_out)
    wts = []
    for i in range(DEPTH):
        j = i // 2
        shards = [mixer_in[i % 2][j].astype(BF), mixer_out[i % 2][j].astype(BF),
                  _pad_cols(ffn_w_gate[i].astype(BF), FF_SHARD_PAD),
                  _pad_cols(ffn_w_up[i].astype(BF), FF_SHARD_PAD),
                  _pad_rows(ffn_w_down[i].astype(BF), FF_SHARD_PAD)]
        wts.append(all_gather_weights(shards, (1, 0, 1, 1, 0), f"ag_weights_l{i}"))

    onehot_t, band = _bucket_onehot_t()
    bias3 = bias_tables(rel_bias.T, onehot_t, band, "bias_tables").reshape(3, H, BLK, 2 * BLK)

    saved = []
    xc = x[0]
    for i in range(DEPTH):
        w_in, w_out, w_g, w_u, w_d = wts[i]
        j = i // 2
        x_mix = xc
        z3, h_mix = norm_matmul3(xc, mix_norm[i:i + 1], w_in, f"mix_in_l{i}")
        if i % 2 == 0:
            act = conv_fwd(z3, taps[j], f"conv_fwd_l{i}")
            lse_b = None
        else:
            act, lse_b = attention_fwd(z3, bias3, f"attn_fwd_l{i}")
        xc = matmul_residual(act, w_out, xc, f"mix_out_l{i}")
        x_ffn = xc
        g, u, a, h_ffn = norm_swiglu_up(xc, ffn_norm[i:i + 1], w_g, w_u, f"ffn_up_l{i}")
        xc = matmul_residual(a, w_d, xc, f"ffn_down_l{i}")
        saved.append((x_mix, h_mix, z3, act, lse_b, x_ffn, h_ffn, g, u, a))

    dx, dxb, dg_final, sq = loss_head(xc, final_norm.reshape(1, D), loss_target[0], "loss_head")
    loss = lax.psum(0.5 * jnp.sum(sq[0]) / D, ("x", "y", "c"))

    w_params = ((conv_w_in, m_conv_w_in, v_conv_w_in), (attn_w_qkv, m_attn_w_qkv, v_attn_w_qkv))
    o_params = ((conv_w_out, m_conv_w_out, v_conv_w_out), (attn_w_out, m_attn_w_out, v_attn_w_out))
    dg_mix = [None] * DEPTH
    dg_ffn = [None] * DEPTH
    dtaps = [None, None]
    dbias_all = []
    shape_in, shape_out = (D, 3 * D // NDEV), (D // NDEV, D)
    shape_up, shape_down = (D, FF_SHARD_PAD), (FF_SHARD_PAD, D)
    stacked = {}

    def pair_stage(grads, landed1, axes, shapes, tag, tok):
        parts = []
        for t in range(len(grads)):
            parts.append(pair_add(grads[t], landed1[t], axes[t], shapes[t], c_idx,
                                  f"rs_add_{tag}_{t}", deps=[tok]))
            tok = parts[-1]
        return parts, chip_exchange_grads(parts, f"rs_chip_{tag}"), tok

    def adamw_stage(parts, landed2, params, tag, tok):
        for t, (pname, w_, m_, v_, layer) in enumerate(params):
            res = reduce_adamw(parts[t], landed2[t], w_, m_, v_, layer, q_idx, f"adamw_{tag}_{t}",
                               prev=stacked.get(pname, ()), deps=[tok])
            stacked[pname] = res
            tok = res[0]
        return tok

    tok = dxb
    mix_wait = None
    mix_chip = None
    ffn_chip = None
    for i in reversed(range(DEPTH)):
        w_in, w_out, w_g, w_u, w_d = wts[i]
        j = i // 2
        x_mix, h_mix, z3, act, lse_b, x_ffn, h_ffn, g, u, a = saved[i]
        ffn_params = [("ffn_w_gate", ffn_w_gate, m_ffn_w_gate, v_ffn_w_gate, i),
                      ("ffn_w_up", ffn_w_up, m_ffn_w_up, v_ffn_w_up, i),
                      ("ffn_w_down", ffn_w_down, m_ffn_w_down, v_ffn_w_down, i)]
        if i % 2 == 0:
            mix_params = [("conv_w_in", conv_w_in, m_conv_w_in, v_conv_w_in, j),
                          ("conv_w_out", conv_w_out, m_conv_w_out, v_conv_w_out, j)]
        else:
            mix_params = [("attn_w_qkv", attn_w_qkv, m_attn_w_qkv, v_attn_w_qkv, j),
                          ("attn_w_out", attn_w_out, m_attn_w_out, v_attn_w_out, j)]
        dgate, dup = swiglu_bwd_da(dxb, w_d, g, u, f"ffn_da_l{i}", deps=[tok])
        gw_d = matmul_tn(a, dxb, f"ffn_dwd_l{i}", deps=[dgate])
        tok = gw_d
        if mix_wait is not None:
            grads_m, landed1_m, params_m, tag_m = mix_wait
            parts_m, landed2_m, tok = pair_stage(grads_m, landed1_m, (1, 0), (shape_in, shape_out),
                                                 tag_m, tok)
            mix_chip = (parts_m, landed2_m, params_m, tag_m)
            mix_wait = None
        gw_g = matmul_tn(h_ffn, dgate, f"ffn_dwg_l{i}", deps=[tok])
        gw_u = matmul_tn(h_ffn, dup, f"ffn_dwu_l{i}", deps=[gw_g])
        grads_f = [gw_g, gw_u, gw_d]
        landed1_f = pair_exchange_grads(grads_f, (1, 1, 0), (shape_up, shape_up, shape_down),
                                        f"rs_pair_f{i}")
        tok = gw_u
        if ffn_chip is not None:
            tok = adamw_stage(*ffn_chip, tok)
            ffn_chip = None
        dx, dxb, dg_ffn[i] = matmul_nt_normbwd(
            [(dgate, w_g, False), (dup, w_u, False)], x_ffn, ffn_norm[i:i + 1], dx, f"ffn_dh_l{i}",
            tk=768, deps=[tok])
        gw_out = matmul_tn(act, dxb, f"mix_dwout_l{i}")
        if i % 2 == 0:
            dact = matmul_nt(dxb, w_out, f"mix_dact_l{i}", out_dtype=F32, deps=[gw_out])
        else:
            dact = matmul_nt(dxb, w_out, f"mix_dact_l{i}", out_dtype=BF, deps=[gw_out])
        parts_f, landed2_f, tok = pair_stage(grads_f, landed1_f, (1, 1, 0),
                                             (shape_up, shape_up, shape_down), f"f{i}", dact)
        ffn_chip = (parts_f, landed2_f, ffn_params, f"f{i}")
        if i % 2 == 0:
            dz3, dtaps[j] = conv_bwd(dact, z3, taps[j], f"conv_bwd_l{i}", deps=[tok])
        else:
            dz3, dbias3 = attention_bwd(z3, dact, act, lse_b, bias3, f"attn_bwd_l{i}", deps=[tok])
            dbias_all.append(dbias3.reshape(3, H, BLK * 2 * BLK))
        gw_in = matmul_tn(h_mix, dz3, f"mix_dwin_l{i}")
        grads_m = [gw_in, gw_out]
        landed1_m = pair_exchange_grads(grads_m, (1, 0), (shape_in, shape_out), f"rs_pair_m{i}")
        mix_wait = (grads_m, landed1_m, mix_params, f"m{i}")
        tok = gw_in
        if mix_chip is not None:
            tok = adamw_stage(*mix_chip, tok)
            mix_chip = None
        dx, dxb, dg_mix[i] = matmul_nt_normbwd(
            [(dz3, w_in, True)], x_mix, mix_norm[i:i + 1], dx, f"mix_dh_l{i}", tk=512, deps=[tok])
        tok = dxb
    grads_m, landed1_m, params_m, tag_m = mix_wait
    parts_m, landed2_m, tok = pair_stage(grads_m, landed1_m, (1, 0), (shape_in, shape_out), tag_m, tok)
    tok = adamw_stage(*ffn_chip, tok)

    grad_relb_t = bias_grad(jnp.concatenate(dbias_all), onehot_t, "bias_grad")
    dtaps_full = jnp.stack([dtaps[0][:3], dtaps[1][:3]])
    g_small = _pack_small(jnp.concatenate([d[0:1] for d in dg_mix], axis=0),
                          jnp.concatenate([d[0:1] for d in dg_ffn], axis=0),
                          dg_final[0], dtaps_full, grad_relb_t.T)
    gathered = all_gather_small(g_small, "ag_small_grads")

    def taps_at_cols(k):
        return lax.dynamic_update_slice(jnp.zeros((2, 3, D), F32), k, (0, 0, col0))

    w_small = _pack_small(mix_norm, ffn_norm, final_norm, taps_at_cols(conv_kernel), rel_bias)
    m_small = _pack_small(m_mix_norm, m_ffn_norm, m_final_norm, taps_at_cols(m_conv_kernel), m_rel_bias)
    v_small = _pack_small(v_mix_norm, v_ffn_norm, v_final_norm, taps_at_cols(v_conv_kernel), v_rel_bias)
    small = small_reduce_adamw(gathered, w_small, m_small, v_small, "adamw_small")

    def unpack_small(p):
        taps_p = lax.dynamic_slice(p[9:15].reshape(2, 3, D), (0, 0, col0), (2, 3, D // NDEV))
        return {"mix_norm": p[0:4], "ffn_norm": p[4:8], "final_norm": p[8],
                "conv_kernel": taps_p, "rel_bias": p[15, :NUM_BUCKETS * H].reshape(NUM_BUCKETS, H)}

    small_out = [unpack_small(p) for p in small]
    adamw_stage(parts_m, landed2_m, params_m, tag_m, small[0])

    names = ["mix_norm", "ffn_norm", "final_norm", "conv_w_in", "conv_kernel", "conv_w_out",
             "attn_w_qkv", "attn_w_out", "rel_bias", "ffn_w_gate", "ffn_w_up", "ffn_w_down"]
    outs = [loss, dx.reshape(1, S, D)]
    for o in range(4):
        for nme in names:
            outs.append(stacked[nme][o] if nme in stacked else small_out[o][nme])
    return tuple(outs)
```

```python
import functools
import math

import jax
import jax.numpy as jnp
from jax import lax
from jax.experimental import pallas as pl
from jax.experimental.pallas import tpu as pltpu
from jax.experimental.pallas import tpu_sc as plsc

S = 2048
D = 1024
H = 16
DH = 64
DFF = 2816
NDEV = 8
DEPTH = 4
FF_SHARD = DFF // NDEV
FF_SHARD_PAD = 384
DFF_PAD = FF_SHARD_PAD * NDEV
BLK = 128
BRANCH_DILATIONS = (1, 4, 16)
NUM_BUCKETS = 32
MAX_DISTANCE = 2048
EPS = 1e-6
NEG_INF = -1e30
SCALE = DH ** -0.5

ADAM_LR = 0.001
ADAM_B1 = 0.9
ADAM_B2 = 0.999
ADAM_EPS = 1e-08
ADAM_WD = 0.01
ADAM_STEP = 10

BF = jnp.bfloat16
F32 = jnp.float32
VMEM_LIMIT_BYTES = 56 * 1024 * 1024
MESH = pl.DeviceIdType.MESH
ANY = pl.BlockSpec(memory_space=pl.ANY)

_NT = (((1,), (1,)), ((), ()))
_TN = (((0,), (0,)), ((), ()))


def _cparams(sem=None):
    return pltpu.CompilerParams(dimension_semantics=sem, vmem_limit_bytes=VMEM_LIMIT_BYTES)


def _after(body, n, deps):
    nd = len(deps)
    if nd == 0:
        return body

    def ordered(*refs):
        body(*refs[:n], *refs[n + nd:])
    return ordered


def _rms(x):
    return lax.rsqrt(jnp.mean(x * x, axis=-1, keepdims=True) + EPS)


def norm_matmul3(x, gain, w, name, tm=512, tn=512):
    per = D // tn

    def body(x_ref, g_ref, w_ref, z_ref, h_ref, hs_ref):
        @pl.when(pl.program_id(1) == 0)
        def _():
            xv = x_ref[...]
            hv = (xv * _rms(xv) * g_ref[...]).astype(BF)
            hs_ref[...] = hv
            h_ref[...] = hv
        z_ref[...] = jnp.dot(hs_ref[...], w_ref[...], preferred_element_type=F32).astype(BF)

    return pl.pallas_call(
        body, name=name,
        grid=(S // tm, 3 * D // tn),
        in_specs=[pl.BlockSpec((tm, D), lambda i, j: (i, 0)),
                  pl.BlockSpec((1, D), lambda i, j: (0, 0)),
                  pl.BlockSpec((D, tn), lambda i, j: (0, j))],
        out_specs=[pl.BlockSpec((None, tm, tn), lambda i, j: (j // per, i, j % per)),
                   pl.BlockSpec((tm, D), lambda i, j: (i, 0))],
        out_shape=[jax.ShapeDtypeStruct((3, S, D), BF), jax.ShapeDtypeStruct((S, D), BF)],
        scratch_shapes=[pltpu.VMEM((tm, D), BF)],
        compiler_params=_cparams(("parallel", "arbitrary")),
    )(x, gain, w)


def norm_swiglu_up(x, gain, wg, wu, name, tm=512, tn=512):
    def body(x_ref, g_ref, wg_ref, wu_ref, go_ref, uo_ref, ao_ref, h_ref, hs_ref):
        @pl.when(pl.program_id(1) == 0)
        def _():
            xv = x_ref[...]
            hv = (xv * _rms(xv) * g_ref[...]).astype(BF)
            hs_ref[...] = hv
            h_ref[...] = hv
        hv = hs_ref[...]
        g = jnp.dot(hv, wg_ref[...], preferred_element_type=F32)
        u = jnp.dot(hv, wu_ref[...], preferred_element_type=F32)
        go_ref[...] = g.astype(BF)
        uo_ref[...] = u.astype(BF)
        ao_ref[...] = (g * jax.nn.sigmoid(g) * u).astype(BF)

    act = jax.ShapeDtypeStruct((S, DFF_PAD), BF)
    blk = pl.BlockSpec((tm, tn), lambda i, j: (i, j))
    return pl.pallas_call(
        body, name=name,
        grid=(S // tm, DFF_PAD // tn),
        in_specs=[pl.BlockSpec((tm, D), lambda i, j: (i, 0)),
                  pl.BlockSpec((1, D), lambda i, j: (0, 0)),
                  pl.BlockSpec((D, tn), lambda i, j: (0, j)),
                  pl.BlockSpec((D, tn), lambda i, j: (0, j))],
        out_specs=[blk, blk, blk, pl.BlockSpec((tm, D), lambda i, j: (i, 0))],
        out_shape=[act, act, act, jax.ShapeDtypeStruct((S, D), BF)],
        scratch_shapes=[pltpu.VMEM((tm, D), BF)],
        compiler_params=_cparams(("parallel", "arbitrary")),
    )(x, gain, wg, wu)


def matmul_residual(a, w, x, name, tm=512, tn=512):
    K = a.shape[1]

    def body(a_ref, w_ref, x_ref, o_ref):
        o_ref[...] = x_ref[...] + jnp.dot(a_ref[...], w_ref[...], preferred_element_type=F32)

    return pl.pallas_call(
        body, name=name,
        grid=(S // tm, D // tn),
        in_specs=[pl.BlockSpec((tm, K), lambda i, j: (i, 0)),
                  pl.BlockSpec((K, tn), lambda i, j: (0, j)),
                  pl.BlockSpec((tm, tn), lambda i, j: (i, j))],
        out_specs=pl.BlockSpec((tm, tn), lambda i, j: (i, j)),
        out_shape=jax.ShapeDtypeStruct((S, D), F32),
        compiler_params=_cparams(("parallel", "parallel")),
    )(a, w, x)


def matmul_nt(a, w, name, out_dtype=BF, tm=512, tn=512, deps=()):
    K = a.shape[1]
    N = w.shape[0]

    def body(a_ref, w_ref, o_ref):
        o_ref[...] = lax.dot_general(a_ref[...], w_ref[...], _NT,
                                     preferred_element_type=F32).astype(o_ref.dtype)

    return pl.pallas_call(
        _after(body, 2, deps), name=name,
        grid=(S // tm, N // tn),
        in_specs=[pl.BlockSpec((tm, K), lambda i, j: (i, 0)),
                  pl.BlockSpec((tn, K), lambda i, j: (j, 0))] + [ANY] * len(deps),
        out_specs=pl.BlockSpec((tm, tn), lambda i, j: (i, j)),
        out_shape=jax.ShapeDtypeStruct((S, N), out_dtype),
        compiler_params=_cparams(("parallel", "parallel")),
    )(a, w, *deps)


def swiglu_bwd_da(dxb, wd, g, u, name, tm=512, tn=512, deps=()):
    def body(dx_ref, w_ref, g_ref, u_ref, dg_ref, du_ref):
        da = lax.dot_general(dx_ref[...], w_ref[...], _NT, preferred_element_type=F32)
        gv = g_ref[...].astype(F32)
        uv = u_ref[...].astype(F32)
        sig = jax.nn.sigmoid(gv)
        dg_ref[...] = (da * uv * (sig * (1.0 + gv * (1.0 - sig)))).astype(BF)
        du_ref[...] = (da * (gv * sig)).astype(BF)

    act = jax.ShapeDtypeStruct((S, DFF_PAD), BF)
    blk = pl.BlockSpec((tm, tn), lambda i, j: (i, j))
    return pl.pallas_call(
        _after(body, 4, deps), name=name,
        grid=(S // tm, DFF_PAD // tn),
        in_specs=[pl.BlockSpec((tm, D), lambda i, j: (i, 0)),
                  pl.BlockSpec((tn, D), lambda i, j: (j, 0)),
                  blk, blk] + [ANY] * len(deps),
        out_specs=[blk, blk],
        out_shape=[act, act],
        compiler_params=_cparams(("parallel", "parallel")),
    )(dxb, wd, g, u, *deps)


def matmul_tn(a, b, name, tm=512, tn=512, deps=()):
    M = a.shape[1]
    if b.ndim == 3:
        per = D // tn
        N = 3 * D
        b_spec = pl.BlockSpec((None, S, tn), lambda i, j: (j // per, 0, j % per))
    else:
        N = b.shape[1]
        b_spec = pl.BlockSpec((S, tn), lambda i, j: (0, j))

    def body(a_ref, b_ref, o_ref):
        o_ref[...] = lax.dot_general(a_ref[...], b_ref[...], _TN,
                                     preferred_element_type=F32).astype(BF)

    return pl.pallas_call(
        _after(body, 2, deps), name=name,
        grid=(M // tm, N // tn),
        in_specs=[pl.BlockSpec((S, tm), lambda i, j: (0, i)), b_spec] + [ANY] * len(deps),
        out_specs=pl.BlockSpec((tm, tn), lambda i, j: (i, j)),
        out_shape=jax.ShapeDtypeStruct((M, N), BF),
        compiler_params=_cparams(("parallel", "parallel")),
    )(a, b, *deps)


def matmul_nt_normbwd(terms, x_in, gain, dx, name, tk, tm=512, deps=()):
    specs, operands, ranges = [], [], []
    start = 0
    for (a, w, stacked) in terms:
        K = w.shape[1]
        n = K // tk
        lo = start

        def rel(k, lo=lo, n=n):
            return jnp.clip(k - lo, 0, n - 1)

        if stacked:
            per = D // tk
            specs.append(pl.BlockSpec((None, tm, tk),
                                      lambda i, k, rel=rel, per=per: (rel(k) // per, i, rel(k) % per)))
        else:
            specs.append(pl.BlockSpec((tm, tk), lambda i, k, rel=rel: (i, rel(k))))
        specs.append(pl.BlockSpec((D, tk), lambda i, k, rel=rel: (0, rel(k))))
        operands += [a, w]
        ranges.append((lo, lo + n))
        start += n
    nk = start
    nt = len(terms)

    def body(*refs):
        aw = refs[:2 * nt]
        x_ref, g_ref, dx_ref, dxo_ref, dxb_ref, dg_ref, acc_ref = refs[2 * nt:]
        i = pl.program_id(0)
        k = pl.program_id(1)

        @pl.when(k == 0)
        def _():
            acc_ref[...] = jnp.zeros_like(acc_ref)

        @pl.when((i == 0) & (k == 0))
        def _():
            dg_ref[...] = jnp.zeros_like(dg_ref)

        for t in range(nt):
            lo, hi = ranges[t]

            @pl.when((k >= lo) & (k < hi))
            def _(t=t):
                acc_ref[...] += lax.dot_general(aw[2 * t][...], aw[2 * t + 1][...], _NT,
                                                preferred_element_type=F32)

        @pl.when(k == nk - 1)
        def _():
            xv = x_ref[...]
            r = _rms(xv)
            xhat = xv * r
            dh = acc_ref[...]
            dg_ref[0:1, :] += jnp.sum(dh * xhat, axis=0, keepdims=True)
            dxh = dh * g_ref[...]
            dxn = r * (dxh - xhat * jnp.mean(dxh * xhat, axis=-1, keepdims=True))
            out = dx_ref[...] + dxn
            dxo_ref[...] = out
            dxb_ref[...] = out.astype(BF)

    row = pl.BlockSpec((tm, D), lambda i, k: (i, 0))
    return pl.pallas_call(
        _after(body, 2 * nt + 3, deps), name=name,
        grid=(S // tm, nk),
        in_specs=specs + [row, pl.BlockSpec((1, D), lambda i, k: (0, 0)), row] + [ANY] * len(deps),
        out_specs=[row, row, pl.BlockSpec((8, D), lambda i, k: (0, 0))],
        out_shape=[jax.ShapeDtypeStruct((S, D), F32), jax.ShapeDtypeStruct((S, D), BF),
                   jax.ShapeDtypeStruct((8, D), F32)],
        scratch_shapes=[pltpu.VMEM((tm, D), F32)],
        compiler_params=_cparams(("arbitrary", "arbitrary")),
    )(*operands, x_in, gain, dx, *deps)


def loss_head(x, gain, target, name, tm=512):
    def body(x_ref, g_ref, t_ref, dxo_ref, dxb_ref, dg_ref, sq_ref):
        @pl.when(pl.program_id(0) == 0)
        def _():
            dg_ref[...] = jnp.zeros_like(dg_ref)
            sq_ref[...] = jnp.zeros_like(sq_ref)
        xv = x_ref[...]
        r = _rms(xv)
        xhat = xv * r
        err = xhat * g_ref[...] - t_ref[...]
        sq_ref[0:1, :] += jnp.sum(err * err, axis=0, keepdims=True)
        dy = err * (1.0 / D)
        dg_ref[0:1, :] += jnp.sum(dy * xhat, axis=0, keepdims=True)
        dxh = dy * g_ref[...]
        out = r * (dxh - xhat * jnp.mean(dxh * xhat, axis=-1, keepdims=True))
        dxo_ref[...] = out
        dxb_ref[...] = out.astype(BF)

    row = pl.BlockSpec((tm, D), lambda i: (i, 0))
    acc = pl.BlockSpec((8, D), lambda i: (0, 0))
    return pl.pallas_call(
        body, name=name,
        grid=(S // tm,),
        in_specs=[row, pl.BlockSpec((1, D), lambda i: (0, 0)), row],
        out_specs=[row, row, acc, acc],
        out_shape=[jax.ShapeDtypeStruct((S, D), F32), jax.ShapeDtypeStruct((S, D), BF),
                   jax.ShapeDtypeStruct((8, D), F32), jax.ShapeDtypeStruct((8, D), F32)],
        compiler_params=_cparams(("arbitrary",)),
    )(x, gain, target)


def _shift_down(p, n, row):
    return jnp.where(row >= n, pltpu.roll(p, n, axis=0), 0.0)


def _shift_up(p, n, row):
    return jnp.where(row < S - n, pltpu.roll(p, S - n, axis=0), 0.0)


def conv_fwd(z3, taps, name, tn=128):
    def body(z_ref, k_ref, m_ref):
        b = z_ref[0].astype(F32)
        p = z_ref[1].astype(F32) * z_ref[2].astype(F32)
        row = lax.broadcasted_iota(jnp.int32, p.shape, 0)
        y = (k_ref[2:3, :] * p + k_ref[1:2, :] * _shift_down(p, 1, row)
             + k_ref[0:1, :] * _shift_down(p, 2, row))
        m_ref[...] = (b * y).astype(BF)

    return pl.pallas_call(
        body, name=name,
        grid=(D // tn,),
        in_specs=[pl.BlockSpec((3, S, tn), lambda j: (0, 0, j)),
                  pl.BlockSpec((8, tn), lambda j: (0, j))],
        out_specs=pl.BlockSpec((S, tn), lambda j: (0, j)),
        out_shape=jax.ShapeDtypeStruct((S, D), BF),
        compiler_params=_cparams(("parallel",)),
    )(z3, taps)


def conv_bwd(dm, z3, taps, name, tn=128, deps=()):
    def body(dm_ref, z_ref, k_ref, dz_ref, dk_ref):
        dmv = dm_ref[...]
        b = z_ref[0].astype(F32)
        c = z_ref[1].astype(F32)
        u = z_ref[2].astype(F32)
        p = c * u
        row = lax.broadcasted_iota(jnp.int32, p.shape, 0)
        p1 = _shift_down(p, 1, row)
        p2 = _shift_down(p, 2, row)
        y = k_ref[2:3, :] * p + k_ref[1:2, :] * p1 + k_ref[0:1, :] * p2
        dy = dmv * b
        dz_ref[0] = (dmv * y).astype(BF)
        dp = (k_ref[2:3, :] * dy + k_ref[1:2, :] * _shift_up(dy, 1, row)
              + k_ref[0:1, :] * _shift_up(dy, 2, row))
        dz_ref[1] = (dp * u).astype(BF)
        dz_ref[2] = (dp * c).astype(BF)
        dk_ref[...] = jnp.zeros_like(dk_ref)
        dk_ref[0:1, :] = jnp.sum(dy * p2, axis=0, keepdims=True)
        dk_ref[1:2, :] = jnp.sum(dy * p1, axis=0, keepdims=True)
        dk_ref[2:3, :] = jnp.sum(dy * p, axis=0, keepdims=True)

    return pl.pallas_call(
        _after(body, 3, deps), name=name,
        grid=(D // tn,),
        in_specs=[pl.BlockSpec((S, tn), lambda j: (0, j)),
                  pl.BlockSpec((3, S, tn), lambda j: (0, 0, j)),
                  pl.BlockSpec((8, tn), lambda j: (0, j))] + [ANY] * len(deps),
        out_specs=[pl.BlockSpec((3, S, tn), lambda j: (0, 0, j)),
                   pl.BlockSpec((8, tn), lambda j: (0, j))],
        out_shape=[jax.ShapeDtypeStruct((3, S, D), BF), jax.ShapeDtypeStruct((8, D), F32)],
        compiler_params=_cparams(("parallel",)),
    )(dm, z3, taps, *deps)


def _t5_bucket(dist):
    exact = NUM_BUCKETS // 2
    df = jnp.maximum(dist, 1).astype(jnp.float32)
    large = exact + (jnp.log(df / exact) / math.log(MAX_DISTANCE / exact)
                     * (NUM_BUCKETS - exact)).astype(jnp.int32)
    large = jnp.minimum(large, NUM_BUCKETS - 1)
    return jnp.where(dist < exact, dist, large)


def _bucket_onehot_t():
    qi = jnp.arange(BLK)[:, None]
    ki = jnp.arange(2 * BLK)[None, :]
    rel = qi + BLK - ki
    band = ((rel >= 0) & (rel <= BLK)).reshape(1, -1).astype(F32)
    hots = []
    for d in BRANCH_DILATIONS:
        bucket = _t5_bucket(jnp.clip(rel, 0) * d).reshape(1, -1)
        hots.append((jnp.arange(NUM_BUCKETS)[:, None] == bucket).astype(F32))
    return jnp.stack(hots), band


def bias_tables(rel_bias_t, onehot_t, band, name):
    def body(rb_ref, oh_ref, band_ref, o_ref):
        b = jnp.dot(rb_ref[...], oh_ref[...], preferred_element_type=F32,
                    precision=lax.Precision.HIGHEST)
        o_ref[...] = jnp.where(band_ref[...] > 0.5, b, NEG_INF)

    n = BLK * 2 * BLK
    return pl.pallas_call(
        body, name=name,
        grid=(3,),
        in_specs=[pl.BlockSpec((H, NUM_BUCKETS), lambda g: (0, 0)),
                  pl.BlockSpec((None, NUM_BUCKETS, n), lambda g: (g, 0, 0)),
                  pl.BlockSpec((1, n), lambda g: (0, 0))],
        out_specs=pl.BlockSpec((None, H, n), lambda g: (g, 0, 0)),
        out_shape=jax.ShapeDtypeStruct((3, H, n), F32),
        compiler_params=_cparams(("parallel",)),
    )(rel_bias_t, onehot_t, band)


def bias_grad(dbias, onehot_t, name):
    def body(db_ref, oh_ref, o_ref):
        @pl.when(pl.program_id(0) == 0)
        def _():
            o_ref[...] = jnp.zeros_like(o_ref)
        o_ref[...] += lax.dot_general(db_ref[...], oh_ref[...], _NT, preferred_element_type=F32,
                                      precision=lax.Precision.HIGHEST)

    n = BLK * 2 * BLK
    return pl.pallas_call(
        body, name=name,
        grid=(dbias.shape[0],),
        in_specs=[pl.BlockSpec((None, H, n), lambda g: (g, 0, 0)),
                  pl.BlockSpec((None, NUM_BUCKETS, n), lambda g: (g % 3, 0, 0))],
        out_specs=pl.BlockSpec((H, NUM_BUCKETS), lambda g: (0, 0)),
        out_shape=jax.ShapeDtypeStruct((H, NUM_BUCKETS), F32),
        compiler_params=_cparams(("arbitrary",)),
    )(dbias, onehot_t)


def _head_masks():
    lane = lax.broadcasted_iota(jnp.int32, (1, 2 * DH), 1)
    return (lane < DH, lane >= DH)


def _stack_heads(x, masks):
    zero = jnp.zeros_like(x)
    return jnp.concatenate([jnp.where(masks[0], x, zero), jnp.where(masks[1], x, zero)], axis=0)


def _deinterleave(src_ref, dst_ref, d, dtype):
    L = S // d
    for r in range(d):
        dst_ref[r * L:(r + 1) * L, :] = src_ref[pl.ds(r, L, stride=d), :].astype(dtype)


def _branch_loops(d, block):
    L = S // d
    nb = L // BLK

    def first(base):
        block(base, base, BLK, True)

    def later(base, n):
        q0 = pl.multiple_of(base + n * BLK, BLK)
        block(q0, pl.multiple_of(q0 - BLK, BLK), 2 * BLK, False)

    if d == 1:
        unroll = 3
        assert (nb - 1) % unroll == 0
        first(0)

        def trip(it, c):
            for u in range(unroll):
                later(0, 1 + it * unroll + u)
            return c
        lax.fori_loop(0, (nb - 1) // unroll, trip, 0)
    elif nb > 1:
        def residue(r, c):
            base = pl.multiple_of(r * L, BLK)
            first(base)
            for n in range(1, nb):
                later(base, n)
            return c
        lax.fori_loop(0, d, residue, 0)
    else:
        unroll = 4
        assert d % unroll == 0

        def trip(it, c):
            for u in range(unroll):
                first(pl.multiple_of((it * unroll + u) * L, BLK))
            return c
        lax.fori_loop(0, d // unroll, trip, 0)


def attention_fwd(z3, bias3, name):
    W = 2 * DH
    CH = 256

    def body(q_ref, k_ref, v_ref, b_ref, o_ref, lse_ref, stage, qd, kd, vd, od, ld, on, ln):
        masks = _head_masks()
        for src, dst in ((q_ref, qd), (k_ref, kd), (v_ref, vd)):
            stage[...] = src[...].astype(F32)
            for gi, d in enumerate(BRANCH_DILATIONS[1:]):
                _deinterleave(stage, dst.at[gi], d, BF)

        for g, d in enumerate(BRANCH_DILATIONS):
            qs, ks, vs = (q_ref, k_ref, v_ref) if d == 1 else (qd.at[g - 1], kd.at[g - 1], vd.at[g - 1])
            o_dst, l_dst = (on.at[0], ln.at[0]) if d == 1 else (od, ld)

            def block(q0, k0, nk, first, g=g, qs=qs, ks=ks, vs=vs, o_dst=o_dst, l_dst=l_dst):
                q2 = _stack_heads(qs[pl.ds(q0, BLK), :], masks)
                kk = ks[pl.ds(k0, nk), :]
                vv = vs[pl.ds(k0, nk), :]
                bias = b_ref[g][:, BLK:] if first else b_ref[g]
                s = lax.dot_general(q2, kk, _NT, preferred_element_type=F32) * SCALE + bias
                mx = jnp.max(s, axis=1, keepdims=True)
                p = jnp.exp(s - mx)
                l = jnp.sum(p, axis=1, keepdims=True)
                o2 = jnp.dot(p.astype(BF), vv, preferred_element_type=F32) / l
                lse2 = mx + jnp.log(l)
                o_dst[pl.ds(q0, BLK), :] = jnp.where(masks[0], o2[:BLK], o2[BLK:])
                l_dst[pl.ds(q0, BLK), :] = jnp.where(masks[0], lse2[:BLK], lse2[BLK:])

            _branch_loops(d, block)
            if d > 1:
                L = S // d
                for r in range(d):
                    on[g, pl.ds(r, L, stride=d), :] = od[r * L:(r + 1) * L, :]
                    ln[g, pl.ds(r, L, stride=d), :] = ld[r * L:(r + 1) * L, :]

        def join(c, carry):
            rows = pl.ds(pl.multiple_of(c * CH, CH), CH)
            a, b, cc = ln[0, rows, :], ln[1, rows, :], ln[2, rows, :]
            mx = jnp.maximum(jnp.maximum(a, b), cc)
            ea, eb, ec = jnp.exp(a - mx), jnp.exp(b - mx), jnp.exp(cc - mx)
            tot = ea + eb + ec
            o_ref[rows, :] = ((ea * on[0, rows, :] + eb * on[1, rows, :] + ec * on[2, rows, :])
                              / tot).astype(BF)
            lse_ref[rows, :] = mx + jnp.log(tot)
            return carry
        lax.fori_loop(0, S // CH, join, 0)

    col = pl.BlockSpec((S, W), lambda hp: (0, hp))
    return pl.pallas_call(
        body, name=name,
        grid=(D // W,),
        in_specs=[pl.BlockSpec((None, S, W), lambda hp: (0, 0, hp)),
                  pl.BlockSpec((None, S, W), lambda hp: (1, 0, hp)),
                  pl.BlockSpec((None, S, W), lambda hp: (2, 0, hp)),
                  pl.BlockSpec((3, 2 * BLK, 2 * BLK), lambda hp: (0, hp, 0))],
        out_specs=[col, col],
        out_shape=[jax.ShapeDtypeStruct((S, D), BF), jax.ShapeDtypeStruct((S, D), F32)],
        scratch_shapes=[pltpu.VMEM((S, W), F32),
                        pltpu.VMEM((2, S, W), BF), pltpu.VMEM((2, S, W), BF), pltpu.VMEM((2, S, W), BF),
                        pltpu.VMEM((S, W), F32), pltpu.VMEM((S, W), F32),
                        pltpu.VMEM((3, S, W), F32), pltpu.VMEM((3, S, W), F32)],
        compiler_params=_cparams(("parallel",)),
    )(z3, z3, z3, bias3)


def attention_bwd(z3, dob, ob, lse_b, bias3, name, deps=()):
    W = 2 * DH
    CH = 256

    def body(q_ref, k_ref, v_ref, do_ref, o_ref, lse_ref, b_ref, dz_ref, db_ref,
             stage, delta, qd, kd, vd, dod, lsd, dld, res, acc):
        masks = _head_masks()

        def rowsum(c, carry):
            rows = pl.ds(pl.multiple_of(c * CH, CH), CH)
            prod = do_ref[rows, :].astype(F32) * o_ref[rows, :].astype(F32)
            sa = jnp.sum(jnp.where(masks[0], prod, 0.0), axis=1, keepdims=True)
            sb = jnp.sum(jnp.where(masks[1], prod, 0.0), axis=1, keepdims=True)
            delta[rows, :] = jnp.where(masks[0], sa, sb)
            return carry
        lax.fori_loop(0, S // CH, rowsum, 0)

        for src, dst in ((q_ref, qd), (k_ref, kd), (v_ref, vd), (do_ref, dod)):
            stage[...] = src[...].astype(F32)
            for gi, d in enumerate(BRANCH_DILATIONS[1:]):
                _deinterleave(stage, dst.at[gi], d, BF)
        for gi, d in enumerate(BRANCH_DILATIONS[1:]):
            _deinterleave(lse_ref, lsd.at[gi], d, F32)
            _deinterleave(delta, dld.at[gi], d, F32)

        db_ref[...] = jnp.zeros_like(db_ref)
        for g, d in enumerate(BRANCH_DILATIONS):
            if d == 1:
                qs, ks, vs, dos, ls, dl = q_ref, k_ref, v_ref, do_ref, lse_ref, delta
            else:
                qs, ks, vs, dos = qd.at[g - 1], kd.at[g - 1], vd.at[g - 1], dod.at[g - 1]
                ls, dl = lsd.at[g - 1], dld.at[g - 1]
            res[1] = jnp.zeros((S, W), F32)
            res[2] = jnp.zeros((S, W), F32)

            def block(q0, k0, nk, first, g=g, qs=qs, ks=ks, vs=vs, dos=dos, ls=ls, dl=dl):
                kk = ks[pl.ds(k0, nk), :]
                vv = vs[pl.ds(k0, nk), :]
                q2 = _stack_heads(qs[pl.ds(q0, BLK), :], masks)
                do2 = _stack_heads(dos[pl.ds(q0, BLK), :], masks)
                lse_blk = ls[pl.ds(q0, BLK), :]
                del_blk = dl[pl.ds(q0, BLK), :]
                lse2 = jnp.concatenate([lse_blk[:, 0:1], lse_blk[:, DH:DH + 1]], axis=0)
                del2 = jnp.concatenate([del_blk[:, 0:1], del_blk[:, DH:DH + 1]], axis=0)
                bias = b_ref[g][:, BLK:] if first else b_ref[g]
                s = lax.dot_general(q2, kk, _NT, preferred_element_type=F32) * SCALE + bias
                p = jnp.exp(s - lse2)
                dp = lax.dot_general(do2, vv, _NT, preferred_element_type=F32)
                ds = p * (dp - del2)
                if first:
                    db_ref[g, :, BLK:] += ds
                else:
                    db_ref[g] += ds
                dsb = ds.astype(BF)
                dq2 = jnp.dot(dsb, kk, preferred_element_type=F32) * SCALE
                res[0, pl.ds(q0, BLK), :] = jnp.where(masks[0], dq2[:BLK], dq2[BLK:])
                res[1, pl.ds(k0, nk), :] += lax.dot_general(dsb, q2, _TN,
                                                            preferred_element_type=F32) * SCALE
                res[2, pl.ds(k0, nk), :] += lax.dot_general(p.astype(BF), do2, _TN,
                                                            preferred_element_type=F32)

            _branch_loops(d, block)
            L = S // d
            for t in range(3):
                if d == 1:
                    acc[t] = res[t]
                else:
                    for r in range(d):
                        acc[t, pl.ds(r, L, stride=d), :] = (acc[t, pl.ds(r, L, stride=d), :]
                                                            + res[t, r * L:(r + 1) * L, :])
        for t in range(3):
            dz_ref[t] = acc[t].astype(BF)

    col = pl.BlockSpec((S, W), lambda hp: (0, hp))
    bspec = pl.BlockSpec((3, 2 * BLK, 2 * BLK), lambda hp: (0, hp, 0))
    return pl.pallas_call(
        _after(body, 7, deps), name=name,
        grid=(D // W,),
        in_specs=[pl.BlockSpec((None, S, W), lambda hp: (0, 0, hp)),
                  pl.BlockSpec((None, S, W), lambda hp: (1, 0, hp)),
                  pl.BlockSpec((None, S, W), lambda hp: (2, 0, hp)),
                  col, col, col, bspec] + [ANY] * len(deps),
        out_specs=[pl.BlockSpec((3, S, W), lambda hp: (0, 0, hp)), bspec],
        out_shape=[jax.ShapeDtypeStruct((3, S, D), BF),
                   jax.ShapeDtypeStruct((3, H * BLK, 2 * BLK), F32)],
        scratch_shapes=[pltpu.VMEM((S, W), F32), pltpu.VMEM((S, W), F32),
                        pltpu.VMEM((2, S, W), BF), pltpu.VMEM((2, S, W), BF),
                        pltpu.VMEM((2, S, W), BF), pltpu.VMEM((2, S, W), BF),
                        pltpu.VMEM((2, S, W), F32), pltpu.VMEM((2, S, W), F32),
                        pltpu.VMEM((3, S, W), F32), pltpu.VMEM((3, S, W), F32)],
        compiler_params=_cparams(("parallel",)),
    )(z3, z3, z3, dob, ob, lse_b, bias3, *deps)


def _me():
    return lax.axis_index("x"), lax.axis_index("y"), lax.axis_index("c")


def _other_chips(x, y):
    return [(1 - x, y), (x, 1 - y), (1 - x, 1 - y)]


def _shard_window(ref, axis, t, shape):
    R, C = shape
    if axis == 0:
        return ref.at[pl.ds(pl.multiple_of(t * R, 128), R), :]
    return ref.at[:, pl.ds(pl.multiple_of(t * C, 128), C)]


def all_gather_weights(shards, axes, name):
    n = len(shards)
    shapes = [s.shape for s in shards]
    outs_shape = [jax.ShapeDtypeStruct((8 * R, C) if ax == 0 else (R, 8 * C), BF)
                  for (R, C), ax in zip(shapes, axes)]

    def body(*refs):
        ins, outs = refs[:n], refs[n:2 * n]
        send_sems, recv_sems, local_sems = refs[2 * n:]
        x, y, c = _me()
        sibling = (x, y, 1 - c)
        chips = _other_chips(x, y)
        barrier = pltpu.get_barrier_semaphore()
        for peer in [sibling] + [(*chip, c) for chip in chips]:
            pl.semaphore_signal(barrier, inc=1, device_id=peer, device_id_type=MESH)
        pl.semaphore_wait(barrier, 4)

        def win(i, px, py, pc):
            return _shard_window(outs[i], axes[i], 4 * px + 2 * py + pc, shapes[i])

        def copy(i, k, block, to, src=None):
            return pltpu.make_async_remote_copy(
                src_ref=win(i, *block) if src is None else src, dst_ref=win(i, *block),
                send_sem=send_sems.at[i * 7 + k], recv_sem=recv_sems.at[i * 7 + k],
                device_id=to, device_id_type=MESH)

        mine = [pltpu.make_async_copy(ins[i], win(i, x, y, c), local_sems.at[i]) for i in range(n)]
        for cp in mine:
            cp.start()
        first = []
        for i in range(n):
            first.append(copy(i, 0, (x, y, c), sibling, src=ins[i]))
            for j, chip in enumerate(chips):
                first.append(copy(i, 1 + j, (x, y, c), (*chip, c), src=ins[i]))
        for cp in first:
            cp.start()
        passed = []
        for j, chip in enumerate(chips):
            for i in range(n):
                copy(i, 1 + j, (*chip, c), (x, y, c)).wait_recv()
                cp = copy(i, 4 + j, (*chip, c), sibling)
                cp.start()
                passed.append(cp)
        for i in range(n):
            copy(i, 0, sibling, (x, y, c)).wait_recv()
        for j, chip in enumerate(chips):
            for i in range(n):
                copy(i, 4 + j, (*chip, 1 - c), (x, y, c)).wait_recv()
        for cp in first + passed:
            cp.wait_send()
        for cp in mine:
            cp.wait()

    return pl.kernel(
        body, out_type=outs_shape, name=name,
        mesh=plsc.ScalarSubcoreMesh(axis_name="sequencer", num_cores=1),
        scratch_types=[pltpu.SemaphoreType.DMA((7 * n,)), pltpu.SemaphoreType.DMA((7 * n,)),
                       pltpu.SemaphoreType.DMA((n,))],
        compiler_params=pltpu.CompilerParams(collective_id=1),
    )(*shards)


def pair_exchange_grads(grads, axes, shapes, name):
    n = len(grads)

    def body(*refs):
        ins, outs = refs[:n], refs[n:2 * n]
        send_sems, recv_sems = refs[2 * n:]
        x, y, c = _me()
        sibling = (x, y, 1 - c)
        barrier = pltpu.get_barrier_semaphore()
        pl.semaphore_signal(barrier, inc=1, device_id=sibling, device_id_type=MESH)
        pl.semaphore_wait(barrier, 1)
        copies = []
        for i in range(n):
            for q in range(4):
                t = 2 * q + (1 - c)
                copies.append(pltpu.make_async_remote_copy(
                    src_ref=_shard_window(ins[i], axes[i], t, shapes[i]), dst_ref=outs[i].at[q],
                    send_sem=send_sems.at[i * 4 + q], recv_sem=recv_sems.at[i * 4 + q],
                    device_id=sibling, device_id_type=MESH))
        for cp in copies:
            cp.start()
        for cp in copies:
            cp.wait_recv()
        for cp in copies:
            cp.wait_send()

    return pl.kernel(
        body, out_type=[jax.ShapeDtypeStruct((4,) + tuple(sh), BF) for sh in shapes], name=name,
        mesh=plsc.ScalarSubcoreMesh(axis_name="sequencer", num_cores=1),
        scratch_types=[pltpu.SemaphoreType.DMA((4 * n,)), pltpu.SemaphoreType.DMA((4 * n,))],
        compiler_params=pltpu.CompilerParams(collective_id=2),
    )(*grads)


def pair_add(grad, landed, axis, shape, c_idx, name, deps=()):
    R, C = shape

    def body(c_ref, g_ref, l_ref, o_ref):
        o_ref[...] = (g_ref[...].astype(F32) + l_ref[...].astype(F32)).astype(BF)

    if axis == 0:
        g_spec = pl.BlockSpec((R, C), lambda q, c_ref: (2 * q + c_ref[0], 0))
    else:
        g_spec = pl.BlockSpec((R, C), lambda q, c_ref: (0, 2 * q + c_ref[0]))
    blk = pl.BlockSpec((None, R, C), lambda q, c_ref: (q, 0, 0))
    return pl.pallas_call(
        _after(body, 3, deps), name=name,
        grid_spec=pltpu.PrefetchScalarGridSpec(
            num_scalar_prefetch=1, grid=(4,), in_specs=[g_spec, blk] + [ANY] * len(deps),
            out_specs=blk),
        out_shape=jax.ShapeDtypeStruct((4, R, C), BF),
        compiler_params=_cparams(("parallel",)),
    )(c_idx, grad, landed, *deps)


def chip_exchange_grads(parts, name):
    n = len(parts)

    def body(*refs):
        ins, outs = refs[:n], refs[n:2 * n]
        send_sems, recv_sems = refs[2 * n:]
        x, y, c = _me()
        barrier = pltpu.get_barrier_semaphore()
        for px, py in _other_chips(x, y):
            pl.semaphore_signal(barrier, inc=1, device_id=(px, py, c), device_id_type=MESH)
        pl.semaphore_wait(barrier, 3)
        copies = []
        for i in range(n):
            for k, (px, py) in enumerate(_other_chips(x, y)):
                copies.append(pltpu.make_async_remote_copy(
                    src_ref=ins[i].at[2 * px + py], dst_ref=outs[i].at[k],
                    send_sem=send_sems.at[i * 3 + k], recv_sem=recv_sems.at[i * 3 + k],
                    device_id=(px, py, c), device_id_type=MESH))
        for cp in copies:
            cp.start()
        for cp in copies:
            cp.wait_recv()
        for cp in copies:
            cp.wait_send()

    return pl.kernel(
        body, out_type=[jax.ShapeDtypeStruct((3,) + tuple(p.shape[1:]), BF) for p in parts], name=name,
        mesh=plsc.ScalarSubcoreMesh(axis_name="sequencer", num_cores=1),
        scratch_types=[pltpu.SemaphoreType.DMA((3 * n,)), pltpu.SemaphoreType.DMA((3 * n,))],
        compiler_params=pltpu.CompilerParams(collective_id=3),
    )(*parts)


def all_gather_small(v, name):
    R, C = v.shape

    def body(v_ref, out_ref, send_sems, recv_sems, local_sem):
        x, y, c = _me()
        me, sibling = (x, y, c), (x, y, 1 - c)
        chips = _other_chips(x, y)

        def slot(px, py, pc):
            return out_ref.at[4 * px + 2 * py + pc]

        def copy(k, block, to, src=None):
            return pltpu.make_async_remote_copy(
                src_ref=slot(*block) if src is None else src, dst_ref=slot(*block),
                send_sem=send_sems.at[k], recv_sem=recv_sems.at[k],
                device_id=to, device_id_type=MESH)

        mine = pltpu.make_async_copy(v_ref, slot(*me), local_sem)
        mine.start()
        first = [copy(0, me, sibling, src=v_ref)]
        first += [copy(1 + j, me, (*chip, c), src=v_ref) for j, chip in enumerate(chips)]
        for cp in first:
            cp.start()
        passed = [copy(4 + j, (*chip, c), sibling) for j, chip in enumerate(chips)]
        for j, chip in enumerate(chips):
            copy(1 + j, (*chip, c), me).wait_recv()
            passed[j].start()
        copy(0, sibling, me).wait_recv()
        for j, chip in enumerate(chips):
            copy(4 + j, (*chip, 1 - c), me).wait_recv()
        for cp in first + passed:
            cp.wait_send()
        mine.wait()

    return pl.pallas_call(
        body, name=name,
        in_specs=[pl.BlockSpec(memory_space=pltpu.VMEM)],
        out_specs=pl.BlockSpec(memory_space=pltpu.VMEM),
        out_shape=jax.ShapeDtypeStruct((NDEV, R, C), F32),
        scratch_shapes=[pltpu.SemaphoreType.DMA((7,)), pltpu.SemaphoreType.DMA((7,)),
                        pltpu.SemaphoreType.DMA],
    )(v)


def _adamw(w, g, m, v):
    m = ADAM_B1 * m + (1.0 - ADAM_B1) * g
    v = ADAM_B2 * v + (1.0 - ADAM_B2) * (g * g)
    m_hat = m / (1.0 - ADAM_B1 ** ADAM_STEP)
    v_hat = v / (1.0 - ADAM_B2 ** ADAM_STEP)
    delta = -ADAM_LR * (m_hat / (jnp.sqrt(v_hat) + ADAM_EPS) + ADAM_WD * w)
    return delta, m, v


def reduce_adamw(part, landed, w, m, v, layer, q_idx, name, prev=(), deps=()):
    R, C = part.shape[1:]
    r, c = w.shape[1:]
    tr = r // 2 if r % 16 == 0 and r >= 256 else r
    tR = tr if tr != r else R
    extra = tuple(prev) + tuple(deps)

    def body(q_ref, p_ref, l_ref, w_ref, m_ref, v_ref, g_out, d_out, m_out, v_out):
        g = p_ref[...].astype(F32)
        for k in range(3):
            g = g + l_ref[k].astype(F32)
        g = g[:tr, :c]
        d, mm, vv = _adamw(w_ref[...], g, m_ref[...], v_ref[...])
        g_out[...] = g
        d_out[...] = d
        m_out[...] = mm
        v_out[...] = vv

    wspec = pl.BlockSpec((None, tr, c), lambda i, q_ref: (layer, i, 0))
    out = jax.ShapeDtypeStruct(w.shape, F32)
    return pl.pallas_call(
        _after(body, 6, extra), name=name,
        grid_spec=pltpu.PrefetchScalarGridSpec(
            num_scalar_prefetch=1, grid=(r // tr,),
            in_specs=[pl.BlockSpec((None, tR, C), lambda i, q_ref: (q_ref[0], i, 0)),
                      pl.BlockSpec((3, tR, C), lambda i, q_ref: (0, i, 0)),
                      wspec, wspec, wspec] + [ANY] * len(extra),
            out_specs=[wspec] * 4),
        out_shape=[out] * 4,
        input_output_aliases={6 + k: k for k in range(len(prev))},
        compiler_params=_cparams(("parallel",)),
    )(q_idx, part, landed, w, m, v, *extra)


def small_reduce_adamw(gathered, w, m, v, name):
    R, C = w.shape

    def body(a_ref, w_ref, m_ref, v_ref, g_out, d_out, m_out, v_out):
        g = a_ref[0]
        for k in range(1, NDEV):
            g = g + a_ref[k]
        d, mm, vv = _adamw(w_ref[...], g, m_ref[...], v_ref[...])
        g_out[...] = g
        d_out[...] = d
        m_out[...] = mm
        v_out[...] = vv

    out = jax.ShapeDtypeStruct((R, C), F32)
    return pl.pallas_call(body, name=name, out_shape=[out] * 4,
                          compiler_params=_cparams())(gathered, w, m, v)


def _pad_cols(a, n):
    return jnp.pad(a, ((0, 0), (0, n - a.shape[1])))


def _pad_rows(a, n):
    return jnp.pad(a, ((0, n - a.shape[0]), (0, 0)))


SMALL_ROWS = 16


def _pack_small(mix, ffn, fin, taps_full, relb):
    return jnp.concatenate([
        mix, ffn, fin.reshape(1, D), taps_full.reshape(6, D),
        jnp.pad(relb.reshape(1, NUM_BUCKETS * H), ((0, 0), (0, D - NUM_BUCKETS * H)))], axis=0)


def kernel(x, mix_norm, ffn_norm, final_norm, conv_w_in, conv_kernel, conv_w_out, attn_w_qkv, attn_w_out, rel_bias, ffn_w_gate, ffn_w_up, ffn_w_down, loss_target, m_mix_norm, m_ffn_norm, m_final_norm, m_conv_w_in, m_conv_kernel, m_conv_w_out, m_attn_w_qkv, m_attn_w_out, m_rel_bias, m_ffn_w_gate, m_ffn_w_up, m_ffn_w_down, v_mix_norm, v_ffn_norm, v_final_norm, v_conv_w_in, v_conv_kernel, v_conv_w_out, v_attn_w_qkv, v_attn_w_out, v_rel_bias, v_ffn_w_gate, v_ffn_w_up, v_ffn_w_down):
    xi, yi, ci = _me()
    me = 4 * xi + 2 * yi + ci
    c_idx = jnp.reshape(ci, (1,)).astype(jnp.int32)
    q_idx = jnp.reshape(2 * xi + yi, (1,)).astype(jnp.int32)
    col0 = me * (D // NDEV)

    taps_local = jnp.zeros((2, 3, D), F32)
    taps_local = lax.dynamic_update_slice(taps_local, conv_kernel, (0, 0, col0))
    taps_pack = jnp.pad(taps_local.reshape(6, D), ((0, 2), (0, 0)))
    taps_all = all_gather_small(taps_pack, "ag_taps")
    taps_sum = jnp.sum(taps_all, axis=0)
    taps = [jnp.pad(taps_sum[3 * j:3 * j + 3], ((0, 5), (0, 0))) for j in range(2)]

    mixer_in = (conv_w_in, attn_w_qkv)
    mixer_out = (conv_w_out, attn_w_out)
    wts = []
    for i in range(DEPTH):
        j = i // 2
        shards = [mixer_in[i % 2][j].astype(BF), mixer_out[i % 2][j].astype(BF),
                  _pad_cols(ffn_w_gate[i].astype(BF), FF_SHARD_PAD),
                  _pad_cols(ffn_w_up[i].astype(BF), FF_SHARD_PAD),
                  _pad_rows(ffn_w_down[i].astype(BF), FF_SHARD_PAD)]
        wts.append(all_gather_weights(shards, (1, 0, 1, 1, 0), f"ag_weights_l{i}"))

    onehot_t, band = _bucket_onehot_t()
    bias3 = bias_tables(rel_bias.T, onehot_t, band, "bias_tables").reshape(3, H * BLK, 2 * BLK)

    saved = []
    xc = x[0]
    for i in range(DEPTH):
        w_in, w_out, w_g, w_u, w_d = wts[i]
        j = i // 2
        x_mix = xc
        z3, h_mix = norm_matmul3(xc, mix_norm[i:i + 1], w_in, f"mix_in_l{i}")
        if i % 2 == 0:
            act = conv_fwd(z3, taps[j], f"conv_fwd_l{i}")
            lse_b = None
        else:
            act, lse_b = attention_fwd(z3, bias3, f"attn_fwd_l{i}")
        xc = matmul_residual(act, w_out, xc, f"mix_out_l{i}")
        x_ffn = xc
        g, u, a, h_ffn = norm_swiglu_up(xc, ffn_norm[i:i + 1], w_g, w_u, f"ffn_up_l{i}")
        xc = matmul_residual(a, w_d, xc, f"ffn_down_l{i}")
        saved.append((x_mix, h_mix, z3, act, lse_b, x_ffn, h_ffn, g, u, a))

    dx, dxb, dg_final, sq = loss_head(xc, final_norm.reshape(1, D), loss_target[0], "loss_head")
    loss = lax.psum(0.5 * jnp.sum(sq[0]) / D, ("x", "y", "c"))

    w_params = ((conv_w_in, m_conv_w_in, v_conv_w_in), (attn_w_qkv, m_attn_w_qkv, v_attn_w_qkv))
    o_params = ((conv_w_out, m_conv_w_out, v_conv_w_out), (attn_w_out, m_attn_w_out, v_attn_w_out))
    dg_mix = [None] * DEPTH
    dg_ffn = [None] * DEPTH
    dtaps = [None, None]
    dbias_all = []
    shape_in, shape_out = (D, 3 * D // NDEV), (D // NDEV, D)
    shape_up, shape_down = (D, FF_SHARD_PAD), (FF_SHARD_PAD, D)
    stacked = {}

    def pair_stage(grads, landed1, axes, shapes, tag, tok):
        parts = []
        for t in range(len(grads)):
            parts.append(pair_add(grads[t], landed1[t], axes[t], shapes[t], c_idx,
                                  f"rs_add_{tag}_{t}", deps=[tok]))
            tok = parts[-1]
        return parts, chip_exchange_grads(parts, f"rs_chip_{tag}"), tok

    def adamw_stage(parts, landed2, params, tag, tok):
        for t, (pname, w_, m_, v_, layer) in enumerate(params):
            res = reduce_adamw(parts[t], landed2[t], w_, m_, v_, layer, q_idx, f"adamw_{tag}_{t}",
                               prev=stacked.get(pname, ()), deps=[tok])
            stacked[pname] = res
            tok = res[0]
        return tok

    tok = dxb
    mix_wait = None
    mix_chip = None
    ffn_chip = None
    for i in reversed(range(DEPTH)):
        w_in, w_out, w_g, w_u, w_d = wts[i]
        j = i // 2
        x_mix, h_mix, z3, act, lse_b, x_ffn, h_ffn, g, u, a = saved[i]
        ffn_params = [("ffn_w_gate", ffn_w_gate, m_ffn_w_gate, v_ffn_w_gate, i),
                      ("ffn_w_up", ffn_w_up, m_ffn_w_up, v_ffn_w_up, i),
                      ("ffn_w_down", ffn_w_down, m_ffn_w_down, v_ffn_w_down, i)]
        if i % 2 == 0:
            mix_params = [("conv_w_in", conv_w_in, m_conv_w_in, v_conv_w_in, j),
                          ("conv_w_out", conv_w_out, m_conv_w_out, v_conv_w_out, j)]
        else:
            mix_params = [("attn_w_qkv", attn_w_qkv, m_attn_w_qkv, v_attn_w_qkv, j),
                          ("attn_w_out", attn_w_out, m_attn_w_out, v_attn_w_out, j)]
        dgate, dup = swiglu_bwd_da(dxb, w_d, g, u, f"ffn_da_l{i}", deps=[tok])
        gw_d = matmul_tn(a, dxb, f"ffn_dwd_l{i}", deps=[dgate])
        tok = gw_d
        if mix_wait is not None:
            grads_m, landed1_m, params_m, tag_m = mix_wait
            parts_m, landed2_m, tok = pair_stage(grads_m, landed1_m, (1, 0), (shape_in, shape_out),
                                                 tag_m, tok)
            mix_chip = (parts_m, landed2_m, params_m, tag_m)
            mix_wait = None
        gw_g = matmul_tn(h_ffn, dgate, f"ffn_dwg_l{i}", deps=[tok])
        gw_u = matmul_tn(h_ffn, dup, f"ffn_dwu_l{i}", deps=[gw_g])
        grads_f = [gw_g, gw_u, gw_d]
        landed1_f = pair_exchange_grads(grads_f, (1, 1, 0), (shape_up, shape_up, shape_down),
                                        f"rs_pair_f{i}")
        tok = gw_u
        if ffn_chip is not None:
            tok = adamw_stage(*ffn_chip, tok)
            ffn_chip = None
        dx, dxb, dg_ffn[i] = matmul_nt_normbwd(
            [(dgate, w_g, False), (dup, w_u, False)], x_ffn, ffn_norm[i:i + 1], dx, f"ffn_dh_l{i}",
            tk=768, deps=[tok])
        gw_out = matmul_tn(act, dxb, f"mix_dwout_l{i}")
        if i % 2 == 0:
            dact = matmul_nt(dxb, w_out, f"mix_dact_l{i}", out_dtype=F32, deps=[gw_out])
        else:
            dact = matmul_nt(dxb, w_out, f"mix_dact_l{i}", out_dtype=BF, deps=[gw_out])
        parts_f, landed2_f, tok = pair_stage(grads_f, landed1_f, (1, 1, 0),
                                             (shape_up, shape_up, shape_down), f"f{i}", dact)
        ffn_chip = (parts_f, landed2_f, ffn_params, f"f{i}")
        if i % 2 == 0:
            dz3, dtaps[j] = conv_bwd(dact, z3, taps[j], f"conv_bwd_l{i}", deps=[tok])
        else:
            dz3, dbias3 = attention_bwd(z3, dact, act, lse_b, bias3, f"attn_bwd_l{i}", deps=[tok])
            dbias_all.append(dbias3.reshape(3, H, BLK * 2 * BLK))
        gw_in = matmul_tn(h_mix, dz3, f"mix_dwin_l{i}")
        grads_m = [gw_in, gw_out]
        landed1_m = pair_exchange_grads(grads_m, (1, 0), (shape_in, shape_out), f"rs_pair_m{i}")
        mix_wait = (grads_m, landed1_m, mix_params, f"m{i}")
        tok = gw_in
        if mix_chip is not None:
            tok = adamw_stage(*mix_chip, tok)
            mix_chip = None
        dx, dxb, dg_mix[i] = matmul_nt_normbwd(
            [(dz3, w_in, True)], x_mix, mix_norm[i:i + 1], dx, f"mix_dh_l{i}", tk=512, deps=[tok])
        tok = dxb
    grads_m, landed1_m, params_m, tag_m = mix_wait
    parts_m, landed2_m, tok = pair_stage(grads_m, landed1_m, (1, 0), (shape_in, shape_out), tag_m, tok)
    tok = adamw_stage(*ffn_chip, tok)

    grad_relb_t = bias_grad(jnp.concatenate(dbias_all), onehot_t, "bias_grad")
    dtaps_full = jnp.stack([dtaps[0][:3], dtaps[1][:3]])
    g_small = _pack_small(jnp.concatenate([d[0:1] for d in dg_mix], axis=0),
                          jnp.concatenate([d[0:1] for d in dg_ffn], axis=0),
                          dg_final[0], dtaps_full, grad_relb_t.T)
    gathered = all_gather_small(g_small, "ag_small_grads")

    def taps_at_cols(k):
        return lax.dynamic_update_slice(jnp.zeros((2, 3, D), F32), k, (0, 0, col0))

    w_small = _pack_small(mix_norm, ffn_norm, final_norm, taps_at_cols(conv_kernel), rel_bias)
    m_small = _pack_small(m_mix_norm, m_ffn_norm, m_final_norm, taps_at_cols(m_conv_kernel), m_rel_bias)
    v_small = _pack_small(v_mix_norm, v_ffn_norm, v_final_norm, taps_at_cols(v_conv_kernel), v_rel_bias)
    small = small_reduce_adamw(gathered, w_small, m_small, v_small, "adamw_small")

    def unpack_small(p):
        taps_p = lax.dynamic_slice(p[9:15].reshape(2, 3, D), (0, 0, col0), (2, 3, D // NDEV))
        return {"mix_norm": p[0:4], "ffn_norm": p[4:8], "final_norm": p[8],
                "conv_kernel": taps_p, "rel_bias": p[15, :NUM_BUCKETS * H].reshape(NUM_BUCKETS, H)}

    small_out = [unpack_small(p) for p in small]
    adamw_stage(parts_m, landed2_m, params_m, tag_m, small[0])

    names = ["mix_norm", "ffn_norm", "final_norm", "conv_w_in", "conv_kernel", "conv_w_out",
             "attn_w_qkv", "attn_w_out", "rel_bias", "ffn_w_gate", "ffn_w_up", "ffn_w_down"]
    outs = [loss, dx.reshape(1, S, D)]
    for o in range(4):
        for nme in names:
            outs.append(stacked[nme][o] if nme in stacked else small_out[o][nme])
    return tuple(outs)
```

```python
import functools
import math

import jax
import jax.numpy as jnp
from jax import lax
from jax.experimental import pallas as pl
from jax.experimental.pallas import tpu as pltpu
from jax.experimental.pallas import tpu_sc as plsc

S = 2048
D = 1024
H = 16
DH = 64
DFF = 2816
NDEV = 8
DEPTH = 4
FF_SHARD = DFF // NDEV
FF_SHARD_PAD = 384
DFF_PAD = FF_SHARD_PAD * NDEV
BLK = 128
BRANCH_DILATIONS = (1, 4, 16)
NUM_BUCKETS = 32
MAX_DISTANCE = 2048
EPS = 1e-6
NEG_INF = -1e30
SCALE = DH ** -0.5

ADAM_LR = 0.001
ADAM_B1 = 0.9
ADAM_B2 = 0.999
ADAM_EPS = 1e-08
ADAM_WD = 0.01
ADAM_STEP = 10

BF = jnp.bfloat16
F32 = jnp.float32
VMEM_LIMIT_BYTES = 56 * 1024 * 1024
MESH = pl.DeviceIdType.MESH
ANY = pl.BlockSpec(memory_space=pl.ANY)

_NT = (((1,), (1,)), ((), ()))
_TN = (((0,), (0,)), ((), ()))


def _cparams(sem=None):
    return pltpu.CompilerParams(dimension_semantics=sem, vmem_limit_bytes=VMEM_LIMIT_BYTES)


def _after(body, n, deps):
    nd = len(deps)
    if nd == 0:
        return body

    def ordered(*refs):
        body(*refs[:n], *refs[n + nd:])
    return ordered


def _rms(x):
    return lax.rsqrt(jnp.mean(x * x, axis=-1, keepdims=True) + EPS)


def norm_matmul3(x, gain, w, name, tm=1024, tn=512):
    per = D // tn

    def body(x_ref, g_ref, w_ref, z_ref, h_ref, hs_ref):
        @pl.when(pl.program_id(1) == 0)
        def _():
            xv = x_ref[...]
            hv = (xv * _rms(xv) * g_ref[...]).astype(BF)
            hs_ref[...] = hv
            h_ref[...] = hv
        z_ref[...] = jnp.dot(hs_ref[...], w_ref[...], preferred_element_type=F32).astype(BF)

    return pl.pallas_call(
        body, name=name,
        grid=(S // tm, 3 * D // tn),
        in_specs=[pl.BlockSpec((tm, D), lambda i, j: (i, 0)),
                  pl.BlockSpec((1, D), lambda i, j: (0, 0)),
                  pl.BlockSpec((D, tn), lambda i, j: (0, j))],
        out_specs=[pl.BlockSpec((None, tm, tn), lambda i, j: (j // per, i, j % per)),
                   pl.BlockSpec((tm, D), lambda i, j: (i, 0))],
        out_shape=[jax.ShapeDtypeStruct((3, S, D), BF), jax.ShapeDtypeStruct((S, D), BF)],
        scratch_shapes=[pltpu.VMEM((tm, D), BF)],
        compiler_params=_cparams(("parallel", "arbitrary")),
    )(x, gain, w)


def norm_swiglu_up(x, gain, wg, wu, name, tm=1024, tn=512):
    def body(x_ref, g_ref, wg_ref, wu_ref, go_ref, uo_ref, ao_ref, h_ref, hs_ref):
        @pl.when(pl.program_id(1) == 0)
        def _():
            xv = x_ref[...]
            hv = (xv * _rms(xv) * g_ref[...]).astype(BF)
            hs_ref[...] = hv
            h_ref[...] = hv
        hv = hs_ref[...]
        g = jnp.dot(hv, wg_ref[...], preferred_element_type=F32)
        u = jnp.dot(hv, wu_ref[...], preferred_element_type=F32)
        go_ref[...] = g.astype(BF)
        uo_ref[...] = u.astype(BF)
        ao_ref[...] = (g * jax.nn.sigmoid(g) * u).astype(BF)

    act = jax.ShapeDtypeStruct((S, DFF_PAD), BF)
    blk = pl.BlockSpec((tm, tn), lambda i, j: (i, j))
    return pl.pallas_call(
        body, name=name,
        grid=(S // tm, DFF_PAD // tn),
        in_specs=[pl.BlockSpec((tm, D), lambda i, j: (i, 0)),
                  pl.BlockSpec((1, D), lambda i, j: (0, 0)),
                  pl.BlockSpec((D, tn), lambda i, j: (0, j)),
                  pl.BlockSpec((D, tn), lambda i, j: (0, j))],
        out_specs=[blk, blk, blk, pl.BlockSpec((tm, D), lambda i, j: (i, 0))],
        out_shape=[act, act, act, jax.ShapeDtypeStruct((S, D), BF)],
        scratch_shapes=[pltpu.VMEM((tm, D), BF)],
        compiler_params=_cparams(("parallel", "arbitrary")),
    )(x, gain, wg, wu)


def matmul_residual(a, w, x, name, tm=1024, tn=512):
    K = a.shape[1]

    def body(a_ref, w_ref, x_ref, o_ref):
        o_ref[...] = x_ref[...] + jnp.dot(a_ref[...], w_ref[...], preferred_element_type=F32)

    return pl.pallas_call(
        body, name=name,
        grid=(S // tm, D // tn),
        in_specs=[pl.BlockSpec((tm, K), lambda i, j: (i, 0)),
                  pl.BlockSpec((K, tn), lambda i, j: (0, j)),
                  pl.BlockSpec((tm, tn), lambda i, j: (i, j))],
        out_specs=pl.BlockSpec((tm, tn), lambda i, j: (i, j)),
        out_shape=jax.ShapeDtypeStruct((S, D), F32),
        compiler_params=_cparams(("parallel", "parallel")),
    )(a, w, x)


def matmul_nt(a, w, name, out_dtype=BF, tm=1024, tn=512, deps=()):
    K = a.shape[1]
    N = w.shape[0]

    def body(a_ref, w_ref, o_ref):
        o_ref[...] = lax.dot_general(a_ref[...], w_ref[...], _NT,
                                     preferred_element_type=F32).astype(o_ref.dtype)

    return pl.pallas_call(
        _after(body, 2, deps), name=name,
        grid=(S // tm, N // tn),
        in_specs=[pl.BlockSpec((tm, K), lambda i, j: (i, 0)),
                  pl.BlockSpec((tn, K), lambda i, j: (j, 0))] + [ANY] * len(deps),
        out_specs=pl.BlockSpec((tm, tn), lambda i, j: (i, j)),
        out_shape=jax.ShapeDtypeStruct((S, N), out_dtype),
        compiler_params=_cparams(("parallel", "parallel")),
    )(a, w, *deps)


def swiglu_bwd_da(dxb, wd, g, u, name, tm=1024, tn=512, deps=()):
    def body(dx_ref, w_ref, g_ref, u_ref, dg_ref, du_ref):
        da = lax.dot_general(dx_ref[...], w_ref[...], _NT, preferred_element_type=F32)
        gv = g_ref[...].astype(F32)
        uv = u_ref[...].astype(F32)
        sig = jax.nn.sigmoid(gv)
        dg_ref[...] = (da * uv * (sig * (1.0 + gv * (1.0 - sig)))).astype(BF)
        du_ref[...] = (da * (gv * sig)).astype(BF)

    act = jax.ShapeDtypeStruct((S, DFF_PAD), BF)
    blk = pl.BlockSpec((tm, tn), lambda i, j: (i, j))
    return pl.pallas_call(
        _after(body, 4, deps), name=name,
        grid=(S // tm, DFF_PAD // tn),
        in_specs=[pl.BlockSpec((tm, D), lambda i, j: (i, 0)),
                  pl.BlockSpec((tn, D), lambda i, j: (j, 0)),
                  blk, blk] + [ANY] * len(deps),
        out_specs=[blk, blk],
        out_shape=[act, act],
        compiler_params=_cparams(("parallel", "parallel")),
    )(dxb, wd, g, u, *deps)


def matmul_tn(a, b, name, tm=1024, tn=512, deps=()):
    M = a.shape[1]
    if b.ndim == 3:
        per = D // tn
        N = 3 * D
        b_spec = pl.BlockSpec((None, S, tn), lambda i, j: (j // per, 0, j % per))
    else:
        N = b.shape[1]
        b_spec = pl.BlockSpec((S, tn), lambda i, j: (0, j))

    def body(a_ref, b_ref, o_ref):
        o_ref[...] = lax.dot_general(a_ref[...], b_ref[...], _TN,
                                     preferred_element_type=F32).astype(BF)

    return pl.pallas_call(
        _after(body, 2, deps), name=name,
        grid=(M // tm, N // tn),
        in_specs=[pl.BlockSpec((S, tm), lambda i, j: (0, i)), b_spec] + [ANY] * len(deps),
        out_specs=pl.BlockSpec((tm, tn), lambda i, j: (i, j)),
        out_shape=jax.ShapeDtypeStruct((M, N), BF),
        compiler_params=_cparams(("parallel", "parallel")),
    )(a, b, *deps)


def matmul_nt_normbwd(terms, x_in, gain, dx, name, tk, tm=1024, ch=256, deps=()):
    specs, operands, ranges = [], [], []
    start = 0
    for (a, w, stacked) in terms:
        K = w.shape[1]
        n = K // tk
        lo = start

        def rel(k, lo=lo, n=n):
            return jnp.clip(k - lo, 0, n - 1)

        if stacked:
            per = D // tk
            specs.append(pl.BlockSpec((None, tm, tk),
                                      lambda i, k, rel=rel, per=per: (rel(k) // per, i, rel(k) % per)))
        else:
            specs.append(pl.BlockSpec((tm, tk), lambda i, k, rel=rel: (i, rel(k))))
        specs.append(pl.BlockSpec((D, tk), lambda i, k, rel=rel: (0, rel(k))))
        operands += [a, w]
        ranges.append((lo, lo + n))
        start += n
    nk = start
    nt = len(terms)

    def body(*refs):
        aw = refs[:2 * nt]
        x_ref, g_ref, dx_ref, dxo_ref, dxb_ref, dg_ref, acc_ref = refs[2 * nt:]
        i = pl.program_id(0)
        k = pl.program_id(1)

        @pl.when(k == 0)
        def _():
            acc_ref[...] = jnp.zeros_like(acc_ref)

        @pl.when((i == 0) & (k == 0))
        def _():
            dg_ref[...] = jnp.zeros_like(dg_ref)

        for t in range(nt):
            lo, hi = ranges[t]

            @pl.when((k >= lo) & (k < hi))
            def _(t=t):
                acc_ref[...] += lax.dot_general(aw[2 * t][...], aw[2 * t + 1][...], _NT,
                                                preferred_element_type=F32)

        @pl.when(k == nk - 1)
        def _():
            def chunk(c, carry):
                rows = pl.ds(pl.multiple_of(c * ch, ch), ch)
                xv = x_ref[rows, :]
                r = _rms(xv)
                xhat = xv * r
                dh = acc_ref[rows, :]
                dg_ref[0:1, :] += jnp.sum(dh * xhat, axis=0, keepdims=True)
                dxh = dh * g_ref[...]
                dxn = r * (dxh - xhat * jnp.mean(dxh * xhat, axis=-1, keepdims=True))
                out = dx_ref[rows, :] + dxn
                dxo_ref[rows, :] = out
                dxb_ref[rows, :] = out.astype(BF)
                return carry
            lax.fori_loop(0, tm // ch, chunk, 0)

    row = pl.BlockSpec((tm, D), lambda i, k: (i, 0))
    return pl.pallas_call(
        _after(body, 2 * nt + 3, deps), name=name,
        grid=(S // tm, nk),
        in_specs=specs + [row, pl.BlockSpec((1, D), lambda i, k: (0, 0)), row] + [ANY] * len(deps),
        out_specs=[row, row, pl.BlockSpec((8, D), lambda i, k: (0, 0))],
        out_shape=[jax.ShapeDtypeStruct((S, D), F32), jax.ShapeDtypeStruct((S, D), BF),
                   jax.ShapeDtypeStruct((8, D), F32)],
        scratch_shapes=[pltpu.VMEM((tm, D), F32)],
        compiler_params=_cparams(("arbitrary", "arbitrary")),
    )(*operands, x_in, gain, dx, *deps)


def loss_head(x, gain, target, name, tm=512):
    def body(x_ref, g_ref, t_ref, dxo_ref, dxb_ref, dg_ref, sq_ref):
        @pl.when(pl.program_id(0) == 0)
        def _():
            dg_ref[...] = jnp.zeros_like(dg_ref)
            sq_ref[...] = jnp.zeros_like(sq_ref)
        xv = x_ref[...]
        r = _rms(xv)
        xhat = xv * r
        err = xhat * g_ref[...] - t_ref[...]
        sq_ref[0:1, :] += jnp.sum(err * err, axis=0, keepdims=True)
        dy = err * (1.0 / D)
        dg_ref[0:1, :] += jnp.sum(dy * xhat, axis=0, keepdims=True)
        dxh = dy * g_ref[...]
        out = r * (dxh - xhat * jnp.mean(dxh * xhat, axis=-1, keepdims=True))
        dxo_ref[...] = out
        dxb_ref[...] = out.astype(BF)

    row = pl.BlockSpec((tm, D), lambda i: (i, 0))
    acc = pl.BlockSpec((8, D), lambda i: (0, 0))
    return pl.pallas_call(
        body, name=name,
        grid=(S // tm,),
        in_specs=[row, pl.BlockSpec((1, D), lambda i: (0, 0)), row],
        out_specs=[row, row, acc, acc],
        out_shape=[jax.ShapeDtypeStruct((S, D), F32), jax.ShapeDtypeStruct((S, D), BF),
                   jax.ShapeDtypeStruct((8, D), F32), jax.ShapeDtypeStruct((8, D), F32)],
        compiler_params=_cparams(("arbitrary",)),
    )(x, gain, target)


def _shift_down(p, n, row):
    return jnp.where(row >= n, pltpu.roll(p, n, axis=0), 0.0)


def _shift_up(p, n, row):
    return jnp.where(row < S - n, pltpu.roll(p, S - n, axis=0), 0.0)


def conv_fwd(z3, taps, name, tn=128):
    def body(z_ref, k_ref, m_ref):
        b = z_ref[0].astype(F32)
        p = z_ref[1].astype(F32) * z_ref[2].astype(F32)
        row = lax.broadcasted_iota(jnp.int32, p.shape, 0)
        y = (k_ref[2:3, :] * p + k_ref[1:2, :] * _shift_down(p, 1, row)
             + k_ref[0:1, :] * _shift_down(p, 2, row))
        m_ref[...] = (b * y).astype(BF)

    return pl.pallas_call(
        body, name=name,
        grid=(D // tn,),
        in_specs=[pl.BlockSpec((3, S, tn), lambda j: (0, 0, j)),
                  pl.BlockSpec((8, tn), lambda j: (0, j))],
        out_specs=pl.BlockSpec((S, tn), lambda j: (0, j)),
        out_shape=jax.ShapeDtypeStruct((S, D), BF),
        compiler_params=_cparams(("parallel",)),
    )(z3, taps)


def conv_bwd(dm, z3, taps, name, tn=128, deps=()):
    def body(dm_ref, z_ref, k_ref, dz_ref, dk_ref):
        dmv = dm_ref[...]
        b = z_ref[0].astype(F32)
        c = z_ref[1].astype(F32)
        u = z_ref[2].astype(F32)
        p = c * u
        row = lax.broadcasted_iota(jnp.int32, p.shape, 0)
        p1 = _shift_down(p, 1, row)
        p2 = _shift_down(p, 2, row)
        y = k_ref[2:3, :] * p + k_ref[1:2, :] * p1 + k_ref[0:1, :] * p2
        dy = dmv * b
        dz_ref[0] = (dmv * y).astype(BF)
        dp = (k_ref[2:3, :] * dy + k_ref[1:2, :] * _shift_up(dy, 1, row)
              + k_ref[0:1, :] * _shift_up(dy, 2, row))
        dz_ref[1] = (dp * u).astype(BF)
        dz_ref[2] = (dp * c).astype(BF)
        dk_ref[...] = jnp.zeros_like(dk_ref)
        dk_ref[0:1, :] = jnp.sum(dy * p2, axis=0, keepdims=True)
        dk_ref[1:2, :] = jnp.sum(dy * p1, axis=0, keepdims=True)
        dk_ref[2:3, :] = jnp.sum(dy * p, axis=0, keepdims=True)

    return pl.pallas_call(
        _after(body, 3, deps), name=name,
        grid=(D // tn,),
        in_specs=[pl.BlockSpec((S, tn), lambda j: (0, j)),
                  pl.BlockSpec((3, S, tn), lambda j: (0, 0, j)),
                  pl.BlockSpec((8, tn), lambda j: (0, j))] + [ANY] * len(deps),
        out_specs=[pl.BlockSpec((3, S, tn), lambda j: (0, 0, j)),
                   pl.BlockSpec((8, tn), lambda j: (0, j))],
        out_shape=[jax.ShapeDtypeStruct((3, S, D), BF), jax.ShapeDtypeStruct((8, D), F32)],
        compiler_params=_cparams(("parallel",)),
    )(dm, z3, taps, *deps)


def _t5_bucket(dist):
    exact = NUM_BUCKETS // 2
    df = jnp.maximum(dist, 1).astype(jnp.float32)
    large = exact + (jnp.log(df / exact) / math.log(MAX_DISTANCE / exact)
                     * (NUM_BUCKETS - exact)).astype(jnp.int32)
    large = jnp.minimum(large, NUM_BUCKETS - 1)
    return jnp.where(dist < exact, dist, large)


def _bucket_onehot_t():
    qi = jnp.arange(BLK)[:, None]
    ki = jnp.arange(2 * BLK)[None, :]
    rel = qi + BLK - ki
    band = ((rel >= 0) & (rel <= BLK)).reshape(1, -1).astype(F32)
    hots = []
    for d in BRANCH_DILATIONS:
        bucket = _t5_bucket(jnp.clip(rel, 0) * d).reshape(1, -1)
        hots.append((jnp.arange(NUM_BUCKETS)[:, None] == bucket).astype(F32))
    return jnp.stack(hots), band


def bias_tables(rel_bias_t, onehot_t, band, name):
    def body(rb_ref, oh_ref, band_ref, o_ref):
        b = jnp.dot(rb_ref[...], oh_ref[...], preferred_element_type=F32,
                    precision=lax.Precision.HIGHEST)
        o_ref[...] = jnp.where(band_ref[...] > 0.5, b, NEG_INF)

    n = BLK * 2 * BLK
    return pl.pallas_call(
        body, name=name,
        grid=(3,),
        in_specs=[pl.BlockSpec((H, NUM_BUCKETS), lambda g: (0, 0)),
                  pl.BlockSpec((None, NUM_BUCKETS, n), lambda g: (g, 0, 0)),
                  pl.BlockSpec((1, n), lambda g: (0, 0))],
        out_specs=pl.BlockSpec((None, H, n), lambda g: (g, 0, 0)),
        out_shape=jax.ShapeDtypeStruct((3, H, n), F32),
        compiler_params=_cparams(("parallel",)),
    )(rel_bias_t, onehot_t, band)


def bias_grad(dbias, onehot_t, name):
    def body(db_ref, oh_ref, o_ref):
        @pl.when(pl.program_id(0) == 0)
        def _():
            o_ref[...] = jnp.zeros_like(o_ref)
        o_ref[...] += lax.dot_general(db_ref[...], oh_ref[...], _NT, preferred_element_type=F32,
                                      precision=lax.Precision.HIGHEST)

    n = BLK * 2 * BLK
    return pl.pallas_call(
        body, name=name,
        grid=(dbias.shape[0],),
        in_specs=[pl.BlockSpec((None, H, n), lambda g: (g, 0, 0)),
                  pl.BlockSpec((None, NUM_BUCKETS, n), lambda g: (g % 3, 0, 0))],
        out_specs=pl.BlockSpec((H, NUM_BUCKETS), lambda g: (0, 0)),
        out_shape=jax.ShapeDtypeStruct((H, NUM_BUCKETS), F32),
        compiler_params=_cparams(("arbitrary",)),
    )(dbias, onehot_t)


def _head_masks():
    lane = lax.broadcasted_iota(jnp.int32, (1, 2 * DH), 1)
    return (lane < DH, lane >= DH)


def _stack_heads(x, masks):
    zero = jnp.zeros_like(x)
    return jnp.concatenate([jnp.where(masks[0], x, zero), jnp.where(masks[1], x, zero)], axis=0)


def _deinterleave(src_ref, dst_ref, d, dtype):
    L = S // d
    for r in range(d):
        dst_ref[r * L:(r + 1) * L, :] = src_ref[pl.ds(r, L, stride=d), :].astype(dtype)


def _branch_loops(d, block):
    L = S // d
    nb = L // BLK

    def first(base):
        block(base, base, BLK, True)

    def later(base, n):
        q0 = pl.multiple_of(base + n * BLK, BLK)
        block(q0, pl.multiple_of(q0 - BLK, BLK), 2 * BLK, False)

    if d == 1:
        unroll = 3
        assert (nb - 1) % unroll == 0
        first(0)

        def trip(it, c):
            for u in range(unroll):
                later(0, 1 + it * unroll + u)
            return c
        lax.fori_loop(0, (nb - 1) // unroll, trip, 0)
    elif nb > 1:
        def residue(r, c):
            base = pl.multiple_of(r * L, BLK)
            first(base)
            for n in range(1, nb):
                later(base, n)
            return c
        lax.fori_loop(0, d, residue, 0)
    else:
        unroll = 4
        assert d % unroll == 0

        def trip(it, c):
            for u in range(unroll):
                first(pl.multiple_of((it * unroll + u) * L, BLK))
            return c
        lax.fori_loop(0, d // unroll, trip, 0)


def attention_fwd(z3, bias3, name):
    W = 2 * DH
    CH = 256

    def body(q_ref, k_ref, v_ref, b_ref, o_ref, lse_ref, stage, qd, kd, vd, od, ld, on, ln):
        masks = _head_masks()
        for src, dst in ((q_ref, qd), (k_ref, kd), (v_ref, vd)):
            stage[...] = src[...].astype(F32)
            for gi, d in enumerate(BRANCH_DILATIONS[1:]):
                _deinterleave(stage, dst.at[gi], d, BF)

        for g, d in enumerate(BRANCH_DILATIONS):
            qs, ks, vs = (q_ref, k_ref, v_ref) if d == 1 else (qd.at[g - 1], kd.at[g - 1], vd.at[g - 1])
            o_dst, l_dst = (on.at[0], ln.at[0]) if d == 1 else (od, ld)

            def block(q0, k0, nk, first, g=g, qs=qs, ks=ks, vs=vs, o_dst=o_dst, l_dst=l_dst):
                q2 = _stack_heads(qs[pl.ds(q0, BLK), :], masks)
                kk = ks[pl.ds(k0, nk), :]
                vv = vs[pl.ds(k0, nk), :]
                bias = b_ref[g][:, BLK:] if first else b_ref[g]
                s = lax.dot_general(q2, kk, _NT, preferred_element_type=F32) * SCALE + bias
                mx = jnp.max(s, axis=1, keepdims=True)
                p = jnp.exp(s - mx)
                l = jnp.sum(p, axis=1, keepdims=True)
                o2 = jnp.dot(p.astype(BF), vv, preferred_element_type=F32) / l
                lse2 = mx + jnp.log(l)
                o_dst[pl.ds(q0, BLK), :] = jnp.where(masks[0], o2[:BLK], o2[BLK:])
                l_dst[pl.ds(q0, BLK), :] = jnp.where(masks[0], lse2[:BLK], lse2[BLK:])

            _branch_loops(d, block)
            if d > 1:
                L = S // d
                for r in range(d):
                    on[g, pl.ds(r, L, stride=d), :] = od[r * L:(r + 1) * L, :]
                    ln[g, pl.ds(r, L, stride=d), :] = ld[r * L:(r + 1) * L, :]

        def join(c, carry):
            rows = pl.ds(pl.multiple_of(c * CH, CH), CH)
            a, b, cc = ln[0, rows, :], ln[1, rows, :], ln[2, rows, :]
            mx = jnp.maximum(jnp.maximum(a, b), cc)
            ea, eb, ec = jnp.exp(a - mx), jnp.exp(b - mx), jnp.exp(cc - mx)
            tot = ea + eb + ec
            o_ref[rows, :] = ((ea * on[0, rows, :] + eb * on[1, rows, :] + ec * on[2, rows, :])
                              / tot).astype(BF)
            lse_ref[rows, :] = mx + jnp.log(tot)
            return carry
        lax.fori_loop(0, S // CH, join, 0)

    col = pl.BlockSpec((S, W), lambda hp: (0, hp))
    return pl.pallas_call(
        body, name=name,
        grid=(D // W,),
        in_specs=[pl.BlockSpec((None, S, W), lambda hp: (0, 0, hp)),
                  pl.BlockSpec((None, S, W), lambda hp: (1, 0, hp)),
                  pl.BlockSpec((None, S, W), lambda hp: (2, 0, hp)),
                  pl.BlockSpec((3, 2 * BLK, 2 * BLK), lambda hp: (0, hp, 0))],
        out_specs=[col, col],
        out_shape=[jax.ShapeDtypeStruct((S, D), BF), jax.ShapeDtypeStruct((S, D), F32)],
        scratch_shapes=[pltpu.VMEM((S, W), F32),
                        pltpu.VMEM((2, S, W), BF), pltpu.VMEM((2, S, W), BF), pltpu.VMEM((2, S, W), BF),
                        pltpu.VMEM((S, W), F32), pltpu.VMEM((S, W), F32),
                        pltpu.VMEM((3, S, W), F32), pltpu.VMEM((3, S, W), F32)],
        compiler_params=_cparams(("parallel",)),
    )(z3, z3, z3, bias3)


def attention_bwd(z3, dob, ob, lse_b, bias3, name, deps=()):
    W = 2 * DH
    CH = 256

    def body(q_ref, k_ref, v_ref, do_ref, o_ref, lse_ref, b_ref, dz_ref, db_ref,
             stage, delta, qd, kd, vd, dod, lsd, dld, res, acc):
        masks = _head_masks()

        def rowsum(c, carry):
            rows = pl.ds(pl.multiple_of(c * CH, CH), CH)
            prod = do_ref[rows, :].astype(F32) * o_ref[rows, :].astype(F32)
            sa = jnp.sum(jnp.where(masks[0], prod, 0.0), axis=1, keepdims=True)
            sb = jnp.sum(jnp.where(masks[1], prod, 0.0), axis=1, keepdims=True)
            delta[rows, :] = jnp.where(masks[0], sa, sb)
            return carry
        lax.fori_loop(0, S // CH, rowsum, 0)

        for src, dst in ((q_ref, qd), (k_ref, kd), (v_ref, vd), (do_ref, dod)):
            stage[...] = src[...].astype(F32)
            for gi, d in enumerate(BRANCH_DILATIONS[1:]):
                _deinterleave(stage, dst.at[gi], d, BF)
        for gi, d in enumerate(BRANCH_DILATIONS[1:]):
            _deinterleave(lse_ref, lsd.at[gi], d, F32)
            _deinterleave(delta, dld.at[gi], d, F32)

        db_ref[...] = jnp.zeros_like(db_ref)
        for g, d in enumerate(BRANCH_DILATIONS):
            if d == 1:
                qs, ks, vs, dos, ls, dl = q_ref, k_ref, v_ref, do_ref, lse_ref, delta
            else:
                qs, ks, vs, dos = qd.at[g - 1], kd.at[g - 1], vd.at[g - 1], dod.at[g - 1]
                ls, dl = lsd.at[g - 1], dld.at[g - 1]
            res[1] = jnp.zeros((S, W), F32)
            res[2] = jnp.zeros((S, W), F32)

            def block(q0, k0, nk, first, g=g, qs=qs, ks=ks, vs=vs, dos=dos, ls=ls, dl=dl):
                kk = ks[pl.ds(k0, nk), :]
                vv = vs[pl.ds(k0, nk), :]
                q2 = _stack_heads(qs[pl.ds(q0, BLK), :], masks)
                do2 = _stack_heads(dos[pl.ds(q0, BLK), :], masks)
                lse_blk = ls[pl.ds(q0, BLK), :]
                del_blk = dl[pl.ds(q0, BLK), :]
                lse2 = jnp.concatenate([lse_blk[:, 0:1], lse_blk[:, DH:DH + 1]], axis=0)
                del2 = jnp.concatenate([del_blk[:, 0:1], del_blk[:, DH:DH + 1]], axis=0)
                bias = b_ref[g][:, BLK:] if first else b_ref[g]
                s = lax.dot_general(q2, kk, _NT, preferred_element_type=F32) * SCALE + bias
                p = jnp.exp(s - lse2)
                dp = lax.dot_general(do2, vv, _NT, preferred_element_type=F32)
                ds = p * (dp - del2)
                if first:
                    db_ref[g, :, BLK:] += ds
                else:
                    db_ref[g] += ds
                dsb = ds.astype(BF)
                dq2 = jnp.dot(dsb, kk, preferred_element_type=F32) * SCALE
                res[0, pl.ds(q0, BLK), :] = jnp.where(masks[0], dq2[:BLK], dq2[BLK:])
                res[1, pl.ds(k0, nk), :] += lax.dot_general(dsb, q2, _TN,
                                                            preferred_element_type=F32) * SCALE
                res[2, pl.ds(k0, nk), :] += lax.dot_general(p.astype(BF), do2, _TN,
                                                            preferred_element_type=F32)

            _branch_loops(d, block)
            L = S // d
            for t in range(3):
                if d == 1:
                    acc[t] = res[t]
                else:
                    for r in range(d):
                        acc[t, pl.ds(r, L, stride=d), :] = (acc[t, pl.ds(r, L, stride=d), :]
                                                            + res[t, r * L:(r + 1) * L, :])
        for t in range(3):
            dz_ref[t] = acc[t].astype(BF)

    col = pl.BlockSpec((S, W), lambda hp: (0, hp))
    bspec = pl.BlockSpec((3, 2 * BLK, 2 * BLK), lambda hp: (0, hp, 0))
    return pl.pallas_call(
        _after(body, 7, deps), name=name,
        grid=(D // W,),
        in_specs=[pl.BlockSpec((None, S, W), lambda hp: (0, 0, hp)),
                  pl.BlockSpec((None, S, W), lambda hp: (1, 0, hp)),
                  pl.BlockSpec((None, S, W), lambda hp: (2, 0, hp)),
                  col, col, col, bspec] + [ANY] * len(deps),
        out_specs=[pl.BlockSpec((3, S, W), lambda hp: (0, 0, hp)), bspec],
        out_shape=[jax.ShapeDtypeStruct((3, S, D), BF),
                   jax.ShapeDtypeStruct((3, H * BLK, 2 * BLK), F32)],
        scratch_shapes=[pltpu.VMEM((S, W), F32), pltpu.VMEM((S, W), F32),
                        pltpu.VMEM((2, S, W), BF), pltpu.VMEM((2, S, W), BF),
                        pltpu.VMEM((2, S, W), BF), pltpu.VMEM((2, S, W), BF),
                        pltpu.VMEM((2, S, W), F32), pltpu.VMEM((2, S, W), F32),
                        pltpu.VMEM((3, S, W), F32), pltpu.VMEM((3, S, W), F32)],
        compiler_params=_cparams(("parallel",)),
    )(z3, z3, z3, dob, ob, lse_b, bias3, *deps)


def _me():
    return lax.axis_index("x"), lax.axis_index("y"), lax.axis_index("c")


def _other_chips(x, y):
    return [(1 - x, y), (x, 1 - y), (1 - x, 1 - y)]


def _shard_window(ref, axis, t, shape):
    R, C = shape
    if axis == 0:
        return ref.at[pl.ds(pl.multiple_of(t * R, 128), R), :]
    return ref.at[:, pl.ds(pl.multiple_of(t * C, 128), C)]


def all_gather_weights(shards, axes, name):
    n = len(shards)
    shapes = [s.shape for s in shards]
    outs_shape = [jax.ShapeDtypeStruct((8 * R, C) if ax == 0 else (R, 8 * C), BF)
                  for (R, C), ax in zip(shapes, axes)]

    def body(*refs):
        ins, outs = refs[:n], refs[n:2 * n]
        send_sems, recv_sems, local_sems = refs[2 * n:]
        x, y, c = _me()
        sibling = (x, y, 1 - c)
        chips = _other_chips(x, y)
        barrier = pltpu.get_barrier_semaphore()
        for peer in [sibling] + [(*chip, c) for chip in chips]:
            pl.semaphore_signal(barrier, inc=1, device_id=peer, device_id_type=MESH)
        pl.semaphore_wait(barrier, 4)

        def win(i, px, py, pc):
            return _shard_window(outs[i], axes[i], 4 * px + 2 * py + pc, shapes[i])

        def copy(i, k, block, to, src=None):
            return pltpu.make_async_remote_copy(
                src_ref=win(i, *block) if src is None else src, dst_ref=win(i, *block),
                send_sem=send_sems.at[i * 7 + k], recv_sem=recv_sems.at[i * 7 + k],
                device_id=to, device_id_type=MESH)

        mine = [pltpu.make_async_copy(ins[i], win(i, x, y, c), local_sems.at[i]) for i in range(n)]
        for cp in mine:
            cp.start()
        first = []
        for i in range(n):
            first.append(copy(i, 0, (x, y, c), sibling, src=ins[i]))
            for j, chip in enumerate(chips):
                first.append(copy(i, 1 + j, (x, y, c), (*chip, c), src=ins[i]))
        for cp in first:
            cp.start()
        passed = []
        for j, chip in enumerate(chips):
            for i in range(n):
                copy(i, 1 + j, (*chip, c), (x, y, c)).wait_recv()
                cp = copy(i, 4 + j, (*chip, c), sibling)
                cp.start()
                passed.append(cp)
        for i in range(n):
            copy(i, 0, sibling, (x, y, c)).wait_recv()
        for j, chip in enumerate(chips):
            for i in range(n):
                copy(i, 4 + j, (*chip, 1 - c), (x, y, c)).wait_recv()
        for cp in first + passed:
            cp.wait_send()
        for cp in mine:
            cp.wait()

    return pl.kernel(
        body, out_type=outs_shape, name=name,
        mesh=plsc.ScalarSubcoreMesh(axis_name="sequencer", num_cores=1),
        scratch_types=[pltpu.SemaphoreType.DMA((7 * n,)), pltpu.SemaphoreType.DMA((7 * n,)),
                       pltpu.SemaphoreType.DMA((n,))],
        compiler_params=pltpu.CompilerParams(collective_id=1),
    )(*shards)


def pair_exchange_grads(grads, axes, shapes, name):
    n = len(grads)

    def body(*refs):
        ins, outs = refs[:n], refs[n:2 * n]
        send_sems, recv_sems = refs[2 * n:]
        x, y, c = _me()
        sibling = (x, y, 1 - c)
        barrier = pltpu.get_barrier_semaphore()
        pl.semaphore_signal(barrier, inc=1, device_id=sibling, device_id_type=MESH)
        pl.semaphore_wait(barrier, 1)
        copies = []
        for i in range(n):
            for q in range(4):
                t = 2 * q + (1 - c)
                copies.append(pltpu.make_async_remote_copy(
                    src_ref=_shard_window(ins[i], axes[i], t, shapes[i]), dst_ref=outs[i].at[q],
                    send_sem=send_sems.at[i * 4 + q], recv_sem=recv_sems.at[i * 4 + q],
                    device_id=sibling, device_id_type=MESH))
        for cp in copies:
            cp.start()
        for cp in copies:
            cp.wait_recv()
        for cp in copies:
            cp.wait_send()

    return pl.kernel(
        body, out_type=[jax.ShapeDtypeStruct((4,) + tuple(sh), BF) for sh in shapes], name=name,
        mesh=plsc.ScalarSubcoreMesh(axis_name="sequencer", num_cores=1),
        scratch_types=[pltpu.SemaphoreType.DMA((4 * n,)), pltpu.SemaphoreType.DMA((4 * n,))],
        compiler_params=pltpu.CompilerParams(collective_id=2),
    )(*grads)


def pair_add(grad, landed, axis, shape, c_idx, name, deps=()):
    R, C = shape

    def body(c_ref, g_ref, l_ref, o_ref):
        o_ref[...] = (g_ref[...].astype(F32) + l_ref[...].astype(F32)).astype(BF)

    if axis == 0:
        g_spec = pl.BlockSpec((R, C), lambda q, c_ref: (2 * q + c_ref[0], 0))
    else:
        g_spec = pl.BlockSpec((R, C), lambda q, c_ref: (0, 2 * q + c_ref[0]))
    blk = pl.BlockSpec((None, R, C), lambda q, c_ref: (q, 0, 0))
    return pl.pallas_call(
        _after(body, 3, deps), name=name,
        grid_spec=pltpu.PrefetchScalarGridSpec(
            num_scalar_prefetch=1, grid=(4,), in_specs=[g_spec, blk] + [ANY] * len(deps),
            out_specs=blk),
        out_shape=jax.ShapeDtypeStruct((4, R, C), BF),
        compiler_params=_cparams(("parallel",)),
    )(c_idx, grad, landed, *deps)


def chip_exchange_grads(parts, name):
    n = len(parts)

    def body(*refs):
        ins, outs = refs[:n], refs[n:2 * n]
        send_sems, recv_sems = refs[2 * n:]
        x, y, c = _me()
        barrier = pltpu.get_barrier_semaphore()
        for px, py in _other_chips(x, y):
            pl.semaphore_signal(barrier, inc=1, device_id=(px, py, c), device_id_type=MESH)
        pl.semaphore_wait(barrier, 3)
        copies = []
        for i in range(n):
            for k, (px, py) in enumerate(_other_chips(x, y)):
                copies.append(pltpu.make_async_remote_copy(
                    src_ref=ins[i].at[2 * px + py], dst_ref=outs[i].at[k],
                    send_sem=send_sems.at[i * 3 + k], recv_sem=recv_sems.at[i * 3 + k],
                    device_id=(px, py, c), device_id_type=MESH))
        for cp in copies:
            cp.start()
        for cp in copies:
            cp.wait_recv()
        for cp in copies:
            cp.wait_send()

    return pl.kernel(
        body, out_type=[jax.ShapeDtypeStruct((3,) + tuple(p.shape[1:]), BF) for p in parts], name=name,
        mesh=plsc.ScalarSubcoreMesh(axis_name="sequencer", num_cores=1),
        scratch_types=[pltpu.SemaphoreType.DMA((3 * n,)), pltpu.SemaphoreType.DMA((3 * n,))],
        compiler_params=pltpu.CompilerParams(collective_id=3),
    )(*parts)


def all_gather_small(v, name):
    R, C = v.shape

    def body(v_ref, out_ref, send_sems, recv_sems, local_sem):
        x, y, c = _me()
        me, sibling = (x, y, c), (x, y, 1 - c)
        chips = _other_chips(x, y)

        def slot(px, py, pc):
            return out_ref.at[4 * px + 2 * py + pc]

        def copy(k, block, to, src=None):
            return pltpu.make_async_remote_copy(
                src_ref=slot(*block) if src is None else src, dst_ref=slot(*block),
                send_sem=send_sems.at[k], recv_sem=recv_sems.at[k],
                device_id=to, device_id_type=MESH)

        mine = pltpu.make_async_copy(v_ref, slot(*me), local_sem)
        mine.start()
        first = [copy(0, me, sibling, src=v_ref)]
        first += [copy(1 + j, me, (*chip, c), src=v_ref) for j, chip in enumerate(chips)]
        for cp in first:
            cp.start()
        passed = [copy(4 + j, (*chip, c), sibling) for j, chip in enumerate(chips)]
        for j, chip in enumerate(chips):
            copy(1 + j, (*chip, c), me).wait_recv()
            passed[j].start()
        copy(0, sibling, me).wait_recv()
        for j, chip in enumerate(chips):
            copy(4 + j, (*chip, 1 - c), me).wait_recv()
        for cp in first + passed:
            cp.wait_send()
        mine.wait()

    return pl.pallas_call(
        body, name=name,
        in_specs=[pl.BlockSpec(memory_space=pltpu.VMEM)],
        out_specs=pl.BlockSpec(memory_space=pltpu.VMEM),
        out_shape=jax.ShapeDtypeStruct((NDEV, R, C), F32),
        scratch_shapes=[pltpu.SemaphoreType.DMA((7,)), pltpu.SemaphoreType.DMA((7,)),
                        pltpu.SemaphoreType.DMA],
    )(v)


def _adamw(w, g, m, v):
    m = ADAM_B1 * m + (1.0 - ADAM_B1) * g
    v = ADAM_B2 * v + (1.0 - ADAM_B2) * (g * g)
    m_hat = m / (1.0 - ADAM_B1 ** ADAM_STEP)
    v_hat = v / (1.0 - ADAM_B2 ** ADAM_STEP)
    delta = -ADAM_LR * (m_hat / (jnp.sqrt(v_hat) + ADAM_EPS) + ADAM_WD * w)
    return delta, m, v


def reduce_adamw(part, landed, w, m, v, layer, q_idx, name, prev=(), deps=()):
    R, C = part.shape[1:]
    r, c = w.shape[1:]
    tr = r // 2 if r % 16 == 0 and r >= 256 else r
    tR = tr if tr != r else R
    extra = tuple(prev) + tuple(deps)

    def body(q_ref, p_ref, l_ref, w_ref, m_ref, v_ref, g_out, d_out, m_out, v_out):
        g = p_ref[...].astype(F32)
        for k in range(3):
            g = g + l_ref[k].astype(F32)
        g = g[:tr, :c]
        d, mm, vv = _adamw(w_ref[...], g, m_ref[...], v_ref[...])
        g_out[...] = g
        d_out[...] = d
        m_out[...] = mm
        v_out[...] = vv

    wspec = pl.BlockSpec((None, tr, c), lambda i, q_ref: (layer, i, 0))
    out = jax.ShapeDtypeStruct(w.shape, F32)
    return pl.pallas_call(
        _after(body, 6, extra), name=name,
        grid_spec=pltpu.PrefetchScalarGridSpec(
            num_scalar_prefetch=1, grid=(r // tr,),
            in_specs=[pl.BlockSpec((None, tR, C), lambda i, q_ref: (q_ref[0], i, 0)),
                      pl.BlockSpec((3, tR, C), lambda i, q_ref: (0, i, 0)),
                      wspec, wspec, wspec] + [ANY] * len(extra),
            out_specs=[wspec] * 4),
        out_shape=[out] * 4,
        input_output_aliases={6 + k: k for k in range(len(prev))},
        compiler_params=_cparams(("parallel",)),
    )(q_idx, part, landed, w, m, v, *extra)


def small_reduce_adamw(gathered, w, m, v, name):
    R, C = w.shape

    def body(a_ref, w_ref, m_ref, v_ref, g_out, d_out, m_out, v_out):
        g = a_ref[0]
        for k in range(1, NDEV):
            g = g + a_ref[k]
        d, mm, vv = _adamw(w_ref[...], g, m_ref[...], v_ref[...])
        g_out[...] = g
        d_out[...] = d
        m_out[...] = mm
        v_out[...] = vv

    out = jax.ShapeDtypeStruct((R, C), F32)
    return pl.pallas_call(body, name=name, out_shape=[out] * 4,
                          compiler_params=_cparams())(gathered, w, m, v)


def _pad_cols(a, n):
    return jnp.pad(a, ((0, 0), (0, n - a.shape[1])))


def _pad_rows(a, n):
    return jnp.pad(a, ((0, n - a.shape[0]), (0, 0)))


SMALL_ROWS = 16


def _pack_small(mix, ffn, fin, taps_full, relb):
    return jnp.concatenate([
        mix, ffn, fin.reshape(1, D), taps_full.reshape(6, D),
        jnp.pad(relb.reshape(1, NUM_BUCKETS * H), ((0, 0), (0, D - NUM_BUCKETS * H)))], axis=0)


def kernel(x, mix_norm, ffn_norm, final_norm, conv_w_in, conv_kernel, conv_w_out, attn_w_qkv, attn_w_out, rel_bias, ffn_w_gate, ffn_w_up, ffn_w_down, loss_target, m_mix_norm, m_ffn_norm, m_final_norm, m_conv_w_in, m_conv_kernel, m_conv_w_out, m_attn_w_qkv, m_attn_w_out, m_rel_bias, m_ffn_w_gate, m_ffn_w_up, m_ffn_w_down, v_mix_norm, v_ffn_norm, v_final_norm, v_conv_w_in, v_conv_kernel, v_conv_w_out, v_attn_w_qkv, v_attn_w_out, v_rel_bias, v_ffn_w_gate, v_ffn_w_up, v_ffn_w_down):
    xi, yi, ci = _me()
    me = 4 * xi + 2 * yi + ci
    c_idx = jnp.reshape(ci, (1,)).astype(jnp.int32)
    q_idx = jnp.reshape(2 * xi + yi, (1,)).astype(jnp.int32)
    col0 = me * (D // NDEV)

    taps_local = jnp.zeros((2, 3, D), F32)
    taps_local = lax.dynamic_update_slice(taps_local, conv_kernel, (0, 0, col0))
    taps_pack = jnp.pad(taps_local.reshape(6, D), ((0, 2), (0, 0)))
    taps_all = all_gather_small(taps_pack, "ag_taps")
    taps_sum = jnp.sum(taps_all, axis=0)
    taps = [jnp.pad(taps_sum[3 * j:3 * j + 3], ((0, 5), (0, 0))) for j in range(2)]

    mixer_in = (conv_w_in, attn_w_qkv)
    mixer_out = (conv_w_out, attn_w_out)
    wts = []
    for i in range(DEPTH):
        j = i // 2
        shards = [mixer_in[i % 2][j].astype(BF), mixer_out[i % 2][j].astype(BF),
                  _pad_cols(ffn_w_gate[i].astype(BF), FF_SHARD_PAD),
                  _pad_cols(ffn_w_up[i].astype(BF), FF_SHARD_PAD),
                  _pad_rows(ffn_w_down[i].astype(BF), FF_SHARD_PAD)]
        wts.append(all_gather_weights(shards, (1, 0, 1, 1, 0), f"ag_weights_l{i}"))

    onehot_t, band = _bucket_onehot_t()
    bias3 = bias_tables(rel_bias.T, onehot_t, band, "bias_tables").reshape(3, H * BLK, 2 * BLK)

    saved = []
    xc = x[0]
    for i in range(DEPTH):
        w_in, w_out, w_g, w_u, w_d = wts[i]
        j = i // 2
        x_mix = xc
        z3, h_mix = norm_matmul3(xc, mix_norm[i:i + 1], w_in, f"mix_in_l{i}")
        if i % 2 == 0:
            act = conv_fwd(z3, taps[j], f"conv_fwd_l{i}")
            lse_b = None
        else:
            act, lse_b = attention_fwd(z3, bias3, f"attn_fwd_l{i}")
        xc = matmul_residual(act, w_out, xc, f"mix_out_l{i}")
        x_ffn = xc
        g, u, a, h_ffn = norm_swiglu_up(xc, ffn_norm[i:i + 1], w_g, w_u, f"ffn_up_l{i}")
        xc = matmul_residual(a, w_d, xc, f"ffn_down_l{i}")
        saved.append((x_mix, h_mix, z3, act, lse_b, x_ffn, h_ffn, g, u, a))

    dx, dxb, dg_final, sq = loss_head(xc, final_norm.reshape(1, D), loss_target[0], "loss_head")
    loss = lax.psum(0.5 * jnp.sum(sq[0]) / D, ("x", "y", "c"))

    w_params = ((conv_w_in, m_conv_w_in, v_conv_w_in), (attn_w_qkv, m_attn_w_qkv, v_attn_w_qkv))
    o_params = ((conv_w_out, m_conv_w_out, v_conv_w_out), (attn_w_out, m_attn_w_out, v_attn_w_out))
    dg_mix = [None] * DEPTH
    dg_ffn = [None] * DEPTH
    dtaps = [None, None]
    dbias_all = []
    shape_in, shape_out = (D, 3 * D // NDEV), (D // NDEV, D)
    shape_up, shape_down = (D, FF_SHARD_PAD), (FF_SHARD_PAD, D)
    stacked = {}

    def pair_stage(grads, landed1, axes, shapes, tag, tok):
        parts = []
        for t in range(len(grads)):
            parts.append(pair_add(grads[t], landed1[t], axes[t], shapes[t], c_idx,
                                  f"rs_add_{tag}_{t}", deps=[tok]))
            tok = parts[-1]
        return parts, chip_exchange_grads(parts, f"rs_chip_{tag}"), tok

    def adamw_stage(parts, landed2, params, tag, tok):
        for t, (pname, w_, m_, v_, layer) in enumerate(params):
            res = reduce_adamw(parts[t], landed2[t], w_, m_, v_, layer, q_idx, f"adamw_{tag}_{t}",
                               prev=stacked.get(pname, ()), deps=[tok])
            stacked[pname] = res
            tok = res[0]
        return tok

    tok = dxb
    mix_wait = None
    mix_chip = None
    ffn_chip = None
    for i in reversed(range(DEPTH)):
        w_in, w_out, w_g, w_u, w_d = wts[i]
        j = i // 2
        x_mix, h_mix, z3, act, lse_b, x_ffn, h_ffn, g, u, a = saved[i]
        ffn_params = [("ffn_w_gate", ffn_w_gate, m_ffn_w_gate, v_ffn_w_gate, i),
                      ("ffn_w_up", ffn_w_up, m_ffn_w_up, v_ffn_w_up, i),
                      ("ffn_w_down", ffn_w_down, m_ffn_w_down, v_ffn_w_down, i)]
        if i % 2 == 0:
            mix_params = [("conv_w_in", conv_w_in, m_conv_w_in, v_conv_w_in, j),
                          ("conv_w_out", conv_w_out, m_conv_w_out, v_conv_w_out, j)]
        else:
            mix_params = [("attn_w_qkv", attn_w_qkv, m_attn_w_qkv, v_attn_w_qkv, j),
                          ("attn_w_out", attn_w_out, m_attn_w_out, v_attn_w_out, j)]
        dgate, dup = swiglu_bwd_da(dxb, w_d, g, u, f"ffn_da_l{i}", deps=[tok])
        gw_d = matmul_tn(a, dxb, f"ffn_dwd_l{i}", deps=[dgate])
        tok = gw_d
        if mix_wait is not None:
            grads_m, landed1_m, params_m, tag_m = mix_wait
            parts_m, landed2_m, tok = pair_stage(grads_m, landed1_m, (1, 0), (shape_in, shape_out),
                                                 tag_m, tok)
            mix_chip = (parts_m, landed2_m, params_m, tag_m)
            mix_wait = None
        gw_g = matmul_tn(h_ffn, dgate, f"ffn_dwg_l{i}", deps=[tok])
        gw_u = matmul_tn(h_ffn, dup, f"ffn_dwu_l{i}", deps=[gw_g])
        grads_f = [gw_g, gw_u, gw_d]
        landed1_f = pair_exchange_grads(grads_f, (1, 1, 0), (shape_up, shape_up, shape_down),
                                        f"rs_pair_f{i}")
        tok = gw_u
        if ffn_chip is not None:
            tok = adamw_stage(*ffn_chip, tok)
            ffn_chip = None
        dx, dxb, dg_ffn[i] = matmul_nt_normbwd(
            [(dgate, w_g, False), (dup, w_u, False)], x_ffn, ffn_norm[i:i + 1], dx, f"ffn_dh_l{i}",
            tk=512, deps=[tok])
        gw_out = matmul_tn(act, dxb, f"mix_dwout_l{i}")
        if i % 2 == 0:
            dact = matmul_nt(dxb, w_out, f"mix_dact_l{i}", out_dtype=F32, deps=[gw_out])
        else:
            dact = matmul_nt(dxb, w_out, f"mix_dact_l{i}", out_dtype=BF, deps=[gw_out])
        parts_f, landed2_f, tok = pair_stage(grads_f, landed1_f, (1, 1, 0),
                                             (shape_up, shape_up, shape_down), f"f{i}", dact)
        ffn_chip = (parts_f, landed2_f, ffn_params, f"f{i}")
        if i % 2 == 0:
            dz3, dtaps[j] = conv_bwd(dact, z3, taps[j], f"conv_bwd_l{i}", deps=[tok])
        else:
            dz3, dbias3 = attention_bwd(z3, dact, act, lse_b, bias3, f"attn_bwd_l{i}", deps=[tok])
            dbias_all.append(dbias3.reshape(3, H, BLK * 2 * BLK))
        gw_in = matmul_tn(h_mix, dz3, f"mix_dwin_l{i}")
        grads_m = [gw_in, gw_out]
        landed1_m = pair_exchange_grads(grads_m, (1, 0), (shape_in, shape_out), f"rs_pair_m{i}")
        mix_wait = (grads_m, landed1_m, mix_params, f"m{i}")
        tok = gw_in
        if mix_chip is not None:
            tok = adamw_stage(*mix_chip, tok)
            mix_chip = None
        dx, dxb, dg_mix[i] = matmul_nt_normbwd(
            [(dz3, w_in, True)], x_mix, mix_norm[i:i + 1], dx, f"mix_dh_l{i}", tk=512, deps=[tok])
        tok = dxb
    grads_m, landed1_m, params_m, tag_m = mix_wait
    parts_m, landed2_m, tok = pair_stage(grads_m, landed1_m, (1, 0), (shape_in, shape_out), tag_m, tok)
    tok = adamw_stage(*ffn_chip, tok)

    grad_relb_t = bias_grad(jnp.concatenate(dbias_all), onehot_t, "bias_grad")
    dtaps_full = jnp.stack([dtaps[0][:3], dtaps[1][:3]])
    g_small = _pack_small(jnp.concatenate([d[0:1] for d in dg_mix], axis=0),
                          jnp.concatenate([d[0:1] for d in dg_ffn], axis=0),
                          dg_final[0], dtaps_full, grad_relb_t.T)
    gathered = all_gather_small(g_small, "ag_small_grads")

    def taps_at_cols(k):
        return lax.dynamic_update_slice(jnp.zeros((2, 3, D), F32), k, (0, 0, col0))

    w_small = _pack_small(mix_norm, ffn_norm, final_norm, taps_at_cols(conv_kernel), rel_bias)
    m_small = _pack_small(m_mix_norm, m_ffn_norm, m_final_norm, taps_at_cols(m_conv_kernel), m_rel_bias)
    v_small = _pack_small(v_mix_norm, v_ffn_norm, v_final_norm, taps_at_cols(v_conv_kernel), v_rel_bias)
    small = small_reduce_adamw(gathered, w_small, m_small, v_small, "adamw_small")

    def unpack_small(p):
        taps_p = lax.dynamic_slice(p[9:15].reshape(2, 3, D), (0, 0, col0), (2, 3, D // NDEV))
        return {"mix_norm": p[0:4], "ffn_norm": p[4:8], "final_norm": p[8],
                "conv_kernel": taps_p, "rel_bias": p[15, :NUM_BUCKETS * H].reshape(NUM_BUCKETS, H)}

    small_out = [unpack_small(p) for p in small]
    adamw_stage(parts_m, landed2_m, params_m, tag_m, small[0])

    names = ["mix_norm", "ffn_norm", "final_norm", "conv_w_in", "conv_kernel", "conv_w_out",
             "attn_w_qkv", "attn_w_out", "rel_bias", "ffn_w_gate", "ffn_w_up", "ffn_w_down"]
    outs = [loss, dx.reshape(1, S, D)]
    for o in range(4):
        for nme in names:
            outs.append(stacked[nme][o] if nme in stacked else small_out[o][nme])
    return tuple(outs)
```

```python
import functools
import math

import jax
import jax.numpy as jnp
from jax import lax
from jax.experimental import pallas as pl
from jax.experimental.pallas import tpu as pltpu
from jax.experimental.pallas import tpu_sc as plsc

S = 2048
D = 1024
H = 16
DH = 64
DFF = 2816
NDEV = 8
DEPTH = 4
FF_SHARD = DFF // NDEV
FF_SHARD_PAD = 384
DFF_PAD = FF_SHARD_PAD * NDEV
BLK = 128
BRANCH_DILATIONS = (1, 4, 16)
NUM_BUCKETS = 32
MAX_DISTANCE = 2048
EPS = 1e-6
NEG_INF = -1e30
SCALE = DH ** -0.5

ADAM_LR = 0.001
ADAM_B1 = 0.9
ADAM_B2 = 0.999
ADAM_EPS = 1e-08
ADAM_WD = 0.01
ADAM_STEP = 10

BF = jnp.bfloat16
F32 = jnp.float32
VMEM_LIMIT_BYTES = 56 * 1024 * 1024
MESH = pl.DeviceIdType.MESH
ANY = pl.BlockSpec(memory_space=pl.ANY)

_NT = (((1,), (1,)), ((), ()))
_TN = (((0,), (0,)), ((), ()))


def _cparams(sem=None):
    return pltpu.CompilerParams(dimension_semantics=sem, vmem_limit_bytes=VMEM_LIMIT_BYTES)


def _after(body, n, deps):
    nd = len(deps)
    if nd == 0:
        return body

    def ordered(*refs):
        body(*refs[:n], *refs[n + nd:])
    return ordered


def _rms(x):
    return lax.rsqrt(jnp.mean(x * x, axis=-1, keepdims=True) + EPS)


def norm_matmul3(x, gain, w, name, tm=1024, tn=512):
    per = D // tn

    def body(x_ref, g_ref, w_ref, z_ref, h_ref, hs_ref):
        @pl.when(pl.program_id(1) == 0)
        def _():
            xv = x_ref[...]
            hv = (xv * _rms(xv) * g_ref[...]).astype(BF)
            hs_ref[...] = hv
            h_ref[...] = hv
        z_ref[...] = jnp.dot(hs_ref[...], w_ref[...], preferred_element_type=F32).astype(BF)

    return pl.pallas_call(
        body, name=name,
        grid=(S // tm, 3 * D // tn),
        in_specs=[pl.BlockSpec((tm, D), lambda i, j: (i, 0)),
                  pl.BlockSpec((1, D), lambda i, j: (0, 0)),
                  pl.BlockSpec((D, tn), lambda i, j: (0, j))],
        out_specs=[pl.BlockSpec((None, tm, tn), lambda i, j: (j // per, i, j % per)),
                   pl.BlockSpec((tm, D), lambda i, j: (i, 0))],
        out_shape=[jax.ShapeDtypeStruct((3, S, D), BF), jax.ShapeDtypeStruct((S, D), BF)],
        scratch_shapes=[pltpu.VMEM((tm, D), BF)],
        compiler_params=_cparams(("parallel", "arbitrary")),
    )(x, gain, w)


def norm_swiglu_up(x, gain, wg_t, wu_t, name, tm=1024, tn=512):
    def body(x_ref, g_ref, wg_ref, wu_ref, go_ref, uo_ref, ao_ref, h_ref, hs_ref):
        @pl.when(pl.program_id(1) == 0)
        def _():
            xv = x_ref[...]
            hv = (xv * _rms(xv) * g_ref[...]).astype(BF)
            hs_ref[...] = hv
            h_ref[...] = hv
        hv = hs_ref[...]
        g = lax.dot_general(hv, wg_ref[...], _NT, preferred_element_type=F32)
        u = lax.dot_general(hv, wu_ref[...], _NT, preferred_element_type=F32)
        go_ref[...] = g.astype(BF)
        uo_ref[...] = u.astype(BF)
        ao_ref[...] = (g * jax.nn.sigmoid(g) * u).astype(BF)

    act = jax.ShapeDtypeStruct((S, DFF_PAD), BF)
    blk = pl.BlockSpec((tm, tn), lambda i, j: (i, j))
    return pl.pallas_call(
        body, name=name,
        grid=(S // tm, DFF_PAD // tn),
        in_specs=[pl.BlockSpec((tm, D), lambda i, j: (i, 0)),
                  pl.BlockSpec((1, D), lambda i, j: (0, 0)),
                  pl.BlockSpec((tn, D), lambda i, j: (j, 0)),
                  pl.BlockSpec((tn, D), lambda i, j: (j, 0))],
        out_specs=[blk, blk, blk, pl.BlockSpec((tm, D), lambda i, j: (i, 0))],
        out_shape=[act, act, act, jax.ShapeDtypeStruct((S, D), BF)],
        scratch_shapes=[pltpu.VMEM((tm, D), BF)],
        compiler_params=_cparams(("parallel", "arbitrary")),
    )(x, gain, wg_t, wu_t)


def matmul_residual(a, w, x, name, tm=1024, tn=512):
    K = a.shape[1]

    def body(a_ref, w_ref, x_ref, o_ref):
        o_ref[...] = x_ref[...] + jnp.dot(a_ref[...], w_ref[...], preferred_element_type=F32)

    return pl.pallas_call(
        body, name=name,
        grid=(S // tm, D // tn),
        in_specs=[pl.BlockSpec((tm, K), lambda i, j: (i, 0)),
                  pl.BlockSpec((K, tn), lambda i, j: (0, j)),
                  pl.BlockSpec((tm, tn), lambda i, j: (i, j))],
        out_specs=pl.BlockSpec((tm, tn), lambda i, j: (i, j)),
        out_shape=jax.ShapeDtypeStruct((S, D), F32),
        compiler_params=_cparams(("parallel", "parallel")),
    )(a, w, x)


def matmul_nt(a, w, name, out_dtype=BF, tm=1024, tn=512, deps=()):
    K = a.shape[1]
    N = w.shape[0]

    def body(a_ref, w_ref, o_ref):
        o_ref[...] = lax.dot_general(a_ref[...], w_ref[...], _NT,
                                     preferred_element_type=F32).astype(o_ref.dtype)

    return pl.pallas_call(
        _after(body, 2, deps), name=name,
        grid=(S // tm, N // tn),
        in_specs=[pl.BlockSpec((tm, K), lambda i, j: (i, 0)),
                  pl.BlockSpec((tn, K), lambda i, j: (j, 0))] + [ANY] * len(deps),
        out_specs=pl.BlockSpec((tm, tn), lambda i, j: (i, j)),
        out_shape=jax.ShapeDtypeStruct((S, N), out_dtype),
        compiler_params=_cparams(("parallel", "parallel")),
    )(a, w, *deps)


def swiglu_bwd_da(dxb, wd, g, u, name, tm=1024, tn=512, deps=()):
    def body(dx_ref, w_ref, g_ref, u_ref, dg_ref, du_ref):
        da = lax.dot_general(dx_ref[...], w_ref[...], _NT, preferred_element_type=F32)
        gv = g_ref[...].astype(F32)
        uv = u_ref[...].astype(F32)
        sig = jax.nn.sigmoid(gv)
        dg_ref[...] = (da * uv * (sig * (1.0 + gv * (1.0 - sig)))).astype(BF)
        du_ref[...] = (da * (gv * sig)).astype(BF)

    act = jax.ShapeDtypeStruct((S, DFF_PAD), BF)
    blk = pl.BlockSpec((tm, tn), lambda i, j: (i, j))
    return pl.pallas_call(
        _after(body, 4, deps), name=name,
        grid=(S // tm, DFF_PAD // tn),
        in_specs=[pl.BlockSpec((tm, D), lambda i, j: (i, 0)),
                  pl.BlockSpec((tn, D), lambda i, j: (j, 0)),
                  blk, blk] + [ANY] * len(deps),
        out_specs=[blk, blk],
        out_shape=[act, act],
        compiler_params=_cparams(("parallel", "parallel")),
    )(dxb, wd, g, u, *deps)


def matmul_tn(a, b, name, tm=1024, tn=512, deps=()):
    M = a.shape[1]
    if b.ndim == 3:
        per = D // tn
        N = 3 * D
        b_spec = pl.BlockSpec((None, S, tn), lambda i, j: (j // per, 0, j % per))
    else:
        N = b.shape[1]
        b_spec = pl.BlockSpec((S, tn), lambda i, j: (0, j))

    def body(a_ref, b_ref, o_ref):
        o_ref[...] = lax.dot_general(a_ref[...], b_ref[...], _TN,
                                     preferred_element_type=F32).astype(BF)

    return pl.pallas_call(
        _after(body, 2, deps), name=name,
        grid=(M // tm, N // tn),
        in_specs=[pl.BlockSpec((S, tm), lambda i, j: (0, i)), b_spec] + [ANY] * len(deps),
        out_specs=pl.BlockSpec((tm, tn), lambda i, j: (i, j)),
        out_shape=jax.ShapeDtypeStruct((M, N), BF),
        compiler_params=_cparams(("parallel", "parallel")),
    )(a, b, *deps)


def matmul_normbwd(terms, x_in, gain, dx, name, tm=512, ch=256, deps=()):
    specs, operands = [], []
    for (a, w, stacked) in terms:
        if stacked:
            specs.append(pl.BlockSpec((3, tm, D), lambda i: (0, i, 0)))
        else:
            specs.append(pl.BlockSpec((tm, a.shape[1]), lambda i: (i, 0)))
        specs.append(pl.BlockSpec(w.shape, lambda i: (0, 0), pipeline_mode=pl.Buffered(1)))
        operands += [a, w]
    nt = len(terms)

    def body(*refs):
        aw = refs[:2 * nt]
        x_ref, g_ref, dx_ref, dxo_ref, dxb_ref, dg_ref, acc_ref = refs[2 * nt:]

        @pl.when(pl.program_id(0) == 0)
        def _():
            dg_ref[...] = jnp.zeros_like(dg_ref)

        dh = None
        for t, (_, _, stacked) in enumerate(terms):
            a_ref, w_ref = aw[2 * t], aw[2 * t + 1]
            if stacked:
                parts = [lax.dot_general(a_ref[k], w_ref[:, k * D:(k + 1) * D], _NT,
                                         preferred_element_type=F32) for k in range(3)]
            else:
                parts = [jnp.dot(a_ref[...], w_ref[...], preferred_element_type=F32)]
            for p in parts:
                dh = p if dh is None else dh + p
        acc_ref[...] = dh

        def chunk(c, carry):
            rows = pl.ds(pl.multiple_of(c * ch, ch), ch)
            xv = x_ref[rows, :]
            r = _rms(xv)
            xhat = xv * r
            dhc = acc_ref[rows, :]
            dg_ref[0:1, :] += jnp.sum(dhc * xhat, axis=0, keepdims=True)
            dxh = dhc * g_ref[...]
            dxn = r * (dxh - xhat * jnp.mean(dxh * xhat, axis=-1, keepdims=True))
            out = dx_ref[rows, :] + dxn
            dxo_ref[rows, :] = out
            dxb_ref[rows, :] = out.astype(BF)
            return carry
        lax.fori_loop(0, tm // ch, chunk, 0)

    row = pl.BlockSpec((tm, D), lambda i: (i, 0))
    return pl.pallas_call(
        _after(body, 2 * nt + 3, deps), name=name,
        grid=(S // tm,),
        in_specs=specs + [row, pl.BlockSpec((1, D), lambda i: (0, 0)), row] + [ANY] * len(deps),
        out_specs=[row, row, pl.BlockSpec((8, D), lambda i: (0, 0))],
        out_shape=[jax.ShapeDtypeStruct((S, D), F32), jax.ShapeDtypeStruct((S, D), BF),
                   jax.ShapeDtypeStruct((8, D), F32)],
        scratch_shapes=[pltpu.VMEM((tm, D), F32)],
        compiler_params=_cparams(("arbitrary",)),
    )(*operands, x_in, gain, dx, *deps)


def loss_head(x, gain, target, name, tm=512):
    def body(x_ref, g_ref, t_ref, dxo_ref, dxb_ref, dg_ref, sq_ref):
        @pl.when(pl.program_id(0) == 0)
        def _():
            dg_ref[...] = jnp.zeros_like(dg_ref)
            sq_ref[...] = jnp.zeros_like(sq_ref)
        xv = x_ref[...]
        r = _rms(xv)
        xhat = xv * r
        err = xhat * g_ref[...] - t_ref[...]
        sq_ref[0:1, :] += jnp.sum(err * err, axis=0, keepdims=True)
        dy = err * (1.0 / D)
        dg_ref[0:1, :] += jnp.sum(dy * xhat, axis=0, keepdims=True)
        dxh = dy * g_ref[...]
        out = r * (dxh - xhat * jnp.mean(dxh * xhat, axis=-1, keepdims=True))
        dxo_ref[...] = out
        dxb_ref[...] = out.astype(BF)

    row = pl.BlockSpec((tm, D), lambda i: (i, 0))
    acc = pl.BlockSpec((8, D), lambda i: (0, 0))
    return pl.pallas_call(
        body, name=name,
        grid=(S // tm,),
        in_specs=[row, pl.BlockSpec((1, D), lambda i: (0, 0)), row],
        out_specs=[row, row, acc, acc],
        out_shape=[jax.ShapeDtypeStruct((S, D), F32), jax.ShapeDtypeStruct((S, D), BF),
                   jax.ShapeDtypeStruct((8, D), F32), jax.ShapeDtypeStruct((8, D), F32)],
        compiler_params=_cparams(("arbitrary",)),
    )(x, gain, target)


def _shift_down(p, n, row):
    return jnp.where(row >= n, pltpu.roll(p, n, axis=0), 0.0)


def _shift_up(p, n, row):
    return jnp.where(row < S - n, pltpu.roll(p, S - n, axis=0), 0.0)


def conv_fwd(z3, taps, name, tn=128):
    def body(z_ref, k_ref, m_ref):
        b = z_ref[0].astype(F32)
        p = z_ref[1].astype(F32) * z_ref[2].astype(F32)
        row = lax.broadcasted_iota(jnp.int32, p.shape, 0)
        y = (k_ref[2:3, :] * p + k_ref[1:2, :] * _shift_down(p, 1, row)
             + k_ref[0:1, :] * _shift_down(p, 2, row))
        m_ref[...] = (b * y).astype(BF)

    return pl.pallas_call(
        body, name=name,
        grid=(D // tn,),
        in_specs=[pl.BlockSpec((3, S, tn), lambda j: (0, 0, j)),
                  pl.BlockSpec((8, tn), lambda j: (0, j))],
        out_specs=pl.BlockSpec((S, tn), lambda j: (0, j)),
        out_shape=jax.ShapeDtypeStruct((S, D), BF),
        compiler_params=_cparams(("parallel",)),
    )(z3, taps)


def conv_bwd(dm, z3, taps, name, tn=128, deps=()):
    def body(dm_ref, z_ref, k_ref, dz_ref, dk_ref):
        dmv = dm_ref[...]
        b = z_ref[0].astype(F32)
        c = z_ref[1].astype(F32)
        u = z_ref[2].astype(F32)
        p = c * u
        row = lax.broadcasted_iota(jnp.int32, p.shape, 0)
        p1 = _shift_down(p, 1, row)
        p2 = _shift_down(p, 2, row)
        y = k_ref[2:3, :] * p + k_ref[1:2, :] * p1 + k_ref[0:1, :] * p2
        dy = dmv * b
        dz_ref[0] = (dmv * y).astype(BF)
        dp = (k_ref[2:3, :] * dy + k_ref[1:2, :] * _shift_up(dy, 1, row)
              + k_ref[0:1, :] * _shift_up(dy, 2, row))
        dz_ref[1] = (dp * u).astype(BF)
        dz_ref[2] = (dp * c).astype(BF)
        dk_ref[...] = jnp.zeros_like(dk_ref)
        dk_ref[0:1, :] = jnp.sum(dy * p2, axis=0, keepdims=True)
        dk_ref[1:2, :] = jnp.sum(dy * p1, axis=0, keepdims=True)
        dk_ref[2:3, :] = jnp.sum(dy * p, axis=0, keepdims=True)

    return pl.pallas_call(
        _after(body, 3, deps), name=name,
        grid=(D // tn,),
        in_specs=[pl.BlockSpec((S, tn), lambda j: (0, j)),
                  pl.BlockSpec((3, S, tn), lambda j: (0, 0, j)),
                  pl.BlockSpec((8, tn), lambda j: (0, j))] + [ANY] * len(deps),
        out_specs=[pl.BlockSpec((3, S, tn), lambda j: (0, 0, j)),
                   pl.BlockSpec((8, tn), lambda j: (0, j))],
        out_shape=[jax.ShapeDtypeStruct((3, S, D), BF), jax.ShapeDtypeStruct((8, D), F32)],
        compiler_params=_cparams(("parallel",)),
    )(dm, z3, taps, *deps)


def _t5_bucket(dist):
    exact = NUM_BUCKETS // 2
    df = jnp.maximum(dist, 1).astype(jnp.float32)
    large = exact + (jnp.log(df / exact) / math.log(MAX_DISTANCE / exact)
                     * (NUM_BUCKETS - exact)).astype(jnp.int32)
    large = jnp.minimum(large, NUM_BUCKETS - 1)
    return jnp.where(dist < exact, dist, large)


def _bucket_onehot_t():
    qi = jnp.arange(BLK)[:, None]
    ki = jnp.arange(2 * BLK)[None, :]
    rel = qi + BLK - ki
    band = ((rel >= 0) & (rel <= BLK)).reshape(1, -1).astype(F32)
    hots = []
    for d in BRANCH_DILATIONS:
        bucket = _t5_bucket(jnp.clip(rel, 0) * d).reshape(1, -1)
        hots.append((jnp.arange(NUM_BUCKETS)[:, None] == bucket).astype(F32))
    return jnp.stack(hots), band


def bias_tables(rel_bias_t, onehot_t, band, name):
    def body(rb_ref, oh_ref, band_ref, o_ref):
        b = jnp.dot(rb_ref[...], oh_ref[...], preferred_element_type=F32,
                    precision=lax.Precision.HIGHEST)
        o_ref[...] = jnp.where(band_ref[...] > 0.5, b, NEG_INF)

    n = BLK * 2 * BLK
    return pl.pallas_call(
        body, name=name,
        grid=(3,),
        in_specs=[pl.BlockSpec((H, NUM_BUCKETS), lambda g: (0, 0)),
                  pl.BlockSpec((None, NUM_BUCKETS, n), lambda g: (g, 0, 0)),
                  pl.BlockSpec((1, n), lambda g: (0, 0))],
        out_specs=pl.BlockSpec((None, H, n), lambda g: (g, 0, 0)),
        out_shape=jax.ShapeDtypeStruct((3, H, n), F32),
        compiler_params=_cparams(("parallel",)),
    )(rel_bias_t, onehot_t, band)


def bias_grad(dbias, onehot_t, name):
    def body(db_ref, oh_ref, o_ref):
        @pl.when(pl.program_id(0) == 0)
        def _():
            o_ref[...] = jnp.zeros_like(o_ref)
        o_ref[...] += lax.dot_general(db_ref[...], oh_ref[...], _NT, preferred_element_type=F32,
                                      precision=lax.Precision.HIGHEST)

    n = BLK * 2 * BLK
    return pl.pallas_call(
        body, name=name,
        grid=(dbias.shape[0],),
        in_specs=[pl.BlockSpec((None, H, n), lambda g: (g, 0, 0)),
                  pl.BlockSpec((None, NUM_BUCKETS, n), lambda g: (g % 3, 0, 0))],
        out_specs=pl.BlockSpec((H, NUM_BUCKETS), lambda g: (0, 0)),
        out_shape=jax.ShapeDtypeStruct((H, NUM_BUCKETS), F32),
        compiler_params=_cparams(("arbitrary",)),
    )(dbias, onehot_t)


def _head_masks():
    lane = lax.broadcasted_iota(jnp.int32, (1, 2 * DH), 1)
    return (lane < DH, lane >= DH)


def _stack_heads(x, masks):
    zero = jnp.zeros_like(x)
    return jnp.concatenate([jnp.where(masks[0], x, zero), jnp.where(masks[1], x, zero)], axis=0)


def _deinterleave(src_ref, dst_ref, d, dtype):
    L = S // d
    for r in range(d):
        dst_ref[r * L:(r + 1) * L, :] = src_ref[pl.ds(r, L, stride=d), :].astype(dtype)


def _branch_loops(d, block):
    L = S // d
    nb = L // BLK

    def first(base):
        block(base, base, BLK, True)

    def later(base, n):
        q0 = pl.multiple_of(base + n * BLK, BLK)
        block(q0, pl.multiple_of(q0 - BLK, BLK), 2 * BLK, False)

    if d == 1:
        unroll = 3
        assert (nb - 1) % unroll == 0
        first(0)

        def trip(it, c):
            for u in range(unroll):
                later(0, 1 + it * unroll + u)
            return c
        lax.fori_loop(0, (nb - 1) // unroll, trip, 0)
    elif nb > 1:
        def residue(r, c):
            base = pl.multiple_of(r * L, BLK)
            first(base)
            for n in range(1, nb):
                later(base, n)
            return c
        lax.fori_loop(0, d, residue, 0)
    else:
        unroll = 4
        assert d % unroll == 0

        def trip(it, c):
            for u in range(unroll):
                first(pl.multiple_of((it * unroll + u) * L, BLK))
            return c
        lax.fori_loop(0, d // unroll, trip, 0)


def attention_fwd(z3, bias3, name):
    W = 2 * DH
    CH = 256

    def body(q_ref, k_ref, v_ref, b_ref, o_ref, lse_ref, stage, qd, kd, vd, od, ld, on, ln):
        masks = _head_masks()
        for src, dst in ((q_ref, qd), (k_ref, kd), (v_ref, vd)):
            stage[...] = src[...].astype(F32)
            for gi, d in enumerate(BRANCH_DILATIONS[1:]):
                _deinterleave(stage, dst.at[gi], d, BF)

        for g, d in enumerate(BRANCH_DILATIONS):
            qs, ks, vs = (q_ref, k_ref, v_ref) if d == 1 else (qd.at[g - 1], kd.at[g - 1], vd.at[g - 1])
            o_dst, l_dst = (on.at[0], ln.at[0]) if d == 1 else (od, ld)

            def block(q0, k0, nk, first, g=g, qs=qs, ks=ks, vs=vs, o_dst=o_dst, l_dst=l_dst):
                q2 = _stack_heads(qs[pl.ds(q0, BLK), :], masks)
                kk = ks[pl.ds(k0, nk), :]
                vv = vs[pl.ds(k0, nk), :]
                bias = b_ref[g][:, BLK:] if first else b_ref[g]
                s = lax.dot_general(q2, kk, _NT, preferred_element_type=F32) * SCALE + bias
                mx = jnp.max(s, axis=1, keepdims=True)
                p = jnp.exp(s - mx)
                l = jnp.sum(p, axis=1, keepdims=True)
                o2 = jnp.dot(p.astype(BF), vv, preferred_element_type=F32) / l
                lse2 = mx + jnp.log(l)
                o_dst[pl.ds(q0, BLK), :] = jnp.where(masks[0], o2[:BLK], o2[BLK:])
                l_dst[pl.ds(q0, BLK), :] = jnp.where(masks[0], lse2[:BLK], lse2[BLK:])

            _branch_loops(d, block)
            if d > 1:
                L = S // d
                for r in range(d):
                    on[g, pl.ds(r, L, stride=d), :] = od[r * L:(r + 1) * L, :]
                    ln[g, pl.ds(r, L, stride=d), :] = ld[r * L:(r + 1) * L, :]

        def join(c, carry):
            rows = pl.ds(pl.multiple_of(c * CH, CH), CH)
            a, b, cc = ln[0, rows, :], ln[1, rows, :], ln[2, rows, :]
            mx = jnp.maximum(jnp.maximum(a, b), cc)
            ea, eb, ec = jnp.exp(a - mx), jnp.exp(b - mx), jnp.exp(cc - mx)
            tot = ea + eb + ec
            o_ref[rows, :] = ((ea * on[0, rows, :] + eb * on[1, rows, :] + ec * on[2, rows, :])
                              / tot).astype(BF)
            lse_ref[rows, :] = mx + jnp.log(tot)
            return carry
        lax.fori_loop(0, S // CH, join, 0)

    col = pl.BlockSpec((S, W), lambda hp: (0, hp))
    return pl.pallas_call(
        body, name=name,
        grid=(D // W,),
        in_specs=[pl.BlockSpec((None, S, W), lambda hp: (0, 0, hp)),
                  pl.BlockSpec((None, S, W), lambda hp: (1, 0, hp)),
                  pl.BlockSpec((None, S, W), lambda hp: (2, 0, hp)),
                  pl.BlockSpec((3, 2 * BLK, 2 * BLK), lambda hp: (0, hp, 0))],
        out_specs=[col, col],
        out_shape=[jax.ShapeDtypeStruct((S, D), BF), jax.ShapeDtypeStruct((S, D), F32)],
        scratch_shapes=[pltpu.VMEM((S, W), F32),
                        pltpu.VMEM((2, S, W), BF), pltpu.VMEM((2, S, W), BF), pltpu.VMEM((2, S, W), BF),
                        pltpu.VMEM((S, W), F32), pltpu.VMEM((S, W), F32),
                        pltpu.VMEM((3, S, W), F32), pltpu.VMEM((3, S, W), F32)],
        compiler_params=_cparams(("parallel",)),
    )(z3, z3, z3, bias3)


def attention_bwd(z3, dob, ob, lse_b, bias3, name, deps=()):
    W = 2 * DH
    CH = 256

    def body(q_ref, k_ref, v_ref, do_ref, o_ref, lse_ref, b_ref, dz_ref, db_ref,
             stage, delta, qd, kd, vd, dod, lsd, dld, res, acc):
        masks = _head_masks()

        def rowsum(c, carry):
            rows = pl.ds(pl.multiple_of(c * CH, CH), CH)
            prod = do_ref[rows, :].astype(F32) * o_ref[rows, :].astype(F32)
            sa = jnp.sum(jnp.where(masks[0], prod, 0.0), axis=1, keepdims=True)
            sb = jnp.sum(jnp.where(masks[1], prod, 0.0), axis=1, keepdims=True)
            delta[rows, :] = jnp.where(masks[0], sa, sb)
            return carry
        lax.fori_loop(0, S // CH, rowsum, 0)

        for src, dst in ((q_ref, qd), (k_ref, kd), (v_ref, vd), (do_ref, dod)):
            stage[...] = src[...].astype(F32)
            for gi, d in enumerate(BRANCH_DILATIONS[1:]):
                _deinterleave(stage, dst.at[gi], d, BF)
        for gi, d in enumerate(BRANCH_DILATIONS[1:]):
            _deinterleave(lse_ref, lsd.at[gi], d, F32)
            _deinterleave(delta, dld.at[gi], d, F32)

        db_ref[...] = jnp.zeros_like(db_ref)
        for g, d in enumerate(BRANCH_DILATIONS):
            if d == 1:
                qs, ks, vs, dos, ls, dl = q_ref, k_ref, v_ref, do_ref, lse_ref, delta
            else:
                qs, ks, vs, dos = qd.at[g - 1], kd.at[g - 1], vd.at[g - 1], dod.at[g - 1]
                ls, dl = lsd.at[g - 1], dld.at[g - 1]
            res[1] = jnp.zeros((S, W), F32)
            res[2] = jnp.zeros((S, W), F32)

            def block(q0, k0, nk, first, g=g, qs=qs, ks=ks, vs=vs, dos=dos, ls=ls, dl=dl):
                kk = ks[pl.ds(k0, nk), :]
                vv = vs[pl.ds(k0, nk), :]
                q2 = _stack_heads(qs[pl.ds(q0, BLK), :], masks)
                do2 = _stack_heads(dos[pl.ds(q0, BLK), :], masks)
                lse_blk = ls[pl.ds(q0, BLK), :]
                del_blk = dl[pl.ds(q0, BLK), :]
                lse2 = jnp.concatenate([lse_blk[:, 0:1], lse_blk[:, DH:DH + 1]], axis=0)
                del2 = jnp.concatenate([del_blk[:, 0:1], del_blk[:, DH:DH + 1]], axis=0)
                bias = b_ref[g][:, BLK:] if first else b_ref[g]
                s = lax.dot_general(q2, kk, _NT, preferred_element_type=F32) * SCALE + bias
                p = jnp.exp(s - lse2)
                dp = lax.dot_general(do2, vv, _NT, preferred_element_type=F32)
                ds = p * (dp - del2)
                if first:
                    db_ref[g, :, BLK:] += ds
                else:
                    db_ref[g] += ds
                dsb = ds.astype(BF)
                dq2 = jnp.dot(dsb, kk, preferred_element_type=F32) * SCALE
                res[0, pl.ds(q0, BLK), :] = jnp.where(masks[0], dq2[:BLK], dq2[BLK:])
                res[1, pl.ds(k0, nk), :] += lax.dot_general(dsb, q2, _TN,
                                                            preferred_element_type=F32) * SCALE
                res[2, pl.ds(k0, nk), :] += lax.dot_general(p.astype(BF), do2, _TN,
                                                            preferred_element_type=F32)

            _branch_loops(d, block)
            L = S // d
            for t in range(3):
                if d == 1:
                    acc[t] = res[t]
                else:
                    for r in range(d):
                        acc[t, pl.ds(r, L, stride=d), :] = (acc[t, pl.ds(r, L, stride=d), :]
                                                            + res[t, r * L:(r + 1) * L, :])
        for t in range(3):
            dz_ref[t] = acc[t].astype(BF)

    col = pl.BlockSpec((S, W), lambda hp: (0, hp))
    bspec = pl.BlockSpec((3, 2 * BLK, 2 * BLK), lambda hp: (0, hp, 0))
    return pl.pallas_call(
        _after(body, 7, deps), name=name,
        grid=(D // W,),
        in_specs=[pl.BlockSpec((None, S, W), lambda hp: (0, 0, hp)),
                  pl.BlockSpec((None, S, W), lambda hp: (1, 0, hp)),
                  pl.BlockSpec((None, S, W), lambda hp: (2, 0, hp)),
                  col, col, col, bspec] + [ANY] * len(deps),
        out_specs=[pl.BlockSpec((3, S, W), lambda hp: (0, 0, hp)), bspec],
        out_shape=[jax.ShapeDtypeStruct((3, S, D), BF),
                   jax.ShapeDtypeStruct((3, H * BLK, 2 * BLK), F32)],
        scratch_shapes=[pltpu.VMEM((S, W), F32), pltpu.VMEM((S, W), F32),
                        pltpu.VMEM((2, S, W), BF), pltpu.VMEM((2, S, W), BF),
                        pltpu.VMEM((2, S, W), BF), pltpu.VMEM((2, S, W), BF),
                        pltpu.VMEM((2, S, W), F32), pltpu.VMEM((2, S, W), F32),
                        pltpu.VMEM((3, S, W), F32), pltpu.VMEM((3, S, W), F32)],
        compiler_params=_cparams(("parallel",)),
    )(z3, z3, z3, dob, ob, lse_b, bias3, *deps)


def _me():
    return lax.axis_index("x"), lax.axis_index("y"), lax.axis_index("c")


def _other_chips(x, y):
    return [(1 - x, y), (x, 1 - y), (1 - x, 1 - y)]


def _shard_window(ref, axis, t, shape):
    R, C = shape
    if axis == 0:
        return ref.at[pl.ds(pl.multiple_of(t * R, 128), R), :]
    return ref.at[:, pl.ds(pl.multiple_of(t * C, 128), C)]


def all_gather_weights(shards, axes, name):
    n = len(shards)
    shapes = [s.shape for s in shards]
    outs_shape = [jax.ShapeDtypeStruct((8 * R, C) if ax == 0 else (R, 8 * C), BF)
                  for (R, C), ax in zip(shapes, axes)]

    def body(*refs):
        ins, outs = refs[:n], refs[n:2 * n]
        send_sems, recv_sems, local_sems = refs[2 * n:]
        x, y, c = _me()
        sibling = (x, y, 1 - c)
        chips = _other_chips(x, y)
        barrier = pltpu.get_barrier_semaphore()
        for peer in [sibling] + [(*chip, c) for chip in chips]:
            pl.semaphore_signal(barrier, inc=1, device_id=peer, device_id_type=MESH)
        pl.semaphore_wait(barrier, 4)

        def win(i, px, py, pc):
            return _shard_window(outs[i], axes[i], 4 * px + 2 * py + pc, shapes[i])

        def copy(i, k, block, to, src=None):
            return pltpu.make_async_remote_copy(
                src_ref=win(i, *block) if src is None else src, dst_ref=win(i, *block),
                send_sem=send_sems.at[i * 7 + k], recv_sem=recv_sems.at[i * 7 + k],
                device_id=to, device_id_type=MESH)

        mine = [pltpu.make_async_copy(ins[i], win(i, x, y, c), local_sems.at[i]) for i in range(n)]
        for cp in mine:
            cp.start()
        first = []
        for i in range(n):
            first.append(copy(i, 0, (x, y, c), sibling, src=ins[i]))
            for j, chip in enumerate(chips):
                first.append(copy(i, 1 + j, (x, y, c), (*chip, c), src=ins[i]))
        for cp in first:
            cp.start()
        passed = []
        for j, chip in enumerate(chips):
            for i in range(n):
                copy(i, 1 + j, (*chip, c), (x, y, c)).wait_recv()
                cp = copy(i, 4 + j, (*chip, c), sibling)
                cp.start()
                passed.append(cp)
        for i in range(n):
            copy(i, 0, sibling, (x, y, c)).wait_recv()
        for j, chip in enumerate(chips):
            for i in range(n):
                copy(i, 4 + j, (*chip, 1 - c), (x, y, c)).wait_recv()
        for cp in first + passed:
            cp.wait_send()
        for cp in mine:
            cp.wait()

    return pl.kernel(
        body, out_type=outs_shape, name=name,
        mesh=plsc.ScalarSubcoreMesh(axis_name="sequencer", num_cores=1),
        scratch_types=[pltpu.SemaphoreType.DMA((7 * n,)), pltpu.SemaphoreType.DMA((7 * n,)),
                       pltpu.SemaphoreType.DMA((n,))],
        compiler_params=pltpu.CompilerParams(collective_id=1),
    )(*shards)


def pair_exchange_grads(grads, axes, shapes, name):
    n = len(grads)

    def body(*refs):
        ins, outs = refs[:n], refs[n:2 * n]
        send_sems, recv_sems = refs[2 * n:]
        x, y, c = _me()
        sibling = (x, y, 1 - c)
        barrier = pltpu.get_barrier_semaphore()
        pl.semaphore_signal(barrier, inc=1, device_id=sibling, device_id_type=MESH)
        pl.semaphore_wait(barrier, 1)
        copies = []
        for i in range(n):
            for q in range(4):
                t = 2 * q + (1 - c)
                copies.append(pltpu.make_async_remote_copy(
                    src_ref=_shard_window(ins[i], axes[i], t, shapes[i]), dst_ref=outs[i].at[q],
                    send_sem=send_sems.at[i * 4 + q], recv_sem=recv_sems.at[i * 4 + q],
                    device_id=sibling, device_id_type=MESH))
        for cp in copies:
            cp.start()
        for cp in copies:
            cp.wait_recv()
        for cp in copies:
            cp.wait_send()

    return pl.kernel(
        body, out_type=[jax.ShapeDtypeStruct((4,) + tuple(sh), BF) for sh in shapes], name=name,
        mesh=plsc.ScalarSubcoreMesh(axis_name="sequencer", num_cores=1),
        scratch_types=[pltpu.SemaphoreType.DMA((4 * n,)), pltpu.SemaphoreType.DMA((4 * n,))],
        compiler_params=pltpu.CompilerParams(collective_id=2),
    )(*grads)


def pair_add(grad, landed, axis, shape, c_idx, name, deps=()):
    R, C = shape

    def body(c_ref, g_ref, l_ref, o_ref):
        o_ref[...] = (g_ref[...].astype(F32) + l_ref[...].astype(F32)).astype(BF)

    if axis == 0:
        g_spec = pl.BlockSpec((R, C), lambda q, c_ref: (2 * q + c_ref[0], 0))
    else:
        g_spec = pl.BlockSpec((R, C), lambda q, c_ref: (0, 2 * q + c_ref[0]))
    blk = pl.BlockSpec((None, R, C), lambda q, c_ref: (q, 0, 0))
    return pl.pallas_call(
        _after(body, 3, deps), name=name,
        grid_spec=pltpu.PrefetchScalarGridSpec(
            num_scalar_prefetch=1, grid=(4,), in_specs=[g_spec, blk] + [ANY] * len(deps),
            out_specs=blk),
        out_shape=jax.ShapeDtypeStruct((4, R, C), BF),
        compiler_params=_cparams(("parallel",)),
    )(c_idx, grad, landed, *deps)


def chip_exchange_grads(parts, name):
    n = len(parts)

    def body(*refs):
        ins, outs = refs[:n], refs[n:2 * n]
        send_sems, recv_sems = refs[2 * n:]
        x, y, c = _me()
        barrier = pltpu.get_barrier_semaphore()
        for px, py in _other_chips(x, y):
            pl.semaphore_signal(barrier, inc=1, device_id=(px, py, c), device_id_type=MESH)
        pl.semaphore_wait(barrier, 3)
        copies = []
        for i in range(n):
            for k, (px, py) in enumerate(_other_chips(x, y)):
                copies.append(pltpu.make_async_remote_copy(
                    src_ref=ins[i].at[2 * px + py], dst_ref=outs[i].at[k],
                    send_sem=send_sems.at[i * 3 + k], recv_sem=recv_sems.at[i * 3 + k],
                    device_id=(px, py, c), device_id_type=MESH))
        for cp in copies:
            cp.start()
        for cp in copies:
            cp.wait_recv()
        for cp in copies:
            cp.wait_send()

    return pl.kernel(
        body, out_type=[jax.ShapeDtypeStruct((3,) + tuple(p.shape[1:]), BF) for p in parts], name=name,
        mesh=plsc.ScalarSubcoreMesh(axis_name="sequencer", num_cores=1),
        scratch_types=[pltpu.SemaphoreType.DMA((3 * n,)), pltpu.SemaphoreType.DMA((3 * n,))],
        compiler_params=pltpu.CompilerParams(collective_id=3),
    )(*parts)


def all_gather_small(v, name):
    R, C = v.shape

    def body(v_ref, out_ref, send_sems, recv_sems, local_sem):
        x, y, c = _me()
        me, sibling = (x, y, c), (x, y, 1 - c)
        chips = _other_chips(x, y)

        def slot(px, py, pc):
            return out_ref.at[4 * px + 2 * py + pc]

        def copy(k, block, to, src=None):
            return pltpu.make_async_remote_copy(
                src_ref=slot(*block) if src is None else src, dst_ref=slot(*block),
                send_sem=send_sems.at[k], recv_sem=recv_sems.at[k],
                device_id=to, device_id_type=MESH)

        mine = pltpu.make_async_copy(v_ref, slot(*me), local_sem)
        mine.start()
        first = [copy(0, me, sibling, src=v_ref)]
        first += [copy(1 + j, me, (*chip, c), src=v_ref) for j, chip in enumerate(chips)]
        for cp in first:
            cp.start()
        passed = [copy(4 + j, (*chip, c), sibling) for j, chip in enumerate(chips)]
        for j, chip in enumerate(chips):
            copy(1 + j, (*chip, c), me).wait_recv()
            passed[j].start()
        copy(0, sibling, me).wait_recv()
        for j, chip in enumerate(chips):
            copy(4 + j, (*chip, 1 - c), me).wait_recv()
        for cp in first + passed:
            cp.wait_send()
        mine.wait()

    return pl.pallas_call(
        body, name=name,
        in_specs=[pl.BlockSpec(memory_space=pltpu.VMEM)],
        out_specs=pl.BlockSpec(memory_space=pltpu.VMEM),
        out_shape=jax.ShapeDtypeStruct((NDEV, R, C), F32),
        scratch_shapes=[pltpu.SemaphoreType.DMA((7,)), pltpu.SemaphoreType.DMA((7,)),
                        pltpu.SemaphoreType.DMA],
    )(v)


def _adamw(w, g, m, v):
    m = ADAM_B1 * m + (1.0 - ADAM_B1) * g
    v = ADAM_B2 * v + (1.0 - ADAM_B2) * (g * g)
    m_hat = m / (1.0 - ADAM_B1 ** ADAM_STEP)
    v_hat = v / (1.0 - ADAM_B2 ** ADAM_STEP)
    delta = -ADAM_LR * (m_hat / (jnp.sqrt(v_hat) + ADAM_EPS) + ADAM_WD * w)
    return delta, m, v


def reduce_adamw(part, landed, w, m, v, layer, q_idx, name, prev=(), deps=()):
    R, C = part.shape[1:]
    r, c = w.shape[1:]
    tr = r // 2 if r % 16 == 0 and r >= 256 else r
    tR = tr if tr != r else R
    extra = tuple(prev) + tuple(deps)

    def body(q_ref, p_ref, l_ref, w_ref, m_ref, v_ref, g_out, d_out, m_out, v_out):
        g = p_ref[...].astype(F32)
        for k in range(3):
            g = g + l_ref[k].astype(F32)
        g = g[:tr, :c]
        d, mm, vv = _adamw(w_ref[...], g, m_ref[...], v_ref[...])
        g_out[...] = g
        d_out[...] = d
        m_out[...] = mm
        v_out[...] = vv

    wspec = pl.BlockSpec((None, tr, c), lambda i, q_ref: (layer, i, 0))
    out = jax.ShapeDtypeStruct(w.shape, F32)
    return pl.pallas_call(
        _after(body, 6, extra), name=name,
        grid_spec=pltpu.PrefetchScalarGridSpec(
            num_scalar_prefetch=1, grid=(r // tr,),
            in_specs=[pl.BlockSpec((None, tR, C), lambda i, q_ref: (q_ref[0], i, 0)),
                      pl.BlockSpec((3, tR, C), lambda i, q_ref: (0, i, 0)),
                      wspec, wspec, wspec] + [ANY] * len(extra),
            out_specs=[wspec] * 4),
        out_shape=[out] * 4,
        input_output_aliases={6 + k: k for k in range(len(prev))},
        compiler_params=_cparams(("parallel",)),
    )(q_idx, part, landed, w, m, v, *extra)


def small_reduce_adamw(gathered, w, m, v, name):
    R, C = w.shape

    def body(a_ref, w_ref, m_ref, v_ref, g_out, d_out, m_out, v_out):
        g = a_ref[0]
        for k in range(1, NDEV):
            g = g + a_ref[k]
        d, mm, vv = _adamw(w_ref[...], g, m_ref[...], v_ref[...])
        g_out[...] = g
        d_out[...] = d
        m_out[...] = mm
        v_out[...] = vv

    out = jax.ShapeDtypeStruct((R, C), F32)
    return pl.pallas_call(body, name=name, out_shape=[out] * 4,
                          compiler_params=_cparams())(gathered, w, m, v)


def _pad_cols(a, n):
    return jnp.pad(a, ((0, 0), (0, n - a.shape[1])))


def _pad_rows(a, n):
    return jnp.pad(a, ((0, n - a.shape[0]), (0, 0)))


SMALL_ROWS = 16


def _pack_small(mix, ffn, fin, taps_full, relb):
    return jnp.concatenate([
        mix, ffn, fin.reshape(1, D), taps_full.reshape(6, D),
        jnp.pad(relb.reshape(1, NUM_BUCKETS * H), ((0, 0), (0, D - NUM_BUCKETS * H)))], axis=0)


def kernel(x, mix_norm, ffn_norm, final_norm, conv_w_in, conv_kernel, conv_w_out, attn_w_qkv, attn_w_out, rel_bias, ffn_w_gate, ffn_w_up, ffn_w_down, loss_target, m_mix_norm, m_ffn_norm, m_final_norm, m_conv_w_in, m_conv_kernel, m_conv_w_out, m_attn_w_qkv, m_attn_w_out, m_rel_bias, m_ffn_w_gate, m_ffn_w_up, m_ffn_w_down, v_mix_norm, v_ffn_norm, v_final_norm, v_conv_w_in, v_conv_kernel, v_conv_w_out, v_attn_w_qkv, v_attn_w_out, v_rel_bias, v_ffn_w_gate, v_ffn_w_up, v_ffn_w_down):
    xi, yi, ci = _me()
    me = 4 * xi + 2 * yi + ci
    c_idx = jnp.reshape(ci, (1,)).astype(jnp.int32)
    q_idx = jnp.reshape(2 * xi + yi, (1,)).astype(jnp.int32)
    col0 = me * (D // NDEV)

    taps_local = jnp.zeros((2, 3, D), F32)
    taps_local = lax.dynamic_update_slice(taps_local, conv_kernel, (0, 0, col0))
    taps_pack = jnp.pad(taps_local.reshape(6, D), ((0, 2), (0, 0)))
    taps_all = all_gather_small(taps_pack, "ag_taps")
    taps_sum = jnp.sum(taps_all, axis=0)
    taps = [jnp.pad(taps_sum[3 * j:3 * j + 3], ((0, 5), (0, 0))) for j in range(2)]

    gate_t, up_t = jnp.swapaxes(ffn_w_gate, 1, 2), jnp.swapaxes(ffn_w_up, 1, 2)
    m_gate_t, m_up_t = jnp.swapaxes(m_ffn_w_gate, 1, 2), jnp.swapaxes(m_ffn_w_up, 1, 2)
    v_gate_t, v_up_t = jnp.swapaxes(v_ffn_w_gate, 1, 2), jnp.swapaxes(v_ffn_w_up, 1, 2)

    mixer_in = (conv_w_in, attn_w_qkv)
    mixer_out = (conv_w_out, attn_w_out)
    wts = []
    for i in range(DEPTH):
        j = i // 2
        shards = [mixer_in[i % 2][j].astype(BF), mixer_out[i % 2][j].astype(BF),
                  _pad_rows(gate_t[i].astype(BF), FF_SHARD_PAD),
                  _pad_rows(up_t[i].astype(BF), FF_SHARD_PAD),
                  _pad_rows(ffn_w_down[i].astype(BF), FF_SHARD_PAD)]
        if i == 0:
            head = all_gather_weights(shards[:1], (1,), "ag_weights_l0_in")
            rest = all_gather_weights(shards[1:], (0, 0, 0, 0), "ag_weights_l0")
            wts.append(list(head) + list(rest))
        else:
            wts.append(all_gather_weights(shards, (1, 0, 0, 0, 0), f"ag_weights_l{i}"))

    onehot_t, band = _bucket_onehot_t()
    bias3 = bias_tables(rel_bias.T, onehot_t, band, "bias_tables").reshape(3, H * BLK, 2 * BLK)

    saved = []
    xc = x[0]
    for i in range(DEPTH):
        w_in, w_out, w_g, w_u, w_d = wts[i]
        j = i // 2
        x_mix = xc
        z3, h_mix = norm_matmul3(xc, mix_norm[i:i + 1], w_in, f"mix_in_l{i}")
        if i % 2 == 0:
            act = conv_fwd(z3, taps[j], f"conv_fwd_l{i}")
            lse_b = None
        else:
            act, lse_b = attention_fwd(z3, bias3, f"attn_fwd_l{i}")
        xc = matmul_residual(act, w_out, xc, f"mix_out_l{i}")
        x_ffn = xc
        g, u, a, h_ffn = norm_swiglu_up(xc, ffn_norm[i:i + 1], w_g, w_u, f"ffn_up_l{i}")
        xc = matmul_residual(a, w_d, xc, f"ffn_down_l{i}")
        saved.append((x_mix, h_mix, z3, act, lse_b, x_ffn, h_ffn, g, u, a))

    dx, dxb, dg_final, sq = loss_head(xc, final_norm.reshape(1, D), loss_target[0], "loss_head")
    loss = lax.psum(0.5 * jnp.sum(sq[0]) / D, ("x", "y", "c"))

    dg_mix = [None] * DEPTH
    dg_ffn = [None] * DEPTH
    dtaps = [None, None]
    dbias_all = []
    shape_in, shape_out = (D, 3 * D // NDEV), (D // NDEV, D)
    ffn_axes, ffn_shapes = (0, 0, 0), ((FF_SHARD_PAD, D),) * 3
    stacked = {}

    def pair_stage(grads, landed1, axes, shapes, tag, tok):
        parts = []
        for t in range(len(grads)):
            parts.append(pair_add(grads[t], landed1[t], axes[t], shapes[t], c_idx,
                                  f"rs_add_{tag}_{t}", deps=[tok]))
            tok = parts[-1]
        return parts, chip_exchange_grads(parts, f"rs_chip_{tag}"), tok

    def adamw_stage(parts, landed2, params, tag, tok):
        for t, (pname, w_, m_, v_, layer) in enumerate(params):
            res = reduce_adamw(parts[t], landed2[t], w_, m_, v_, layer, q_idx, f"adamw_{tag}_{t}",
                               prev=stacked.get(pname, ()), deps=[tok])
            stacked[pname] = res
            tok = res[0]
        return tok

    tok = dxb
    mix_wait = None
    mix_chip = None
    ffn_chip = None
    for i in reversed(range(DEPTH)):
        w_in, w_out, w_g, w_u, w_d = wts[i]
        j = i // 2
        x_mix, h_mix, z3, act, lse_b, x_ffn, h_ffn, g, u, a = saved[i]
        ffn_params = [("ffn_w_gate", gate_t, m_gate_t, v_gate_t, i),
                      ("ffn_w_up", up_t, m_up_t, v_up_t, i),
                      ("ffn_w_down", ffn_w_down, m_ffn_w_down, v_ffn_w_down, i)]
        if i % 2 == 0:
            mix_params = [("conv_w_in", conv_w_in, m_conv_w_in, v_conv_w_in, j),
                          ("conv_w_out", conv_w_out, m_conv_w_out, v_conv_w_out, j)]
        else:
            mix_params = [("attn_w_qkv", attn_w_qkv, m_attn_w_qkv, v_attn_w_qkv, j),
                          ("attn_w_out", attn_w_out, m_attn_w_out, v_attn_w_out, j)]
        dgate, dup = swiglu_bwd_da(dxb, w_d, g, u, f"ffn_da_l{i}", deps=[tok])
        gw_d = matmul_tn(a, dxb, f"ffn_dwd_l{i}", deps=[dgate])
        tok = gw_d
        if mix_wait is not None:
            grads_m, landed1_m, params_m, tag_m = mix_wait
            parts_m, landed2_m, tok = pair_stage(grads_m, landed1_m, (1, 0), (shape_in, shape_out),
                                                 tag_m, tok)
            mix_chip = (parts_m, landed2_m, params_m, tag_m)
            mix_wait = None
        gw_g = matmul_tn(dgate, h_ffn, f"ffn_dwg_l{i}", deps=[tok])
        gw_u = matmul_tn(dup, h_ffn, f"ffn_dwu_l{i}", deps=[gw_g])
        grads_f = [gw_g, gw_u, gw_d]
        landed1_f = pair_exchange_grads(grads_f, ffn_axes, ffn_shapes, f"rs_pair_f{i}")
        tok = gw_u
        if ffn_chip is not None:
            tok = adamw_stage(*ffn_chip, tok)
            ffn_chip = None
        dx, dxb, dg_ffn[i] = matmul_normbwd(
            [(dgate, w_g, False), (dup, w_u, False)], x_ffn, ffn_norm[i:i + 1], dx, f"ffn_dh_l{i}",
            deps=[tok])
        gw_out = matmul_tn(act, dxb, f"mix_dwout_l{i}")
        if i % 2 == 0:
            dact = matmul_nt(dxb, w_out, f"mix_dact_l{i}", out_dtype=F32, deps=[gw_out])
        else:
            dact = matmul_nt(dxb, w_out, f"mix_dact_l{i}", out_dtype=BF, deps=[gw_out])
        parts_f, landed2_f, tok = pair_stage(grads_f, landed1_f, ffn_axes, ffn_shapes, f"f{i}", dact)
        ffn_chip = (parts_f, landed2_f, ffn_params, f"f{i}")
        if i % 2 == 0:
            dz3, dtaps[j] = conv_bwd(dact, z3, taps[j], f"conv_bwd_l{i}", deps=[tok])
        else:
            dz3, dbias3 = attention_bwd(z3, dact, act, lse_b, bias3, f"attn_bwd_l{i}", deps=[tok])
            dbias_all.append(dbias3.reshape(3, H, BLK * 2 * BLK))
        gw_in = matmul_tn(h_mix, dz3, f"mix_dwin_l{i}")
        grads_m = [gw_in, gw_out]
        landed1_m = pair_exchange_grads(grads_m, (1, 0), (shape_in, shape_out), f"rs_pair_m{i}")
        mix_wait = (grads_m, landed1_m, mix_params, f"m{i}")
        tok = gw_in
        if mix_chip is not None:
            tok = adamw_stage(*mix_chip, tok)
            mix_chip = None
        dx, dxb, dg_mix[i] = matmul_normbwd(
            [(dz3, w_in, True)], x_mix, mix_norm[i:i + 1], dx, f"mix_dh_l{i}", deps=[tok])
        tok = dxb
    grads_m, landed1_m, params_m, tag_m = mix_wait
    parts_m, landed2_m, tok = pair_stage(grads_m, landed1_m, (1, 0), (shape_in, shape_out), tag_m, tok)
    tok = adamw_stage(*ffn_chip, tok)

    grad_relb_t = bias_grad(jnp.concatenate(dbias_all), onehot_t, "bias_grad")
    dtaps_full = jnp.stack([dtaps[0][:3], dtaps[1][:3]])
    g_small = _pack_small(jnp.concatenate([d[0:1] for d in dg_mix], axis=0),
                          jnp.concatenate([d[0:1] for d in dg_ffn], axis=0),
                          dg_final[0], dtaps_full, grad_relb_t.T)
    gathered = all_gather_small(g_small, "ag_small_grads")

    def taps_at_cols(k):
        return lax.dynamic_update_slice(jnp.zeros((2, 3, D), F32), k, (0, 0, col0))

    w_small = _pack_small(mix_norm, ffn_norm, final_norm, taps_at_cols(conv_kernel), rel_bias)
    m_small = _pack_small(m_mix_norm, m_ffn_norm, m_final_norm, taps_at_cols(m_conv_kernel), m_rel_bias)
    v_small = _pack_small(v_mix_norm, v_ffn_norm, v_final_norm, taps_at_cols(v_conv_kernel), v_rel_bias)
    small = small_reduce_adamw(gathered, w_small, m_small, v_small, "adamw_small")

    def unpack_small(p):
        taps_p = lax.dynamic_slice(p[9:15].reshape(2, 3, D), (0, 0, col0), (2, 3, D // NDEV))
        return {"mix_norm": p[0:4], "ffn_norm": p[4:8], "final_norm": p[8],
                "conv_kernel": taps_p, "rel_bias": p[15, :NUM_BUCKETS * H].reshape(NUM_BUCKETS, H)}

    small_out = [unpack_small(p) for p in small]
    adamw_stage(parts_m, landed2_m, params_m, tag_m, small[0])

    names = ["mix_norm", "ffn_norm", "final_norm", "conv_w_in", "conv_kernel", "conv_w_out",
             "attn_w_qkv", "attn_w_out", "rel_bias", "ffn_w_gate", "ffn_w_up", "ffn_w_down"]
    outs = [loss, dx.reshape(1, S, D)]
    for o in range(4):
        for nme in names:
            if nme in ("ffn_w_gate", "ffn_w_up"):
                outs.append(jnp.swapaxes(stacked[nme][o], 1, 2))
            else:
                outs.append(stacked[nme][o] if nme in stacked else small_out[o][nme])
    return tuple(outs)
```

```python
import functools
import math

import jax
import jax.numpy as jnp
from jax import lax
from jax.experimental import pallas as pl
from jax.experimental.pallas import tpu as pltpu
from jax.experimental.pallas import tpu_sc as plsc

S = 2048
D = 1024
H = 16
DH = 64
DFF = 2816
NDEV = 8
DEPTH = 4
FF_SHARD = DFF // NDEV
FF_SHARD_PAD = 384
DFF_PAD = FF_SHARD_PAD * NDEV
BLK = 128
BRANCH_DILATIONS = (1, 4, 16)
NUM_BUCKETS = 32
MAX_DISTANCE = 2048
EPS = 1e-6
NEG_INF = -1e30
SCALE = DH ** -0.5

ADAM_LR = 0.001
ADAM_B1 = 0.9
ADAM_B2 = 0.999
ADAM_EPS = 1e-08
ADAM_WD = 0.01
ADAM_STEP = 10

BF = jnp.bfloat16
F32 = jnp.float32
VMEM_LIMIT_BYTES = 56 * 1024 * 1024
MESH = pl.DeviceIdType.MESH
ANY = pl.BlockSpec(memory_space=pl.ANY)

_NT = (((1,), (1,)), ((), ()))
_TN = (((0,), (0,)), ((), ()))


def _cparams(sem=None):
    return pltpu.CompilerParams(dimension_semantics=sem, vmem_limit_bytes=VMEM_LIMIT_BYTES)


def _after(body, n, deps):
    nd = len(deps)
    if nd == 0:
        return body

    def ordered(*refs):
        body(*refs[:n], *refs[n + nd:])
    return ordered


def _rms(x):
    return lax.rsqrt(jnp.mean(x * x, axis=-1, keepdims=True) + EPS)


def norm_matmul3(x, gain, w, name, tm=1024, tn=512):
    per = D // tn

    def body(x_ref, g_ref, w_ref, z_ref, h_ref, hs_ref):
        @pl.when(pl.program_id(1) == 0)
        def _():
            xv = x_ref[...]
            hv = (xv * _rms(xv) * g_ref[...]).astype(BF)
            hs_ref[...] = hv
            h_ref[...] = hv
        z_ref[...] = jnp.dot(hs_ref[...], w_ref[...], preferred_element_type=F32).astype(BF)

    return pl.pallas_call(
        body, name=name,
        grid=(S // tm, 3 * D // tn),
        in_specs=[pl.BlockSpec((tm, D), lambda i, j: (i, 0)),
                  pl.BlockSpec((1, D), lambda i, j: (0, 0)),
                  pl.BlockSpec((D, tn), lambda i, j: (0, j))],
        out_specs=[pl.BlockSpec((None, tm, tn), lambda i, j: (j // per, i, j % per)),
                   pl.BlockSpec((tm, D), lambda i, j: (i, 0))],
        out_shape=[jax.ShapeDtypeStruct((3, S, D), BF), jax.ShapeDtypeStruct((S, D), BF)],
        scratch_shapes=[pltpu.VMEM((tm, D), BF)],
        compiler_params=_cparams(("parallel", "arbitrary")),
    )(x, gain, w)


def norm_swiglu_up(x, gain, wg_t, wu_t, name, tm=1024, tn=512):
    def body(x_ref, g_ref, wg_ref, wu_ref, go_ref, uo_ref, ao_ref, h_ref, hs_ref):
        @pl.when(pl.program_id(1) == 0)
        def _():
            xv = x_ref[...]
            hv = (xv * _rms(xv) * g_ref[...]).astype(BF)
            hs_ref[...] = hv
            h_ref[...] = hv
        hv = hs_ref[...]
        g = lax.dot_general(hv, wg_ref[...], _NT, preferred_element_type=F32)
        u = lax.dot_general(hv, wu_ref[...], _NT, preferred_element_type=F32)
        go_ref[...] = g.astype(BF)
        uo_ref[...] = u.astype(BF)
        ao_ref[...] = (g * jax.nn.sigmoid(g) * u).astype(BF)

    act = jax.ShapeDtypeStruct((S, DFF_PAD), BF)
    blk = pl.BlockSpec((tm, tn), lambda i, j: (i, j))
    return pl.pallas_call(
        body, name=name,
        grid=(S // tm, DFF_PAD // tn),
        in_specs=[pl.BlockSpec((tm, D), lambda i, j: (i, 0)),
                  pl.BlockSpec((1, D), lambda i, j: (0, 0)),
                  pl.BlockSpec((tn, D), lambda i, j: (j, 0)),
                  pl.BlockSpec((tn, D), lambda i, j: (j, 0))],
        out_specs=[blk, blk, blk, pl.BlockSpec((tm, D), lambda i, j: (i, 0))],
        out_shape=[act, act, act, jax.ShapeDtypeStruct((S, D), BF)],
        scratch_shapes=[pltpu.VMEM((tm, D), BF)],
        compiler_params=_cparams(("parallel", "arbitrary")),
    )(x, gain, wg_t, wu_t)


def matmul_residual(a, w, x, name, tm=1024, tn=512):
    K = a.shape[1]

    def body(a_ref, w_ref, x_ref, o_ref):
        o_ref[...] = x_ref[...] + jnp.dot(a_ref[...], w_ref[...], preferred_element_type=F32)

    return pl.pallas_call(
        body, name=name,
        grid=(S // tm, D // tn),
        in_specs=[pl.BlockSpec((tm, K), lambda i, j: (i, 0)),
                  pl.BlockSpec((K, tn), lambda i, j: (0, j)),
                  pl.BlockSpec((tm, tn), lambda i, j: (i, j))],
        out_specs=pl.BlockSpec((tm, tn), lambda i, j: (i, j)),
        out_shape=jax.ShapeDtypeStruct((S, D), F32),
        compiler_params=_cparams(("parallel", "parallel")),
    )(a, w, x)


def matmul_nt(a, w, name, out_dtype=BF, tm=1024, tn=512, deps=()):
    K = a.shape[1]
    N = w.shape[0]

    def body(a_ref, w_ref, o_ref):
        o_ref[...] = lax.dot_general(a_ref[...], w_ref[...], _NT,
                                     preferred_element_type=F32).astype(o_ref.dtype)

    return pl.pallas_call(
        _after(body, 2, deps), name=name,
        grid=(S // tm, N // tn),
        in_specs=[pl.BlockSpec((tm, K), lambda i, j: (i, 0)),
                  pl.BlockSpec((tn, K), lambda i, j: (j, 0))] + [ANY] * len(deps),
        out_specs=pl.BlockSpec((tm, tn), lambda i, j: (i, j)),
        out_shape=jax.ShapeDtypeStruct((S, N), out_dtype),
        compiler_params=_cparams(("parallel", "parallel")),
    )(a, w, *deps)


def swiglu_bwd_da(dxb, wd, g, u, name, tm=1024, tn=512, deps=()):
    def body(dx_ref, w_ref, g_ref, u_ref, dg_ref, du_ref):
        da = lax.dot_general(dx_ref[...], w_ref[...], _NT, preferred_element_type=F32)
        gv = g_ref[...].astype(F32)
        uv = u_ref[...].astype(F32)
        sig = jax.nn.sigmoid(gv)
        dg_ref[...] = (da * uv * (sig * (1.0 + gv * (1.0 - sig)))).astype(BF)
        du_ref[...] = (da * (gv * sig)).astype(BF)

    act = jax.ShapeDtypeStruct((S, DFF_PAD), BF)
    blk = pl.BlockSpec((tm, tn), lambda i, j: (i, j))
    return pl.pallas_call(
        _after(body, 4, deps), name=name,
        grid=(S // tm, DFF_PAD // tn),
        in_specs=[pl.BlockSpec((tm, D), lambda i, j: (i, 0)),
                  pl.BlockSpec((tn, D), lambda i, j: (j, 0)),
                  blk, blk] + [ANY] * len(deps),
        out_specs=[blk, blk],
        out_shape=[act, act],
        compiler_params=_cparams(("parallel", "parallel")),
    )(dxb, wd, g, u, *deps)


def matmul_tn(a, b, name, tm=1024, tn=512, deps=()):
    M = a.shape[1]
    if b.ndim == 3:
        per = D // tn
        N = 3 * D
        b_spec = pl.BlockSpec((None, S, tn), lambda i, j: (j // per, 0, j % per))
    else:
        N = b.shape[1]
        b_spec = pl.BlockSpec((S, tn), lambda i, j: (0, j))

    def body(a_ref, b_ref, o_ref):
        o_ref[...] = lax.dot_general(a_ref[...], b_ref[...], _TN,
                                     preferred_element_type=F32).astype(BF)

    return pl.pallas_call(
        _after(body, 2, deps), name=name,
        grid=(M // tm, N // tn),
        in_specs=[pl.BlockSpec((S, tm), lambda i, j: (0, i)), b_spec] + [ANY] * len(deps),
        out_specs=pl.BlockSpec((tm, tn), lambda i, j: (i, j)),
        out_shape=jax.ShapeDtypeStruct((M, N), BF),
        compiler_params=_cparams(("parallel", "parallel")),
    )(a, b, *deps)


def matmul_normbwd(terms, x_in, gain, dx, name, tm=512, ch=256, deps=()):
    specs, operands = [], []
    for (a, w, stacked) in terms:
        if stacked:
            specs.append(pl.BlockSpec((3, tm, D), lambda i: (0, i, 0)))
        else:
            specs.append(pl.BlockSpec((tm, a.shape[1]), lambda i: (i, 0)))
        specs.append(pl.BlockSpec(w.shape, lambda i: (0, 0), pipeline_mode=pl.Buffered(1)))
        operands += [a, w]
    nt = len(terms)

    def body(*refs):
        aw = refs[:2 * nt]
        x_ref, g_ref, dx_ref, dxo_ref, dxb_ref, dg_ref, acc_ref = refs[2 * nt:]

        @pl.when(pl.program_id(0) == 0)
        def _():
            dg_ref[...] = jnp.zeros_like(dg_ref)

        dh = None
        for t, (_, _, stacked) in enumerate(terms):
            a_ref, w_ref = aw[2 * t], aw[2 * t + 1]
            if stacked:
                parts = [lax.dot_general(a_ref[k], w_ref[:, k * D:(k + 1) * D], _NT,
                                         preferred_element_type=F32) for k in range(3)]
            else:
                parts = [jnp.dot(a_ref[...], w_ref[...], preferred_element_type=F32)]
            for p in parts:
                dh = p if dh is None else dh + p
        acc_ref[...] = dh

        def chunk(c, carry):
            rows = pl.ds(pl.multiple_of(c * ch, ch), ch)
            xv = x_ref[rows, :]
            r = _rms(xv)
            xhat = xv * r
            dhc = acc_ref[rows, :]
            dg_ref[0:1, :] += jnp.sum(dhc * xhat, axis=0, keepdims=True)
            dxh = dhc * g_ref[...]
            dxn = r * (dxh - xhat * jnp.mean(dxh * xhat, axis=-1, keepdims=True))
            out = dx_ref[rows, :] + dxn
            dxo_ref[rows, :] = out
            dxb_ref[rows, :] = out.astype(BF)
            return carry
        lax.fori_loop(0, tm // ch, chunk, 0)

    row = pl.BlockSpec((tm, D), lambda i: (i, 0))
    return pl.pallas_call(
        _after(body, 2 * nt + 3, deps), name=name,
        grid=(S // tm,),
        in_specs=specs + [row, pl.BlockSpec((1, D), lambda i: (0, 0)), row] + [ANY] * len(deps),
        out_specs=[row, row, pl.BlockSpec((8, D), lambda i: (0, 0))],
        out_shape=[jax.ShapeDtypeStruct((S, D), F32), jax.ShapeDtypeStruct((S, D), BF),
                   jax.ShapeDtypeStruct((8, D), F32)],
        scratch_shapes=[pltpu.VMEM((tm, D), F32)],
        compiler_params=_cparams(("arbitrary",)),
    )(*operands, x_in, gain, dx, *deps)


def loss_head(x, gain, target, name, tm=512):
    def body(x_ref, g_ref, t_ref, dxo_ref, dxb_ref, dg_ref, sq_ref):
        @pl.when(pl.program_id(0) == 0)
        def _():
            dg_ref[...] = jnp.zeros_like(dg_ref)
            sq_ref[...] = jnp.zeros_like(sq_ref)
        xv = x_ref[...]
        r = _rms(xv)
        xhat = xv * r
        err = xhat * g_ref[...] - t_ref[...]
        sq_ref[0:1, :] += jnp.sum(err * err, axis=0, keepdims=True)
        dy = err * (1.0 / D)
        dg_ref[0:1, :] += jnp.sum(dy * xhat, axis=0, keepdims=True)
        dxh = dy * g_ref[...]
        out = r * (dxh - xhat * jnp.mean(dxh * xhat, axis=-1, keepdims=True))
        dxo_ref[...] = out
        dxb_ref[...] = out.astype(BF)

    row = pl.BlockSpec((tm, D), lambda i: (i, 0))
    acc = pl.BlockSpec((8, D), lambda i: (0, 0))
    return pl.pallas_call(
        body, name=name,
        grid=(S // tm,),
        in_specs=[row, pl.BlockSpec((1, D), lambda i: (0, 0)), row],
        out_specs=[row, row, acc, acc],
        out_shape=[jax.ShapeDtypeStruct((S, D), F32), jax.ShapeDtypeStruct((S, D), BF),
                   jax.ShapeDtypeStruct((8, D), F32), jax.ShapeDtypeStruct((8, D), F32)],
        compiler_params=_cparams(("arbitrary",)),
    )(x, gain, target)


def _shift_down(p, n, row):
    return jnp.where(row >= n, pltpu.roll(p, n, axis=0), 0.0)


def _shift_up(p, n, row):
    return jnp.where(row < S - n, pltpu.roll(p, S - n, axis=0), 0.0)


def conv_fwd(z3, taps, name, tn=128):
    def body(z_ref, k_ref, m_ref):
        b = z_ref[0].astype(F32)
        p = z_ref[1].astype(F32) * z_ref[2].astype(F32)
        row = lax.broadcasted_iota(jnp.int32, p.shape, 0)
        y = (k_ref[2:3, :] * p + k_ref[1:2, :] * _shift_down(p, 1, row)
             + k_ref[0:1, :] * _shift_down(p, 2, row))
        m_ref[...] = (b * y).astype(BF)

    return pl.pallas_call(
        body, name=name,
        grid=(D // tn,),
        in_specs=[pl.BlockSpec((3, S, tn), lambda j: (0, 0, j)),
                  pl.BlockSpec((8, tn), lambda j: (0, j))],
        out_specs=pl.BlockSpec((S, tn), lambda j: (0, j)),
        out_shape=jax.ShapeDtypeStruct((S, D), BF),
        compiler_params=_cparams(("parallel",)),
    )(z3, taps)


def conv_bwd(dm, z3, taps, name, tn=128, deps=()):
    def body(dm_ref, z_ref, k_ref, dz_ref, dk_ref):
        dmv = dm_ref[...]
        b = z_ref[0].astype(F32)
        c = z_ref[1].astype(F32)
        u = z_ref[2].astype(F32)
        p = c * u
        row = lax.broadcasted_iota(jnp.int32, p.shape, 0)
        p1 = _shift_down(p, 1, row)
        p2 = _shift_down(p, 2, row)
        y = k_ref[2:3, :] * p + k_ref[1:2, :] * p1 + k_ref[0:1, :] * p2
        dy = dmv * b
        dz_ref[0] = (dmv * y).astype(BF)
        dp = (k_ref[2:3, :] * dy + k_ref[1:2, :] * _shift_up(dy, 1, row)
              + k_ref[0:1, :] * _shift_up(dy, 2, row))
        dz_ref[1] = (dp * u).astype(BF)
        dz_ref[2] = (dp * c).astype(BF)
        dk_ref[...] = jnp.zeros_like(dk_ref)
        dk_ref[0:1, :] = jnp.sum(dy * p2, axis=0, keepdims=True)
        dk_ref[1:2, :] = jnp.sum(dy * p1, axis=0, keepdims=True)
        dk_ref[2:3, :] = jnp.sum(dy * p, axis=0, keepdims=True)

    return pl.pallas_call(
        _after(body, 3, deps), name=name,
        grid=(D // tn,),
        in_specs=[pl.BlockSpec((S, tn), lambda j: (0, j)),
                  pl.BlockSpec((3, S, tn), lambda j: (0, 0, j)),
                  pl.BlockSpec((8, tn), lambda j: (0, j))] + [ANY] * len(deps),
        out_specs=[pl.BlockSpec((3, S, tn), lambda j: (0, 0, j)),
                   pl.BlockSpec((8, tn), lambda j: (0, j))],
        out_shape=[jax.ShapeDtypeStruct((3, S, D), BF), jax.ShapeDtypeStruct((8, D), F32)],
        compiler_params=_cparams(("parallel",)),
    )(dm, z3, taps, *deps)


def _t5_bucket(dist):
    exact = NUM_BUCKETS // 2
    df = jnp.maximum(dist, 1).astype(jnp.float32)
    large = exact + (jnp.log(df / exact) / math.log(MAX_DISTANCE / exact)
                     * (NUM_BUCKETS - exact)).astype(jnp.int32)
    large = jnp.minimum(large, NUM_BUCKETS - 1)
    return jnp.where(dist < exact, dist, large)


def _bucket_onehot_t():
    qi = jnp.arange(BLK)[:, None]
    ki = jnp.arange(2 * BLK)[None, :]
    rel = qi + BLK - ki
    band = ((rel >= 0) & (rel <= BLK)).reshape(1, -1).astype(F32)
    hots = []
    for d in BRANCH_DILATIONS:
        bucket = _t5_bucket(jnp.clip(rel, 0) * d).reshape(1, -1)
        hots.append((jnp.arange(NUM_BUCKETS)[:, None] == bucket).astype(F32))
    return jnp.stack(hots), band


def bias_tables(rel_bias_t, onehot_t, band, name):
    def body(rb_ref, oh_ref, band_ref, o_ref):
        b = jnp.dot(rb_ref[...], oh_ref[...], preferred_element_type=F32,
                    precision=lax.Precision.HIGHEST)
        o_ref[...] = jnp.where(band_ref[...] > 0.5, b, NEG_INF)

    n = BLK * 2 * BLK
    return pl.pallas_call(
        body, name=name,
        grid=(3,),
        in_specs=[pl.BlockSpec((H, NUM_BUCKETS), lambda g: (0, 0)),
                  pl.BlockSpec((None, NUM_BUCKETS, n), lambda g: (g, 0, 0)),
                  pl.BlockSpec((1, n), lambda g: (0, 0))],
        out_specs=pl.BlockSpec((None, H, n), lambda g: (g, 0, 0)),
        out_shape=jax.ShapeDtypeStruct((3, H, n), F32),
        compiler_params=_cparams(("parallel",)),
    )(rel_bias_t, onehot_t, band)


def bias_grad(dbias, onehot_t, name):
    def body(db_ref, oh_ref, o_ref):
        @pl.when(pl.program_id(0) == 0)
        def _():
            o_ref[...] = jnp.zeros_like(o_ref)
        o_ref[...] += lax.dot_general(db_ref[...], oh_ref[...], _NT, preferred_element_type=F32,
                                      precision=lax.Precision.HIGHEST)

    n = BLK * 2 * BLK
    return pl.pallas_call(
        body, name=name,
        grid=(dbias.shape[0],),
        in_specs=[pl.BlockSpec((None, H, n), lambda g: (g, 0, 0)),
                  pl.BlockSpec((None, NUM_BUCKETS, n), lambda g: (g % 3, 0, 0))],
        out_specs=pl.BlockSpec((H, NUM_BUCKETS), lambda g: (0, 0)),
        out_shape=jax.ShapeDtypeStruct((H, NUM_BUCKETS), F32),
        compiler_params=_cparams(("arbitrary",)),
    )(dbias, onehot_t)


def _head_masks():
    lane = lax.broadcasted_iota(jnp.int32, (1, 2 * DH), 1)
    return (lane < DH, lane >= DH)


def _stack_heads(x, masks):
    zero = jnp.zeros_like(x)
    return jnp.concatenate([jnp.where(masks[0], x, zero), jnp.where(masks[1], x, zero)], axis=0)


def _deinterleave(src_ref, dst_ref, d, dtype):
    L = S // d
    for r in range(d):
        dst_ref[r * L:(r + 1) * L, :] = src_ref[pl.ds(r, L, stride=d), :].astype(dtype)


def _branch_loops(d, block):
    L = S // d
    nb = L // BLK

    def first(base):
        block(base, base, BLK, True)

    def later(base, n):
        q0 = pl.multiple_of(base + n * BLK, BLK)
        block(q0, pl.multiple_of(q0 - BLK, BLK), 2 * BLK, False)

    if d == 1:
        unroll = 3
        assert (nb - 1) % unroll == 0
        first(0)

        def trip(it, c):
            for u in range(unroll):
                later(0, 1 + it * unroll + u)
            return c
        lax.fori_loop(0, (nb - 1) // unroll, trip, 0)
    elif nb > 1:
        def residue(r, c):
            base = pl.multiple_of(r * L, BLK)
            first(base)
            for n in range(1, nb):
                later(base, n)
            return c
        lax.fori_loop(0, d, residue, 0)
    else:
        unroll = 4
        assert d % unroll == 0

        def trip(it, c):
            for u in range(unroll):
                first(pl.multiple_of((it * unroll + u) * L, BLK))
            return c
        lax.fori_loop(0, d // unroll, trip, 0)


def attention_fwd(z3, bias3, name):
    W = 2 * DH
    CH = 256

    def body(q_ref, k_ref, v_ref, b_ref, o_ref, lse_ref, stage, qd, kd, vd, od, ld, on, ln):
        masks = _head_masks()
        for src, dst in ((q_ref, qd), (k_ref, kd), (v_ref, vd)):
            stage[...] = src[...].astype(F32)
            for gi, d in enumerate(BRANCH_DILATIONS[1:]):
                _deinterleave(stage, dst.at[gi], d, BF)

        for g, d in enumerate(BRANCH_DILATIONS):
            qs, ks, vs = (q_ref, k_ref, v_ref) if d == 1 else (qd.at[g - 1], kd.at[g - 1], vd.at[g - 1])
            o_dst, l_dst = (on.at[0], ln.at[0]) if d == 1 else (od, ld)

            def block(q0, k0, nk, first, g=g, qs=qs, ks=ks, vs=vs, o_dst=o_dst, l_dst=l_dst):
                q2 = _stack_heads(qs[pl.ds(q0, BLK), :], masks)
                kk = ks[pl.ds(k0, nk), :]
                vv = vs[pl.ds(k0, nk), :]
                bias = b_ref[g][:, BLK:] if first else b_ref[g]
                s = lax.dot_general(q2, kk, _NT, preferred_element_type=F32) * SCALE + bias
                mx = jnp.max(s, axis=1, keepdims=True)
                p = jnp.exp(s - mx)
                l = jnp.sum(p, axis=1, keepdims=True)
                o2 = jnp.dot(p.astype(BF), vv, preferred_element_type=F32) / l
                lse2 = mx + jnp.log(l)
                o_dst[pl.ds(q0, BLK), :] = jnp.where(masks[0], o2[:BLK], o2[BLK:])
                l_dst[pl.ds(q0, BLK), :] = jnp.where(masks[0], lse2[:BLK], lse2[BLK:])

            _branch_loops(d, block)
            if d > 1:
                L = S // d
                for r in range(d):
                    on[g, pl.ds(r, L, stride=d), :] = od[r * L:(r + 1) * L, :]
                    ln[g, pl.ds(r, L, stride=d), :] = ld[r * L:(r + 1) * L, :]

        def join(c, carry):
            rows = pl.ds(pl.multiple_of(c * CH, CH), CH)
            a, b, cc = ln[0, rows, :], ln[1, rows, :], ln[2, rows, :]
            mx = jnp.maximum(jnp.maximum(a, b), cc)
            ea, eb, ec = jnp.exp(a - mx), jnp.exp(b - mx), jnp.exp(cc - mx)
            tot = ea + eb + ec
            o_ref[rows, :] = ((ea * on[0, rows, :] + eb * on[1, rows, :] + ec * on[2, rows, :])
                              / tot).astype(BF)
            lse_ref[rows, :] = mx + jnp.log(tot)
            return carry
        lax.fori_loop(0, S // CH, join, 0)

    col = pl.BlockSpec((S, W), lambda hp: (0, hp))
    return pl.pallas_call(
        body, name=name,
        grid=(D // W,),
        in_specs=[pl.BlockSpec((None, S, W), lambda hp: (0, 0, hp)),
                  pl.BlockSpec((None, S, W), lambda hp: (1, 0, hp)),
                  pl.BlockSpec((None, S, W), lambda hp: (2, 0, hp)),
                  pl.BlockSpec((3, 2 * BLK, 2 * BLK), lambda hp: (0, hp, 0))],
        out_specs=[col, col],
        out_shape=[jax.ShapeDtypeStruct((S, D), BF), jax.ShapeDtypeStruct((S, D), F32)],
        scratch_shapes=[pltpu.VMEM((S, W), F32),
                        pltpu.VMEM((2, S, W), BF), pltpu.VMEM((2, S, W), BF), pltpu.VMEM((2, S, W), BF),
                        pltpu.VMEM((S, W), F32), pltpu.VMEM((S, W), F32),
                        pltpu.VMEM((3, S, W), F32), pltpu.VMEM((3, S, W), F32)],
        compiler_params=_cparams(("parallel",)),
    )(z3, z3, z3, bias3)


def attention_bwd(z3, dob, ob, lse_b, bias3, name, deps=()):
    W = 2 * DH
    CH = 256

    def body(q_ref, k_ref, v_ref, do_ref, o_ref, lse_ref, b_ref, dz_ref, db_ref,
             stage, delta, qd, kd, vd, dod, lsd, dld, res, acc):
        masks = _head_masks()

        def rowsum(c, carry):
            rows = pl.ds(pl.multiple_of(c * CH, CH), CH)
            prod = do_ref[rows, :].astype(F32) * o_ref[rows, :].astype(F32)
            sa = jnp.sum(jnp.where(masks[0], prod, 0.0), axis=1, keepdims=True)
            sb = jnp.sum(jnp.where(masks[1], prod, 0.0), axis=1, keepdims=True)
            delta[rows, :] = jnp.where(masks[0], sa, sb)
            return carry
        lax.fori_loop(0, S // CH, rowsum, 0)

        for src, dst in ((q_ref, qd), (k_ref, kd), (v_ref, vd), (do_ref, dod)):
            stage[...] = src[...].astype(F32)
            for gi, d in enumerate(BRANCH_DILATIONS[1:]):
                _deinterleave(stage, dst.at[gi], d, BF)
        for gi, d in enumerate(BRANCH_DILATIONS[1:]):
            _deinterleave(lse_ref, lsd.at[gi], d, F32)
            _deinterleave(delta, dld.at[gi], d, F32)

        db_ref[...] = jnp.zeros_like(db_ref)
        for g, d in enumerate(BRANCH_DILATIONS):
            if d == 1:
                qs, ks, vs, dos, ls, dl = q_ref, k_ref, v_ref, do_ref, lse_ref, delta
            else:
                qs, ks, vs, dos = qd.at[g - 1], kd.at[g - 1], vd.at[g - 1], dod.at[g - 1]
                ls, dl = lsd.at[g - 1], dld.at[g - 1]
            res[1] = jnp.zeros((S, W), F32)
            res[2] = jnp.zeros((S, W), F32)

            def block(q0, k0, nk, first, g=g, qs=qs, ks=ks, vs=vs, dos=dos, ls=ls, dl=dl):
                kk = ks[pl.ds(k0, nk), :]
                vv = vs[pl.ds(k0, nk), :]
                q2 = _stack_heads(qs[pl.ds(q0, BLK), :], masks)
                do2 = _stack_heads(dos[pl.ds(q0, BLK), :], masks)
                lse_blk = ls[pl.ds(q0, BLK), :]
                del_blk = dl[pl.ds(q0, BLK), :]
                lse2 = jnp.concatenate([lse_blk[:, 0:1], lse_blk[:, DH:DH + 1]], axis=0)
                del2 = jnp.concatenate([del_blk[:, 0:1], del_blk[:, DH:DH + 1]], axis=0)
                bias = b_ref[g][:, BLK:] if first else b_ref[g]
                s = lax.dot_general(q2, kk, _NT, preferred_element_type=F32) * SCALE + bias
                p = jnp.exp(s - lse2)
                dp = lax.dot_general(do2, vv, _NT, preferred_element_type=F32)
                ds = p * (dp - del2)
                if first:
                    db_ref[g, :, BLK:] += ds
                else:
                    db_ref[g] += ds
                dsb = ds.astype(BF)
                dq2 = jnp.dot(dsb, kk, preferred_element_type=F32) * SCALE
                res[0, pl.ds(q0, BLK), :] = jnp.where(masks[0], dq2[:BLK], dq2[BLK:])
                res[1, pl.ds(k0, nk), :] += lax.dot_general(dsb, q2, _TN,
                                                            preferred_element_type=F32) * SCALE
                res[2, pl.ds(k0, nk), :] += lax.dot_general(p.astype(BF), do2, _TN,
                                                            preferred_element_type=F32)

            _branch_loops(d, block)
            L = S // d
            for t in range(3):
                if d == 1:
                    acc[t] = res[t]
                else:
                    for r in range(d):
                        acc[t, pl.ds(r, L, stride=d), :] = (acc[t, pl.ds(r, L, stride=d), :]
                                                            + res[t, r * L:(r + 1) * L, :])
        for t in range(3):
            dz_ref[t] = acc[t].astype(BF)

    col = pl.BlockSpec((S, W), lambda hp: (0, hp))
    bspec = pl.BlockSpec((3, 2 * BLK, 2 * BLK), lambda hp: (0, hp, 0))
    return pl.pallas_call(
        _after(body, 7, deps), name=name,
        grid=(D // W,),
        in_specs=[pl.BlockSpec((None, S, W), lambda hp: (0, 0, hp)),
                  pl.BlockSpec((None, S, W), lambda hp: (1, 0, hp)),
                  pl.BlockSpec((None, S, W), lambda hp: (2, 0, hp)),
                  col, col, col, bspec] + [ANY] * len(deps),
        out_specs=[pl.BlockSpec((3, S, W), lambda hp: (0, 0, hp)), bspec],
        out_shape=[jax.ShapeDtypeStruct((3, S, D), BF),
                   jax.ShapeDtypeStruct((3, H * BLK, 2 * BLK), F32)],
        scratch_shapes=[pltpu.VMEM((S, W), F32), pltpu.VMEM((S, W), F32),
                        pltpu.VMEM((2, S, W), BF), pltpu.VMEM((2, S, W), BF),
                        pltpu.VMEM((2, S, W), BF), pltpu.VMEM((2, S, W), BF),
                        pltpu.VMEM((2, S, W), F32), pltpu.VMEM((2, S, W), F32),
                        pltpu.VMEM((3, S, W), F32), pltpu.VMEM((3, S, W), F32)],
        compiler_params=_cparams(("parallel",)),
    )(z3, z3, z3, dob, ob, lse_b, bias3, *deps)


def _me():
    return lax.axis_index("x"), lax.axis_index("y"), lax.axis_index("c")


def _other_chips(x, y):
    return [(1 - x, y), (x, 1 - y), (1 - x, 1 - y)]


def _shard_window(ref, axis, t, shape):
    R, C = shape
    if axis == 0:
        return ref.at[pl.ds(pl.multiple_of(t * R, 128), R), :]
    return ref.at[:, pl.ds(pl.multiple_of(t * C, 128), C)]


def all_gather_weights(shards, axes, name):
    n = len(shards)
    shapes = [s.shape for s in shards]
    outs_shape = [jax.ShapeDtypeStruct((8 * R, C) if ax == 0 else (R, 8 * C), BF)
                  for (R, C), ax in zip(shapes, axes)]

    def body(*refs):
        ins, outs = refs[:n], refs[n:2 * n]
        send_sems, recv_sems, local_sems = refs[2 * n:]
        x, y, c = _me()
        me, sibling = (x, y, c), (x, y, 1 - c)
        xnb, ynb, diag = (1 - x, y), (x, 1 - y), (1 - x, 1 - y)
        south = c == 0
        relay_from = (jnp.where(south, x, 1 - x), jnp.where(south, 1 - y, y))
        relay_to = (jnp.where(south, 1 - x, x), jnp.where(south, y, 1 - y))
        barrier = pltpu.get_barrier_semaphore()
        for peer in [sibling, (*xnb, c), (*ynb, c)]:
            pl.semaphore_signal(barrier, inc=1, device_id=peer, device_id_type=MESH)
        pl.semaphore_wait(barrier, 3)

        def win(i, px, py, pc):
            return _shard_window(outs[i], axes[i], 4 * px + 2 * py + pc, shapes[i])

        def copy(i, k, block, to, src=None):
            return pltpu.make_async_remote_copy(
                src_ref=win(i, *block) if src is None else src, dst_ref=win(i, *block),
                send_sem=send_sems.at[i * 7 + k], recv_sem=recv_sems.at[i * 7 + k],
                device_id=to, device_id_type=MESH)

        mine = [pltpu.make_async_copy(ins[i], win(i, *me), local_sems.at[i]) for i in range(n)]
        for cp in mine:
            cp.start()
        sent = []
        for i in range(n):
            sent += [copy(i, 0, me, sibling, src=ins[i]), copy(i, 1, me, (*xnb, c), src=ins[i]),
                     copy(i, 2, me, (*ynb, c), src=ins[i])]
        for cp in sent:
            cp.start()
        for i in range(n):
            for k, chip in ((1, xnb), (2, ynb)):
                copy(i, k, (*chip, c), me).wait_recv()
                sent.append(copy(i, 3 + k, (*chip, c), sibling))
                sent[-1].start()
            sent.append(copy(i, 3, (*relay_from, c), (*relay_to, c)))
            sent[-1].start()
        for i in range(n):
            copy(i, 3, (*diag, c), me).wait_recv()
            sent.append(copy(i, 6, (*diag, c), sibling))
            sent[-1].start()
        for i in range(n):
            copy(i, 0, sibling, me).wait_recv()
            for k, chip in ((4, xnb), (5, ynb), (6, diag)):
                copy(i, k, (*chip, 1 - c), me).wait_recv()
        for cp in sent:
            cp.wait_send()
        for cp in mine:
            cp.wait()

    return pl.kernel(
        body, out_type=outs_shape, name=name,
        mesh=plsc.ScalarSubcoreMesh(axis_name="sequencer", num_cores=1),
        scratch_types=[pltpu.SemaphoreType.DMA((7 * n,)), pltpu.SemaphoreType.DMA((7 * n,)),
                       pltpu.SemaphoreType.DMA((n,))],
        compiler_params=pltpu.CompilerParams(collective_id=1),
    )(*shards)


def pair_exchange_grads(grads, axes, shapes, name):
    n = len(grads)

    def body(*refs):
        ins, outs = refs[:n], refs[n:2 * n]
        send_sems, recv_sems = refs[2 * n:]
        x, y, c = _me()
        sibling = (x, y, 1 - c)
        barrier = pltpu.get_barrier_semaphore()
        pl.semaphore_signal(barrier, inc=1, device_id=sibling, device_id_type=MESH)
        pl.semaphore_wait(barrier, 1)
        copies = []
        for i in range(n):
            for q in range(4):
                t = 2 * q + (1 - c)
                copies.append(pltpu.make_async_remote_copy(
                    src_ref=_shard_window(ins[i], axes[i], t, shapes[i]), dst_ref=outs[i].at[q],
                    send_sem=send_sems.at[i * 4 + q], recv_sem=recv_sems.at[i * 4 + q],
                    device_id=sibling, device_id_type=MESH))
        for cp in copies:
            cp.start()
        for cp in copies:
            cp.wait_recv()
        for cp in copies:
            cp.wait_send()

    return pl.kernel(
        body, out_type=[jax.ShapeDtypeStruct((4,) + tuple(sh), BF) for sh in shapes], name=name,
        mesh=plsc.ScalarSubcoreMesh(axis_name="sequencer", num_cores=1),
        scratch_types=[pltpu.SemaphoreType.DMA((4 * n,)), pltpu.SemaphoreType.DMA((4 * n,))],
        compiler_params=pltpu.CompilerParams(collective_id=2),
    )(*grads)


def pair_add(grad, landed, axis, shape, c_idx, name, deps=()):
    R, C = shape

    def body(c_ref, g_ref, l_ref, o_ref):
        o_ref[...] = (g_ref[...].astype(F32) + l_ref[...].astype(F32)).astype(BF)

    if axis == 0:
        g_spec = pl.BlockSpec((R, C), lambda q, c_ref: (2 * q + c_ref[0], 0))
    else:
        g_spec = pl.BlockSpec((R, C), lambda q, c_ref: (0, 2 * q + c_ref[0]))
    blk = pl.BlockSpec((None, R, C), lambda q, c_ref: (q, 0, 0))
    return pl.pallas_call(
        _after(body, 3, deps), name=name,
        grid_spec=pltpu.PrefetchScalarGridSpec(
            num_scalar_prefetch=1, grid=(4,), in_specs=[g_spec, blk] + [ANY] * len(deps),
            out_specs=blk),
        out_shape=jax.ShapeDtypeStruct((4, R, C), BF),
        compiler_params=_cparams(("parallel",)),
    )(c_idx, grad, landed, *deps)


def chip_exchange_grads(parts, name):
    n = len(parts)

    def body(*refs):
        ins, outs = refs[:n], refs[n:2 * n]
        send_sems, recv_sems = refs[2 * n:]
        x, y, c = _me()
        barrier = pltpu.get_barrier_semaphore()
        for px, py in _other_chips(x, y):
            pl.semaphore_signal(barrier, inc=1, device_id=(px, py, c), device_id_type=MESH)
        pl.semaphore_wait(barrier, 3)
        copies = []
        for i in range(n):
            for k, (px, py) in enumerate(_other_chips(x, y)):
                copies.append(pltpu.make_async_remote_copy(
                    src_ref=ins[i].at[2 * px + py], dst_ref=outs[i].at[k],
                    send_sem=send_sems.at[i * 3 + k], recv_sem=recv_sems.at[i * 3 + k],
                    device_id=(px, py, c), device_id_type=MESH))
        for cp in copies:
            cp.start()
        for cp in copies:
            cp.wait_recv()
        for cp in copies:
            cp.wait_send()

    return pl.kernel(
        body, out_type=[jax.ShapeDtypeStruct((3,) + tuple(p.shape[1:]), BF) for p in parts], name=name,
        mesh=plsc.ScalarSubcoreMesh(axis_name="sequencer", num_cores=1),
        scratch_types=[pltpu.SemaphoreType.DMA((3 * n,)), pltpu.SemaphoreType.DMA((3 * n,))],
        compiler_params=pltpu.CompilerParams(collective_id=3),
    )(*parts)


def all_gather_small(v, name):
    R, C = v.shape

    def body(v_ref, out_ref, send_sems, recv_sems, local_sem):
        x, y, c = _me()
        me, sibling = (x, y, c), (x, y, 1 - c)
        chips = _other_chips(x, y)

        def slot(px, py, pc):
            return out_ref.at[4 * px + 2 * py + pc]

        def copy(k, block, to, src=None):
            return pltpu.make_async_remote_copy(
                src_ref=slot(*block) if src is None else src, dst_ref=slot(*block),
                send_sem=send_sems.at[k], recv_sem=recv_sems.at[k],
                device_id=to, device_id_type=MESH)

        mine = pltpu.make_async_copy(v_ref, slot(*me), local_sem)
        mine.start()
        first = [copy(0, me, sibling, src=v_ref)]
        first += [copy(1 + j, me, (*chip, c), src=v_ref) for j, chip in enumerate(chips)]
        for cp in first:
            cp.start()
        passed = [copy(4 + j, (*chip, c), sibling) for j, chip in enumerate(chips)]
        for j, chip in enumerate(chips):
            copy(1 + j, (*chip, c), me).wait_recv()
            passed[j].start()
        copy(0, sibling, me).wait_recv()
        for j, chip in enumerate(chips):
            copy(4 + j, (*chip, 1 - c), me).wait_recv()
        for cp in first + passed:
            cp.wait_send()
        mine.wait()

    return pl.pallas_call(
        body, name=name,
        in_specs=[pl.BlockSpec(memory_space=pltpu.VMEM)],
        out_specs=pl.BlockSpec(memory_space=pltpu.VMEM),
        out_shape=jax.ShapeDtypeStruct((NDEV, R, C), F32),
        scratch_shapes=[pltpu.SemaphoreType.DMA((7,)), pltpu.SemaphoreType.DMA((7,)),
                        pltpu.SemaphoreType.DMA],
    )(v)


def _adamw(w, g, m, v):
    m = ADAM_B1 * m + (1.0 - ADAM_B1) * g
    v = ADAM_B2 * v + (1.0 - ADAM_B2) * (g * g)
    m_hat = m / (1.0 - ADAM_B1 ** ADAM_STEP)
    v_hat = v / (1.0 - ADAM_B2 ** ADAM_STEP)
    delta = -ADAM_LR * (m_hat / (jnp.sqrt(v_hat) + ADAM_EPS) + ADAM_WD * w)
    return delta, m, v


def reduce_adamw(part, landed, w, m, v, layer, q_idx, name, prev=(), deps=()):
    R, C = part.shape[1:]
    r, c = w.shape[1:]
    tr = r // 2 if r % 16 == 0 and r >= 256 else r
    tR = tr if tr != r else R
    extra = tuple(prev) + tuple(deps)

    def body(q_ref, p_ref, l_ref, w_ref, m_ref, v_ref, g_out, d_out, m_out, v_out):
        g = p_ref[...].astype(F32)
        for k in range(3):
            g = g + l_ref[k].astype(F32)
        g = g[:tr, :c]
        d, mm, vv = _adamw(w_ref[...], g, m_ref[...], v_ref[...])
        g_out[...] = g
        d_out[...] = d
        m_out[...] = mm
        v_out[...] = vv

    wspec = pl.BlockSpec((None, tr, c), lambda i, q_ref: (layer, i, 0))
    out = jax.ShapeDtypeStruct(w.shape, F32)
    return pl.pallas_call(
        _after(body, 6, extra), name=name,
        grid_spec=pltpu.PrefetchScalarGridSpec(
            num_scalar_prefetch=1, grid=(r // tr,),
            in_specs=[pl.BlockSpec((None, tR, C), lambda i, q_ref: (q_ref[0], i, 0)),
                      pl.BlockSpec((3, tR, C), lambda i, q_ref: (0, i, 0)),
                      wspec, wspec, wspec] + [ANY] * len(extra),
            out_specs=[wspec] * 4),
        out_shape=[out] * 4,
        input_output_aliases={6 + k: k for k in range(len(prev))},
        compiler_params=_cparams(("parallel",)),
    )(q_idx, part, landed, w, m, v, *extra)


def small_reduce_adamw(gathered, w, m, v, name):
    R, C = w.shape

    def body(a_ref, w_ref, m_ref, v_ref, g_out, d_out, m_out, v_out):
        g = a_ref[0]
        for k in range(1, NDEV):
            g = g + a_ref[k]
        d, mm, vv = _adamw(w_ref[...], g, m_ref[...], v_ref[...])
        g_out[...] = g
        d_out[...] = d
        m_out[...] = mm
        v_out[...] = vv

    out = jax.ShapeDtypeStruct((R, C), F32)
    return pl.pallas_call(body, name=name, out_shape=[out] * 4,
                          compiler_params=_cparams())(gathered, w, m, v)


def _pad_cols(a, n):
    return jnp.pad(a, ((0, 0), (0, n - a.shape[1])))


def _pad_rows(a, n):
    return jnp.pad(a, ((0, n - a.shape[0]), (0, 0)))


SMALL_ROWS = 16


def _pack_small(mix, ffn, fin, taps_full, relb):
    return jnp.concatenate([
        mix, ffn, fin.reshape(1, D), taps_full.reshape(6, D),
        jnp.pad(relb.reshape(1, NUM_BUCKETS * H), ((0, 0), (0, D - NUM_BUCKETS * H)))], axis=0)


def kernel(x, mix_norm, ffn_norm, final_norm, conv_w_in, conv_kernel, conv_w_out, attn_w_qkv, attn_w_out, rel_bias, ffn_w_gate, ffn_w_up, ffn_w_down, loss_target, m_mix_norm, m_ffn_norm, m_final_norm, m_conv_w_in, m_conv_kernel, m_conv_w_out, m_attn_w_qkv, m_attn_w_out, m_rel_bias, m_ffn_w_gate, m_ffn_w_up, m_ffn_w_down, v_mix_norm, v_ffn_norm, v_final_norm, v_conv_w_in, v_conv_kernel, v_conv_w_out, v_attn_w_qkv, v_attn_w_out, v_rel_bias, v_ffn_w_gate, v_ffn_w_up, v_ffn_w_down):
    xi, yi, ci = _me()
    me = 4 * xi + 2 * yi + ci
    c_idx = jnp.reshape(ci, (1,)).astype(jnp.int32)
    q_idx = jnp.reshape(2 * xi + yi, (1,)).astype(jnp.int32)
    col0 = me * (D // NDEV)

    taps_local = jnp.zeros((2, 3, D), F32)
    taps_local = lax.dynamic_update_slice(taps_local, conv_kernel, (0, 0, col0))
    taps_pack = jnp.pad(taps_local.reshape(6, D), ((0, 2), (0, 0)))
    taps_all = all_gather_small(taps_pack, "ag_taps")
    taps_sum = jnp.sum(taps_all, axis=0)
    taps = [jnp.pad(taps_sum[3 * j:3 * j + 3], ((0, 5), (0, 0))) for j in range(2)]

    gate_t, up_t = jnp.swapaxes(ffn_w_gate, 1, 2), jnp.swapaxes(ffn_w_up, 1, 2)
    m_gate_t, m_up_t = jnp.swapaxes(m_ffn_w_gate, 1, 2), jnp.swapaxes(m_ffn_w_up, 1, 2)
    v_gate_t, v_up_t = jnp.swapaxes(v_ffn_w_gate, 1, 2), jnp.swapaxes(v_ffn_w_up, 1, 2)

    mixer_in = (conv_w_in, attn_w_qkv)
    mixer_out = (conv_w_out, attn_w_out)
    wts = []
    for i in range(DEPTH):
        j = i // 2
        w_in = all_gather_weights([mixer_in[i % 2][j].astype(BF)], (1,), f"ag_in_l{i}")
        w_out = all_gather_weights([mixer_out[i % 2][j].astype(BF)], (0,), f"ag_out_l{i}")
        w_gu = all_gather_weights([_pad_rows(gate_t[i].astype(BF), FF_SHARD_PAD),
                                   _pad_rows(up_t[i].astype(BF), FF_SHARD_PAD)], (0, 0), f"ag_up_l{i}")
        w_d = all_gather_weights([_pad_rows(ffn_w_down[i].astype(BF), FF_SHARD_PAD)], (0,),
                                 f"ag_down_l{i}")
        wts.append(list(w_in) + list(w_out) + list(w_gu) + list(w_d))

    onehot_t, band = _bucket_onehot_t()
    bias3 = bias_tables(rel_bias.T, onehot_t, band, "bias_tables").reshape(3, H * BLK, 2 * BLK)

    saved = []
    xc = x[0]
    for i in range(DEPTH):
        w_in, w_out, w_g, w_u, w_d = wts[i]
        j = i // 2
        x_mix = xc
        z3, h_mix = norm_matmul3(xc, mix_norm[i:i + 1], w_in, f"mix_in_l{i}")
        if i % 2 == 0:
            act = conv_fwd(z3, taps[j], f"conv_fwd_l{i}")
            lse_b = None
        else:
            act, lse_b = attention_fwd(z3, bias3, f"attn_fwd_l{i}")
        xc = matmul_residual(act, w_out, xc, f"mix_out_l{i}")
        x_ffn = xc
        g, u, a, h_ffn = norm_swiglu_up(xc, ffn_norm[i:i + 1], w_g, w_u, f"ffn_up_l{i}")
        xc = matmul_residual(a, w_d, xc, f"ffn_down_l{i}")
        saved.append((x_mix, h_mix, z3, act, lse_b, x_ffn, h_ffn, g, u, a))

    dx, dxb, dg_final, sq = loss_head(xc, final_norm.reshape(1, D), loss_target[0], "loss_head")
    loss = lax.psum(0.5 * jnp.sum(sq[0]) / D, ("x", "y", "c"))

    dg_mix = [None] * DEPTH
    dg_ffn = [None] * DEPTH
    dtaps = [None, None]
    dbias_all = []
    shape_in, shape_out = (D, 3 * D // NDEV), (D // NDEV, D)
    ffn_axes, ffn_shapes = (0, 0, 0), ((FF_SHARD_PAD, D),) * 3
    stacked = {}

    def pair_stage(grads, landed1, axes, shapes, tag, tok):
        parts = []
        for t in range(len(grads)):
            parts.append(pair_add(grads[t], landed1[t], axes[t], shapes[t], c_idx,
                                  f"rs_add_{tag}_{t}", deps=[tok]))
            tok = parts[-1]
        return parts, chip_exchange_grads(parts, f"rs_chip_{tag}"), tok

    def adamw_stage(parts, landed2, params, tag, tok):
        for t, (pname, w_, m_, v_, layer) in enumerate(params):
            res = reduce_adamw(parts[t], landed2[t], w_, m_, v_, layer, q_idx, f"adamw_{tag}_{t}",
                               prev=stacked.get(pname, ()), deps=[tok])
            stacked[pname] = res
            tok = res[0]
        return tok

    tok = dxb
    mix_wait = None
    mix_chip = None
    ffn_chip = None
    for i in reversed(range(DEPTH)):
        w_in, w_out, w_g, w_u, w_d = wts[i]
        j = i // 2
        x_mix, h_mix, z3, act, lse_b, x_ffn, h_ffn, g, u, a = saved[i]
        ffn_params = [("ffn_w_gate", gate_t, m_gate_t, v_gate_t, i),
                      ("ffn_w_up", up_t, m_up_t, v_up_t, i),
                      ("ffn_w_down", ffn_w_down, m_ffn_w_down, v_ffn_w_down, i)]
        if i % 2 == 0:
            mix_params = [("conv_w_in", conv_w_in, m_conv_w_in, v_conv_w_in, j),
                          ("conv_w_out", conv_w_out, m_conv_w_out, v_conv_w_out, j)]
        else:
            mix_params = [("attn_w_qkv", attn_w_qkv, m_attn_w_qkv, v_attn_w_qkv, j),
                          ("attn_w_out", attn_w_out, m_attn_w_out, v_attn_w_out, j)]
        dgate, dup = swiglu_bwd_da(dxb, w_d, g, u, f"ffn_da_l{i}", deps=[tok])
        gw_d = matmul_tn(a, dxb, f"ffn_dwd_l{i}", deps=[dgate])
        tok = gw_d
        if mix_wait is not None:
            grads_m, landed1_m, params_m, tag_m = mix_wait
            parts_m, landed2_m, tok = pair_stage(grads_m, landed1_m, (1, 0), (shape_in, shape_out),
                                                 tag_m, tok)
            mix_chip = (parts_m, landed2_m, params_m, tag_m)
            mix_wait = None
        gw_g = matmul_tn(dgate, h_ffn, f"ffn_dwg_l{i}", deps=[tok])
        gw_u = matmul_tn(dup, h_ffn, f"ffn_dwu_l{i}", deps=[gw_g])
        grads_f = [gw_g, gw_u, gw_d]
        landed1_f = pair_exchange_grads(grads_f, ffn_axes, ffn_shapes, f"rs_pair_f{i}")
        tok = gw_u
        if ffn_chip is not None:
            tok = adamw_stage(*ffn_chip, tok)
            ffn_chip = None
        dx, dxb, dg_ffn[i] = matmul_normbwd(
            [(dgate, w_g, False), (dup, w_u, False)], x_ffn, ffn_norm[i:i + 1], dx, f"ffn_dh_l{i}",
            deps=[tok])
        gw_out = matmul_tn(act, dxb, f"mix_dwout_l{i}")
        if i % 2 == 0:
            dact = matmul_nt(dxb, w_out, f"mix_dact_l{i}", out_dtype=F32, deps=[gw_out])
        else:
            dact = matmul_nt(dxb, w_out, f"mix_dact_l{i}", out_dtype=BF, deps=[gw_out])
        parts_f, landed2_f, tok = pair_stage(grads_f, landed1_f, ffn_axes, ffn_shapes, f"f{i}", dact)
        ffn_chip = (parts_f, landed2_f, ffn_params, f"f{i}")
        if i % 2 == 0:
            dz3, dtaps[j] = conv_bwd(dact, z3, taps[j], f"conv_bwd_l{i}", deps=[tok])
        else:
            dz3, dbias3 = attention_bwd(z3, dact, act, lse_b, bias3, f"attn_bwd_l{i}", deps=[tok])
            dbias_all.append(dbias3.reshape(3, H, BLK * 2 * BLK))
        gw_in = matmul_tn(h_mix, dz3, f"mix_dwin_l{i}")
        grads_m = [gw_in, gw_out]
        landed1_m = pair_exchange_grads(grads_m, (1, 0), (shape_in, shape_out), f"rs_pair_m{i}")
        mix_wait = (grads_m, landed1_m, mix_params, f"m{i}")
        tok = gw_in
        if mix_chip is not None:
            tok = adamw_stage(*mix_chip, tok)
            mix_chip = None
        dx, dxb, dg_mix[i] = matmul_normbwd(
            [(dz3, w_in, True)], x_mix, mix_norm[i:i + 1], dx, f"mix_dh_l{i}", deps=[tok])
        tok = dxb
    grads_m, landed1_m, params_m, tag_m = mix_wait
    parts_m, landed2_m, tok = pair_stage(grads_m, landed1_m, (1, 0), (shape_in, shape_out), tag_m, tok)
    tok = adamw_stage(*ffn_chip, tok)

    grad_relb_t = bias_grad(jnp.concatenate(dbias_all), onehot_t, "bias_grad")
    dtaps_full = jnp.stack([dtaps[0][:3], dtaps[1][:3]])
    g_small = _pack_small(jnp.concatenate([d[0:1] for d in dg_mix], axis=0),
                          jnp.concatenate([d[0:1] for d in dg_ffn], axis=0),
                          dg_final[0], dtaps_full, grad_relb_t.T)
    gathered = all_gather_small(g_small, "ag_small_grads")

    def taps_at_cols(k):
        return lax.dynamic_update_slice(jnp.zeros((2, 3, D), F32), k, (0, 0, col0))

    w_small = _pack_small(mix_norm, ffn_norm, final_norm, taps_at_cols(conv_kernel), rel_bias)
    m_small = _pack_small(m_mix_norm, m_ffn_norm, m_final_norm, taps_at_cols(m_conv_kernel), m_rel_bias)
    v_small = _pack_small(v_mix_norm, v_ffn_norm, v_final_norm, taps_at_cols(v_conv_kernel), v_rel_bias)
    small = small_reduce_adamw(gathered, w_small, m_small, v_small, "adamw_small")

    def unpack_small(p):
        taps_p = lax.dynamic_slice(p[9:15].reshape(2, 3, D), (0, 0, col0), (2, 3, D // NDEV))
        return {"mix_norm": p[0:4], "ffn_norm": p[4:8], "final_norm": p[8],
                "conv_kernel": taps_p, "rel_bias": p[15, :NUM_BUCKETS * H].reshape(NUM_BUCKETS, H)}

    small_out = [unpack_small(p) for p in small]
    adamw_stage(parts_m, landed2_m, params_m, tag_m, small[0])

    names = ["mix_norm", "ffn_norm", "final_norm", "conv_w_in", "conv_kernel", "conv_w_out",
             "attn_w_qkv", "attn_w_out", "rel_bias", "ffn_w_gate", "ffn_w_up", "ffn_w_down"]
    outs = [loss, dx.reshape(1, S, D)]
    for o in range(4):
        for nme in names:
            if nme in ("ffn_w_gate", "ffn_w_up"):
                outs.append(jnp.swapaxes(stacked[nme][o], 1, 2))
            else:
                outs.append(stacked[nme][o] if nme in stacked else small_out[o][nme])
    return tuple(outs)
```

```python
import math

import numpy as np
import jax
import jax.numpy as jnp
from jax import lax
from jax.experimental import pallas as pl
from jax.experimental.pallas import tpu as pltpu
from jax.experimental.pallas import tpu_sc as plsc

S = 2048
D = 1024
H = 16
DH = 64
DFF = 2816
NDEV = 8
DEPTH = 4
FF_SHARD = DFF // NDEV
FF_SHARD_PAD = 384
DFF_PAD = FF_SHARD_PAD * NDEV
BLK = 128
BRANCH_DILATIONS = (1, 4, 16)
NUM_BUCKETS = 32
MAX_DISTANCE = 2048
EPS = 1e-6
NEG_INF = -1e30
SCALE = DH ** -0.5

ADAM_LR = 0.001
ADAM_B1 = 0.9
ADAM_B2 = 0.999
ADAM_EPS = 1e-08
ADAM_WD = 0.01
ADAM_STEP = 10

BF = jnp.bfloat16
F32 = jnp.float32
VMEM_LIMIT_BYTES = 56 * 1024 * 1024
KSPLIT = 512
MESH = pl.DeviceIdType.MESH
ANY = pl.BlockSpec(memory_space=pl.ANY)

_NT = (((1,), (1,)), ((), ()))
_TN = (((0,), (0,)), ((), ()))


def _cparams(sem=None):
    return pltpu.CompilerParams(dimension_semantics=sem, vmem_limit_bytes=VMEM_LIMIT_BYTES)


def _after(body, n, deps):
    nd = len(deps)
    if nd == 0:
        return body

    def ordered(*refs):
        body(*refs[:n], *refs[n + nd:])
    return ordered


def _rms(x):
    return lax.rsqrt(jnp.mean(x * x, axis=-1, keepdims=True) + EPS)


def norm_matmul3(x, gain, w, name, tm=1024, tn=512):
    per = D // tn

    def body(x_ref, g_ref, w_ref, z_ref, h_ref, hs_ref):
        @pl.when(pl.program_id(1) == 0)
        def _():
            xv = x_ref[...]
            hv = (xv * _rms(xv) * g_ref[...]).astype(BF)
            hs_ref[...] = hv
            h_ref[...] = hv
        z_ref[...] = jnp.dot(hs_ref[...], w_ref[...], preferred_element_type=F32).astype(BF)

    return pl.pallas_call(
        body, name=name,
        grid=(S // tm, 3 * D // tn),
        in_specs=[pl.BlockSpec((tm, D), lambda i, j: (i, 0)),
                  pl.BlockSpec((1, D), lambda i, j: (0, 0)),
                  pl.BlockSpec((D, tn), lambda i, j: (0, j))],
        out_specs=[pl.BlockSpec((None, tm, tn), lambda i, j: (j // per, i, j % per)),
                   pl.BlockSpec((tm, D), lambda i, j: (i, 0))],
        out_shape=[jax.ShapeDtypeStruct((3, S, D), BF), jax.ShapeDtypeStruct((S, D), BF)],
        scratch_shapes=[pltpu.VMEM((tm, D), BF)],
        compiler_params=_cparams(("parallel", "arbitrary")),
    )(x, gain, w)


def norm_swiglu_up(x, gain, wg_t, wu_t, name, tm=1024, tn=512):
    def body(x_ref, g_ref, wg_ref, wu_ref, go_ref, uo_ref, ao_ref, h_ref, hs_ref):
        @pl.when(pl.program_id(1) == 0)
        def _():
            xv = x_ref[...]
            hv = (xv * _rms(xv) * g_ref[...]).astype(BF)
            hs_ref[...] = hv
            h_ref[...] = hv
        hv = hs_ref[...]
        g = lax.dot_general(hv, wg_ref[...], _NT, preferred_element_type=F32)
        u = lax.dot_general(hv, wu_ref[...], _NT, preferred_element_type=F32)
        go_ref[...] = g.astype(BF)
        uo_ref[...] = u.astype(BF)
        ao_ref[...] = (g * jax.nn.sigmoid(g) * u).astype(BF)

    act = jax.ShapeDtypeStruct((S, DFF_PAD), BF)
    blk = pl.BlockSpec((tm, tn), lambda i, j: (i, j))
    return pl.pallas_call(
        body, name=name,
        grid=(S // tm, DFF_PAD // tn),
        in_specs=[pl.BlockSpec((tm, D), lambda i, j: (i, 0)),
                  pl.BlockSpec((1, D), lambda i, j: (0, 0)),
                  pl.BlockSpec((tn, D), lambda i, j: (j, 0)),
                  pl.BlockSpec((tn, D), lambda i, j: (j, 0))],
        out_specs=[blk, blk, blk, pl.BlockSpec((tm, D), lambda i, j: (i, 0))],
        out_shape=[act, act, act, jax.ShapeDtypeStruct((S, D), BF)],
        scratch_shapes=[pltpu.VMEM((tm, D), BF)],
        compiler_params=_cparams(("parallel", "arbitrary")),
    )(x, gain, wg_t, wu_t)


def matmul_residual(a, w, x, name, tm=1024, tn=512):
    K = a.shape[1]
    ns = K // KSPLIT
    kc = K // ns

    def body(*refs):
        x_ref, o_ref = refs[2 * ns:]
        acc = x_ref[...]
        for s in range(ns):
            acc = acc + jnp.dot(refs[s][...], refs[ns + s][...], preferred_element_type=F32)
        o_ref[...] = acc

    return pl.pallas_call(
        body, name=name,
        grid=(S // tm, D // tn),
        in_specs=[pl.BlockSpec((tm, kc), lambda i, j, s=s: (i, s)) for s in range(ns)]
        + [pl.BlockSpec((kc, tn), lambda i, j, s=s: (s, j)) for s in range(ns)]
        + [pl.BlockSpec((tm, tn), lambda i, j: (i, j))],
        out_specs=pl.BlockSpec((tm, tn), lambda i, j: (i, j)),
        out_shape=jax.ShapeDtypeStruct((S, D), F32),
        compiler_params=_cparams(("parallel", "parallel")),
    )(*([a] * ns), *([w] * ns), x)


def matmul_nt(a, w, name, out_dtype=BF, tm=1024, tn=512, deps=()):
    K = a.shape[1]
    N = w.shape[0]

    def body(a_ref, w_ref, o_ref):
        o_ref[...] = lax.dot_general(a_ref[...], w_ref[...], _NT,
                                     preferred_element_type=F32).astype(o_ref.dtype)

    return pl.pallas_call(
        _after(body, 2, deps), name=name,
        grid=(S // tm, N // tn),
        in_specs=[pl.BlockSpec((tm, K), lambda i, j: (i, 0)),
                  pl.BlockSpec((tn, K), lambda i, j: (j, 0))] + [ANY] * len(deps),
        out_specs=pl.BlockSpec((tm, tn), lambda i, j: (i, j)),
        out_shape=jax.ShapeDtypeStruct((S, N), out_dtype),
        compiler_params=_cparams(("parallel", "parallel")),
    )(a, w, *deps)


def swiglu_bwd_da(dxb, wd, g, u, name, tm=1024, tn=512, deps=()):
    def body(dx_ref, w_ref, g_ref, u_ref, dg_ref, du_ref):
        da = lax.dot_general(dx_ref[...], w_ref[...], _NT, preferred_element_type=F32)
        gv = g_ref[...].astype(F32)
        uv = u_ref[...].astype(F32)
        sig = jax.nn.sigmoid(gv)
        dg_ref[...] = (da * uv * (sig * (1.0 + gv * (1.0 - sig)))).astype(BF)
        du_ref[...] = (da * (gv * sig)).astype(BF)

    act = jax.ShapeDtypeStruct((S, DFF_PAD), BF)
    blk = pl.BlockSpec((tm, tn), lambda i, j: (i, j))
    return pl.pallas_call(
        _after(body, 4, deps), name=name,
        grid=(S // tm, DFF_PAD // tn),
        in_specs=[pl.BlockSpec((tm, D), lambda i, j: (i, 0)),
                  pl.BlockSpec((tn, D), lambda i, j: (j, 0)),
                  blk, blk] + [ANY] * len(deps),
        out_specs=[blk, blk],
        out_shape=[act, act],
        compiler_params=_cparams(("parallel", "parallel")),
    )(dxb, wd, g, u, *deps)


def matmul_tn_group(pairs, name, tm=512, tn=512, deps=()):
    P = len(pairs)
    steps = []
    for p, (a, b) in enumerate(pairs):
        N = 3 * D if b.ndim == 3 else b.shape[1]
        steps += [(p, i, j) for i in range(a.shape[1] // tm) for j in range(N // tn)]
    T = len(steps)
    tab = np.zeros((T, 1 + 2 * P), np.int32)
    for p in range(P):
        cur = (0, 0)
        for s, (ph, i, j) in enumerate(steps):
            if ph == p:
                cur = (i, j)
            tab[s, 1 + 2 * p:3 + 2 * p] = cur
    tab[:, 0] = [ph for ph, _, _ in steps]

    in_specs, out_specs, out_shapes, operands = [], [], [], []
    per = D // tn
    for p, (a, b) in enumerate(pairs):
        ci, cj = 1 + 2 * p, 2 + 2 * p
        in_specs.append(pl.BlockSpec((S, tm), lambda s, t, ci=ci: (0, t[s, ci])))
        if b.ndim == 3:
            in_specs.append(pl.BlockSpec((None, S, tn),
                                         lambda s, t, cj=cj: (t[s, cj] // per, 0, t[s, cj] % per)))
            N = 3 * D
        else:
            in_specs.append(pl.BlockSpec((S, tn), lambda s, t, cj=cj: (0, t[s, cj])))
            N = b.shape[1]
        out_specs.append(pl.BlockSpec((tm, tn), lambda s, t, ci=ci, cj=cj: (t[s, ci], t[s, cj])))
        out_shapes.append(jax.ShapeDtypeStruct((a.shape[1], N), BF))
        operands += [a, b]

    def body(tab_ref, *refs):
        phase = tab_ref[pl.program_id(0), 0]
        for p in range(P):
            @pl.when(phase == p)
            def _(p=p):
                refs[2 * P + p][...] = lax.dot_general(
                    refs[2 * p][...], refs[2 * p + 1][...], _TN,
                    preferred_element_type=F32).astype(BF)

    return pl.pallas_call(
        _after(body, 1 + 2 * P, deps), name=name,
        grid_spec=pltpu.PrefetchScalarGridSpec(
            num_scalar_prefetch=1, grid=(T,), in_specs=in_specs + [ANY] * len(deps),
            out_specs=out_specs),
        out_shape=out_shapes,
        compiler_params=_cparams(("arbitrary",)),
    )(jnp.asarray(tab), *operands, *deps)


def matmul_normbwd(terms, x_in, gain, dx, name, tm=512, ch=256, deps=()):
    specs, operands = [], []
    for (a, w, stacked) in terms:
        if stacked:
            specs.append(pl.BlockSpec((3, tm, D), lambda i: (0, i, 0)))
        else:
            specs.append(pl.BlockSpec((tm, a.shape[1]), lambda i: (i, 0)))
        specs.append(pl.BlockSpec(w.shape, lambda i: (0, 0), pipeline_mode=pl.Buffered(1)))
        operands += [a, w]
    nt = len(terms)

    def body(*refs):
        aw = refs[:2 * nt]
        x_ref, g_ref, dx_ref, dxo_ref, dxb_ref, dg_ref, acc_ref = refs[2 * nt:]

        @pl.when(pl.program_id(0) == 0)
        def _():
            dg_ref[...] = jnp.zeros_like(dg_ref)

        dh = None
        for t, (_, _, stacked) in enumerate(terms):
            a_ref, w_ref = aw[2 * t], aw[2 * t + 1]
            if stacked:
                parts = [lax.dot_general(a_ref[k], w_ref[:, k * D:(k + 1) * D], _NT,
                                         preferred_element_type=F32) for k in range(3)]
            else:
                parts = [jnp.dot(a_ref[...], w_ref[...], preferred_element_type=F32)]
            for p in parts:
                dh = p if dh is None else dh + p
        acc_ref[...] = dh

        def chunk(c, carry):
            rows = pl.ds(pl.multiple_of(c * ch, ch), ch)
            xv = x_ref[rows, :]
            r = _rms(xv)
            xhat = xv * r
            dhc = acc_ref[rows, :]
            dg_ref[0:1, :] += jnp.sum(dhc * xhat, axis=0, keepdims=True)
            dxh = dhc * g_ref[...]
            dxn = r * (dxh - xhat * jnp.mean(dxh * xhat, axis=-1, keepdims=True))
            out = dx_ref[rows, :] + dxn
            dxo_ref[rows, :] = out
            dxb_ref[rows, :] = out.astype(BF)
            return carry
        lax.fori_loop(0, tm // ch, chunk, 0)

    row = pl.BlockSpec((tm, D), lambda i: (i, 0))
    return pl.pallas_call(
        _after(body, 2 * nt + 3, deps), name=name,
        grid=(S // tm,),
        in_specs=specs + [row, pl.BlockSpec((1, D), lambda i: (0, 0)), row] + [ANY] * len(deps),
        out_specs=[row, row, pl.BlockSpec((8, D), lambda i: (0, 0))],
        out_shape=[jax.ShapeDtypeStruct((S, D), F32), jax.ShapeDtypeStruct((S, D), BF),
                   jax.ShapeDtypeStruct((8, D), F32)],
        scratch_shapes=[pltpu.VMEM((tm, D), F32)],
        compiler_params=_cparams(("arbitrary",)),
    )(*operands, x_in, gain, dx, *deps)


def loss_head(x, gain, target, name, tm=512):
    def body(x_ref, g_ref, t_ref, dxo_ref, dxb_ref, dg_ref, sq_ref):
        @pl.when(pl.program_id(0) == 0)
        def _():
            dg_ref[...] = jnp.zeros_like(dg_ref)
            sq_ref[...] = jnp.zeros_like(sq_ref)
        xv = x_ref[...]
        r = _rms(xv)
        xhat = xv * r
        err = xhat * g_ref[...] - t_ref[...]
        sq_ref[0:1, :] += jnp.sum(err * err, axis=0, keepdims=True)
        dy = err * (1.0 / D)
        dg_ref[0:1, :] += jnp.sum(dy * xhat, axis=0, keepdims=True)
        dxh = dy * g_ref[...]
        out = r * (dxh - xhat * jnp.mean(dxh * xhat, axis=-1, keepdims=True))
        dxo_ref[...] = out
        dxb_ref[...] = out.astype(BF)

    row = pl.BlockSpec((tm, D), lambda i: (i, 0))
    acc = pl.BlockSpec((8, D), lambda i: (0, 0))
    return pl.pallas_call(
        body, name=name,
        grid=(S // tm,),
        in_specs=[row, pl.BlockSpec((1, D), lambda i: (0, 0)), row],
        out_specs=[row, row, acc, acc],
        out_shape=[jax.ShapeDtypeStruct((S, D), F32), jax.ShapeDtypeStruct((S, D), BF),
                   jax.ShapeDtypeStruct((8, D), F32), jax.ShapeDtypeStruct((8, D), F32)],
        compiler_params=_cparams(("arbitrary",)),
    )(x, gain, target)


def _shift_down(p, n, row):
    return jnp.where(row >= n, pltpu.roll(p, n, axis=0), 0.0)


def _shift_up(p, n, row):
    return jnp.where(row < S - n, pltpu.roll(p, S - n, axis=0), 0.0)


def conv_fwd(z3, taps, name, tn=128):
    def body(z_ref, k_ref, m_ref):
        b = z_ref[0].astype(F32)
        p = z_ref[1].astype(F32) * z_ref[2].astype(F32)
        row = lax.broadcasted_iota(jnp.int32, p.shape, 0)
        y = (k_ref[2:3, :] * p + k_ref[1:2, :] * _shift_down(p, 1, row)
             + k_ref[0:1, :] * _shift_down(p, 2, row))
        m_ref[...] = (b * y).astype(BF)

    return pl.pallas_call(
        body, name=name,
        grid=(D // tn,),
        in_specs=[pl.BlockSpec((3, S, tn), lambda j: (0, 0, j)),
                  pl.BlockSpec((8, tn), lambda j: (0, j))],
        out_specs=pl.BlockSpec((S, tn), lambda j: (0, j)),
        out_shape=jax.ShapeDtypeStruct((S, D), BF),
        compiler_params=_cparams(("parallel",)),
    )(z3, taps)


def conv_bwd(dm, z3, taps, name, tn=128, deps=()):
    def body(dm_ref, z_ref, k_ref, dz_ref, dk_ref):
        dmv = dm_ref[...]
        b = z_ref[0].astype(F32)
        c = z_ref[1].astype(F32)
        u = z_ref[2].astype(F32)
        p = c * u
        row = lax.broadcasted_iota(jnp.int32, p.shape, 0)
        p1 = _shift_down(p, 1, row)
        p2 = _shift_down(p, 2, row)
        y = k_ref[2:3, :] * p + k_ref[1:2, :] * p1 + k_ref[0:1, :] * p2
        dy = dmv * b
        dz_ref[0] = (dmv * y).astype(BF)
        dp = (k_ref[2:3, :] * dy + k_ref[1:2, :] * _shift_up(dy, 1, row)
              + k_ref[0:1, :] * _shift_up(dy, 2, row))
        dz_ref[1] = (dp * u).astype(BF)
        dz_ref[2] = (dp * c).astype(BF)
        dk_ref[...] = jnp.zeros_like(dk_ref)
        dk_ref[0:1, :] = jnp.sum(dy * p2, axis=0, keepdims=True)
        dk_ref[1:2, :] = jnp.sum(dy * p1, axis=0, keepdims=True)
        dk_ref[2:3, :] = jnp.sum(dy * p, axis=0, keepdims=True)

    return pl.pallas_call(
        _after(body, 3, deps), name=name,
        grid=(D // tn,),
        in_specs=[pl.BlockSpec((S, tn), lambda j: (0, j)),
                  pl.BlockSpec((3, S, tn), lambda j: (0, 0, j)),
                  pl.BlockSpec((8, tn), lambda j: (0, j))] + [ANY] * len(deps),
        out_specs=[pl.BlockSpec((3, S, tn), lambda j: (0, 0, j)),
                   pl.BlockSpec((8, tn), lambda j: (0, j))],
        out_shape=[jax.ShapeDtypeStruct((3, S, D), BF), jax.ShapeDtypeStruct((8, D), F32)],
        compiler_params=_cparams(("parallel",)),
    )(dm, z3, taps, *deps)


def _t5_bucket(dist):
    exact = NUM_BUCKETS // 2
    df = jnp.maximum(dist, 1).astype(jnp.float32)
    large = exact + (jnp.log(df / exact) / math.log(MAX_DISTANCE / exact)
                     * (NUM_BUCKETS - exact)).astype(jnp.int32)
    large = jnp.minimum(large, NUM_BUCKETS - 1)
    return jnp.where(dist < exact, dist, large)


def _bucket_onehot_t():
    qi = jnp.arange(BLK)[:, None]
    ki = jnp.arange(2 * BLK)[None, :]
    rel = qi + BLK - ki
    band = ((rel >= 0) & (rel <= BLK)).reshape(1, -1).astype(F32)
    hots = []
    for d in BRANCH_DILATIONS:
        bucket = _t5_bucket(jnp.clip(rel, 0) * d).reshape(1, -1)
        hots.append((jnp.arange(NUM_BUCKETS)[:, None] == bucket).astype(F32))
    return jnp.stack(hots), band


def bias_tables(rel_bias_t, onehot_t, band, name):
    def body(rb_ref, oh_ref, band_ref, o_ref):
        b = jnp.dot(rb_ref[...], oh_ref[...], preferred_element_type=F32,
                    precision=lax.Precision.HIGHEST)
        o_ref[...] = jnp.where(band_ref[...] > 0.5, b, NEG_INF)

    n = BLK * 2 * BLK
    return pl.pallas_call(
        body, name=name,
        grid=(3,),
        in_specs=[pl.BlockSpec((H, NUM_BUCKETS), lambda g: (0, 0)),
                  pl.BlockSpec((None, NUM_BUCKETS, n), lambda g: (g, 0, 0)),
                  pl.BlockSpec((1, n), lambda g: (0, 0))],
        out_specs=pl.BlockSpec((None, H, n), lambda g: (g, 0, 0)),
        out_shape=jax.ShapeDtypeStruct((3, H, n), F32),
        compiler_params=_cparams(("parallel",)),
    )(rel_bias_t, onehot_t, band)


def bias_grad(dbias, onehot_t, name):
    def body(db_ref, oh_ref, o_ref):
        @pl.when(pl.program_id(0) == 0)
        def _():
            o_ref[...] = jnp.zeros_like(o_ref)
        o_ref[...] += lax.dot_general(db_ref[...], oh_ref[...], _NT, preferred_element_type=F32,
                                      precision=lax.Precision.HIGHEST)

    n = BLK * 2 * BLK
    return pl.pallas_call(
        body, name=name,
        grid=(dbias.shape[0],),
        in_specs=[pl.BlockSpec((None, H, n), lambda g: (g, 0, 0)),
                  pl.BlockSpec((None, NUM_BUCKETS, n), lambda g: (g % 3, 0, 0))],
        out_specs=pl.BlockSpec((H, NUM_BUCKETS), lambda g: (0, 0)),
        out_shape=jax.ShapeDtypeStruct((H, NUM_BUCKETS), F32),
        compiler_params=_cparams(("arbitrary",)),
    )(dbias, onehot_t)


def _head_masks():
    lane = lax.broadcasted_iota(jnp.int32, (1, 2 * DH), 1)
    return (lane < DH, lane >= DH)


def _stack_heads(x, masks):
    zero = jnp.zeros_like(x)
    return jnp.concatenate([jnp.where(masks[0], x, zero), jnp.where(masks[1], x, zero)], axis=0)


def _deinterleave(src_ref, dst_ref, d, dtype):
    L = S // d
    for r in range(d):
        dst_ref[r * L:(r + 1) * L, :] = src_ref[pl.ds(r, L, stride=d), :].astype(dtype)


def _branch_loops(d, block):
    L = S // d
    nb = L // BLK

    def first(base):
        block(base, base, BLK, True)

    def later(base, n):
        q0 = pl.multiple_of(base + n * BLK, BLK)
        block(q0, pl.multiple_of(q0 - BLK, BLK), 2 * BLK, False)

    if d == 1:
        unroll = 3
        assert (nb - 1) % unroll == 0
        first(0)

        def trip(it, c):
            for u in range(unroll):
                later(0, 1 + it * unroll + u)
            return c
        lax.fori_loop(0, (nb - 1) // unroll, trip, 0)
    elif nb > 1:
        def residue(r, c):
            base = pl.multiple_of(r * L, BLK)
            first(base)
            for n in range(1, nb):
                later(base, n)
            return c
        lax.fori_loop(0, d, residue, 0)
    else:
        unroll = 4
        assert d % unroll == 0

        def trip(it, c):
            for u in range(unroll):
                first(pl.multiple_of((it * unroll + u) * L, BLK))
            return c
        lax.fori_loop(0, d // unroll, trip, 0)


def attention_fwd(z3, bias3, name):
    W = 2 * DH
    CH = 256

    def body(q_ref, k_ref, v_ref, b_ref, o_ref, lse_ref, stage, qd, kd, vd, od, ld, on, ln):
        masks = _head_masks()
        for src, dst in ((q_ref, qd), (k_ref, kd), (v_ref, vd)):
            stage[...] = src[...].astype(F32)
            for gi, d in enumerate(BRANCH_DILATIONS[1:]):
                _deinterleave(stage, dst.at[gi], d, BF)

        for g, d in enumerate(BRANCH_DILATIONS):
            qs, ks, vs = (q_ref, k_ref, v_ref) if d == 1 else (qd.at[g - 1], kd.at[g - 1], vd.at[g - 1])
            o_dst, l_dst = (on.at[0], ln.at[0]) if d == 1 else (od, ld)

            def block(q0, k0, nk, first, g=g, qs=qs, ks=ks, vs=vs, o_dst=o_dst, l_dst=l_dst):
                q2 = _stack_heads(qs[pl.ds(q0, BLK), :], masks)
                kk = ks[pl.ds(k0, nk), :]
                vv = vs[pl.ds(k0, nk), :]
                bias = b_ref[g][:, BLK:] if first else b_ref[g]
                s = lax.dot_general(q2, kk, _NT, preferred_element_type=F32) * SCALE + bias
                mx = jnp.max(s, axis=1, keepdims=True)
                p = jnp.exp(s - mx)
                l = jnp.sum(p, axis=1, keepdims=True)
                o2 = jnp.dot(p.astype(BF), vv, preferred_element_type=F32) / l
                lse2 = mx + jnp.log(l)
                o_dst[pl.ds(q0, BLK), :] = jnp.where(masks[0], o2[:BLK], o2[BLK:])
                l_dst[pl.ds(q0, BLK), :] = jnp.where(masks[0], lse2[:BLK], lse2[BLK:])

            _branch_loops(d, block)
            if d > 1:
                L = S // d
                for r in range(d):
                    on[g, pl.ds(r, L, stride=d), :] = od[r * L:(r + 1) * L, :]
                    ln[g, pl.ds(r, L, stride=d), :] = ld[r * L:(r + 1) * L, :]

        def join(c, carry):
            rows = pl.ds(pl.multiple_of(c * CH, CH), CH)
            a, b, cc = ln[0, rows, :], ln[1, rows, :], ln[2, rows, :]
            mx = jnp.maximum(jnp.maximum(a, b), cc)
            ea, eb, ec = jnp.exp(a - mx), jnp.exp(b - mx), jnp.exp(cc - mx)
            tot = ea + eb + ec
            o_ref[rows, :] = ((ea * on[0, rows, :] + eb * on[1, rows, :] + ec * on[2, rows, :])
                              / tot).astype(BF)
            lse_ref[rows, :] = mx + jnp.log(tot)
            return carry
        lax.fori_loop(0, S // CH, join, 0)

    col = pl.BlockSpec((S, W), lambda hp: (0, hp))
    return pl.pallas_call(
        body, name=name,
        grid=(D // W,),
        in_specs=[pl.BlockSpec((None, S, W), lambda hp: (0, 0, hp)),
                  pl.BlockSpec((None, S, W), lambda hp: (1, 0, hp)),
                  pl.BlockSpec((None, S, W), lambda hp: (2, 0, hp)),
                  pl.BlockSpec((3, 2 * BLK, 2 * BLK), lambda hp: (0, hp, 0))],
        out_specs=[col, col],
        out_shape=[jax.ShapeDtypeStruct((S, D), BF), jax.ShapeDtypeStruct((S, D), F32)],
        scratch_shapes=[pltpu.VMEM((S, W), F32),
                        pltpu.VMEM((2, S, W), BF), pltpu.VMEM((2, S, W), BF), pltpu.VMEM((2, S, W), BF),
                        pltpu.VMEM((S, W), F32), pltpu.VMEM((S, W), F32),
                        pltpu.VMEM((3, S, W), F32), pltpu.VMEM((3, S, W), F32)],
        compiler_params=_cparams(("parallel",)),
    )(z3, z3, z3, bias3)


def attention_bwd(z3, dob, ob, lse_b, bias3, name, deps=()):
    W = 2 * DH
    CH = 256

    def body(q_ref, k_ref, v_ref, do_ref, o_ref, lse_ref, b_ref, dz_ref, db_ref,
             stage, delta, qd, kd, vd, dod, lsd, dld, res, acc):
        masks = _head_masks()

        def rowsum(c, carry):
            rows = pl.ds(pl.multiple_of(c * CH, CH), CH)
            prod = do_ref[rows, :].astype(F32) * o_ref[rows, :].astype(F32)
            sa = jnp.sum(jnp.where(masks[0], prod, 0.0), axis=1, keepdims=True)
            sb = jnp.sum(jnp.where(masks[1], prod, 0.0), axis=1, keepdims=True)
            delta[rows, :] = jnp.where(masks[0], sa, sb)
            return carry
        lax.fori_loop(0, S // CH, rowsum, 0)

        for src, dst in ((q_ref, qd), (k_ref, kd), (v_ref, vd), (do_ref, dod)):
            stage[...] = src[...].astype(F32)
            for gi, d in enumerate(BRANCH_DILATIONS[1:]):
                _deinterleave(stage, dst.at[gi], d, BF)
        for gi, d in enumerate(BRANCH_DILATIONS[1:]):
            _deinterleave(lse_ref, lsd.at[gi], d, F32)
            _deinterleave(delta, dld.at[gi], d, F32)

        db_ref[...] = jnp.zeros_like(db_ref)
        for g, d in enumerate(BRANCH_DILATIONS):
            if d == 1:
                qs, ks, vs, dos, ls, dl = q_ref, k_ref, v_ref, do_ref, lse_ref, delta
            else:
                qs, ks, vs, dos = qd.at[g - 1], kd.at[g - 1], vd.at[g - 1], dod.at[g - 1]
                ls, dl = lsd.at[g - 1], dld.at[g - 1]
            res[1] = jnp.zeros((S, W), F32)
            res[2] = jnp.zeros((S, W), F32)

            def block(q0, k0, nk, first, g=g, qs=qs, ks=ks, vs=vs, dos=dos, ls=ls, dl=dl):
                kk = ks[pl.ds(k0, nk), :]
                vv = vs[pl.ds(k0, nk), :]
                q2 = _stack_heads(qs[pl.ds(q0, BLK), :], masks)
                do2 = _stack_heads(dos[pl.ds(q0, BLK), :], masks)
                lse_blk = ls[pl.ds(q0, BLK), :]
                del_blk = dl[pl.ds(q0, BLK), :]
                lse2 = jnp.concatenate([lse_blk[:, 0:1], lse_blk[:, DH:DH + 1]], axis=0)
                del2 = jnp.concatenate([del_blk[:, 0:1], del_blk[:, DH:DH + 1]], axis=0)
                bias = b_ref[g][:, BLK:] if first else b_ref[g]
                s = lax.dot_general(q2, kk, _NT, preferred_element_type=F32) * SCALE + bias
                p = jnp.exp(s - lse2)
                dp = lax.dot_general(do2, vv, _NT, preferred_element_type=F32)
                ds = p * (dp - del2)
                if first:
                    db_ref[g, :, BLK:] += ds
                else:
                    db_ref[g] += ds
                dsb = ds.astype(BF)
                dq2 = jnp.dot(dsb, kk, preferred_element_type=F32) * SCALE
                res[0, pl.ds(q0, BLK), :] = jnp.where(masks[0], dq2[:BLK], dq2[BLK:])
                res[1, pl.ds(k0, nk), :] += lax.dot_general(dsb, q2, _TN,
                                                            preferred_element_type=F32) * SCALE
                res[2, pl.ds(k0, nk), :] += lax.dot_general(p.astype(BF), do2, _TN,
                                                            preferred_element_type=F32)

            _branch_loops(d, block)
            L = S // d
            for t in range(3):
                if d == 1:
                    acc[t] = res[t]
                else:
                    for r in range(d):
                        acc[t, pl.ds(r, L, stride=d), :] = (acc[t, pl.ds(r, L, stride=d), :]
                                                            + res[t, r * L:(r + 1) * L, :])
        for t in range(3):
            dz_ref[t] = acc[t].astype(BF)

    col = pl.BlockSpec((S, W), lambda hp: (0, hp))
    bspec = pl.BlockSpec((3, 2 * BLK, 2 * BLK), lambda hp: (0, hp, 0))
    return pl.pallas_call(
        _after(body, 7, deps), name=name,
        grid=(D // W,),
        in_specs=[pl.BlockSpec((None, S, W), lambda hp: (0, 0, hp)),
                  pl.BlockSpec((None, S, W), lambda hp: (1, 0, hp)),
                  pl.BlockSpec((None, S, W), lambda hp: (2, 0, hp)),
                  col, col, col, bspec] + [ANY] * len(deps),
        out_specs=[pl.BlockSpec((3, S, W), lambda hp: (0, 0, hp)), bspec],
        out_shape=[jax.ShapeDtypeStruct((3, S, D), BF),
                   jax.ShapeDtypeStruct((3, H * BLK, 2 * BLK), F32)],
        scratch_shapes=[pltpu.VMEM((S, W), F32), pltpu.VMEM((S, W), F32),
                        pltpu.VMEM((2, S, W), BF), pltpu.VMEM((2, S, W), BF),
                        pltpu.VMEM((2, S, W), BF), pltpu.VMEM((2, S, W), BF),
                        pltpu.VMEM((2, S, W), F32), pltpu.VMEM((2, S, W), F32),
                        pltpu.VMEM((3, S, W), F32), pltpu.VMEM((3, S, W), F32)],
        compiler_params=_cparams(("parallel",)),
    )(z3, z3, z3, dob, ob, lse_b, bias3, *deps)


def _me():
    return lax.axis_index("x"), lax.axis_index("y"), lax.axis_index("c")


def _other_chips(x, y):
    return [(1 - x, y), (x, 1 - y), (1 - x, 1 - y)]


def _shard_window(ref, axis, t, shape):
    R, C = shape
    if axis == 0:
        return ref.at[pl.ds(pl.multiple_of(t * R, 128), R), :]
    return ref.at[:, pl.ds(pl.multiple_of(t * C, 128), C)]


def all_gather_weights(shards, axes, name):
    n = len(shards)
    shapes = [s.shape for s in shards]
    outs_shape = [jax.ShapeDtypeStruct((8 * R, C) if ax == 0 else (R, 8 * C), BF)
                  for (R, C), ax in zip(shapes, axes)]

    def body(*refs):
        ins, outs = refs[:n], refs[n:2 * n]
        send_sems, recv_sems, local_sems = refs[2 * n:]
        x, y, c = _me()
        me, sibling = (x, y, c), (x, y, 1 - c)
        xnb, ynb, diag = (1 - x, y), (x, 1 - y), (1 - x, 1 - y)
        south = c == 0
        relay_from = (jnp.where(south, x, 1 - x), jnp.where(south, 1 - y, y))
        relay_to = (jnp.where(south, 1 - x, x), jnp.where(south, y, 1 - y))
        barrier = pltpu.get_barrier_semaphore()
        for peer in [sibling, (*xnb, c), (*ynb, c)]:
            pl.semaphore_signal(barrier, inc=1, device_id=peer, device_id_type=MESH)
        pl.semaphore_wait(barrier, 3)

        def win(i, px, py, pc):
            return _shard_window(outs[i], axes[i], 4 * px + 2 * py + pc, shapes[i])

        def copy(i, k, block, to, src=None):
            return pltpu.make_async_remote_copy(
                src_ref=win(i, *block) if src is None else src, dst_ref=win(i, *block),
                send_sem=send_sems.at[i * 7 + k], recv_sem=recv_sems.at[i * 7 + k],
                device_id=to, device_id_type=MESH)

        mine = [pltpu.make_async_copy(ins[i], win(i, *me), local_sems.at[i]) for i in range(n)]
        for cp in mine:
            cp.start()
        sent = []
        for i in range(n):
            sent += [copy(i, 0, me, sibling, src=ins[i]), copy(i, 1, me, (*xnb, c), src=ins[i]),
                     copy(i, 2, me, (*ynb, c), src=ins[i])]
        for cp in sent:
            cp.start()
        for i in range(n):
            for k, chip in ((1, xnb), (2, ynb)):
                copy(i, k, (*chip, c), me).wait_recv()
                sent.append(copy(i, 3 + k, (*chip, c), sibling))
                sent[-1].start()
            sent.append(copy(i, 3, (*relay_from, c), (*relay_to, c)))
            sent[-1].start()
        for i in range(n):
            copy(i, 3, (*diag, c), me).wait_recv()
            sent.append(copy(i, 6, (*diag, c), sibling))
            sent[-1].start()
        for i in range(n):
            copy(i, 0, sibling, me).wait_recv()
            for k, chip in ((4, xnb), (5, ynb), (6, diag)):
                copy(i, k, (*chip, 1 - c), me).wait_recv()
        for cp in sent:
            cp.wait_send()
        for cp in mine:
            cp.wait()

    return pl.kernel(
        body, out_type=outs_shape, name=name,
        mesh=plsc.ScalarSubcoreMesh(axis_name="sequencer", num_cores=1),
        scratch_types=[pltpu.SemaphoreType.DMA((7 * n,)), pltpu.SemaphoreType.DMA((7 * n,)),
                       pltpu.SemaphoreType.DMA((n,))],
        compiler_params=pltpu.CompilerParams(collective_id=1),
    )(*shards)


def pair_exchange_grads(grads, axes, shapes, name):
    n = len(grads)

    def body(*refs):
        ins, outs = refs[:n], refs[n:2 * n]
        send_sems, recv_sems = refs[2 * n:]
        x, y, c = _me()
        sibling = (x, y, 1 - c)
        barrier = pltpu.get_barrier_semaphore()
        pl.semaphore_signal(barrier, inc=1, device_id=sibling, device_id_type=MESH)
        pl.semaphore_wait(barrier, 1)
        copies = []
        for i in range(n):
            for q in range(4):
                t = 2 * q + (1 - c)
                copies.append(pltpu.make_async_remote_copy(
                    src_ref=_shard_window(ins[i], axes[i], t, shapes[i]), dst_ref=outs[i].at[q],
                    send_sem=send_sems.at[i * 4 + q], recv_sem=recv_sems.at[i * 4 + q],
                    device_id=sibling, device_id_type=MESH))
        for cp in copies:
            cp.start()
        for cp in copies:
            cp.wait_recv()
        for cp in copies:
            cp.wait_send()

    return pl.kernel(
        body, out_type=[jax.ShapeDtypeStruct((4,) + tuple(sh), BF) for sh in shapes], name=name,
        mesh=plsc.ScalarSubcoreMesh(axis_name="sequencer", num_cores=1),
        scratch_types=[pltpu.SemaphoreType.DMA((4 * n,)), pltpu.SemaphoreType.DMA((4 * n,))],
        compiler_params=pltpu.CompilerParams(collective_id=2),
    )(*grads)


def pair_add(grads, landed, axes, shapes, c_idx, name, deps=()):
    n = len(grads)

    def body(c_ref, *refs):
        for t in range(n):
            refs[2 * n + t][...] = (refs[2 * t][...].astype(F32)
                                    + refs[2 * t + 1][...].astype(F32)).astype(BF)

    in_specs, out_specs, out_shapes, operands = [], [], [], []
    for t in range(n):
        R, C = shapes[t]
        if axes[t] == 0:
            in_specs.append(pl.BlockSpec((R, C), lambda q, c_ref: (2 * q + c_ref[0], 0)))
        else:
            in_specs.append(pl.BlockSpec((R, C), lambda q, c_ref: (0, 2 * q + c_ref[0])))
        blk = pl.BlockSpec((None, R, C), lambda q, c_ref: (q, 0, 0))
        in_specs.append(blk)
        out_specs.append(blk)
        out_shapes.append(jax.ShapeDtypeStruct((4, R, C), BF))
        operands += [grads[t], landed[t]]
    return pl.pallas_call(
        _after(body, 1 + 2 * n, deps), name=name,
        grid_spec=pltpu.PrefetchScalarGridSpec(
            num_scalar_prefetch=1, grid=(4,), in_specs=in_specs + [ANY] * len(deps),
            out_specs=out_specs),
        out_shape=out_shapes,
        compiler_params=_cparams(("parallel",)),
    )(c_idx, *operands, *deps)


def chip_exchange_grads(parts, name):
    n = len(parts)

    def body(*refs):
        ins, outs = refs[:n], refs[n:2 * n]
        send_sems, recv_sems = refs[2 * n:]
        x, y, c = _me()
        barrier = pltpu.get_barrier_semaphore()
        for px, py in _other_chips(x, y):
            pl.semaphore_signal(barrier, inc=1, device_id=(px, py, c), device_id_type=MESH)
        pl.semaphore_wait(barrier, 3)
        copies = []
        for i in range(n):
            for k, (px, py) in enumerate(_other_chips(x, y)):
                copies.append(pltpu.make_async_remote_copy(
                    src_ref=ins[i].at[2 * px + py], dst_ref=outs[i].at[k],
                    send_sem=send_sems.at[i * 3 + k], recv_sem=recv_sems.at[i * 3 + k],
                    device_id=(px, py, c), device_id_type=MESH))
        for cp in copies:
            cp.start()
        for cp in copies:
            cp.wait_recv()
        for cp in copies:
            cp.wait_send()

    return pl.kernel(
        body, out_type=[jax.ShapeDtypeStruct((3,) + tuple(p.shape[1:]), BF) for p in parts], name=name,
        mesh=plsc.ScalarSubcoreMesh(axis_name="sequencer", num_cores=1),
        scratch_types=[pltpu.SemaphoreType.DMA((3 * n,)), pltpu.SemaphoreType.DMA((3 * n,))],
        compiler_params=pltpu.CompilerParams(collective_id=3),
    )(*parts)


def all_gather_small(v, name):
    R, C = v.shape

    def body(v_ref, out_ref, send_sems, recv_sems, local_sem):
        x, y, c = _me()
        me, sibling = (x, y, c), (x, y, 1 - c)
        chips = _other_chips(x, y)

        def slot(px, py, pc):
            return out_ref.at[4 * px + 2 * py + pc]

        def copy(k, block, to, src=None):
            return pltpu.make_async_remote_copy(
                src_ref=slot(*block) if src is None else src, dst_ref=slot(*block),
                send_sem=send_sems.at[k], recv_sem=recv_sems.at[k],
                device_id=to, device_id_type=MESH)

        mine = pltpu.make_async_copy(v_ref, slot(*me), local_sem)
        mine.start()
        first = [copy(0, me, sibling, src=v_ref)]
        first += [copy(1 + j, me, (*chip, c), src=v_ref) for j, chip in enumerate(chips)]
        for cp in first:
            cp.start()
        passed = [copy(4 + j, (*chip, c), sibling) for j, chip in enumerate(chips)]
        for j, chip in enumerate(chips):
            copy(1 + j, (*chip, c), me).wait_recv()
            passed[j].start()
        copy(0, sibling, me).wait_recv()
        for j, chip in enumerate(chips):
            copy(4 + j, (*chip, 1 - c), me).wait_recv()
        for cp in first + passed:
            cp.wait_send()
        mine.wait()

    return pl.pallas_call(
        body, name=name,
        in_specs=[pl.BlockSpec(memory_space=pltpu.VMEM)],
        out_specs=pl.BlockSpec(memory_space=pltpu.VMEM),
        out_shape=jax.ShapeDtypeStruct((NDEV, R, C), F32),
        scratch_shapes=[pltpu.SemaphoreType.DMA((7,)), pltpu.SemaphoreType.DMA((7,)),
                        pltpu.SemaphoreType.DMA],
    )(v)


def _adamw(w, g, m, v):
    m = ADAM_B1 * m + (1.0 - ADAM_B1) * g
    v = ADAM_B2 * v + (1.0 - ADAM_B2) * (g * g)
    m_hat = m / (1.0 - ADAM_B1 ** ADAM_STEP)
    v_hat = v / (1.0 - ADAM_B2 ** ADAM_STEP)
    delta = -ADAM_LR * (m_hat / (jnp.sqrt(v_hat) + ADAM_EPS) + ADAM_WD * w)
    return delta, m, v


def reduce_adamw(parts, landed, params, q_idx, name, prevs, deps=()):
    n = len(parts)
    halves = 2
    in_specs, out_specs, out_shapes, operands, extra, aliases, trs = [], [], [], [], [], {}, []
    for t in range(n):
        R, C = parts[t].shape[1:]
        w, m, v, layer = params[t]
        r, c = w.shape[1:]
        tr = r // halves
        assert tr % 16 == 0 and c == C
        trs.append(tr)
        wspec = pl.BlockSpec((None, tr, c), lambda i, q_ref, layer=layer: (layer, i, 0))
        in_specs += [pl.BlockSpec((None, tr, C), lambda i, q_ref: (q_ref[0], i, 0)),
                     pl.BlockSpec((3, tr, C), lambda i, q_ref: (0, i, 0)), wspec, wspec, wspec]
        out_specs += [wspec] * 4
        out_shapes += [jax.ShapeDtypeStruct(w.shape, F32)] * 4
        operands += [parts[t], landed[t], w, m, v]
        for k, buf in enumerate(prevs[t]):
            aliases[1 + 5 * n + len(extra)] = 4 * t + k
            extra.append(buf)
    extra += list(deps)

    def body(q_ref, *refs):
        for t in range(n):
            p_ref, l_ref, w_ref, m_ref, v_ref = refs[5 * t:5 * t + 5]
            g = p_ref[...].astype(F32)
            for k in range(3):
                g = g + l_ref[k].astype(F32)
            d, mm, vv = _adamw(w_ref[...], g, m_ref[...], v_ref[...])
            outs = refs[5 * n + 4 * t:5 * n + 4 * t + 4]
            outs[0][...] = g
            outs[1][...] = d
            outs[2][...] = mm
            outs[3][...] = vv

    res = pl.pallas_call(
        _after(body, 1 + 5 * n, extra), name=name,
        grid_spec=pltpu.PrefetchScalarGridSpec(
            num_scalar_prefetch=1, grid=(halves,),
            in_specs=in_specs + [ANY] * len(extra), out_specs=out_specs),
        out_shape=out_shapes,
        input_output_aliases=aliases,
        compiler_params=_cparams(("parallel",)),
    )(q_idx, *operands, *extra)
    return [res[4 * t:4 * t + 4] for t in range(n)]


def small_reduce_adamw(gathered, w, m, v, name):
    R, C = w.shape

    def body(a_ref, w_ref, m_ref, v_ref, g_out, d_out, m_out, v_out):
        g = a_ref[0]
        for k in range(1, NDEV):
            g = g + a_ref[k]
        d, mm, vv = _adamw(w_ref[...], g, m_ref[...], v_ref[...])
        g_out[...] = g
        d_out[...] = d
        m_out[...] = mm
        v_out[...] = vv

    out = jax.ShapeDtypeStruct((R, C), F32)
    return pl.pallas_call(body, name=name, out_shape=[out] * 4,
                          compiler_params=_cparams())(gathered, w, m, v)


def _pad_cols(a, n):
    return jnp.pad(a, ((0, 0), (0, n - a.shape[1])))


def _pad_rows(a, n):
    return jnp.pad(a, ((0, n - a.shape[0]), (0, 0)))


SMALL_ROWS = 16


def _pack_small(mix, ffn, fin, taps_full, relb):
    return jnp.concatenate([
        mix, ffn, fin.reshape(1, D), taps_full.reshape(6, D),
        jnp.pad(relb.reshape(1, NUM_BUCKETS * H), ((0, 0), (0, D - NUM_BUCKETS * H)))], axis=0)


def kernel(x, mix_norm, ffn_norm, final_norm, conv_w_in, conv_kernel, conv_w_out, attn_w_qkv, attn_w_out, rel_bias, ffn_w_gate, ffn_w_up, ffn_w_down, loss_target, m_mix_norm, m_ffn_norm, m_final_norm, m_conv_w_in, m_conv_kernel, m_conv_w_out, m_attn_w_qkv, m_attn_w_out, m_rel_bias, m_ffn_w_gate, m_ffn_w_up, m_ffn_w_down, v_mix_norm, v_ffn_norm, v_final_norm, v_conv_w_in, v_conv_kernel, v_conv_w_out, v_attn_w_qkv, v_attn_w_out, v_rel_bias, v_ffn_w_gate, v_ffn_w_up, v_ffn_w_down):
    xi, yi, ci = _me()
    me = 4 * xi + 2 * yi + ci
    c_idx = jnp.reshape(ci, (1,)).astype(jnp.int32)
    q_idx = jnp.reshape(2 * xi + yi, (1,)).astype(jnp.int32)
    col0 = me * (D // NDEV)

    taps_local = jnp.zeros((2, 3, D), F32)
    taps_local = lax.dynamic_update_slice(taps_local, conv_kernel, (0, 0, col0))
    taps_pack = jnp.pad(taps_local.reshape(6, D), ((0, 2), (0, 0)))
    taps_all = all_gather_small(taps_pack, "ag_taps")
    taps_sum = jnp.sum(taps_all, axis=0)
    taps = [jnp.pad(taps_sum[3 * j:3 * j + 3], ((0, 5), (0, 0))) for j in range(2)]

    gate_t, up_t = jnp.swapaxes(ffn_w_gate, 1, 2), jnp.swapaxes(ffn_w_up, 1, 2)
    m_gate_t, m_up_t = jnp.swapaxes(m_ffn_w_gate, 1, 2), jnp.swapaxes(m_ffn_w_up, 1, 2)
    v_gate_t, v_up_t = jnp.swapaxes(v_ffn_w_gate, 1, 2), jnp.swapaxes(v_ffn_w_up, 1, 2)

    mixer_in = (conv_w_in, attn_w_qkv)
    mixer_out = (conv_w_out, attn_w_out)
    wts = []
    for i in range(DEPTH):
        j = i // 2
        w_in = all_gather_weights([mixer_in[i % 2][j].astype(BF)], (1,), f"ag_in_l{i}")
        w_out = all_gather_weights([mixer_out[i % 2][j].astype(BF)], (0,), f"ag_out_l{i}")
        w_gu = all_gather_weights([_pad_rows(gate_t[i].astype(BF), FF_SHARD_PAD),
                                   _pad_rows(up_t[i].astype(BF), FF_SHARD_PAD)], (0, 0), f"ag_up_l{i}")
        w_d = all_gather_weights([_pad_rows(ffn_w_down[i].astype(BF), FF_SHARD_PAD)], (0,),
                                 f"ag_down_l{i}")
        wts.append(list(w_in) + list(w_out) + list(w_gu) + list(w_d))

    onehot_t, band = _bucket_onehot_t()
    bias3 = bias_tables(rel_bias.T, onehot_t, band, "bias_tables").reshape(3, H * BLK, 2 * BLK)

    saved = []
    xc = x[0]
    for i in range(DEPTH):
        w_in, w_out, w_g, w_u, w_d = wts[i]
        j = i // 2
        x_mix = xc
        z3, h_mix = norm_matmul3(xc, mix_norm[i:i + 1], w_in, f"mix_in_l{i}")
        if i % 2 == 0:
            act = conv_fwd(z3, taps[j], f"conv_fwd_l{i}")
            lse_b = None
        else:
            act, lse_b = attention_fwd(z3, bias3, f"attn_fwd_l{i}")
        xc = matmul_residual(act, w_out, xc, f"mix_out_l{i}")
        x_ffn = xc
        g, u, a, h_ffn = norm_swiglu_up(xc, ffn_norm[i:i + 1], w_g, w_u, f"ffn_up_l{i}")
        xc = matmul_residual(a, w_d, xc, f"ffn_down_l{i}")
        saved.append((x_mix, h_mix, z3, act, lse_b, x_ffn, h_ffn, g, u, a))

    dx, dxb, dg_final, sq = loss_head(xc, final_norm.reshape(1, D), loss_target[0], "loss_head")
    loss = lax.psum(0.5 * jnp.sum(sq[0]) / D, ("x", "y", "c"))

    dg_mix = [None] * DEPTH
    dg_ffn = [None] * DEPTH
    dtaps = [None, None]
    dbias_all = []
    shape_in, shape_out = (D, 3 * D // NDEV), (D // NDEV, D)
    ffn_axes, ffn_shapes = (0, 0, 0), ((FF_SHARD_PAD, D),) * 3
    stacked = {}

    def pair_stage(grads, landed1, axes, shapes, tag, tok):
        parts = pair_add(grads, landed1, axes, shapes, c_idx, f"rs_add_{tag}", deps=[tok])
        return parts, chip_exchange_grads(parts, f"rs_chip_{tag}"), parts[-1]

    def adamw_stage(parts, landed2, params, tag, tok):
        names = [p[0] for p in params]
        res = reduce_adamw(parts, landed2, [p[1:] for p in params], q_idx, f"adamw_{tag}",
                           [stacked.get(nm, ()) for nm in names], deps=[tok])
        for nm, r4 in zip(names, res):
            stacked[nm] = r4
        return res[-1][0]

    tok = dxb
    mix_wait = None
    mix_chip = None
    ffn_chip = None
    for i in reversed(range(DEPTH)):
        w_in, w_out, w_g, w_u, w_d = wts[i]
        j = i // 2
        x_mix, h_mix, z3, act, lse_b, x_ffn, h_ffn, g, u, a = saved[i]
        ffn_params = [("ffn_w_gate", gate_t, m_gate_t, v_gate_t, i),
                      ("ffn_w_up", up_t, m_up_t, v_up_t, i),
                      ("ffn_w_down", ffn_w_down, m_ffn_w_down, v_ffn_w_down, i)]
        if i % 2 == 0:
            mix_params = [("conv_w_in", conv_w_in, m_conv_w_in, v_conv_w_in, j),
                          ("conv_w_out", conv_w_out, m_conv_w_out, v_conv_w_out, j)]
        else:
            mix_params = [("attn_w_qkv", attn_w_qkv, m_attn_w_qkv, v_attn_w_qkv, j),
                          ("attn_w_out", attn_w_out, m_attn_w_out, v_attn_w_out, j)]
        dgate, dup = swiglu_bwd_da(dxb, w_d, g, u, f"ffn_da_l{i}", deps=[tok])
        tok = dgate
        if mix_wait is not None:
            grads_m, landed1_m, params_m, tag_m = mix_wait
            parts_m, landed2_m, tok = pair_stage(grads_m, landed1_m, (1, 0), (shape_in, shape_out),
                                                 tag_m, tok)
            mix_chip = (parts_m, landed2_m, params_m, tag_m)
            mix_wait = None
        grads_f = matmul_tn_group([(dgate, h_ffn), (dup, h_ffn), (a, dxb)], f"ffn_dw_l{i}",
                                  deps=[tok])
        landed1_f = pair_exchange_grads(grads_f, ffn_axes, ffn_shapes, f"rs_pair_f{i}")
        tok = grads_f[-1]
        if ffn_chip is not None:
            tok = adamw_stage(*ffn_chip, tok)
            ffn_chip = None
        dx, dxb, dg_ffn[i] = matmul_normbwd(
            [(dgate, w_g, False), (dup, w_u, False)], x_ffn, ffn_norm[i:i + 1], dx, f"ffn_dh_l{i}",
            deps=[tok])
        dxb_mix = dxb
        dact = matmul_nt(dxb, w_out, f"mix_dact_l{i}", out_dtype=F32 if i % 2 == 0 else BF)
        parts_f, landed2_f, tok = pair_stage(grads_f, landed1_f, ffn_axes, ffn_shapes, f"f{i}", dact)
        ffn_chip = (parts_f, landed2_f, ffn_params, f"f{i}")
        if i % 2 == 0:
            dz3, dtaps[j] = conv_bwd(dact, z3, taps[j], f"conv_bwd_l{i}", deps=[tok])
        else:
            dz3, dbias3 = attention_bwd(z3, dact, act, lse_b, bias3, f"attn_bwd_l{i}", deps=[tok])
            dbias_all.append(dbias3.reshape(3, H, BLK * 2 * BLK))
        grads_m = matmul_tn_group([(h_mix, dz3), (act, dxb_mix)], f"mix_dw_l{i}")
        landed1_m = pair_exchange_grads(grads_m, (1, 0), (shape_in, shape_out), f"rs_pair_m{i}")
        mix_wait = (grads_m, landed1_m, mix_params, f"m{i}")
        tok = grads_m[-1]
        if mix_chip is not None:
            tok = adamw_stage(*mix_chip, tok)
            mix_chip = None
        dx, dxb, dg_mix[i] = matmul_normbwd(
            [(dz3, w_in, True)], x_mix, mix_norm[i:i + 1], dx, f"mix_dh_l{i}", deps=[tok])
        tok = dxb
    grads_m, landed1_m, params_m, tag_m = mix_wait
    parts_m, landed2_m, tok = pair_stage(grads_m, landed1_m, (1, 0), (shape_in, shape_out), tag_m, tok)
    tok = adamw_stage(*ffn_chip, tok)

    grad_relb_t = bias_grad(jnp.concatenate(dbias_all), onehot_t, "bias_grad")
    dtaps_full = jnp.stack([dtaps[0][:3], dtaps[1][:3]])
    g_small = _pack_small(jnp.concatenate([d[0:1] for d in dg_mix], axis=0),
                          jnp.concatenate([d[0:1] for d in dg_ffn], axis=0),
                          dg_final[0], dtaps_full, grad_relb_t.T)
    gathered = all_gather_small(g_small, "ag_small_grads")

    def taps_at_cols(k):
        return lax.dynamic_update_slice(jnp.zeros((2, 3, D), F32), k, (0, 0, col0))

    w_small = _pack_small(mix_norm, ffn_norm, final_norm, taps_at_cols(conv_kernel), rel_bias)
    m_small = _pack_small(m_mix_norm, m_ffn_norm, m_final_norm, taps_at_cols(m_conv_kernel), m_rel_bias)
    v_small = _pack_small(v_mix_norm, v_ffn_norm, v_final_norm, taps_at_cols(v_conv_kernel), v_rel_bias)
    small = small_reduce_adamw(gathered, w_small, m_small, v_small, "adamw_small")

    def unpack_small(p):
        taps_p = lax.dynamic_slice(p[9:15].reshape(2, 3, D), (0, 0, col0), (2, 3, D // NDEV))
        return {"mix_norm": p[0:4], "ffn_norm": p[4:8], "final_norm": p[8],
                "conv_kernel": taps_p, "rel_bias": p[15, :NUM_BUCKETS * H].reshape(NUM_BUCKETS, H)}

    small_out = [unpack_small(p) for p in small]
    adamw_stage(parts_m, landed2_m, params_m, tag_m, small[0])

    names = ["mix_norm", "ffn_norm", "final_norm", "conv_w_in", "conv_kernel", "conv_w_out",
             "attn_w_qkv", "attn_w_out", "rel_bias", "ffn_w_gate", "ffn_w_up", "ffn_w_down"]
    outs = [loss, dx.reshape(1, S, D)]
    for o in range(4):
        for nme in names:
            if nme in ("ffn_w_gate", "ffn_w_up"):
                outs.append(jnp.swapaxes(stacked[nme][o], 1, 2))
            else:
                outs.append(stacked[nme][o] if nme in stacked else small_out[o][nme])
    return tuple(outs)
```

```python
import math

import numpy as np
import jax
import jax.numpy as jnp
from jax import lax
from jax.experimental import pallas as pl
from jax.experimental.pallas import tpu as pltpu
from jax.experimental.pallas import tpu_sc as plsc

S = 2048
D = 1024
H = 16
DH = 64
DFF = 2816
NDEV = 8
DEPTH = 4
FF_SHARD = DFF // NDEV
FF_SHARD_PAD = 384
DFF_PAD = FF_SHARD_PAD * NDEV
BLK = 128
BRANCH_DILATIONS = (1, 4, 16)
NUM_BUCKETS = 32
MAX_DISTANCE = 2048
EPS = 1e-6
NEG_INF = -1e30
SCALE = DH ** -0.5

ADAM_LR = 0.001
ADAM_B1 = 0.9
ADAM_B2 = 0.999
ADAM_EPS = 1e-08
ADAM_WD = 0.01
ADAM_STEP = 10

BF = jnp.bfloat16
F32 = jnp.float32
VMEM_LIMIT_BYTES = 56 * 1024 * 1024
KSPLIT = 512
MESH = pl.DeviceIdType.MESH
ANY = pl.BlockSpec(memory_space=pl.ANY)

_NT = (((1,), (1,)), ((), ()))
_TN = (((0,), (0,)), ((), ()))


def _cparams(sem=None):
    return pltpu.CompilerParams(dimension_semantics=sem, vmem_limit_bytes=VMEM_LIMIT_BYTES)


def _after(body, n, deps):
    nd = len(deps)
    if nd == 0:
        return body

    def ordered(*refs):
        body(*refs[:n], *refs[n + nd:])
    return ordered


def _rms(x):
    return lax.rsqrt(jnp.mean(x * x, axis=-1, keepdims=True) + EPS)


def norm_matmul3(x, gain, w, name, tm=1024, tn=1024):
    per = D // tn

    def body(x_ref, g_ref, w_ref, z_ref, h_ref, hs_ref):
        @pl.when(pl.program_id(1) == 0)
        def _():
            xv = x_ref[...]
            hv = (xv * _rms(xv) * g_ref[...]).astype(BF)
            hs_ref[...] = hv
            h_ref[...] = hv
        z_ref[...] = jnp.dot(hs_ref[...], w_ref[...], preferred_element_type=F32).astype(BF)

    return pl.pallas_call(
        body, name=name,
        grid=(S // tm, 3 * D // tn),
        in_specs=[pl.BlockSpec((tm, D), lambda i, j: (i, 0)),
                  pl.BlockSpec((1, D), lambda i, j: (0, 0)),
                  pl.BlockSpec((D, tn), lambda i, j: (0, j))],
        out_specs=[pl.BlockSpec((None, tm, tn), lambda i, j: (j // per, i, j % per)),
                   pl.BlockSpec((tm, D), lambda i, j: (i, 0))],
        out_shape=[jax.ShapeDtypeStruct((3, S, D), BF), jax.ShapeDtypeStruct((S, D), BF)],
        scratch_shapes=[pltpu.VMEM((tm, D), BF)],
        compiler_params=_cparams(("parallel", "arbitrary")),
    )(x, gain, w)


def norm_swiglu_up(x, gain, wg_t, wu_t, name, tm=1024, tn=768):
    def body(x_ref, g_ref, wg_ref, wu_ref, go_ref, uo_ref, ao_ref, h_ref, hs_ref):
        @pl.when(pl.program_id(1) == 0)
        def _():
            xv = x_ref[...]
            hv = (xv * _rms(xv) * g_ref[...]).astype(BF)
            hs_ref[...] = hv
            h_ref[...] = hv
        hv = hs_ref[...]
        g = lax.dot_general(hv, wg_ref[...], _NT, preferred_element_type=F32)
        u = lax.dot_general(hv, wu_ref[...], _NT, preferred_element_type=F32)
        go_ref[...] = g.astype(BF)
        uo_ref[...] = u.astype(BF)
        ao_ref[...] = (g * jax.nn.sigmoid(g) * u).astype(BF)

    act = jax.ShapeDtypeStruct((S, DFF_PAD), BF)
    blk = pl.BlockSpec((tm, tn), lambda i, j: (i, j))
    return pl.pallas_call(
        body, name=name,
        grid=(S // tm, DFF_PAD // tn),
        in_specs=[pl.BlockSpec((tm, D), lambda i, j: (i, 0)),
                  pl.BlockSpec((1, D), lambda i, j: (0, 0)),
                  pl.BlockSpec((tn, D), lambda i, j: (j, 0)),
                  pl.BlockSpec((tn, D), lambda i, j: (j, 0))],
        out_specs=[blk, blk, blk, pl.BlockSpec((tm, D), lambda i, j: (i, 0))],
        out_shape=[act, act, act, jax.ShapeDtypeStruct((S, D), BF)],
        scratch_shapes=[pltpu.VMEM((tm, D), BF)],
        compiler_params=_cparams(("parallel", "arbitrary")),
    )(x, gain, wg_t, wu_t)


def matmul_residual(a, w, x, name, tm=1024):
    K = a.shape[1]
    tn = D if K <= D else D // 2
    ns = K // KSPLIT
    kc = K // ns

    def body(*refs):
        x_ref, o_ref = refs[2 * ns:]
        acc = x_ref[...]
        for s in range(ns):
            acc = acc + jnp.dot(refs[s][...], refs[ns + s][...], preferred_element_type=F32)
        o_ref[...] = acc

    return pl.pallas_call(
        body, name=name,
        grid=(S // tm, D // tn),
        in_specs=[pl.BlockSpec((tm, kc), lambda i, j, s=s: (i, s)) for s in range(ns)]
        + [pl.BlockSpec((kc, tn), lambda i, j, s=s: (s, j)) for s in range(ns)]
        + [pl.BlockSpec((tm, tn), lambda i, j: (i, j))],
        out_specs=pl.BlockSpec((tm, tn), lambda i, j: (i, j)),
        out_shape=jax.ShapeDtypeStruct((S, D), F32),
        compiler_params=_cparams(("parallel", "parallel")),
    )(*([a] * ns), *([w] * ns), x)


def matmul_nt(a, w, name, out_dtype=BF, tm=1024, tn=1024, deps=()):
    K = a.shape[1]
    N = w.shape[0]

    def body(a_ref, w_ref, o_ref):
        o_ref[...] = lax.dot_general(a_ref[...], w_ref[...], _NT,
                                     preferred_element_type=F32).astype(o_ref.dtype)

    return pl.pallas_call(
        _after(body, 2, deps), name=name,
        grid=(S // tm, N // tn),
        in_specs=[pl.BlockSpec((tm, K), lambda i, j: (i, 0)),
                  pl.BlockSpec((tn, K), lambda i, j: (j, 0))] + [ANY] * len(deps),
        out_specs=pl.BlockSpec((tm, tn), lambda i, j: (i, j)),
        out_shape=jax.ShapeDtypeStruct((S, N), out_dtype),
        compiler_params=_cparams(("parallel", "parallel")),
    )(a, w, *deps)


def swiglu_bwd_da(dxb, wd, g, u, name, tm=1024, tn=768, deps=()):
    def body(dx_ref, w_ref, g_ref, u_ref, dg_ref, du_ref):
        da = lax.dot_general(dx_ref[...], w_ref[...], _NT, preferred_element_type=F32)
        gv = g_ref[...].astype(F32)
        uv = u_ref[...].astype(F32)
        sig = jax.nn.sigmoid(gv)
        dg_ref[...] = (da * uv * (sig * (1.0 + gv * (1.0 - sig)))).astype(BF)
        du_ref[...] = (da * (gv * sig)).astype(BF)

    act = jax.ShapeDtypeStruct((S, DFF_PAD), BF)
    blk = pl.BlockSpec((tm, tn), lambda i, j: (i, j))
    return pl.pallas_call(
        _after(body, 4, deps), name=name,
        grid=(S // tm, DFF_PAD // tn),
        in_specs=[pl.BlockSpec((tm, D), lambda i, j: (i, 0)),
                  pl.BlockSpec((tn, D), lambda i, j: (j, 0)),
                  blk, blk] + [ANY] * len(deps),
        out_specs=[blk, blk],
        out_shape=[act, act],
        compiler_params=_cparams(("parallel", "parallel")),
    )(dxb, wd, g, u, *deps)


def matmul_tn_group(pairs, name, tm=1024, tn=512, deps=()):
    P = len(pairs)
    steps = []
    for p, (a, b) in enumerate(pairs):
        N = 3 * D if b.ndim == 3 else b.shape[1]
        steps += [(p, i, j) for i in range(a.shape[1] // tm) for j in range(N // tn)]
    T = len(steps)
    tab = np.zeros((T, 1 + 2 * P), np.int32)
    for p in range(P):
        cur = (0, 0)
        for s, (ph, i, j) in enumerate(steps):
            if ph == p:
                cur = (i, j)
            tab[s, 1 + 2 * p:3 + 2 * p] = cur
    tab[:, 0] = [ph for ph, _, _ in steps]

    in_specs, out_specs, out_shapes, operands = [], [], [], []
    per = D // tn
    for p, (a, b) in enumerate(pairs):
        ci, cj = 1 + 2 * p, 2 + 2 * p
        in_specs.append(pl.BlockSpec((S, tm), lambda s, t, ci=ci: (0, t[s, ci])))
        if b.ndim == 3:
            in_specs.append(pl.BlockSpec((None, S, tn),
                                         lambda s, t, cj=cj: (t[s, cj] // per, 0, t[s, cj] % per)))
            N = 3 * D
        else:
            in_specs.append(pl.BlockSpec((S, tn), lambda s, t, cj=cj: (0, t[s, cj])))
            N = b.shape[1]
        out_specs.append(pl.BlockSpec((tm, tn), lambda s, t, ci=ci, cj=cj: (t[s, ci], t[s, cj])))
        out_shapes.append(jax.ShapeDtypeStruct((a.shape[1], N), BF))
        operands += [a, b]

    def body(tab_ref, *refs):
        phase = tab_ref[pl.program_id(0), 0]
        for p in range(P):
            @pl.when(phase == p)
            def _(p=p):
                refs[2 * P + p][...] = lax.dot_general(
                    refs[2 * p][...], refs[2 * p + 1][...], _TN,
                    preferred_element_type=F32).astype(BF)

    return pl.pallas_call(
        _after(body, 1 + 2 * P, deps), name=name,
        grid_spec=pltpu.PrefetchScalarGridSpec(
            num_scalar_prefetch=1, grid=(T,), in_specs=in_specs + [ANY] * len(deps),
            out_specs=out_specs),
        out_shape=out_shapes,
        compiler_params=_cparams(("arbitrary",)),
    )(jnp.asarray(tab), *operands, *deps)


def matmul_normbwd(terms, x_in, gain, dx, name, tm=512, ch=256, deps=()):
    specs, operands = [], []
    for (a, w, stacked) in terms:
        if stacked:
            specs.append(pl.BlockSpec((3, tm, D), lambda i: (0, i, 0)))
        else:
            specs.append(pl.BlockSpec((tm, a.shape[1]), lambda i: (i, 0)))
        specs.append(pl.BlockSpec(w.shape, lambda i: (0, 0), pipeline_mode=pl.Buffered(1)))
        operands += [a, w]
    nt = len(terms)

    def body(*refs):
        aw = refs[:2 * nt]
        x_ref, g_ref, dx_ref, dxo_ref, dxb_ref, dg_ref, acc_ref = refs[2 * nt:]

        @pl.when(pl.program_id(0) == 0)
        def _():
            dg_ref[...] = jnp.zeros_like(dg_ref)

        dh = None
        for t, (_, _, stacked) in enumerate(terms):
            a_ref, w_ref = aw[2 * t], aw[2 * t + 1]
            if stacked:
                parts = [lax.dot_general(a_ref[k], w_ref[:, k * D:(k + 1) * D], _NT,
                                         preferred_element_type=F32) for k in range(3)]
            else:
                parts = [jnp.dot(a_ref[...], w_ref[...], preferred_element_type=F32)]
            for p in parts:
                dh = p if dh is None else dh + p
        acc_ref[...] = dh

        def chunk(c, carry):
            rows = pl.ds(pl.multiple_of(c * ch, ch), ch)
            xv = x_ref[rows, :]
            r = _rms(xv)
            xhat = xv * r
            dhc = acc_ref[rows, :]
            dg_ref[0:1, :] += jnp.sum(dhc * xhat, axis=0, keepdims=True)
            dxh = dhc * g_ref[...]
            dxn = r * (dxh - xhat * jnp.mean(dxh * xhat, axis=-1, keepdims=True))
            out = dx_ref[rows, :] + dxn
            dxo_ref[rows, :] = out
            dxb_ref[rows, :] = out.astype(BF)
            return carry
        lax.fori_loop(0, tm // ch, chunk, 0)

    row = pl.BlockSpec((tm, D), lambda i: (i, 0))
    return pl.pallas_call(
        _after(body, 2 * nt + 3, deps), name=name,
        grid=(S // tm,),
        in_specs=specs + [row, pl.BlockSpec((1, D), lambda i: (0, 0)), row] + [ANY] * len(deps),
        out_specs=[row, row, pl.BlockSpec((8, D), lambda i: (0, 0))],
        out_shape=[jax.ShapeDtypeStruct((S, D), F32), jax.ShapeDtypeStruct((S, D), BF),
                   jax.ShapeDtypeStruct((8, D), F32)],
        scratch_shapes=[pltpu.VMEM((tm, D), F32)],
        compiler_params=_cparams(("arbitrary",)),
    )(*operands, x_in, gain, dx, *deps)


def loss_head(x, gain, target, name, tm=512):
    def body(x_ref, g_ref, t_ref, dxo_ref, dxb_ref, dg_ref, sq_ref):
        @pl.when(pl.program_id(0) == 0)
        def _():
            dg_ref[...] = jnp.zeros_like(dg_ref)
            sq_ref[...] = jnp.zeros_like(sq_ref)
        xv = x_ref[...]
        r = _rms(xv)
        xhat = xv * r
        err = xhat * g_ref[...] - t_ref[...]
        sq_ref[0:1, :] += jnp.sum(err * err, axis=0, keepdims=True)
        dy = err * (1.0 / D)
        dg_ref[0:1, :] += jnp.sum(dy * xhat, axis=0, keepdims=True)
        dxh = dy * g_ref[...]
        out = r * (dxh - xhat * jnp.mean(dxh * xhat, axis=-1, keepdims=True))
        dxo_ref[...] = out
        dxb_ref[...] = out.astype(BF)

    row = pl.BlockSpec((tm, D), lambda i: (i, 0))
    acc = pl.BlockSpec((8, D), lambda i: (0, 0))
    return pl.pallas_call(
        body, name=name,
        grid=(S // tm,),
        in_specs=[row, pl.BlockSpec((1, D), lambda i: (0, 0)), row],
        out_specs=[row, row, acc, acc],
        out_shape=[jax.ShapeDtypeStruct((S, D), F32), jax.ShapeDtypeStruct((S, D), BF),
                   jax.ShapeDtypeStruct((8, D), F32), jax.ShapeDtypeStruct((8, D), F32)],
        compiler_params=_cparams(("arbitrary",)),
    )(x, gain, target)


def _shift_down(p, n, row):
    return jnp.where(row >= n, pltpu.roll(p, n, axis=0), 0.0)


def _shift_up(p, n, row):
    return jnp.where(row < S - n, pltpu.roll(p, S - n, axis=0), 0.0)


def conv_fwd(z3, taps, name, tn=128):
    def body(z_ref, k_ref, m_ref):
        b = z_ref[0].astype(F32)
        p = z_ref[1].astype(F32) * z_ref[2].astype(F32)
        row = lax.broadcasted_iota(jnp.int32, p.shape, 0)
        y = (k_ref[2:3, :] * p + k_ref[1:2, :] * _shift_down(p, 1, row)
             + k_ref[0:1, :] * _shift_down(p, 2, row))
        m_ref[...] = (b * y).astype(BF)

    return pl.pallas_call(
        body, name=name,
        grid=(D // tn,),
        in_specs=[pl.BlockSpec((3, S, tn), lambda j: (0, 0, j)),
                  pl.BlockSpec((8, tn), lambda j: (0, j))],
        out_specs=pl.BlockSpec((S, tn), lambda j: (0, j)),
        out_shape=jax.ShapeDtypeStruct((S, D), BF),
        compiler_params=_cparams(("parallel",)),
    )(z3, taps)


def conv_bwd(dm, z3, taps, name, tn=128, deps=()):
    def body(dm_ref, z_ref, k_ref, dz_ref, dk_ref):
        dmv = dm_ref[...]
        b = z_ref[0].astype(F32)
        c = z_ref[1].astype(F32)
        u = z_ref[2].astype(F32)
        p = c * u
        row = lax.broadcasted_iota(jnp.int32, p.shape, 0)
        p1 = _shift_down(p, 1, row)
        p2 = _shift_down(p, 2, row)
        y = k_ref[2:3, :] * p + k_ref[1:2, :] * p1 + k_ref[0:1, :] * p2
        dy = dmv * b
        dz_ref[0] = (dmv * y).astype(BF)
        dp = (k_ref[2:3, :] * dy + k_ref[1:2, :] * _shift_up(dy, 1, row)
              + k_ref[0:1, :] * _shift_up(dy, 2, row))
        dz_ref[1] = (dp * u).astype(BF)
        dz_ref[2] = (dp * c).astype(BF)
        dk_ref[...] = jnp.zeros_like(dk_ref)
        dk_ref[0:1, :] = jnp.sum(dy * p2, axis=0, keepdims=True)
        dk_ref[1:2, :] = jnp.sum(dy * p1, axis=0, keepdims=True)
        dk_ref[2:3, :] = jnp.sum(dy * p, axis=0, keepdims=True)

    return pl.pallas_call(
        _after(body, 3, deps), name=name,
        grid=(D // tn,),
        in_specs=[pl.BlockSpec((S, tn), lambda j: (0, j)),
                  pl.BlockSpec((3, S, tn), lambda j: (0, 0, j)),
                  pl.BlockSpec((8, tn), lambda j: (0, j))] + [ANY] * len(deps),
        out_specs=[pl.BlockSpec((3, S, tn), lambda j: (0, 0, j)),
                   pl.BlockSpec((8, tn), lambda j: (0, j))],
        out_shape=[jax.ShapeDtypeStruct((3, S, D), BF), jax.ShapeDtypeStruct((8, D), F32)],
        compiler_params=_cparams(("parallel",)),
    )(dm, z3, taps, *deps)


def _t5_bucket(dist):
    exact = NUM_BUCKETS // 2
    df = jnp.maximum(dist, 1).astype(jnp.float32)
    large = exact + (jnp.log(df / exact) / math.log(MAX_DISTANCE / exact)
                     * (NUM_BUCKETS - exact)).astype(jnp.int32)
    large = jnp.minimum(large, NUM_BUCKETS - 1)
    return jnp.where(dist < exact, dist, large)


def _bucket_onehot_t():
    qi = jnp.arange(BLK)[:, None]
    ki = jnp.arange(2 * BLK)[None, :]
    rel = qi + BLK - ki
    band = ((rel >= 0) & (rel <= BLK)).reshape(1, -1).astype(F32)
    hots = []
    for d in BRANCH_DILATIONS:
        bucket = _t5_bucket(jnp.clip(rel, 0) * d).reshape(1, -1)
        hots.append((jnp.arange(NUM_BUCKETS)[:, None] == bucket).astype(F32))
    return jnp.stack(hots), band


def bias_tables(rel_bias_t, onehot_t, band, name):
    def body(rb_ref, oh_ref, band_ref, o_ref):
        b = jnp.dot(rb_ref[...], oh_ref[...], preferred_element_type=F32,
                    precision=lax.Precision.HIGHEST)
        o_ref[...] = jnp.where(band_ref[...] > 0.5, b, NEG_INF)

    n = BLK * 2 * BLK
    return pl.pallas_call(
        body, name=name,
        grid=(3,),
        in_specs=[pl.BlockSpec((H, NUM_BUCKETS), lambda g: (0, 0)),
                  pl.BlockSpec((None, NUM_BUCKETS, n), lambda g: (g, 0, 0)),
                  pl.BlockSpec((1, n), lambda g: (0, 0))],
        out_specs=pl.BlockSpec((None, H, n), lambda g: (g, 0, 0)),
        out_shape=jax.ShapeDtypeStruct((3, H, n), F32),
        compiler_params=_cparams(("parallel",)),
    )(rel_bias_t, onehot_t, band)


def bias_grad(dbias, onehot_t, name):
    def body(db_ref, oh_ref, o_ref):
        @pl.when(pl.program_id(0) == 0)
        def _():
            o_ref[...] = jnp.zeros_like(o_ref)
        o_ref[...] += lax.dot_general(db_ref[...], oh_ref[...], _NT, preferred_element_type=F32,
                                      precision=lax.Precision.HIGHEST)

    n = BLK * 2 * BLK
    return pl.pallas_call(
        body, name=name,
        grid=(dbias.shape[0],),
        in_specs=[pl.BlockSpec((None, H, n), lambda g: (g, 0, 0)),
                  pl.BlockSpec((None, NUM_BUCKETS, n), lambda g: (g % 3, 0, 0))],
        out_specs=pl.BlockSpec((H, NUM_BUCKETS), lambda g: (0, 0)),
        out_shape=jax.ShapeDtypeStruct((H, NUM_BUCKETS), F32),
        compiler_params=_cparams(("arbitrary",)),
    )(dbias, onehot_t)


def _head_masks():
    lane = lax.broadcasted_iota(jnp.int32, (1, 2 * DH), 1)
    return (lane < DH, lane >= DH)


def _stack_heads(x, masks):
    zero = jnp.zeros_like(x)
    return jnp.concatenate([jnp.where(masks[0], x, zero), jnp.where(masks[1], x, zero)], axis=0)


def _deinterleave(src_ref, dst_ref, d, dtype):
    L = S // d
    for r in range(d):
        dst_ref[r * L:(r + 1) * L, :] = src_ref[pl.ds(r, L, stride=d), :].astype(dtype)


def _branch_loops(d, block):
    L = S // d
    nb = L // BLK

    def first(base):
        block(base, base, BLK, True)

    def later(base, n):
        q0 = pl.multiple_of(base + n * BLK, BLK)
        block(q0, pl.multiple_of(q0 - BLK, BLK), 2 * BLK, False)

    if d == 1:
        unroll = 3
        assert (nb - 1) % unroll == 0
        first(0)

        def trip(it, c):
            for u in range(unroll):
                later(0, 1 + it * unroll + u)
            return c
        lax.fori_loop(0, (nb - 1) // unroll, trip, 0)
    elif nb > 1:
        def residue(r, c):
            base = pl.multiple_of(r * L, BLK)
            first(base)
            for n in range(1, nb):
                later(base, n)
            return c
        lax.fori_loop(0, d, residue, 0)
    else:
        unroll = 4
        assert d % unroll == 0

        def trip(it, c):
            for u in range(unroll):
                first(pl.multiple_of((it * unroll + u) * L, BLK))
            return c
        lax.fori_loop(0, d // unroll, trip, 0)


def attention_fwd(z3, bias3, name):
    W = 2 * DH
    CH = 256

    def body(q_ref, k_ref, v_ref, b_ref, o_ref, lse_ref, stage, qd, kd, vd, od, ld, on, ln):
        masks = _head_masks()
        for src, dst in ((q_ref, qd), (k_ref, kd), (v_ref, vd)):
            stage[...] = src[...].astype(F32)
            for gi, d in enumerate(BRANCH_DILATIONS[1:]):
                _deinterleave(stage, dst.at[gi], d, BF)

        for g, d in enumerate(BRANCH_DILATIONS):
            qs, ks, vs = (q_ref, k_ref, v_ref) if d == 1 else (qd.at[g - 1], kd.at[g - 1], vd.at[g - 1])
            o_dst, l_dst = (on.at[0], ln.at[0]) if d == 1 else (od, ld)

            def block(q0, k0, nk, first, g=g, qs=qs, ks=ks, vs=vs, o_dst=o_dst, l_dst=l_dst):
                q2 = _stack_heads(qs[pl.ds(q0, BLK), :], masks)
                kk = ks[pl.ds(k0, nk), :]
                vv = vs[pl.ds(k0, nk), :]
                bias = b_ref[g][:, BLK:] if first else b_ref[g]
                s = lax.dot_general(q2, kk, _NT, preferred_element_type=F32) * SCALE + bias
                mx = jnp.max(s, axis=1, keepdims=True)
                p = jnp.exp(s - mx)
                l = jnp.sum(p, axis=1, keepdims=True)
                o2 = jnp.dot(p.astype(BF), vv, preferred_element_type=F32) / l
                lse2 = mx + jnp.log(l)
                o_dst[pl.ds(q0, BLK), :] = jnp.where(masks[0], o2[:BLK], o2[BLK:])
                l_dst[pl.ds(q0, BLK), :] = jnp.where(masks[0], lse2[:BLK], lse2[BLK:])

            _branch_loops(d, block)
            if d > 1:
                L = S // d
                for r in range(d):
                    on[g, pl.ds(r, L, stride=d), :] = od[r * L:(r + 1) * L, :]
                    ln[g, pl.ds(r, L, stride=d), :] = ld[r * L:(r + 1) * L, :]

        def join(c, carry):
            rows = pl.ds(pl.multiple_of(c * CH, CH), CH)
            a, b, cc = ln[0, rows, :], ln[1, rows, :], ln[2, rows, :]
            mx = jnp.maximum(jnp.maximum(a, b), cc)
            ea, eb, ec = jnp.exp(a - mx), jnp.exp(b - mx), jnp.exp(cc - mx)
            tot = ea + eb + ec
            o_ref[rows, :] = ((ea * on[0, rows, :] + eb * on[1, rows, :] + ec * on[2, rows, :])
                              / tot).astype(BF)
            lse_ref[rows, :] = mx + jnp.log(tot)
            return carry
        lax.fori_loop(0, S // CH, join, 0)

    col = pl.BlockSpec((S, W), lambda hp: (0, hp))
    return pl.pallas_call(
        body, name=name,
        grid=(D // W,),
        in_specs=[pl.BlockSpec((None, S, W), lambda hp: (0, 0, hp)),
                  pl.BlockSpec((None, S, W), lambda hp: (1, 0, hp)),
                  pl.BlockSpec((None, S, W), lambda hp: (2, 0, hp)),
                  pl.BlockSpec((3, 2 * BLK, 2 * BLK), lambda hp: (0, hp, 0))],
        out_specs=[col, col],
        out_shape=[jax.ShapeDtypeStruct((S, D), BF), jax.ShapeDtypeStruct((S, D), F32)],
        scratch_shapes=[pltpu.VMEM((S, W), F32),
                        pltpu.VMEM((2, S, W), BF), pltpu.VMEM((2, S, W), BF), pltpu.VMEM((2, S, W), BF),
                        pltpu.VMEM((S, W), F32), pltpu.VMEM((S, W), F32),
                        pltpu.VMEM((3, S, W), F32), pltpu.VMEM((3, S, W), F32)],
        compiler_params=_cparams(("parallel",)),
    )(z3, z3, z3, bias3)


def attention_bwd(z3, dob, ob, lse_b, bias3, name, deps=()):
    W = 2 * DH
    CH = 256

    def body(q_ref, k_ref, v_ref, do_ref, o_ref, lse_ref, b_ref, dz_ref, db_ref,
             stage, delta, qd, kd, vd, dod, lsd, dld, res, acc):
        masks = _head_masks()

        def rowsum(c, carry):
            rows = pl.ds(pl.multiple_of(c * CH, CH), CH)
            prod = do_ref[rows, :].astype(F32) * o_ref[rows, :].astype(F32)
            sa = jnp.sum(jnp.where(masks[0], prod, 0.0), axis=1, keepdims=True)
            sb = jnp.sum(jnp.where(masks[1], prod, 0.0), axis=1, keepdims=True)
            delta[rows, :] = jnp.where(masks[0], sa, sb)
            return carry
        lax.fori_loop(0, S // CH, rowsum, 0)

        for src, dst in ((q_ref, qd), (k_ref, kd), (v_ref, vd), (do_ref, dod)):
            stage[...] = src[...].astype(F32)
            for gi, d in enumerate(BRANCH_DILATIONS[1:]):
                _deinterleave(stage, dst.at[gi], d, BF)
        for gi, d in enumerate(BRANCH_DILATIONS[1:]):
            _deinterleave(lse_ref, lsd.at[gi], d, F32)
            _deinterleave(delta, dld.at[gi], d, F32)

        db_ref[...] = jnp.zeros_like(db_ref)
        for g, d in enumerate(BRANCH_DILATIONS):
            if d == 1:
                qs, ks, vs, dos, ls, dl = q_ref, k_ref, v_ref, do_ref, lse_ref, delta
            else:
                qs, ks, vs, dos = qd.at[g - 1], kd.at[g - 1], vd.at[g - 1], dod.at[g - 1]
                ls, dl = lsd.at[g - 1], dld.at[g - 1]
            res[1] = jnp.zeros((S, W), F32)
            res[2] = jnp.zeros((S, W), F32)

            def block(q0, k0, nk, first, g=g, qs=qs, ks=ks, vs=vs, dos=dos, ls=ls, dl=dl):
                kk = ks[pl.ds(k0, nk), :]
                vv = vs[pl.ds(k0, nk), :]
                q2 = _stack_heads(qs[pl.ds(q0, BLK), :], masks)
                do2 = _stack_heads(dos[pl.ds(q0, BLK), :], masks)
                lse_blk = ls[pl.ds(q0, BLK), :]
                del_blk = dl[pl.ds(q0, BLK), :]
                lse2 = jnp.concatenate([lse_blk[:, 0:1], lse_blk[:, DH:DH + 1]], axis=0)
                del2 = jnp.concatenate([del_blk[:, 0:1], del_blk[:, DH:DH + 1]], axis=0)
                bias = b_ref[g][:, BLK:] if first else b_ref[g]
                s = lax.dot_general(q2, kk, _NT, preferred_element_type=F32) * SCALE + bias
                p = jnp.exp(s - lse2)
                dp = lax.dot_general(do2, vv, _NT, preferred_element_type=F32)
                ds = p * (dp - del2)
                if first:
                    db_ref[g, :, BLK:] += ds
                else:
                    db_ref[g] += ds
                dsb = ds.astype(BF)
                dq2 = jnp.dot(dsb, kk, preferred_element_type=F32) * SCALE
                res[0, pl.ds(q0, BLK), :] = jnp.where(masks[0], dq2[:BLK], dq2[BLK:])
                res[1, pl.ds(k0, nk), :] += lax.dot_general(dsb, q2, _TN,
                                                            preferred_element_type=F32) * SCALE
                res[2, pl.ds(k0, nk), :] += lax.dot_general(p.astype(BF), do2, _TN,
                                                            preferred_element_type=F32)

            _branch_loops(d, block)
            L = S // d
            for t in range(3):
                if d == 1:
                    acc[t] = res[t]
                else:
                    for r in range(d):
                        acc[t, pl.ds(r, L, stride=d), :] = (acc[t, pl.ds(r, L, stride=d), :]
                                                            + res[t, r * L:(r + 1) * L, :])
        for t in range(3):
            dz_ref[t] = acc[t].astype(BF)

    col = pl.BlockSpec((S, W), lambda hp: (0, hp))
    bspec = pl.BlockSpec((3, 2 * BLK, 2 * BLK), lambda hp: (0, hp, 0))
    return pl.pallas_call(
        _after(body, 7, deps), name=name,
        grid=(D // W,),
        in_specs=[pl.BlockSpec((None, S, W), lambda hp: (0, 0, hp)),
                  pl.BlockSpec((None, S, W), lambda hp: (1, 0, hp)),
                  pl.BlockSpec((None, S, W), lambda hp: (2, 0, hp)),
                  col, col, col, bspec] + [ANY] * len(deps),
        out_specs=[pl.BlockSpec((3, S, W), lambda hp: (0, 0, hp)), bspec],
        out_shape=[jax.ShapeDtypeStruct((3, S, D), BF),
                   jax.ShapeDtypeStruct((3, H * BLK, 2 * BLK), F32)],
        scratch_shapes=[pltpu.VMEM((S, W), F32), pltpu.VMEM((S, W), F32),
                        pltpu.VMEM((2, S, W), BF), pltpu.VMEM((2, S, W), BF),
                        pltpu.VMEM((2, S, W), BF), pltpu.VMEM((2, S, W), BF),
                        pltpu.VMEM((2, S, W), F32), pltpu.VMEM((2, S, W), F32),
                        pltpu.VMEM((3, S, W), F32), pltpu.VMEM((3, S, W), F32)],
        compiler_params=_cparams(("parallel",)),
    )(z3, z3, z3, dob, ob, lse_b, bias3, *deps)


def _me():
    return lax.axis_index("x"), lax.axis_index("y"), lax.axis_index("c")


def _other_chips(x, y):
    return [(1 - x, y), (x, 1 - y), (1 - x, 1 - y)]


def _shard_window(ref, axis, t, shape):
    R, C = shape
    if axis == 0:
        return ref.at[pl.ds(pl.multiple_of(t * R, 128), R), :]
    return ref.at[:, pl.ds(pl.multiple_of(t * C, 128), C)]


def all_gather_weights(shards, axes, name):
    n = len(shards)
    shapes = [s.shape for s in shards]
    outs_shape = [jax.ShapeDtypeStruct((8 * R, C) if ax == 0 else (R, 8 * C), BF)
                  for (R, C), ax in zip(shapes, axes)]

    def body(*refs):
        ins, outs = refs[:n], refs[n:2 * n]
        send_sems, recv_sems, local_sems = refs[2 * n:]
        x, y, c = _me()
        me, sibling = (x, y, c), (x, y, 1 - c)
        xnb, ynb, diag = (1 - x, y), (x, 1 - y), (1 - x, 1 - y)
        south = c == 0
        relay_from = (jnp.where(south, x, 1 - x), jnp.where(south, 1 - y, y))
        relay_to = (jnp.where(south, 1 - x, x), jnp.where(south, y, 1 - y))
        barrier = pltpu.get_barrier_semaphore()
        for peer in [sibling, (*xnb, c), (*ynb, c)]:
            pl.semaphore_signal(barrier, inc=1, device_id=peer, device_id_type=MESH)
        pl.semaphore_wait(barrier, 3)

        def win(i, px, py, pc):
            return _shard_window(outs[i], axes[i], 4 * px + 2 * py + pc, shapes[i])

        def copy(i, k, block, to, src=None):
            return pltpu.make_async_remote_copy(
                src_ref=win(i, *block) if src is None else src, dst_ref=win(i, *block),
                send_sem=send_sems.at[i * 7 + k], recv_sem=recv_sems.at[i * 7 + k],
                device_id=to, device_id_type=MESH)

        mine = [pltpu.make_async_copy(ins[i], win(i, *me), local_sems.at[i]) for i in range(n)]
        for cp in mine:
            cp.start()
        sent = []
        for i in range(n):
            sent += [copy(i, 0, me, sibling, src=ins[i]), copy(i, 1, me, (*xnb, c), src=ins[i]),
                     copy(i, 2, me, (*ynb, c), src=ins[i])]
        for cp in sent:
            cp.start()
        for i in range(n):
            for k, chip in ((1, xnb), (2, ynb)):
                copy(i, k, (*chip, c), me).wait_recv()
                sent.append(copy(i, 3 + k, (*chip, c), sibling))
                sent[-1].start()
            sent.append(copy(i, 3, (*relay_from, c), (*relay_to, c)))
            sent[-1].start()
        for i in range(n):
            copy(i, 3, (*diag, c), me).wait_recv()
            sent.append(copy(i, 6, (*diag, c), sibling))
            sent[-1].start()
        for i in range(n):
            copy(i, 0, sibling, me).wait_recv()
            for k, chip in ((4, xnb), (5, ynb), (6, diag)):
                copy(i, k, (*chip, 1 - c), me).wait_recv()
        for cp in sent:
            cp.wait_send()
        for cp in mine:
            cp.wait()

    return pl.kernel(
        body, out_type=outs_shape, name=name,
        mesh=plsc.ScalarSubcoreMesh(axis_name="sequencer", num_cores=1),
        scratch_types=[pltpu.SemaphoreType.DMA((7 * n,)), pltpu.SemaphoreType.DMA((7 * n,)),
                       pltpu.SemaphoreType.DMA((n,))],
        compiler_params=pltpu.CompilerParams(collective_id=1),
    )(*shards)


def pair_exchange_grads(grads, axes, shapes, name):
    n = len(grads)

    def body(*refs):
        ins, outs = refs[:n], refs[n:2 * n]
        send_sems, recv_sems = refs[2 * n:]
        x, y, c = _me()
        sibling = (x, y, 1 - c)
        barrier = pltpu.get_barrier_semaphore()
        pl.semaphore_signal(barrier, inc=1, device_id=sibling, device_id_type=MESH)
        pl.semaphore_wait(barrier, 1)
        copies = []
        for i in range(n):
            for q in range(4):
                t = 2 * q + (1 - c)
                copies.append(pltpu.make_async_remote_copy(
                    src_ref=_shard_window(ins[i], axes[i], t, shapes[i]), dst_ref=outs[i].at[q],
                    send_sem=send_sems.at[i * 4 + q], recv_sem=recv_sems.at[i * 4 + q],
                    device_id=sibling, device_id_type=MESH))
        for cp in copies:
            cp.start()
        for cp in copies:
            cp.wait_recv()
        for cp in copies:
            cp.wait_send()

    return pl.kernel(
        body, out_type=[jax.ShapeDtypeStruct((4,) + tuple(sh), BF) for sh in shapes], name=name,
        mesh=plsc.ScalarSubcoreMesh(axis_name="sequencer", num_cores=1),
        scratch_types=[pltpu.SemaphoreType.DMA((4 * n,)), pltpu.SemaphoreType.DMA((4 * n,))],
        compiler_params=pltpu.CompilerParams(collective_id=2),
    )(*grads)


def pair_add(grads, landed, axes, shapes, c_idx, name, deps=()):
    n = len(grads)

    def body(c_ref, *refs):
        for t in range(n):
            refs[2 * n + t][...] = (refs[2 * t][...].astype(F32)
                                    + refs[2 * t + 1][...].astype(F32)).astype(BF)

    in_specs, out_specs, out_shapes, operands = [], [], [], []
    for t in range(n):
        R, C = shapes[t]
        if axes[t] == 0:
            in_specs.append(pl.BlockSpec((R, C), lambda q, c_ref: (2 * q + c_ref[0], 0)))
        else:
            in_specs.append(pl.BlockSpec((R, C), lambda q, c_ref: (0, 2 * q + c_ref[0])))
        blk = pl.BlockSpec((None, R, C), lambda q, c_ref: (q, 0, 0))
        in_specs.append(blk)
        out_specs.append(blk)
        out_shapes.append(jax.ShapeDtypeStruct((4, R, C), BF))
        operands += [grads[t], landed[t]]
    return pl.pallas_call(
        _after(body, 1 + 2 * n, deps), name=name,
        grid_spec=pltpu.PrefetchScalarGridSpec(
            num_scalar_prefetch=1, grid=(4,), in_specs=in_specs + [ANY] * len(deps),
            out_specs=out_specs),
        out_shape=out_shapes,
        compiler_params=_cparams(("parallel",)),
    )(c_idx, *operands, *deps)


def chip_exchange_grads(parts, name):
    n = len(parts)

    def body(*refs):
        ins, outs = refs[:n], refs[n:2 * n]
        send_sems, recv_sems = refs[2 * n:]
        x, y, c = _me()
        barrier = pltpu.get_barrier_semaphore()
        for px, py in _other_chips(x, y):
            pl.semaphore_signal(barrier, inc=1, device_id=(px, py, c), device_id_type=MESH)
        pl.semaphore_wait(barrier, 3)
        copies = []
        for i in range(n):
            for k, (px, py) in enumerate(_other_chips(x, y)):
                copies.append(pltpu.make_async_remote_copy(
                    src_ref=ins[i].at[2 * px + py], dst_ref=outs[i].at[k],
                    send_sem=send_sems.at[i * 3 + k], recv_sem=recv_sems.at[i * 3 + k],
                    device_id=(px, py, c), device_id_type=MESH))
        for cp in copies:
            cp.start()
        for cp in copies:
            cp.wait_recv()
        for cp in copies:
            cp.wait_send()

    return pl.kernel(
        body, out_type=[jax.ShapeDtypeStruct((3,) + tuple(p.shape[1:]), BF) for p in parts], name=name,
        mesh=plsc.ScalarSubcoreMesh(axis_name="sequencer", num_cores=1),
        scratch_types=[pltpu.SemaphoreType.DMA((3 * n,)), pltpu.SemaphoreType.DMA((3 * n,))],
        compiler_params=pltpu.CompilerParams(collective_id=3),
    )(*parts)


def all_gather_small(v, name):
    R, C = v.shape

    def body(v_ref, out_ref, send_sems, recv_sems, local_sem):
        x, y, c = _me()
        me, sibling = (x, y, c), (x, y, 1 - c)
        chips = _other_chips(x, y)

        def slot(px, py, pc):
            return out_ref.at[4 * px + 2 * py + pc]

        def copy(k, block, to, src=None):
            return pltpu.make_async_remote_copy(
                src_ref=slot(*block) if src is None else src, dst_ref=slot(*block),
                send_sem=send_sems.at[k], recv_sem=recv_sems.at[k],
                device_id=to, device_id_type=MESH)

        mine = pltpu.make_async_copy(v_ref, slot(*me), local_sem)
        mine.start()
        first = [copy(0, me, sibling, src=v_ref)]
        first += [copy(1 + j, me, (*chip, c), src=v_ref) for j, chip in enumerate(chips)]
        for cp in first:
            cp.start()
        passed = [copy(4 + j, (*chip, c), sibling) for j, chip in enumerate(chips)]
        for j, chip in enumerate(chips):
            copy(1 + j, (*chip, c), me).wait_recv()
            passed[j].start()
        copy(0, sibling, me).wait_recv()
        for j, chip in enumerate(chips):
            copy(4 + j, (*chip, 1 - c), me).wait_recv()
        for cp in first + passed:
            cp.wait_send()
        mine.wait()

    return pl.pallas_call(
        body, name=name,
        in_specs=[pl.BlockSpec(memory_space=pltpu.VMEM)],
        out_specs=pl.BlockSpec(memory_space=pltpu.VMEM),
        out_shape=jax.ShapeDtypeStruct((NDEV, R, C), F32),
        scratch_shapes=[pltpu.SemaphoreType.DMA((7,)), pltpu.SemaphoreType.DMA((7,)),
                        pltpu.SemaphoreType.DMA],
    )(v)


def _adamw(w, g, m, v):
    m = ADAM_B1 * m + (1.0 - ADAM_B1) * g
    v = ADAM_B2 * v + (1.0 - ADAM_B2) * (g * g)
    m_hat = m / (1.0 - ADAM_B1 ** ADAM_STEP)
    v_hat = v / (1.0 - ADAM_B2 ** ADAM_STEP)
    delta = -ADAM_LR * (m_hat / (jnp.sqrt(v_hat) + ADAM_EPS) + ADAM_WD * w)
    return delta, m, v


def reduce_adamw(parts, landed, params, q_idx, name, prevs, deps=()):
    n = len(parts)
    halves = 2
    in_specs, out_specs, out_shapes, operands, extra, aliases, trs = [], [], [], [], [], {}, []
    for t in range(n):
        R, C = parts[t].shape[1:]
        w, m, v, layer = params[t]
        r, c = w.shape[1:]
        tr = r // halves
        assert tr % 16 == 0 and c == C
        trs.append(tr)
        wspec = pl.BlockSpec((None, tr, c), lambda i, q_ref, layer=layer: (layer, i, 0))
        in_specs += [pl.BlockSpec((None, tr, C), lambda i, q_ref: (q_ref[0], i, 0)),
                     pl.BlockSpec((3, tr, C), lambda i, q_ref: (0, i, 0)), wspec, wspec, wspec]
        out_specs += [wspec] * 4
        out_shapes += [jax.ShapeDtypeStruct(w.shape, F32)] * 4
        operands += [parts[t], landed[t], w, m, v]
        for k, buf in enumerate(prevs[t]):
            aliases[1 + 5 * n + len(extra)] = 4 * t + k
            extra.append(buf)
    extra += list(deps)

    def body(q_ref, *refs):
        for t in range(n):
            p_ref, l_ref, w_ref, m_ref, v_ref = refs[5 * t:5 * t + 5]
            g = p_ref[...].astype(F32)
            for k in range(3):
                g = g + l_ref[k].astype(F32)
            d, mm, vv = _adamw(w_ref[...], g, m_ref[...], v_ref[...])
            outs = refs[5 * n + 4 * t:5 * n + 4 * t + 4]
            outs[0][...] = g
            outs[1][...] = d
            outs[2][...] = mm
            outs[3][...] = vv

    res = pl.pallas_call(
        _after(body, 1 + 5 * n, extra), name=name,
        grid_spec=pltpu.PrefetchScalarGridSpec(
            num_scalar_prefetch=1, grid=(halves,),
            in_specs=in_specs + [ANY] * len(extra), out_specs=out_specs),
        out_shape=out_shapes,
        input_output_aliases=aliases,
        compiler_params=_cparams(("parallel",)),
    )(q_idx, *operands, *extra)
    return [res[4 * t:4 * t + 4] for t in range(n)]


def small_reduce_adamw(gathered, w, m, v, name):
    R, C = w.shape

    def body(a_ref, w_ref, m_ref, v_ref, g_out, d_out, m_out, v_out):
        g = a_ref[0]
        for k in range(1, NDEV):
            g = g + a_ref[k]
        d, mm, vv = _adamw(w_ref[...], g, m_ref[...], v_ref[...])
        g_out[...] = g
        d_out[...] = d
        m_out[...] = mm
        v_out[...] = vv

    out = jax.ShapeDtypeStruct((R, C), F32)
    return pl.pallas_call(body, name=name, out_shape=[out] * 4,
                          compiler_params=_cparams())(gathered, w, m, v)


def _pad_cols(a, n):
    return jnp.pad(a, ((0, 0), (0, n - a.shape[1])))


def _pad_rows(a, n):
    return jnp.pad(a, ((0, n - a.shape[0]), (0, 0)))


SMALL_ROWS = 16


def _pack_small(mix, ffn, fin, taps_full, relb):
    return jnp.concatenate([
        mix, ffn, fin.reshape(1, D), taps_full.reshape(6, D),
        jnp.pad(relb.reshape(1, NUM_BUCKETS * H), ((0, 0), (0, D - NUM_BUCKETS * H)))], axis=0)


def kernel(x, mix_norm, ffn_norm, final_norm, conv_w_in, conv_kernel, conv_w_out, attn_w_qkv, attn_w_out, rel_bias, ffn_w_gate, ffn_w_up, ffn_w_down, loss_target, m_mix_norm, m_ffn_norm, m_final_norm, m_conv_w_in, m_conv_kernel, m_conv_w_out, m_attn_w_qkv, m_attn_w_out, m_rel_bias, m_ffn_w_gate, m_ffn_w_up, m_ffn_w_down, v_mix_norm, v_ffn_norm, v_final_norm, v_conv_w_in, v_conv_kernel, v_conv_w_out, v_attn_w_qkv, v_attn_w_out, v_rel_bias, v_ffn_w_gate, v_ffn_w_up, v_ffn_w_down):
    xi, yi, ci = _me()
    me = 4 * xi + 2 * yi + ci
    c_idx = jnp.reshape(ci, (1,)).astype(jnp.int32)
    q_idx = jnp.reshape(2 * xi + yi, (1,)).astype(jnp.int32)
    col0 = me * (D // NDEV)

    taps_local = jnp.zeros((2, 3, D), F32)
    taps_local = lax.dynamic_update_slice(taps_local, conv_kernel, (0, 0, col0))
    taps_pack = jnp.pad(taps_local.reshape(6, D), ((0, 2), (0, 0)))
    taps_all = all_gather_small(taps_pack, "ag_taps")
    taps_sum = jnp.sum(taps_all, axis=0)
    taps = [jnp.pad(taps_sum[3 * j:3 * j + 3], ((0, 5), (0, 0))) for j in range(2)]

    gate_t, up_t = jnp.swapaxes(ffn_w_gate, 1, 2), jnp.swapaxes(ffn_w_up, 1, 2)
    m_gate_t, m_up_t = jnp.swapaxes(m_ffn_w_gate, 1, 2), jnp.swapaxes(m_ffn_w_up, 1, 2)
    v_gate_t, v_up_t = jnp.swapaxes(v_ffn_w_gate, 1, 2), jnp.swapaxes(v_ffn_w_up, 1, 2)

    mixer_in = (conv_w_in, attn_w_qkv)
    mixer_out = (conv_w_out, attn_w_out)
    wts = []
    for i in range(DEPTH):
        j = i // 2
        w_in = all_gather_weights([mixer_in[i % 2][j].astype(BF)], (1,), f"ag_in_l{i}")
        w_out = all_gather_weights([mixer_out[i % 2][j].astype(BF)], (0,), f"ag_out_l{i}")
        w_gu = all_gather_weights([_pad_rows(gate_t[i].astype(BF), FF_SHARD_PAD),
                                   _pad_rows(up_t[i].astype(BF), FF_SHARD_PAD)], (0, 0), f"ag_up_l{i}")
        w_d = all_gather_weights([_pad_rows(ffn_w_down[i].astype(BF), FF_SHARD_PAD)], (0,),
                                 f"ag_down_l{i}")
        wts.append(list(w_in) + list(w_out) + list(w_gu) + list(w_d))

    onehot_t, band = _bucket_onehot_t()
    bias3 = bias_tables(rel_bias.T, onehot_t, band, "bias_tables").reshape(3, H * BLK, 2 * BLK)

    saved = []
    xc = x[0]
    for i in range(DEPTH):
        w_in, w_out, w_g, w_u, w_d = wts[i]
        j = i // 2
        x_mix = xc
        z3, h_mix = norm_matmul3(xc, mix_norm[i:i + 1], w_in, f"mix_in_l{i}")
        if i % 2 == 0:
            act = conv_fwd(z3, taps[j], f"conv_fwd_l{i}")
            lse_b = None
        else:
            act, lse_b = attention_fwd(z3, bias3, f"attn_fwd_l{i}")
        xc = matmul_residual(act, w_out, xc, f"mix_out_l{i}")
        x_ffn = xc
        g, u, a, h_ffn = norm_swiglu_up(xc, ffn_norm[i:i + 1], w_g, w_u, f"ffn_up_l{i}")
        xc = matmul_residual(a, w_d, xc, f"ffn_down_l{i}")
        saved.append((x_mix, h_mix, z3, act, lse_b, x_ffn, h_ffn, g, u, a))

    dx, dxb, dg_final, sq = loss_head(xc, final_norm.reshape(1, D), loss_target[0], "loss_head")
    loss = lax.psum(0.5 * jnp.sum(sq[0]) / D, ("x", "y", "c"))

    dg_mix = [None] * DEPTH
    dg_ffn = [None] * DEPTH
    dtaps = [None, None]
    dbias_all = []
    shape_in, shape_out = (D, 3 * D // NDEV), (D // NDEV, D)
    ffn_axes, ffn_shapes = (0, 0, 0), ((FF_SHARD_PAD, D),) * 3
    stacked = {}

    def pair_stage(grads, landed1, axes, shapes, tag, tok):
        parts = pair_add(grads, landed1, axes, shapes, c_idx, f"rs_add_{tag}", deps=[tok])
        return parts, chip_exchange_grads(parts, f"rs_chip_{tag}"), parts[-1]

    def adamw_stage(parts, landed2, params, tag, tok):
        names = [p[0] for p in params]
        res = reduce_adamw(parts, landed2, [p[1:] for p in params], q_idx, f"adamw_{tag}",
                           [stacked.get(nm, ()) for nm in names], deps=[tok])
        for nm, r4 in zip(names, res):
            stacked[nm] = r4
        return res[-1][0]

    tok = dxb
    mix_wait = None
    mix_chip = None
    ffn_chip = None
    for i in reversed(range(DEPTH)):
        w_in, w_out, w_g, w_u, w_d = wts[i]
        j = i // 2
        x_mix, h_mix, z3, act, lse_b, x_ffn, h_ffn, g, u, a = saved[i]
        ffn_params = [("ffn_w_gate", gate_t, m_gate_t, v_gate_t, i),
                      ("ffn_w_up", up_t, m_up_t, v_up_t, i),
                      ("ffn_w_down", ffn_w_down, m_ffn_w_down, v_ffn_w_down, i)]
        if i % 2 == 0:
            mix_params = [("conv_w_in", conv_w_in, m_conv_w_in, v_conv_w_in, j),
                          ("conv_w_out", conv_w_out, m_conv_w_out, v_conv_w_out, j)]
        else:
            mix_params = [("attn_w_qkv", attn_w_qkv, m_attn_w_qkv, v_attn_w_qkv, j),
                          ("attn_w_out", attn_w_out, m_attn_w_out, v_attn_w_out, j)]
        dgate, dup = swiglu_bwd_da(dxb, w_d, g, u, f"ffn_da_l{i}", deps=[tok])
        tok = dgate
        if mix_wait is not None:
            grads_m, landed1_m, params_m, tag_m = mix_wait
            parts_m, landed2_m, tok = pair_stage(grads_m, landed1_m, (1, 0), (shape_in, shape_out),
                                                 tag_m, tok)
            mix_chip = (parts_m, landed2_m, params_m, tag_m)
            mix_wait = None
        grads_f = matmul_tn_group([(dgate, h_ffn), (dup, h_ffn), (a, dxb)], f"ffn_dw_l{i}",
                                  deps=[tok])
        landed1_f = pair_exchange_grads(grads_f, ffn_axes, ffn_shapes, f"rs_pair_f{i}")
        tok = grads_f[-1]
        if ffn_chip is not None:
            tok = adamw_stage(*ffn_chip, tok)
            ffn_chip = None
        dx, dxb, dg_ffn[i] = matmul_normbwd(
            [(dgate, w_g, False), (dup, w_u, False)], x_ffn, ffn_norm[i:i + 1], dx, f"ffn_dh_l{i}",
            deps=[tok])
        dxb_mix = dxb
        dact = matmul_nt(dxb, w_out, f"mix_dact_l{i}", out_dtype=F32 if i % 2 == 0 else BF)
        parts_f, landed2_f, tok = pair_stage(grads_f, landed1_f, ffn_axes, ffn_shapes, f"f{i}", dact)
        ffn_chip = (parts_f, landed2_f, ffn_params, f"f{i}")
        if i % 2 == 0:
            dz3, dtaps[j] = conv_bwd(dact, z3, taps[j], f"conv_bwd_l{i}", deps=[tok])
        else:
            dz3, dbias3 = attention_bwd(z3, dact, act, lse_b, bias3, f"attn_bwd_l{i}", deps=[tok])
            dbias_all.append(dbias3.reshape(3, H, BLK * 2 * BLK))
        grads_m = matmul_tn_group([(h_mix, dz3), (act, dxb_mix)], f"mix_dw_l{i}")
        landed1_m = pair_exchange_grads(grads_m, (1, 0), (shape_in, shape_out), f"rs_pair_m{i}")
        mix_wait = (grads_m, landed1_m, mix_params, f"m{i}")
        tok = grads_m[-1]
        if mix_chip is not None:
            tok = adamw_stage(*mix_chip, tok)
            mix_chip = None
        dx, dxb, dg_mix[i] = matmul_normbwd(
            [(dz3, w_in, True)], x_mix, mix_norm[i:i + 1], dx, f"mix_dh_l{i}", deps=[tok])
        tok = dxb
    grads_m, landed1_m, params_m, tag_m = mix_wait
    parts_m, landed2_m, tok = pair_stage(grads_m, landed1_m, (1, 0), (shape_in, shape_out), tag_m, tok)
    tok = adamw_stage(*ffn_chip, tok)

    grad_relb_t = bias_grad(jnp.concatenate(dbias_all), onehot_t, "bias_grad")
    dtaps_full = jnp.stack([dtaps[0][:3], dtaps[1][:3]])
    g_small = _pack_small(jnp.concatenate([d[0:1] for d in dg_mix], axis=0),
                          jnp.concatenate([d[0:1] for d in dg_ffn], axis=0),
                          dg_final[0], dtaps_full, grad_relb_t.T)
    gathered = all_gather_small(g_small, "ag_small_grads")

    def taps_at_cols(k):
        return lax.dynamic_update_slice(jnp.zeros((2, 3, D), F32), k, (0, 0, col0))

    w_small = _pack_small(mix_norm, ffn_norm, final_norm, taps_at_cols(conv_kernel), rel_bias)
    m_small = _pack_small(m_mix_norm, m_ffn_norm, m_final_norm, taps_at_cols(m_conv_kernel), m_rel_bias)
    v_small = _pack_small(v_mix_norm, v_ffn_norm, v_final_norm, taps_at_cols(v_conv_kernel), v_rel_bias)
    small = small_reduce_adamw(gathered, w_small, m_small, v_small, "adamw_small")

    def unpack_small(p):
        taps_p = lax.dynamic_slice(p[9:15].reshape(2, 3, D), (0, 0, col0), (2, 3, D // NDEV))
        return {"mix_norm": p[0:4], "ffn_norm": p[4:8], "final_norm": p[8],
                "conv_kernel": taps_p, "rel_bias": p[15, :NUM_BUCKETS * H].reshape(NUM_BUCKETS, H)}

    small_out = [unpack_small(p) for p in small]
    adamw_stage(parts_m, landed2_m, params_m, tag_m, small[0])

    names = ["mix_norm", "ffn_norm", "final_norm", "conv_w_in", "conv_kernel", "conv_w_out",
             "attn_w_qkv", "attn_w_out", "rel_bias", "ffn_w_gate", "ffn_w_up", "ffn_w_down"]
    outs = [loss, dx.reshape(1, S, D)]
    for o in range(4):
        for nme in names:
            if nme in ("ffn_w_gate", "ffn_w_up"):
                outs.append(jnp.swapaxes(stacked[nme][o], 1, 2))
            else:
                outs.append(stacked[nme][o] if nme in stacked else small_out[o][nme])
    return tuple(outs)
```

```python
import math

import numpy as np
import jax
import jax.numpy as jnp
from jax import lax
from jax.experimental import pallas as pl
from jax.experimental.pallas import tpu as pltpu
from jax.experimental.pallas import tpu_sc as plsc

S = 2048
D = 1024
H = 16
DH = 64
DFF = 2816
NDEV = 8
DEPTH = 4
FF_SHARD = DFF // NDEV
FF_SHARD_PAD = 384
DFF_PAD = FF_SHARD_PAD * NDEV
BLK = 128
BRANCH_DILATIONS = (1, 4, 16)
NUM_BUCKETS = 32
MAX_DISTANCE = 2048
EPS = 1e-6
NEG_INF = -1e30
SCALE = DH ** -0.5

ADAM_LR = 0.001
ADAM_B1 = 0.9
ADAM_B2 = 0.999
ADAM_EPS = 1e-08
ADAM_WD = 0.01
ADAM_STEP = 10

BF = jnp.bfloat16
F32 = jnp.float32
VMEM_LIMIT_BYTES = 56 * 1024 * 1024
KSPLIT = 512
MESH = pl.DeviceIdType.MESH
ANY = pl.BlockSpec(memory_space=pl.ANY)

_NT = (((1,), (1,)), ((), ()))
_TN = (((0,), (0,)), ((), ()))


def _cparams(sem=None):
    return pltpu.CompilerParams(dimension_semantics=sem, vmem_limit_bytes=VMEM_LIMIT_BYTES)


def _after(body, n, deps):
    nd = len(deps)
    if nd == 0:
        return body

    def ordered(*refs):
        body(*refs[:n], *refs[n + nd:])
    return ordered


def _rms(x):
    return lax.rsqrt(jnp.mean(x * x, axis=-1, keepdims=True) + EPS)


def norm_matmul3(x, gain, w, name, tm=1024, tn=1024):
    per = D // tn

    def body(x_ref, g_ref, w_ref, z_ref, h_ref, hs_ref):
        @pl.when(pl.program_id(1) == 0)
        def _():
            xv = x_ref[...]
            hv = (xv * _rms(xv) * g_ref[...]).astype(BF)
            hs_ref[...] = hv
            h_ref[...] = hv
        z_ref[...] = jnp.dot(hs_ref[...], w_ref[...], preferred_element_type=F32).astype(BF)

    return pl.pallas_call(
        body, name=name,
        grid=(S // tm, 3 * D // tn),
        in_specs=[pl.BlockSpec((tm, D), lambda i, j: (i, 0)),
                  pl.BlockSpec((1, D), lambda i, j: (0, 0)),
                  pl.BlockSpec((D, tn), lambda i, j: (0, j))],
        out_specs=[pl.BlockSpec((None, tm, tn), lambda i, j: (j // per, i, j % per)),
                   pl.BlockSpec((tm, D), lambda i, j: (i, 0))],
        out_shape=[jax.ShapeDtypeStruct((3, S, D), BF), jax.ShapeDtypeStruct((S, D), BF)],
        scratch_shapes=[pltpu.VMEM((tm, D), BF)],
        compiler_params=_cparams(("parallel", "arbitrary")),
    )(x, gain, w)


def norm_swiglu_up(x, gain, wg_t, wu_t, name, tm=1024, tn=768):
    def body(x_ref, g_ref, wg_ref, wu_ref, go_ref, uo_ref, ao_ref, h_ref, hs_ref):
        @pl.when(pl.program_id(1) == 0)
        def _():
            xv = x_ref[...]
            hv = (xv * _rms(xv) * g_ref[...]).astype(BF)
            hs_ref[...] = hv
            h_ref[...] = hv
        hv = hs_ref[...]
        g = lax.dot_general(hv, wg_ref[...], _NT, preferred_element_type=F32)
        u = lax.dot_general(hv, wu_ref[...], _NT, preferred_element_type=F32)
        go_ref[...] = g.astype(BF)
        uo_ref[...] = u.astype(BF)
        ao_ref[...] = (g * jax.nn.sigmoid(g) * u).astype(BF)

    act = jax.ShapeDtypeStruct((S, DFF_PAD), BF)
    blk = pl.BlockSpec((tm, tn), lambda i, j: (i, j))
    return pl.pallas_call(
        body, name=name,
        grid=(S // tm, DFF_PAD // tn),
        in_specs=[pl.BlockSpec((tm, D), lambda i, j: (i, 0)),
                  pl.BlockSpec((1, D), lambda i, j: (0, 0)),
                  pl.BlockSpec((tn, D), lambda i, j: (j, 0)),
                  pl.BlockSpec((tn, D), lambda i, j: (j, 0))],
        out_specs=[blk, blk, blk, pl.BlockSpec((tm, D), lambda i, j: (i, 0))],
        out_shape=[act, act, act, jax.ShapeDtypeStruct((S, D), BF)],
        scratch_shapes=[pltpu.VMEM((tm, D), BF)],
        compiler_params=_cparams(("parallel", "arbitrary")),
    )(x, gain, wg_t, wu_t)


def matmul_residual(a, w, x, name, tm=1024):
    K = a.shape[1]
    tn = D if K <= D else D // 2
    ns = K // KSPLIT
    kc = K // ns

    def body(*refs):
        x_ref, o_ref = refs[2 * ns:]
        acc = x_ref[...]
        for s in range(ns):
            acc = acc + jnp.dot(refs[s][...], refs[ns + s][...], preferred_element_type=F32)
        o_ref[...] = acc

    return pl.pallas_call(
        body, name=name,
        grid=(S // tm, D // tn),
        in_specs=[pl.BlockSpec((tm, kc), lambda i, j, s=s: (i, s)) for s in range(ns)]
        + [pl.BlockSpec((kc, tn), lambda i, j, s=s: (s, j)) for s in range(ns)]
        + [pl.BlockSpec((tm, tn), lambda i, j: (i, j))],
        out_specs=pl.BlockSpec((tm, tn), lambda i, j: (i, j)),
        out_shape=jax.ShapeDtypeStruct((S, D), F32),
        compiler_params=_cparams(("parallel", "parallel")),
    )(*([a] * ns), *([w] * ns), x)


def matmul_nt(a, w, name, out_dtype=BF, tm=1024, tn=1024, deps=()):
    K = a.shape[1]
    N = w.shape[0]

    def body(a_ref, w_ref, o_ref):
        o_ref[...] = lax.dot_general(a_ref[...], w_ref[...], _NT,
                                     preferred_element_type=F32).astype(o_ref.dtype)

    return pl.pallas_call(
        _after(body, 2, deps), name=name,
        grid=(S // tm, N // tn),
        in_specs=[pl.BlockSpec((tm, K), lambda i, j: (i, 0)),
                  pl.BlockSpec((tn, K), lambda i, j: (j, 0))] + [ANY] * len(deps),
        out_specs=pl.BlockSpec((tm, tn), lambda i, j: (i, j)),
        out_shape=jax.ShapeDtypeStruct((S, N), out_dtype),
        compiler_params=_cparams(("parallel", "parallel")),
    )(a, w, *deps)


def swiglu_bwd_da(dxb, wd, g, u, name, tm=1024, tn=768, deps=()):
    def body(dx_ref, w_ref, g_ref, u_ref, dg_ref, du_ref):
        da = lax.dot_general(dx_ref[...], w_ref[...], _NT, preferred_element_type=F32)
        gv = g_ref[...].astype(F32)
        uv = u_ref[...].astype(F32)
        sig = jax.nn.sigmoid(gv)
        dg_ref[...] = (da * uv * (sig * (1.0 + gv * (1.0 - sig)))).astype(BF)
        du_ref[...] = (da * (gv * sig)).astype(BF)

    act = jax.ShapeDtypeStruct((S, DFF_PAD), BF)
    blk = pl.BlockSpec((tm, tn), lambda i, j: (i, j))
    return pl.pallas_call(
        _after(body, 4, deps), name=name,
        grid=(S // tm, DFF_PAD // tn),
        in_specs=[pl.BlockSpec((tm, D), lambda i, j: (i, 0)),
                  pl.BlockSpec((tn, D), lambda i, j: (j, 0)),
                  blk, blk] + [ANY] * len(deps),
        out_specs=[blk, blk],
        out_shape=[act, act],
        compiler_params=_cparams(("parallel", "parallel")),
    )(dxb, wd, g, u, *deps)


def matmul_tn_group(pairs, name, tm=1024, tn=512, deps=()):
    P = len(pairs)
    steps = []
    for p, (a, b) in enumerate(pairs):
        N = 3 * D if b.ndim == 3 else b.shape[1]
        steps += [(p, i, j) for i in range(a.shape[1] // tm) for j in range(N // tn)]
    T = len(steps)
    tab = np.zeros((T, 1 + 2 * P), np.int32)
    for p in range(P):
        cur = (0, 0)
        for s, (ph, i, j) in enumerate(steps):
            if ph == p:
                cur = (i, j)
            tab[s, 1 + 2 * p:3 + 2 * p] = cur
    tab[:, 0] = [ph for ph, _, _ in steps]

    in_specs, out_specs, out_shapes, operands = [], [], [], []
    per = D // tn
    for p, (a, b) in enumerate(pairs):
        ci, cj = 1 + 2 * p, 2 + 2 * p
        in_specs.append(pl.BlockSpec((S, tm), lambda s, t, ci=ci: (0, t[s, ci])))
        if b.ndim == 3:
            in_specs.append(pl.BlockSpec((None, S, tn),
                                         lambda s, t, cj=cj: (t[s, cj] // per, 0, t[s, cj] % per)))
            N = 3 * D
        else:
            in_specs.append(pl.BlockSpec((S, tn), lambda s, t, cj=cj: (0, t[s, cj])))
            N = b.shape[1]
        out_specs.append(pl.BlockSpec((tm, tn), lambda s, t, ci=ci, cj=cj: (t[s, ci], t[s, cj])))
        out_shapes.append(jax.ShapeDtypeStruct((a.shape[1], N), BF))
        operands += [a, b]

    def body(tab_ref, *refs):
        phase = tab_ref[pl.program_id(0), 0]
        for p in range(P):
            @pl.when(phase == p)
            def _(p=p):
                refs[2 * P + p][...] = lax.dot_general(
                    refs[2 * p][...], refs[2 * p + 1][...], _TN,
                    preferred_element_type=F32).astype(BF)

    return pl.pallas_call(
        _after(body, 1 + 2 * P, deps), name=name,
        grid_spec=pltpu.PrefetchScalarGridSpec(
            num_scalar_prefetch=1, grid=(T,), in_specs=in_specs + [ANY] * len(deps),
            out_specs=out_specs),
        out_shape=out_shapes,
        compiler_params=_cparams(("arbitrary",)),
    )(jnp.asarray(tab), *operands, *deps)


def matmul_normbwd(terms, x_in, gain, dx, name, tm=512, ch=256, deps=()):
    specs, operands = [], []
    for (a, w, stacked) in terms:
        if stacked:
            specs.append(pl.BlockSpec((3, tm, D), lambda i: (0, i, 0)))
        else:
            specs.append(pl.BlockSpec((tm, a.shape[1]), lambda i: (i, 0)))
        specs.append(pl.BlockSpec(w.shape, lambda i: (0, 0), pipeline_mode=pl.Buffered(1)))
        operands += [a, w]
    nt = len(terms)

    def body(*refs):
        aw = refs[:2 * nt]
        x_ref, g_ref, dx_ref, dxo_ref, dxb_ref, dg_ref, acc_ref = refs[2 * nt:]

        @pl.when(pl.program_id(0) == 0)
        def _():
            dg_ref[...] = jnp.zeros_like(dg_ref)

        dh = None
        for t, (_, _, stacked) in enumerate(terms):
            a_ref, w_ref = aw[2 * t], aw[2 * t + 1]
            if stacked:
                parts = [lax.dot_general(a_ref[k], w_ref[:, k * D:(k + 1) * D], _NT,
                                         preferred_element_type=F32) for k in range(3)]
            else:
                parts = [jnp.dot(a_ref[...], w_ref[...], preferred_element_type=F32)]
            for p in parts:
                dh = p if dh is None else dh + p
        acc_ref[...] = dh

        def chunk(c, carry):
            rows = pl.ds(pl.multiple_of(c * ch, ch), ch)
            xv = x_ref[rows, :]
            r = _rms(xv)
            xhat = xv * r
            dhc = acc_ref[rows, :]
            dg_ref[0:1, :] += jnp.sum(dhc * xhat, axis=0, keepdims=True)
            dxh = dhc * g_ref[...]
            dxn = r * (dxh - xhat * jnp.mean(dxh * xhat, axis=-1, keepdims=True))
            out = dx_ref[rows, :] + dxn
            dxo_ref[rows, :] = out
            dxb_ref[rows, :] = out.astype(BF)
            return carry
        lax.fori_loop(0, tm // ch, chunk, 0)

    row = pl.BlockSpec((tm, D), lambda i: (i, 0))
    return pl.pallas_call(
        _after(body, 2 * nt + 3, deps), name=name,
        grid=(S // tm,),
        in_specs=specs + [row, pl.BlockSpec((1, D), lambda i: (0, 0)), row] + [ANY] * len(deps),
        out_specs=[row, row, pl.BlockSpec((8, D), lambda i: (0, 0))],
        out_shape=[jax.ShapeDtypeStruct((S, D), F32), jax.ShapeDtypeStruct((S, D), BF),
                   jax.ShapeDtypeStruct((8, D), F32)],
        scratch_shapes=[pltpu.VMEM((tm, D), F32)],
        compiler_params=_cparams(("arbitrary",)),
    )(*operands, x_in, gain, dx, *deps)


def loss_head(x, gain, target, name, tm=512):
    def body(x_ref, g_ref, t_ref, dxo_ref, dxb_ref, dg_ref, sq_ref):
        @pl.when(pl.program_id(0) == 0)
        def _():
            dg_ref[...] = jnp.zeros_like(dg_ref)
            sq_ref[...] = jnp.zeros_like(sq_ref)
        xv = x_ref[...]
        r = _rms(xv)
        xhat = xv * r
        err = xhat * g_ref[...] - t_ref[...]
        sq_ref[0:1, :] += jnp.sum(err * err, axis=0, keepdims=True)
        dy = err * (1.0 / D)
        dg_ref[0:1, :] += jnp.sum(dy * xhat, axis=0, keepdims=True)
        dxh = dy * g_ref[...]
        out = r * (dxh - xhat * jnp.mean(dxh * xhat, axis=-1, keepdims=True))
        dxo_ref[...] = out
        dxb_ref[...] = out.astype(BF)

    row = pl.BlockSpec((tm, D), lambda i: (i, 0))
    acc = pl.BlockSpec((8, D), lambda i: (0, 0))
    return pl.pallas_call(
        body, name=name,
        grid=(S // tm,),
        in_specs=[row, pl.BlockSpec((1, D), lambda i: (0, 0)), row],
        out_specs=[row, row, acc, acc],
        out_shape=[jax.ShapeDtypeStruct((S, D), F32), jax.ShapeDtypeStruct((S, D), BF),
                   jax.ShapeDtypeStruct((8, D), F32), jax.ShapeDtypeStruct((8, D), F32)],
        compiler_params=_cparams(("arbitrary",)),
    )(x, gain, target)


def _shift_down(p, n, row):
    return jnp.where(row >= n, pltpu.roll(p, n, axis=0), 0.0)


def _shift_up(p, n, row):
    return jnp.where(row < S - n, pltpu.roll(p, S - n, axis=0), 0.0)


def conv_fwd(z3, taps, name, tn=128):
    def body(z_ref, k_ref, m_ref):
        b = z_ref[0].astype(F32)
        p = z_ref[1].astype(F32) * z_ref[2].astype(F32)
        row = lax.broadcasted_iota(jnp.int32, p.shape, 0)
        y = (k_ref[2:3, :] * p + k_ref[1:2, :] * _shift_down(p, 1, row)
             + k_ref[0:1, :] * _shift_down(p, 2, row))
        m_ref[...] = (b * y).astype(BF)

    return pl.pallas_call(
        body, name=name,
        grid=(D // tn,),
        in_specs=[pl.BlockSpec((3, S, tn), lambda j: (0, 0, j)),
                  pl.BlockSpec((8, tn), lambda j: (0, j))],
        out_specs=pl.BlockSpec((S, tn), lambda j: (0, j)),
        out_shape=jax.ShapeDtypeStruct((S, D), BF),
        compiler_params=_cparams(("parallel",)),
    )(z3, taps)


def conv_bwd(dm, z3, taps, name, tn=128, deps=()):
    def body(dm_ref, z_ref, k_ref, dz_ref, dk_ref):
        dmv = dm_ref[...]
        b = z_ref[0].astype(F32)
        c = z_ref[1].astype(F32)
        u = z_ref[2].astype(F32)
        p = c * u
        row = lax.broadcasted_iota(jnp.int32, p.shape, 0)
        p1 = _shift_down(p, 1, row)
        p2 = _shift_down(p, 2, row)
        y = k_ref[2:3, :] * p + k_ref[1:2, :] * p1 + k_ref[0:1, :] * p2
        dy = dmv * b
        dz_ref[0] = (dmv * y).astype(BF)
        dp = (k_ref[2:3, :] * dy + k_ref[1:2, :] * _shift_up(dy, 1, row)
              + k_ref[0:1, :] * _shift_up(dy, 2, row))
        dz_ref[1] = (dp * u).astype(BF)
        dz_ref[2] = (dp * c).astype(BF)
        dk_ref[...] = jnp.zeros_like(dk_ref)
        dk_ref[0:1, :] = jnp.sum(dy * p2, axis=0, keepdims=True)
        dk_ref[1:2, :] = jnp.sum(dy * p1, axis=0, keepdims=True)
        dk_ref[2:3, :] = jnp.sum(dy * p, axis=0, keepdims=True)

    return pl.pallas_call(
        _after(body, 3, deps), name=name,
        grid=(D // tn,),
        in_specs=[pl.BlockSpec((S, tn), lambda j: (0, j)),
                  pl.BlockSpec((3, S, tn), lambda j: (0, 0, j)),
                  pl.BlockSpec((8, tn), lambda j: (0, j))] + [ANY] * len(deps),
        out_specs=[pl.BlockSpec((3, S, tn), lambda j: (0, 0, j)),
                   pl.BlockSpec((8, tn), lambda j: (0, j))],
        out_shape=[jax.ShapeDtypeStruct((3, S, D), BF), jax.ShapeDtypeStruct((8, D), F32)],
        compiler_params=_cparams(("parallel",)),
    )(dm, z3, taps, *deps)


def _t5_bucket(dist):
    exact = NUM_BUCKETS // 2
    df = jnp.maximum(dist, 1).astype(jnp.float32)
    large = exact + (jnp.log(df / exact) / math.log(MAX_DISTANCE / exact)
                     * (NUM_BUCKETS - exact)).astype(jnp.int32)
    large = jnp.minimum(large, NUM_BUCKETS - 1)
    return jnp.where(dist < exact, dist, large)


def _bucket_onehot_t():
    qi = jnp.arange(BLK)[:, None]
    ki = jnp.arange(2 * BLK)[None, :]
    rel = qi + BLK - ki
    band = ((rel >= 0) & (rel <= BLK)).reshape(1, -1).astype(F32)
    hots = []
    for d in BRANCH_DILATIONS:
        bucket = _t5_bucket(jnp.clip(rel, 0) * d).reshape(1, -1)
        hots.append((jnp.arange(NUM_BUCKETS)[:, None] == bucket).astype(F32))
    return jnp.stack(hots), band


def bias_tables(rel_bias_t, onehot_t, band, name):
    def body(rb_ref, oh_ref, band_ref, o_ref):
        b = jnp.dot(rb_ref[...], oh_ref[...], preferred_element_type=F32,
                    precision=lax.Precision.HIGHEST)
        o_ref[...] = jnp.where(band_ref[...] > 0.5, b, NEG_INF)

    n = BLK * 2 * BLK
    return pl.pallas_call(
        body, name=name,
        grid=(3,),
        in_specs=[pl.BlockSpec((H, NUM_BUCKETS), lambda g: (0, 0)),
                  pl.BlockSpec((None, NUM_BUCKETS, n), lambda g: (g, 0, 0)),
                  pl.BlockSpec((1, n), lambda g: (0, 0))],
        out_specs=pl.BlockSpec((None, H, n), lambda g: (g, 0, 0)),
        out_shape=jax.ShapeDtypeStruct((3, H, n), F32),
        compiler_params=_cparams(("parallel",)),
    )(rel_bias_t, onehot_t, band)


def bias_grad(dbias, onehot_t, name):
    def body(db_ref, oh_ref, o_ref):
        @pl.when(pl.program_id(0) == 0)
        def _():
            o_ref[...] = jnp.zeros_like(o_ref)
        o_ref[...] += lax.dot_general(db_ref[...], oh_ref[...], _NT, preferred_element_type=F32,
                                      precision=lax.Precision.HIGHEST)

    n = BLK * 2 * BLK
    return pl.pallas_call(
        body, name=name,
        grid=(dbias.shape[0],),
        in_specs=[pl.BlockSpec((None, H, n), lambda g: (g, 0, 0)),
                  pl.BlockSpec((None, NUM_BUCKETS, n), lambda g: (g % 3, 0, 0))],
        out_specs=pl.BlockSpec((H, NUM_BUCKETS), lambda g: (0, 0)),
        out_shape=jax.ShapeDtypeStruct((H, NUM_BUCKETS), F32),
        compiler_params=_cparams(("arbitrary",)),
    )(dbias, onehot_t)


def _head_masks():
    lane = lax.broadcasted_iota(jnp.int32, (1, 2 * DH), 1)
    return (lane < DH, lane >= DH)


def _stack_heads(x, masks):
    zero = jnp.zeros_like(x)
    return jnp.concatenate([jnp.where(masks[0], x, zero), jnp.where(masks[1], x, zero)], axis=0)


def _deinterleave(src_ref, dst_ref, d, dtype):
    L = S // d
    for r in range(d):
        dst_ref[r * L:(r + 1) * L, :] = src_ref[pl.ds(r, L, stride=d), :].astype(dtype)


def _branch_loops(d, block):
    L = S // d
    for r in range(d):
        base = r * L
        block(base, base, BLK, True)
        for n in range(1, L // BLK):
            block(base + n * BLK, base + (n - 1) * BLK, 2 * BLK, False)


def attention_fwd(z3, bias3, name):
    W = 2 * DH
    CH = 256

    def body(q_ref, k_ref, v_ref, b_ref, o_ref, lse_ref, stage, qd, kd, vd, od, ld, on, ln):
        masks = _head_masks()
        for src, dst in ((q_ref, qd), (k_ref, kd), (v_ref, vd)):
            stage[...] = src[...].astype(F32)
            for gi, d in enumerate(BRANCH_DILATIONS[1:]):
                _deinterleave(stage, dst.at[gi], d, BF)

        for g, d in enumerate(BRANCH_DILATIONS):
            qs, ks, vs = (q_ref, k_ref, v_ref) if d == 1 else (qd.at[g - 1], kd.at[g - 1], vd.at[g - 1])
            o_dst, l_dst = (on.at[0], ln.at[0]) if d == 1 else (od, ld)

            def block(q0, k0, nk, first, g=g, qs=qs, ks=ks, vs=vs, o_dst=o_dst, l_dst=l_dst):
                q2 = _stack_heads(qs[pl.ds(q0, BLK), :], masks)
                kk = ks[pl.ds(k0, nk), :]
                vv = vs[pl.ds(k0, nk), :]
                bias = b_ref[g][:, BLK:] if first else b_ref[g]
                s = lax.dot_general(q2, kk, _NT, preferred_element_type=F32) * SCALE + bias
                mx = jnp.max(s, axis=1, keepdims=True)
                p = jnp.exp(s - mx)
                l = jnp.sum(p, axis=1, keepdims=True)
                o2 = jnp.dot(p.astype(BF), vv, preferred_element_type=F32) / l
                lse2 = mx + jnp.log(l)
                o_dst[pl.ds(q0, BLK), :] = jnp.where(masks[0], o2[:BLK], o2[BLK:])
                l_dst[pl.ds(q0, BLK), :] = jnp.where(masks[0], lse2[:BLK], lse2[BLK:])

            _branch_loops(d, block)
            if d > 1:
                L = S // d
                for r in range(d):
                    on[g, pl.ds(r, L, stride=d), :] = od[r * L:(r + 1) * L, :]
                    ln[g, pl.ds(r, L, stride=d), :] = ld[r * L:(r + 1) * L, :]

        def join(c, carry):
            rows = pl.ds(pl.multiple_of(c * CH, CH), CH)
            a, b, cc = ln[0, rows, :], ln[1, rows, :], ln[2, rows, :]
            mx = jnp.maximum(jnp.maximum(a, b), cc)
            ea, eb, ec = jnp.exp(a - mx), jnp.exp(b - mx), jnp.exp(cc - mx)
            tot = ea + eb + ec
            o_ref[rows, :] = ((ea * on[0, rows, :] + eb * on[1, rows, :] + ec * on[2, rows, :])
                              / tot).astype(BF)
            lse_ref[rows, :] = mx + jnp.log(tot)
            return carry
        lax.fori_loop(0, S // CH, join, 0)

    col = pl.BlockSpec((S, W), lambda hp: (0, hp))
    return pl.pallas_call(
        body, name=name,
        grid=(D // W,),
        in_specs=[pl.BlockSpec((None, S, W), lambda hp: (0, 0, hp)),
                  pl.BlockSpec((None, S, W), lambda hp: (1, 0, hp)),
                  pl.BlockSpec((None, S, W), lambda hp: (2, 0, hp)),
                  pl.BlockSpec((3, 2 * BLK, 2 * BLK), lambda hp: (0, hp, 0))],
        out_specs=[col, col],
        out_shape=[jax.ShapeDtypeStruct((S, D), BF), jax.ShapeDtypeStruct((S, D), F32)],
        scratch_shapes=[pltpu.VMEM((S, W), F32),
                        pltpu.VMEM((2, S, W), BF), pltpu.VMEM((2, S, W), BF), pltpu.VMEM((2, S, W), BF),
                        pltpu.VMEM((S, W), F32), pltpu.VMEM((S, W), F32),
                        pltpu.VMEM((3, S, W), F32), pltpu.VMEM((3, S, W), F32)],
        compiler_params=_cparams(("parallel",)),
    )(z3, z3, z3, bias3)


def attention_bwd(z3, dob, ob, lse_b, bias3, name, deps=()):
    W = 2 * DH
    CH = 256

    def body(q_ref, k_ref, v_ref, do_ref, o_ref, lse_ref, b_ref, dz_ref, db_ref,
             stage, delta, qd, kd, vd, dod, lsd, dld, res, acc):
        masks = _head_masks()

        def rowsum(c, carry):
            rows = pl.ds(pl.multiple_of(c * CH, CH), CH)
            prod = do_ref[rows, :].astype(F32) * o_ref[rows, :].astype(F32)
            sa = jnp.sum(jnp.where(masks[0], prod, 0.0), axis=1, keepdims=True)
            sb = jnp.sum(jnp.where(masks[1], prod, 0.0), axis=1, keepdims=True)
            delta[rows, :] = jnp.where(masks[0], sa, sb)
            return carry
        lax.fori_loop(0, S // CH, rowsum, 0)

        for src, dst in ((q_ref, qd), (k_ref, kd), (v_ref, vd), (do_ref, dod)):
            stage[...] = src[...].astype(F32)
            for gi, d in enumerate(BRANCH_DILATIONS[1:]):
                _deinterleave(stage, dst.at[gi], d, BF)
        for gi, d in enumerate(BRANCH_DILATIONS[1:]):
            _deinterleave(lse_ref, lsd.at[gi], d, F32)
            _deinterleave(delta, dld.at[gi], d, F32)

        db_ref[...] = jnp.zeros_like(db_ref)
        for g, d in enumerate(BRANCH_DILATIONS):
            if d == 1:
                qs, ks, vs, dos, ls, dl = q_ref, k_ref, v_ref, do_ref, lse_ref, delta
            else:
                qs, ks, vs, dos = qd.at[g - 1], kd.at[g - 1], vd.at[g - 1], dod.at[g - 1]
                ls, dl = lsd.at[g - 1], dld.at[g - 1]
            res[1] = jnp.zeros((S, W), F32)
            res[2] = jnp.zeros((S, W), F32)

            def block(q0, k0, nk, first, g=g, qs=qs, ks=ks, vs=vs, dos=dos, ls=ls, dl=dl):
                kk = ks[pl.ds(k0, nk), :]
                vv = vs[pl.ds(k0, nk), :]
                q2 = _stack_heads(qs[pl.ds(q0, BLK), :], masks)
                do2 = _stack_heads(dos[pl.ds(q0, BLK), :], masks)
                lse_blk = ls[pl.ds(q0, BLK), :]
                del_blk = dl[pl.ds(q0, BLK), :]
                lse2 = jnp.concatenate([lse_blk[:, 0:1], lse_blk[:, DH:DH + 1]], axis=0)
                del2 = jnp.concatenate([del_blk[:, 0:1], del_blk[:, DH:DH + 1]], axis=0)
                bias = b_ref[g][:, BLK:] if first else b_ref[g]
                s = lax.dot_general(q2, kk, _NT, preferred_element_type=F32) * SCALE + bias
                p = jnp.exp(s - lse2)
                dp = lax.dot_general(do2, vv, _NT, preferred_element_type=F32)
                ds = p * (dp - del2)
                if first:
                    db_ref[g, :, BLK:] += ds
                else:
                    db_ref[g] += ds
                dsb = ds.astype(BF)
                dq2 = jnp.dot(dsb, kk, preferred_element_type=F32) * SCALE
                res[0, pl.ds(q0, BLK), :] = jnp.where(masks[0], dq2[:BLK], dq2[BLK:])
                res[1, pl.ds(k0, nk), :] += lax.dot_general(dsb, q2, _TN,
                                                            preferred_element_type=F32) * SCALE
                res[2, pl.ds(k0, nk), :] += lax.dot_general(p.astype(BF), do2, _TN,
                                                            preferred_element_type=F32)

            _branch_loops(d, block)
            L = S // d
            for t in range(3):
                if d == 1:
                    acc[t] = res[t]
                else:
                    for r in range(d):
                        acc[t, pl.ds(r, L, stride=d), :] = (acc[t, pl.ds(r, L, stride=d), :]
                                                            + res[t, r * L:(r + 1) * L, :])
        for t in range(3):
            dz_ref[t] = acc[t].astype(BF)

    col = pl.BlockSpec((S, W), lambda hp: (0, hp))
    bspec = pl.BlockSpec((3, 2 * BLK, 2 * BLK), lambda hp: (0, hp, 0))
    return pl.pallas_call(
        _after(body, 7, deps), name=name,
        grid=(D // W,),
        in_specs=[pl.BlockSpec((None, S, W), lambda hp: (0, 0, hp)),
                  pl.BlockSpec((None, S, W), lambda hp: (1, 0, hp)),
                  pl.BlockSpec((None, S, W), lambda hp: (2, 0, hp)),
                  col, col, col, bspec] + [ANY] * len(deps),
        out_specs=[pl.BlockSpec((3, S, W), lambda hp: (0, 0, hp)), bspec],
        out_shape=[jax.ShapeDtypeStruct((3, S, D), BF),
                   jax.ShapeDtypeStruct((3, H * BLK, 2 * BLK), F32)],
        scratch_shapes=[pltpu.VMEM((S, W), F32), pltpu.VMEM((S, W), F32),
                        pltpu.VMEM((2, S, W), BF), pltpu.VMEM((2, S, W), BF),
                        pltpu.VMEM((2, S, W), BF), pltpu.VMEM((2, S, W), BF),
                        pltpu.VMEM((2, S, W), F32), pltpu.VMEM((2, S, W), F32),
                        pltpu.VMEM((3, S, W), F32), pltpu.VMEM((3, S, W), F32)],
        compiler_params=_cparams(("parallel",)),
    )(z3, z3, z3, dob, ob, lse_b, bias3, *deps)


def _me():
    return lax.axis_index("x"), lax.axis_index("y"), lax.axis_index("c")


def _other_chips(x, y):
    return [(1 - x, y), (x, 1 - y), (1 - x, 1 - y)]


def _shard_window(ref, axis, t, shape):
    R, C = shape
    if axis == 0:
        return ref.at[pl.ds(pl.multiple_of(t * R, 128), R), :]
    return ref.at[:, pl.ds(pl.multiple_of(t * C, 128), C)]


def all_gather_weights(shards, axes, name):
    n = len(shards)
    shapes = [s.shape for s in shards]
    outs_shape = [jax.ShapeDtypeStruct((8 * R, C) if ax == 0 else (R, 8 * C), BF)
                  for (R, C), ax in zip(shapes, axes)]

    def body(*refs):
        ins, outs = refs[:n], refs[n:2 * n]
        send_sems, recv_sems, local_sems = refs[2 * n:]
        x, y, c = _me()
        me, sibling = (x, y, c), (x, y, 1 - c)
        xnb, ynb, diag = (1 - x, y), (x, 1 - y), (1 - x, 1 - y)
        south = c == 0
        relay_from = (jnp.where(south, x, 1 - x), jnp.where(south, 1 - y, y))
        relay_to = (jnp.where(south, 1 - x, x), jnp.where(south, y, 1 - y))
        barrier = pltpu.get_barrier_semaphore()
        for peer in [sibling, (*xnb, c), (*ynb, c)]:
            pl.semaphore_signal(barrier, inc=1, device_id=peer, device_id_type=MESH)
        pl.semaphore_wait(barrier, 3)

        def win(i, px, py, pc):
            return _shard_window(outs[i], axes[i], 4 * px + 2 * py + pc, shapes[i])

        def copy(i, k, block, to, src=None):
            return pltpu.make_async_remote_copy(
                src_ref=win(i, *block) if src is None else src, dst_ref=win(i, *block),
                send_sem=send_sems.at[i * 7 + k], recv_sem=recv_sems.at[i * 7 + k],
                device_id=to, device_id_type=MESH)

        mine = [pltpu.make_async_copy(ins[i], win(i, *me), local_sems.at[i]) for i in range(n)]
        for cp in mine:
            cp.start()
        sent = []
        for i in range(n):
            sent += [copy(i, 0, me, sibling, src=ins[i]), copy(i, 1, me, (*xnb, c), src=ins[i]),
                     copy(i, 2, me, (*ynb, c), src=ins[i])]
        for cp in sent:
            cp.start()
        for i in range(n):
            for k, chip in ((1, xnb), (2, ynb)):
                copy(i, k, (*chip, c), me).wait_recv()
                sent.append(copy(i, 3 + k, (*chip, c), sibling))
                sent[-1].start()
            sent.append(copy(i, 3, (*relay_from, c), (*relay_to, c)))
            sent[-1].start()
        for i in range(n):
            copy(i, 3, (*diag, c), me).wait_recv()
            sent.append(copy(i, 6, (*diag, c), sibling))
            sent[-1].start()
        for i in range(n):
            copy(i, 0, sibling, me).wait_recv()
            for k, chip in ((4, xnb), (5, ynb), (6, diag)):
                copy(i, k, (*chip, 1 - c), me).wait_recv()
        for cp in sent:
            cp.wait_send()
        for cp in mine:
            cp.wait()

    return pl.kernel(
        body, out_type=outs_shape, name=name,
        mesh=plsc.ScalarSubcoreMesh(axis_name="sequencer", num_cores=1),
        scratch_types=[pltpu.SemaphoreType.DMA((7 * n,)), pltpu.SemaphoreType.DMA((7 * n,)),
                       pltpu.SemaphoreType.DMA((n,))],
        compiler_params=pltpu.CompilerParams(collective_id=1),
    )(*shards)


def pair_exchange_grads(grads, axes, shapes, name):
    n = len(grads)

    def body(*refs):
        ins, outs = refs[:n], refs[n:2 * n]
        send_sems, recv_sems = refs[2 * n:]
        x, y, c = _me()
        sibling = (x, y, 1 - c)
        barrier = pltpu.get_barrier_semaphore()
        pl.semaphore_signal(barrier, inc=1, device_id=sibling, device_id_type=MESH)
        pl.semaphore_wait(barrier, 1)
        copies = []
        for i in range(n):
            for q in range(4):
                t = 2 * q + (1 - c)
                copies.append(pltpu.make_async_remote_copy(
                    src_ref=_shard_window(ins[i], axes[i], t, shapes[i]), dst_ref=outs[i].at[q],
                    send_sem=send_sems.at[i * 4 + q], recv_sem=recv_sems.at[i * 4 + q],
                    device_id=sibling, device_id_type=MESH))
        for cp in copies:
            cp.start()
        for cp in copies:
            cp.wait_recv()
        for cp in copies:
            cp.wait_send()

    return pl.kernel(
        body, out_type=[jax.ShapeDtypeStruct((4,) + tuple(sh), BF) for sh in shapes], name=name,
        mesh=plsc.ScalarSubcoreMesh(axis_name="sequencer", num_cores=1),
        scratch_types=[pltpu.SemaphoreType.DMA((4 * n,)), pltpu.SemaphoreType.DMA((4 * n,))],
        compiler_params=pltpu.CompilerParams(collective_id=2),
    )(*grads)


def pair_add(grads, landed, axes, shapes, c_idx, name, deps=()):
    n = len(grads)

    def body(c_ref, *refs):
        for t in range(n):
            refs[2 * n + t][...] = (refs[2 * t][...].astype(F32)
                                    + refs[2 * t + 1][...].astype(F32)).astype(BF)

    halves = 1
    in_specs, out_specs, out_shapes, operands = [], [], [], []
    for t in range(n):
        R, C = shapes[t]
        rh = R // halves
        if axes[t] == 0:
            in_specs.append(pl.BlockSpec(
                (rh, C), lambda q, h, c_ref: (halves * (2 * q + c_ref[0]) + h, 0)))
        else:
            in_specs.append(pl.BlockSpec((rh, C), lambda q, h, c_ref: (h, 2 * q + c_ref[0])))
        blk = pl.BlockSpec((None, rh, C), lambda q, h, c_ref: (q, h, 0))
        in_specs.append(blk)
        out_specs.append(blk)
        out_shapes.append(jax.ShapeDtypeStruct((4, R, C), BF))
        operands += [grads[t], landed[t]]
    return pl.pallas_call(
        _after(body, 1 + 2 * n, deps), name=name,
        grid_spec=pltpu.PrefetchScalarGridSpec(
            num_scalar_prefetch=1, grid=(4, halves), in_specs=in_specs + [ANY] * len(deps),
            out_specs=out_specs),
        out_shape=out_shapes,
        compiler_params=_cparams(("parallel", "parallel")),
    )(c_idx, *operands, *deps)


def chip_exchange_grads(parts, name):
    n = len(parts)

    def body(*refs):
        ins, outs = refs[:n], refs[n:2 * n]
        send_sems, recv_sems = refs[2 * n:]
        x, y, c = _me()
        barrier = pltpu.get_barrier_semaphore()
        for px, py in _other_chips(x, y):
            pl.semaphore_signal(barrier, inc=1, device_id=(px, py, c), device_id_type=MESH)
        pl.semaphore_wait(barrier, 3)
        copies = []
        for i in range(n):
            for k, (px, py) in enumerate(_other_chips(x, y)):
                copies.append(pltpu.make_async_remote_copy(
                    src_ref=ins[i].at[2 * px + py], dst_ref=outs[i].at[k],
                    send_sem=send_sems.at[i * 3 + k], recv_sem=recv_sems.at[i * 3 + k],
                    device_id=(px, py, c), device_id_type=MESH))
        for cp in copies:
            cp.start()
        for cp in copies:
            cp.wait_recv()
        for cp in copies:
            cp.wait_send()

    return pl.kernel(
        body, out_type=[jax.ShapeDtypeStruct((3,) + tuple(p.shape[1:]), BF) for p in parts], name=name,
        mesh=plsc.ScalarSubcoreMesh(axis_name="sequencer", num_cores=1),
        scratch_types=[pltpu.SemaphoreType.DMA((3 * n,)), pltpu.SemaphoreType.DMA((3 * n,))],
        compiler_params=pltpu.CompilerParams(collective_id=3),
    )(*parts)


def all_gather_small(v, name):
    R, C = v.shape

    def body(v_ref, out_ref, send_sems, recv_sems, local_sem):
        x, y, c = _me()
        me, sibling = (x, y, c), (x, y, 1 - c)
        chips = _other_chips(x, y)

        def slot(px, py, pc):
            return out_ref.at[4 * px + 2 * py + pc]

        def copy(k, block, to, src=None):
            return pltpu.make_async_remote_copy(
                src_ref=slot(*block) if src is None else src, dst_ref=slot(*block),
                send_sem=send_sems.at[k], recv_sem=recv_sems.at[k],
                device_id=to, device_id_type=MESH)

        mine = pltpu.make_async_copy(v_ref, slot(*me), local_sem)
        mine.start()
        first = [copy(0, me, sibling, src=v_ref)]
        first += [copy(1 + j, me, (*chip, c), src=v_ref) for j, chip in enumerate(chips)]
        for cp in first:
            cp.start()
        passed = [copy(4 + j, (*chip, c), sibling) for j, chip in enumerate(chips)]
        for j, chip in enumerate(chips):
            copy(1 + j, (*chip, c), me).wait_recv()
            passed[j].start()
        copy(0, sibling, me).wait_recv()
        for j, chip in enumerate(chips):
            copy(4 + j, (*chip, 1 - c), me).wait_recv()
        for cp in first + passed:
            cp.wait_send()
        mine.wait()

    return pl.pallas_call(
        body, name=name,
        in_specs=[pl.BlockSpec(memory_space=pltpu.VMEM)],
        out_specs=pl.BlockSpec(memory_space=pltpu.VMEM),
        out_shape=jax.ShapeDtypeStruct((NDEV, R, C), F32),
        scratch_shapes=[pltpu.SemaphoreType.DMA((7,)), pltpu.SemaphoreType.DMA((7,)),
                        pltpu.SemaphoreType.DMA],
    )(v)


def _adamw(w, g, m, v):
    m = ADAM_B1 * m + (1.0 - ADAM_B1) * g
    v = ADAM_B2 * v + (1.0 - ADAM_B2) * (g * g)
    m_hat = m / (1.0 - ADAM_B1 ** ADAM_STEP)
    v_hat = v / (1.0 - ADAM_B2 ** ADAM_STEP)
    delta = -ADAM_LR * (m_hat / (jnp.sqrt(v_hat) + ADAM_EPS) + ADAM_WD * w)
    return delta, m, v


def reduce_adamw(parts, landed, params, q_idx, name, prevs, deps=()):
    n = len(parts)
    halves = 2
    in_specs, out_specs, out_shapes, operands, extra, aliases = [], [], [], [], [], {}
    for t in range(n):
        R, C = parts[t].shape[1:]
        w, m, v, layer = params[t]
        r, c = w.shape[1:]
        tr = r // halves
        assert tr % 16 == 0 and c == C
        wspec = pl.BlockSpec((None, tr, c), lambda i, q_ref, layer=layer: (layer, i, 0))
        in_specs += [pl.BlockSpec((None, tr, C), lambda i, q_ref: (q_ref[0], i, 0)),
                     pl.BlockSpec((3, tr, C), lambda i, q_ref: (0, i, 0)), wspec, wspec, wspec]
        out_specs += [wspec] * 4
        out_shapes += [jax.ShapeDtypeStruct(w.shape, F32)] * 4
        operands += [parts[t], landed[t], w, m, v]
        for k, buf in enumerate(prevs[t]):
            aliases[1 + 5 * n + len(extra)] = 4 * t + k
            extra.append(buf)
    extra += list(deps)

    def body(q_ref, *refs):
        for t in range(n):
            p_ref, l_ref, w_ref, m_ref, v_ref = refs[5 * t:5 * t + 5]
            g = p_ref[...].astype(F32)
            for k in range(3):
                g = g + l_ref[k].astype(F32)
            d, mm, vv = _adamw(w_ref[...], g, m_ref[...], v_ref[...])
            outs = refs[5 * n + 4 * t:5 * n + 4 * t + 4]
            outs[0][...] = g
            outs[1][...] = d
            outs[2][...] = mm
            outs[3][...] = vv

    res = pl.pallas_call(
        _after(body, 1 + 5 * n, extra), name=name,
        grid_spec=pltpu.PrefetchScalarGridSpec(
            num_scalar_prefetch=1, grid=(halves,),
            in_specs=in_specs + [ANY] * len(extra), out_specs=out_specs),
        out_shape=out_shapes,
        input_output_aliases=aliases,
        compiler_params=_cparams(("parallel",)),
    )(q_idx, *operands, *extra)
    return [res[4 * t:4 * t + 4] for t in range(n)]


def small_reduce_adamw(gathered, w, m, v, name):
    R, C = w.shape

    def body(a_ref, w_ref, m_ref, v_ref, g_out, d_out, m_out, v_out):
        g = a_ref[0]
        for k in range(1, NDEV):
            g = g + a_ref[k]
        d, mm, vv = _adamw(w_ref[...], g, m_ref[...], v_ref[...])
        g_out[...] = g
        d_out[...] = d
        m_out[...] = mm
        v_out[...] = vv

    out = jax.ShapeDtypeStruct((R, C), F32)
    return pl.pallas_call(body, name=name, out_shape=[out] * 4,
                          compiler_params=_cparams())(gathered, w, m, v)


def _pad_cols(a, n):
    return jnp.pad(a, ((0, 0), (0, n - a.shape[1])))


def _pad_rows(a, n):
    return jnp.pad(a, ((0, n - a.shape[0]), (0, 0)))


SMALL_ROWS = 16


def _pack_small(mix, ffn, fin, taps_full, relb):
    return jnp.concatenate([
        mix, ffn, fin.reshape(1, D), taps_full.reshape(6, D),
        jnp.pad(relb.reshape(1, NUM_BUCKETS * H), ((0, 0), (0, D - NUM_BUCKETS * H)))], axis=0)


def kernel(x, mix_norm, ffn_norm, final_norm, conv_w_in, conv_kernel, conv_w_out, attn_w_qkv, attn_w_out, rel_bias, ffn_w_gate, ffn_w_up, ffn_w_down, loss_target, m_mix_norm, m_ffn_norm, m_final_norm, m_conv_w_in, m_conv_kernel, m_conv_w_out, m_attn_w_qkv, m_attn_w_out, m_rel_bias, m_ffn_w_gate, m_ffn_w_up, m_ffn_w_down, v_mix_norm, v_ffn_norm, v_final_norm, v_conv_w_in, v_conv_kernel, v_conv_w_out, v_attn_w_qkv, v_attn_w_out, v_rel_bias, v_ffn_w_gate, v_ffn_w_up, v_ffn_w_down):
    xi, yi, ci = _me()
    me = 4 * xi + 2 * yi + ci
    c_idx = jnp.reshape(ci, (1,)).astype(jnp.int32)
    q_idx = jnp.reshape(2 * xi + yi, (1,)).astype(jnp.int32)
    col0 = me * (D // NDEV)

    taps_local = jnp.zeros((2, 3, D), F32)
    taps_local = lax.dynamic_update_slice(taps_local, conv_kernel, (0, 0, col0))
    taps_pack = jnp.pad(taps_local.reshape(6, D), ((0, 2), (0, 0)))
    taps_all = all_gather_small(taps_pack, "ag_taps")
    taps_sum = jnp.sum(taps_all, axis=0)
    taps = [jnp.pad(taps_sum[3 * j:3 * j + 3], ((0, 5), (0, 0))) for j in range(2)]

    gate_t, up_t = jnp.swapaxes(ffn_w_gate, 1, 2), jnp.swapaxes(ffn_w_up, 1, 2)
    m_gate_t, m_up_t = jnp.swapaxes(m_ffn_w_gate, 1, 2), jnp.swapaxes(m_ffn_w_up, 1, 2)
    v_gate_t, v_up_t = jnp.swapaxes(v_ffn_w_gate, 1, 2), jnp.swapaxes(v_ffn_w_up, 1, 2)

    mixer_in = (conv_w_in, attn_w_qkv)
    mixer_out = (conv_w_out, attn_w_out)
    wts = []
    for i in range(DEPTH):
        j = i // 2
        w_in = all_gather_weights([mixer_in[i % 2][j].astype(BF)], (1,), f"ag_in_l{i}")
        w_out = all_gather_weights([mixer_out[i % 2][j].astype(BF)], (0,), f"ag_out_l{i}")
        w_gu = all_gather_weights([_pad_rows(gate_t[i].astype(BF), FF_SHARD_PAD),
                                   _pad_rows(up_t[i].astype(BF), FF_SHARD_PAD)], (0, 0), f"ag_up_l{i}")
        w_d = all_gather_weights([_pad_rows(ffn_w_down[i].astype(BF), FF_SHARD_PAD)], (0,),
                                 f"ag_down_l{i}")
        wts.append(list(w_in) + list(w_out) + list(w_gu) + list(w_d))

    onehot_t, band = _bucket_onehot_t()
    bias3 = bias_tables(rel_bias.T, onehot_t, band, "bias_tables").reshape(3, H * BLK, 2 * BLK)

    saved = []
    xc = x[0]
    for i in range(DEPTH):
        w_in, w_out, w_g, w_u, w_d = wts[i]
        j = i // 2
        x_mix = xc
        z3, h_mix = norm_matmul3(xc, mix_norm[i:i + 1], w_in, f"mix_in_l{i}")
        if i % 2 == 0:
            act = conv_fwd(z3, taps[j], f"conv_fwd_l{i}")
            lse_b = None
        else:
            act, lse_b = attention_fwd(z3, bias3, f"attn_fwd_l{i}")
        xc = matmul_residual(act, w_out, xc, f"mix_out_l{i}")
        x_ffn = xc
        g, u, a, h_ffn = norm_swiglu_up(xc, ffn_norm[i:i + 1], w_g, w_u, f"ffn_up_l{i}")
        xc = matmul_residual(a, w_d, xc, f"ffn_down_l{i}")
        saved.append((x_mix, h_mix, z3, act, lse_b, x_ffn, h_ffn, g, u, a))

    dx, dxb, dg_final, sq = loss_head(xc, final_norm.reshape(1, D), loss_target[0], "loss_head")
    loss = lax.psum(0.5 * jnp.sum(sq[0]) / D, ("x", "y", "c"))

    dg_mix = [None] * DEPTH
    dg_ffn = [None] * DEPTH
    dtaps = [None, None]
    dbias_all = []
    shape_in, shape_out = (D, 3 * D // NDEV), (D // NDEV, D)
    ffn_axes, ffn_shapes = (0, 0, 0), ((FF_SHARD_PAD, D),) * 3
    stacked = {}

    def pair_stage(grads, landed1, axes, shapes, tag, tok):
        parts = pair_add(grads, landed1, axes, shapes, c_idx, f"rs_add_{tag}", deps=[tok])
        return parts, chip_exchange_grads(parts, f"rs_chip_{tag}"), parts[-1]

    def adamw_stage(parts, landed2, params, tag, tok):
        names = [p[0] for p in params]
        res = reduce_adamw(parts, landed2, [p[1:] for p in params], q_idx, f"adamw_{tag}",
                           [stacked.get(nm, ()) for nm in names], deps=[tok])
        for nm, r4 in zip(names, res):
            stacked[nm] = r4
        return res[-1][0]

    tok = dxb
    mix_wait = None
    mix_chip = None
    ffn_chip = None
    for i in reversed(range(DEPTH)):
        w_in, w_out, w_g, w_u, w_d = wts[i]
        j = i // 2
        x_mix, h_mix, z3, act, lse_b, x_ffn, h_ffn, g, u, a = saved[i]
        ffn_params = [("ffn_w_gate", gate_t, m_gate_t, v_gate_t, i),
                      ("ffn_w_up", up_t, m_up_t, v_up_t, i),
                      ("ffn_w_down", ffn_w_down, m_ffn_w_down, v_ffn_w_down, i)]
        if i % 2 == 0:
            mix_params = [("conv_w_in", conv_w_in, m_conv_w_in, v_conv_w_in, j),
                          ("conv_w_out", conv_w_out, m_conv_w_out, v_conv_w_out, j)]
        else:
            mix_params = [("attn_w_qkv", attn_w_qkv, m_attn_w_qkv, v_attn_w_qkv, j),
                          ("attn_w_out", attn_w_out, m_attn_w_out, v_attn_w_out, j)]
        dgate, dup = swiglu_bwd_da(dxb, w_d, g, u, f"ffn_da_l{i}", deps=[tok])
        tok = dgate
        if mix_wait is not None:
            grads_m, landed1_m, params_m, tag_m = mix_wait
            parts_m, landed2_m, tok = pair_stage(grads_m, landed1_m, (1, 0), (shape_in, shape_out),
                                                 tag_m, tok)
            mix_chip = (parts_m, landed2_m, params_m, tag_m)
            mix_wait = None
        grads_f = matmul_tn_group([(dgate, h_ffn), (dup, h_ffn), (a, dxb)], f"ffn_dw_l{i}",
                                  deps=[tok])
        landed1_f = pair_exchange_grads(grads_f, ffn_axes, ffn_shapes, f"rs_pair_f{i}")
        tok = grads_f[-1]
        if ffn_chip is not None:
            tok = adamw_stage(*ffn_chip, tok)
            ffn_chip = None
        dx, dxb, dg_ffn[i] = matmul_normbwd(
            [(dgate, w_g, False), (dup, w_u, False)], x_ffn, ffn_norm[i:i + 1], dx, f"ffn_dh_l{i}",
            deps=[tok])
        dxb_mix = dxb
        dact = matmul_nt(dxb, w_out, f"mix_dact_l{i}", out_dtype=F32 if i % 2 == 0 else BF)
        parts_f, landed2_f, tok = pair_stage(grads_f, landed1_f, ffn_axes, ffn_shapes, f"f{i}", dact)
        ffn_chip = (parts_f, landed2_f, ffn_params, f"f{i}")
        if i % 2 == 0:
            dz3, dtaps[j] = conv_bwd(dact, z3, taps[j], f"conv_bwd_l{i}", deps=[tok])
        else:
            dz3, dbias3 = attention_bwd(z3, dact, act, lse_b, bias3, f"attn_bwd_l{i}", deps=[tok])
            dbias_all.append(dbias3.reshape(3, H, BLK * 2 * BLK))
        grads_m = matmul_tn_group([(h_mix, dz3), (act, dxb_mix)], f"mix_dw_l{i}")
        landed1_m = pair_exchange_grads(grads_m, (1, 0), (shape_in, shape_out), f"rs_pair_m{i}")
        mix_wait = (grads_m, landed1_m, mix_params, f"m{i}")
        tok = grads_m[-1]
        if mix_chip is not None:
            tok = adamw_stage(*mix_chip, tok)
            mix_chip = None
        dx, dxb, dg_mix[i] = matmul_normbwd(
            [(dz3, w_in, True)], x_mix, mix_norm[i:i + 1], dx, f"mix_dh_l{i}", deps=[tok])
        tok = dxb
    grads_m, landed1_m, params_m, tag_m = mix_wait
    parts_m, landed2_m, tok = pair_stage(grads_m, landed1_m, (1, 0), (shape_in, shape_out), tag_m, tok)
    tok = adamw_stage(*ffn_chip, tok)

    grad_relb_t = bias_grad(jnp.concatenate(dbias_all), onehot_t, "bias_grad")
    dtaps_full = jnp.stack([dtaps[0][:3], dtaps[1][:3]])
    g_small = _pack_small(jnp.concatenate([d[0:1] for d in dg_mix], axis=0),
                          jnp.concatenate([d[0:1] for d in dg_ffn], axis=0),
                          dg_final[0], dtaps_full, grad_relb_t.T)
    gathered = all_gather_small(g_small, "ag_small_grads")

    def taps_at_cols(k):
        return lax.dynamic_update_slice(jnp.zeros((2, 3, D), F32), k, (0, 0, col0))

    w_small = _pack_small(mix_norm, ffn_norm, final_norm, taps_at_cols(conv_kernel), rel_bias)
    m_small = _pack_small(m_mix_norm, m_ffn_norm, m_final_norm, taps_at_cols(m_conv_kernel), m_rel_bias)
    v_small = _pack_small(v_mix_norm, v_ffn_norm, v_final_norm, taps_at_cols(v_conv_kernel), v_rel_bias)
    small = small_reduce_adamw(gathered, w_small, m_small, v_small, "adamw_small")

    def unpack_small(p):
        taps_p = lax.dynamic_slice(p[9:15].reshape(2, 3, D), (0, 0, col0), (2, 3, D // NDEV))
        return {"mix_norm": p[0:4], "ffn_norm": p[4:8], "final_norm": p[8],
                "conv_kernel": taps_p, "rel_bias": p[15, :NUM_BUCKETS * H].reshape(NUM_BUCKETS, H)}

    small_out = [unpack_small(p) for p in small]
    adamw_stage(parts_m, landed2_m, params_m, tag_m, small[0])

    names = ["mix_norm", "ffn_norm", "final_norm", "conv_w_in", "conv_kernel", "conv_w_out",
             "attn_w_qkv", "attn_w_out", "rel_bias", "ffn_w_gate", "ffn_w_up", "ffn_w_down"]
    outs = [loss, dx.reshape(1, S, D)]
    for o in range(4):
        for nme in names:
            if nme in ("ffn_w_gate", "ffn_w_up"):
                outs.append(jnp.swapaxes(stacked[nme][o], 1, 2))
            else:
                outs.append(stacked[nme][o] if nme in stacked else small_out[o][nme])
    return tuple(outs)
```

```python
import math

import numpy as np
import jax
import jax.numpy as jnp
from jax import lax
from jax.experimental import pallas as pl
from jax.experimental.pallas import tpu as pltpu
from jax.experimental.pallas import tpu_sc as plsc

S = 2048
D = 1024
H = 16
DH = 64
DFF = 2816
NDEV = 8
DEPTH = 4
FF_SHARD = DFF // NDEV
FF_SHARD_PAD = 384
DFF_PAD = FF_SHARD_PAD * NDEV
BLK = 128
BRANCH_DILATIONS = (1, 4, 16)
NUM_BUCKETS = 32
MAX_DISTANCE = 2048
EPS = 1e-6
NEG_INF = -1e30
SCALE = DH ** -0.5

ADAM_LR = 0.001
ADAM_B1 = 0.9
ADAM_B2 = 0.999
ADAM_EPS = 1e-08
ADAM_WD = 0.01
ADAM_STEP = 10

BF = jnp.bfloat16
F32 = jnp.float32
VMEM_LIMIT_BYTES = 56 * 1024 * 1024
KSPLIT = 512
MESH = pl.DeviceIdType.MESH
ANY = pl.BlockSpec(memory_space=pl.ANY)

_NT = (((1,), (1,)), ((), ()))
_TN = (((0,), (0,)), ((), ()))


def _cparams(sem=None):
    return pltpu.CompilerParams(dimension_semantics=sem, vmem_limit_bytes=VMEM_LIMIT_BYTES)


def _after(body, n, deps):
    nd = len(deps)
    if nd == 0:
        return body

    def ordered(*refs):
        body(*refs[:n], *refs[n + nd:])
    return ordered


def _rms(x):
    return lax.rsqrt(jnp.mean(x * x, axis=-1, keepdims=True) + EPS)


def norm_matmul3(x, gain, w, name, tm=1024, tn=1024):
    per = D // tn

    def body(x_ref, g_ref, w_ref, z_ref, h_ref, hs_ref):
        @pl.when(pl.program_id(1) == 0)
        def _():
            xv = x_ref[...]
            hv = (xv * _rms(xv) * g_ref[...]).astype(BF)
            hs_ref[...] = hv
            h_ref[...] = hv
        z_ref[...] = jnp.dot(hs_ref[...], w_ref[...], preferred_element_type=F32).astype(BF)

    return pl.pallas_call(
        body, name=name,
        grid=(S // tm, 3 * D // tn),
        in_specs=[pl.BlockSpec((tm, D), lambda i, j: (i, 0)),
                  pl.BlockSpec((1, D), lambda i, j: (0, 0)),
                  pl.BlockSpec((D, tn), lambda i, j: (0, j))],
        out_specs=[pl.BlockSpec((None, tm, tn), lambda i, j: (j // per, i, j % per)),
                   pl.BlockSpec((tm, D), lambda i, j: (i, 0))],
        out_shape=[jax.ShapeDtypeStruct((3, S, D), BF), jax.ShapeDtypeStruct((S, D), BF)],
        scratch_shapes=[pltpu.VMEM((tm, D), BF)],
        compiler_params=_cparams(("parallel", "arbitrary")),
    )(x, gain, w)


def norm_swiglu_up(x, gain, wg_t, wu_t, name, tm=1024, tn=768):
    def body(x_ref, g_ref, wg_ref, wu_ref, go_ref, uo_ref, ao_ref, h_ref, hs_ref):
        @pl.when(pl.program_id(1) == 0)
        def _():
            xv = x_ref[...]
            hv = (xv * _rms(xv) * g_ref[...]).astype(BF)
            hs_ref[...] = hv
            h_ref[...] = hv
        hv = hs_ref[...]
        g = lax.dot_general(hv, wg_ref[...], _NT, preferred_element_type=F32)
        u = lax.dot_general(hv, wu_ref[...], _NT, preferred_element_type=F32)
        go_ref[...] = g.astype(BF)
        uo_ref[...] = u.astype(BF)
        ao_ref[...] = (g * jax.nn.sigmoid(g) * u).astype(BF)

    act = jax.ShapeDtypeStruct((S, DFF_PAD), BF)
    blk = pl.BlockSpec((tm, tn), lambda i, j: (i, j))
    return pl.pallas_call(
        body, name=name,
        grid=(S // tm, DFF_PAD // tn),
        in_specs=[pl.BlockSpec((tm, D), lambda i, j: (i, 0)),
                  pl.BlockSpec((1, D), lambda i, j: (0, 0)),
                  pl.BlockSpec((tn, D), lambda i, j: (j, 0)),
                  pl.BlockSpec((tn, D), lambda i, j: (j, 0))],
        out_specs=[blk, blk, blk, pl.BlockSpec((tm, D), lambda i, j: (i, 0))],
        out_shape=[act, act, act, jax.ShapeDtypeStruct((S, D), BF)],
        scratch_shapes=[pltpu.VMEM((tm, D), BF)],
        compiler_params=_cparams(("parallel", "arbitrary")),
    )(x, gain, wg_t, wu_t)


def matmul_residual(a, w, x, name, tm=1024):
    K = a.shape[1]
    tn = D if K <= D else D // 2
    ns = K // KSPLIT
    kc = K // ns

    def body(*refs):
        x_ref, o_ref = refs[2 * ns:]
        acc = x_ref[...]
        for s in range(ns):
            acc = acc + jnp.dot(refs[s][...], refs[ns + s][...], preferred_element_type=F32)
        o_ref[...] = acc

    return pl.pallas_call(
        body, name=name,
        grid=(S // tm, D // tn),
        in_specs=[pl.BlockSpec((tm, kc), lambda i, j, s=s: (i, s)) for s in range(ns)]
        + [pl.BlockSpec((kc, tn), lambda i, j, s=s: (s, j)) for s in range(ns)]
        + [pl.BlockSpec((tm, tn), lambda i, j: (i, j))],
        out_specs=pl.BlockSpec((tm, tn), lambda i, j: (i, j)),
        out_shape=jax.ShapeDtypeStruct((S, D), F32),
        compiler_params=_cparams(("parallel", "parallel")),
    )(*([a] * ns), *([w] * ns), x)


def matmul_nt(a, w, name, out_dtype=BF, tm=1024, tn=1024, deps=()):
    K = a.shape[1]
    N = w.shape[0]

    def body(a_ref, w_ref, o_ref):
        o_ref[...] = lax.dot_general(a_ref[...], w_ref[...], _NT,
                                     preferred_element_type=F32).astype(o_ref.dtype)

    return pl.pallas_call(
        _after(body, 2, deps), name=name,
        grid=(S // tm, N // tn),
        in_specs=[pl.BlockSpec((tm, K), lambda i, j: (i, 0)),
                  pl.BlockSpec((tn, K), lambda i, j: (j, 0))] + [ANY] * len(deps),
        out_specs=pl.BlockSpec((tm, tn), lambda i, j: (i, j)),
        out_shape=jax.ShapeDtypeStruct((S, N), out_dtype),
        compiler_params=_cparams(("parallel", "parallel")),
    )(a, w, *deps)


def swiglu_bwd_da(dxb, wd, g, u, name, tm=1024, tn=768, deps=()):
    def body(dx_ref, w_ref, g_ref, u_ref, dg_ref, du_ref):
        da = lax.dot_general(dx_ref[...], w_ref[...], _NT, preferred_element_type=F32)
        gv = g_ref[...].astype(F32)
        uv = u_ref[...].astype(F32)
        sig = jax.nn.sigmoid(gv)
        dg_ref[...] = (da * uv * (sig * (1.0 + gv * (1.0 - sig)))).astype(BF)
        du_ref[...] = (da * (gv * sig)).astype(BF)

    act = jax.ShapeDtypeStruct((S, DFF_PAD), BF)
    blk = pl.BlockSpec((tm, tn), lambda i, j: (i, j))
    return pl.pallas_call(
        _after(body, 4, deps), name=name,
        grid=(S // tm, DFF_PAD // tn),
        in_specs=[pl.BlockSpec((tm, D), lambda i, j: (i, 0)),
                  pl.BlockSpec((tn, D), lambda i, j: (j, 0)),
                  blk, blk] + [ANY] * len(deps),
        out_specs=[blk, blk],
        out_shape=[act, act],
        compiler_params=_cparams(("parallel", "parallel")),
    )(dxb, wd, g, u, *deps)


def matmul_tn_group(pairs, name, tm=1024, tn=512, deps=()):
    P = len(pairs)
    steps = []
    for p, (a, b) in enumerate(pairs):
        N = 3 * D if b.ndim == 3 else b.shape[1]
        steps += [(p, i, j) for i in range(a.shape[1] // tm) for j in range(N // tn)]
    T = len(steps)
    tab = np.zeros((T, 1 + 2 * P), np.int32)
    for p in range(P):
        cur = (0, 0)
        for s, (ph, i, j) in enumerate(steps):
            if ph == p:
                cur = (i, j)
            tab[s, 1 + 2 * p:3 + 2 * p] = cur
    tab[:, 0] = [ph for ph, _, _ in steps]

    in_specs, out_specs, out_shapes, operands = [], [], [], []
    per = D // tn
    for p, (a, b) in enumerate(pairs):
        ci, cj = 1 + 2 * p, 2 + 2 * p
        in_specs.append(pl.BlockSpec((S, tm), lambda s, t, ci=ci: (0, t[s, ci])))
        if b.ndim == 3:
            in_specs.append(pl.BlockSpec((None, S, tn),
                                         lambda s, t, cj=cj: (t[s, cj] // per, 0, t[s, cj] % per)))
            N = 3 * D
        else:
            in_specs.append(pl.BlockSpec((S, tn), lambda s, t, cj=cj: (0, t[s, cj])))
            N = b.shape[1]
        out_specs.append(pl.BlockSpec((tm, tn), lambda s, t, ci=ci, cj=cj: (t[s, ci], t[s, cj])))
        out_shapes.append(jax.ShapeDtypeStruct((a.shape[1], N), BF))
        operands += [a, b]

    def body(tab_ref, *refs):
        phase = tab_ref[pl.program_id(0), 0]
        for p in range(P):
            @pl.when(phase == p)
            def _(p=p):
                refs[2 * P + p][...] = lax.dot_general(
                    refs[2 * p][...], refs[2 * p + 1][...], _TN,
                    preferred_element_type=F32).astype(BF)

    return pl.pallas_call(
        _after(body, 1 + 2 * P, deps), name=name,
        grid_spec=pltpu.PrefetchScalarGridSpec(
            num_scalar_prefetch=1, grid=(T,), in_specs=in_specs + [ANY] * len(deps),
            out_specs=out_specs),
        out_shape=out_shapes,
        compiler_params=_cparams(("arbitrary",)),
    )(jnp.asarray(tab), *operands, *deps)


def matmul_normbwd(terms, x_in, gain, dx, name, tm=512, ch=256, deps=()):
    specs, operands = [], []
    for (a, w, stacked) in terms:
        if stacked:
            specs.append(pl.BlockSpec((3, tm, D), lambda i: (0, i, 0)))
        else:
            specs.append(pl.BlockSpec((tm, a.shape[1]), lambda i: (i, 0)))
        specs.append(pl.BlockSpec(w.shape, lambda i: (0, 0), pipeline_mode=pl.Buffered(1)))
        operands += [a, w]
    nt = len(terms)

    def body(*refs):
        aw = refs[:2 * nt]
        x_ref, g_ref, dx_ref, dxo_ref, dxb_ref, dg_ref, acc_ref = refs[2 * nt:]

        @pl.when(pl.program_id(0) == 0)
        def _():
            dg_ref[...] = jnp.zeros_like(dg_ref)

        dh = None
        for t, (_, _, stacked) in enumerate(terms):
            a_ref, w_ref = aw[2 * t], aw[2 * t + 1]
            if stacked:
                parts = [lax.dot_general(a_ref[k], w_ref[:, k * D:(k + 1) * D], _NT,
                                         preferred_element_type=F32) for k in range(3)]
            else:
                parts = [jnp.dot(a_ref[...], w_ref[...], preferred_element_type=F32)]
            for p in parts:
                dh = p if dh is None else dh + p
        acc_ref[...] = dh

        def chunk(c, carry):
            rows = pl.ds(pl.multiple_of(c * ch, ch), ch)
            xv = x_ref[rows, :]
            r = _rms(xv)
            xhat = xv * r
            dhc = acc_ref[rows, :]
            dg_ref[0:1, :] += jnp.sum(dhc * xhat, axis=0, keepdims=True)
            dxh = dhc * g_ref[...]
            dxn = r * (dxh - xhat * jnp.mean(dxh * xhat, axis=-1, keepdims=True))
            out = dx_ref[rows, :] + dxn
            dxo_ref[rows, :] = out
            dxb_ref[rows, :] = out.astype(BF)
            return carry
        lax.fori_loop(0, tm // ch, chunk, 0)

    row = pl.BlockSpec((tm, D), lambda i: (i, 0))
    return pl.pallas_call(
        _after(body, 2 * nt + 3, deps), name=name,
        grid=(S // tm,),
        in_specs=specs + [row, pl.BlockSpec((1, D), lambda i: (0, 0)), row] + [ANY] * len(deps),
        out_specs=[row, row, pl.BlockSpec((8, D), lambda i: (0, 0))],
        out_shape=[jax.ShapeDtypeStruct((S, D), F32), jax.ShapeDtypeStruct((S, D), BF),
                   jax.ShapeDtypeStruct((8, D), F32)],
        scratch_shapes=[pltpu.VMEM((tm, D), F32)],
        compiler_params=_cparams(("arbitrary",)),
    )(*operands, x_in, gain, dx, *deps)


def loss_head(x, gain, target, name, tm=512):
    def body(x_ref, g_ref, t_ref, dxo_ref, dxb_ref, dg_ref, sq_ref):
        @pl.when(pl.program_id(0) == 0)
        def _():
            dg_ref[...] = jnp.zeros_like(dg_ref)
            sq_ref[...] = jnp.zeros_like(sq_ref)
        xv = x_ref[...]
        r = _rms(xv)
        xhat = xv * r
        err = xhat * g_ref[...] - t_ref[...]
        sq_ref[0:1, :] += jnp.sum(err * err, axis=0, keepdims=True)
        dy = err * (1.0 / D)
        dg_ref[0:1, :] += jnp.sum(dy * xhat, axis=0, keepdims=True)
        dxh = dy * g_ref[...]
        out = r * (dxh - xhat * jnp.mean(dxh * xhat, axis=-1, keepdims=True))
        dxo_ref[...] = out
        dxb_ref[...] = out.astype(BF)

    row = pl.BlockSpec((tm, D), lambda i: (i, 0))
    acc = pl.BlockSpec((8, D), lambda i: (0, 0))
    return pl.pallas_call(
        body, name=name,
        grid=(S // tm,),
        in_specs=[row, pl.BlockSpec((1, D), lambda i: (0, 0)), row],
        out_specs=[row, row, acc, acc],
        out_shape=[jax.ShapeDtypeStruct((S, D), F32), jax.ShapeDtypeStruct((S, D), BF),
                   jax.ShapeDtypeStruct((8, D), F32), jax.ShapeDtypeStruct((8, D), F32)],
        compiler_params=_cparams(("arbitrary",)),
    )(x, gain, target)


def _shift_down(p, n, row):
    return jnp.where(row >= n, pltpu.roll(p, n, axis=0), 0.0)


def _shift_up(p, n, row):
    return jnp.where(row < S - n, pltpu.roll(p, S - n, axis=0), 0.0)


def conv_fwd(z3, taps, name, tn=128):
    def body(z_ref, k_ref, m_ref):
        b = z_ref[0].astype(F32)
        p = z_ref[1].astype(F32) * z_ref[2].astype(F32)
        row = lax.broadcasted_iota(jnp.int32, p.shape, 0)
        y = (k_ref[2:3, :] * p + k_ref[1:2, :] * _shift_down(p, 1, row)
             + k_ref[0:1, :] * _shift_down(p, 2, row))
        m_ref[...] = (b * y).astype(BF)

    return pl.pallas_call(
        body, name=name,
        grid=(D // tn,),
        in_specs=[pl.BlockSpec((3, S, tn), lambda j: (0, 0, j)),
                  pl.BlockSpec((8, tn), lambda j: (0, j))],
        out_specs=pl.BlockSpec((S, tn), lambda j: (0, j)),
        out_shape=jax.ShapeDtypeStruct((S, D), BF),
        compiler_params=_cparams(("parallel",)),
    )(z3, taps)


def conv_bwd(dm, z3, taps, name, tn=128, deps=()):
    def body(dm_ref, z_ref, k_ref, dz_ref, dk_ref):
        dmv = dm_ref[...]
        b = z_ref[0].astype(F32)
        c = z_ref[1].astype(F32)
        u = z_ref[2].astype(F32)
        p = c * u
        row = lax.broadcasted_iota(jnp.int32, p.shape, 0)
        p1 = _shift_down(p, 1, row)
        p2 = _shift_down(p, 2, row)
        y = k_ref[2:3, :] * p + k_ref[1:2, :] * p1 + k_ref[0:1, :] * p2
        dy = dmv * b
        dz_ref[0] = (dmv * y).astype(BF)
        dp = (k_ref[2:3, :] * dy + k_ref[1:2, :] * _shift_up(dy, 1, row)
              + k_ref[0:1, :] * _shift_up(dy, 2, row))
        dz_ref[1] = (dp * u).astype(BF)
        dz_ref[2] = (dp * c).astype(BF)
        dk_ref[...] = jnp.zeros_like(dk_ref)
        dk_ref[0:1, :] = jnp.sum(dy * p2, axis=0, keepdims=True)
        dk_ref[1:2, :] = jnp.sum(dy * p1, axis=0, keepdims=True)
        dk_ref[2:3, :] = jnp.sum(dy * p, axis=0, keepdims=True)

    return pl.pallas_call(
        _after(body, 3, deps), name=name,
        grid=(D // tn,),
        in_specs=[pl.BlockSpec((S, tn), lambda j: (0, j)),
                  pl.BlockSpec((3, S, tn), lambda j: (0, 0, j)),
                  pl.BlockSpec((8, tn), lambda j: (0, j))] + [ANY] * len(deps),
        out_specs=[pl.BlockSpec((3, S, tn), lambda j: (0, 0, j)),
                   pl.BlockSpec((8, tn), lambda j: (0, j))],
        out_shape=[jax.ShapeDtypeStruct((3, S, D), BF), jax.ShapeDtypeStruct((8, D), F32)],
        compiler_params=_cparams(("parallel",)),
    )(dm, z3, taps, *deps)


def _t5_bucket(dist):
    exact = NUM_BUCKETS // 2
    df = jnp.maximum(dist, 1).astype(jnp.float32)
    large = exact + (jnp.log(df / exact) / math.log(MAX_DISTANCE / exact)
                     * (NUM_BUCKETS - exact)).astype(jnp.int32)
    large = jnp.minimum(large, NUM_BUCKETS - 1)
    return jnp.where(dist < exact, dist, large)


def _bucket_onehot_t():
    qi = jnp.arange(BLK)[:, None]
    ki = jnp.arange(2 * BLK)[None, :]
    rel = qi + BLK - ki
    band = ((rel >= 0) & (rel <= BLK)).reshape(1, -1).astype(F32)
    hots = []
    for d in BRANCH_DILATIONS:
        bucket = _t5_bucket(jnp.clip(rel, 0) * d).reshape(1, -1)
        hots.append((jnp.arange(NUM_BUCKETS)[:, None] == bucket).astype(F32))
    return jnp.stack(hots), band


def bias_tables(rel_bias_t, onehot_t, band, name):
    def body(rb_ref, oh_ref, band_ref, o_ref):
        b = jnp.dot(rb_ref[...], oh_ref[...], preferred_element_type=F32,
                    precision=lax.Precision.HIGHEST)
        o_ref[...] = jnp.where(band_ref[...] > 0.5, b, NEG_INF)

    n = BLK * 2 * BLK
    return pl.pallas_call(
        body, name=name,
        grid=(3,),
        in_specs=[pl.BlockSpec((H, NUM_BUCKETS), lambda g: (0, 0)),
                  pl.BlockSpec((None, NUM_BUCKETS, n), lambda g: (g, 0, 0)),
                  pl.BlockSpec((1, n), lambda g: (0, 0))],
        out_specs=pl.BlockSpec((None, H, n), lambda g: (g, 0, 0)),
        out_shape=jax.ShapeDtypeStruct((3, H, n), F32),
        compiler_params=_cparams(("parallel",)),
    )(rel_bias_t, onehot_t, band)


def _diagonal_onehot_t():
    c = jnp.arange(BLK)
    dist = jnp.concatenate([c + 1, (c + 1) % BLK])[None, :]
    hots = [(jnp.arange(NUM_BUCKETS)[:, None] == _t5_bucket(dist * d)).astype(F32)
            for d in BRANCH_DILATIONS]
    return jnp.stack(hots)


def bias_grad(dsums, onehot_t, name):
    def body(ds_ref, oh_ref, o_ref):
        @pl.when(pl.program_id(0) == 0)
        def _():
            o_ref[...] = jnp.zeros_like(o_ref)
        o_ref[...] += lax.dot_general(ds_ref[...], oh_ref[...], _NT, preferred_element_type=F32,
                                      precision=lax.Precision.HIGHEST)

    return pl.pallas_call(
        body, name=name,
        grid=(dsums.shape[0],),
        in_specs=[pl.BlockSpec((None, H, 2 * BLK), lambda g: (g, 0, 0)),
                  pl.BlockSpec((None, NUM_BUCKETS, 2 * BLK), lambda g: (g % 3, 0, 0))],
        out_specs=pl.BlockSpec((H, NUM_BUCKETS), lambda g: (0, 0)),
        out_shape=jax.ShapeDtypeStruct((H, NUM_BUCKETS), F32),
        compiler_params=_cparams(("arbitrary",)),
    )(dsums, onehot_t)


def _head_masks():
    lane = lax.broadcasted_iota(jnp.int32, (1, 2 * DH), 1)
    return (lane < DH, lane >= DH)


def _stack_heads(x, masks):
    zero = jnp.zeros_like(x)
    return jnp.concatenate([jnp.where(masks[0], x, zero), jnp.where(masks[1], x, zero)], axis=0)


def _deinterleave(src_ref, dst_ref, d, dtype):
    L = S // d
    for r in range(d):
        dst_ref[r * L:(r + 1) * L, :] = src_ref[pl.ds(r, L, stride=d), :].astype(dtype)


def _branch_loops(d, block):
    L = S // d
    for r in range(d):
        base = r * L
        block(base, base, BLK, True)
        for n in range(1, L // BLK):
            block(base + n * BLK, base + (n - 1) * BLK, 2 * BLK, False)


def attention_fwd(z3, bias3, name):
    W = 2 * DH
    CH = 256

    def body(q_ref, k_ref, v_ref, b_ref, o_ref, lse_ref, stage, qd, kd, vd, od, ld, on, ln):
        masks = _head_masks()
        for src, dst in ((q_ref, qd), (k_ref, kd), (v_ref, vd)):
            stage[...] = src[...].astype(F32)
            for gi, d in enumerate(BRANCH_DILATIONS[1:]):
                _deinterleave(stage, dst.at[gi], d, BF)

        for g, d in enumerate(BRANCH_DILATIONS):
            qs, ks, vs = (q_ref, k_ref, v_ref) if d == 1 else (qd.at[g - 1], kd.at[g - 1], vd.at[g - 1])
            o_dst, l_dst = (on.at[0], ln.at[0]) if d == 1 else (od, ld)

            def block(q0, k0, nk, first, g=g, qs=qs, ks=ks, vs=vs, o_dst=o_dst, l_dst=l_dst):
                q2 = _stack_heads(qs[pl.ds(q0, BLK), :], masks)
                kk = ks[pl.ds(k0, nk), :]
                vv = vs[pl.ds(k0, nk), :]
                bias = b_ref[g][:, BLK:] if first else b_ref[g]
                s = lax.dot_general(q2, kk, _NT, preferred_element_type=F32) * SCALE + bias
                mx = jnp.max(s, axis=1, keepdims=True)
                p = jnp.exp(s - mx)
                l = jnp.sum(p, axis=1, keepdims=True)
                o2 = jnp.dot(p.astype(BF), vv, preferred_element_type=F32) / l
                lse2 = mx + jnp.log(l)
                o_dst[pl.ds(q0, BLK), :] = jnp.where(masks[0], o2[:BLK], o2[BLK:])
                l_dst[pl.ds(q0, BLK), :] = jnp.where(masks[0], lse2[:BLK], lse2[BLK:])

            _branch_loops(d, block)
            if d > 1:
                L = S // d
                for r in range(d):
                    on[g, pl.ds(r, L, stride=d), :] = od[r * L:(r + 1) * L, :]
                    ln[g, pl.ds(r, L, stride=d), :] = ld[r * L:(r + 1) * L, :]

        def join(c, carry):
            rows = pl.ds(pl.multiple_of(c * CH, CH), CH)
            a, b, cc = ln[0, rows, :], ln[1, rows, :], ln[2, rows, :]
            mx = jnp.maximum(jnp.maximum(a, b), cc)
            ea, eb, ec = jnp.exp(a - mx), jnp.exp(b - mx), jnp.exp(cc - mx)
            tot = ea + eb + ec
            o_ref[rows, :] = ((ea * on[0, rows, :] + eb * on[1, rows, :] + ec * on[2, rows, :])
                              / tot).astype(BF)
            lse_ref[rows, :] = mx + jnp.log(tot)
            return carry
        lax.fori_loop(0, S // CH, join, 0)

    col = pl.BlockSpec((S, W), lambda hp: (0, hp))
    return pl.pallas_call(
        body, name=name,
        grid=(D // W,),
        in_specs=[pl.BlockSpec((None, S, W), lambda hp: (0, 0, hp)),
                  pl.BlockSpec((None, S, W), lambda hp: (1, 0, hp)),
                  pl.BlockSpec((None, S, W), lambda hp: (2, 0, hp)),
                  pl.BlockSpec((3, 2 * BLK, 2 * BLK), lambda hp: (0, hp, 0))],
        out_specs=[col, col],
        out_shape=[jax.ShapeDtypeStruct((S, D), BF), jax.ShapeDtypeStruct((S, D), F32)],
        scratch_shapes=[pltpu.VMEM((S, W), F32),
                        pltpu.VMEM((2, S, W), BF), pltpu.VMEM((2, S, W), BF), pltpu.VMEM((2, S, W), BF),
                        pltpu.VMEM((S, W), F32), pltpu.VMEM((S, W), F32),
                        pltpu.VMEM((3, S, W), F32), pltpu.VMEM((3, S, W), F32)],
        compiler_params=_cparams(("parallel",)),
    )(z3, z3, z3, bias3)


def attention_bwd(z3, dob, ob, lse_b, bias3, name, deps=()):
    W = 2 * DH
    CH = 256

    def body(q_ref, k_ref, v_ref, do_ref, o_ref, lse_ref, b_ref, dz_ref, dsum_ref,
             stage, delta, qd, kd, vd, dod, lsd, dld, res, acc, db_ref):
        masks = _head_masks()

        def rowsum(c, carry):
            rows = pl.ds(pl.multiple_of(c * CH, CH), CH)
            prod = do_ref[rows, :].astype(F32) * o_ref[rows, :].astype(F32)
            sa = jnp.sum(jnp.where(masks[0], prod, 0.0), axis=1, keepdims=True)
            sb = jnp.sum(jnp.where(masks[1], prod, 0.0), axis=1, keepdims=True)
            delta[rows, :] = jnp.where(masks[0], sa, sb)
            return carry
        lax.fori_loop(0, S // CH, rowsum, 0)

        for src, dst in ((q_ref, qd), (k_ref, kd), (v_ref, vd), (do_ref, dod)):
            stage[...] = src[...].astype(F32)
            for gi, d in enumerate(BRANCH_DILATIONS[1:]):
                _deinterleave(stage, dst.at[gi], d, BF)
        for gi, d in enumerate(BRANCH_DILATIONS[1:]):
            _deinterleave(lse_ref, lsd.at[gi], d, F32)
            _deinterleave(delta, dld.at[gi], d, F32)

        db_ref[...] = jnp.zeros_like(db_ref)
        for g, d in enumerate(BRANCH_DILATIONS):
            if d == 1:
                qs, ks, vs, dos, ls, dl = q_ref, k_ref, v_ref, do_ref, lse_ref, delta
            else:
                qs, ks, vs, dos = qd.at[g - 1], kd.at[g - 1], vd.at[g - 1], dod.at[g - 1]
                ls, dl = lsd.at[g - 1], dld.at[g - 1]
            res[1] = jnp.zeros((S, W), F32)
            res[2] = jnp.zeros((S, W), F32)

            def block(q0, k0, nk, first, g=g, qs=qs, ks=ks, vs=vs, dos=dos, ls=ls, dl=dl):
                kk = ks[pl.ds(k0, nk), :]
                vv = vs[pl.ds(k0, nk), :]
                q2 = _stack_heads(qs[pl.ds(q0, BLK), :], masks)
                do2 = _stack_heads(dos[pl.ds(q0, BLK), :], masks)
                lse_blk = ls[pl.ds(q0, BLK), :]
                del_blk = dl[pl.ds(q0, BLK), :]
                lse2 = jnp.concatenate([lse_blk[:, 0:1], lse_blk[:, DH:DH + 1]], axis=0)
                del2 = jnp.concatenate([del_blk[:, 0:1], del_blk[:, DH:DH + 1]], axis=0)
                bias = b_ref[g][:, BLK:] if first else b_ref[g]
                s = lax.dot_general(q2, kk, _NT, preferred_element_type=F32) * SCALE + bias
                p = jnp.exp(s - lse2)
                dp = lax.dot_general(do2, vv, _NT, preferred_element_type=F32)
                ds = p * (dp - del2)
                if first:
                    db_ref[g, :, BLK:] += ds
                else:
                    db_ref[g] += ds
                dsb = ds.astype(BF)
                dq2 = jnp.dot(dsb, kk, preferred_element_type=F32) * SCALE
                res[0, pl.ds(q0, BLK), :] = jnp.where(masks[0], dq2[:BLK], dq2[BLK:])
                res[1, pl.ds(k0, nk), :] += lax.dot_general(dsb, q2, _TN,
                                                            preferred_element_type=F32) * SCALE
                res[2, pl.ds(k0, nk), :] += lax.dot_general(p.astype(BF), do2, _TN,
                                                            preferred_element_type=F32)

            _branch_loops(d, block)
            L = S // d
            for t in range(3):
                if d == 1:
                    acc[t] = res[t]
                else:
                    for r in range(d):
                        acc[t, pl.ds(r, L, stride=d), :] = (acc[t, pl.ds(r, L, stride=d), :]
                                                            + res[t, r * L:(r + 1) * L, :])
        for t in range(3):
            dz_ref[t] = acc[t].astype(BF)

        flip = (lax.broadcasted_iota(jnp.int32, (BLK, BLK), 0)
                + lax.broadcasted_iota(jnp.int32, (BLK, BLK), 1) == BLK - 1).astype(F32)
        dsum_ref[...] = jnp.zeros_like(dsum_ref)
        for g in range(3):
            for hh in range(2):
                halves = []
                for half in range(2):
                    tile = db_ref[g, hh * BLK:(hh + 1) * BLK, half * BLK:(half + 1) * BLK]
                    rev = jnp.dot(tile, flip, preferred_element_type=F32,
                                  precision=lax.Precision.HIGHEST)
                    skew = pltpu.roll(rev, 0, 1, stride=1, stride_axis=0)
                    halves.append(jnp.sum(skew, axis=0, keepdims=True))
                dsum_ref[2 * g + hh:2 * g + hh + 1, :] = jnp.concatenate(halves, axis=1)

    col = pl.BlockSpec((S, W), lambda hp: (0, hp))
    return pl.pallas_call(
        _after(body, 7, deps), name=name,
        grid=(D // W,),
        in_specs=[pl.BlockSpec((None, S, W), lambda hp: (0, 0, hp)),
                  pl.BlockSpec((None, S, W), lambda hp: (1, 0, hp)),
                  pl.BlockSpec((None, S, W), lambda hp: (2, 0, hp)),
                  col, col, col,
                  pl.BlockSpec((3, 2 * BLK, 2 * BLK), lambda hp: (0, hp, 0))] + [ANY] * len(deps),
        out_specs=[pl.BlockSpec((3, S, W), lambda hp: (0, 0, hp)),
                   pl.BlockSpec((None, 8, 2 * BLK), lambda hp: (hp, 0, 0))],
        out_shape=[jax.ShapeDtypeStruct((3, S, D), BF),
                   jax.ShapeDtypeStruct((D // W, 8, 2 * BLK), F32)],
        scratch_shapes=[pltpu.VMEM((S, W), F32), pltpu.VMEM((S, W), F32),
                        pltpu.VMEM((2, S, W), BF), pltpu.VMEM((2, S, W), BF),
                        pltpu.VMEM((2, S, W), BF), pltpu.VMEM((2, S, W), BF),
                        pltpu.VMEM((2, S, W), F32), pltpu.VMEM((2, S, W), F32),
                        pltpu.VMEM((3, S, W), F32), pltpu.VMEM((3, S, W), F32),
                        pltpu.VMEM((3, 2 * BLK, 2 * BLK), F32)],
        compiler_params=_cparams(("parallel",)),
    )(z3, z3, z3, dob, ob, lse_b, bias3, *deps)


def _me():
    return lax.axis_index("x"), lax.axis_index("y"), lax.axis_index("c")


def _other_chips(x, y):
    return [(1 - x, y), (x, 1 - y), (1 - x, 1 - y)]


def _shard_window(ref, axis, t, shape):
    R, C = shape
    if axis == 0:
        return ref.at[pl.ds(pl.multiple_of(t * R, 128), R), :]
    return ref.at[:, pl.ds(pl.multiple_of(t * C, 128), C)]


def all_gather_weights(shards, axes, name):
    n = len(shards)
    shapes = [s.shape for s in shards]
    outs_shape = [jax.ShapeDtypeStruct((8 * R, C) if ax == 0 else (R, 8 * C), BF)
                  for (R, C), ax in zip(shapes, axes)]

    def body(*refs):
        ins, outs = refs[:n], refs[n:2 * n]
        send_sems, recv_sems, local_sems = refs[2 * n:]
        x, y, c = _me()
        me, sibling = (x, y, c), (x, y, 1 - c)
        xnb, ynb, diag = (1 - x, y), (x, 1 - y), (1 - x, 1 - y)
        south = c == 0
        relay_from = (jnp.where(south, x, 1 - x), jnp.where(south, 1 - y, y))
        relay_to = (jnp.where(south, 1 - x, x), jnp.where(south, y, 1 - y))
        barrier = pltpu.get_barrier_semaphore()
        for peer in [sibling, (*xnb, c), (*ynb, c)]:
            pl.semaphore_signal(barrier, inc=1, device_id=peer, device_id_type=MESH)
        pl.semaphore_wait(barrier, 3)

        def win(i, px, py, pc):
            return _shard_window(outs[i], axes[i], 4 * px + 2 * py + pc, shapes[i])

        def copy(i, k, block, to, src=None):
            return pltpu.make_async_remote_copy(
                src_ref=win(i, *block) if src is None else src, dst_ref=win(i, *block),
                send_sem=send_sems.at[i * 7 + k], recv_sem=recv_sems.at[i * 7 + k],
                device_id=to, device_id_type=MESH)

        mine = [pltpu.make_async_copy(ins[i], win(i, *me), local_sems.at[i]) for i in range(n)]
        for cp in mine:
            cp.start()
        sent = []
        for i in range(n):
            sent += [copy(i, 0, me, sibling, src=ins[i]), copy(i, 1, me, (*xnb, c), src=ins[i]),
                     copy(i, 2, me, (*ynb, c), src=ins[i])]
        for cp in sent:
            cp.start()
        for i in range(n):
            for k, chip in ((1, xnb), (2, ynb)):
                copy(i, k, (*chip, c), me).wait_recv()
                sent.append(copy(i, 3 + k, (*chip, c), sibling))
                sent[-1].start()
            sent.append(copy(i, 3, (*relay_from, c), (*relay_to, c)))
            sent[-1].start()
        for i in range(n):
            copy(i, 3, (*diag, c), me).wait_recv()
            sent.append(copy(i, 6, (*diag, c), sibling))
            sent[-1].start()
        for i in range(n):
            copy(i, 0, sibling, me).wait_recv()
            for k, chip in ((4, xnb), (5, ynb), (6, diag)):
                copy(i, k, (*chip, 1 - c), me).wait_recv()
        for cp in sent:
            cp.wait_send()
        for cp in mine:
            cp.wait()

    return pl.kernel(
        body, out_type=outs_shape, name=name,
        mesh=plsc.ScalarSubcoreMesh(axis_name="sequencer", num_cores=1),
        scratch_types=[pltpu.SemaphoreType.DMA((7 * n,)), pltpu.SemaphoreType.DMA((7 * n,)),
                       pltpu.SemaphoreType.DMA((n,))],
        compiler_params=pltpu.CompilerParams(collective_id=1),
    )(*shards)


def pair_exchange_grads(grads, axes, shapes, name):
    n = len(grads)

    def body(*refs):
        ins, outs = refs[:n], refs[n:2 * n]
        send_sems, recv_sems = refs[2 * n:]
        x, y, c = _me()
        sibling = (x, y, 1 - c)
        barrier = pltpu.get_barrier_semaphore()
        pl.semaphore_signal(barrier, inc=1, device_id=sibling, device_id_type=MESH)
        pl.semaphore_wait(barrier, 1)
        copies = []
        for i in range(n):
            for q in range(4):
                t = 2 * q + (1 - c)
                copies.append(pltpu.make_async_remote_copy(
                    src_ref=_shard_window(ins[i], axes[i], t, shapes[i]), dst_ref=outs[i].at[q],
                    send_sem=send_sems.at[i * 4 + q], recv_sem=recv_sems.at[i * 4 + q],
                    device_id=sibling, device_id_type=MESH))
        for cp in copies:
            cp.start()
        for cp in copies:
            cp.wait_recv()
        for cp in copies:
            cp.wait_send()

    return pl.kernel(
        body, out_type=[jax.ShapeDtypeStruct((4,) + tuple(sh), BF) for sh in shapes], name=name,
        mesh=plsc.ScalarSubcoreMesh(axis_name="sequencer", num_cores=1),
        scratch_types=[pltpu.SemaphoreType.DMA((4 * n,)), pltpu.SemaphoreType.DMA((4 * n,))],
        compiler_params=pltpu.CompilerParams(collective_id=2),
    )(*grads)


def pair_add(grads, landed, axes, shapes, c_idx, name, deps=()):
    n = len(grads)

    def body(c_ref, *refs):
        for t in range(n):
            refs[2 * n + t][...] = (refs[2 * t][...].astype(F32)
                                    + refs[2 * t + 1][...].astype(F32)).astype(BF)

    halves = 1
    in_specs, out_specs, out_shapes, operands = [], [], [], []
    for t in range(n):
        R, C = shapes[t]
        rh = R // halves
        if axes[t] == 0:
            in_specs.append(pl.BlockSpec(
                (rh, C), lambda q, h, c_ref: (halves * (2 * q + c_ref[0]) + h, 0)))
        else:
            in_specs.append(pl.BlockSpec((rh, C), lambda q, h, c_ref: (h, 2 * q + c_ref[0])))
        blk = pl.BlockSpec((None, rh, C), lambda q, h, c_ref: (q, h, 0))
        in_specs.append(blk)
        out_specs.append(blk)
        out_shapes.append(jax.ShapeDtypeStruct((4, R, C), BF))
        operands += [grads[t], landed[t]]
    return pl.pallas_call(
        _after(body, 1 + 2 * n, deps), name=name,
        grid_spec=pltpu.PrefetchScalarGridSpec(
            num_scalar_prefetch=1, grid=(4, halves), in_specs=in_specs + [ANY] * len(deps),
            out_specs=out_specs),
        out_shape=out_shapes,
        compiler_params=_cparams(("parallel", "parallel")),
    )(c_idx, *operands, *deps)


def chip_exchange_grads(parts, name):
    n = len(parts)

    def body(*refs):
        ins, outs = refs[:n], refs[n:2 * n]
        send_sems, recv_sems = refs[2 * n:]
        x, y, c = _me()
        barrier = pltpu.get_barrier_semaphore()
        for px, py in _other_chips(x, y):
            pl.semaphore_signal(barrier, inc=1, device_id=(px, py, c), device_id_type=MESH)
        pl.semaphore_wait(barrier, 3)
        copies = []
        for i in range(n):
            for k, (px, py) in enumerate(_other_chips(x, y)):
                copies.append(pltpu.make_async_remote_copy(
                    src_ref=ins[i].at[2 * px + py], dst_ref=outs[i].at[k],
                    send_sem=send_sems.at[i * 3 + k], recv_sem=recv_sems.at[i * 3 + k],
                    device_id=(px, py, c), device_id_type=MESH))
        for cp in copies:
            cp.start()
        for cp in copies:
            cp.wait_recv()
        for cp in copies:
            cp.wait_send()

    return pl.kernel(
        body, out_type=[jax.ShapeDtypeStruct((3,) + tuple(p.shape[1:]), BF) for p in parts], name=name,
        mesh=plsc.ScalarSubcoreMesh(axis_name="sequencer", num_cores=1),
        scratch_types=[pltpu.SemaphoreType.DMA((3 * n,)), pltpu.SemaphoreType.DMA((3 * n,))],
        compiler_params=pltpu.CompilerParams(collective_id=3),
    )(*parts)


def all_gather_small(v, name):
    R, C = v.shape

    def body(v_ref, out_ref, send_sems, recv_sems, local_sem):
        x, y, c = _me()
        me, sibling = (x, y, c), (x, y, 1 - c)
        chips = _other_chips(x, y)

        def slot(px, py, pc):
            return out_ref.at[4 * px + 2 * py + pc]

        def copy(k, block, to, src=None):
            return pltpu.make_async_remote_copy(
                src_ref=slot(*block) if src is None else src, dst_ref=slot(*block),
                send_sem=send_sems.at[k], recv_sem=recv_sems.at[k],
                device_id=to, device_id_type=MESH)

        mine = pltpu.make_async_copy(v_ref, slot(*me), local_sem)
        mine.start()
        first = [copy(0, me, sibling, src=v_ref)]
        first += [copy(1 + j, me, (*chip, c), src=v_ref) for j, chip in enumerate(chips)]
        for cp in first:
            cp.start()
        passed = [copy(4 + j, (*chip, c), sibling) for j, chip in enumerate(chips)]
        for j, chip in enumerate(chips):
            copy(1 + j, (*chip, c), me).wait_recv()
            passed[j].start()
        copy(0, sibling, me).wait_recv()
        for j, chip in enumerate(chips):
            copy(4 + j, (*chip, 1 - c), me).wait_recv()
        for cp in first + passed:
            cp.wait_send()
        mine.wait()

    return pl.pallas_call(
        body, name=name,
        in_specs=[pl.BlockSpec(memory_space=pltpu.VMEM)],
        out_specs=pl.BlockSpec(memory_space=pltpu.VMEM),
        out_shape=jax.ShapeDtypeStruct((NDEV, R, C), F32),
        scratch_shapes=[pltpu.SemaphoreType.DMA((7,)), pltpu.SemaphoreType.DMA((7,)),
                        pltpu.SemaphoreType.DMA],
    )(v)


def _adamw(w, g, m, v):
    m = ADAM_B1 * m + (1.0 - ADAM_B1) * g
    v = ADAM_B2 * v + (1.0 - ADAM_B2) * (g * g)
    m_hat = m / (1.0 - ADAM_B1 ** ADAM_STEP)
    v_hat = v / (1.0 - ADAM_B2 ** ADAM_STEP)
    delta = -ADAM_LR * (m_hat / (jnp.sqrt(v_hat) + ADAM_EPS) + ADAM_WD * w)
    return delta, m, v


def reduce_adamw(parts, landed, params, q_idx, name, prevs, deps=()):
    n = len(parts)
    halves = 2
    in_specs, out_specs, out_shapes, operands, extra, aliases = [], [], [], [], [], {}
    for t in range(n):
        R, C = parts[t].shape[1:]
        w, m, v, layer = params[t]
        r, c = w.shape[1:]
        tr = r // halves
        assert tr % 16 == 0 and c == C
        wspec = pl.BlockSpec((None, tr, c), lambda i, q_ref, layer=layer: (layer, i, 0))
        in_specs += [pl.BlockSpec((None, tr, C), lambda i, q_ref: (q_ref[0], i, 0)),
                     pl.BlockSpec((3, tr, C), lambda i, q_ref: (0, i, 0)), wspec, wspec, wspec]
        out_specs += [wspec] * 4
        out_shapes += [jax.ShapeDtypeStruct(w.shape, F32)] * 4
        operands += [parts[t], landed[t], w, m, v]
        for k, buf in enumerate(prevs[t]):
            aliases[1 + 5 * n + len(extra)] = 4 * t + k
            extra.append(buf)
    extra += list(deps)

    def body(q_ref, *refs):
        for t in range(n):
            p_ref, l_ref, w_ref, m_ref, v_ref = refs[5 * t:5 * t + 5]
            g = p_ref[...].astype(F32)
            for k in range(3):
                g = g + l_ref[k].astype(F32)
            d, mm, vv = _adamw(w_ref[...], g, m_ref[...], v_ref[...])
            outs = refs[5 * n + 4 * t:5 * n + 4 * t + 4]
            outs[0][...] = g
            outs[1][...] = d
            outs[2][...] = mm
            outs[3][...] = vv

    res = pl.pallas_call(
        _after(body, 1 + 5 * n, extra), name=name,
        grid_spec=pltpu.PrefetchScalarGridSpec(
            num_scalar_prefetch=1, grid=(halves,),
            in_specs=in_specs + [ANY] * len(extra), out_specs=out_specs),
        out_shape=out_shapes,
        input_output_aliases=aliases,
        compiler_params=_cparams(("parallel",)),
    )(q_idx, *operands, *extra)
    return [res[4 * t:4 * t + 4] for t in range(n)]


def small_reduce_adamw(gathered, w, m, v, name):
    R, C = w.shape

    def body(a_ref, w_ref, m_ref, v_ref, g_out, d_out, m_out, v_out):
        g = a_ref[0]
        for k in range(1, NDEV):
            g = g + a_ref[k]
        d, mm, vv = _adamw(w_ref[...], g, m_ref[...], v_ref[...])
        g_out[...] = g
        d_out[...] = d
        m_out[...] = mm
        v_out[...] = vv

    out = jax.ShapeDtypeStruct((R, C), F32)
    return pl.pallas_call(body, name=name, out_shape=[out] * 4,
                          compiler_params=_cparams())(gathered, w, m, v)


def _pad_cols(a, n):
    return jnp.pad(a, ((0, 0), (0, n - a.shape[1])))


def _pad_rows(a, n):
    return jnp.pad(a, ((0, n - a.shape[0]), (0, 0)))


SMALL_ROWS = 16


def _pack_small(mix, ffn, fin, taps_full, relb):
    return jnp.concatenate([
        mix, ffn, fin.reshape(1, D), taps_full.reshape(6, D),
        jnp.pad(relb.reshape(1, NUM_BUCKETS * H), ((0, 0), (0, D - NUM_BUCKETS * H)))], axis=0)


def kernel(x, mix_norm, ffn_norm, final_norm, conv_w_in, conv_kernel, conv_w_out, attn_w_qkv, attn_w_out, rel_bias, ffn_w_gate, ffn_w_up, ffn_w_down, loss_target, m_mix_norm, m_ffn_norm, m_final_norm, m_conv_w_in, m_conv_kernel, m_conv_w_out, m_attn_w_qkv, m_attn_w_out, m_rel_bias, m_ffn_w_gate, m_ffn_w_up, m_ffn_w_down, v_mix_norm, v_ffn_norm, v_final_norm, v_conv_w_in, v_conv_kernel, v_conv_w_out, v_attn_w_qkv, v_attn_w_out, v_rel_bias, v_ffn_w_gate, v_ffn_w_up, v_ffn_w_down):
    xi, yi, ci = _me()
    me = 4 * xi + 2 * yi + ci
    c_idx = jnp.reshape(ci, (1,)).astype(jnp.int32)
    q_idx = jnp.reshape(2 * xi + yi, (1,)).astype(jnp.int32)
    col0 = me * (D // NDEV)

    taps_local = jnp.zeros((2, 3, D), F32)
    taps_local = lax.dynamic_update_slice(taps_local, conv_kernel, (0, 0, col0))
    taps_pack = jnp.pad(taps_local.reshape(6, D), ((0, 2), (0, 0)))
    taps_all = all_gather_small(taps_pack, "ag_taps")
    taps_sum = jnp.sum(taps_all, axis=0)
    taps = [jnp.pad(taps_sum[3 * j:3 * j + 3], ((0, 5), (0, 0))) for j in range(2)]

    gate_t, up_t = jnp.swapaxes(ffn_w_gate, 1, 2), jnp.swapaxes(ffn_w_up, 1, 2)
    m_gate_t, m_up_t = jnp.swapaxes(m_ffn_w_gate, 1, 2), jnp.swapaxes(m_ffn_w_up, 1, 2)
    v_gate_t, v_up_t = jnp.swapaxes(v_ffn_w_gate, 1, 2), jnp.swapaxes(v_ffn_w_up, 1, 2)

    mixer_in = (conv_w_in, attn_w_qkv)
    mixer_out = (conv_w_out, attn_w_out)
    wts = []
    for i in range(DEPTH):
        j = i // 2
        shards = [mixer_in[i % 2][j].astype(BF), mixer_out[i % 2][j].astype(BF),
                  _pad_rows(gate_t[i].astype(BF), FF_SHARD_PAD),
                  _pad_rows(up_t[i].astype(BF), FF_SHARD_PAD),
                  _pad_rows(ffn_w_down[i].astype(BF), FF_SHARD_PAD)]
        axes = (1, 0, 0, 0, 0)
        groups = ((0, 1), (1, 2), (2, 4), (4, 5)) if i == 0 else ((0, 2), (2, 5))
        layer = []
        for lo, hi in groups:
            layer += list(all_gather_weights(shards[lo:hi], axes[lo:hi], f"ag_l{i}_{lo}"))
        wts.append(layer)

    onehot_t, band = _bucket_onehot_t()
    bias3 = bias_tables(rel_bias.T, onehot_t, band, "bias_tables").reshape(3, H * BLK, 2 * BLK)

    saved = []
    xc = x[0]
    for i in range(DEPTH):
        w_in, w_out, w_g, w_u, w_d = wts[i]
        j = i // 2
        x_mix = xc
        z3, h_mix = norm_matmul3(xc, mix_norm[i:i + 1], w_in, f"mix_in_l{i}")
        if i % 2 == 0:
            act = conv_fwd(z3, taps[j], f"conv_fwd_l{i}")
            lse_b = None
        else:
            act, lse_b = attention_fwd(z3, bias3, f"attn_fwd_l{i}")
        xc = matmul_residual(act, w_out, xc, f"mix_out_l{i}")
        x_ffn = xc
        g, u, a, h_ffn = norm_swiglu_up(xc, ffn_norm[i:i + 1], w_g, w_u, f"ffn_up_l{i}")
        xc = matmul_residual(a, w_d, xc, f"ffn_down_l{i}")
        saved.append((x_mix, h_mix, z3, act, lse_b, x_ffn, h_ffn, g, u, a))

    dx, dxb, dg_final, sq = loss_head(xc, final_norm.reshape(1, D), loss_target[0], "loss_head")
    loss = lax.psum(0.5 * jnp.sum(sq[0]) / D, ("x", "y", "c"))

    dg_mix = [None] * DEPTH
    dg_ffn = [None] * DEPTH
    dtaps = [None, None]
    dbias_all = []
    shape_in, shape_out = (D, 3 * D // NDEV), (D // NDEV, D)
    ffn_axes, ffn_shapes = (0, 0, 0), ((FF_SHARD_PAD, D),) * 3
    stacked = {}

    def pair_stage(grads, landed1, axes, shapes, tag, tok):
        parts = pair_add(grads, landed1, axes, shapes, c_idx, f"rs_add_{tag}", deps=[tok])
        return parts, chip_exchange_grads(parts, f"rs_chip_{tag}"), parts[-1]

    def adamw_stage(parts, landed2, params, tag, tok):
        names = [p[0] for p in params]
        res = reduce_adamw(parts, landed2, [p[1:] for p in params], q_idx, f"adamw_{tag}",
                           [stacked.get(nm, ()) for nm in names], deps=[tok])
        for nm, r4 in zip(names, res):
            stacked[nm] = r4
        return res[-1][0]

    tok = dxb
    mix_wait = None
    mix_chip = None
    ffn_chip = None
    for i in reversed(range(DEPTH)):
        w_in, w_out, w_g, w_u, w_d = wts[i]
        j = i // 2
        x_mix, h_mix, z3, act, lse_b, x_ffn, h_ffn, g, u, a = saved[i]
        ffn_params = [("ffn_w_gate", gate_t, m_gate_t, v_gate_t, i),
                      ("ffn_w_up", up_t, m_up_t, v_up_t, i),
                      ("ffn_w_down", ffn_w_down, m_ffn_w_down, v_ffn_w_down, i)]
        if i % 2 == 0:
            mix_params = [("conv_w_in", conv_w_in, m_conv_w_in, v_conv_w_in, j),
                          ("conv_w_out", conv_w_out, m_conv_w_out, v_conv_w_out, j)]
        else:
            mix_params = [("attn_w_qkv", attn_w_qkv, m_attn_w_qkv, v_attn_w_qkv, j),
                          ("attn_w_out", attn_w_out, m_attn_w_out, v_attn_w_out, j)]
        dgate, dup = swiglu_bwd_da(dxb, w_d, g, u, f"ffn_da_l{i}", deps=[tok])
        tok = dgate
        if mix_wait is not None:
            grads_m, landed1_m, params_m, tag_m = mix_wait
            parts_m, landed2_m, tok = pair_stage(grads_m, landed1_m, (1, 0), (shape_in, shape_out),
                                                 tag_m, tok)
            mix_chip = (parts_m, landed2_m, params_m, tag_m)
            mix_wait = None
        grads_f = matmul_tn_group([(dgate, h_ffn), (dup, h_ffn), (a, dxb)], f"ffn_dw_l{i}",
                                  deps=[tok])
        landed1_f = pair_exchange_grads(grads_f, ffn_axes, ffn_shapes, f"rs_pair_f{i}")
        tok = grads_f[-1]
        if ffn_chip is not None:
            tok = adamw_stage(*ffn_chip, tok)
            ffn_chip = None
        dx, dxb, dg_ffn[i] = matmul_normbwd(
            [(dgate, w_g, False), (dup, w_u, False)], x_ffn, ffn_norm[i:i + 1], dx, f"ffn_dh_l{i}",
            deps=[tok])
        dxb_mix = dxb
        dact = matmul_nt(dxb, w_out, f"mix_dact_l{i}", out_dtype=F32 if i % 2 == 0 else BF)
        parts_f, landed2_f, tok = pair_stage(grads_f, landed1_f, ffn_axes, ffn_shapes, f"f{i}", dact)
        ffn_chip = (parts_f, landed2_f, ffn_params, f"f{i}")
        if i % 2 == 0:
            dz3, dtaps[j] = conv_bwd(dact, z3, taps[j], f"conv_bwd_l{i}", deps=[tok])
        else:
            dz3, dsum = attention_bwd(z3, dact, act, lse_b, bias3, f"attn_bwd_l{i}", deps=[tok])
            dbias_all.append(dsum[:, :6].reshape(H // 2, 3, 2, 2 * BLK).transpose(1, 0, 2, 3)
                             .reshape(3, H, 2 * BLK))
        grads_m = matmul_tn_group([(h_mix, dz3), (act, dxb_mix)], f"mix_dw_l{i}")
        landed1_m = pair_exchange_grads(grads_m, (1, 0), (shape_in, shape_out), f"rs_pair_m{i}")
        mix_wait = (grads_m, landed1_m, mix_params, f"m{i}")
        tok = grads_m[-1]
        if mix_chip is not None:
            tok = adamw_stage(*mix_chip, tok)
            mix_chip = None
        dx, dxb, dg_mix[i] = matmul_normbwd(
            [(dz3, w_in, True)], x_mix, mix_norm[i:i + 1], dx, f"mix_dh_l{i}", deps=[tok])
        tok = dxb
    grads_m, landed1_m, params_m, tag_m = mix_wait
    parts_m, landed2_m, tok = pair_stage(grads_m, landed1_m, (1, 0), (shape_in, shape_out), tag_m, tok)
    tok = adamw_stage(*ffn_chip, tok)

    grad_relb_t = bias_grad(jnp.concatenate(dbias_all), _diagonal_onehot_t(), "bias_grad")
    dtaps_full = jnp.stack([dtaps[0][:3], dtaps[1][:3]])
    g_small = _pack_small(jnp.concatenate([d[0:1] for d in dg_mix], axis=0),
                          jnp.concatenate([d[0:1] for d in dg_ffn], axis=0),
                          dg_final[0], dtaps_full, grad_relb_t.T)
    gathered = all_gather_small(g_small, "ag_small_grads")

    def taps_at_cols(k):
        return lax.dynamic_update_slice(jnp.zeros((2, 3, D), F32), k, (0, 0, col0))

    w_small = _pack_small(mix_norm, ffn_norm, final_norm, taps_at_cols(conv_kernel), rel_bias)
    m_small = _pack_small(m_mix_norm, m_ffn_norm, m_final_norm, taps_at_cols(m_conv_kernel), m_rel_bias)
    v_small = _pack_small(v_mix_norm, v_ffn_norm, v_final_norm, taps_at_cols(v_conv_kernel), v_rel_bias)
    small = small_reduce_adamw(gathered, w_small, m_small, v_small, "adamw_small")

    def unpack_small(p):
        taps_p = lax.dynamic_slice(p[9:15].reshape(2, 3, D), (0, 0, col0), (2, 3, D // NDEV))
        return {"mix_norm": p[0:4], "ffn_norm": p[4:8], "final_norm": p[8],
                "conv_kernel": taps_p, "rel_bias": p[15, :NUM_BUCKETS * H].reshape(NUM_BUCKETS, H)}

    small_out = [unpack_small(p) for p in small]
    adamw_stage(parts_m, landed2_m, params_m, tag_m, small[0])

    names = ["mix_norm", "ffn_norm", "final_norm", "conv_w_in", "conv_kernel", "conv_w_out",
             "attn_w_qkv", "attn_w_out", "rel_bias", "ffn_w_gate", "ffn_w_up", "ffn_w_down"]
    outs = [loss, dx.reshape(1, S, D)]
    for o in range(4):
        for nme in names:
            if nme in ("ffn_w_gate", "ffn_w_up"):
                outs.append(jnp.swapaxes(stacked[nme][o], 1, 2))
            else:
                outs.append(stacked[nme][o] if nme in stacked else small_out[o][nme])
    return tuple(outs)
```

```python
import math

import numpy as np
import jax
import jax.numpy as jnp
from jax import lax
from jax.experimental import pallas as pl
from jax.experimental.pallas import tpu as pltpu
from jax.experimental.pallas import tpu_sc as plsc

S = 2048
D = 1024
H = 16
DH = 64
DFF = 2816
NDEV = 8
DEPTH = 4
FF_SHARD = DFF // NDEV
FF_SHARD_PAD = 384
DFF_PAD = FF_SHARD_PAD * NDEV
BLK = 128
BRANCH_DILATIONS = (1, 4, 16)
NUM_BUCKETS = 32
MAX_DISTANCE = 2048
EPS = 1e-6
NEG_INF = -1e30
SCALE = DH ** -0.5

ADAM_LR = 0.001
ADAM_B1 = 0.9
ADAM_B2 = 0.999
ADAM_EPS = 1e-08
ADAM_WD = 0.01
ADAM_STEP = 10

BF = jnp.bfloat16
F32 = jnp.float32
VMEM_LIMIT_BYTES = 56 * 1024 * 1024
KSPLIT = 512
MESH = pl.DeviceIdType.MESH
ANY = pl.BlockSpec(memory_space=pl.ANY)

_NT = (((1,), (1,)), ((), ()))
_TN = (((0,), (0,)), ((), ()))


def _cparams(sem=None):
    return pltpu.CompilerParams(dimension_semantics=sem, vmem_limit_bytes=VMEM_LIMIT_BYTES)


def _after(body, n, deps):
    nd = len(deps)
    if nd == 0:
        return body

    def ordered(*refs):
        body(*refs[:n], *refs[n + nd:])
    return ordered


def _rms(x):
    return lax.rsqrt(jnp.mean(x * x, axis=-1, keepdims=True) + EPS)


def norm_matmul3(x, gain, w, name, tm=1024, tn=1024):
    per = D // tn

    def body(x_ref, g_ref, w_ref, z_ref, h_ref, hs_ref):
        @pl.when(pl.program_id(1) == 0)
        def _():
            xv = x_ref[...]
            hv = (xv * _rms(xv) * g_ref[...]).astype(BF)
            hs_ref[...] = hv
            h_ref[...] = hv
        z_ref[...] = jnp.dot(hs_ref[...], w_ref[...], preferred_element_type=F32).astype(BF)

    return pl.pallas_call(
        body, name=name,
        grid=(S // tm, 3 * D // tn),
        in_specs=[pl.BlockSpec((tm, D), lambda i, j: (i, 0)),
                  pl.BlockSpec((1, D), lambda i, j: (0, 0)),
                  pl.BlockSpec((D, tn), lambda i, j: (0, j))],
        out_specs=[pl.BlockSpec((None, tm, tn), lambda i, j: (j // per, i, j % per)),
                   pl.BlockSpec((tm, D), lambda i, j: (i, 0))],
        out_shape=[jax.ShapeDtypeStruct((3, S, D), BF), jax.ShapeDtypeStruct((S, D), BF)],
        scratch_shapes=[pltpu.VMEM((tm, D), BF)],
        compiler_params=_cparams(("parallel", "arbitrary")),
    )(x, gain, w)


def norm_swiglu_up(x, gain, wg_t, wu_t, name, tm=1024, tn=768):
    def body(x_ref, g_ref, wg_ref, wu_ref, go_ref, uo_ref, ao_ref, h_ref, hs_ref):
        @pl.when(pl.program_id(1) == 0)
        def _():
            xv = x_ref[...]
            hv = (xv * _rms(xv) * g_ref[...]).astype(BF)
            hs_ref[...] = hv
            h_ref[...] = hv
        hv = hs_ref[...]
        g = lax.dot_general(hv, wg_ref[...], _NT, preferred_element_type=F32)
        u = lax.dot_general(hv, wu_ref[...], _NT, preferred_element_type=F32)
        go_ref[...] = g.astype(BF)
        uo_ref[...] = u.astype(BF)
        ao_ref[...] = (g * jax.nn.sigmoid(g) * u).astype(BF)

    act = jax.ShapeDtypeStruct((S, DFF_PAD), BF)
    blk = pl.BlockSpec((tm, tn), lambda i, j: (i, j))
    return pl.pallas_call(
        body, name=name,
        grid=(S // tm, DFF_PAD // tn),
        in_specs=[pl.BlockSpec((tm, D), lambda i, j: (i, 0)),
                  pl.BlockSpec((1, D), lambda i, j: (0, 0)),
                  pl.BlockSpec((tn, D), lambda i, j: (j, 0)),
                  pl.BlockSpec((tn, D), lambda i, j: (j, 0))],
        out_specs=[blk, blk, blk, pl.BlockSpec((tm, D), lambda i, j: (i, 0))],
        out_shape=[act, act, act, jax.ShapeDtypeStruct((S, D), BF)],
        scratch_shapes=[pltpu.VMEM((tm, D), BF)],
        compiler_params=_cparams(("parallel", "arbitrary")),
    )(x, gain, wg_t, wu_t)


def matmul_residual(a, w, x, name, tm=1024):
    K = a.shape[1]
    tn = D if K <= D else D // 2
    ns = K // KSPLIT
    kc = K // ns

    def body(*refs):
        x_ref, o_ref = refs[2 * ns:]
        acc = x_ref[...]
        for s in range(ns):
            acc = acc + jnp.dot(refs[s][...], refs[ns + s][...], preferred_element_type=F32)
        o_ref[...] = acc

    return pl.pallas_call(
        body, name=name,
        grid=(S // tm, D // tn),
        in_specs=[pl.BlockSpec((tm, kc), lambda i, j, s=s: (i, s)) for s in range(ns)]
        + [pl.BlockSpec((kc, tn), lambda i, j, s=s: (s, j)) for s in range(ns)]
        + [pl.BlockSpec((tm, tn), lambda i, j: (i, j))],
        out_specs=pl.BlockSpec((tm, tn), lambda i, j: (i, j)),
        out_shape=jax.ShapeDtypeStruct((S, D), F32),
        compiler_params=_cparams(("parallel", "parallel")),
    )(*([a] * ns), *([w] * ns), x)


def matmul_nt(a, w, name, out_dtype=BF, tm=1024, tn=1024, deps=()):
    K = a.shape[1]
    N = w.shape[0]

    def body(a_ref, w_ref, o_ref):
        o_ref[...] = lax.dot_general(a_ref[...], w_ref[...], _NT,
                                     preferred_element_type=F32).astype(o_ref.dtype)

    return pl.pallas_call(
        _after(body, 2, deps), name=name,
        grid=(S // tm, N // tn),
        in_specs=[pl.BlockSpec((tm, K), lambda i, j: (i, 0)),
                  pl.BlockSpec((tn, K), lambda i, j: (j, 0))] + [ANY] * len(deps),
        out_specs=pl.BlockSpec((tm, tn), lambda i, j: (i, j)),
        out_shape=jax.ShapeDtypeStruct((S, N), out_dtype),
        compiler_params=_cparams(("parallel", "parallel")),
    )(a, w, *deps)


def swiglu_bwd_da(dxb, wd, g, u, name, tm=1024, tn=768, deps=()):
    def body(dx_ref, w_ref, g_ref, u_ref, dg_ref, du_ref):
        da = lax.dot_general(dx_ref[...], w_ref[...], _NT, preferred_element_type=F32)
        gv = g_ref[...].astype(F32)
        uv = u_ref[...].astype(F32)
        sig = jax.nn.sigmoid(gv)
        dg_ref[...] = (da * uv * (sig * (1.0 + gv * (1.0 - sig)))).astype(BF)
        du_ref[...] = (da * (gv * sig)).astype(BF)

    act = jax.ShapeDtypeStruct((S, DFF_PAD), BF)
    blk = pl.BlockSpec((tm, tn), lambda i, j: (i, j))
    return pl.pallas_call(
        _after(body, 4, deps), name=name,
        grid=(S // tm, DFF_PAD // tn),
        in_specs=[pl.BlockSpec((tm, D), lambda i, j: (i, 0)),
                  pl.BlockSpec((tn, D), lambda i, j: (j, 0)),
                  blk, blk] + [ANY] * len(deps),
        out_specs=[blk, blk],
        out_shape=[act, act],
        compiler_params=_cparams(("parallel", "parallel")),
    )(dxb, wd, g, u, *deps)


def matmul_tn_group(pairs, name, tm=1024, tn=512, deps=()):
    P = len(pairs)
    steps = []
    for p, (a, b) in enumerate(pairs):
        N = 3 * D if b.ndim == 3 else b.shape[1]
        steps += [(p, i, j) for i in range(a.shape[1] // tm) for j in range(N // tn)]
    T = len(steps)
    tab = np.zeros((T, 1 + 2 * P), np.int32)
    for p in range(P):
        cur = (0, 0)
        for s, (ph, i, j) in enumerate(steps):
            if ph == p:
                cur = (i, j)
            tab[s, 1 + 2 * p:3 + 2 * p] = cur
    tab[:, 0] = [ph for ph, _, _ in steps]

    in_specs, out_specs, out_shapes, operands = [], [], [], []
    per = D // tn
    for p, (a, b) in enumerate(pairs):
        ci, cj = 1 + 2 * p, 2 + 2 * p
        in_specs.append(pl.BlockSpec((S, tm), lambda s, t, ci=ci: (0, t[s, ci])))
        if b.ndim == 3:
            in_specs.append(pl.BlockSpec((None, S, tn),
                                         lambda s, t, cj=cj: (t[s, cj] // per, 0, t[s, cj] % per)))
            N = 3 * D
        else:
            in_specs.append(pl.BlockSpec((S, tn), lambda s, t, cj=cj: (0, t[s, cj])))
            N = b.shape[1]
        out_specs.append(pl.BlockSpec((tm, tn), lambda s, t, ci=ci, cj=cj: (t[s, ci], t[s, cj])))
        out_shapes.append(jax.ShapeDtypeStruct((a.shape[1], N), BF))
        operands += [a, b]

    def body(tab_ref, *refs):
        phase = tab_ref[pl.program_id(0), 0]
        for p in range(P):
            @pl.when(phase == p)
            def _(p=p):
                refs[2 * P + p][...] = lax.dot_general(
                    refs[2 * p][...], refs[2 * p + 1][...], _TN,
                    preferred_element_type=F32).astype(BF)

    return pl.pallas_call(
        _after(body, 1 + 2 * P, deps), name=name,
        grid_spec=pltpu.PrefetchScalarGridSpec(
            num_scalar_prefetch=1, grid=(T,), in_specs=in_specs + [ANY] * len(deps),
            out_specs=out_specs),
        out_shape=out_shapes,
        compiler_params=_cparams(("arbitrary",)),
    )(jnp.asarray(tab), *operands, *deps)


def matmul_normbwd(terms, x_in, gain, dx, name, tm=512, ch=256, deps=()):
    specs, operands = [], []
    for (a, w, stacked) in terms:
        if stacked:
            specs.append(pl.BlockSpec((3, tm, D), lambda i: (0, i, 0)))
        else:
            specs.append(pl.BlockSpec((tm, a.shape[1]), lambda i: (i, 0)))
        specs.append(pl.BlockSpec(w.shape, lambda i: (0, 0), pipeline_mode=pl.Buffered(1)))
        operands += [a, w]
    nt = len(terms)

    def body(*refs):
        aw = refs[:2 * nt]
        x_ref, g_ref, dx_ref, dxo_ref, dxb_ref, dg_ref, acc_ref = refs[2 * nt:]

        @pl.when(pl.program_id(0) == 0)
        def _():
            dg_ref[...] = jnp.zeros_like(dg_ref)

        dh = None
        for t, (_, _, stacked) in enumerate(terms):
            a_ref, w_ref = aw[2 * t], aw[2 * t + 1]
            if stacked:
                parts = [lax.dot_general(a_ref[k], w_ref[:, k * D:(k + 1) * D], _NT,
                                         preferred_element_type=F32) for k in range(3)]
            else:
                parts = [jnp.dot(a_ref[...], w_ref[...], preferred_element_type=F32)]
            for p in parts:
                dh = p if dh is None else dh + p
        acc_ref[...] = dh

        def chunk(c, carry):
            rows = pl.ds(pl.multiple_of(c * ch, ch), ch)
            xv = x_ref[rows, :]
            r = _rms(xv)
            xhat = xv * r
            dhc = acc_ref[rows, :]
            dg_ref[0:1, :] += jnp.sum(dhc * xhat, axis=0, keepdims=True)
            dxh = dhc * g_ref[...]
            dxn = r * (dxh - xhat * jnp.mean(dxh * xhat, axis=-1, keepdims=True))
            out = dx_ref[rows, :] + dxn
            dxo_ref[rows, :] = out
            dxb_ref[rows, :] = out.astype(BF)
            return carry
        lax.fori_loop(0, tm // ch, chunk, 0)

    row = pl.BlockSpec((tm, D), lambda i: (i, 0))
    return pl.pallas_call(
        _after(body, 2 * nt + 3, deps), name=name,
        grid=(S // tm,),
        in_specs=specs + [row, pl.BlockSpec((1, D), lambda i: (0, 0)), row] + [ANY] * len(deps),
        out_specs=[row, row, pl.BlockSpec((8, D), lambda i: (0, 0))],
        out_shape=[jax.ShapeDtypeStruct((S, D), F32), jax.ShapeDtypeStruct((S, D), BF),
                   jax.ShapeDtypeStruct((8, D), F32)],
        scratch_shapes=[pltpu.VMEM((tm, D), F32)],
        compiler_params=_cparams(("arbitrary",)),
    )(*operands, x_in, gain, dx, *deps)


def loss_head(x, gain, target, name, tm=512):
    def body(x_ref, g_ref, t_ref, dxo_ref, dxb_ref, dg_ref, sq_ref):
        @pl.when(pl.program_id(0) == 0)
        def _():
            dg_ref[...] = jnp.zeros_like(dg_ref)
            sq_ref[...] = jnp.zeros_like(sq_ref)
        xv = x_ref[...]
        r = _rms(xv)
        xhat = xv * r
        err = xhat * g_ref[...] - t_ref[...]
        sq_ref[0:1, :] += jnp.sum(err * err, axis=0, keepdims=True)
        dy = err * (1.0 / D)
        dg_ref[0:1, :] += jnp.sum(dy * xhat, axis=0, keepdims=True)
        dxh = dy * g_ref[...]
        out = r * (dxh - xhat * jnp.mean(dxh * xhat, axis=-1, keepdims=True))
        dxo_ref[...] = out
        dxb_ref[...] = out.astype(BF)

    row = pl.BlockSpec((tm, D), lambda i: (i, 0))
    acc = pl.BlockSpec((8, D), lambda i: (0, 0))
    return pl.pallas_call(
        body, name=name,
        grid=(S // tm,),
        in_specs=[row, pl.BlockSpec((1, D), lambda i: (0, 0)), row],
        out_specs=[row, row, acc, acc],
        out_shape=[jax.ShapeDtypeStruct((S, D), F32), jax.ShapeDtypeStruct((S, D), BF),
                   jax.ShapeDtypeStruct((8, D), F32), jax.ShapeDtypeStruct((8, D), F32)],
        compiler_params=_cparams(("arbitrary",)),
    )(x, gain, target)


def _shift_down(p, n, row):
    return jnp.where(row >= n, pltpu.roll(p, n, axis=0), 0.0)


def _shift_up(p, n, row):
    return jnp.where(row < S - n, pltpu.roll(p, S - n, axis=0), 0.0)


def conv_fwd(z3, taps, name, tn=128):
    def body(z_ref, k_ref, m_ref):
        b = z_ref[0].astype(F32)
        p = z_ref[1].astype(F32) * z_ref[2].astype(F32)
        row = lax.broadcasted_iota(jnp.int32, p.shape, 0)
        y = (k_ref[2:3, :] * p + k_ref[1:2, :] * _shift_down(p, 1, row)
             + k_ref[0:1, :] * _shift_down(p, 2, row))
        m_ref[...] = (b * y).astype(BF)

    return pl.pallas_call(
        body, name=name,
        grid=(D // tn,),
        in_specs=[pl.BlockSpec((3, S, tn), lambda j: (0, 0, j)),
                  pl.BlockSpec((8, tn), lambda j: (0, j))],
        out_specs=pl.BlockSpec((S, tn), lambda j: (0, j)),
        out_shape=jax.ShapeDtypeStruct((S, D), BF),
        compiler_params=_cparams(("parallel",)),
    )(z3, taps)


def conv_bwd(dm, z3, taps, name, tn=128, deps=()):
    def body(dm_ref, z_ref, k_ref, dz_ref, dk_ref):
        dmv = dm_ref[...]
        b = z_ref[0].astype(F32)
        c = z_ref[1].astype(F32)
        u = z_ref[2].astype(F32)
        p = c * u
        row = lax.broadcasted_iota(jnp.int32, p.shape, 0)
        p1 = _shift_down(p, 1, row)
        p2 = _shift_down(p, 2, row)
        y = k_ref[2:3, :] * p + k_ref[1:2, :] * p1 + k_ref[0:1, :] * p2
        dy = dmv * b
        dz_ref[0] = (dmv * y).astype(BF)
        dp = (k_ref[2:3, :] * dy + k_ref[1:2, :] * _shift_up(dy, 1, row)
              + k_ref[0:1, :] * _shift_up(dy, 2, row))
        dz_ref[1] = (dp * u).astype(BF)
        dz_ref[2] = (dp * c).astype(BF)
        dk_ref[...] = jnp.zeros_like(dk_ref)
        dk_ref[0:1, :] = jnp.sum(dy * p2, axis=0, keepdims=True)
        dk_ref[1:2, :] = jnp.sum(dy * p1, axis=0, keepdims=True)
        dk_ref[2:3, :] = jnp.sum(dy * p, axis=0, keepdims=True)

    return pl.pallas_call(
        _after(body, 3, deps), name=name,
        grid=(D // tn,),
        in_specs=[pl.BlockSpec((S, tn), lambda j: (0, j)),
                  pl.BlockSpec((3, S, tn), lambda j: (0, 0, j)),
                  pl.BlockSpec((8, tn), lambda j: (0, j))] + [ANY] * len(deps),
        out_specs=[pl.BlockSpec((3, S, tn), lambda j: (0, 0, j)),
                   pl.BlockSpec((8, tn), lambda j: (0, j))],
        out_shape=[jax.ShapeDtypeStruct((3, S, D), BF), jax.ShapeDtypeStruct((8, D), F32)],
        compiler_params=_cparams(("parallel",)),
    )(dm, z3, taps, *deps)


def _t5_bucket(dist):
    exact = NUM_BUCKETS // 2
    df = jnp.maximum(dist, 1).astype(jnp.float32)
    large = exact + (jnp.log(df / exact) / math.log(MAX_DISTANCE / exact)
                     * (NUM_BUCKETS - exact)).astype(jnp.int32)
    large = jnp.minimum(large, NUM_BUCKETS - 1)
    return jnp.where(dist < exact, dist, large)


def _bucket_onehot_t():
    qi = jnp.arange(BLK)[:, None]
    ki = jnp.arange(2 * BLK)[None, :]
    rel = qi + BLK - ki
    band = ((rel >= 0) & (rel <= BLK)).reshape(1, -1).astype(F32)
    hots = []
    for d in BRANCH_DILATIONS:
        bucket = _t5_bucket(jnp.clip(rel, 0) * d).reshape(1, -1)
        hots.append((jnp.arange(NUM_BUCKETS)[:, None] == bucket).astype(F32))
    return jnp.stack(hots), band


def bias_tables(rel_bias_t, onehot_t, band, name):
    def body(rb_ref, oh_ref, band_ref, o_ref):
        b = jnp.dot(rb_ref[...], oh_ref[...], preferred_element_type=F32,
                    precision=lax.Precision.HIGHEST)
        o_ref[...] = jnp.where(band_ref[...] > 0.5, b, NEG_INF)

    n = BLK * 2 * BLK
    return pl.pallas_call(
        body, name=name,
        grid=(3,),
        in_specs=[pl.BlockSpec((H, NUM_BUCKETS), lambda g: (0, 0)),
                  pl.BlockSpec((None, NUM_BUCKETS, n), lambda g: (g, 0, 0)),
                  pl.BlockSpec((1, n), lambda g: (0, 0))],
        out_specs=pl.BlockSpec((None, H, n), lambda g: (g, 0, 0)),
        out_shape=jax.ShapeDtypeStruct((3, H, n), F32),
        compiler_params=_cparams(("parallel",)),
    )(rel_bias_t, onehot_t, band)


def _diagonal_onehot_t():
    c = jnp.arange(BLK)
    dist = jnp.concatenate([c + 1, (c + 1) % BLK])[None, :]
    hots = [(jnp.arange(NUM_BUCKETS)[:, None] == _t5_bucket(dist * d)).astype(F32)
            for d in BRANCH_DILATIONS]
    return jnp.stack(hots)


def bias_grad(dsums, onehot_t, name):
    def body(ds_ref, oh_ref, o_ref):
        @pl.when(pl.program_id(0) == 0)
        def _():
            o_ref[...] = jnp.zeros_like(o_ref)
        o_ref[...] += lax.dot_general(ds_ref[...], oh_ref[...], _NT, preferred_element_type=F32,
                                      precision=lax.Precision.HIGHEST)

    return pl.pallas_call(
        body, name=name,
        grid=(dsums.shape[0],),
        in_specs=[pl.BlockSpec((None, H, 2 * BLK), lambda g: (g, 0, 0)),
                  pl.BlockSpec((None, NUM_BUCKETS, 2 * BLK), lambda g: (g % 3, 0, 0))],
        out_specs=pl.BlockSpec((H, NUM_BUCKETS), lambda g: (0, 0)),
        out_shape=jax.ShapeDtypeStruct((H, NUM_BUCKETS), F32),
        compiler_params=_cparams(("arbitrary",)),
    )(dsums, onehot_t)


def _head_masks():
    lane = lax.broadcasted_iota(jnp.int32, (1, 2 * DH), 1)
    return (lane < DH, lane >= DH)


def _stack_heads(x, masks):
    zero = jnp.zeros_like(x)
    return jnp.concatenate([jnp.where(masks[0], x, zero), jnp.where(masks[1], x, zero)], axis=0)


def _deinterleave(src_ref, dst_ref, d, dtype):
    L = S // d
    for r in range(d):
        dst_ref[r * L:(r + 1) * L, :] = src_ref[pl.ds(r, L, stride=d), :].astype(dtype)


def _branch_loops(d, block):
    L = S // d
    for r in range(d):
        base = r * L
        block(base, base, BLK, True)
        for n in range(1, L // BLK):
            block(base + n * BLK, base + (n - 1) * BLK, 2 * BLK, False)


def attention_fwd(z3, bias3, name):
    W = 2 * DH
    CH = 256

    def body(q_ref, k_ref, v_ref, b_ref, o_ref, lse_ref, stage, qd, kd, vd, od, ld, on, ln):
        masks = _head_masks()
        for src, dst in ((q_ref, qd), (k_ref, kd), (v_ref, vd)):
            stage[...] = src[...].astype(F32)
            for gi, d in enumerate(BRANCH_DILATIONS[1:]):
                _deinterleave(stage, dst.at[gi], d, BF)

        for g, d in enumerate(BRANCH_DILATIONS):
            qs, ks, vs = (q_ref, k_ref, v_ref) if d == 1 else (qd.at[g - 1], kd.at[g - 1], vd.at[g - 1])
            o_dst, l_dst = (on.at[0], ln.at[0]) if d == 1 else (od, ld)

            def block(q0, k0, nk, first, g=g, qs=qs, ks=ks, vs=vs, o_dst=o_dst, l_dst=l_dst):
                q2 = _stack_heads(qs[pl.ds(q0, BLK), :], masks)
                kk = ks[pl.ds(k0, nk), :]
                vv = vs[pl.ds(k0, nk), :]
                bias = b_ref[g][:, BLK:] if first else b_ref[g]
                s = lax.dot_general(q2, kk, _NT, preferred_element_type=F32) * SCALE + bias
                mx = jnp.max(s, axis=1, keepdims=True)
                p = jnp.exp(s - mx)
                l = jnp.sum(p, axis=1, keepdims=True)
                o2 = jnp.dot(p.astype(BF), vv, preferred_element_type=F32) / l
                lse2 = mx + jnp.log(l)
                o_dst[pl.ds(q0, BLK), :] = jnp.where(masks[0], o2[:BLK], o2[BLK:])
                l_dst[pl.ds(q0, BLK), :] = jnp.where(masks[0], lse2[:BLK], lse2[BLK:])

            _branch_loops(d, block)
            if d > 1:
                L = S // d
                for r in range(d):
                    on[g, pl.ds(r, L, stride=d), :] = od[r * L:(r + 1) * L, :]
                    ln[g, pl.ds(r, L, stride=d), :] = ld[r * L:(r + 1) * L, :]

        def join(c, carry):
            rows = pl.ds(pl.multiple_of(c * CH, CH), CH)
            a, b, cc = ln[0, rows, :], ln[1, rows, :], ln[2, rows, :]
            mx = jnp.maximum(jnp.maximum(a, b), cc)
            ea, eb, ec = jnp.exp(a - mx), jnp.exp(b - mx), jnp.exp(cc - mx)
            tot = ea + eb + ec
            o_ref[rows, :] = ((ea * on[0, rows, :] + eb * on[1, rows, :] + ec * on[2, rows, :])
                              / tot).astype(BF)
            lse_ref[rows, :] = mx + jnp.log(tot)
            return carry
        lax.fori_loop(0, S // CH, join, 0)

    col = pl.BlockSpec((S, W), lambda hp: (0, hp))
    return pl.pallas_call(
        body, name=name,
        grid=(D // W,),
        in_specs=[pl.BlockSpec((None, S, W), lambda hp: (0, 0, hp)),
                  pl.BlockSpec((None, S, W), lambda hp: (1, 0, hp)),
                  pl.BlockSpec((None, S, W), lambda hp: (2, 0, hp)),
                  pl.BlockSpec((3, 2 * BLK, 2 * BLK), lambda hp: (0, hp, 0))],
        out_specs=[col, col],
        out_shape=[jax.ShapeDtypeStruct((S, D), BF), jax.ShapeDtypeStruct((S, D), F32)],
        scratch_shapes=[pltpu.VMEM((S, W), F32),
                        pltpu.VMEM((2, S, W), BF), pltpu.VMEM((2, S, W), BF), pltpu.VMEM((2, S, W), BF),
                        pltpu.VMEM((S, W), F32), pltpu.VMEM((S, W), F32),
                        pltpu.VMEM((3, S, W), F32), pltpu.VMEM((3, S, W), F32)],
        compiler_params=_cparams(("parallel",)),
    )(z3, z3, z3, bias3)


def attention_bwd(z3, dob, ob, lse_b, bias3, name, deps=()):
    W = 2 * DH
    CH = 256

    def body(q_ref, k_ref, v_ref, do_ref, o_ref, lse_ref, b_ref, dz_ref, dsum_ref,
             stage, delta, qd, kd, vd, dod, lsd, dld, res, acc, db_ref):
        masks = _head_masks()

        def rowsum(c, carry):
            rows = pl.ds(pl.multiple_of(c * CH, CH), CH)
            prod = do_ref[rows, :].astype(F32) * o_ref[rows, :].astype(F32)
            sa = jnp.sum(jnp.where(masks[0], prod, 0.0), axis=1, keepdims=True)
            sb = jnp.sum(jnp.where(masks[1], prod, 0.0), axis=1, keepdims=True)
            delta[rows, :] = jnp.where(masks[0], sa, sb)
            return carry
        lax.fori_loop(0, S // CH, rowsum, 0)

        for src, dst in ((q_ref, qd), (k_ref, kd), (v_ref, vd), (do_ref, dod)):
            stage[...] = src[...].astype(F32)
            for gi, d in enumerate(BRANCH_DILATIONS[1:]):
                _deinterleave(stage, dst.at[gi], d, BF)
        for gi, d in enumerate(BRANCH_DILATIONS[1:]):
            _deinterleave(lse_ref, lsd.at[gi], d, F32)
            _deinterleave(delta, dld.at[gi], d, F32)

        db_ref[...] = jnp.zeros_like(db_ref)
        for g, d in enumerate(BRANCH_DILATIONS):
            if d == 1:
                qs, ks, vs, dos, ls, dl = q_ref, k_ref, v_ref, do_ref, lse_ref, delta
            else:
                qs, ks, vs, dos = qd.at[g - 1], kd.at[g - 1], vd.at[g - 1], dod.at[g - 1]
                ls, dl = lsd.at[g - 1], dld.at[g - 1]
            res[1] = jnp.zeros((S, W), F32)
            res[2] = jnp.zeros((S, W), F32)

            def block(q0, k0, nk, first, g=g, qs=qs, ks=ks, vs=vs, dos=dos, ls=ls, dl=dl):
                kk = ks[pl.ds(k0, nk), :]
                vv = vs[pl.ds(k0, nk), :]
                q2 = _stack_heads(qs[pl.ds(q0, BLK), :], masks)
                do2 = _stack_heads(dos[pl.ds(q0, BLK), :], masks)
                lse_blk = ls[pl.ds(q0, BLK), :]
                del_blk = dl[pl.ds(q0, BLK), :]
                lse2 = jnp.concatenate([lse_blk[:, 0:1], lse_blk[:, DH:DH + 1]], axis=0)
                del2 = jnp.concatenate([del_blk[:, 0:1], del_blk[:, DH:DH + 1]], axis=0)
                bias = b_ref[g][:, BLK:] if first else b_ref[g]
                s = lax.dot_general(q2, kk, _NT, preferred_element_type=F32) * SCALE + bias
                p = jnp.exp(s - lse2)
                dp = lax.dot_general(do2, vv, _NT, preferred_element_type=F32)
                ds = p * (dp - del2)
                if first:
                    db_ref[g, :, BLK:] += ds
                else:
                    db_ref[g] += ds
                dsb = ds.astype(BF)
                dq2 = jnp.dot(dsb, kk, preferred_element_type=F32) * SCALE
                res[0, pl.ds(q0, BLK), :] = jnp.where(masks[0], dq2[:BLK], dq2[BLK:])
                res[1, pl.ds(k0, nk), :] += lax.dot_general(dsb, q2, _TN,
                                                            preferred_element_type=F32) * SCALE
                res[2, pl.ds(k0, nk), :] += lax.dot_general(p.astype(BF), do2, _TN,
                                                            preferred_element_type=F32)

            _branch_loops(d, block)
            L = S // d
            for t in range(3):
                if d == 1:
                    acc[t] = res[t]
                else:
                    for r in range(d):
                        acc[t, pl.ds(r, L, stride=d), :] = (acc[t, pl.ds(r, L, stride=d), :]
                                                            + res[t, r * L:(r + 1) * L, :])
        for t in range(3):
            dz_ref[t] = acc[t].astype(BF)

        flip = (lax.broadcasted_iota(jnp.int32, (BLK, BLK), 0)
                + lax.broadcasted_iota(jnp.int32, (BLK, BLK), 1) == BLK - 1).astype(BF)
        dsum_ref[...] = jnp.zeros_like(dsum_ref)
        for g in range(3):
            for hh in range(2):
                halves = []
                for half in range(2):
                    tile = db_ref[g, hh * BLK:(hh + 1) * BLK, half * BLK:(half + 1) * BLK]
                    hi = tile.astype(BF)
                    lo = (tile - hi.astype(F32)).astype(BF)
                    rev = (jnp.dot(hi, flip, preferred_element_type=F32)
                           + jnp.dot(lo, flip, preferred_element_type=F32))
                    skew = pltpu.roll(rev, 0, 1, stride=1, stride_axis=0)
                    halves.append(jnp.sum(skew, axis=0, keepdims=True))
                dsum_ref[2 * g + hh:2 * g + hh + 1, :] = jnp.concatenate(halves, axis=1)

    col = pl.BlockSpec((S, W), lambda hp: (0, hp))
    return pl.pallas_call(
        _after(body, 7, deps), name=name,
        grid=(D // W,),
        in_specs=[pl.BlockSpec((None, S, W), lambda hp: (0, 0, hp)),
                  pl.BlockSpec((None, S, W), lambda hp: (1, 0, hp)),
                  pl.BlockSpec((None, S, W), lambda hp: (2, 0, hp)),
                  col, col, col,
                  pl.BlockSpec((3, 2 * BLK, 2 * BLK), lambda hp: (0, hp, 0))] + [ANY] * len(deps),
        out_specs=[pl.BlockSpec((3, S, W), lambda hp: (0, 0, hp)),
                   pl.BlockSpec((None, 8, 2 * BLK), lambda hp: (hp, 0, 0))],
        out_shape=[jax.ShapeDtypeStruct((3, S, D), BF),
                   jax.ShapeDtypeStruct((D // W, 8, 2 * BLK), F32)],
        scratch_shapes=[pltpu.VMEM((S, W), F32), pltpu.VMEM((S, W), F32),
                        pltpu.VMEM((2, S, W), BF), pltpu.VMEM((2, S, W), BF),
                        pltpu.VMEM((2, S, W), BF), pltpu.VMEM((2, S, W), BF),
                        pltpu.VMEM((2, S, W), F32), pltpu.VMEM((2, S, W), F32),
                        pltpu.VMEM((3, S, W), F32), pltpu.VMEM((3, S, W), F32),
                        pltpu.VMEM((3, 2 * BLK, 2 * BLK), F32)],
        compiler_params=_cparams(("parallel",)),
    )(z3, z3, z3, dob, ob, lse_b, bias3, *deps)


def _me():
    return lax.axis_index("x"), lax.axis_index("y"), lax.axis_index("c")


def _other_chips(x, y):
    return [(1 - x, y), (x, 1 - y), (1 - x, 1 - y)]


def _shard_window(ref, axis, t, shape):
    R, C = shape
    if axis == 0:
        return ref.at[pl.ds(pl.multiple_of(t * R, 128), R), :]
    return ref.at[:, pl.ds(pl.multiple_of(t * C, 128), C)]


def all_gather_weights(shards, axes, name):
    n = len(shards)
    shapes = [s.shape for s in shards]
    outs_shape = [jax.ShapeDtypeStruct((8 * R, C) if ax == 0 else (R, 8 * C), BF)
                  for (R, C), ax in zip(shapes, axes)]

    def body(*refs):
        ins, outs = refs[:n], refs[n:2 * n]
        send_sems, recv_sems, local_sems = refs[2 * n:]
        x, y, c = _me()
        me, sibling = (x, y, c), (x, y, 1 - c)
        xnb, ynb, diag = (1 - x, y), (x, 1 - y), (1 - x, 1 - y)
        south = c == 0
        relay_from = (jnp.where(south, x, 1 - x), jnp.where(south, 1 - y, y))
        relay_to = (jnp.where(south, 1 - x, x), jnp.where(south, y, 1 - y))
        barrier = pltpu.get_barrier_semaphore()
        for peer in [sibling, (*xnb, c), (*ynb, c)]:
            pl.semaphore_signal(barrier, inc=1, device_id=peer, device_id_type=MESH)
        pl.semaphore_wait(barrier, 3)

        def win(i, px, py, pc):
            return _shard_window(outs[i], axes[i], 4 * px + 2 * py + pc, shapes[i])

        def copy(i, k, block, to, src=None):
            return pltpu.make_async_remote_copy(
                src_ref=win(i, *block) if src is None else src, dst_ref=win(i, *block),
                send_sem=send_sems.at[i * 7 + k], recv_sem=recv_sems.at[i * 7 + k],
                device_id=to, device_id_type=MESH)

        mine = [pltpu.make_async_copy(ins[i], win(i, *me), local_sems.at[i]) for i in range(n)]
        for cp in mine:
            cp.start()
        sent = []
        for i in range(n):
            sent += [copy(i, 0, me, sibling, src=ins[i]), copy(i, 1, me, (*xnb, c), src=ins[i]),
                     copy(i, 2, me, (*ynb, c), src=ins[i])]
        for cp in sent:
            cp.start()
        for i in range(n):
            for k, chip in ((1, xnb), (2, ynb)):
                copy(i, k, (*chip, c), me).wait_recv()
                sent.append(copy(i, 3 + k, (*chip, c), sibling))
                sent[-1].start()
            sent.append(copy(i, 3, (*relay_from, c), (*relay_to, c)))
            sent[-1].start()
        for i in range(n):
            copy(i, 3, (*diag, c), me).wait_recv()
            sent.append(copy(i, 6, (*diag, c), sibling))
            sent[-1].start()
        for i in range(n):
            copy(i, 0, sibling, me).wait_recv()
            for k, chip in ((4, xnb), (5, ynb), (6, diag)):
                copy(i, k, (*chip, 1 - c), me).wait_recv()
        for cp in sent:
            cp.wait_send()
        for cp in mine:
            cp.wait()

    return pl.kernel(
        body, out_type=outs_shape, name=name,
        mesh=plsc.ScalarSubcoreMesh(axis_name="sequencer", num_cores=1),
        scratch_types=[pltpu.SemaphoreType.DMA((7 * n,)), pltpu.SemaphoreType.DMA((7 * n,)),
                       pltpu.SemaphoreType.DMA((n,))],
        compiler_params=pltpu.CompilerParams(collective_id=1),
    )(*shards)


def pair_exchange_grads(grads, axes, shapes, name):
    n = len(grads)

    def body(*refs):
        ins, outs = refs[:n], refs[n:2 * n]
        send_sems, recv_sems = refs[2 * n:]
        x, y, c = _me()
        sibling = (x, y, 1 - c)
        barrier = pltpu.get_barrier_semaphore()
        pl.semaphore_signal(barrier, inc=1, device_id=sibling, device_id_type=MESH)
        pl.semaphore_wait(barrier, 1)
        copies = []
        for i in range(n):
            for q in range(4):
                t = 2 * q + (1 - c)
                copies.append(pltpu.make_async_remote_copy(
                    src_ref=_shard_window(ins[i], axes[i], t, shapes[i]), dst_ref=outs[i].at[q],
                    send_sem=send_sems.at[i * 4 + q], recv_sem=recv_sems.at[i * 4 + q],
                    device_id=sibling, device_id_type=MESH))
        for cp in copies:
            cp.start()
        for cp in copies:
            cp.wait_recv()
        for cp in copies:
            cp.wait_send()

    return pl.kernel(
        body, out_type=[jax.ShapeDtypeStruct((4,) + tuple(sh), BF) for sh in shapes], name=name,
        mesh=plsc.ScalarSubcoreMesh(axis_name="sequencer", num_cores=1),
        scratch_types=[pltpu.SemaphoreType.DMA((4 * n,)), pltpu.SemaphoreType.DMA((4 * n,))],
        compiler_params=pltpu.CompilerParams(collective_id=2),
    )(*grads)


def pair_add(grads, landed, axes, shapes, c_idx, name, deps=()):
    n = len(grads)

    def body(c_ref, *refs):
        for t in range(n):
            refs[2 * n + t][...] = (refs[2 * t][...].astype(F32)
                                    + refs[2 * t + 1][...].astype(F32)).astype(BF)

    halves = 1
    in_specs, out_specs, out_shapes, operands = [], [], [], []
    for t in range(n):
        R, C = shapes[t]
        rh = R // halves
        if axes[t] == 0:
            in_specs.append(pl.BlockSpec(
                (rh, C), lambda q, h, c_ref: (halves * (2 * q + c_ref[0]) + h, 0)))
        else:
            in_specs.append(pl.BlockSpec((rh, C), lambda q, h, c_ref: (h, 2 * q + c_ref[0])))
        blk = pl.BlockSpec((None, rh, C), lambda q, h, c_ref: (q, h, 0))
        in_specs.append(blk)
        out_specs.append(blk)
        out_shapes.append(jax.ShapeDtypeStruct((4, R, C), BF))
        operands += [grads[t], landed[t]]
    return pl.pallas_call(
        _after(body, 1 + 2 * n, deps), name=name,
        grid_spec=pltpu.PrefetchScalarGridSpec(
            num_scalar_prefetch=1, grid=(4, halves), in_specs=in_specs + [ANY] * len(deps),
            out_specs=out_specs),
        out_shape=out_shapes,
        compiler_params=_cparams(("parallel", "parallel")),
    )(c_idx, *operands, *deps)


def chip_exchange_grads(parts, name):
    n = len(parts)

    def body(*refs):
        ins, outs, relay = refs[:n], refs[n:2 * n], refs[2 * n:3 * n]
        send_sems, recv_sems = refs[3 * n:]
        x, y, c = _me()
        xnb, ynb, diag = (1 - x, y), (x, 1 - y), (1 - x, 1 - y)
        south = c == 0
        via = (jnp.where(south, 1 - x, x), jnp.where(south, y, 1 - y))
        onward = (jnp.where(south, x, 1 - x), jnp.where(south, 1 - y, y))
        barrier = pltpu.get_barrier_semaphore()
        for peer in (xnb, ynb):
            pl.semaphore_signal(barrier, inc=1, device_id=(*peer, c), device_id_type=MESH)
        pl.semaphore_wait(barrier, 2)

        def copy(i, k, src, dst, to):
            return pltpu.make_async_remote_copy(
                src_ref=src, dst_ref=dst, send_sem=send_sems.at[i * 4 + k],
                recv_sem=recv_sems.at[i * 4 + k], device_id=(*to, c), device_id_type=MESH)

        sent = []
        for i in range(n):
            sent += [copy(i, 0, ins[i].at[2 * xnb[0] + xnb[1]], outs[i].at[0], xnb),
                     copy(i, 1, ins[i].at[2 * ynb[0] + ynb[1]], outs[i].at[1], ynb),
                     copy(i, 2, ins[i].at[2 * diag[0] + diag[1]], relay[i], via)]
        for cp in sent:
            cp.start()
        for i in range(n):
            copy(i, 2, relay[i], relay[i], via).wait_recv()
            sent.append(copy(i, 3, relay[i], outs[i].at[2], onward))
            sent[-1].start()
        for i in range(n):
            copy(i, 0, outs[i].at[0], outs[i].at[0], xnb).wait_recv()
            copy(i, 1, outs[i].at[1], outs[i].at[1], ynb).wait_recv()
            copy(i, 3, outs[i].at[2], outs[i].at[2], onward).wait_recv()
        for cp in sent:
            cp.wait_send()

    landing = [jax.ShapeDtypeStruct((3,) + tuple(p.shape[1:]), BF) for p in parts]
    staging = [jax.ShapeDtypeStruct(tuple(p.shape[1:]), BF) for p in parts]
    return pl.kernel(
        body, out_type=landing + staging, name=name,
        mesh=plsc.ScalarSubcoreMesh(axis_name="sequencer", num_cores=1),
        scratch_types=[pltpu.SemaphoreType.DMA((4 * n,)), pltpu.SemaphoreType.DMA((4 * n,))],
        compiler_params=pltpu.CompilerParams(collective_id=3),
    )(*parts)[:n]


def all_gather_small(v, name):
    R, C = v.shape

    def body(v_ref, out_ref, send_sems, recv_sems, local_sem):
        x, y, c = _me()
        me, sibling = (x, y, c), (x, y, 1 - c)
        chips = _other_chips(x, y)

        def slot(px, py, pc):
            return out_ref.at[4 * px + 2 * py + pc]

        def copy(k, block, to, src=None):
            return pltpu.make_async_remote_copy(
                src_ref=slot(*block) if src is None else src, dst_ref=slot(*block),
                send_sem=send_sems.at[k], recv_sem=recv_sems.at[k],
                device_id=to, device_id_type=MESH)

        mine = pltpu.make_async_copy(v_ref, slot(*me), local_sem)
        mine.start()
        first = [copy(0, me, sibling, src=v_ref)]
        first += [copy(1 + j, me, (*chip, c), src=v_ref) for j, chip in enumerate(chips)]
        for cp in first:
            cp.start()
        passed = [copy(4 + j, (*chip, c), sibling) for j, chip in enumerate(chips)]
        for j, chip in enumerate(chips):
            copy(1 + j, (*chip, c), me).wait_recv()
            passed[j].start()
        copy(0, sibling, me).wait_recv()
        for j, chip in enumerate(chips):
            copy(4 + j, (*chip, 1 - c), me).wait_recv()
        for cp in first + passed:
            cp.wait_send()
        mine.wait()

    return pl.pallas_call(
        body, name=name,
        in_specs=[pl.BlockSpec(memory_space=pltpu.VMEM)],
        out_specs=pl.BlockSpec(memory_space=pltpu.VMEM),
        out_shape=jax.ShapeDtypeStruct((NDEV, R, C), F32),
        scratch_shapes=[pltpu.SemaphoreType.DMA((7,)), pltpu.SemaphoreType.DMA((7,)),
                        pltpu.SemaphoreType.DMA],
    )(v)


def _adamw(w, g, m, v):
    m = ADAM_B1 * m + (1.0 - ADAM_B1) * g
    v = ADAM_B2 * v + (1.0 - ADAM_B2) * (g * g)
    m_hat = m / (1.0 - ADAM_B1 ** ADAM_STEP)
    v_hat = v / (1.0 - ADAM_B2 ** ADAM_STEP)
    delta = -ADAM_LR * (m_hat / (jnp.sqrt(v_hat) + ADAM_EPS) + ADAM_WD * w)
    return delta, m, v


def reduce_adamw(parts, landed, params, q_idx, name, prevs, deps=()):
    n = len(parts)
    halves = 2
    in_specs, out_specs, out_shapes, operands, extra, aliases = [], [], [], [], [], {}
    for t in range(n):
        R, C = parts[t].shape[1:]
        w, m, v, layer = params[t]
        r, c = w.shape[1:]
        tr = r // halves
        assert tr % 16 == 0 and c == C
        wspec = pl.BlockSpec((None, tr, c), lambda i, q_ref, layer=layer: (layer, i, 0))
        in_specs += [pl.BlockSpec((None, tr, C), lambda i, q_ref: (q_ref[0], i, 0)),
                     pl.BlockSpec((3, tr, C), lambda i, q_ref: (0, i, 0)), wspec, wspec, wspec]
        out_specs += [wspec] * 4
        out_shapes += [jax.ShapeDtypeStruct(w.shape, F32)] * 4
        operands += [parts[t], landed[t], w, m, v]
        for k, buf in enumerate(prevs[t]):
            aliases[1 + 5 * n + len(extra)] = 4 * t + k
            extra.append(buf)
    extra += list(deps)

    def body(q_ref, *refs):
        for t in range(n):
            p_ref, l_ref, w_ref, m_ref, v_ref = refs[5 * t:5 * t + 5]
            g = p_ref[...].astype(F32)
            for k in range(3):
                g = g + l_ref[k].astype(F32)
            d, mm, vv = _adamw(w_ref[...], g, m_ref[...], v_ref[...])
            outs = refs[5 * n + 4 * t:5 * n + 4 * t + 4]
            outs[0][...] = g
            outs[1][...] = d
            outs[2][...] = mm
            outs[3][...] = vv

    res = pl.pallas_call(
        _after(body, 1 + 5 * n, extra), name=name,
        grid_spec=pltpu.PrefetchScalarGridSpec(
            num_scalar_prefetch=1, grid=(halves,),
            in_specs=in_specs + [ANY] * len(extra), out_specs=out_specs),
        out_shape=out_shapes,
        input_output_aliases=aliases,
        compiler_params=_cparams(("parallel",)),
    )(q_idx, *operands, *extra)
    return [res[4 * t:4 * t + 4] for t in range(n)]


def small_reduce_adamw(gathered, w, m, v, name):
    R, C = w.shape

    def body(a_ref, w_ref, m_ref, v_ref, g_out, d_out, m_out, v_out):
        g = a_ref[0]
        for k in range(1, NDEV):
            g = g + a_ref[k]
        d, mm, vv = _adamw(w_ref[...], g, m_ref[...], v_ref[...])
        g_out[...] = g
        d_out[...] = d
        m_out[...] = mm
        v_out[...] = vv

    out = jax.ShapeDtypeStruct((R, C), F32)
    return pl.pallas_call(body, name=name, out_shape=[out] * 4,
                          compiler_params=_cparams())(gathered, w, m, v)


def _pad_cols(a, n):
    return jnp.pad(a, ((0, 0), (0, n - a.shape[1])))


def _pad_rows(a, n):
    return jnp.pad(a, ((0, n - a.shape[0]), (0, 0)))


SMALL_ROWS = 16


def _pack_small(mix, ffn, fin, taps_full, relb):
    return jnp.concatenate([
        mix, ffn, fin.reshape(1, D), taps_full.reshape(6, D),
        jnp.pad(relb.reshape(1, NUM_BUCKETS * H), ((0, 0), (0, D - NUM_BUCKETS * H)))], axis=0)


def kernel(x, mix_norm, ffn_norm, final_norm, conv_w_in, conv_kernel, conv_w_out, attn_w_qkv, attn_w_out, rel_bias, ffn_w_gate, ffn_w_up, ffn_w_down, loss_target, m_mix_norm, m_ffn_norm, m_final_norm, m_conv_w_in, m_conv_kernel, m_conv_w_out, m_attn_w_qkv, m_attn_w_out, m_rel_bias, m_ffn_w_gate, m_ffn_w_up, m_ffn_w_down, v_mix_norm, v_ffn_norm, v_final_norm, v_conv_w_in, v_conv_kernel, v_conv_w_out, v_attn_w_qkv, v_attn_w_out, v_rel_bias, v_ffn_w_gate, v_ffn_w_up, v_ffn_w_down):
    xi, yi, ci = _me()
    me = 4 * xi + 2 * yi + ci
    c_idx = jnp.reshape(ci, (1,)).astype(jnp.int32)
    q_idx = jnp.reshape(2 * xi + yi, (1,)).astype(jnp.int32)
    col0 = me * (D // NDEV)

    taps_local = jnp.zeros((2, 3, D), F32)
    taps_local = lax.dynamic_update_slice(taps_local, conv_kernel, (0, 0, col0))
    taps_pack = jnp.pad(taps_local.reshape(6, D), ((0, 2), (0, 0)))
    taps_all = all_gather_small(taps_pack, "ag_taps")
    taps_sum = jnp.sum(taps_all, axis=0)
    taps = [jnp.pad(taps_sum[3 * j:3 * j + 3], ((0, 5), (0, 0))) for j in range(2)]

    gate_t, up_t = jnp.swapaxes(ffn_w_gate, 1, 2), jnp.swapaxes(ffn_w_up, 1, 2)
    m_gate_t, m_up_t = jnp.swapaxes(m_ffn_w_gate, 1, 2), jnp.swapaxes(m_ffn_w_up, 1, 2)
    v_gate_t, v_up_t = jnp.swapaxes(v_ffn_w_gate, 1, 2), jnp.swapaxes(v_ffn_w_up, 1, 2)

    mixer_in = (conv_w_in, attn_w_qkv)
    mixer_out = (conv_w_out, attn_w_out)
    wts = []
    for i in range(DEPTH):
        j = i // 2
        shards = [mixer_in[i % 2][j].astype(BF), mixer_out[i % 2][j].astype(BF),
                  _pad_rows(gate_t[i].astype(BF), FF_SHARD_PAD),
                  _pad_rows(up_t[i].astype(BF), FF_SHARD_PAD),
                  _pad_rows(ffn_w_down[i].astype(BF), FF_SHARD_PAD)]
        axes = (1, 0, 0, 0, 0)
        groups = ((0, 1), (1, 2), (2, 4), (4, 5)) if i == 0 else ((0, 2), (2, 5))
        layer = []
        for lo, hi in groups:
            layer += list(all_gather_weights(shards[lo:hi], axes[lo:hi], f"ag_l{i}_{lo}"))
        wts.append(layer)

    onehot_t, band = _bucket_onehot_t()
    bias3 = bias_tables(rel_bias.T, onehot_t, band, "bias_tables").reshape(3, H * BLK, 2 * BLK)

    saved = []
    xc = x[0]
    for i in range(DEPTH):
        w_in, w_out, w_g, w_u, w_d = wts[i]
        j = i // 2
        x_mix = xc
        z3, h_mix = norm_matmul3(xc, mix_norm[i:i + 1], w_in, f"mix_in_l{i}")
        if i % 2 == 0:
            act = conv_fwd(z3, taps[j], f"conv_fwd_l{i}")
            lse_b = None
        else:
            act, lse_b = attention_fwd(z3, bias3, f"attn_fwd_l{i}")
        xc = matmul_residual(act, w_out, xc, f"mix_out_l{i}")
        x_ffn = xc
        g, u, a, h_ffn = norm_swiglu_up(xc, ffn_norm[i:i + 1], w_g, w_u, f"ffn_up_l{i}")
        xc = matmul_residual(a, w_d, xc, f"ffn_down_l{i}")
        saved.append((x_mix, h_mix, z3, act, lse_b, x_ffn, h_ffn, g, u, a))

    dx, dxb, dg_final, sq = loss_head(xc, final_norm.reshape(1, D), loss_target[0], "loss_head")
    loss = lax.psum(0.5 * jnp.sum(sq[0]) / D, ("x", "y", "c"))

    dg_mix = [None] * DEPTH
    dg_ffn = [None] * DEPTH
    dtaps = [None, None]
    dbias_all = []
    shape_in, shape_out = (D, 3 * D // NDEV), (D // NDEV, D)
    ffn_axes, ffn_shapes = (0, 0, 0), ((FF_SHARD_PAD, D),) * 3
    stacked = {}

    def pair_stage(grads, landed1, axes, shapes, tag, tok):
        parts = pair_add(grads, landed1, axes, shapes, c_idx, f"rs_add_{tag}", deps=[tok])
        return parts, chip_exchange_grads(parts, f"rs_chip_{tag}"), parts[-1]

    def adamw_stage(parts, landed2, params, tag, tok):
        names = [p[0] for p in params]
        res = reduce_adamw(parts, landed2, [p[1:] for p in params], q_idx, f"adamw_{tag}",
                           [stacked.get(nm, ()) for nm in names], deps=[tok])
        for nm, r4 in zip(names, res):
            stacked[nm] = r4
        return res[-1][0]

    tok = dxb
    mix_wait = None
    mix_chip = None
    ffn_chip = None
    for i in reversed(range(DEPTH)):
        w_in, w_out, w_g, w_u, w_d = wts[i]
        j = i // 2
        x_mix, h_mix, z3, act, lse_b, x_ffn, h_ffn, g, u, a = saved[i]
        ffn_params = [("ffn_w_gate", gate_t, m_gate_t, v_gate_t, i),
                      ("ffn_w_up", up_t, m_up_t, v_up_t, i),
                      ("ffn_w_down", ffn_w_down, m_ffn_w_down, v_ffn_w_down, i)]
        if i % 2 == 0:
            mix_params = [("conv_w_in", conv_w_in, m_conv_w_in, v_conv_w_in, j),
                          ("conv_w_out", conv_w_out, m_conv_w_out, v_conv_w_out, j)]
        else:
            mix_params = [("attn_w_qkv", attn_w_qkv, m_attn_w_qkv, v_attn_w_qkv, j),
                          ("attn_w_out", attn_w_out, m_attn_w_out, v_attn_w_out, j)]
        dgate, dup = swiglu_bwd_da(dxb, w_d, g, u, f"ffn_da_l{i}", deps=[tok])
        tok = dgate
        if mix_wait is not None:
            grads_m, landed1_m, params_m, tag_m = mix_wait
            parts_m, landed2_m, tok = pair_stage(grads_m, landed1_m, (1, 0), (shape_in, shape_out),
                                                 tag_m, tok)
            mix_chip = (parts_m, landed2_m, params_m, tag_m)
            mix_wait = None
        grads_f = matmul_tn_group([(dgate, h_ffn), (dup, h_ffn), (a, dxb)], f"ffn_dw_l{i}",
                                  deps=[tok])
        landed1_f = pair_exchange_grads(grads_f, ffn_axes, ffn_shapes, f"rs_pair_f{i}")
        tok = grads_f[-1]
        if ffn_chip is not None:
            tok = adamw_stage(*ffn_chip, tok)
            ffn_chip = None
        dx, dxb, dg_ffn[i] = matmul_normbwd(
            [(dgate, w_g, False), (dup, w_u, False)], x_ffn, ffn_norm[i:i + 1], dx, f"ffn_dh_l{i}",
            deps=[tok])
        dxb_mix = dxb
        dact = matmul_nt(dxb, w_out, f"mix_dact_l{i}", out_dtype=F32 if i % 2 == 0 else BF)
        parts_f, landed2_f, tok = pair_stage(grads_f, landed1_f, ffn_axes, ffn_shapes, f"f{i}", dact)
        ffn_chip = (parts_f, landed2_f, ffn_params, f"f{i}")
        if i % 2 == 0:
            dz3, dtaps[j] = conv_bwd(dact, z3, taps[j], f"conv_bwd_l{i}", deps=[tok])
        else:
            dz3, dsum = attention_bwd(z3, dact, act, lse_b, bias3, f"attn_bwd_l{i}", deps=[tok])
            dbias_all.append(dsum[:, :6].reshape(H // 2, 3, 2, 2 * BLK).transpose(1, 0, 2, 3)
                             .reshape(3, H, 2 * BLK))
        grads_m = matmul_tn_group([(h_mix, dz3), (act, dxb_mix)], f"mix_dw_l{i}")
        landed1_m = pair_exchange_grads(grads_m, (1, 0), (shape_in, shape_out), f"rs_pair_m{i}")
        mix_wait = (grads_m, landed1_m, mix_params, f"m{i}")
        tok = grads_m[-1]
        if mix_chip is not None:
            tok = adamw_stage(*mix_chip, tok)
            mix_chip = None
        dx, dxb, dg_mix[i] = matmul_normbwd(
            [(dz3, w_in, True)], x_mix, mix_norm[i:i + 1], dx, f"mix_dh_l{i}", deps=[tok])
        tok = dxb
    grads_m, landed1_m, params_m, tag_m = mix_wait
    parts_m, landed2_m, tok = pair_stage(grads_m, landed1_m, (1, 0), (shape_in, shape_out), tag_m, tok)
    tok = adamw_stage(*ffn_chip, tok)

    grad_relb_t = bias_grad(jnp.concatenate(dbias_all), _diagonal_onehot_t(), "bias_grad")
    dtaps_full = jnp.stack([dtaps[0][:3], dtaps[1][:3]])
    g_small = _pack_small(jnp.concatenate([d[0:1] for d in dg_mix], axis=0),
                          jnp.concatenate([d[0:1] for d in dg_ffn], axis=0),
                          dg_final[0], dtaps_full, grad_relb_t.T)
    gathered = all_gather_small(g_small, "ag_small_grads")

    def taps_at_cols(k):
        return lax.dynamic_update_slice(jnp.zeros((2, 3, D), F32), k, (0, 0, col0))

    w_small = _pack_small(mix_norm, ffn_norm, final_norm, taps_at_cols(conv_kernel), rel_bias)
    m_small = _pack_small(m_mix_norm, m_ffn_norm, m_final_norm, taps_at_cols(m_conv_kernel), m_rel_bias)
    v_small = _pack_small(v_mix_norm, v_ffn_norm, v_final_norm, taps_at_cols(v_conv_kernel), v_rel_bias)
    small = small_reduce_adamw(gathered, w_small, m_small, v_small, "adamw_small")

    def unpack_small(p):
        taps_p = lax.dynamic_slice(p[9:15].reshape(2, 3, D), (0, 0, col0), (2, 3, D // NDEV))
        return {"mix_norm": p[0:4], "ffn_norm": p[4:8], "final_norm": p[8],
                "conv_kernel": taps_p, "rel_bias": p[15, :NUM_BUCKETS * H].reshape(NUM_BUCKETS, H)}

    small_out = [unpack_small(p) for p in small]
    adamw_stage(parts_m, landed2_m, params_m, tag_m, small[0])

    names = ["mix_norm", "ffn_norm", "final_norm", "conv_w_in", "conv_kernel", "conv_w_out",
             "attn_w_qkv", "attn_w_out", "rel_bias", "ffn_w_gate", "ffn_w_up", "ffn_w_down"]
    outs = [loss, dx.reshape(1, S, D)]
    for o in range(4):
        for nme in names:
            if nme in ("ffn_w_gate", "ffn_w_up"):
                outs.append(jnp.swapaxes(stacked[nme][o], 1, 2))
            else:
                outs.append(stacked[nme][o] if nme in stacked else small_out[o][nme])
    return tuple(outs)
```

```python
import math

import numpy as np
import jax
import jax.numpy as jnp
from jax import lax
from jax.experimental import pallas as pl
from jax.experimental.pallas import tpu as pltpu
from jax.experimental.pallas import tpu_sc as plsc

S = 2048
D = 1024
H = 16
DH = 64
DFF = 2816
NDEV = 8
DEPTH = 4
FF_SHARD = DFF // NDEV
FF_SHARD_PAD = 384
DFF_PAD = FF_SHARD_PAD * NDEV
BLK = 128
BRANCH_DILATIONS = (1, 4, 16)
NUM_BUCKETS = 32
MAX_DISTANCE = 2048
EPS = 1e-6
NEG_INF = -1e30
SCALE = DH ** -0.5

ADAM_LR = 0.001
ADAM_B1 = 0.9
ADAM_B2 = 0.999
ADAM_EPS = 1e-08
ADAM_WD = 0.01
ADAM_STEP = 10

BF = jnp.bfloat16
F32 = jnp.float32
VMEM_LIMIT_BYTES = 56 * 1024 * 1024
KSPLIT = 512
NORM_CHUNK = 256
MESH = pl.DeviceIdType.MESH
ANY = pl.BlockSpec(memory_space=pl.ANY)

_NT = (((1,), (1,)), ((), ()))
_TN = (((0,), (0,)), ((), ()))


def _cparams(sem=None):
    return pltpu.CompilerParams(dimension_semantics=sem, vmem_limit_bytes=VMEM_LIMIT_BYTES)


def _after(body, n, deps):
    nd = len(deps)
    if nd == 0:
        return body

    def ordered(*refs):
        body(*refs[:n], *refs[n + nd:])
    return ordered


def _rms(x):
    return lax.rsqrt(jnp.mean(x * x, axis=-1, keepdims=True) + EPS)


def norm_matmul3(x, gain, w, name, tm=1024, tn=1024):
    per = D // tn

    def body(x_ref, g_ref, w_ref, z_ref, h_ref, hs_ref):
        @pl.when(pl.program_id(1) == 0)
        def _():
            for c in range(tm // NORM_CHUNK):
                rows = slice(c * NORM_CHUNK, (c + 1) * NORM_CHUNK)
                xv = x_ref[rows, :]
                hv = (xv * _rms(xv) * g_ref[...]).astype(BF)
                hs_ref[rows, :] = hv
                h_ref[rows, :] = hv
                z_ref[rows, :] = jnp.dot(hv, w_ref[...], preferred_element_type=F32).astype(BF)

        @pl.when(pl.program_id(1) > 0)
        def _():
            z_ref[...] = jnp.dot(hs_ref[...], w_ref[...], preferred_element_type=F32).astype(BF)

    return pl.pallas_call(
        body, name=name,
        grid=(S // tm, 3 * D // tn),
        in_specs=[pl.BlockSpec((tm, D), lambda i, j: (i, 0)),
                  pl.BlockSpec((1, D), lambda i, j: (0, 0)),
                  pl.BlockSpec((D, tn), lambda i, j: (0, j))],
        out_specs=[pl.BlockSpec((None, tm, tn), lambda i, j: (j // per, i, j % per)),
                   pl.BlockSpec((tm, D), lambda i, j: (i, 0))],
        out_shape=[jax.ShapeDtypeStruct((3, S, D), BF), jax.ShapeDtypeStruct((S, D), BF)],
        scratch_shapes=[pltpu.VMEM((tm, D), BF)],
        compiler_params=_cparams(("parallel", "arbitrary")),
    )(x, gain, w)


def norm_swiglu_up(x, gain, wg_t, wu_t, name, tm=1024, tn=768):
    def body(x_ref, g_ref, wg_ref, wu_ref, go_ref, uo_ref, ao_ref, h_ref, hs_ref):
        def gate_up(hv, rows):
            g = lax.dot_general(hv, wg_ref[...], _NT, preferred_element_type=F32)
            u = lax.dot_general(hv, wu_ref[...], _NT, preferred_element_type=F32)
            go_ref[rows, :] = g.astype(BF)
            uo_ref[rows, :] = u.astype(BF)
            ao_ref[rows, :] = (g * jax.nn.sigmoid(g) * u).astype(BF)

        @pl.when(pl.program_id(1) == 0)
        def _():
            for c in range(tm // NORM_CHUNK):
                rows = slice(c * NORM_CHUNK, (c + 1) * NORM_CHUNK)
                xv = x_ref[rows, :]
                hv = (xv * _rms(xv) * g_ref[...]).astype(BF)
                hs_ref[rows, :] = hv
                h_ref[rows, :] = hv
                gate_up(hv, rows)

        @pl.when(pl.program_id(1) > 0)
        def _():
            gate_up(hs_ref[...], slice(None))

    act = jax.ShapeDtypeStruct((S, DFF_PAD), BF)
    blk = pl.BlockSpec((tm, tn), lambda i, j: (i, j))
    return pl.pallas_call(
        body, name=name,
        grid=(S // tm, DFF_PAD // tn),
        in_specs=[pl.BlockSpec((tm, D), lambda i, j: (i, 0)),
                  pl.BlockSpec((1, D), lambda i, j: (0, 0)),
                  pl.BlockSpec((tn, D), lambda i, j: (j, 0)),
                  pl.BlockSpec((tn, D), lambda i, j: (j, 0))],
        out_specs=[blk, blk, blk, pl.BlockSpec((tm, D), lambda i, j: (i, 0))],
        out_shape=[act, act, act, jax.ShapeDtypeStruct((S, D), BF)],
        scratch_shapes=[pltpu.VMEM((tm, D), BF)],
        compiler_params=_cparams(("parallel", "arbitrary")),
    )(x, gain, wg_t, wu_t)


def matmul_residual(a, w, x, name, tm=1024):
    K = a.shape[1]
    tn = D if K <= D else D // 2
    ns = K // KSPLIT
    kc = K // ns

    def body(*refs):
        x_ref, o_ref = refs[2 * ns:]
        acc = x_ref[...]
        for s in range(ns):
            acc = acc + jnp.dot(refs[s][...], refs[ns + s][...], preferred_element_type=F32)
        o_ref[...] = acc

    return pl.pallas_call(
        body, name=name,
        grid=(S // tm, D // tn),
        in_specs=[pl.BlockSpec((tm, kc), lambda i, j, s=s: (i, s)) for s in range(ns)]
        + [pl.BlockSpec((kc, tn), lambda i, j, s=s: (s, j)) for s in range(ns)]
        + [pl.BlockSpec((tm, tn), lambda i, j: (i, j))],
        out_specs=pl.BlockSpec((tm, tn), lambda i, j: (i, j)),
        out_shape=jax.ShapeDtypeStruct((S, D), F32),
        compiler_params=_cparams(("parallel", "parallel")),
    )(*([a] * ns), *([w] * ns), x)


def matmul_nt(a, w, name, out_dtype=BF, tm=1024, tn=1024, deps=()):
    K = a.shape[1]
    N = w.shape[0]

    def body(a_ref, w_ref, o_ref):
        o_ref[...] = lax.dot_general(a_ref[...], w_ref[...], _NT,
                                     preferred_element_type=F32).astype(o_ref.dtype)

    return pl.pallas_call(
        _after(body, 2, deps), name=name,
        grid=(S // tm, N // tn),
        in_specs=[pl.BlockSpec((tm, K), lambda i, j: (i, 0)),
                  pl.BlockSpec((tn, K), lambda i, j: (j, 0))] + [ANY] * len(deps),
        out_specs=pl.BlockSpec((tm, tn), lambda i, j: (i, j)),
        out_shape=jax.ShapeDtypeStruct((S, N), out_dtype),
        compiler_params=_cparams(("parallel", "parallel")),
    )(a, w, *deps)


def swiglu_bwd_da(dxb, wd, g, u, name, tm=1024, tn=768, deps=()):
    def body(dx_ref, w_ref, g_ref, u_ref, dg_ref, du_ref):
        da = lax.dot_general(dx_ref[...], w_ref[...], _NT, preferred_element_type=F32)
        gv = g_ref[...].astype(F32)
        uv = u_ref[...].astype(F32)
        sig = jax.nn.sigmoid(gv)
        dg_ref[...] = (da * uv * (sig * (1.0 + gv * (1.0 - sig)))).astype(BF)
        du_ref[...] = (da * (gv * sig)).astype(BF)

    act = jax.ShapeDtypeStruct((S, DFF_PAD), BF)
    blk = pl.BlockSpec((tm, tn), lambda i, j: (i, j))
    return pl.pallas_call(
        _after(body, 4, deps), name=name,
        grid=(S // tm, DFF_PAD // tn),
        in_specs=[pl.BlockSpec((tm, D), lambda i, j: (i, 0)),
                  pl.BlockSpec((tn, D), lambda i, j: (j, 0)),
                  blk, blk] + [ANY] * len(deps),
        out_specs=[blk, blk],
        out_shape=[act, act],
        compiler_params=_cparams(("parallel", "parallel")),
    )(dxb, wd, g, u, *deps)


def matmul_tn_group(pairs, name, tm=1024, tn=512, deps=()):
    P = len(pairs)
    steps = []
    for p, (a, b) in enumerate(pairs):
        N = 3 * D if b.ndim == 3 else b.shape[1]
        steps += [(p, i, j) for i in range(a.shape[1] // tm) for j in range(N // tn)]
    T = len(steps)
    tab = np.zeros((T, 1 + 2 * P), np.int32)
    for p in range(P):
        cur = (0, 0)
        for s, (ph, i, j) in enumerate(steps):
            if ph == p:
                cur = (i, j)
            tab[s, 1 + 2 * p:3 + 2 * p] = cur
    tab[:, 0] = [ph for ph, _, _ in steps]

    in_specs, out_specs, out_shapes, operands = [], [], [], []
    per = D // tn
    for p, (a, b) in enumerate(pairs):
        ci, cj = 1 + 2 * p, 2 + 2 * p
        in_specs.append(pl.BlockSpec((S, tm), lambda s, t, ci=ci: (0, t[s, ci])))
        if b.ndim == 3:
            in_specs.append(pl.BlockSpec((None, S, tn),
                                         lambda s, t, cj=cj: (t[s, cj] // per, 0, t[s, cj] % per)))
            N = 3 * D
        else:
            in_specs.append(pl.BlockSpec((S, tn), lambda s, t, cj=cj: (0, t[s, cj])))
            N = b.shape[1]
        out_specs.append(pl.BlockSpec((tm, tn), lambda s, t, ci=ci, cj=cj: (t[s, ci], t[s, cj])))
        out_shapes.append(jax.ShapeDtypeStruct((a.shape[1], N), BF))
        operands += [a, b]

    def body(tab_ref, *refs):
        phase = tab_ref[pl.program_id(0), 0]
        for p in range(P):
            @pl.when(phase == p)
            def _(p=p):
                refs[2 * P + p][...] = lax.dot_general(
                    refs[2 * p][...], refs[2 * p + 1][...], _TN,
                    preferred_element_type=F32).astype(BF)

    return pl.pallas_call(
        _after(body, 1 + 2 * P, deps), name=name,
        grid_spec=pltpu.PrefetchScalarGridSpec(
            num_scalar_prefetch=1, grid=(T,), in_specs=in_specs + [ANY] * len(deps),
            out_specs=out_specs),
        out_shape=out_shapes,
        compiler_params=_cparams(("arbitrary",)),
    )(jnp.asarray(tab), *operands, *deps)


def matmul_normbwd(terms, x_in, gain, dx, name, tm=512, ch=256, deps=()):
    specs, operands = [], []
    for (a, w, stacked) in terms:
        if stacked:
            specs.append(pl.BlockSpec((3, tm, D), lambda i: (0, i, 0)))
        else:
            specs.append(pl.BlockSpec((tm, a.shape[1]), lambda i: (i, 0)))
        specs.append(pl.BlockSpec(w.shape, lambda i: (0, 0), pipeline_mode=pl.Buffered(1)))
        operands += [a, w]
    nt = len(terms)

    def body(*refs):
        aw = refs[:2 * nt]
        x_ref, g_ref, dx_ref, dxo_ref, dxb_ref, dg_ref = refs[2 * nt:]

        @pl.when(pl.program_id(0) == 0)
        def _():
            dg_ref[...] = jnp.zeros_like(dg_ref)

        dgain = None
        for c in range(tm // ch):
            rows = slice(c * ch, (c + 1) * ch)
            dh = None
            for t, (_, _, stacked) in enumerate(terms):
                a_ref, w_ref = aw[2 * t], aw[2 * t + 1]
                if stacked:
                    parts = [lax.dot_general(a_ref[k, rows, :], w_ref[:, k * D:(k + 1) * D], _NT,
                                             preferred_element_type=F32) for k in range(3)]
                else:
                    parts = [jnp.dot(a_ref[rows, :], w_ref[...], preferred_element_type=F32)]
                for p in parts:
                    dh = p if dh is None else dh + p
            xv = x_ref[rows, :]
            r = _rms(xv)
            xhat = xv * r
            part = jnp.sum(dh * xhat, axis=0, keepdims=True)
            dgain = part if dgain is None else dgain + part
            dxh = dh * g_ref[...]
            dxn = r * (dxh - xhat * jnp.mean(dxh * xhat, axis=-1, keepdims=True))
            out = dx_ref[rows, :] + dxn
            dxo_ref[rows, :] = out
            dxb_ref[rows, :] = out.astype(BF)
        dg_ref[0:1, :] += dgain

    row = pl.BlockSpec((tm, D), lambda i: (i, 0))
    return pl.pallas_call(
        _after(body, 2 * nt + 3, deps), name=name,
        grid=(S // tm,),
        in_specs=specs + [row, pl.BlockSpec((1, D), lambda i: (0, 0)), row] + [ANY] * len(deps),
        out_specs=[row, row, pl.BlockSpec((8, D), lambda i: (0, 0))],
        out_shape=[jax.ShapeDtypeStruct((S, D), F32), jax.ShapeDtypeStruct((S, D), BF),
                   jax.ShapeDtypeStruct((8, D), F32)],
        compiler_params=_cparams(("arbitrary",)),
    )(*operands, x_in, gain, dx, *deps)


def loss_head(x, gain, target, name, tm=512):
    def body(x_ref, g_ref, t_ref, dxo_ref, dxb_ref, dg_ref, sq_ref):
        @pl.when(pl.program_id(0) == 0)
        def _():
            dg_ref[...] = jnp.zeros_like(dg_ref)
            sq_ref[...] = jnp.zeros_like(sq_ref)
        xv = x_ref[...]
        r = _rms(xv)
        xhat = xv * r
        err = xhat * g_ref[...] - t_ref[...]
        sq_ref[0:1, :] += jnp.sum(err * err, axis=0, keepdims=True)
        dy = err * (1.0 / D)
        dg_ref[0:1, :] += jnp.sum(dy * xhat, axis=0, keepdims=True)
        dxh = dy * g_ref[...]
        out = r * (dxh - xhat * jnp.mean(dxh * xhat, axis=-1, keepdims=True))
        dxo_ref[...] = out
        dxb_ref[...] = out.astype(BF)

    row = pl.BlockSpec((tm, D), lambda i: (i, 0))
    acc = pl.BlockSpec((8, D), lambda i: (0, 0))
    return pl.pallas_call(
        body, name=name,
        grid=(S // tm,),
        in_specs=[row, pl.BlockSpec((1, D), lambda i: (0, 0)), row],
        out_specs=[row, row, acc, acc],
        out_shape=[jax.ShapeDtypeStruct((S, D), F32), jax.ShapeDtypeStruct((S, D), BF),
                   jax.ShapeDtypeStruct((8, D), F32), jax.ShapeDtypeStruct((8, D), F32)],
        compiler_params=_cparams(("arbitrary",)),
    )(x, gain, target)


CONV_TM = 256
HALO = 16


def _halo_row(halo, r):
    hrow = lax.broadcasted_iota(jnp.int32, halo.shape, 0)
    return jnp.sum(jnp.where(hrow == r, halo, 0.0), axis=0, keepdims=True)


def _prev_rows(p, halo_p, n, row):
    out = pltpu.roll(p, n, axis=0)
    for k in range(n):
        out = jnp.where(row == k, _halo_row(halo_p, HALO - n + k), out)
    return out


def _next_rows(p, halo_p, n, row):
    tm = p.shape[0]
    out = pltpu.roll(p, tm - n, axis=0)
    for k in range(n):
        out = jnp.where(row == tm - n + k, _halo_row(halo_p, k), out)
    return out


def _conv_specs():
    per = CONV_TM // HALO
    main = pl.BlockSpec((3, CONV_TM, D), lambda i: (0, i, 0))
    prev = pl.BlockSpec((3, HALO, D), lambda i: (0, jnp.maximum(i * per - 1, 0), 0))
    nxt = pl.BlockSpec((3, HALO, D), lambda i: (0, jnp.minimum((i + 1) * per, S // HALO - 1), 0))
    return main, prev, nxt


def conv_fwd(z3, taps, name):
    def body(z_ref, zp_ref, k_ref, m_ref):
        i = pl.program_id(0)
        p = z_ref[1].astype(F32) * z_ref[2].astype(F32)
        halo = zp_ref[1].astype(F32) * zp_ref[2].astype(F32) * jnp.where(i > 0, 1.0, 0.0)
        row = lax.broadcasted_iota(jnp.int32, p.shape, 0)
        y = (k_ref[2:3, :] * p + k_ref[1:2, :] * _prev_rows(p, halo, 1, row)
             + k_ref[0:1, :] * _prev_rows(p, halo, 2, row))
        m_ref[...] = (z_ref[0].astype(F32) * y).astype(BF)

    main, prev, _ = _conv_specs()
    return pl.pallas_call(
        body, name=name,
        grid=(S // CONV_TM,),
        in_specs=[main, prev, pl.BlockSpec((8, D), lambda i: (0, 0))],
        out_specs=pl.BlockSpec((CONV_TM, D), lambda i: (i, 0)),
        out_shape=jax.ShapeDtypeStruct((S, D), BF),
        compiler_params=_cparams(("parallel",)),
    )(z3, z3, taps)


def conv_bwd(dm, z3, taps, name, deps=()):
    per = CONV_TM // HALO

    def body(dm_ref, dmn_ref, z_ref, zp_ref, zn_ref, k_ref, dz_ref, dk_ref):
        i = pl.program_id(0)

        @pl.when(i == 0)
        def _():
            dk_ref[...] = jnp.zeros_like(dk_ref)

        dmv = dm_ref[...]
        b = z_ref[0].astype(F32)
        c = z_ref[1].astype(F32)
        u = z_ref[2].astype(F32)
        p = c * u
        halo_p = zp_ref[1].astype(F32) * zp_ref[2].astype(F32) * jnp.where(i > 0, 1.0, 0.0)
        halo_dy = (dmn_ref[...] * zn_ref[0].astype(F32)
                   * jnp.where(i < S // CONV_TM - 1, 1.0, 0.0))
        row = lax.broadcasted_iota(jnp.int32, p.shape, 0)
        p1 = _prev_rows(p, halo_p, 1, row)
        p2 = _prev_rows(p, halo_p, 2, row)
        y = k_ref[2:3, :] * p + k_ref[1:2, :] * p1 + k_ref[0:1, :] * p2
        dy = dmv * b
        dz_ref[0] = (dmv * y).astype(BF)
        dp = (k_ref[2:3, :] * dy + k_ref[1:2, :] * _next_rows(dy, halo_dy, 1, row)
              + k_ref[0:1, :] * _next_rows(dy, halo_dy, 2, row))
        dz_ref[1] = (dp * u).astype(BF)
        dz_ref[2] = (dp * c).astype(BF)
        dk_ref[0:1, :] += jnp.sum(dy * p2, axis=0, keepdims=True)
        dk_ref[1:2, :] += jnp.sum(dy * p1, axis=0, keepdims=True)
        dk_ref[2:3, :] += jnp.sum(dy * p, axis=0, keepdims=True)

    main, prev, nxt = _conv_specs()
    return pl.pallas_call(
        _after(body, 6, deps), name=name,
        grid=(S // CONV_TM,),
        in_specs=[pl.BlockSpec((CONV_TM, D), lambda i: (i, 0)),
                  pl.BlockSpec((HALO, D), lambda i: (jnp.minimum((i + 1) * per, S // HALO - 1), 0)),
                  main, prev, nxt, pl.BlockSpec((8, D), lambda i: (0, 0))] + [ANY] * len(deps),
        out_specs=[main, pl.BlockSpec((8, D), lambda i: (0, 0))],
        out_shape=[jax.ShapeDtypeStruct((3, S, D), BF), jax.ShapeDtypeStruct((8, D), F32)],
        compiler_params=_cparams(("arbitrary",)),
    )(dm, dm, z3, z3, z3, taps, *deps)


def _t5_bucket(dist):
    exact = NUM_BUCKETS // 2
    df = jnp.maximum(dist, 1).astype(jnp.float32)
    large = exact + (jnp.log(df / exact) / math.log(MAX_DISTANCE / exact)
                     * (NUM_BUCKETS - exact)).astype(jnp.int32)
    large = jnp.minimum(large, NUM_BUCKETS - 1)
    return jnp.where(dist < exact, dist, large)


def _bucket_onehot_t():
    qi = jnp.arange(BLK)[:, None]
    ki = jnp.arange(2 * BLK)[None, :]
    rel = qi + BLK - ki
    band = ((rel >= 0) & (rel <= BLK)).reshape(1, -1).astype(F32)
    hots = []
    for d in BRANCH_DILATIONS:
        bucket = _t5_bucket(jnp.clip(rel, 0) * d).reshape(1, -1)
        hots.append((jnp.arange(NUM_BUCKETS)[:, None] == bucket).astype(F32))
    return jnp.stack(hots), band


def bias_tables(rel_bias_t, onehot_t, band, name):
    def body(rb_ref, oh_ref, band_ref, o_ref):
        b = jnp.dot(rb_ref[...], oh_ref[...], preferred_element_type=F32,
                    precision=lax.Precision.HIGHEST)
        o_ref[...] = jnp.where(band_ref[...] > 0.5, b, NEG_INF)

    n = BLK * 2 * BLK
    return pl.pallas_call(
        body, name=name,
        grid=(3,),
        in_specs=[pl.BlockSpec((H, NUM_BUCKETS), lambda g: (0, 0)),
                  pl.BlockSpec((None, NUM_BUCKETS, n), lambda g: (g, 0, 0)),
                  pl.BlockSpec((1, n), lambda g: (0, 0))],
        out_specs=pl.BlockSpec((None, H, n), lambda g: (g, 0, 0)),
        out_shape=jax.ShapeDtypeStruct((3, H, n), F32),
        compiler_params=_cparams(("parallel",)),
    )(rel_bias_t, onehot_t, band)


def _diagonal_onehot_t():
    c = jnp.arange(BLK)
    dist = jnp.concatenate([c + 1, (c + 1) % BLK])[None, :]
    hots = [(jnp.arange(NUM_BUCKETS)[:, None] == _t5_bucket(dist * d)).astype(F32)
            for d in BRANCH_DILATIONS]
    return jnp.stack(hots)


def bias_grad(dsums, onehot_t, name):
    def body(ds_ref, oh_ref, o_ref):
        @pl.when(pl.program_id(0) == 0)
        def _():
            o_ref[...] = jnp.zeros_like(o_ref)
        o_ref[...] += lax.dot_general(ds_ref[...], oh_ref[...], _NT, preferred_element_type=F32,
                                      precision=lax.Precision.HIGHEST)

    return pl.pallas_call(
        body, name=name,
        grid=(dsums.shape[0],),
        in_specs=[pl.BlockSpec((None, H, 2 * BLK), lambda g: (g, 0, 0)),
                  pl.BlockSpec((None, NUM_BUCKETS, 2 * BLK), lambda g: (g % 3, 0, 0))],
        out_specs=pl.BlockSpec((H, NUM_BUCKETS), lambda g: (0, 0)),
        out_shape=jax.ShapeDtypeStruct((H, NUM_BUCKETS), F32),
        compiler_params=_cparams(("arbitrary",)),
    )(dsums, onehot_t)


def _head_masks():
    lane = lax.broadcasted_iota(jnp.int32, (1, 2 * DH), 1)
    return (lane < DH, lane >= DH)


def _stack_heads(x, masks):
    zero = jnp.zeros_like(x)
    return jnp.concatenate([jnp.where(masks[0], x, zero), jnp.where(masks[1], x, zero)], axis=0)


def _deinterleave(src_ref, dst_ref, d, dtype):
    L = S // d
    for r in range(d):
        dst_ref[r * L:(r + 1) * L, :] = src_ref[pl.ds(r, L, stride=d), :].astype(dtype)


def _branch_loops(d, block):
    L = S // d
    for r in range(d):
        base = r * L
        block(base, base, BLK, True)
        for n in range(1, L // BLK):
            block(base + n * BLK, base + (n - 1) * BLK, 2 * BLK, False)


def attention_fwd(z3, bias3, name):
    W = 2 * DH
    CH = 256

    def body(q_ref, k_ref, v_ref, b_ref, o_ref, lse_ref, stage, qd, kd, vd, od, ld, on, ln):
        masks = _head_masks()
        for src, dst in ((q_ref, qd), (k_ref, kd), (v_ref, vd)):
            stage[...] = src[...].astype(F32)
            for gi, d in enumerate(BRANCH_DILATIONS[1:]):
                _deinterleave(stage, dst.at[gi], d, BF)

        for g, d in enumerate(BRANCH_DILATIONS):
            qs, ks, vs = (q_ref, k_ref, v_ref) if d == 1 else (qd.at[g - 1], kd.at[g - 1], vd.at[g - 1])
            o_dst, l_dst = (on.at[0], ln.at[0]) if d == 1 else (od, ld)

            def block(q0, k0, nk, first, g=g, qs=qs, ks=ks, vs=vs, o_dst=o_dst, l_dst=l_dst):
                q2 = _stack_heads(qs[pl.ds(q0, BLK), :], masks)
                kk = ks[pl.ds(k0, nk), :]
                vv = vs[pl.ds(k0, nk), :]
                bias = b_ref[g][:, BLK:] if first else b_ref[g]
                s = lax.dot_general(q2, kk, _NT, preferred_element_type=F32) * SCALE + bias
                mx = jnp.max(s, axis=1, keepdims=True)
                p = jnp.exp(s - mx)
                l = jnp.sum(p, axis=1, keepdims=True)
                o2 = jnp.dot(p.astype(BF), vv, preferred_element_type=F32) / l
                lse2 = mx + jnp.log(l)
                o_dst[pl.ds(q0, BLK), :] = jnp.where(masks[0], o2[:BLK], o2[BLK:])
                l_dst[pl.ds(q0, BLK), :] = jnp.where(masks[0], lse2[:BLK], lse2[BLK:])

            _branch_loops(d, block)
            if d > 1:
                L = S // d
                for r in range(d):
                    on[g, pl.ds(r, L, stride=d), :] = od[r * L:(r + 1) * L, :]
                    ln[g, pl.ds(r, L, stride=d), :] = ld[r * L:(r + 1) * L, :]

        def join(c, carry):
            rows = pl.ds(pl.multiple_of(c * CH, CH), CH)
            a, b, cc = ln[0, rows, :], ln[1, rows, :], ln[2, rows, :]
            mx = jnp.maximum(jnp.maximum(a, b), cc)
            ea, eb, ec = jnp.exp(a - mx), jnp.exp(b - mx), jnp.exp(cc - mx)
            tot = ea + eb + ec
            o_ref[rows, :] = ((ea * on[0, rows, :] + eb * on[1, rows, :] + ec * on[2, rows, :])
                              / tot).astype(BF)
            lse_ref[rows, :] = mx + jnp.log(tot)
            return carry
        lax.fori_loop(0, S // CH, join, 0)

    col = pl.BlockSpec((S, W), lambda hp: (0, hp))
    return pl.pallas_call(
        body, name=name,
        grid=(D // W,),
        in_specs=[pl.BlockSpec((None, S, W), lambda hp: (0, 0, hp)),
                  pl.BlockSpec((None, S, W), lambda hp: (1, 0, hp)),
                  pl.BlockSpec((None, S, W), lambda hp: (2, 0, hp)),
                  pl.BlockSpec((3, 2 * BLK, 2 * BLK), lambda hp: (0, hp, 0))],
        out_specs=[col, col],
        out_shape=[jax.ShapeDtypeStruct((S, D), BF), jax.ShapeDtypeStruct((S, D), F32)],
        scratch_shapes=[pltpu.VMEM((S, W), F32),
                        pltpu.VMEM((2, S, W), BF), pltpu.VMEM((2, S, W), BF), pltpu.VMEM((2, S, W), BF),
                        pltpu.VMEM((S, W), F32), pltpu.VMEM((S, W), F32),
                        pltpu.VMEM((3, S, W), F32), pltpu.VMEM((3, S, W), F32)],
        compiler_params=_cparams(("parallel",)),
    )(z3, z3, z3, bias3)


def attention_bwd(z3, dob, ob, lse_b, bias3, name, deps=()):
    W = 2 * DH
    CH = 256

    def body(q_ref, k_ref, v_ref, do_ref, o_ref, lse_ref, b_ref, dz_ref, dsum_ref,
             stage, delta, qd, kd, vd, dod, lsd, dld, res, acc, db_ref):
        masks = _head_masks()

        def rowsum(c, carry):
            rows = pl.ds(pl.multiple_of(c * CH, CH), CH)
            prod = do_ref[rows, :].astype(F32) * o_ref[rows, :].astype(F32)
            sa = jnp.sum(jnp.where(masks[0], prod, 0.0), axis=1, keepdims=True)
            sb = jnp.sum(jnp.where(masks[1], prod, 0.0), axis=1, keepdims=True)
            delta[rows, :] = jnp.where(masks[0], sa, sb)
            return carry
        lax.fori_loop(0, S // CH, rowsum, 0)

        for src, dst in ((q_ref, qd), (k_ref, kd), (v_ref, vd), (do_ref, dod)):
            stage[...] = src[...].astype(F32)
            for gi, d in enumerate(BRANCH_DILATIONS[1:]):
                _deinterleave(stage, dst.at[gi], d, BF)
        for gi, d in enumerate(BRANCH_DILATIONS[1:]):
            _deinterleave(lse_ref, lsd.at[gi], d, F32)
            _deinterleave(delta, dld.at[gi], d, F32)

        db_ref[...] = jnp.zeros_like(db_ref)
        for g, d in enumerate(BRANCH_DILATIONS):
            if d == 1:
                qs, ks, vs, dos, ls, dl = q_ref, k_ref, v_ref, do_ref, lse_ref, delta
            else:
                qs, ks, vs, dos = qd.at[g - 1], kd.at[g - 1], vd.at[g - 1], dod.at[g - 1]
                ls, dl = lsd.at[g - 1], dld.at[g - 1]
            res[1] = jnp.zeros((S, W), F32)
            res[2] = jnp.zeros((S, W), F32)

            def block(q0, k0, nk, first, g=g, qs=qs, ks=ks, vs=vs, dos=dos, ls=ls, dl=dl):
                kk = ks[pl.ds(k0, nk), :]
                vv = vs[pl.ds(k0, nk), :]
                q2 = _stack_heads(qs[pl.ds(q0, BLK), :], masks)
                do2 = _stack_heads(dos[pl.ds(q0, BLK), :], masks)
                lse_blk = ls[pl.ds(q0, BLK), :]
                del_blk = dl[pl.ds(q0, BLK), :]
                lse2 = jnp.concatenate([lse_blk[:, 0:1], lse_blk[:, DH:DH + 1]], axis=0)
                del2 = jnp.concatenate([del_blk[:, 0:1], del_blk[:, DH:DH + 1]], axis=0)
                bias = b_ref[g][:, BLK:] if first else b_ref[g]
                s = lax.dot_general(q2, kk, _NT, preferred_element_type=F32) * SCALE + bias
                p = jnp.exp(s - lse2)
                dp = lax.dot_general(do2, vv, _NT, preferred_element_type=F32)
                ds = p * (dp - del2)
                if first:
                    db_ref[g, :, BLK:] += ds
                else:
                    db_ref[g] += ds
                dsb = ds.astype(BF)
                dq2 = jnp.dot(dsb, kk, preferred_element_type=F32) * SCALE
                res[0, pl.ds(q0, BLK), :] = jnp.where(masks[0], dq2[:BLK], dq2[BLK:])
                res[1, pl.ds(k0, nk), :] += lax.dot_general(dsb, q2, _TN,
                                                            preferred_element_type=F32) * SCALE
                res[2, pl.ds(k0, nk), :] += lax.dot_general(p.astype(BF), do2, _TN,
                                                            preferred_element_type=F32)

            _branch_loops(d, block)
            L = S // d
            for t in range(3):
                if d == 1:
                    acc[t] = res[t]
                else:
                    for r in range(d):
                        acc[t, pl.ds(r, L, stride=d), :] = (acc[t, pl.ds(r, L, stride=d), :]
                                                            + res[t, r * L:(r + 1) * L, :])
        for t in range(3):
            dz_ref[t] = acc[t].astype(BF)

        flip = (lax.broadcasted_iota(jnp.int32, (BLK, BLK), 0)
                + lax.broadcasted_iota(jnp.int32, (BLK, BLK), 1) == BLK - 1).astype(BF)
        dsum_ref[...] = jnp.zeros_like(dsum_ref)
        for g in range(3):
            for hh in range(2):
                halves = []
                for half in range(2):
                    tile = db_ref[g, hh * BLK:(hh + 1) * BLK, half * BLK:(half + 1) * BLK]
                    hi = tile.astype(BF)
                    lo = (tile - hi.astype(F32)).astype(BF)
                    rev = (jnp.dot(hi, flip, preferred_element_type=F32)
                           + jnp.dot(lo, flip, preferred_element_type=F32))
                    skew = pltpu.roll(rev, 0, 1, stride=1, stride_axis=0)
                    halves.append(jnp.sum(skew, axis=0, keepdims=True))
                dsum_ref[2 * g + hh:2 * g + hh + 1, :] = jnp.concatenate(halves, axis=1)

    col = pl.BlockSpec((S, W), lambda hp: (0, hp))
    return pl.pallas_call(
        _after(body, 7, deps), name=name,
        grid=(D // W,),
        in_specs=[pl.BlockSpec((None, S, W), lambda hp: (0, 0, hp)),
                  pl.BlockSpec((None, S, W), lambda hp: (1, 0, hp)),
                  pl.BlockSpec((None, S, W), lambda hp: (2, 0, hp)),
                  col, col, col,
                  pl.BlockSpec((3, 2 * BLK, 2 * BLK), lambda hp: (0, hp, 0))] + [ANY] * len(deps),
        out_specs=[pl.BlockSpec((3, S, W), lambda hp: (0, 0, hp)),
                   pl.BlockSpec((None, 8, 2 * BLK), lambda hp: (hp, 0, 0))],
        out_shape=[jax.ShapeDtypeStruct((3, S, D), BF),
                   jax.ShapeDtypeStruct((D // W, 8, 2 * BLK), F32)],
        scratch_shapes=[pltpu.VMEM((S, W), F32), pltpu.VMEM((S, W), F32),
                        pltpu.VMEM((2, S, W), BF), pltpu.VMEM((2, S, W), BF),
                        pltpu.VMEM((2, S, W), BF), pltpu.VMEM((2, S, W), BF),
                        pltpu.VMEM((2, S, W), F32), pltpu.VMEM((2, S, W), F32),
                        pltpu.VMEM((3, S, W), F32), pltpu.VMEM((3, S, W), F32),
                        pltpu.VMEM((3, 2 * BLK, 2 * BLK), F32)],
        compiler_params=_cparams(("parallel",)),
    )(z3, z3, z3, dob, ob, lse_b, bias3, *deps)


def _me():
    return lax.axis_index("x"), lax.axis_index("y"), lax.axis_index("c")


def _other_chips(x, y):
    return [(1 - x, y), (x, 1 - y), (1 - x, 1 - y)]


def _shard_window(ref, axis, t, shape):
    R, C = shape
    if axis == 0:
        return ref.at[pl.ds(pl.multiple_of(t * R, 128), R), :]
    return ref.at[:, pl.ds(pl.multiple_of(t * C, 128), C)]


def all_gather_weights(shards, axes, name):
    n = len(shards)
    shapes = [s.shape for s in shards]
    outs_shape = [jax.ShapeDtypeStruct((8 * R, C) if ax == 0 else (R, 8 * C), BF)
                  for (R, C), ax in zip(shapes, axes)]

    def body(*refs):
        ins, outs = refs[:n], refs[n:2 * n]
        send_sems, recv_sems, local_sems = refs[2 * n:]
        x, y, c = _me()
        me, sibling = (x, y, c), (x, y, 1 - c)
        xnb, ynb, diag = (1 - x, y), (x, 1 - y), (1 - x, 1 - y)
        south = c == 0
        relay_from = (jnp.where(south, x, 1 - x), jnp.where(south, 1 - y, y))
        relay_to = (jnp.where(south, 1 - x, x), jnp.where(south, y, 1 - y))
        barrier = pltpu.get_barrier_semaphore()
        for peer in [sibling, (*xnb, c), (*ynb, c)]:
            pl.semaphore_signal(barrier, inc=1, device_id=peer, device_id_type=MESH)
        pl.semaphore_wait(barrier, 3)

        def win(i, px, py, pc):
            return _shard_window(outs[i], axes[i], 4 * px + 2 * py + pc, shapes[i])

        def copy(i, k, block, to, src=None):
            return pltpu.make_async_remote_copy(
                src_ref=win(i, *block) if src is None else src, dst_ref=win(i, *block),
                send_sem=send_sems.at[i * 7 + k], recv_sem=recv_sems.at[i * 7 + k],
                device_id=to, device_id_type=MESH)

        mine = [pltpu.make_async_copy(ins[i], win(i, *me), local_sems.at[i]) for i in range(n)]
        for cp in mine:
            cp.start()
        sent = []
        for i in range(n):
            sent += [copy(i, 0, me, sibling, src=ins[i]), copy(i, 1, me, (*xnb, c), src=ins[i]),
                     copy(i, 2, me, (*ynb, c), src=ins[i])]
        for cp in sent:
            cp.start()
        for i in range(n):
            for k, chip in ((1, xnb), (2, ynb)):
                copy(i, k, (*chip, c), me).wait_recv()
                sent.append(copy(i, 3 + k, (*chip, c), sibling))
                sent[-1].start()
            sent.append(copy(i, 3, (*relay_from, c), (*relay_to, c)))
            sent[-1].start()
        for i in range(n):
            copy(i, 3, (*diag, c), me).wait_recv()
            sent.append(copy(i, 6, (*diag, c), sibling))
            sent[-1].start()
        for i in range(n):
            copy(i, 0, sibling, me).wait_recv()
            for k, chip in ((4, xnb), (5, ynb), (6, diag)):
                copy(i, k, (*chip, 1 - c), me).wait_recv()
        for cp in sent:
            cp.wait_send()
        for cp in mine:
            cp.wait()

    return pl.kernel(
        body, out_type=outs_shape, name=name,
        mesh=plsc.ScalarSubcoreMesh(axis_name="sequencer", num_cores=1),
        scratch_types=[pltpu.SemaphoreType.DMA((7 * n,)), pltpu.SemaphoreType.DMA((7 * n,)),
                       pltpu.SemaphoreType.DMA((n,))],
        compiler_params=pltpu.CompilerParams(collective_id=1),
    )(*shards)


def pair_exchange_grads(grads, axes, shapes, name):
    n = len(grads)

    def body(*refs):
        ins, outs = refs[:n], refs[n:2 * n]
        send_sems, recv_sems = refs[2 * n:]
        x, y, c = _me()
        sibling = (x, y, 1 - c)
        barrier = pltpu.get_barrier_semaphore()
        pl.semaphore_signal(barrier, inc=1, device_id=sibling, device_id_type=MESH)
        pl.semaphore_wait(barrier, 1)
        copies = []
        for i in range(n):
            for q in range(4):
                t = 2 * q + (1 - c)
                copies.append(pltpu.make_async_remote_copy(
                    src_ref=_shard_window(ins[i], axes[i], t, shapes[i]), dst_ref=outs[i].at[q],
                    send_sem=send_sems.at[i * 4 + q], recv_sem=recv_sems.at[i * 4 + q],
                    device_id=sibling, device_id_type=MESH))
        for cp in copies:
            cp.start()
        for cp in copies:
            cp.wait_recv()
        for cp in copies:
            cp.wait_send()

    return pl.kernel(
        body, out_type=[jax.ShapeDtypeStruct((4,) + tuple(sh), BF) for sh in shapes], name=name,
        mesh=plsc.ScalarSubcoreMesh(axis_name="sequencer", num_cores=1),
        scratch_types=[pltpu.SemaphoreType.DMA((4 * n,)), pltpu.SemaphoreType.DMA((4 * n,))],
        compiler_params=pltpu.CompilerParams(collective_id=2),
    )(*grads)


def pair_add(grads, landed, axes, shapes, c_idx, name, deps=()):
    n = len(grads)

    def body(c_ref, *refs):
        for t in range(n):
            refs[2 * n + t][...] = (refs[2 * t][...].astype(F32)
                                    + refs[2 * t + 1][...].astype(F32)).astype(BF)

    halves = 1
    in_specs, out_specs, out_shapes, operands = [], [], [], []
    for t in range(n):
        R, C = shapes[t]
        rh = R // halves
        if axes[t] == 0:
            in_specs.append(pl.BlockSpec(
                (rh, C), lambda q, h, c_ref: (halves * (2 * q + c_ref[0]) + h, 0)))
        else:
            in_specs.append(pl.BlockSpec((rh, C), lambda q, h, c_ref: (h, 2 * q + c_ref[0])))
        blk = pl.BlockSpec((None, rh, C), lambda q, h, c_ref: (q, h, 0))
        in_specs.append(blk)
        out_specs.append(blk)
        out_shapes.append(jax.ShapeDtypeStruct((4, R, C), BF))
        operands += [grads[t], landed[t]]
    return pl.pallas_call(
        _after(body, 1 + 2 * n, deps), name=name,
        grid_spec=pltpu.PrefetchScalarGridSpec(
            num_scalar_prefetch=1, grid=(4, halves), in_specs=in_specs + [ANY] * len(deps),
            out_specs=out_specs),
        out_shape=out_shapes,
        compiler_params=_cparams(("parallel", "parallel")),
    )(c_idx, *operands, *deps)


def chip_exchange_grads(parts, name):
    n = len(parts)

    def body(*refs):
        ins, outs, relay = refs[:n], refs[n:2 * n], refs[2 * n:3 * n]
        send_sems, recv_sems = refs[3 * n:]
        x, y, c = _me()
        xnb, ynb, diag = (1 - x, y), (x, 1 - y), (1 - x, 1 - y)
        south = c == 0
        via = (jnp.where(south, 1 - x, x), jnp.where(south, y, 1 - y))
        onward = (jnp.where(south, x, 1 - x), jnp.where(south, 1 - y, y))
        barrier = pltpu.get_barrier_semaphore()
        for peer in (xnb, ynb):
            pl.semaphore_signal(barrier, inc=1, device_id=(*peer, c), device_id_type=MESH)
        pl.semaphore_wait(barrier, 2)

        def copy(i, k, src, dst, to):
            return pltpu.make_async_remote_copy(
                src_ref=src, dst_ref=dst, send_sem=send_sems.at[i * 4 + k],
                recv_sem=recv_sems.at[i * 4 + k], device_id=(*to, c), device_id_type=MESH)

        sent = []
        for i in range(n):
            sent += [copy(i, 0, ins[i].at[2 * xnb[0] + xnb[1]], outs[i].at[0], xnb),
                     copy(i, 1, ins[i].at[2 * ynb[0] + ynb[1]], outs[i].at[1], ynb),
                     copy(i, 2, ins[i].at[2 * diag[0] + diag[1]], relay[i], via)]
        for cp in sent:
            cp.start()
        for i in range(n):
            copy(i, 2, relay[i], relay[i], via).wait_recv()
            sent.append(copy(i, 3, relay[i], outs[i].at[2], onward))
            sent[-1].start()
        for i in range(n):
            copy(i, 0, outs[i].at[0], outs[i].at[0], xnb).wait_recv()
            copy(i, 1, outs[i].at[1], outs[i].at[1], ynb).wait_recv()
            copy(i, 3, outs[i].at[2], outs[i].at[2], onward).wait_recv()
        for cp in sent:
            cp.wait_send()

    landing = [jax.ShapeDtypeStruct((3,) + tuple(p.shape[1:]), BF) for p in parts]
    staging = [jax.ShapeDtypeStruct(tuple(p.shape[1:]), BF) for p in parts]
    return pl.kernel(
        body, out_type=landing + staging, name=name,
        mesh=plsc.ScalarSubcoreMesh(axis_name="sequencer", num_cores=1),
        scratch_types=[pltpu.SemaphoreType.DMA((4 * n,)), pltpu.SemaphoreType.DMA((4 * n,))],
        compiler_params=pltpu.CompilerParams(collective_id=3),
    )(*parts)[:n]


def all_gather_small(v, name):
    R, C = v.shape

    def body(v_ref, out_ref, send_sems, recv_sems, local_sem):
        x, y, c = _me()
        me, sibling = (x, y, c), (x, y, 1 - c)
        chips = _other_chips(x, y)

        def slot(px, py, pc):
            return out_ref.at[4 * px + 2 * py + pc]

        def copy(k, block, to, src=None):
            return pltpu.make_async_remote_copy(
                src_ref=slot(*block) if src is None else src, dst_ref=slot(*block),
                send_sem=send_sems.at[k], recv_sem=recv_sems.at[k],
                device_id=to, device_id_type=MESH)

        mine = pltpu.make_async_copy(v_ref, slot(*me), local_sem)
        mine.start()
        first = [copy(0, me, sibling, src=v_ref)]
        first += [copy(1 + j, me, (*chip, c), src=v_ref) for j, chip in enumerate(chips)]
        for cp in first:
            cp.start()
        passed = [copy(4 + j, (*chip, c), sibling) for j, chip in enumerate(chips)]
        for j, chip in enumerate(chips):
            copy(1 + j, (*chip, c), me).wait_recv()
            passed[j].start()
        copy(0, sibling, me).wait_recv()
        for j, chip in enumerate(chips):
            copy(4 + j, (*chip, 1 - c), me).wait_recv()
        for cp in first + passed:
            cp.wait_send()
        mine.wait()

    return pl.pallas_call(
        body, name=name,
        in_specs=[pl.BlockSpec(memory_space=pltpu.VMEM)],
        out_specs=pl.BlockSpec(memory_space=pltpu.VMEM),
        out_shape=jax.ShapeDtypeStruct((NDEV, R, C), F32),
        scratch_shapes=[pltpu.SemaphoreType.DMA((7,)), pltpu.SemaphoreType.DMA((7,)),
                        pltpu.SemaphoreType.DMA],
    )(v)


def _adamw(w, g, m, v):
    m = ADAM_B1 * m + (1.0 - ADAM_B1) * g
    v = ADAM_B2 * v + (1.0 - ADAM_B2) * (g * g)
    m_hat = m / (1.0 - ADAM_B1 ** ADAM_STEP)
    v_hat = v / (1.0 - ADAM_B2 ** ADAM_STEP)
    delta = -ADAM_LR * (m_hat / (jnp.sqrt(v_hat) + ADAM_EPS) + ADAM_WD * w)
    return delta, m, v


def reduce_adamw(parts, landed, params, q_idx, name, prevs, deps=()):
    n = len(parts)
    halves = 2
    in_specs, out_specs, out_shapes, operands, extra, aliases = [], [], [], [], [], {}
    for t in range(n):
        R, C = parts[t].shape[1:]
        w, m, v, layer = params[t]
        r, c = w.shape[1:]
        tr = r // halves
        assert tr % 16 == 0 and c == C
        wspec = pl.BlockSpec((None, tr, c), lambda i, q_ref, layer=layer: (layer, i, 0))
        in_specs += [pl.BlockSpec((None, tr, C), lambda i, q_ref: (q_ref[0], i, 0)),
                     pl.BlockSpec((3, tr, C), lambda i, q_ref: (0, i, 0)), wspec, wspec, wspec]
        out_specs += [wspec] * 4
        out_shapes += [jax.ShapeDtypeStruct(w.shape, F32)] * 4
        operands += [parts[t], landed[t], w, m, v]
        for k, buf in enumerate(prevs[t]):
            aliases[1 + 5 * n + len(extra)] = 4 * t + k
            extra.append(buf)
    extra += list(deps)

    def body(q_ref, *refs):
        for t in range(n):
            p_ref, l_ref, w_ref, m_ref, v_ref = refs[5 * t:5 * t + 5]
            g = p_ref[...].astype(F32)
            for k in range(3):
                g = g + l_ref[k].astype(F32)
            d, mm, vv = _adamw(w_ref[...], g, m_ref[...], v_ref[...])
            outs = refs[5 * n + 4 * t:5 * n + 4 * t + 4]
            outs[0][...] = g
            outs[1][...] = d
            outs[2][...] = mm
            outs[3][...] = vv

    res = pl.pallas_call(
        _after(body, 1 + 5 * n, extra), name=name,
        grid_spec=pltpu.PrefetchScalarGridSpec(
            num_scalar_prefetch=1, grid=(halves,),
            in_specs=in_specs + [ANY] * len(extra), out_specs=out_specs),
        out_shape=out_shapes,
        input_output_aliases=aliases,
        compiler_params=_cparams(("parallel",)),
    )(q_idx, *operands, *extra)
    return [res[4 * t:4 * t + 4] for t in range(n)]


def small_reduce_adamw(gathered, w, m, v, name):
    R, C = w.shape

    def body(a_ref, w_ref, m_ref, v_ref, g_out, d_out, m_out, v_out):
        g = a_ref[0]
        for k in range(1, NDEV):
            g = g + a_ref[k]
        d, mm, vv = _adamw(w_ref[...], g, m_ref[...], v_ref[...])
        g_out[...] = g
        d_out[...] = d
        m_out[...] = mm
        v_out[...] = vv

    out = jax.ShapeDtypeStruct((R, C), F32)
    return pl.pallas_call(body, name=name, out_shape=[out] * 4,
                          compiler_params=_cparams())(gathered, w, m, v)


def _pad_cols(a, n):
    return jnp.pad(a, ((0, 0), (0, n - a.shape[1])))


def _pad_rows(a, n):
    return jnp.pad(a, ((0, n - a.shape[0]), (0, 0)))


SMALL_ROWS = 16


def _pack_small(mix, ffn, fin, taps_full, relb):
    return jnp.concatenate([
        mix, ffn, fin.reshape(1, D), taps_full.reshape(6, D),
        jnp.pad(relb.reshape(1, NUM_BUCKETS * H), ((0, 0), (0, D - NUM_BUCKETS * H)))], axis=0)


def kernel(x, mix_norm, ffn_norm, final_norm, conv_w_in, conv_kernel, conv_w_out, attn_w_qkv, attn_w_out, rel_bias, ffn_w_gate, ffn_w_up, ffn_w_down, loss_target, m_mix_norm, m_ffn_norm, m_final_norm, m_conv_w_in, m_conv_kernel, m_conv_w_out, m_attn_w_qkv, m_attn_w_out, m_rel_bias, m_ffn_w_gate, m_ffn_w_up, m_ffn_w_down, v_mix_norm, v_ffn_norm, v_final_norm, v_conv_w_in, v_conv_kernel, v_conv_w_out, v_attn_w_qkv, v_attn_w_out, v_rel_bias, v_ffn_w_gate, v_ffn_w_up, v_ffn_w_down):
    xi, yi, ci = _me()
    me = 4 * xi + 2 * yi + ci
    c_idx = jnp.reshape(ci, (1,)).astype(jnp.int32)
    q_idx = jnp.reshape(2 * xi + yi, (1,)).astype(jnp.int32)
    col0 = me * (D // NDEV)

    taps_local = jnp.zeros((2, 3, D), F32)
    taps_local = lax.dynamic_update_slice(taps_local, conv_kernel, (0, 0, col0))
    taps_pack = jnp.pad(taps_local.reshape(6, D), ((0, 2), (0, 0)))
    taps_all = all_gather_small(taps_pack, "ag_taps")
    taps_sum = jnp.sum(taps_all, axis=0)
    taps = [jnp.pad(taps_sum[3 * j:3 * j + 3], ((0, 5), (0, 0))) for j in range(2)]

    gate_t, up_t = jnp.swapaxes(ffn_w_gate, 1, 2), jnp.swapaxes(ffn_w_up, 1, 2)
    m_gate_t, m_up_t = jnp.swapaxes(m_ffn_w_gate, 1, 2), jnp.swapaxes(m_ffn_w_up, 1, 2)
    v_gate_t, v_up_t = jnp.swapaxes(v_ffn_w_gate, 1, 2), jnp.swapaxes(v_ffn_w_up, 1, 2)

    mixer_in = (conv_w_in, attn_w_qkv)
    mixer_out = (conv_w_out, attn_w_out)
    wts = []
    for i in range(DEPTH):
        j = i // 2
        shards = [mixer_in[i % 2][j].astype(BF), mixer_out[i % 2][j].astype(BF),
                  _pad_rows(gate_t[i].astype(BF), FF_SHARD_PAD),
                  _pad_rows(up_t[i].astype(BF), FF_SHARD_PAD),
                  _pad_rows(ffn_w_down[i].astype(BF), FF_SHARD_PAD)]
        axes = (1, 0, 0, 0, 0)
        groups = ((0, 1), (1, 2), (2, 4), (4, 5)) if i == 0 else ((0, 2), (2, 5))
        layer = []
        for lo, hi in groups:
            layer += list(all_gather_weights(shards[lo:hi], axes[lo:hi], f"ag_l{i}_{lo}"))
        wts.append(layer)

    onehot_t, band = _bucket_onehot_t()
    bias3 = bias_tables(rel_bias.T, onehot_t, band, "bias_tables").reshape(3, H * BLK, 2 * BLK)

    saved = []
    xc = x[0]
    for i in range(DEPTH):
        w_in, w_out, w_g, w_u, w_d = wts[i]
        j = i // 2
        x_mix = xc
        z3, h_mix = norm_matmul3(xc, mix_norm[i:i + 1], w_in, f"mix_in_l{i}")
        if i % 2 == 0:
            act = conv_fwd(z3, taps[j], f"conv_fwd_l{i}")
            lse_b = None
        else:
            act, lse_b = attention_fwd(z3, bias3, f"attn_fwd_l{i}")
        xc = matmul_residual(act, w_out, xc, f"mix_out_l{i}")
        x_ffn = xc
        g, u, a, h_ffn = norm_swiglu_up(xc, ffn_norm[i:i + 1], w_g, w_u, f"ffn_up_l{i}")
        xc = matmul_residual(a, w_d, xc, f"ffn_down_l{i}")
        saved.append((x_mix, h_mix, z3, act, lse_b, x_ffn, h_ffn, g, u, a))

    dx, dxb, dg_final, sq = loss_head(xc, final_norm.reshape(1, D), loss_target[0], "loss_head")
    loss = lax.psum(0.5 * jnp.sum(sq[0]) / D, ("x", "y", "c"))

    dg_mix = [None] * DEPTH
    dg_ffn = [None] * DEPTH
    dtaps = [None, None]
    dbias_all = []
    shape_in, shape_out = (D, 3 * D // NDEV), (D // NDEV, D)
    ffn_axes, ffn_shapes = (0, 0, 0), ((FF_SHARD_PAD, D),) * 3
    stacked = {}

    def pair_stage(grads, landed1, axes, shapes, tag, tok):
        parts = pair_add(grads, landed1, axes, shapes, c_idx, f"rs_add_{tag}", deps=[tok])
        return parts, chip_exchange_grads(parts, f"rs_chip_{tag}"), parts[-1]

    def adamw_stage(parts, landed2, params, tag, tok):
        names = [p[0] for p in params]
        res = reduce_adamw(parts, landed2, [p[1:] for p in params], q_idx, f"adamw_{tag}",
                           [stacked.get(nm, ()) for nm in names], deps=[tok])
        for nm, r4 in zip(names, res):
            stacked[nm] = r4
        return res[-1][0]

    tok = dxb
    mix_wait = None
    mix_chip = None
    ffn_chip = None
    for i in reversed(range(DEPTH)):
        w_in, w_out, w_g, w_u, w_d = wts[i]
        j = i // 2
        x_mix, h_mix, z3, act, lse_b, x_ffn, h_ffn, g, u, a = saved[i]
        ffn_params = [("ffn_w_gate", gate_t, m_gate_t, v_gate_t, i),
                      ("ffn_w_up", up_t, m_up_t, v_up_t, i),
                      ("ffn_w_down", ffn_w_down, m_ffn_w_down, v_ffn_w_down, i)]
        if i % 2 == 0:
            mix_params = [("conv_w_in", conv_w_in, m_conv_w_in, v_conv_w_in, j),
                          ("conv_w_out", conv_w_out, m_conv_w_out, v_conv_w_out, j)]
        else:
            mix_params = [("attn_w_qkv", attn_w_qkv, m_attn_w_qkv, v_attn_w_qkv, j),
                          ("attn_w_out", attn_w_out, m_attn_w_out, v_attn_w_out, j)]
        dgate, dup = swiglu_bwd_da(dxb, w_d, g, u, f"ffn_da_l{i}", deps=[tok])
        tok = dgate
        if mix_wait is not None:
            grads_m, landed1_m, params_m, tag_m = mix_wait
            parts_m, landed2_m, tok = pair_stage(grads_m, landed1_m, (1, 0), (shape_in, shape_out),
                                                 tag_m, tok)
            mix_chip = (parts_m, landed2_m, params_m, tag_m)
            mix_wait = None
        grads_f = matmul_tn_group([(dgate, h_ffn), (dup, h_ffn), (a, dxb)], f"ffn_dw_l{i}",
                                  deps=[tok])
        landed1_f = pair_exchange_grads(grads_f, ffn_axes, ffn_shapes, f"rs_pair_f{i}")
        tok = grads_f[-1]
        if ffn_chip is not None:
            tok = adamw_stage(*ffn_chip, tok)
            ffn_chip = None
        dx, dxb, dg_ffn[i] = matmul_normbwd(
            [(dgate, w_g, False), (dup, w_u, False)], x_ffn, ffn_norm[i:i + 1], dx, f"ffn_dh_l{i}",
            deps=[tok])
        dxb_mix = dxb
        dact = matmul_nt(dxb, w_out, f"mix_dact_l{i}", out_dtype=F32 if i % 2 == 0 else BF)
        parts_f, landed2_f, tok = pair_stage(grads_f, landed1_f, ffn_axes, ffn_shapes, f"f{i}", dact)
        ffn_chip = (parts_f, landed2_f, ffn_params, f"f{i}")
        if i % 2 == 0:
            dz3, dtaps[j] = conv_bwd(dact, z3, taps[j], f"conv_bwd_l{i}", deps=[tok])
        else:
            dz3, dsum = attention_bwd(z3, dact, act, lse_b, bias3, f"attn_bwd_l{i}", deps=[tok])
            dbias_all.append(dsum[:, :6].reshape(H // 2, 3, 2, 2 * BLK).transpose(1, 0, 2, 3)
                             .reshape(3, H, 2 * BLK))
        grads_m = matmul_tn_group([(h_mix, dz3), (act, dxb_mix)], f"mix_dw_l{i}")
        landed1_m = pair_exchange_grads(grads_m, (1, 0), (shape_in, shape_out), f"rs_pair_m{i}")
        mix_wait = (grads_m, landed1_m, mix_params, f"m{i}")
        tok = grads_m[-1]
        if mix_chip is not None:
            tok = adamw_stage(*mix_chip, tok)
            mix_chip = None
        dx, dxb, dg_mix[i] = matmul_normbwd(
            [(dz3, w_in, True)], x_mix, mix_norm[i:i + 1], dx, f"mix_dh_l{i}", deps=[tok])
        tok = dxb
    grads_m, landed1_m, params_m, tag_m = mix_wait
    parts_m, landed2_m, tok = pair_stage(grads_m, landed1_m, (1, 0), (shape_in, shape_out), tag_m, tok)
    tok = adamw_stage(*ffn_chip, tok)

    grad_relb_t = bias_grad(jnp.concatenate(dbias_all), _diagonal_onehot_t(), "bias_grad")
    dtaps_full = jnp.stack([dtaps[0][:3], dtaps[1][:3]])
    g_small = _pack_small(jnp.concatenate([d[0:1] for d in dg_mix], axis=0),
                          jnp.concatenate([d[0:1] for d in dg_ffn], axis=0),
                          dg_final[0], dtaps_full, grad_relb_t.T)
    gathered = all_gather_small(g_small, "ag_small_grads")

    def taps_at_cols(k):
        return lax.dynamic_update_slice(jnp.zeros((2, 3, D), F32), k, (0, 0, col0))

    w_small = _pack_small(mix_norm, ffn_norm, final_norm, taps_at_cols(conv_kernel), rel_bias)
    m_small = _pack_small(m_mix_norm, m_ffn_norm, m_final_norm, taps_at_cols(m_conv_kernel), m_rel_bias)
    v_small = _pack_small(v_mix_norm, v_ffn_norm, v_final_norm, taps_at_cols(v_conv_kernel), v_rel_bias)
    small = small_reduce_adamw(gathered, w_small, m_small, v_small, "adamw_small")

    def unpack_small(p):
        taps_p = lax.dynamic_slice(p[9:15].reshape(2, 3, D), (0, 0, col0), (2, 3, D // NDEV))
        return {"mix_norm": p[0:4], "ffn_norm": p[4:8], "final_norm": p[8],
                "conv_kernel": taps_p, "rel_bias": p[15, :NUM_BUCKETS * H].reshape(NUM_BUCKETS, H)}

    small_out = [unpack_small(p) for p in small]
    adamw_stage(parts_m, landed2_m, params_m, tag_m, small[0])

    names = ["mix_norm", "ffn_norm", "final_norm", "conv_w_in", "conv_kernel", "conv_w_out",
             "attn_w_qkv", "attn_w_out", "rel_bias", "ffn_w_gate", "ffn_w_up", "ffn_w_down"]
    outs = [loss, dx.reshape(1, S, D)]
    for o in range(4):
        for nme in names:
            if nme in ("ffn_w_gate", "ffn_w_up"):
                outs.append(jnp.swapaxes(stacked[nme][o], 1, 2))
            else:
                outs.append(stacked[nme][o] if nme in stacked else small_out[o][nme])
    return tuple(outs)
```

```python
import math

import numpy as np
import jax
import jax.numpy as jnp
from jax import lax
from jax.experimental import pallas as pl
from jax.experimental.pallas import tpu as pltpu
from jax.experimental.pallas import tpu_sc as plsc

S = 2048
D = 1024
H = 16
DH = 64
DFF = 2816
NDEV = 8
DEPTH = 4
FF_SHARD = DFF // NDEV
FF_SHARD_PAD = 384
DFF_PAD = FF_SHARD_PAD * NDEV
BLK = 128
BRANCH_DILATIONS = (1, 4, 16)
NUM_BUCKETS = 32
MAX_DISTANCE = 2048
EPS = 1e-6
NEG_INF = -1e30
SCALE = DH ** -0.5

ADAM_LR = 0.001
ADAM_B1 = 0.9
ADAM_B2 = 0.999
ADAM_EPS = 1e-08
ADAM_WD = 0.01
ADAM_STEP = 10

BF = jnp.bfloat16
F32 = jnp.float32
VMEM_LIMIT_BYTES = 56 * 1024 * 1024
KSPLIT = 512
NORM_CHUNK = 256
MESH = pl.DeviceIdType.MESH
ANY = pl.BlockSpec(memory_space=pl.ANY)

_NT = (((1,), (1,)), ((), ()))
_TN = (((0,), (0,)), ((), ()))


def _cparams(sem=None):
    return pltpu.CompilerParams(dimension_semantics=sem, vmem_limit_bytes=VMEM_LIMIT_BYTES)


def _after(body, n, deps):
    nd = len(deps)
    if nd == 0:
        return body

    def ordered(*refs):
        body(*refs[:n], *refs[n + nd:])
    return ordered


def _rms(x):
    return lax.rsqrt(jnp.mean(x * x, axis=-1, keepdims=True) + EPS)


def norm_matmul3(x, gain, w, name, tm=1024, tn=1024):
    per = D // tn

    def body(x_ref, g_ref, w_ref, z_ref, h_ref, hs_ref):
        @pl.when(pl.program_id(1) == 0)
        def _():
            for c in range(tm // NORM_CHUNK):
                rows = slice(c * NORM_CHUNK, (c + 1) * NORM_CHUNK)
                xv = x_ref[rows, :]
                hv = (xv * _rms(xv) * g_ref[...]).astype(BF)
                hs_ref[rows, :] = hv
                h_ref[rows, :] = hv
                z_ref[rows, :] = jnp.dot(hv, w_ref[...], preferred_element_type=F32).astype(BF)

        @pl.when(pl.program_id(1) > 0)
        def _():
            z_ref[...] = jnp.dot(hs_ref[...], w_ref[...], preferred_element_type=F32).astype(BF)

    return pl.pallas_call(
        body, name=name,
        grid=(S // tm, 3 * D // tn),
        in_specs=[pl.BlockSpec((tm, D), lambda i, j: (i, 0)),
                  pl.BlockSpec((1, D), lambda i, j: (0, 0)),
                  pl.BlockSpec((D, tn), lambda i, j: (0, j))],
        out_specs=[pl.BlockSpec((None, tm, tn), lambda i, j: (j // per, i, j % per)),
                   pl.BlockSpec((tm, D), lambda i, j: (i, 0))],
        out_shape=[jax.ShapeDtypeStruct((3, S, D), BF), jax.ShapeDtypeStruct((S, D), BF)],
        scratch_shapes=[pltpu.VMEM((tm, D), BF)],
        compiler_params=_cparams(("parallel", "arbitrary")),
    )(x, gain, w)


def norm_swiglu_up(x, gain, wg_t, wu_t, name, tm=1024, tn=768):
    def body(x_ref, g_ref, wg_ref, wu_ref, go_ref, uo_ref, ao_ref, h_ref, hs_ref):
        def gate_up(hv, rows):
            g = lax.dot_general(hv, wg_ref[...], _NT, preferred_element_type=F32)
            u = lax.dot_general(hv, wu_ref[...], _NT, preferred_element_type=F32)
            go_ref[rows, :] = g.astype(BF)
            uo_ref[rows, :] = u.astype(BF)
            ao_ref[rows, :] = (g * jax.nn.sigmoid(g) * u).astype(BF)

        @pl.when(pl.program_id(1) == 0)
        def _():
            for c in range(tm // NORM_CHUNK):
                rows = slice(c * NORM_CHUNK, (c + 1) * NORM_CHUNK)
                xv = x_ref[rows, :]
                hv = (xv * _rms(xv) * g_ref[...]).astype(BF)
                hs_ref[rows, :] = hv
                h_ref[rows, :] = hv
                gate_up(hv, rows)

        @pl.when(pl.program_id(1) > 0)
        def _():
            gate_up(hs_ref[...], slice(None))

    act = jax.ShapeDtypeStruct((S, DFF_PAD), BF)
    blk = pl.BlockSpec((tm, tn), lambda i, j: (i, j))
    return pl.pallas_call(
        body, name=name,
        grid=(S // tm, DFF_PAD // tn),
        in_specs=[pl.BlockSpec((tm, D), lambda i, j: (i, 0)),
                  pl.BlockSpec((1, D), lambda i, j: (0, 0)),
                  pl.BlockSpec((tn, D), lambda i, j: (j, 0)),
                  pl.BlockSpec((tn, D), lambda i, j: (j, 0))],
        out_specs=[blk, blk, blk, pl.BlockSpec((tm, D), lambda i, j: (i, 0))],
        out_shape=[act, act, act, jax.ShapeDtypeStruct((S, D), BF)],
        scratch_shapes=[pltpu.VMEM((tm, D), BF)],
        compiler_params=_cparams(("parallel", "arbitrary")),
    )(x, gain, wg_t, wu_t)


def matmul_residual(a, w, x, name, tm=1024):
    K = a.shape[1]
    tn = D if K <= D else D // 2
    ns = K // KSPLIT
    kc = K // ns

    def body(*refs):
        x_ref, o_ref = refs[2 * ns:]
        acc = x_ref[...]
        for s in range(ns):
            acc = acc + jnp.dot(refs[s][...], refs[ns + s][...], preferred_element_type=F32)
        o_ref[...] = acc

    return pl.pallas_call(
        body, name=name,
        grid=(S // tm, D // tn),
        in_specs=[pl.BlockSpec((tm, kc), lambda i, j, s=s: (i, s)) for s in range(ns)]
        + [pl.BlockSpec((kc, tn), lambda i, j, s=s: (s, j)) for s in range(ns)]
        + [pl.BlockSpec((tm, tn), lambda i, j: (i, j))],
        out_specs=pl.BlockSpec((tm, tn), lambda i, j: (i, j)),
        out_shape=jax.ShapeDtypeStruct((S, D), F32),
        compiler_params=_cparams(("parallel", "parallel")),
    )(*([a] * ns), *([w] * ns), x)


def matmul_nt(a, w, name, out_dtype=BF, tm=1024, tn=1024, deps=()):
    K = a.shape[1]
    N = w.shape[0]

    def body(a_ref, w_ref, o_ref):
        o_ref[...] = lax.dot_general(a_ref[...], w_ref[...], _NT,
                                     preferred_element_type=F32).astype(o_ref.dtype)

    return pl.pallas_call(
        _after(body, 2, deps), name=name,
        grid=(S // tm, N // tn),
        in_specs=[pl.BlockSpec((tm, K), lambda i, j: (i, 0)),
                  pl.BlockSpec((tn, K), lambda i, j: (j, 0))] + [ANY] * len(deps),
        out_specs=pl.BlockSpec((tm, tn), lambda i, j: (i, j)),
        out_shape=jax.ShapeDtypeStruct((S, N), out_dtype),
        compiler_params=_cparams(("parallel", "parallel")),
    )(a, w, *deps)


def swiglu_bwd_da(dxb, wd, g, u, name, tm=1024, tn=768, deps=()):
    def body(dx_ref, w_ref, g_ref, u_ref, dg_ref, du_ref):
        da = lax.dot_general(dx_ref[...], w_ref[...], _NT, preferred_element_type=F32)
        gv = g_ref[...].astype(F32)
        uv = u_ref[...].astype(F32)
        sig = jax.nn.sigmoid(gv)
        dg_ref[...] = (da * uv * (sig * (1.0 + gv * (1.0 - sig)))).astype(BF)
        du_ref[...] = (da * (gv * sig)).astype(BF)

    act = jax.ShapeDtypeStruct((S, DFF_PAD), BF)
    blk = pl.BlockSpec((tm, tn), lambda i, j: (i, j))
    return pl.pallas_call(
        _after(body, 4, deps), name=name,
        grid=(S // tm, DFF_PAD // tn),
        in_specs=[pl.BlockSpec((tm, D), lambda i, j: (i, 0)),
                  pl.BlockSpec((tn, D), lambda i, j: (j, 0)),
                  blk, blk] + [ANY] * len(deps),
        out_specs=[blk, blk],
        out_shape=[act, act],
        compiler_params=_cparams(("parallel", "parallel")),
    )(dxb, wd, g, u, *deps)


def matmul_tn_group(pairs, name, tm=1024, tn=512, deps=()):
    P = len(pairs)
    steps = []
    for p, (a, b) in enumerate(pairs):
        N = 3 * D if b.ndim == 3 else b.shape[1]
        steps += [(p, i, j) for i in range(a.shape[1] // tm) for j in range(N // tn)]
    T = len(steps)
    tab = np.zeros((T, 1 + 2 * P), np.int32)
    for p in range(P):
        cur = (0, 0)
        for s, (ph, i, j) in enumerate(steps):
            if ph == p:
                cur = (i, j)
            tab[s, 1 + 2 * p:3 + 2 * p] = cur
    tab[:, 0] = [ph for ph, _, _ in steps]

    in_specs, out_specs, out_shapes, operands = [], [], [], []
    per = D // tn
    for p, (a, b) in enumerate(pairs):
        ci, cj = 1 + 2 * p, 2 + 2 * p
        in_specs.append(pl.BlockSpec((S, tm), lambda s, t, ci=ci: (0, t[s, ci])))
        if b.ndim == 3:
            in_specs.append(pl.BlockSpec((None, S, tn),
                                         lambda s, t, cj=cj: (t[s, cj] // per, 0, t[s, cj] % per)))
            N = 3 * D
        else:
            in_specs.append(pl.BlockSpec((S, tn), lambda s, t, cj=cj: (0, t[s, cj])))
            N = b.shape[1]
        out_specs.append(pl.BlockSpec((tm, tn), lambda s, t, ci=ci, cj=cj: (t[s, ci], t[s, cj])))
        out_shapes.append(jax.ShapeDtypeStruct((a.shape[1], N), BF))
        operands += [a, b]

    def body(tab_ref, *refs):
        phase = tab_ref[pl.program_id(0), 0]
        for p in range(P):
            @pl.when(phase == p)
            def _(p=p):
                refs[2 * P + p][...] = lax.dot_general(
                    refs[2 * p][...], refs[2 * p + 1][...], _TN,
                    preferred_element_type=F32).astype(BF)

    return pl.pallas_call(
        _after(body, 1 + 2 * P, deps), name=name,
        grid_spec=pltpu.PrefetchScalarGridSpec(
            num_scalar_prefetch=1, grid=(T,), in_specs=in_specs + [ANY] * len(deps),
            out_specs=out_specs),
        out_shape=out_shapes,
        compiler_params=_cparams(("arbitrary",)),
    )(jnp.asarray(tab), *operands, *deps)


def matmul_normbwd(terms, x_in, gain, dx, name, tm=512, ch=256, deps=()):
    specs, operands = [], []
    for (a, w, stacked) in terms:
        if stacked:
            specs.append(pl.BlockSpec((3, tm, D), lambda i: (0, i, 0)))
        else:
            specs.append(pl.BlockSpec((tm, a.shape[1]), lambda i: (i, 0)))
        specs.append(pl.BlockSpec(w.shape, lambda i: (0, 0), pipeline_mode=pl.Buffered(1)))
        operands += [a, w]
    nt = len(terms)

    def body(*refs):
        aw = refs[:2 * nt]
        x_ref, g_ref, dx_ref, dxo_ref, dxb_ref, dg_ref = refs[2 * nt:]

        @pl.when(pl.program_id(0) == 0)
        def _():
            dg_ref[...] = jnp.zeros_like(dg_ref)

        dgain = None
        for c in range(tm // ch):
            rows = slice(c * ch, (c + 1) * ch)
            dh = None
            for t, (_, _, stacked) in enumerate(terms):
                a_ref, w_ref = aw[2 * t], aw[2 * t + 1]
                if stacked:
                    parts = [lax.dot_general(a_ref[k, rows, :], w_ref[:, k * D:(k + 1) * D], _NT,
                                             preferred_element_type=F32) for k in range(3)]
                else:
                    parts = [jnp.dot(a_ref[rows, :], w_ref[...], preferred_element_type=F32)]
                for p in parts:
                    dh = p if dh is None else dh + p
            xv = x_ref[rows, :]
            r = _rms(xv)
            xhat = xv * r
            part = jnp.sum(dh * xhat, axis=0, keepdims=True)
            dgain = part if dgain is None else dgain + part
            dxh = dh * g_ref[...]
            dxn = r * (dxh - xhat * jnp.mean(dxh * xhat, axis=-1, keepdims=True))
            out = dx_ref[rows, :] + dxn
            dxo_ref[rows, :] = out
            dxb_ref[rows, :] = out.astype(BF)
        dg_ref[0:1, :] += dgain

    row = pl.BlockSpec((tm, D), lambda i: (i, 0))
    return pl.pallas_call(
        _after(body, 2 * nt + 3, deps), name=name,
        grid=(S // tm,),
        in_specs=specs + [row, pl.BlockSpec((1, D), lambda i: (0, 0)), row] + [ANY] * len(deps),
        out_specs=[row, row, pl.BlockSpec((8, D), lambda i: (0, 0))],
        out_shape=[jax.ShapeDtypeStruct((S, D), F32), jax.ShapeDtypeStruct((S, D), BF),
                   jax.ShapeDtypeStruct((8, D), F32)],
        compiler_params=_cparams(("arbitrary",)),
    )(*operands, x_in, gain, dx, *deps)


def loss_head(x, gain, target, name, tm=512):
    def body(x_ref, g_ref, t_ref, dxo_ref, dxb_ref, dg_ref, sq_ref):
        @pl.when(pl.program_id(0) == 0)
        def _():
            dg_ref[...] = jnp.zeros_like(dg_ref)
            sq_ref[...] = jnp.zeros_like(sq_ref)
        xv = x_ref[...]
        r = _rms(xv)
        xhat = xv * r
        err = xhat * g_ref[...] - t_ref[...]
        sq_ref[0:1, :] += jnp.sum(err * err, axis=0, keepdims=True)
        dy = err * (1.0 / D)
        dg_ref[0:1, :] += jnp.sum(dy * xhat, axis=0, keepdims=True)
        dxh = dy * g_ref[...]
        out = r * (dxh - xhat * jnp.mean(dxh * xhat, axis=-1, keepdims=True))
        dxo_ref[...] = out
        dxb_ref[...] = out.astype(BF)

    row = pl.BlockSpec((tm, D), lambda i: (i, 0))
    acc = pl.BlockSpec((8, D), lambda i: (0, 0))
    return pl.pallas_call(
        body, name=name,
        grid=(S // tm,),
        in_specs=[row, pl.BlockSpec((1, D), lambda i: (0, 0)), row],
        out_specs=[row, row, acc, acc],
        out_shape=[jax.ShapeDtypeStruct((S, D), F32), jax.ShapeDtypeStruct((S, D), BF),
                   jax.ShapeDtypeStruct((8, D), F32), jax.ShapeDtypeStruct((8, D), F32)],
        compiler_params=_cparams(("arbitrary",)),
    )(x, gain, target)


CONV_TM = 256
HALO = 16


def _halo_row(halo, r):
    hrow = lax.broadcasted_iota(jnp.int32, halo.shape, 0)
    return jnp.sum(jnp.where(hrow == r, halo, 0.0), axis=0, keepdims=True)


def _prev_rows(p, halo_p, n, row):
    out = pltpu.roll(p, n, axis=0)
    for k in range(n):
        out = jnp.where(row == k, _halo_row(halo_p, HALO - n + k), out)
    return out


def _next_rows(p, halo_p, n, row):
    tm = p.shape[0]
    out = pltpu.roll(p, tm - n, axis=0)
    for k in range(n):
        out = jnp.where(row == tm - n + k, _halo_row(halo_p, k), out)
    return out


def _conv_specs():
    per = CONV_TM // HALO
    main = pl.BlockSpec((3, CONV_TM, D), lambda i: (0, i, 0))
    prev = pl.BlockSpec((3, HALO, D), lambda i: (0, jnp.maximum(i * per - 1, 0), 0))
    nxt = pl.BlockSpec((3, HALO, D), lambda i: (0, jnp.minimum((i + 1) * per, S // HALO - 1), 0))
    return main, prev, nxt


def conv_fwd(z3, taps, name):
    def body(z_ref, zp_ref, k_ref, m_ref):
        i = pl.program_id(0)
        p = z_ref[1].astype(F32) * z_ref[2].astype(F32)
        halo = zp_ref[1].astype(F32) * zp_ref[2].astype(F32) * jnp.where(i > 0, 1.0, 0.0)
        row = lax.broadcasted_iota(jnp.int32, p.shape, 0)
        y = (k_ref[2:3, :] * p + k_ref[1:2, :] * _prev_rows(p, halo, 1, row)
             + k_ref[0:1, :] * _prev_rows(p, halo, 2, row))
        m_ref[...] = (z_ref[0].astype(F32) * y).astype(BF)

    main, prev, _ = _conv_specs()
    return pl.pallas_call(
        body, name=name,
        grid=(S // CONV_TM,),
        in_specs=[main, prev, pl.BlockSpec((8, D), lambda i: (0, 0))],
        out_specs=pl.BlockSpec((CONV_TM, D), lambda i: (i, 0)),
        out_shape=jax.ShapeDtypeStruct((S, D), BF),
        compiler_params=_cparams(("parallel",)),
    )(z3, z3, taps)


def conv_bwd(dm, z3, taps, name, deps=()):
    per = CONV_TM // HALO

    def body(dm_ref, dmn_ref, z_ref, zp_ref, zn_ref, k_ref, dz_ref, dk_ref):
        i = pl.program_id(0)

        @pl.when(i == 0)
        def _():
            dk_ref[...] = jnp.zeros_like(dk_ref)

        dmv = dm_ref[...]
        b = z_ref[0].astype(F32)
        c = z_ref[1].astype(F32)
        u = z_ref[2].astype(F32)
        p = c * u
        halo_p = zp_ref[1].astype(F32) * zp_ref[2].astype(F32) * jnp.where(i > 0, 1.0, 0.0)
        halo_dy = (dmn_ref[...] * zn_ref[0].astype(F32)
                   * jnp.where(i < S // CONV_TM - 1, 1.0, 0.0))
        row = lax.broadcasted_iota(jnp.int32, p.shape, 0)
        p1 = _prev_rows(p, halo_p, 1, row)
        p2 = _prev_rows(p, halo_p, 2, row)
        y = k_ref[2:3, :] * p + k_ref[1:2, :] * p1 + k_ref[0:1, :] * p2
        dy = dmv * b
        dz_ref[0] = (dmv * y).astype(BF)
        dp = (k_ref[2:3, :] * dy + k_ref[1:2, :] * _next_rows(dy, halo_dy, 1, row)
              + k_ref[0:1, :] * _next_rows(dy, halo_dy, 2, row))
        dz_ref[1] = (dp * u).astype(BF)
        dz_ref[2] = (dp * c).astype(BF)
        dk_ref[0:1, :] += jnp.sum(dy * p2, axis=0, keepdims=True)
        dk_ref[1:2, :] += jnp.sum(dy * p1, axis=0, keepdims=True)
        dk_ref[2:3, :] += jnp.sum(dy * p, axis=0, keepdims=True)

    main, prev, nxt = _conv_specs()
    return pl.pallas_call(
        _after(body, 6, deps), name=name,
        grid=(S // CONV_TM,),
        in_specs=[pl.BlockSpec((CONV_TM, D), lambda i: (i, 0)),
                  pl.BlockSpec((HALO, D), lambda i: (jnp.minimum((i + 1) * per, S // HALO - 1), 0)),
                  main, prev, nxt, pl.BlockSpec((8, D), lambda i: (0, 0))] + [ANY] * len(deps),
        out_specs=[main, pl.BlockSpec((8, D), lambda i: (0, 0))],
        out_shape=[jax.ShapeDtypeStruct((3, S, D), BF), jax.ShapeDtypeStruct((8, D), F32)],
        compiler_params=_cparams(("arbitrary",)),
    )(dm, dm, z3, z3, z3, taps, *deps)


def _t5_bucket(dist):
    exact = NUM_BUCKETS // 2
    df = jnp.maximum(dist, 1).astype(jnp.float32)
    large = exact + (jnp.log(df / exact) / math.log(MAX_DISTANCE / exact)
                     * (NUM_BUCKETS - exact)).astype(jnp.int32)
    large = jnp.minimum(large, NUM_BUCKETS - 1)
    return jnp.where(dist < exact, dist, large)


def _bucket_onehot_t():
    qi = jnp.arange(BLK)[:, None]
    ki = jnp.arange(2 * BLK)[None, :]
    rel = qi + BLK - ki
    band = ((rel >= 0) & (rel <= BLK)).reshape(1, -1).astype(F32)
    hots = []
    for d in BRANCH_DILATIONS:
        bucket = _t5_bucket(jnp.clip(rel, 0) * d).reshape(1, -1)
        hots.append((jnp.arange(NUM_BUCKETS)[:, None] == bucket).astype(F32))
    return jnp.stack(hots), band


def bias_tables(rel_bias_t, onehot_t, band, name):
    def body(rb_ref, oh_ref, band_ref, o_ref):
        b = jnp.dot(rb_ref[...], oh_ref[...], preferred_element_type=F32,
                    precision=lax.Precision.HIGHEST)
        o_ref[...] = jnp.where(band_ref[...] > 0.5, b, NEG_INF)

    n = BLK * 2 * BLK
    return pl.pallas_call(
        body, name=name,
        grid=(3,),
        in_specs=[pl.BlockSpec((H, NUM_BUCKETS), lambda g: (0, 0)),
                  pl.BlockSpec((None, NUM_BUCKETS, n), lambda g: (g, 0, 0)),
                  pl.BlockSpec((1, n), lambda g: (0, 0))],
        out_specs=pl.BlockSpec((None, H, n), lambda g: (g, 0, 0)),
        out_shape=jax.ShapeDtypeStruct((3, H, n), F32),
        compiler_params=_cparams(("parallel",)),
    )(rel_bias_t, onehot_t, band)


def _diagonal_onehot_t():
    c = jnp.arange(BLK)
    dist = jnp.concatenate([c + 1, (c + 1) % BLK])[None, :]
    hots = [(jnp.arange(NUM_BUCKETS)[:, None] == _t5_bucket(dist * d)).astype(F32)
            for d in BRANCH_DILATIONS]
    return jnp.stack(hots)


def bias_grad(dsums, onehot_t, name):
    def body(ds_ref, oh_ref, o_ref):
        @pl.when(pl.program_id(0) == 0)
        def _():
            o_ref[...] = jnp.zeros_like(o_ref)
        o_ref[...] += lax.dot_general(ds_ref[...], oh_ref[...], _NT, preferred_element_type=F32,
                                      precision=lax.Precision.HIGHEST)

    return pl.pallas_call(
        body, name=name,
        grid=(dsums.shape[0],),
        in_specs=[pl.BlockSpec((None, H, 2 * BLK), lambda g: (g, 0, 0)),
                  pl.BlockSpec((None, NUM_BUCKETS, 2 * BLK), lambda g: (g % 3, 0, 0))],
        out_specs=pl.BlockSpec((H, NUM_BUCKETS), lambda g: (0, 0)),
        out_shape=jax.ShapeDtypeStruct((H, NUM_BUCKETS), F32),
        compiler_params=_cparams(("arbitrary",)),
    )(dsums, onehot_t)


def _head_masks():
    lane = lax.broadcasted_iota(jnp.int32, (1, 2 * DH), 1)
    return (lane < DH, lane >= DH)


def _stack_heads(x, masks):
    zero = jnp.zeros_like(x)
    return jnp.concatenate([jnp.where(masks[0], x, zero), jnp.where(masks[1], x, zero)], axis=0)


def _deinterleave(src_ref, dst_ref, d, dtype):
    L = S // d
    for r in range(d):
        dst_ref[r * L:(r + 1) * L, :] = src_ref[pl.ds(r, L, stride=d), :].astype(dtype)


def _branch_loops(d, block):
    L = S // d
    for r in range(d):
        base = r * L
        block(base, base, BLK, True)
        for n in range(1, L // BLK):
            block(base + n * BLK, base + (n - 1) * BLK, 2 * BLK, False)


def attention_fwd(z3, bias3, name):
    W = 2 * DH
    CH = 256

    def body(q_ref, k_ref, v_ref, b_ref, o_ref, lse_ref, stage, qd, kd, vd, od, ld, on, ln):
        masks = _head_masks()
        for src, dst in ((q_ref, qd), (k_ref, kd), (v_ref, vd)):
            stage[...] = src[...].astype(F32)
            for gi, d in enumerate(BRANCH_DILATIONS[1:]):
                _deinterleave(stage, dst.at[gi], d, BF)

        for g, d in enumerate(BRANCH_DILATIONS):
            qs, ks, vs = (q_ref, k_ref, v_ref) if d == 1 else (qd.at[g - 1], kd.at[g - 1], vd.at[g - 1])
            o_dst, l_dst = (on.at[0], ln.at[0]) if d == 1 else (od, ld)

            def block(q0, k0, nk, first, g=g, qs=qs, ks=ks, vs=vs, o_dst=o_dst, l_dst=l_dst):
                q2 = _stack_heads(qs[pl.ds(q0, BLK), :], masks)
                kk = ks[pl.ds(k0, nk), :]
                vv = vs[pl.ds(k0, nk), :]
                bias = b_ref[g][:, BLK:] if first else b_ref[g]
                s = lax.dot_general(q2, kk, _NT, preferred_element_type=F32) * SCALE + bias
                mx = jnp.max(s, axis=1, keepdims=True)
                p = jnp.exp(s - mx)
                l = jnp.sum(p, axis=1, keepdims=True)
                o2 = jnp.dot(p.astype(BF), vv, preferred_element_type=F32) / l
                lse2 = mx + jnp.log(l)
                o_dst[pl.ds(q0, BLK), :] = jnp.where(masks[0], o2[:BLK], o2[BLK:])
                l_dst[pl.ds(q0, BLK), :] = jnp.where(masks[0], lse2[:BLK], lse2[BLK:])

            _branch_loops(d, block)
            if d > 1:
                L = S // d
                for r in range(d):
                    on[g, pl.ds(r, L, stride=d), :] = od[r * L:(r + 1) * L, :]
                    ln[g, pl.ds(r, L, stride=d), :] = ld[r * L:(r + 1) * L, :]

        def join(c, carry):
            rows = pl.ds(pl.multiple_of(c * CH, CH), CH)
            a, b, cc = ln[0, rows, :], ln[1, rows, :], ln[2, rows, :]
            mx = jnp.maximum(jnp.maximum(a, b), cc)
            ea, eb, ec = jnp.exp(a - mx), jnp.exp(b - mx), jnp.exp(cc - mx)
            tot = ea + eb + ec
            o_ref[rows, :] = ((ea * on[0, rows, :] + eb * on[1, rows, :] + ec * on[2, rows, :])
                              / tot).astype(BF)
            lse_ref[rows, :] = mx + jnp.log(tot)
            return carry
        lax.fori_loop(0, S // CH, join, 0)

    col = pl.BlockSpec((S, W), lambda hp: (0, hp))
    return pl.pallas_call(
        body, name=name,
        grid=(D // W,),
        in_specs=[pl.BlockSpec((None, S, W), lambda hp: (0, 0, hp)),
                  pl.BlockSpec((None, S, W), lambda hp: (1, 0, hp)),
                  pl.BlockSpec((None, S, W), lambda hp: (2, 0, hp)),
                  pl.BlockSpec((3, 2 * BLK, 2 * BLK), lambda hp: (0, hp, 0))],
        out_specs=[col, col],
        out_shape=[jax.ShapeDtypeStruct((S, D), BF), jax.ShapeDtypeStruct((S, D), F32)],
        scratch_shapes=[pltpu.VMEM((S, W), F32),
                        pltpu.VMEM((2, S, W), BF), pltpu.VMEM((2, S, W), BF), pltpu.VMEM((2, S, W), BF),
                        pltpu.VMEM((S, W), F32), pltpu.VMEM((S, W), F32),
                        pltpu.VMEM((3, S, W), F32), pltpu.VMEM((3, S, W), F32)],
        compiler_params=_cparams(("parallel",)),
    )(z3, z3, z3, bias3)


def attention_bwd(z3, dob, ob, lse_b, bias3, name, deps=()):
    W = 2 * DH
    CH = 256

    def body(q_ref, k_ref, v_ref, do_ref, o_ref, lse_ref, b_ref, dz_ref, dsum_ref,
             stage, delta, qd, kd, vd, dod, lsd, dld, res, acc, db_ref):
        masks = _head_masks()

        def rowsum(c, carry):
            rows = pl.ds(pl.multiple_of(c * CH, CH), CH)
            prod = do_ref[rows, :].astype(F32) * o_ref[rows, :].astype(F32)
            sa = jnp.sum(jnp.where(masks[0], prod, 0.0), axis=1, keepdims=True)
            sb = jnp.sum(jnp.where(masks[1], prod, 0.0), axis=1, keepdims=True)
            delta[rows, :] = jnp.where(masks[0], sa, sb)
            return carry
        lax.fori_loop(0, S // CH, rowsum, 0)

        for src, dst in ((q_ref, qd), (k_ref, kd), (v_ref, vd), (do_ref, dod)):
            stage[...] = src[...].astype(F32)
            for gi, d in enumerate(BRANCH_DILATIONS[1:]):
                _deinterleave(stage, dst.at[gi], d, BF)
        for gi, d in enumerate(BRANCH_DILATIONS[1:]):
            _deinterleave(lse_ref, lsd.at[gi], d, F32)
            _deinterleave(delta, dld.at[gi], d, F32)

        db_ref[...] = jnp.zeros_like(db_ref)
        for g, d in enumerate(BRANCH_DILATIONS):
            if d == 1:
                qs, ks, vs, dos, ls, dl = q_ref, k_ref, v_ref, do_ref, lse_ref, delta
            else:
                qs, ks, vs, dos = qd.at[g - 1], kd.at[g - 1], vd.at[g - 1], dod.at[g - 1]
                ls, dl = lsd.at[g - 1], dld.at[g - 1]
            res[1] = jnp.zeros((S, W), F32)
            res[2] = jnp.zeros((S, W), F32)

            def block(q0, k0, nk, first, g=g, qs=qs, ks=ks, vs=vs, dos=dos, ls=ls, dl=dl):
                kk = ks[pl.ds(k0, nk), :]
                vv = vs[pl.ds(k0, nk), :]
                q2 = _stack_heads(qs[pl.ds(q0, BLK), :], masks)
                do2 = _stack_heads(dos[pl.ds(q0, BLK), :], masks)
                lse_blk = ls[pl.ds(q0, BLK), :]
                del_blk = dl[pl.ds(q0, BLK), :]
                lse2 = jnp.concatenate([lse_blk[:, 0:1], lse_blk[:, DH:DH + 1]], axis=0)
                del2 = jnp.concatenate([del_blk[:, 0:1], del_blk[:, DH:DH + 1]], axis=0)
                bias = b_ref[g][:, BLK:] if first else b_ref[g]
                s = lax.dot_general(q2, kk, _NT, preferred_element_type=F32) * SCALE + bias
                p = jnp.exp(s - lse2)
                dp = lax.dot_general(do2, vv, _NT, preferred_element_type=F32)
                ds = p * (dp - del2)
                if first:
                    db_ref[g, :, BLK:] += ds
                else:
                    db_ref[g] += ds
                dsb = ds.astype(BF)
                dq2 = jnp.dot(dsb, kk, preferred_element_type=F32) * SCALE
                res[0, pl.ds(q0, BLK), :] = jnp.where(masks[0], dq2[:BLK], dq2[BLK:])
                res[1, pl.ds(k0, nk), :] += lax.dot_general(dsb, q2, _TN,
                                                            preferred_element_type=F32) * SCALE
                res[2, pl.ds(k0, nk), :] += lax.dot_general(p.astype(BF), do2, _TN,
                                                            preferred_element_type=F32)

            _branch_loops(d, block)
            L = S // d
            for t in range(3):
                if d == 1:
                    acc[t] = res[t]
                else:
                    for r in range(d):
                        acc[t, pl.ds(r, L, stride=d), :] = (acc[t, pl.ds(r, L, stride=d), :]
                                                            + res[t, r * L:(r + 1) * L, :])
        for t in range(3):
            dz_ref[t] = acc[t].astype(BF)

        flip = (lax.broadcasted_iota(jnp.int32, (BLK, BLK), 0)
                + lax.broadcasted_iota(jnp.int32, (BLK, BLK), 1) == BLK - 1).astype(BF)
        dsum_ref[...] = jnp.zeros_like(dsum_ref)
        for g in range(3):
            for hh in range(2):
                halves = []
                for half in range(2):
                    tile = db_ref[g, hh * BLK:(hh + 1) * BLK, half * BLK:(half + 1) * BLK]
                    hi = tile.astype(BF)
                    lo = (tile - hi.astype(F32)).astype(BF)
                    rev = (jnp.dot(hi, flip, preferred_element_type=F32)
                           + jnp.dot(lo, flip, preferred_element_type=F32))
                    skew = pltpu.roll(rev, 0, 1, stride=1, stride_axis=0)
                    halves.append(jnp.sum(skew, axis=0, keepdims=True))
                dsum_ref[2 * g + hh:2 * g + hh + 1, :] = jnp.concatenate(halves, axis=1)

    col = pl.BlockSpec((S, W), lambda hp: (0, hp))
    return pl.pallas_call(
        _after(body, 7, deps), name=name,
        grid=(D // W,),
        in_specs=[pl.BlockSpec((None, S, W), lambda hp: (0, 0, hp)),
                  pl.BlockSpec((None, S, W), lambda hp: (1, 0, hp)),
                  pl.BlockSpec((None, S, W), lambda hp: (2, 0, hp)),
                  col, col, col,
                  pl.BlockSpec((3, 2 * BLK, 2 * BLK), lambda hp: (0, hp, 0))] + [ANY] * len(deps),
        out_specs=[pl.BlockSpec((3, S, W), lambda hp: (0, 0, hp)),
                   pl.BlockSpec((None, 8, 2 * BLK), lambda hp: (hp, 0, 0))],
        out_shape=[jax.ShapeDtypeStruct((3, S, D), BF),
                   jax.ShapeDtypeStruct((D // W, 8, 2 * BLK), F32)],
        scratch_shapes=[pltpu.VMEM((S, W), F32), pltpu.VMEM((S, W), F32),
                        pltpu.VMEM((2, S, W), BF), pltpu.VMEM((2, S, W), BF),
                        pltpu.VMEM((2, S, W), BF), pltpu.VMEM((2, S, W), BF),
                        pltpu.VMEM((2, S, W), F32), pltpu.VMEM((2, S, W), F32),
                        pltpu.VMEM((3, S, W), F32), pltpu.VMEM((3, S, W), F32),
                        pltpu.VMEM((3, 2 * BLK, 2 * BLK), F32)],
        compiler_params=_cparams(("parallel",)),
    )(z3, z3, z3, dob, ob, lse_b, bias3, *deps)


def _me():
    return lax.axis_index("x"), lax.axis_index("y"), lax.axis_index("c")


def _other_chips(x, y):
    return [(1 - x, y), (x, 1 - y), (1 - x, 1 - y)]


def _shard_window(ref, axis, t, shape):
    R, C = shape
    if axis == 0:
        return ref.at[pl.ds(pl.multiple_of(t * R, 128), R), :]
    return ref.at[:, pl.ds(pl.multiple_of(t * C, 128), C)]


def all_gather_weights(shards, axes, name):
    n = len(shards)
    shapes = [s.shape for s in shards]
    outs_shape = [jax.ShapeDtypeStruct((8 * R, C) if ax == 0 else (R, 8 * C), BF)
                  for (R, C), ax in zip(shapes, axes)]

    def body(*refs):
        ins, outs = refs[:n], refs[n:2 * n]
        send_sems, recv_sems, local_sems = refs[2 * n:]
        x, y, c = _me()
        me, sibling = (x, y, c), (x, y, 1 - c)
        xnb, ynb, diag = (1 - x, y), (x, 1 - y), (1 - x, 1 - y)
        south = c == 0
        relay_from = (jnp.where(south, x, 1 - x), jnp.where(south, 1 - y, y))
        relay_to = (jnp.where(south, 1 - x, x), jnp.where(south, y, 1 - y))
        barrier = pltpu.get_barrier_semaphore()
        for peer in [sibling, (*xnb, c), (*ynb, c)]:
            pl.semaphore_signal(barrier, inc=1, device_id=peer, device_id_type=MESH)
        pl.semaphore_wait(barrier, 3)

        def win(i, px, py, pc):
            return _shard_window(outs[i], axes[i], 4 * px + 2 * py + pc, shapes[i])

        def copy(i, k, block, to, src=None):
            return pltpu.make_async_remote_copy(
                src_ref=win(i, *block) if src is None else src, dst_ref=win(i, *block),
                send_sem=send_sems.at[i * 7 + k], recv_sem=recv_sems.at[i * 7 + k],
                device_id=to, device_id_type=MESH)

        mine = [pltpu.make_async_copy(ins[i], win(i, *me), local_sems.at[i]) for i in range(n)]
        for cp in mine:
            cp.start()
        sent = []
        for i in range(n):
            sent += [copy(i, 0, me, sibling, src=ins[i]), copy(i, 1, me, (*xnb, c), src=ins[i]),
                     copy(i, 2, me, (*ynb, c), src=ins[i])]
        for cp in sent:
            cp.start()
        for i in range(n):
            for k, chip in ((1, xnb), (2, ynb)):
                copy(i, k, (*chip, c), me).wait_recv()
                sent.append(copy(i, 3 + k, (*chip, c), sibling))
                sent[-1].start()
            sent.append(copy(i, 3, (*relay_from, c), (*relay_to, c)))
            sent[-1].start()
        for i in range(n):
            copy(i, 3, (*diag, c), me).wait_recv()
            sent.append(copy(i, 6, (*diag, c), sibling))
            sent[-1].start()
        for i in range(n):
            copy(i, 0, sibling, me).wait_recv()
            for k, chip in ((4, xnb), (5, ynb), (6, diag)):
                copy(i, k, (*chip, 1 - c), me).wait_recv()
        for cp in sent:
            cp.wait_send()
        for cp in mine:
            cp.wait()

    return pl.kernel(
        body, out_type=outs_shape, name=name,
        mesh=plsc.ScalarSubcoreMesh(axis_name="sequencer", num_cores=1),
        scratch_types=[pltpu.SemaphoreType.DMA((7 * n,)), pltpu.SemaphoreType.DMA((7 * n,)),
                       pltpu.SemaphoreType.DMA((n,))],
        compiler_params=pltpu.CompilerParams(collective_id=1),
    )(*shards)


def pair_exchange_grads(grads, axes, shapes, name):
    n = len(grads)

    def body(*refs):
        ins, outs = refs[:n], refs[n:2 * n]
        send_sems, recv_sems = refs[2 * n:]
        x, y, c = _me()
        sibling = (x, y, 1 - c)
        barrier = pltpu.get_barrier_semaphore()
        pl.semaphore_signal(barrier, inc=1, device_id=sibling, device_id_type=MESH)
        pl.semaphore_wait(barrier, 1)
        copies = []
        for i in range(n):
            for q in range(4):
                t = 2 * q + (1 - c)
                copies.append(pltpu.make_async_remote_copy(
                    src_ref=_shard_window(ins[i], axes[i], t, shapes[i]), dst_ref=outs[i].at[q],
                    send_sem=send_sems.at[i * 4 + q], recv_sem=recv_sems.at[i * 4 + q],
                    device_id=sibling, device_id_type=MESH))
        for cp in copies:
            cp.start()
        for cp in copies:
            cp.wait_recv()
        for cp in copies:
            cp.wait_send()

    return pl.kernel(
        body, out_type=[jax.ShapeDtypeStruct((4,) + tuple(sh), BF) for sh in shapes], name=name,
        mesh=plsc.ScalarSubcoreMesh(axis_name="sequencer", num_cores=1),
        scratch_types=[pltpu.SemaphoreType.DMA((4 * n,)), pltpu.SemaphoreType.DMA((4 * n,))],
        compiler_params=pltpu.CompilerParams(collective_id=2),
    )(*grads)


def pair_add(grads, landed, axes, shapes, c_idx, name, deps=()):
    n = len(grads)

    def body(c_ref, *refs):
        for t in range(n):
            refs[2 * n + t][...] = (refs[2 * t][...].astype(F32)
                                    + refs[2 * t + 1][...].astype(F32)).astype(BF)

    halves = 1
    in_specs, out_specs, out_shapes, operands = [], [], [], []
    for t in range(n):
        R, C = shapes[t]
        rh = R // halves
        if axes[t] == 0:
            in_specs.append(pl.BlockSpec(
                (rh, C), lambda q, h, c_ref: (halves * (2 * q + c_ref[0]) + h, 0)))
        else:
            in_specs.append(pl.BlockSpec((rh, C), lambda q, h, c_ref: (h, 2 * q + c_ref[0])))
        blk = pl.BlockSpec((None, rh, C), lambda q, h, c_ref: (q, h, 0))
        in_specs.append(blk)
        out_specs.append(blk)
        out_shapes.append(jax.ShapeDtypeStruct((4, R, C), BF))
        operands += [grads[t], landed[t]]
    return pl.pallas_call(
        _after(body, 1 + 2 * n, deps), name=name,
        grid_spec=pltpu.PrefetchScalarGridSpec(
            num_scalar_prefetch=1, grid=(4, halves), in_specs=in_specs + [ANY] * len(deps),
            out_specs=out_specs),
        out_shape=out_shapes,
        compiler_params=_cparams(("parallel", "parallel")),
    )(c_idx, *operands, *deps)


def chip_exchange_grads(parts, name):
    n = len(parts)

    def body(*refs):
        ins, outs, relay = refs[:n], refs[n:2 * n], refs[2 * n:3 * n]
        send_sems, recv_sems = refs[3 * n:]
        x, y, c = _me()
        xnb, ynb, diag = (1 - x, y), (x, 1 - y), (1 - x, 1 - y)
        south = c == 0
        via = (jnp.where(south, 1 - x, x), jnp.where(south, y, 1 - y))
        onward = (jnp.where(south, x, 1 - x), jnp.where(south, 1 - y, y))
        barrier = pltpu.get_barrier_semaphore()
        for peer in (xnb, ynb):
            pl.semaphore_signal(barrier, inc=1, device_id=(*peer, c), device_id_type=MESH)
        pl.semaphore_wait(barrier, 2)

        def copy(i, k, src, dst, to):
            return pltpu.make_async_remote_copy(
                src_ref=src, dst_ref=dst, send_sem=send_sems.at[i * 4 + k],
                recv_sem=recv_sems.at[i * 4 + k], device_id=(*to, c), device_id_type=MESH)

        sent = []
        for i in range(n):
            sent += [copy(i, 0, ins[i].at[2 * xnb[0] + xnb[1]], outs[i].at[0], xnb),
                     copy(i, 1, ins[i].at[2 * ynb[0] + ynb[1]], outs[i].at[1], ynb),
                     copy(i, 2, ins[i].at[2 * diag[0] + diag[1]], relay[i], via)]
        for cp in sent:
            cp.start()
        for i in range(n):
            copy(i, 2, relay[i], relay[i], via).wait_recv()
            sent.append(copy(i, 3, relay[i], outs[i].at[2], onward))
            sent[-1].start()
        for i in range(n):
            copy(i, 0, outs[i].at[0], outs[i].at[0], xnb).wait_recv()
            copy(i, 1, outs[i].at[1], outs[i].at[1], ynb).wait_recv()
            copy(i, 3, outs[i].at[2], outs[i].at[2], onward).wait_recv()
        for cp in sent:
            cp.wait_send()

    landing = [jax.ShapeDtypeStruct((3,) + tuple(p.shape[1:]), BF) for p in parts]
    staging = [jax.ShapeDtypeStruct(tuple(p.shape[1:]), BF) for p in parts]
    return pl.kernel(
        body, out_type=landing + staging, name=name,
        mesh=plsc.ScalarSubcoreMesh(axis_name="sequencer", num_cores=1),
        scratch_types=[pltpu.SemaphoreType.DMA((4 * n,)), pltpu.SemaphoreType.DMA((4 * n,))],
        compiler_params=pltpu.CompilerParams(collective_id=3),
    )(*parts)[:n]


def all_gather_small(v, name):
    R, C = v.shape

    def body(v_ref, out_ref, send_sems, recv_sems, local_sem):
        x, y, c = _me()
        me, sibling = (x, y, c), (x, y, 1 - c)
        chips = _other_chips(x, y)

        def slot(px, py, pc):
            return out_ref.at[4 * px + 2 * py + pc]

        def copy(k, block, to, src=None):
            return pltpu.make_async_remote_copy(
                src_ref=slot(*block) if src is None else src, dst_ref=slot(*block),
                send_sem=send_sems.at[k], recv_sem=recv_sems.at[k],
                device_id=to, device_id_type=MESH)

        mine = pltpu.make_async_copy(v_ref, slot(*me), local_sem)
        mine.start()
        first = [copy(0, me, sibling, src=v_ref)]
        first += [copy(1 + j, me, (*chip, c), src=v_ref) for j, chip in enumerate(chips)]
        for cp in first:
            cp.start()
        passed = [copy(4 + j, (*chip, c), sibling) for j, chip in enumerate(chips)]
        for j, chip in enumerate(chips):
            copy(1 + j, (*chip, c), me).wait_recv()
            passed[j].start()
        copy(0, sibling, me).wait_recv()
        for j, chip in enumerate(chips):
            copy(4 + j, (*chip, 1 - c), me).wait_recv()
        for cp in first + passed:
            cp.wait_send()
        mine.wait()

    return pl.pallas_call(
        body, name=name,
        in_specs=[pl.BlockSpec(memory_space=pltpu.VMEM)],
        out_specs=pl.BlockSpec(memory_space=pltpu.VMEM),
        out_shape=jax.ShapeDtypeStruct((NDEV, R, C), F32),
        scratch_shapes=[pltpu.SemaphoreType.DMA((7,)), pltpu.SemaphoreType.DMA((7,)),
                        pltpu.SemaphoreType.DMA],
    )(v)


def _adamw(w, g, m, v):
    m = ADAM_B1 * m + (1.0 - ADAM_B1) * g
    v = ADAM_B2 * v + (1.0 - ADAM_B2) * (g * g)
    m_hat = m / (1.0 - ADAM_B1 ** ADAM_STEP)
    v_hat = v / (1.0 - ADAM_B2 ** ADAM_STEP)
    delta = -ADAM_LR * (m_hat / (jnp.sqrt(v_hat) + ADAM_EPS) + ADAM_WD * w)
    return delta, m, v


def reduce_adamw(parts, landed, params, q_idx, name, prevs, deps=()):
    n = len(parts)
    halves = 2
    in_specs, out_specs, out_shapes, operands, extra, aliases = [], [], [], [], [], {}
    for t in range(n):
        R, C = parts[t].shape[1:]
        w, m, v, layer = params[t]
        r, c = w.shape[1:]
        tr = r // halves
        assert tr % 16 == 0 and c == C
        wspec = pl.BlockSpec((None, tr, c), lambda i, q_ref, layer=layer: (layer, i, 0))
        in_specs += [pl.BlockSpec((None, tr, C), lambda i, q_ref: (q_ref[0], i, 0)),
                     pl.BlockSpec((3, tr, C), lambda i, q_ref: (0, i, 0)), wspec, wspec, wspec]
        out_specs += [wspec] * 4
        out_shapes += [jax.ShapeDtypeStruct(w.shape, F32)] * 4
        operands += [parts[t], landed[t], w, m, v]
        for k, buf in enumerate(prevs[t]):
            aliases[1 + 5 * n + len(extra)] = 4 * t + k
            extra.append(buf)
    extra += list(deps)

    def body(q_ref, *refs):
        for t in range(n):
            p_ref, l_ref, w_ref, m_ref, v_ref = refs[5 * t:5 * t + 5]
            g = p_ref[...].astype(F32)
            for k in range(3):
                g = g + l_ref[k].astype(F32)
            d, mm, vv = _adamw(w_ref[...], g, m_ref[...], v_ref[...])
            outs = refs[5 * n + 4 * t:5 * n + 4 * t + 4]
            outs[0][...] = g
            outs[1][...] = d
            outs[2][...] = mm
            outs[3][...] = vv

    res = pl.pallas_call(
        _after(body, 1 + 5 * n, extra), name=name,
        grid_spec=pltpu.PrefetchScalarGridSpec(
            num_scalar_prefetch=1, grid=(halves,),
            in_specs=in_specs + [ANY] * len(extra), out_specs=out_specs),
        out_shape=out_shapes,
        input_output_aliases=aliases,
        compiler_params=_cparams(("parallel",)),
    )(q_idx, *operands, *extra)
    return [res[4 * t:4 * t + 4] for t in range(n)]


def small_reduce_adamw(gathered, w, m, v, name):
    R, C = w.shape

    def body(a_ref, w_ref, m_ref, v_ref, g_out, d_out, m_out, v_out):
        g = a_ref[0]
        for k in range(1, NDEV):
            g = g + a_ref[k]
        d, mm, vv = _adamw(w_ref[...], g, m_ref[...], v_ref[...])
        g_out[...] = g
        d_out[...] = d
        m_out[...] = mm
        v_out[...] = vv

    out = jax.ShapeDtypeStruct((R, C), F32)
    return pl.pallas_call(body, name=name, out_shape=[out] * 4,
                          compiler_params=_cparams())(gathered, w, m, v)


def _pad_cols(a, n):
    return jnp.pad(a, ((0, 0), (0, n - a.shape[1])))


def _pad_rows(a, n):
    return jnp.pad(a, ((0, n - a.shape[0]), (0, 0)))


SMALL_ROWS = 16


def _pack_small(mix, ffn, fin, taps_full, relb):
    return jnp.concatenate([
        mix, ffn, fin.reshape(1, D), taps_full.reshape(6, D),
        jnp.pad(relb.reshape(1, NUM_BUCKETS * H), ((0, 0), (0, D - NUM_BUCKETS * H)))], axis=0)


def kernel(x, mix_norm, ffn_norm, final_norm, conv_w_in, conv_kernel, conv_w_out, attn_w_qkv, attn_w_out, rel_bias, ffn_w_gate, ffn_w_up, ffn_w_down, loss_target, m_mix_norm, m_ffn_norm, m_final_norm, m_conv_w_in, m_conv_kernel, m_conv_w_out, m_attn_w_qkv, m_attn_w_out, m_rel_bias, m_ffn_w_gate, m_ffn_w_up, m_ffn_w_down, v_mix_norm, v_ffn_norm, v_final_norm, v_conv_w_in, v_conv_kernel, v_conv_w_out, v_attn_w_qkv, v_attn_w_out, v_rel_bias, v_ffn_w_gate, v_ffn_w_up, v_ffn_w_down):
    xi, yi, ci = _me()
    me = 4 * xi + 2 * yi + ci
    c_idx = jnp.reshape(ci, (1,)).astype(jnp.int32)
    q_idx = jnp.reshape(2 * xi + yi, (1,)).astype(jnp.int32)
    col0 = me * (D // NDEV)

    taps_local = jnp.zeros((2, 3, D), F32)
    taps_local = lax.dynamic_update_slice(taps_local, conv_kernel, (0, 0, col0))
    taps_pack = jnp.pad(taps_local.reshape(6, D), ((0, 2), (0, 0)))
    taps_all = all_gather_small(taps_pack, "ag_taps")
    taps_sum = jnp.sum(taps_all, axis=0)
    taps = [jnp.pad(taps_sum[3 * j:3 * j + 3], ((0, 5), (0, 0))) for j in range(2)]

    gate_t, up_t = jnp.swapaxes(ffn_w_gate, 1, 2), jnp.swapaxes(ffn_w_up, 1, 2)
    m_gate_t, m_up_t = jnp.swapaxes(m_ffn_w_gate, 1, 2), jnp.swapaxes(m_ffn_w_up, 1, 2)
    v_gate_t, v_up_t = jnp.swapaxes(v_ffn_w_gate, 1, 2), jnp.swapaxes(v_ffn_w_up, 1, 2)

    mixer_in = (conv_w_in, attn_w_qkv)
    mixer_out = (conv_w_out, attn_w_out)
    wts = []
    for i in range(DEPTH):
        j = i // 2
        shards = [mixer_in[i % 2][j].astype(BF), mixer_out[i % 2][j].astype(BF),
                  _pad_rows(gate_t[i].astype(BF), FF_SHARD_PAD),
                  _pad_rows(up_t[i].astype(BF), FF_SHARD_PAD),
                  _pad_rows(ffn_w_down[i].astype(BF), FF_SHARD_PAD)]
        axes = (1, 0, 0, 0, 0)
        groups = ((0, 1), (1, 2), (2, 4), (4, 5)) if i == 0 else ((0, 2), (2, 5))
        layer = []
        for lo, hi in groups:
            layer += list(all_gather_weights(shards[lo:hi], axes[lo:hi], f"ag_l{i}_{lo}"))
        wts.append(layer)

    onehot_t, band = _bucket_onehot_t()
    bias3 = bias_tables(rel_bias.T, onehot_t, band, "bias_tables").reshape(3, H * BLK, 2 * BLK)

    saved = []
    xc = x[0]
    for i in range(DEPTH):
        w_in, w_out, w_g, w_u, w_d = wts[i]
        j = i // 2
        x_mix = xc
        z3, h_mix = norm_matmul3(xc, mix_norm[i:i + 1], w_in, f"mix_in_l{i}")
        if i % 2 == 0:
            act = conv_fwd(z3, taps[j], f"conv_fwd_l{i}")
            lse_b = None
        else:
            act, lse_b = attention_fwd(z3, bias3, f"attn_fwd_l{i}")
        xc = matmul_residual(act, w_out, xc, f"mix_out_l{i}")
        x_ffn = xc
        g, u, a, h_ffn = norm_swiglu_up(xc, ffn_norm[i:i + 1], w_g, w_u, f"ffn_up_l{i}")
        xc = matmul_residual(a, w_d, xc, f"ffn_down_l{i}")
        saved.append((x_mix, h_mix, z3, act, lse_b, x_ffn, h_ffn, g, u, a))

    dx, dxb, dg_final, sq = loss_head(xc, final_norm.reshape(1, D), loss_target[0], "loss_head")
    loss = lax.psum(0.5 * jnp.sum(sq[0]) / D, ("x", "y", "c"))

    dg_mix = [None] * DEPTH
    dg_ffn = [None] * DEPTH
    dtaps = [None, None]
    dbias_all = []
    shape_in, shape_out = (D, 3 * D // NDEV), (D // NDEV, D)
    ffn_axes, ffn_shapes = (0, 0, 0), ((FF_SHARD_PAD, D),) * 3
    stacked = {}

    def pair_stage(grads, landed1, axes, shapes, tag, tok):
        parts = pair_add(grads, landed1, axes, shapes, c_idx, f"rs_add_{tag}", deps=[tok])
        return parts, chip_exchange_grads(parts, f"rs_chip_{tag}"), parts[-1]

    def adamw_stage(parts, landed2, params, tag, tok):
        names = [p[0] for p in params]
        res = reduce_adamw(parts, landed2, [p[1:] for p in params], q_idx, f"adamw_{tag}",
                           [stacked.get(nm, ()) for nm in names], deps=[tok])
        for nm, r4 in zip(names, res):
            stacked[nm] = r4
        return res[-1][0]

    tok = dxb
    mix_wait = None
    mix_chip = None
    ffn_chip = []
    for i in reversed(range(DEPTH)):
        w_in, w_out, w_g, w_u, w_d = wts[i]
        j = i // 2
        x_mix, h_mix, z3, act, lse_b, x_ffn, h_ffn, g, u, a = saved[i]
        ffn_params = [("ffn_w_gate", gate_t, m_gate_t, v_gate_t, i),
                      ("ffn_w_up", up_t, m_up_t, v_up_t, i),
                      ("ffn_w_down", ffn_w_down, m_ffn_w_down, v_ffn_w_down, i)]
        if i % 2 == 0:
            mix_params = [("conv_w_in", conv_w_in, m_conv_w_in, v_conv_w_in, j),
                          ("conv_w_out", conv_w_out, m_conv_w_out, v_conv_w_out, j)]
        else:
            mix_params = [("attn_w_qkv", attn_w_qkv, m_attn_w_qkv, v_attn_w_qkv, j),
                          ("attn_w_out", attn_w_out, m_attn_w_out, v_attn_w_out, j)]
        dgate, dup = swiglu_bwd_da(dxb, w_d, g, u, f"ffn_da_l{i}", deps=[tok])
        tok = dgate
        if mix_wait is not None:
            grads_m, landed1_m, params_m, tag_m = mix_wait
            parts_m, landed2_m, tok = pair_stage(grads_m, landed1_m, (1, 0), (shape_in, shape_out),
                                                 tag_m, tok)
            mix_chip = (parts_m, landed2_m, params_m, tag_m)
            mix_wait = None
        products = [(dgate, h_ffn), (dup, h_ffn), (a, dxb)]
        ffn_wait = []
        for gi, idxs in enumerate(((0,), (1, 2)) if i == 0 else ((0, 1, 2),)):
            grads_f = matmul_tn_group([products[k] for k in idxs], f"ffn_dw_l{i}_{gi}", deps=[tok])
            landed1_f = pair_exchange_grads(grads_f, ffn_axes[:len(idxs)], ffn_shapes[:len(idxs)],
                                            f"rs_pair_f{i}_{gi}")
            ffn_wait.append((grads_f, landed1_f, [ffn_params[k] for k in idxs], f"f{i}_{gi}"))
            tok = grads_f[-1]
        for group in ffn_chip:
            tok = adamw_stage(*group, tok)
        dx, dxb, dg_ffn[i] = matmul_normbwd(
            [(dgate, w_g, False), (dup, w_u, False)], x_ffn, ffn_norm[i:i + 1], dx, f"ffn_dh_l{i}",
            deps=[tok])
        dxb_mix = dxb
        tok = dxb
        ffn_chip = []
        for grads_f, landed1_f, params_f, tag_f in ffn_wait:
            nf = len(grads_f)
            parts_f, landed2_f, tok = pair_stage(grads_f, landed1_f, ffn_axes[:nf], ffn_shapes[:nf],
                                                 tag_f, tok)
            ffn_chip.append((parts_f, landed2_f, params_f, tag_f))
        dact = matmul_nt(dxb, w_out, f"mix_dact_l{i}", out_dtype=F32 if i % 2 == 0 else BF,
                         deps=[tok])
        if i % 2 == 0:
            dz3, dtaps[j] = conv_bwd(dact, z3, taps[j], f"conv_bwd_l{i}")
        else:
            dz3, dsum = attention_bwd(z3, dact, act, lse_b, bias3, f"attn_bwd_l{i}")
            dbias_all.append(dsum[:, :6].reshape(H // 2, 3, 2, 2 * BLK).transpose(1, 0, 2, 3)
                             .reshape(3, H, 2 * BLK))
        grads_m = matmul_tn_group([(h_mix, dz3), (act, dxb_mix)], f"mix_dw_l{i}")
        landed1_m = pair_exchange_grads(grads_m, (1, 0), (shape_in, shape_out), f"rs_pair_m{i}")
        mix_wait = (grads_m, landed1_m, mix_params, f"m{i}")
        tok = grads_m[-1]
        if mix_chip is not None:
            tok = adamw_stage(*mix_chip, tok)
            mix_chip = None
        dx, dxb, dg_mix[i] = matmul_normbwd(
            [(dz3, w_in, True)], x_mix, mix_norm[i:i + 1], dx, f"mix_dh_l{i}", deps=[tok])
        tok = dxb
    grads_m, landed1_m, params_m, tag_m = mix_wait
    parts_m, landed2_m, tok = pair_stage(grads_m, landed1_m, (1, 0), (shape_in, shape_out), tag_m, tok)
    for group in ffn_chip:
        tok = adamw_stage(*group, tok)

    grad_relb_t = bias_grad(jnp.concatenate(dbias_all), _diagonal_onehot_t(), "bias_grad")
    dtaps_full = jnp.stack([dtaps[0][:3], dtaps[1][:3]])
    g_small = _pack_small(jnp.concatenate([d[0:1] for d in dg_mix], axis=0),
                          jnp.concatenate([d[0:1] for d in dg_ffn], axis=0),
                          dg_final[0], dtaps_full, grad_relb_t.T)
    gathered = all_gather_small(g_small, "ag_small_grads")

    def taps_at_cols(k):
        return lax.dynamic_update_slice(jnp.zeros((2, 3, D), F32), k, (0, 0, col0))

    w_small = _pack_small(mix_norm, ffn_norm, final_norm, taps_at_cols(conv_kernel), rel_bias)
    m_small = _pack_small(m_mix_norm, m_ffn_norm, m_final_norm, taps_at_cols(m_conv_kernel), m_rel_bias)
    v_small = _pack_small(v_mix_norm, v_ffn_norm, v_final_norm, taps_at_cols(v_conv_kernel), v_rel_bias)
    small = small_reduce_adamw(gathered, w_small, m_small, v_small, "adamw_small")

    def unpack_small(p):
        taps_p = lax.dynamic_slice(p[9:15].reshape(2, 3, D), (0, 0, col0), (2, 3, D // NDEV))
        return {"mix_norm": p[0:4], "ffn_norm": p[4:8], "final_norm": p[8],
                "conv_kernel": taps_p, "rel_bias": p[15, :NUM_BUCKETS * H].reshape(NUM_BUCKETS, H)}

    small_out = [unpack_small(p) for p in small]
    adamw_stage(parts_m, landed2_m, params_m, tag_m, small[0])

    names = ["mix_norm", "ffn_norm", "final_norm", "conv_w_in", "conv_kernel", "conv_w_out",
             "attn_w_qkv", "attn_w_out", "rel_bias", "ffn_w_gate", "ffn_w_up", "ffn_w_down"]
    outs = [loss, dx.reshape(1, S, D)]
    for o in range(4):
        for nme in names:
            if nme in ("ffn_w_gate", "ffn_w_up"):
                outs.append(jnp.swapaxes(stacked[nme][o], 1, 2))
            else:
                outs.append(stacked[nme][o] if nme in stacked else small_out[o][nme])
    return tuple(outs)
```

```python
import math

import numpy as np
import jax
import jax.numpy as jnp
from jax import lax
from jax.experimental import pallas as pl
from jax.experimental.pallas import tpu as pltpu
from jax.experimental.pallas import tpu_sc as plsc

S = 2048
D = 1024
H = 16
DH = 64
DFF = 2816
NDEV = 8
DEPTH = 4
FF_SHARD = DFF // NDEV
FF_SHARD_PAD = 384
DFF_PAD = FF_SHARD_PAD * NDEV
BLK = 128
BRANCH_DILATIONS = (1, 4, 16)
NUM_BUCKETS = 32
MAX_DISTANCE = 2048
EPS = 1e-6
NEG_INF = -1e30
SCALE = DH ** -0.5

ADAM_LR = 0.001
ADAM_B1 = 0.9
ADAM_B2 = 0.999
ADAM_EPS = 1e-08
ADAM_WD = 0.01
ADAM_STEP = 10

BF = jnp.bfloat16
F32 = jnp.float32
VMEM_LIMIT_BYTES = 56 * 1024 * 1024
KSPLIT = 512
NORM_CHUNK = 256
MESH = pl.DeviceIdType.MESH
ANY = pl.BlockSpec(memory_space=pl.ANY)

_NT = (((1,), (1,)), ((), ()))
_TN = (((0,), (0,)), ((), ()))


def _cparams(sem=None):
    return pltpu.CompilerParams(dimension_semantics=sem, vmem_limit_bytes=VMEM_LIMIT_BYTES)


def _after(body, n, deps):
    nd = len(deps)
    if nd == 0:
        return body

    def ordered(*refs):
        body(*refs[:n], *refs[n + nd:])
    return ordered


def _rms(x):
    return lax.rsqrt(jnp.mean(x * x, axis=-1, keepdims=True) + EPS)


def norm_matmul3(x, gain, w, name, tm=1024, tn=1024):
    per = D // tn

    def body(x_ref, g_ref, w_ref, z_ref, h_ref, hs_ref):
        @pl.when(pl.program_id(1) == 0)
        def _():
            for c in range(tm // NORM_CHUNK):
                rows = slice(c * NORM_CHUNK, (c + 1) * NORM_CHUNK)
                xv = x_ref[rows, :]
                hv = (xv * _rms(xv) * g_ref[...]).astype(BF)
                hs_ref[rows, :] = hv
                h_ref[rows, :] = hv
                z_ref[rows, :] = jnp.dot(hv, w_ref[...], preferred_element_type=F32).astype(BF)

        @pl.when(pl.program_id(1) > 0)
        def _():
            z_ref[...] = jnp.dot(hs_ref[...], w_ref[...], preferred_element_type=F32).astype(BF)

    return pl.pallas_call(
        body, name=name,
        grid=(S // tm, 3 * D // tn),
        in_specs=[pl.BlockSpec((tm, D), lambda i, j: (i, 0)),
                  pl.BlockSpec((1, D), lambda i, j: (0, 0)),
                  pl.BlockSpec((D, tn), lambda i, j: (0, j))],
        out_specs=[pl.BlockSpec((None, tm, tn), lambda i, j: (j // per, i, j % per)),
                   pl.BlockSpec((tm, D), lambda i, j: (i, 0))],
        out_shape=[jax.ShapeDtypeStruct((3, S, D), BF), jax.ShapeDtypeStruct((S, D), BF)],
        scratch_shapes=[pltpu.VMEM((tm, D), BF)],
        compiler_params=_cparams(("parallel", "arbitrary")),
    )(x, gain, w)


def norm_swiglu_up(x, gain, wg_t, wu_t, name, tm=1024, tn=768):
    def body(x_ref, g_ref, wg_ref, wu_ref, go_ref, uo_ref, ao_ref, h_ref, hs_ref):
        def gate_up(hv, rows):
            g = lax.dot_general(hv, wg_ref[...], _NT, preferred_element_type=F32)
            u = lax.dot_general(hv, wu_ref[...], _NT, preferred_element_type=F32)
            go_ref[rows, :] = g.astype(BF)
            uo_ref[rows, :] = u.astype(BF)
            ao_ref[rows, :] = (g * jax.nn.sigmoid(g) * u).astype(BF)

        @pl.when(pl.program_id(1) == 0)
        def _():
            for c in range(tm // NORM_CHUNK):
                rows = slice(c * NORM_CHUNK, (c + 1) * NORM_CHUNK)
                xv = x_ref[rows, :]
                hv = (xv * _rms(xv) * g_ref[...]).astype(BF)
                hs_ref[rows, :] = hv
                h_ref[rows, :] = hv
                gate_up(hv, rows)

        @pl.when(pl.program_id(1) > 0)
        def _():
            gate_up(hs_ref[...], slice(None))

    act = jax.ShapeDtypeStruct((S, DFF_PAD), BF)
    blk = pl.BlockSpec((tm, tn), lambda i, j: (i, j))
    return pl.pallas_call(
        body, name=name,
        grid=(S // tm, DFF_PAD // tn),
        in_specs=[pl.BlockSpec((tm, D), lambda i, j: (i, 0)),
                  pl.BlockSpec((1, D), lambda i, j: (0, 0)),
                  pl.BlockSpec((tn, D), lambda i, j: (j, 0)),
                  pl.BlockSpec((tn, D), lambda i, j: (j, 0))],
        out_specs=[blk, blk, blk, pl.BlockSpec((tm, D), lambda i, j: (i, 0))],
        out_shape=[act, act, act, jax.ShapeDtypeStruct((S, D), BF)],
        scratch_shapes=[pltpu.VMEM((tm, D), BF)],
        compiler_params=_cparams(("parallel", "arbitrary")),
    )(x, gain, wg_t, wu_t)


def matmul_residual(a, w, x, name, tm=1024):
    K = a.shape[1]
    tn = D if K <= D else D // 2
    ns = K // KSPLIT
    kc = K // ns

    def body(*refs):
        x_ref, o_ref = refs[2 * ns:]
        acc = x_ref[...]
        for s in range(ns):
            acc = acc + jnp.dot(refs[s][...], refs[ns + s][...], preferred_element_type=F32)
        o_ref[...] = acc

    return pl.pallas_call(
        body, name=name,
        grid=(S // tm, D // tn),
        in_specs=[pl.BlockSpec((tm, kc), lambda i, j, s=s: (i, s)) for s in range(ns)]
        + [pl.BlockSpec((kc, tn), lambda i, j, s=s: (s, j)) for s in range(ns)]
        + [pl.BlockSpec((tm, tn), lambda i, j: (i, j))],
        out_specs=pl.BlockSpec((tm, tn), lambda i, j: (i, j)),
        out_shape=jax.ShapeDtypeStruct((S, D), F32),
        compiler_params=_cparams(("parallel", "parallel")),
    )(*([a] * ns), *([w] * ns), x)


def matmul_nt(a, w, name, out_dtype=BF, tm=1024, tn=1024, deps=()):
    K = a.shape[1]
    N = w.shape[0]

    def body(a_ref, w_ref, o_ref):
        o_ref[...] = lax.dot_general(a_ref[...], w_ref[...], _NT,
                                     preferred_element_type=F32).astype(o_ref.dtype)

    return pl.pallas_call(
        _after(body, 2, deps), name=name,
        grid=(S // tm, N // tn),
        in_specs=[pl.BlockSpec((tm, K), lambda i, j: (i, 0)),
                  pl.BlockSpec((tn, K), lambda i, j: (j, 0))] + [ANY] * len(deps),
        out_specs=pl.BlockSpec((tm, tn), lambda i, j: (i, j)),
        out_shape=jax.ShapeDtypeStruct((S, N), out_dtype),
        compiler_params=_cparams(("parallel", "parallel")),
    )(a, w, *deps)


def swiglu_bwd_da(dxb, wd, g, u, name, tm=1024, tn=768, deps=()):
    def body(dx_ref, w_ref, g_ref, u_ref, dg_ref, du_ref):
        da = lax.dot_general(dx_ref[...], w_ref[...], _NT, preferred_element_type=F32)
        gv = g_ref[...].astype(F32)
        uv = u_ref[...].astype(F32)
        sig = jax.nn.sigmoid(gv)
        dg_ref[...] = (da * uv * (sig * (1.0 + gv * (1.0 - sig)))).astype(BF)
        du_ref[...] = (da * (gv * sig)).astype(BF)

    act = jax.ShapeDtypeStruct((S, DFF_PAD), BF)
    blk = pl.BlockSpec((tm, tn), lambda i, j: (i, j))
    return pl.pallas_call(
        _after(body, 4, deps), name=name,
        grid=(S // tm, DFF_PAD // tn),
        in_specs=[pl.BlockSpec((tm, D), lambda i, j: (i, 0)),
                  pl.BlockSpec((tn, D), lambda i, j: (j, 0)),
                  blk, blk] + [ANY] * len(deps),
        out_specs=[blk, blk],
        out_shape=[act, act],
        compiler_params=_cparams(("parallel", "parallel")),
    )(dxb, wd, g, u, *deps)


def matmul_tn_group(pairs, name, tm=1024, tn=512, deps=()):
    P = len(pairs)
    steps = []
    for p, (a, b) in enumerate(pairs):
        N = 3 * D if b.ndim == 3 else b.shape[1]
        steps += [(p, i, j) for i in range(a.shape[1] // tm) for j in range(N // tn)]
    T = len(steps)
    tab = np.zeros((T, 1 + 2 * P), np.int32)
    for p in range(P):
        cur = (0, 0)
        for s, (ph, i, j) in enumerate(steps):
            if ph == p:
                cur = (i, j)
            tab[s, 1 + 2 * p:3 + 2 * p] = cur
    tab[:, 0] = [ph for ph, _, _ in steps]

    in_specs, out_specs, out_shapes, operands = [], [], [], []
    per = D // tn
    for p, (a, b) in enumerate(pairs):
        ci, cj = 1 + 2 * p, 2 + 2 * p
        in_specs.append(pl.BlockSpec((S, tm), lambda s, t, ci=ci: (0, t[s, ci])))
        if b.ndim == 3:
            in_specs.append(pl.BlockSpec((None, S, tn),
                                         lambda s, t, cj=cj: (t[s, cj] // per, 0, t[s, cj] % per)))
            N = 3 * D
        else:
            in_specs.append(pl.BlockSpec((S, tn), lambda s, t, cj=cj: (0, t[s, cj])))
            N = b.shape[1]
        out_specs.append(pl.BlockSpec((tm, tn), lambda s, t, ci=ci, cj=cj: (t[s, ci], t[s, cj])))
        out_shapes.append(jax.ShapeDtypeStruct((a.shape[1], N), BF))
        operands += [a, b]

    def body(tab_ref, *refs):
        phase = tab_ref[pl.program_id(0), 0]
        for p in range(P):
            @pl.when(phase == p)
            def _(p=p):
                refs[2 * P + p][...] = lax.dot_general(
                    refs[2 * p][...], refs[2 * p + 1][...], _TN,
                    preferred_element_type=F32).astype(BF)

    return pl.pallas_call(
        _after(body, 1 + 2 * P, deps), name=name,
        grid_spec=pltpu.PrefetchScalarGridSpec(
            num_scalar_prefetch=1, grid=(T,), in_specs=in_specs + [ANY] * len(deps),
            out_specs=out_specs),
        out_shape=out_shapes,
        compiler_params=_cparams(("arbitrary",)),
    )(jnp.asarray(tab), *operands, *deps)


def matmul_normbwd(terms, x_in, gain, dx, name, tm=512, ch=256, deps=()):
    specs, operands = [], []
    for (a, w, stacked) in terms:
        if stacked:
            specs.append(pl.BlockSpec((3, tm, D), lambda i: (0, i, 0)))
        else:
            specs.append(pl.BlockSpec((tm, a.shape[1]), lambda i: (i, 0)))
        specs.append(pl.BlockSpec(w.shape, lambda i: (0, 0), pipeline_mode=pl.Buffered(1)))
        operands += [a, w]
    nt = len(terms)

    def body(*refs):
        aw = refs[:2 * nt]
        x_ref, g_ref, dx_ref, dxo_ref, dxb_ref, dg_ref = refs[2 * nt:]

        @pl.when(pl.program_id(0) == 0)
        def _():
            dg_ref[...] = jnp.zeros_like(dg_ref)

        dgain = None
        for c in range(tm // ch):
            rows = slice(c * ch, (c + 1) * ch)
            dh = None
            for t, (_, _, stacked) in enumerate(terms):
                a_ref, w_ref = aw[2 * t], aw[2 * t + 1]
                if stacked:
                    parts = [lax.dot_general(a_ref[k, rows, :], w_ref[:, k * D:(k + 1) * D], _NT,
                                             preferred_element_type=F32) for k in range(3)]
                else:
                    parts = [jnp.dot(a_ref[rows, :], w_ref[...], preferred_element_type=F32)]
                for p in parts:
                    dh = p if dh is None else dh + p
            xv = x_ref[rows, :]
            r = _rms(xv)
            xhat = xv * r
            part = jnp.sum(dh * xhat, axis=0, keepdims=True)
            dgain = part if dgain is None else dgain + part
            dxh = dh * g_ref[...]
            dxn = r * (dxh - xhat * jnp.mean(dxh * xhat, axis=-1, keepdims=True))
            out = dx_ref[rows, :] + dxn
            dxo_ref[rows, :] = out
            dxb_ref[rows, :] = out.astype(BF)
        dg_ref[0:1, :] += dgain

    row = pl.BlockSpec((tm, D), lambda i: (i, 0))
    return pl.pallas_call(
        _after(body, 2 * nt + 3, deps), name=name,
        grid=(S // tm,),
        in_specs=specs + [row, pl.BlockSpec((1, D), lambda i: (0, 0)), row] + [ANY] * len(deps),
        out_specs=[row, row, pl.BlockSpec((8, D), lambda i: (0, 0))],
        out_shape=[jax.ShapeDtypeStruct((S, D), F32), jax.ShapeDtypeStruct((S, D), BF),
                   jax.ShapeDtypeStruct((8, D), F32)],
        compiler_params=_cparams(("arbitrary",)),
    )(*operands, x_in, gain, dx, *deps)


def loss_head(x, gain, target, name, tm=512):
    def body(x_ref, g_ref, t_ref, dxo_ref, dxb_ref, dg_ref, sq_ref):
        @pl.when(pl.program_id(0) == 0)
        def _():
            dg_ref[...] = jnp.zeros_like(dg_ref)
            sq_ref[...] = jnp.zeros_like(sq_ref)
        xv = x_ref[...]
        r = _rms(xv)
        xhat = xv * r
        err = xhat * g_ref[...] - t_ref[...]
        sq_ref[0:1, :] += jnp.sum(err * err, axis=0, keepdims=True)
        dy = err * (1.0 / D)
        dg_ref[0:1, :] += jnp.sum(dy * xhat, axis=0, keepdims=True)
        dxh = dy * g_ref[...]
        out = r * (dxh - xhat * jnp.mean(dxh * xhat, axis=-1, keepdims=True))
        dxo_ref[...] = out
        dxb_ref[...] = out.astype(BF)

    row = pl.BlockSpec((tm, D), lambda i: (i, 0))
    acc = pl.BlockSpec((8, D), lambda i: (0, 0))
    return pl.pallas_call(
        body, name=name,
        grid=(S // tm,),
        in_specs=[row, pl.BlockSpec((1, D), lambda i: (0, 0)), row],
        out_specs=[row, row, acc, acc],
        out_shape=[jax.ShapeDtypeStruct((S, D), F32), jax.ShapeDtypeStruct((S, D), BF),
                   jax.ShapeDtypeStruct((8, D), F32), jax.ShapeDtypeStruct((8, D), F32)],
        compiler_params=_cparams(("arbitrary",)),
    )(x, gain, target)


CONV_TM = 256
HALO = 16


def _halo_row(halo, r):
    hrow = lax.broadcasted_iota(jnp.int32, halo.shape, 0)
    return jnp.sum(jnp.where(hrow == r, halo, 0.0), axis=0, keepdims=True)


def _prev_rows(p, halo_p, n, row):
    out = pltpu.roll(p, n, axis=0)
    for k in range(n):
        out = jnp.where(row == k, _halo_row(halo_p, HALO - n + k), out)
    return out


def _next_rows(p, halo_p, n, row):
    tm = p.shape[0]
    out = pltpu.roll(p, tm - n, axis=0)
    for k in range(n):
        out = jnp.where(row == tm - n + k, _halo_row(halo_p, k), out)
    return out


def _conv_specs():
    per = CONV_TM // HALO
    main = pl.BlockSpec((3, CONV_TM, D), lambda i: (0, i, 0))
    prev = pl.BlockSpec((3, HALO, D), lambda i: (0, jnp.maximum(i * per - 1, 0), 0))
    nxt = pl.BlockSpec((3, HALO, D), lambda i: (0, jnp.minimum((i + 1) * per, S // HALO - 1), 0))
    return main, prev, nxt


def conv_fwd(z3, taps, name):
    def body(z_ref, zp_ref, k_ref, m_ref):
        i = pl.program_id(0)
        p = z_ref[1].astype(F32) * z_ref[2].astype(F32)
        halo = zp_ref[1].astype(F32) * zp_ref[2].astype(F32) * jnp.where(i > 0, 1.0, 0.0)
        row = lax.broadcasted_iota(jnp.int32, p.shape, 0)
        y = (k_ref[2:3, :] * p + k_ref[1:2, :] * _prev_rows(p, halo, 1, row)
             + k_ref[0:1, :] * _prev_rows(p, halo, 2, row))
        m_ref[...] = (z_ref[0].astype(F32) * y).astype(BF)

    main, prev, _ = _conv_specs()
    return pl.pallas_call(
        body, name=name,
        grid=(S // CONV_TM,),
        in_specs=[main, prev, pl.BlockSpec((8, D), lambda i: (0, 0))],
        out_specs=pl.BlockSpec((CONV_TM, D), lambda i: (i, 0)),
        out_shape=jax.ShapeDtypeStruct((S, D), BF),
        compiler_params=_cparams(("parallel",)),
    )(z3, z3, taps)


def conv_bwd(dm, z3, taps, name, deps=()):
    per = CONV_TM // HALO

    def body(dm_ref, dmn_ref, z_ref, zp_ref, zn_ref, k_ref, dz_ref, dk_ref):
        i = pl.program_id(0)

        @pl.when(i == 0)
        def _():
            dk_ref[...] = jnp.zeros_like(dk_ref)

        dmv = dm_ref[...]
        b = z_ref[0].astype(F32)
        c = z_ref[1].astype(F32)
        u = z_ref[2].astype(F32)
        p = c * u
        halo_p = zp_ref[1].astype(F32) * zp_ref[2].astype(F32) * jnp.where(i > 0, 1.0, 0.0)
        halo_dy = (dmn_ref[...] * zn_ref[0].astype(F32)
                   * jnp.where(i < S // CONV_TM - 1, 1.0, 0.0))
        row = lax.broadcasted_iota(jnp.int32, p.shape, 0)
        p1 = _prev_rows(p, halo_p, 1, row)
        p2 = _prev_rows(p, halo_p, 2, row)
        y = k_ref[2:3, :] * p + k_ref[1:2, :] * p1 + k_ref[0:1, :] * p2
        dy = dmv * b
        dz_ref[0] = (dmv * y).astype(BF)
        dp = (k_ref[2:3, :] * dy + k_ref[1:2, :] * _next_rows(dy, halo_dy, 1, row)
              + k_ref[0:1, :] * _next_rows(dy, halo_dy, 2, row))
        dz_ref[1] = (dp * u).astype(BF)
        dz_ref[2] = (dp * c).astype(BF)
        dk_ref[0:1, :] += jnp.sum(dy * p2, axis=0, keepdims=True)
        dk_ref[1:2, :] += jnp.sum(dy * p1, axis=0, keepdims=True)
        dk_ref[2:3, :] += jnp.sum(dy * p, axis=0, keepdims=True)

    main, prev, nxt = _conv_specs()
    return pl.pallas_call(
        _after(body, 6, deps), name=name,
        grid=(S // CONV_TM,),
        in_specs=[pl.BlockSpec((CONV_TM, D), lambda i: (i, 0)),
                  pl.BlockSpec((HALO, D), lambda i: (jnp.minimum((i + 1) * per, S // HALO - 1), 0)),
                  main, prev, nxt, pl.BlockSpec((8, D), lambda i: (0, 0))] + [ANY] * len(deps),
        out_specs=[main, pl.BlockSpec((8, D), lambda i: (0, 0))],
        out_shape=[jax.ShapeDtypeStruct((3, S, D), BF), jax.ShapeDtypeStruct((8, D), F32)],
        compiler_params=_cparams(("arbitrary",)),
    )(dm, dm, z3, z3, z3, taps, *deps)


def _t5_bucket(dist):
    exact = NUM_BUCKETS // 2
    df = jnp.maximum(dist, 1).astype(jnp.float32)
    large = exact + (jnp.log(df / exact) / math.log(MAX_DISTANCE / exact)
                     * (NUM_BUCKETS - exact)).astype(jnp.int32)
    large = jnp.minimum(large, NUM_BUCKETS - 1)
    return jnp.where(dist < exact, dist, large)


def _bucket_onehot_t():
    qi = jnp.arange(BLK)[:, None]
    ki = jnp.arange(2 * BLK)[None, :]
    rel = qi + BLK - ki
    band = ((rel >= 0) & (rel <= BLK)).reshape(1, -1).astype(F32)
    hots = []
    for d in BRANCH_DILATIONS:
        bucket = _t5_bucket(jnp.clip(rel, 0) * d).reshape(1, -1)
        hots.append((jnp.arange(NUM_BUCKETS)[:, None] == bucket).astype(F32))
    return jnp.stack(hots), band


def bias_tables(rel_bias_t, onehot_t, band, name):
    def body(rb_ref, oh_ref, band_ref, o_ref):
        b = jnp.dot(rb_ref[...], oh_ref[...], preferred_element_type=F32,
                    precision=lax.Precision.HIGHEST)
        o_ref[...] = jnp.where(band_ref[...] > 0.5, b, NEG_INF)

    n = BLK * 2 * BLK
    return pl.pallas_call(
        body, name=name,
        grid=(3,),
        in_specs=[pl.BlockSpec((H, NUM_BUCKETS), lambda g: (0, 0)),
                  pl.BlockSpec((None, NUM_BUCKETS, n), lambda g: (g, 0, 0)),
                  pl.BlockSpec((1, n), lambda g: (0, 0))],
        out_specs=pl.BlockSpec((None, H, n), lambda g: (g, 0, 0)),
        out_shape=jax.ShapeDtypeStruct((3, H, n), F32),
        compiler_params=_cparams(("parallel",)),
    )(rel_bias_t, onehot_t, band)


def _diagonal_onehot_t():
    c = jnp.arange(BLK)
    dist = jnp.concatenate([c + 1, (c + 1) % BLK])[None, :]
    hots = [(jnp.arange(NUM_BUCKETS)[:, None] == _t5_bucket(dist * d)).astype(F32)
            for d in BRANCH_DILATIONS]
    return jnp.stack(hots)


def bias_grad(dsums, onehot_t, name):
    def body(ds_ref, oh_ref, o_ref):
        @pl.when(pl.program_id(0) == 0)
        def _():
            o_ref[...] = jnp.zeros_like(o_ref)
        o_ref[...] += lax.dot_general(ds_ref[...], oh_ref[...], _NT, preferred_element_type=F32,
                                      precision=lax.Precision.HIGHEST)

    return pl.pallas_call(
        body, name=name,
        grid=(dsums.shape[0],),
        in_specs=[pl.BlockSpec((None, H, 2 * BLK), lambda g: (g, 0, 0)),
                  pl.BlockSpec((None, NUM_BUCKETS, 2 * BLK), lambda g: (g % 3, 0, 0))],
        out_specs=pl.BlockSpec((H, NUM_BUCKETS), lambda g: (0, 0)),
        out_shape=jax.ShapeDtypeStruct((H, NUM_BUCKETS), F32),
        compiler_params=_cparams(("arbitrary",)),
    )(dsums, onehot_t)


def _head_masks():
    lane = lax.broadcasted_iota(jnp.int32, (1, 2 * DH), 1)
    return (lane < DH, lane >= DH)


def _stack_heads(x, masks):
    zero = jnp.zeros_like(x)
    return jnp.concatenate([jnp.where(masks[0], x, zero), jnp.where(masks[1], x, zero)], axis=0)


def _deinterleave(src_ref, dst_ref, d, dtype):
    L = S // d
    for r in range(d):
        dst_ref[r * L:(r + 1) * L, :] = src_ref[pl.ds(r, L, stride=d), :].astype(dtype)


def _branch_loops(d, block):
    L = S // d
    for r in range(d):
        base = r * L
        block(base, base, BLK, True)
        for n in range(1, L // BLK):
            block(base + n * BLK, base + (n - 1) * BLK, 2 * BLK, False)


def attention_fwd(z3, bias3, name):
    W = 2 * DH
    CH = 256

    def body(q_ref, k_ref, v_ref, b_ref, o_ref, lse_ref, stage, qd, kd, vd, od, ld, on, ln):
        masks = _head_masks()
        for src, dst in ((q_ref, qd), (k_ref, kd), (v_ref, vd)):
            stage[...] = src[...].astype(F32)
            for gi, d in enumerate(BRANCH_DILATIONS[1:]):
                _deinterleave(stage, dst.at[gi], d, BF)

        for g, d in enumerate(BRANCH_DILATIONS):
            qs, ks, vs = (q_ref, k_ref, v_ref) if d == 1 else (qd.at[g - 1], kd.at[g - 1], vd.at[g - 1])
            o_dst, l_dst = (on.at[0], ln.at[0]) if d == 1 else (od, ld)

            def block(q0, k0, nk, first, g=g, qs=qs, ks=ks, vs=vs, o_dst=o_dst, l_dst=l_dst):
                q2 = _stack_heads(qs[pl.ds(q0, BLK), :], masks)
                kk = ks[pl.ds(k0, nk), :]
                vv = vs[pl.ds(k0, nk), :]
                bias = b_ref[g][:, BLK:] if first else b_ref[g]
                s = lax.dot_general(q2, kk, _NT, preferred_element_type=F32) * SCALE + bias
                mx = jnp.max(s, axis=1, keepdims=True)
                p = jnp.exp(s - mx)
                l = jnp.sum(p, axis=1, keepdims=True)
                o2 = jnp.dot(p.astype(BF), vv, preferred_element_type=F32) / l
                lse2 = mx + jnp.log(l)
                o_dst[pl.ds(q0, BLK), :] = jnp.where(masks[0], o2[:BLK], o2[BLK:])
                l_dst[pl.ds(q0, BLK), :] = jnp.where(masks[0], lse2[:BLK], lse2[BLK:])

            _branch_loops(d, block)
            if d > 1:
                L = S // d
                for r in range(d):
                    on[g, pl.ds(r, L, stride=d), :] = od[r * L:(r + 1) * L, :]
                    ln[g, pl.ds(r, L, stride=d), :] = ld[r * L:(r + 1) * L, :]

        def join(c, carry):
            rows = pl.ds(pl.multiple_of(c * CH, CH), CH)
            a, b, cc = ln[0, rows, :], ln[1, rows, :], ln[2, rows, :]
            mx = jnp.maximum(jnp.maximum(a, b), cc)
            ea, eb, ec = jnp.exp(a - mx), jnp.exp(b - mx), jnp.exp(cc - mx)
            tot = ea + eb + ec
            o_ref[rows, :] = ((ea * on[0, rows, :] + eb * on[1, rows, :] + ec * on[2, rows, :])
                              / tot).astype(BF)
            lse_ref[rows, :] = mx + jnp.log(tot)
            return carry
        lax.fori_loop(0, S // CH, join, 0)

    col = pl.BlockSpec((S, W), lambda hp: (0, hp))
    return pl.pallas_call(
        body, name=name,
        grid=(D // W,),
        in_specs=[pl.BlockSpec((None, S, W), lambda hp: (0, 0, hp)),
                  pl.BlockSpec((None, S, W), lambda hp: (1, 0, hp)),
                  pl.BlockSpec((None, S, W), lambda hp: (2, 0, hp)),
                  pl.BlockSpec((3, 2 * BLK, 2 * BLK), lambda hp: (0, hp, 0))],
        out_specs=[col, col],
        out_shape=[jax.ShapeDtypeStruct((S, D), BF), jax.ShapeDtypeStruct((S, D), F32)],
        scratch_shapes=[pltpu.VMEM((S, W), F32),
                        pltpu.VMEM((2, S, W), BF), pltpu.VMEM((2, S, W), BF), pltpu.VMEM((2, S, W), BF),
                        pltpu.VMEM((S, W), F32), pltpu.VMEM((S, W), F32),
                        pltpu.VMEM((3, S, W), F32), pltpu.VMEM((3, S, W), F32)],
        compiler_params=_cparams(("parallel",)),
    )(z3, z3, z3, bias3)


def attention_bwd(z3, dob, ob, lse_b, bias3, name, deps=()):
    W = 2 * DH
    CH = 256

    def body(q_ref, k_ref, v_ref, do_ref, o_ref, lse_ref, b_ref, dz_ref, dsum_ref,
             stage, delta, qd, kd, vd, dod, lsd, dld, res, acc, db_ref):
        masks = _head_masks()

        def rowsum(c, carry):
            rows = pl.ds(pl.multiple_of(c * CH, CH), CH)
            prod = do_ref[rows, :].astype(F32) * o_ref[rows, :].astype(F32)
            sa = jnp.sum(jnp.where(masks[0], prod, 0.0), axis=1, keepdims=True)
            sb = jnp.sum(jnp.where(masks[1], prod, 0.0), axis=1, keepdims=True)
            delta[rows, :] = jnp.where(masks[0], sa, sb)
            return carry
        lax.fori_loop(0, S // CH, rowsum, 0)

        for src, dst in ((q_ref, qd), (k_ref, kd), (v_ref, vd), (do_ref, dod)):
            stage[...] = src[...].astype(F32)
            for gi, d in enumerate(BRANCH_DILATIONS[1:]):
                _deinterleave(stage, dst.at[gi], d, BF)
        for gi, d in enumerate(BRANCH_DILATIONS[1:]):
            _deinterleave(lse_ref, lsd.at[gi], d, F32)
            _deinterleave(delta, dld.at[gi], d, F32)

        db_ref[...] = jnp.zeros_like(db_ref)
        for g, d in enumerate(BRANCH_DILATIONS):
            if d == 1:
                qs, ks, vs, dos, ls, dl = q_ref, k_ref, v_ref, do_ref, lse_ref, delta
            else:
                qs, ks, vs, dos = qd.at[g - 1], kd.at[g - 1], vd.at[g - 1], dod.at[g - 1]
                ls, dl = lsd.at[g - 1], dld.at[g - 1]
            res[1] = jnp.zeros((S, W), F32)
            res[2] = jnp.zeros((S, W), F32)

            def block(q0, k0, nk, first, g=g, qs=qs, ks=ks, vs=vs, dos=dos, ls=ls, dl=dl):
                kk = ks[pl.ds(k0, nk), :]
                vv = vs[pl.ds(k0, nk), :]
                q2 = _stack_heads(qs[pl.ds(q0, BLK), :], masks)
                do2 = _stack_heads(dos[pl.ds(q0, BLK), :], masks)
                lse_blk = ls[pl.ds(q0, BLK), :]
                del_blk = dl[pl.ds(q0, BLK), :]
                lse2 = jnp.concatenate([lse_blk[:, 0:1], lse_blk[:, DH:DH + 1]], axis=0)
                del2 = jnp.concatenate([del_blk[:, 0:1], del_blk[:, DH:DH + 1]], axis=0)
                bias = b_ref[g][:, BLK:] if first else b_ref[g]
                s = lax.dot_general(q2, kk, _NT, preferred_element_type=F32) * SCALE + bias
                p = jnp.exp(s - lse2)
                dp = lax.dot_general(do2, vv, _NT, preferred_element_type=F32)
                ds = p * (dp - del2)
                if first:
                    db_ref[g, :, BLK:] += ds
                else:
                    db_ref[g] += ds
                dsb = ds.astype(BF)
                dq2 = jnp.dot(dsb, kk, preferred_element_type=F32) * SCALE
                res[0, pl.ds(q0, BLK), :] = jnp.where(masks[0], dq2[:BLK], dq2[BLK:])
                res[1, pl.ds(k0, nk), :] += lax.dot_general(dsb, q2, _TN,
                                                            preferred_element_type=F32) * SCALE
                res[2, pl.ds(k0, nk), :] += lax.dot_general(p.astype(BF), do2, _TN,
                                                            preferred_element_type=F32)

            _branch_loops(d, block)
            L = S // d
            for t in range(3):
                if d == 1:
                    acc[t] = res[t]
                else:
                    for r in range(d):
                        acc[t, pl.ds(r, L, stride=d), :] = (acc[t, pl.ds(r, L, stride=d), :]
                                                            + res[t, r * L:(r + 1) * L, :])
        for t in range(3):
            dz_ref[t] = acc[t].astype(BF)

        flip = (lax.broadcasted_iota(jnp.int32, (BLK, BLK), 0)
                + lax.broadcasted_iota(jnp.int32, (BLK, BLK), 1) == BLK - 1).astype(BF)
        dsum_ref[...] = jnp.zeros_like(dsum_ref)
        for g in range(3):
            for hh in range(2):
                halves = []
                for half in range(2):
                    tile = db_ref[g, hh * BLK:(hh + 1) * BLK, half * BLK:(half + 1) * BLK]
                    hi = tile.astype(BF)
                    lo = (tile - hi.astype(F32)).astype(BF)
                    rev = (jnp.dot(hi, flip, preferred_element_type=F32)
                           + jnp.dot(lo, flip, preferred_element_type=F32))
                    skew = pltpu.roll(rev, 0, 1, stride=1, stride_axis=0)
                    halves.append(jnp.sum(skew, axis=0, keepdims=True))
                dsum_ref[2 * g + hh:2 * g + hh + 1, :] = jnp.concatenate(halves, axis=1)

    col = pl.BlockSpec((S, W), lambda hp: (0, hp))
    return pl.pallas_call(
        _after(body, 7, deps), name=name,
        grid=(D // W,),
        in_specs=[pl.BlockSpec((None, S, W), lambda hp: (0, 0, hp)),
                  pl.BlockSpec((None, S, W), lambda hp: (1, 0, hp)),
                  pl.BlockSpec((None, S, W), lambda hp: (2, 0, hp)),
                  col, col, col,
                  pl.BlockSpec((3, 2 * BLK, 2 * BLK), lambda hp: (0, hp, 0))] + [ANY] * len(deps),
        out_specs=[pl.BlockSpec((3, S, W), lambda hp: (0, 0, hp)),
                   pl.BlockSpec((None, 8, 2 * BLK), lambda hp: (hp, 0, 0))],
        out_shape=[jax.ShapeDtypeStruct((3, S, D), BF),
                   jax.ShapeDtypeStruct((D // W, 8, 2 * BLK), F32)],
        scratch_shapes=[pltpu.VMEM((S, W), F32), pltpu.VMEM((S, W), F32),
                        pltpu.VMEM((2, S, W), BF), pltpu.VMEM((2, S, W), BF),
                        pltpu.VMEM((2, S, W), BF), pltpu.VMEM((2, S, W), BF),
                        pltpu.VMEM((2, S, W), F32), pltpu.VMEM((2, S, W), F32),
                        pltpu.VMEM((3, S, W), F32), pltpu.VMEM((3, S, W), F32),
                        pltpu.VMEM((3, 2 * BLK, 2 * BLK), F32)],
        compiler_params=_cparams(("parallel",)),
    )(z3, z3, z3, dob, ob, lse_b, bias3, *deps)


def _me():
    return lax.axis_index("x"), lax.axis_index("y"), lax.axis_index("c")


def _other_chips(x, y):
    return [(1 - x, y), (x, 1 - y), (1 - x, 1 - y)]


def _shard_window(ref, axis, t, shape):
    R, C = shape
    if axis == 0:
        return ref.at[pl.ds(pl.multiple_of(t * R, 128), R), :]
    return ref.at[:, pl.ds(pl.multiple_of(t * C, 128), C)]


def all_gather_weights(shards, axes, name):
    n = len(shards)
    shapes = [s.shape for s in shards]
    outs_shape = [jax.ShapeDtypeStruct((8 * R, C) if ax == 0 else (R, 8 * C), BF)
                  for (R, C), ax in zip(shapes, axes)]

    def body(*refs):
        ins, outs = refs[:n], refs[n:2 * n]
        send_sems, recv_sems, local_sems = refs[2 * n:]
        x, y, c = _me()
        me, sibling = (x, y, c), (x, y, 1 - c)
        xnb, ynb, diag = (1 - x, y), (x, 1 - y), (1 - x, 1 - y)
        south = c == 0
        relay_from = (jnp.where(south, x, 1 - x), jnp.where(south, 1 - y, y))
        relay_to = (jnp.where(south, 1 - x, x), jnp.where(south, y, 1 - y))
        barrier = pltpu.get_barrier_semaphore()
        for peer in [sibling, (*xnb, c), (*ynb, c)]:
            pl.semaphore_signal(barrier, inc=1, device_id=peer, device_id_type=MESH)
        pl.semaphore_wait(barrier, 3)

        def win(i, px, py, pc):
            return _shard_window(outs[i], axes[i], 4 * px + 2 * py + pc, shapes[i])

        def copy(i, k, block, to, src=None):
            return pltpu.make_async_remote_copy(
                src_ref=win(i, *block) if src is None else src, dst_ref=win(i, *block),
                send_sem=send_sems.at[i * 7 + k], recv_sem=recv_sems.at[i * 7 + k],
                device_id=to, device_id_type=MESH)

        mine = [pltpu.make_async_copy(ins[i], win(i, *me), local_sems.at[i]) for i in range(n)]
        for cp in mine:
            cp.start()
        sent = []
        for i in range(n):
            sent += [copy(i, 0, me, sibling, src=ins[i]), copy(i, 1, me, (*xnb, c), src=ins[i]),
                     copy(i, 2, me, (*ynb, c), src=ins[i])]
        for cp in sent:
            cp.start()
        for i in range(n):
            for k, chip in ((1, xnb), (2, ynb)):
                copy(i, k, (*chip, c), me).wait_recv()
                sent.append(copy(i, 3 + k, (*chip, c), sibling))
                sent[-1].start()
            sent.append(copy(i, 3, (*relay_from, c), (*relay_to, c)))
            sent[-1].start()
        for i in range(n):
            copy(i, 3, (*diag, c), me).wait_recv()
            sent.append(copy(i, 6, (*diag, c), sibling))
            sent[-1].start()
        for i in range(n):
            copy(i, 0, sibling, me).wait_recv()
            for k, chip in ((4, xnb), (5, ynb), (6, diag)):
                copy(i, k, (*chip, 1 - c), me).wait_recv()
        for cp in sent:
            cp.wait_send()
        for cp in mine:
            cp.wait()

    return pl.kernel(
        body, out_type=outs_shape, name=name,
        mesh=plsc.ScalarSubcoreMesh(axis_name="sequencer", num_cores=1),
        scratch_types=[pltpu.SemaphoreType.DMA((7 * n,)), pltpu.SemaphoreType.DMA((7 * n,)),
                       pltpu.SemaphoreType.DMA((n,))],
        compiler_params=pltpu.CompilerParams(collective_id=1),
    )(*shards)


def pair_exchange_grads(grads, axes, shapes, name):
    n = len(grads)

    def body(*refs):
        ins, outs = refs[:n], refs[n:2 * n]
        send_sems, recv_sems = refs[2 * n:]
        x, y, c = _me()
        sibling = (x, y, 1 - c)
        barrier = pltpu.get_barrier_semaphore()
        pl.semaphore_signal(barrier, inc=1, device_id=sibling, device_id_type=MESH)
        pl.semaphore_wait(barrier, 1)
        copies = []
        for i in range(n):
            for q in range(4):
                t = 2 * q + (1 - c)
                copies.append(pltpu.make_async_remote_copy(
                    src_ref=_shard_window(ins[i], axes[i], t, shapes[i]), dst_ref=outs[i].at[q],
                    send_sem=send_sems.at[i * 4 + q], recv_sem=recv_sems.at[i * 4 + q],
                    device_id=sibling, device_id_type=MESH))
        for cp in copies:
            cp.start()
        for cp in copies:
            cp.wait_recv()
        for cp in copies:
            cp.wait_send()

    return pl.kernel(
        body, out_type=[jax.ShapeDtypeStruct((4,) + tuple(sh), BF) for sh in shapes], name=name,
        mesh=plsc.ScalarSubcoreMesh(axis_name="sequencer", num_cores=1),
        scratch_types=[pltpu.SemaphoreType.DMA((4 * n,)), pltpu.SemaphoreType.DMA((4 * n,))],
        compiler_params=pltpu.CompilerParams(collective_id=2),
    )(*grads)


def pair_add(grads, landed, axes, shapes, c_idx, name, deps=()):
    n = len(grads)

    def body(c_ref, *refs):
        for t in range(n):
            refs[2 * n + t][...] = (refs[2 * t][...].astype(F32)
                                    + refs[2 * t + 1][...].astype(F32)).astype(BF)

    halves = 1
    in_specs, out_specs, out_shapes, operands = [], [], [], []
    for t in range(n):
        R, C = shapes[t]
        rh = R // halves
        if axes[t] == 0:
            in_specs.append(pl.BlockSpec(
                (rh, C), lambda q, h, c_ref: (halves * (2 * q + c_ref[0]) + h, 0)))
        else:
            in_specs.append(pl.BlockSpec((rh, C), lambda q, h, c_ref: (h, 2 * q + c_ref[0])))
        blk = pl.BlockSpec((None, rh, C), lambda q, h, c_ref: (q, h, 0))
        in_specs.append(blk)
        out_specs.append(blk)
        out_shapes.append(jax.ShapeDtypeStruct((4, R, C), BF))
        operands += [grads[t], landed[t]]
    return pl.pallas_call(
        _after(body, 1 + 2 * n, deps), name=name,
        grid_spec=pltpu.PrefetchScalarGridSpec(
            num_scalar_prefetch=1, grid=(4, halves), in_specs=in_specs + [ANY] * len(deps),
            out_specs=out_specs),
        out_shape=out_shapes,
        compiler_params=_cparams(("parallel", "parallel")),
    )(c_idx, *operands, *deps)


def chip_exchange_grads(parts, name):
    n = len(parts)

    def body(*refs):
        ins, outs, relay = refs[:n], refs[n:2 * n], refs[2 * n:3 * n]
        send_sems, recv_sems = refs[3 * n:]
        x, y, c = _me()
        xnb, ynb, diag = (1 - x, y), (x, 1 - y), (1 - x, 1 - y)
        south = c == 0
        via = (jnp.where(south, 1 - x, x), jnp.where(south, y, 1 - y))
        onward = (jnp.where(south, x, 1 - x), jnp.where(south, 1 - y, y))
        barrier = pltpu.get_barrier_semaphore()
        for peer in (xnb, ynb):
            pl.semaphore_signal(barrier, inc=1, device_id=(*peer, c), device_id_type=MESH)
        pl.semaphore_wait(barrier, 2)

        def copy(i, k, src, dst, to):
            return pltpu.make_async_remote_copy(
                src_ref=src, dst_ref=dst, send_sem=send_sems.at[i * 4 + k],
                recv_sem=recv_sems.at[i * 4 + k], device_id=(*to, c), device_id_type=MESH)

        sent = []
        for i in range(n):
            sent += [copy(i, 0, ins[i].at[2 * xnb[0] + xnb[1]], outs[i].at[0], xnb),
                     copy(i, 1, ins[i].at[2 * ynb[0] + ynb[1]], outs[i].at[1], ynb),
                     copy(i, 2, ins[i].at[2 * diag[0] + diag[1]], relay[i], via)]
        for cp in sent:
            cp.start()
        for i in range(n):
            copy(i, 2, relay[i], relay[i], via).wait_recv()
            sent.append(copy(i, 3, relay[i], outs[i].at[2], onward))
            sent[-1].start()
        for i in range(n):
            copy(i, 0, outs[i].at[0], outs[i].at[0], xnb).wait_recv()
            copy(i, 1, outs[i].at[1], outs[i].at[1], ynb).wait_recv()
            copy(i, 3, outs[i].at[2], outs[i].at[2], onward).wait_recv()
        for cp in sent:
            cp.wait_send()

    landing = [jax.ShapeDtypeStruct((3,) + tuple(p.shape[1:]), BF) for p in parts]
    staging = [jax.ShapeDtypeStruct(tuple(p.shape[1:]), BF) for p in parts]
    return pl.kernel(
        body, out_type=landing + staging, name=name,
        mesh=plsc.ScalarSubcoreMesh(axis_name="sequencer", num_cores=1),
        scratch_types=[pltpu.SemaphoreType.DMA((4 * n,)), pltpu.SemaphoreType.DMA((4 * n,))],
        compiler_params=pltpu.CompilerParams(collective_id=3),
    )(*parts)[:n]


def all_gather_small(v, name):
    R, C = v.shape

    def body(v_ref, out_ref, send_sems, recv_sems, local_sem):
        x, y, c = _me()
        me, sibling = (x, y, c), (x, y, 1 - c)
        chips = _other_chips(x, y)

        def slot(px, py, pc):
            return out_ref.at[4 * px + 2 * py + pc]

        def copy(k, block, to, src=None):
            return pltpu.make_async_remote_copy(
                src_ref=slot(*block) if src is None else src, dst_ref=slot(*block),
                send_sem=send_sems.at[k], recv_sem=recv_sems.at[k],
                device_id=to, device_id_type=MESH)

        mine = pltpu.make_async_copy(v_ref, slot(*me), local_sem)
        mine.start()
        first = [copy(0, me, sibling, src=v_ref)]
        first += [copy(1 + j, me, (*chip, c), src=v_ref) for j, chip in enumerate(chips)]
        for cp in first:
            cp.start()
        passed = [copy(4 + j, (*chip, c), sibling) for j, chip in enumerate(chips)]
        for j, chip in enumerate(chips):
            copy(1 + j, (*chip, c), me).wait_recv()
            passed[j].start()
        copy(0, sibling, me).wait_recv()
        for j, chip in enumerate(chips):
            copy(4 + j, (*chip, 1 - c), me).wait_recv()
        for cp in first + passed:
            cp.wait_send()
        mine.wait()

    return pl.pallas_call(
        body, name=name,
        in_specs=[pl.BlockSpec(memory_space=pltpu.VMEM)],
        out_specs=pl.BlockSpec(memory_space=pltpu.VMEM),
        out_shape=jax.ShapeDtypeStruct((NDEV, R, C), F32),
        scratch_shapes=[pltpu.SemaphoreType.DMA((7,)), pltpu.SemaphoreType.DMA((7,)),
                        pltpu.SemaphoreType.DMA],
    )(v)


def _adamw(w, g, m, v):
    m = ADAM_B1 * m + (1.0 - ADAM_B1) * g
    v = ADAM_B2 * v + (1.0 - ADAM_B2) * (g * g)
    m_hat = m / (1.0 - ADAM_B1 ** ADAM_STEP)
    v_hat = v / (1.0 - ADAM_B2 ** ADAM_STEP)
    delta = -ADAM_LR * (m_hat / (jnp.sqrt(v_hat) + ADAM_EPS) + ADAM_WD * w)
    return delta, m, v


def reduce_adamw(parts, landed, params, q_idx, name, prevs, deps=()):
    n = len(parts)
    halves = 2
    in_specs, out_specs, out_shapes, operands, extra, aliases = [], [], [], [], [], {}
    for t in range(n):
        R, C = parts[t].shape[1:]
        w, m, v, layer = params[t]
        r, c = w.shape[1:]
        tr = r // halves
        assert tr % 16 == 0 and c == C
        wspec = pl.BlockSpec((None, tr, c), lambda i, q_ref, layer=layer: (layer, i, 0))
        in_specs += [pl.BlockSpec((None, tr, C), lambda i, q_ref: (q_ref[0], i, 0)),
                     pl.BlockSpec((3, tr, C), lambda i, q_ref: (0, i, 0)), wspec, wspec, wspec]
        out_specs += [wspec] * 4
        out_shapes += [jax.ShapeDtypeStruct(w.shape, F32)] * 4
        operands += [parts[t], landed[t], w, m, v]
        for k, buf in enumerate(prevs[t]):
            aliases[1 + 5 * n + len(extra)] = 4 * t + k
            extra.append(buf)
    extra += list(deps)

    def body(q_ref, *refs):
        for t in range(n):
            p_ref, l_ref, w_ref, m_ref, v_ref = refs[5 * t:5 * t + 5]
            g = p_ref[...].astype(F32)
            for k in range(3):
                g = g + l_ref[k].astype(F32)
            d, mm, vv = _adamw(w_ref[...], g, m_ref[...], v_ref[...])
            outs = refs[5 * n + 4 * t:5 * n + 4 * t + 4]
            outs[0][...] = g
            outs[1][...] = d
            outs[2][...] = mm
            outs[3][...] = vv

    res = pl.pallas_call(
        _after(body, 1 + 5 * n, extra), name=name,
        grid_spec=pltpu.PrefetchScalarGridSpec(
            num_scalar_prefetch=1, grid=(halves,),
            in_specs=in_specs + [ANY] * len(extra), out_specs=out_specs),
        out_shape=out_shapes,
        input_output_aliases=aliases,
        compiler_params=_cparams(("parallel",)),
    )(q_idx, *operands, *extra)
    return [res[4 * t:4 * t + 4] for t in range(n)]


def small_reduce_adamw(gathered, w, m, v, name):
    R, C = w.shape

    def body(a_ref, w_ref, m_ref, v_ref, g_out, d_out, m_out, v_out):
        g = a_ref[0]
        for k in range(1, NDEV):
            g = g + a_ref[k]
        d, mm, vv = _adamw(w_ref[...], g, m_ref[...], v_ref[...])
        g_out[...] = g
        d_out[...] = d
        m_out[...] = mm
        v_out[...] = vv

    out = jax.ShapeDtypeStruct((R, C), F32)
    return pl.pallas_call(body, name=name, out_shape=[out] * 4,
                          compiler_params=_cparams())(gathered, w, m, v)


def _pad_cols(a, n):
    return jnp.pad(a, ((0, 0), (0, n - a.shape[1])))


def _pad_rows(a, n):
    return jnp.pad(a, ((0, n - a.shape[0]), (0, 0)))


SMALL_ROWS = 16


def _pack_small(mix, ffn, fin, taps_full, relb):
    return jnp.concatenate([
        mix, ffn, fin.reshape(1, D), taps_full.reshape(6, D),
        jnp.pad(relb.reshape(1, NUM_BUCKETS * H), ((0, 0), (0, D - NUM_BUCKETS * H)))], axis=0)


def kernel(x, mix_norm, ffn_norm, final_norm, conv_w_in, conv_kernel, conv_w_out, attn_w_qkv, attn_w_out, rel_bias, ffn_w_gate, ffn_w_up, ffn_w_down, loss_target, m_mix_norm, m_ffn_norm, m_final_norm, m_conv_w_in, m_conv_kernel, m_conv_w_out, m_attn_w_qkv, m_attn_w_out, m_rel_bias, m_ffn_w_gate, m_ffn_w_up, m_ffn_w_down, v_mix_norm, v_ffn_norm, v_final_norm, v_conv_w_in, v_conv_kernel, v_conv_w_out, v_attn_w_qkv, v_attn_w_out, v_rel_bias, v_ffn_w_gate, v_ffn_w_up, v_ffn_w_down):
    xi, yi, ci = _me()
    me = 4 * xi + 2 * yi + ci
    c_idx = jnp.reshape(ci, (1,)).astype(jnp.int32)
    q_idx = jnp.reshape(2 * xi + yi, (1,)).astype(jnp.int32)
    col0 = me * (D // NDEV)

    taps_local = jnp.zeros((2, 3, D), F32)
    taps_local = lax.dynamic_update_slice(taps_local, conv_kernel, (0, 0, col0))
    taps_pack = jnp.pad(taps_local.reshape(6, D), ((0, 2), (0, 0)))
    taps_all = all_gather_small(taps_pack, "ag_taps")
    taps_sum = jnp.sum(taps_all, axis=0)
    taps = [jnp.pad(taps_sum[3 * j:3 * j + 3], ((0, 5), (0, 0))) for j in range(2)]

    gate_t, up_t = jnp.swapaxes(ffn_w_gate, 1, 2), jnp.swapaxes(ffn_w_up, 1, 2)
    m_gate_t, m_up_t = jnp.swapaxes(m_ffn_w_gate, 1, 2), jnp.swapaxes(m_ffn_w_up, 1, 2)
    v_gate_t, v_up_t = jnp.swapaxes(v_ffn_w_gate, 1, 2), jnp.swapaxes(v_ffn_w_up, 1, 2)

    mixer_in = (conv_w_in, attn_w_qkv)
    mixer_out = (conv_w_out, attn_w_out)
    wts = []
    for i in range(DEPTH):
        j = i // 2
        shards = [mixer_in[i % 2][j].astype(BF), mixer_out[i % 2][j].astype(BF),
                  _pad_rows(gate_t[i].astype(BF), FF_SHARD_PAD),
                  _pad_rows(up_t[i].astype(BF), FF_SHARD_PAD),
                  _pad_rows(ffn_w_down[i].astype(BF), FF_SHARD_PAD)]
        axes = (1, 0, 0, 0, 0)
        groups = ((0, 1), (1, 2), (2, 4), (4, 5)) if i == 0 else ((0, 2), (2, 5))
        layer = []
        for lo, hi in groups:
            layer += list(all_gather_weights(shards[lo:hi], axes[lo:hi], f"ag_l{i}_{lo}"))
        wts.append(layer)

    onehot_t, band = _bucket_onehot_t()
    bias3 = bias_tables(rel_bias.T, onehot_t, band, "bias_tables").reshape(3, H * BLK, 2 * BLK)

    saved = []
    xc = x[0]
    for i in range(DEPTH):
        w_in, w_out, w_g, w_u, w_d = wts[i]
        j = i // 2
        x_mix = xc
        z3, h_mix = norm_matmul3(xc, mix_norm[i:i + 1], w_in, f"mix_in_l{i}")
        if i % 2 == 0:
            act = conv_fwd(z3, taps[j], f"conv_fwd_l{i}")
            lse_b = None
        else:
            act, lse_b = attention_fwd(z3, bias3, f"attn_fwd_l{i}")
        xc = matmul_residual(act, w_out, xc, f"mix_out_l{i}")
        x_ffn = xc
        g, u, a, h_ffn = norm_swiglu_up(xc, ffn_norm[i:i + 1], w_g, w_u, f"ffn_up_l{i}")
        xc = matmul_residual(a, w_d, xc, f"ffn_down_l{i}")
        saved.append((x_mix, h_mix, z3, act, lse_b, x_ffn, h_ffn, g, u, a))

    dx, dxb, dg_final, sq = loss_head(xc, final_norm.reshape(1, D), loss_target[0], "loss_head")
    loss = lax.psum(0.5 * jnp.sum(sq[0]) / D, ("x", "y", "c"))

    dg_mix = [None] * DEPTH
    dg_ffn = [None] * DEPTH
    dtaps = [None, None]
    dbias_all = []
    shape_in, shape_out = (D, 3 * D // NDEV), (D // NDEV, D)
    ffn_axes, ffn_shapes = (0, 0, 0), ((FF_SHARD_PAD, D),) * 3
    stacked = {}

    def pair_stage(grads, landed1, axes, shapes, tag, tok):
        parts = pair_add(grads, landed1, axes, shapes, c_idx, f"rs_add_{tag}", deps=[tok])
        return parts, chip_exchange_grads(parts, f"rs_chip_{tag}"), parts[-1]

    def adamw_stage(parts, landed2, params, tag, tok):
        names = [p[0] for p in params]
        res = reduce_adamw(parts, landed2, [p[1:] for p in params], q_idx, f"adamw_{tag}",
                           [stacked.get(nm, ()) for nm in names], deps=[tok])
        for nm, r4 in zip(names, res):
            stacked[nm] = r4
        return res[-1][0]

    tok = dxb
    mix_wait = None
    mix_chip = None
    ffn_chip = []
    for i in reversed(range(DEPTH)):
        w_in, w_out, w_g, w_u, w_d = wts[i]
        j = i // 2
        x_mix, h_mix, z3, act, lse_b, x_ffn, h_ffn, g, u, a = saved[i]
        ffn_params = [("ffn_w_gate", gate_t, m_gate_t, v_gate_t, i),
                      ("ffn_w_up", up_t, m_up_t, v_up_t, i),
                      ("ffn_w_down", ffn_w_down, m_ffn_w_down, v_ffn_w_down, i)]
        if i % 2 == 0:
            mix_params = [("conv_w_in", conv_w_in, m_conv_w_in, v_conv_w_in, j),
                          ("conv_w_out", conv_w_out, m_conv_w_out, v_conv_w_out, j)]
        else:
            mix_params = [("attn_w_qkv", attn_w_qkv, m_attn_w_qkv, v_attn_w_qkv, j),
                          ("attn_w_out", attn_w_out, m_attn_w_out, v_attn_w_out, j)]
        dgate, dup = swiglu_bwd_da(dxb, w_d, g, u, f"ffn_da_l{i}", deps=[tok])
        tok = dgate
        for group in ffn_chip:
            tok = adamw_stage(*group, tok)
        ffn_chip = []
        if mix_wait is not None:
            grads_m, landed1_m, params_m, tag_m = mix_wait
            parts_m, landed2_m, tok = pair_stage(grads_m, landed1_m, (1, 0), (shape_in, shape_out),
                                                 tag_m, tok)
            mix_chip = (parts_m, landed2_m, params_m, tag_m)
            mix_wait = None
        products = [(dgate, h_ffn), (dup, h_ffn), (a, dxb)]
        ffn_wait = []
        for gi, idxs in enumerate(((0,), (1, 2)) if i == 0 else ((0, 1, 2),)):
            grads_f = matmul_tn_group([products[k] for k in idxs], f"ffn_dw_l{i}_{gi}", deps=[tok])
            landed1_f = pair_exchange_grads(grads_f, ffn_axes[:len(idxs)], ffn_shapes[:len(idxs)],
                                            f"rs_pair_f{i}_{gi}")
            ffn_wait.append((grads_f, landed1_f, [ffn_params[k] for k in idxs], f"f{i}_{gi}"))
            tok = grads_f[-1]

        def ffn_pair_stages(groups, tok):
            for grads_f, landed1_f, params_f, tag_f in groups:
                nf = len(grads_f)
                parts_f, landed2_f, tok = pair_stage(grads_f, landed1_f, ffn_axes[:nf],
                                                     ffn_shapes[:nf], tag_f, tok)
                ffn_chip.append((parts_f, landed2_f, params_f, tag_f))
            return tok

        if len(ffn_wait) > 1:
            if mix_chip is not None:
                tok = adamw_stage(*mix_chip, tok)
                mix_chip = None
            tok = ffn_pair_stages(ffn_wait[:-1], tok)
        dx, dxb, dg_ffn[i] = matmul_normbwd(
            [(dgate, w_g, False), (dup, w_u, False)], x_ffn, ffn_norm[i:i + 1], dx, f"ffn_dh_l{i}",
            deps=[tok])
        dxb_mix = dxb
        tok = dxb
        if mix_chip is not None:
            tok = adamw_stage(*mix_chip, tok)
            mix_chip = None
        tok = ffn_pair_stages(ffn_wait[-1:], tok)
        dact = matmul_nt(dxb, w_out, f"mix_dact_l{i}", out_dtype=F32 if i % 2 == 0 else BF,
                         deps=[tok])
        if i % 2 == 0:
            dz3, dtaps[j] = conv_bwd(dact, z3, taps[j], f"conv_bwd_l{i}")
        else:
            dz3, dsum = attention_bwd(z3, dact, act, lse_b, bias3, f"attn_bwd_l{i}")
            dbias_all.append(dsum[:, :6].reshape(H // 2, 3, 2, 2 * BLK).transpose(1, 0, 2, 3)
                             .reshape(3, H, 2 * BLK))
        grads_m = matmul_tn_group([(h_mix, dz3), (act, dxb_mix)], f"mix_dw_l{i}")
        landed1_m = pair_exchange_grads(grads_m, (1, 0), (shape_in, shape_out), f"rs_pair_m{i}")
        mix_wait = (grads_m, landed1_m, mix_params, f"m{i}")
        dx, dxb, dg_mix[i] = matmul_normbwd(
            [(dz3, w_in, True)], x_mix, mix_norm[i:i + 1], dx, f"mix_dh_l{i}", deps=[grads_m[-1]])
        tok = dxb
    for group in ffn_chip:
        tok = adamw_stage(*group, tok)
    grads_m, landed1_m, params_m, tag_m = mix_wait
    parts_m, landed2_m, tok = pair_stage(grads_m, landed1_m, (1, 0), (shape_in, shape_out), tag_m, tok)

    grad_relb_t = bias_grad(jnp.concatenate(dbias_all), _diagonal_onehot_t(), "bias_grad")
    dtaps_full = jnp.stack([dtaps[0][:3], dtaps[1][:3]])
    g_small = _pack_small(jnp.concatenate([d[0:1] for d in dg_mix], axis=0),
                          jnp.concatenate([d[0:1] for d in dg_ffn], axis=0),
                          dg_final[0], dtaps_full, grad_relb_t.T)
    gathered = all_gather_small(g_small, "ag_small_grads")

    def taps_at_cols(k):
        return lax.dynamic_update_slice(jnp.zeros((2, 3, D), F32), k, (0, 0, col0))

    w_small = _pack_small(mix_norm, ffn_norm, final_norm, taps_at_cols(conv_kernel), rel_bias)
    m_small = _pack_small(m_mix_norm, m_ffn_norm, m_final_norm, taps_at_cols(m_conv_kernel), m_rel_bias)
    v_small = _pack_small(v_mix_norm, v_ffn_norm, v_final_norm, taps_at_cols(v_conv_kernel), v_rel_bias)
    small = small_reduce_adamw(gathered, w_small, m_small, v_small, "adamw_small")

    def unpack_small(p):
        taps_p = lax.dynamic_slice(p[9:15].reshape(2, 3, D), (0, 0, col0), (2, 3, D // NDEV))
        return {"mix_norm": p[0:4], "ffn_norm": p[4:8], "final_norm": p[8],
                "conv_kernel": taps_p, "rel_bias": p[15, :NUM_BUCKETS * H].reshape(NUM_BUCKETS, H)}

    small_out = [unpack_small(p) for p in small]
    adamw_stage(parts_m, landed2_m, params_m, tag_m, small[0])

    names = ["mix_norm", "ffn_norm", "final_norm", "conv_w_in", "conv_kernel", "conv_w_out",
             "attn_w_qkv", "attn_w_out", "rel_bias", "ffn_w_gate", "ffn_w_up", "ffn_w_down"]
    outs = [loss, dx.reshape(1, S, D)]
    for o in range(4):
        for nme in names:
            if nme in ("ffn_w_gate", "ffn_w_up"):
                outs.append(jnp.swapaxes(stacked[nme][o], 1, 2))
            else:
                outs.append(stacked[nme][o] if nme in stacked else small_out[o][nme])
    return tuple(outs)
```

```python
import math

import numpy as np
import jax
import jax.numpy as jnp
from jax import lax
from jax.experimental import pallas as pl
from jax.experimental.pallas import tpu as pltpu
from jax.experimental.pallas import tpu_sc as plsc

S = 2048
D = 1024
H = 16
DH = 64
DFF = 2816
NDEV = 8
DEPTH = 4
FF_SHARD = DFF // NDEV
FF_SHARD_PAD = 384
DFF_PAD = FF_SHARD_PAD * NDEV
BLK = 128
BRANCH_DILATIONS = (1, 4, 16)
NUM_BUCKETS = 32
MAX_DISTANCE = 2048
EPS = 1e-6
NEG_INF = -1e30
SCALE = DH ** -0.5

ADAM_LR = 0.001
ADAM_B1 = 0.9
ADAM_B2 = 0.999
ADAM_EPS = 1e-08
ADAM_WD = 0.01
ADAM_STEP = 10

BF = jnp.bfloat16
F32 = jnp.float32
VMEM_LIMIT_BYTES = 56 * 1024 * 1024
KSPLIT = 512
NORM_CHUNK = 256
MESH = pl.DeviceIdType.MESH
ANY = pl.BlockSpec(memory_space=pl.ANY)

_NT = (((1,), (1,)), ((), ()))
_TN = (((0,), (0,)), ((), ()))


def _cparams(sem=None):
    return pltpu.CompilerParams(dimension_semantics=sem, vmem_limit_bytes=VMEM_LIMIT_BYTES)


def _after(body, n, deps):
    nd = len(deps)
    if nd == 0:
        return body

    def ordered(*refs):
        body(*refs[:n], *refs[n + nd:])
    return ordered


def _rms(x):
    return lax.rsqrt(jnp.mean(x * x, axis=-1, keepdims=True) + EPS)


def norm_matmul3(x, gain, w, name, tm=1024, tn=1024):
    per = D // tn

    def body(x_ref, g_ref, w_ref, z_ref, h_ref, hs_ref):
        @pl.when(pl.program_id(1) == 0)
        def _():
            for c in range(tm // NORM_CHUNK):
                rows = slice(c * NORM_CHUNK, (c + 1) * NORM_CHUNK)
                xv = x_ref[rows, :]
                hv = (xv * _rms(xv) * g_ref[...]).astype(BF)
                hs_ref[rows, :] = hv
                h_ref[rows, :] = hv
                z_ref[rows, :] = jnp.dot(hv, w_ref[...], preferred_element_type=F32).astype(BF)

        @pl.when(pl.program_id(1) > 0)
        def _():
            z_ref[...] = jnp.dot(hs_ref[...], w_ref[...], preferred_element_type=F32).astype(BF)

    return pl.pallas_call(
        body, name=name,
        grid=(S // tm, 3 * D // tn),
        in_specs=[pl.BlockSpec((tm, D), lambda i, j: (i, 0)),
                  pl.BlockSpec((1, D), lambda i, j: (0, 0)),
                  pl.BlockSpec((D, tn), lambda i, j: (0, j))],
        out_specs=[pl.BlockSpec((None, tm, tn), lambda i, j: (j // per, i, j % per)),
                   pl.BlockSpec((tm, D), lambda i, j: (i, 0))],
        out_shape=[jax.ShapeDtypeStruct((3, S, D), BF), jax.ShapeDtypeStruct((S, D), BF)],
        scratch_shapes=[pltpu.VMEM((tm, D), BF)],
        compiler_params=_cparams(("parallel", "arbitrary")),
    )(x, gain, w)


def norm_swiglu_up(x, gain, wg_t, wu_t, name, tm=1024, tn=768):
    def body(x_ref, g_ref, wg_ref, wu_ref, go_ref, uo_ref, ao_ref, h_ref, hs_ref):
        def gate_up(hv, rows):
            g = lax.dot_general(hv, wg_ref[...], _NT, preferred_element_type=F32)
            u = lax.dot_general(hv, wu_ref[...], _NT, preferred_element_type=F32)
            go_ref[rows, :] = g.astype(BF)
            uo_ref[rows, :] = u.astype(BF)
            ao_ref[rows, :] = (g * jax.nn.sigmoid(g) * u).astype(BF)

        @pl.when(pl.program_id(1) == 0)
        def _():
            for c in range(tm // NORM_CHUNK):
                rows = slice(c * NORM_CHUNK, (c + 1) * NORM_CHUNK)
                xv = x_ref[rows, :]
                hv = (xv * _rms(xv) * g_ref[...]).astype(BF)
                hs_ref[rows, :] = hv
                h_ref[rows, :] = hv
                gate_up(hv, rows)

        @pl.when(pl.program_id(1) > 0)
        def _():
            gate_up(hs_ref[...], slice(None))

    act = jax.ShapeDtypeStruct((S, DFF_PAD), BF)
    blk = pl.BlockSpec((tm, tn), lambda i, j: (i, j))
    return pl.pallas_call(
        body, name=name,
        grid=(S // tm, DFF_PAD // tn),
        in_specs=[pl.BlockSpec((tm, D), lambda i, j: (i, 0)),
                  pl.BlockSpec((1, D), lambda i, j: (0, 0)),
                  pl.BlockSpec((tn, D), lambda i, j: (j, 0)),
                  pl.BlockSpec((tn, D), lambda i, j: (j, 0))],
        out_specs=[blk, blk, blk, pl.BlockSpec((tm, D), lambda i, j: (i, 0))],
        out_shape=[act, act, act, jax.ShapeDtypeStruct((S, D), BF)],
        scratch_shapes=[pltpu.VMEM((tm, D), BF)],
        compiler_params=_cparams(("parallel", "arbitrary")),
    )(x, gain, wg_t, wu_t)


def matmul_residual(a, w, x, name, tm=1024):
    K = a.shape[1]
    tn = D if K <= D else D // 2
    ns = K // KSPLIT
    kc = K // ns

    def body(*refs):
        x_ref, o_ref = refs[2 * ns:]
        acc = x_ref[...]
        for s in range(ns):
            acc = acc + jnp.dot(refs[s][...], refs[ns + s][...], preferred_element_type=F32)
        o_ref[...] = acc

    return pl.pallas_call(
        body, name=name,
        grid=(S // tm, D // tn),
        in_specs=[pl.BlockSpec((tm, kc), lambda i, j, s=s: (i, s)) for s in range(ns)]
        + [pl.BlockSpec((kc, tn), lambda i, j, s=s: (s, j)) for s in range(ns)]
        + [pl.BlockSpec((tm, tn), lambda i, j: (i, j))],
        out_specs=pl.BlockSpec((tm, tn), lambda i, j: (i, j)),
        out_shape=jax.ShapeDtypeStruct((S, D), F32),
        compiler_params=_cparams(("parallel", "parallel")),
    )(*([a] * ns), *([w] * ns), x)


def matmul_nt(a, w, name, out_dtype=BF, tm=1024, tn=1024, deps=()):
    K = a.shape[1]
    N = w.shape[0]

    def body(a_ref, w_ref, o_ref):
        o_ref[...] = lax.dot_general(a_ref[...], w_ref[...], _NT,
                                     preferred_element_type=F32).astype(o_ref.dtype)

    return pl.pallas_call(
        _after(body, 2, deps), name=name,
        grid=(S // tm, N // tn),
        in_specs=[pl.BlockSpec((tm, K), lambda i, j: (i, 0)),
                  pl.BlockSpec((tn, K), lambda i, j: (j, 0))] + [ANY] * len(deps),
        out_specs=pl.BlockSpec((tm, tn), lambda i, j: (i, j)),
        out_shape=jax.ShapeDtypeStruct((S, N), out_dtype),
        compiler_params=_cparams(("parallel", "parallel")),
    )(a, w, *deps)


def swiglu_bwd_da(dxb, wd, g, u, name, tm=1024, tn=768, deps=()):
    def body(dx_ref, w_ref, g_ref, u_ref, dg_ref, du_ref):
        da = lax.dot_general(dx_ref[...], w_ref[...], _NT, preferred_element_type=F32)
        gv = g_ref[...].astype(F32)
        uv = u_ref[...].astype(F32)
        sig = jax.nn.sigmoid(gv)
        dg_ref[...] = (da * uv * (sig * (1.0 + gv * (1.0 - sig)))).astype(BF)
        du_ref[...] = (da * (gv * sig)).astype(BF)

    act = jax.ShapeDtypeStruct((S, DFF_PAD), BF)
    blk = pl.BlockSpec((tm, tn), lambda i, j: (i, j))
    return pl.pallas_call(
        _after(body, 4, deps), name=name,
        grid=(S // tm, DFF_PAD // tn),
        in_specs=[pl.BlockSpec((tm, D), lambda i, j: (i, 0)),
                  pl.BlockSpec((tn, D), lambda i, j: (j, 0)),
                  blk, blk] + [ANY] * len(deps),
        out_specs=[blk, blk],
        out_shape=[act, act],
        compiler_params=_cparams(("parallel", "parallel")),
    )(dxb, wd, g, u, *deps)


def matmul_tn_group(pairs, name, tm=1024, deps=()):
    P = len(pairs)
    tn = D if P <= 2 else D // 2
    steps = []
    for p, (a, b) in enumerate(pairs):
        N = 3 * D if b.ndim == 3 else b.shape[1]
        steps += [(p, i, j) for i in range(a.shape[1] // tm) for j in range(N // tn)]
    T = len(steps)
    tab = np.zeros((T, 1 + 2 * P), np.int32)
    for p in range(P):
        cur = (0, 0)
        for s, (ph, i, j) in enumerate(steps):
            if ph == p:
                cur = (i, j)
            tab[s, 1 + 2 * p:3 + 2 * p] = cur
    tab[:, 0] = [ph for ph, _, _ in steps]

    in_specs, out_specs, out_shapes, operands = [], [], [], []
    per = D // tn
    for p, (a, b) in enumerate(pairs):
        ci, cj = 1 + 2 * p, 2 + 2 * p
        in_specs.append(pl.BlockSpec((S, tm), lambda s, t, ci=ci: (0, t[s, ci])))
        if b.ndim == 3:
            in_specs.append(pl.BlockSpec((None, S, tn),
                                         lambda s, t, cj=cj: (t[s, cj] // per, 0, t[s, cj] % per)))
            N = 3 * D
        else:
            in_specs.append(pl.BlockSpec((S, tn), lambda s, t, cj=cj: (0, t[s, cj])))
            N = b.shape[1]
        out_specs.append(pl.BlockSpec((tm, tn), lambda s, t, ci=ci, cj=cj: (t[s, ci], t[s, cj])))
        out_shapes.append(jax.ShapeDtypeStruct((a.shape[1], N), BF))
        operands += [a, b]

    def body(tab_ref, *refs):
        phase = tab_ref[pl.program_id(0), 0]
        for p in range(P):
            @pl.when(phase == p)
            def _(p=p):
                refs[2 * P + p][...] = lax.dot_general(
                    refs[2 * p][...], refs[2 * p + 1][...], _TN,
                    preferred_element_type=F32).astype(BF)

    return pl.pallas_call(
        _after(body, 1 + 2 * P, deps), name=name,
        grid_spec=pltpu.PrefetchScalarGridSpec(
            num_scalar_prefetch=1, grid=(T,), in_specs=in_specs + [ANY] * len(deps),
            out_specs=out_specs),
        out_shape=out_shapes,
        compiler_params=_cparams(("arbitrary",)),
    )(jnp.asarray(tab), *operands, *deps)


def matmul_normbwd(terms, x_in, gain, dx, name, tm=512, ch=256, deps=()):
    specs, operands = [], []
    for (a, w, stacked) in terms:
        if stacked:
            specs.append(pl.BlockSpec((3, tm, D), lambda i: (0, i, 0)))
        else:
            specs.append(pl.BlockSpec((tm, a.shape[1]), lambda i: (i, 0)))
        specs.append(pl.BlockSpec(w.shape, lambda i: (0, 0), pipeline_mode=pl.Buffered(1)))
        operands += [a, w]
    nt = len(terms)

    def body(*refs):
        aw = refs[:2 * nt]
        x_ref, g_ref, dx_ref, dxo_ref, dxb_ref, dg_ref = refs[2 * nt:]

        @pl.when(pl.program_id(0) == 0)
        def _():
            dg_ref[...] = jnp.zeros_like(dg_ref)

        dgain = None
        for c in range(tm // ch):
            rows = slice(c * ch, (c + 1) * ch)
            dh = None
            for t, (_, _, stacked) in enumerate(terms):
                a_ref, w_ref = aw[2 * t], aw[2 * t + 1]
                if stacked:
                    parts = [lax.dot_general(a_ref[k, rows, :], w_ref[:, k * D:(k + 1) * D], _NT,
                                             preferred_element_type=F32) for k in range(3)]
                else:
                    parts = [jnp.dot(a_ref[rows, :], w_ref[...], preferred_element_type=F32)]
                for p in parts:
                    dh = p if dh is None else dh + p
            xv = x_ref[rows, :]
            r = _rms(xv)
            xhat = xv * r
            part = jnp.sum(dh * xhat, axis=0, keepdims=True)
            dgain = part if dgain is None else dgain + part
            dxh = dh * g_ref[...]
            dxn = r * (dxh - xhat * jnp.mean(dxh * xhat, axis=-1, keepdims=True))
            out = dx_ref[rows, :] + dxn
            dxo_ref[rows, :] = out
            dxb_ref[rows, :] = out.astype(BF)
        dg_ref[0:1, :] += dgain

    row = pl.BlockSpec((tm, D), lambda i: (i, 0))
    return pl.pallas_call(
        _after(body, 2 * nt + 3, deps), name=name,
        grid=(S // tm,),
        in_specs=specs + [row, pl.BlockSpec((1, D), lambda i: (0, 0)), row] + [ANY] * len(deps),
        out_specs=[row, row, pl.BlockSpec((8, D), lambda i: (0, 0))],
        out_shape=[jax.ShapeDtypeStruct((S, D), F32), jax.ShapeDtypeStruct((S, D), BF),
                   jax.ShapeDtypeStruct((8, D), F32)],
        compiler_params=_cparams(("arbitrary",)),
    )(*operands, x_in, gain, dx, *deps)


def loss_head(x, gain, target, name, tm=512):
    def body(x_ref, g_ref, t_ref, dxo_ref, dxb_ref, dg_ref, sq_ref):
        @pl.when(pl.program_id(0) == 0)
        def _():
            dg_ref[...] = jnp.zeros_like(dg_ref)
            sq_ref[...] = jnp.zeros_like(sq_ref)
        xv = x_ref[...]
        r = _rms(xv)
        xhat = xv * r
        err = xhat * g_ref[...] - t_ref[...]
        sq_ref[0:1, :] += jnp.sum(err * err, axis=0, keepdims=True)
        dy = err * (1.0 / D)
        dg_ref[0:1, :] += jnp.sum(dy * xhat, axis=0, keepdims=True)
        dxh = dy * g_ref[...]
        out = r * (dxh - xhat * jnp.mean(dxh * xhat, axis=-1, keepdims=True))
        dxo_ref[...] = out
        dxb_ref[...] = out.astype(BF)

    row = pl.BlockSpec((tm, D), lambda i: (i, 0))
    acc = pl.BlockSpec((8, D), lambda i: (0, 0))
    return pl.pallas_call(
        body, name=name,
        grid=(S // tm,),
        in_specs=[row, pl.BlockSpec((1, D), lambda i: (0, 0)), row],
        out_specs=[row, row, acc, acc],
        out_shape=[jax.ShapeDtypeStruct((S, D), F32), jax.ShapeDtypeStruct((S, D), BF),
                   jax.ShapeDtypeStruct((8, D), F32), jax.ShapeDtypeStruct((8, D), F32)],
        compiler_params=_cparams(("arbitrary",)),
    )(x, gain, target)


CONV_TM = 256
HALO = 16


def _halo_row(halo, r):
    hrow = lax.broadcasted_iota(jnp.int32, halo.shape, 0)
    return jnp.sum(jnp.where(hrow == r, halo, 0.0), axis=0, keepdims=True)


def _prev_rows(p, halo_p, n, row):
    out = pltpu.roll(p, n, axis=0)
    for k in range(n):
        out = jnp.where(row == k, _halo_row(halo_p, HALO - n + k), out)
    return out


def _next_rows(p, halo_p, n, row):
    tm = p.shape[0]
    out = pltpu.roll(p, tm - n, axis=0)
    for k in range(n):
        out = jnp.where(row == tm - n + k, _halo_row(halo_p, k), out)
    return out


def _conv_specs():
    per = CONV_TM // HALO
    main = pl.BlockSpec((3, CONV_TM, D), lambda i: (0, i, 0))
    prev = pl.BlockSpec((3, HALO, D), lambda i: (0, jnp.maximum(i * per - 1, 0), 0))
    nxt = pl.BlockSpec((3, HALO, D), lambda i: (0, jnp.minimum((i + 1) * per, S // HALO - 1), 0))
    return main, prev, nxt


def conv_fwd(z3, taps, name):
    def body(z_ref, zp_ref, k_ref, m_ref):
        i = pl.program_id(0)
        p = z_ref[1].astype(F32) * z_ref[2].astype(F32)
        halo = zp_ref[1].astype(F32) * zp_ref[2].astype(F32) * jnp.where(i > 0, 1.0, 0.0)
        row = lax.broadcasted_iota(jnp.int32, p.shape, 0)
        y = (k_ref[2:3, :] * p + k_ref[1:2, :] * _prev_rows(p, halo, 1, row)
             + k_ref[0:1, :] * _prev_rows(p, halo, 2, row))
        m_ref[...] = (z_ref[0].astype(F32) * y).astype(BF)

    main, prev, _ = _conv_specs()
    return pl.pallas_call(
        body, name=name,
        grid=(S // CONV_TM,),
        in_specs=[main, prev, pl.BlockSpec((8, D), lambda i: (0, 0))],
        out_specs=pl.BlockSpec((CONV_TM, D), lambda i: (i, 0)),
        out_shape=jax.ShapeDtypeStruct((S, D), BF),
        compiler_params=_cparams(("parallel",)),
    )(z3, z3, taps)


def conv_bwd(dm, z3, taps, name, deps=()):
    per = CONV_TM // HALO

    def body(dm_ref, dmn_ref, z_ref, zp_ref, zn_ref, k_ref, dz_ref, dk_ref):
        i = pl.program_id(0)

        @pl.when(i == 0)
        def _():
            dk_ref[...] = jnp.zeros_like(dk_ref)

        dmv = dm_ref[...]
        b = z_ref[0].astype(F32)
        c = z_ref[1].astype(F32)
        u = z_ref[2].astype(F32)
        p = c * u
        halo_p = zp_ref[1].astype(F32) * zp_ref[2].astype(F32) * jnp.where(i > 0, 1.0, 0.0)
        halo_dy = (dmn_ref[...] * zn_ref[0].astype(F32)
                   * jnp.where(i < S // CONV_TM - 1, 1.0, 0.0))
        row = lax.broadcasted_iota(jnp.int32, p.shape, 0)
        p1 = _prev_rows(p, halo_p, 1, row)
        p2 = _prev_rows(p, halo_p, 2, row)
        y = k_ref[2:3, :] * p + k_ref[1:2, :] * p1 + k_ref[0:1, :] * p2
        dy = dmv * b
        dz_ref[0] = (dmv * y).astype(BF)
        dp = (k_ref[2:3, :] * dy + k_ref[1:2, :] * _next_rows(dy, halo_dy, 1, row)
              + k_ref[0:1, :] * _next_rows(dy, halo_dy, 2, row))
        dz_ref[1] = (dp * u).astype(BF)
        dz_ref[2] = (dp * c).astype(BF)
        dk_ref[0:1, :] += jnp.sum(dy * p2, axis=0, keepdims=True)
        dk_ref[1:2, :] += jnp.sum(dy * p1, axis=0, keepdims=True)
        dk_ref[2:3, :] += jnp.sum(dy * p, axis=0, keepdims=True)

    main, prev, nxt = _conv_specs()
    return pl.pallas_call(
        _after(body, 6, deps), name=name,
        grid=(S // CONV_TM,),
        in_specs=[pl.BlockSpec((CONV_TM, D), lambda i: (i, 0)),
                  pl.BlockSpec((HALO, D), lambda i: (jnp.minimum((i + 1) * per, S // HALO - 1), 0)),
                  main, prev, nxt, pl.BlockSpec((8, D), lambda i: (0, 0))] + [ANY] * len(deps),
        out_specs=[main, pl.BlockSpec((8, D), lambda i: (0, 0))],
        out_shape=[jax.ShapeDtypeStruct((3, S, D), BF), jax.ShapeDtypeStruct((8, D), F32)],
        compiler_params=_cparams(("arbitrary",)),
    )(dm, dm, z3, z3, z3, taps, *deps)


def _t5_bucket(dist):
    exact = NUM_BUCKETS // 2
    df = jnp.maximum(dist, 1).astype(jnp.float32)
    large = exact + (jnp.log(df / exact) / math.log(MAX_DISTANCE / exact)
                     * (NUM_BUCKETS - exact)).astype(jnp.int32)
    large = jnp.minimum(large, NUM_BUCKETS - 1)
    return jnp.where(dist < exact, dist, large)


def _bucket_onehot_t():
    qi = jnp.arange(BLK)[:, None]
    ki = jnp.arange(2 * BLK)[None, :]
    rel = qi + BLK - ki
    band = ((rel >= 0) & (rel <= BLK)).reshape(1, -1).astype(F32)
    hots = []
    for d in BRANCH_DILATIONS:
        bucket = _t5_bucket(jnp.clip(rel, 0) * d).reshape(1, -1)
        hots.append((jnp.arange(NUM_BUCKETS)[:, None] == bucket).astype(F32))
    return jnp.stack(hots), band


def bias_tables(rel_bias_t, onehot_t, band, name):
    def body(rb_ref, oh_ref, band_ref, o_ref):
        b = jnp.dot(rb_ref[...], oh_ref[...], preferred_element_type=F32,
                    precision=lax.Precision.HIGHEST)
        o_ref[...] = jnp.where(band_ref[...] > 0.5, b, NEG_INF)

    n = BLK * 2 * BLK
    return pl.pallas_call(
        body, name=name,
        grid=(3,),
        in_specs=[pl.BlockSpec((H, NUM_BUCKETS), lambda g: (0, 0)),
                  pl.BlockSpec((None, NUM_BUCKETS, n), lambda g: (g, 0, 0)),
                  pl.BlockSpec((1, n), lambda g: (0, 0))],
        out_specs=pl.BlockSpec((None, H, n), lambda g: (g, 0, 0)),
        out_shape=jax.ShapeDtypeStruct((3, H, n), F32),
        compiler_params=_cparams(("parallel",)),
    )(rel_bias_t, onehot_t, band)


def _diagonal_onehot_t():
    c = jnp.arange(BLK)
    dist = jnp.concatenate([c + 1, (c + 1) % BLK])[None, :]
    hots = [(jnp.arange(NUM_BUCKETS)[:, None] == _t5_bucket(dist * d)).astype(F32)
            for d in BRANCH_DILATIONS]
    return jnp.stack(hots)


def bias_grad(dsums, onehot_t, name):
    def body(ds_ref, oh_ref, o_ref):
        @pl.when(pl.program_id(0) == 0)
        def _():
            o_ref[...] = jnp.zeros_like(o_ref)
        o_ref[...] += lax.dot_general(ds_ref[...], oh_ref[...], _NT, preferred_element_type=F32,
                                      precision=lax.Precision.HIGHEST)

    return pl.pallas_call(
        body, name=name,
        grid=(dsums.shape[0],),
        in_specs=[pl.BlockSpec((None, H, 2 * BLK), lambda g: (g, 0, 0)),
                  pl.BlockSpec((None, NUM_BUCKETS, 2 * BLK), lambda g: (g % 3, 0, 0))],
        out_specs=pl.BlockSpec((H, NUM_BUCKETS), lambda g: (0, 0)),
        out_shape=jax.ShapeDtypeStruct((H, NUM_BUCKETS), F32),
        compiler_params=_cparams(("arbitrary",)),
    )(dsums, onehot_t)


def _head_masks():
    lane = lax.broadcasted_iota(jnp.int32, (1, 2 * DH), 1)
    return (lane < DH, lane >= DH)


def _stack_heads(x, masks):
    zero = jnp.zeros_like(x)
    return jnp.concatenate([jnp.where(masks[0], x, zero), jnp.where(masks[1], x, zero)], axis=0)


def _deinterleave(src_ref, dst_ref, d, dtype):
    L = S // d
    for r in range(d):
        dst_ref[r * L:(r + 1) * L, :] = src_ref[pl.ds(r, L, stride=d), :].astype(dtype)


def _branch_loops(d, block):
    L = S // d
    for r in range(d):
        base = r * L
        block(base, base, BLK, True)
        for n in range(1, L // BLK):
            block(base + n * BLK, base + (n - 1) * BLK, 2 * BLK, False)


def attention_fwd(z3, bias3, name):
    W = 2 * DH
    CH = 256

    def body(q_ref, k_ref, v_ref, b_ref, o_ref, lse_ref, stage, qd, kd, vd, od, ld, on, ln):
        masks = _head_masks()
        for src, dst in ((q_ref, qd), (k_ref, kd), (v_ref, vd)):
            stage[...] = src[...].astype(F32)
            for gi, d in enumerate(BRANCH_DILATIONS[1:]):
                _deinterleave(stage, dst.at[gi], d, BF)

        for g, d in enumerate(BRANCH_DILATIONS):
            qs, ks, vs = (q_ref, k_ref, v_ref) if d == 1 else (qd.at[g - 1], kd.at[g - 1], vd.at[g - 1])
            o_dst, l_dst = (on.at[0], ln.at[0]) if d == 1 else (od, ld)

            def block(q0, k0, nk, first, g=g, qs=qs, ks=ks, vs=vs, o_dst=o_dst, l_dst=l_dst):
                q2 = _stack_heads(qs[pl.ds(q0, BLK), :], masks)
                kk = ks[pl.ds(k0, nk), :]
                vv = vs[pl.ds(k0, nk), :]
                bias = b_ref[g][:, BLK:] if first else b_ref[g]
                s = lax.dot_general(q2, kk, _NT, preferred_element_type=F32) * SCALE + bias
                mx = jnp.max(s, axis=1, keepdims=True)
                p = jnp.exp(s - mx)
                l = jnp.sum(p, axis=1, keepdims=True)
                o2 = jnp.dot(p.astype(BF), vv, preferred_element_type=F32) / l
                lse2 = mx + jnp.log(l)
                o_dst[pl.ds(q0, BLK), :] = jnp.where(masks[0], o2[:BLK], o2[BLK:])
                l_dst[pl.ds(q0, BLK), :] = jnp.where(masks[0], lse2[:BLK], lse2[BLK:])

            _branch_loops(d, block)
            if d > 1:
                L = S // d
                for r in range(d):
                    on[g, pl.ds(r, L, stride=d), :] = od[r * L:(r + 1) * L, :]
                    ln[g, pl.ds(r, L, stride=d), :] = ld[r * L:(r + 1) * L, :]

        def join(c, carry):
            rows = pl.ds(pl.multiple_of(c * CH, CH), CH)
            a, b, cc = ln[0, rows, :], ln[1, rows, :], ln[2, rows, :]
            mx = jnp.maximum(jnp.maximum(a, b), cc)
            ea, eb, ec = jnp.exp(a - mx), jnp.exp(b - mx), jnp.exp(cc - mx)
            tot = ea + eb + ec
            o_ref[rows, :] = ((ea * on[0, rows, :] + eb * on[1, rows, :] + ec * on[2, rows, :])
                              / tot).astype(BF)
            lse_ref[rows, :] = mx + jnp.log(tot)
            return carry
        lax.fori_loop(0, S // CH, join, 0)

    col = pl.BlockSpec((S, W), lambda hp: (0, hp))
    return pl.pallas_call(
        body, name=name,
        grid=(D // W,),
        in_specs=[pl.BlockSpec((None, S, W), lambda hp: (0, 0, hp)),
                  pl.BlockSpec((None, S, W), lambda hp: (1, 0, hp)),
                  pl.BlockSpec((None, S, W), lambda hp: (2, 0, hp)),
                  pl.BlockSpec((3, 2 * BLK, 2 * BLK), lambda hp: (0, hp, 0))],
        out_specs=[col, col],
        out_shape=[jax.ShapeDtypeStruct((S, D), BF), jax.ShapeDtypeStruct((S, D), F32)],
        scratch_shapes=[pltpu.VMEM((S, W), F32),
                        pltpu.VMEM((2, S, W), BF), pltpu.VMEM((2, S, W), BF), pltpu.VMEM((2, S, W), BF),
                        pltpu.VMEM((S, W), F32), pltpu.VMEM((S, W), F32),
                        pltpu.VMEM((3, S, W), F32), pltpu.VMEM((3, S, W), F32)],
        compiler_params=_cparams(("parallel",)),
    )(z3, z3, z3, bias3)


def attention_bwd(z3, dob, ob, lse_b, bias3, name, deps=()):
    W = 2 * DH
    CH = 256

    def body(q_ref, k_ref, v_ref, do_ref, o_ref, lse_ref, b_ref, dz_ref, dsum_ref,
             stage, delta, qd, kd, vd, dod, lsd, dld, res, acc, db_ref):
        masks = _head_masks()

        def rowsum(c, carry):
            rows = pl.ds(pl.multiple_of(c * CH, CH), CH)
            prod = do_ref[rows, :].astype(F32) * o_ref[rows, :].astype(F32)
            sa = jnp.sum(jnp.where(masks[0], prod, 0.0), axis=1, keepdims=True)
            sb = jnp.sum(jnp.where(masks[1], prod, 0.0), axis=1, keepdims=True)
            delta[rows, :] = jnp.where(masks[0], sa, sb)
            return carry
        lax.fori_loop(0, S // CH, rowsum, 0)

        for src, dst in ((q_ref, qd), (k_ref, kd), (v_ref, vd), (do_ref, dod)):
            stage[...] = src[...].astype(F32)
            for gi, d in enumerate(BRANCH_DILATIONS[1:]):
                _deinterleave(stage, dst.at[gi], d, BF)
        for gi, d in enumerate(BRANCH_DILATIONS[1:]):
            _deinterleave(lse_ref, lsd.at[gi], d, F32)
            _deinterleave(delta, dld.at[gi], d, F32)

        db_ref[...] = jnp.zeros_like(db_ref)
        for g, d in enumerate(BRANCH_DILATIONS):
            if d == 1:
                qs, ks, vs, dos, ls, dl = q_ref, k_ref, v_ref, do_ref, lse_ref, delta
            else:
                qs, ks, vs, dos = qd.at[g - 1], kd.at[g - 1], vd.at[g - 1], dod.at[g - 1]
                ls, dl = lsd.at[g - 1], dld.at[g - 1]
            res[1] = jnp.zeros((S, W), F32)
            res[2] = jnp.zeros((S, W), F32)

            def block(q0, k0, nk, first, g=g, qs=qs, ks=ks, vs=vs, dos=dos, ls=ls, dl=dl):
                kk = ks[pl.ds(k0, nk), :]
                vv = vs[pl.ds(k0, nk), :]
                q2 = _stack_heads(qs[pl.ds(q0, BLK), :], masks)
                do2 = _stack_heads(dos[pl.ds(q0, BLK), :], masks)
                lse_blk = ls[pl.ds(q0, BLK), :]
                del_blk = dl[pl.ds(q0, BLK), :]
                lse2 = jnp.concatenate([lse_blk[:, 0:1], lse_blk[:, DH:DH + 1]], axis=0)
                del2 = jnp.concatenate([del_blk[:, 0:1], del_blk[:, DH:DH + 1]], axis=0)
                bias = b_ref[g][:, BLK:] if first else b_ref[g]
                s = lax.dot_general(q2, kk, _NT, preferred_element_type=F32) * SCALE + bias
                p = jnp.exp(s - lse2)
                dp = lax.dot_general(do2, vv, _NT, preferred_element_type=F32)
                ds = p * (dp - del2)
                if first:
                    db_ref[g, :, BLK:] += ds
                else:
                    db_ref[g] += ds
                dsb = ds.astype(BF)
                dq2 = jnp.dot(dsb, kk, preferred_element_type=F32) * SCALE
                res[0, pl.ds(q0, BLK), :] = jnp.where(masks[0], dq2[:BLK], dq2[BLK:])
                res[1, pl.ds(k0, nk), :] += lax.dot_general(dsb, q2, _TN,
                                                            preferred_element_type=F32) * SCALE
                res[2, pl.ds(k0, nk), :] += lax.dot_general(p.astype(BF), do2, _TN,
                                                            preferred_element_type=F32)

            _branch_loops(d, block)
            L = S // d
            for t in range(3):
                if d == 1:
                    acc[t] = res[t]
                else:
                    for r in range(d):
                        acc[t, pl.ds(r, L, stride=d), :] = (acc[t, pl.ds(r, L, stride=d), :]
                                                            + res[t, r * L:(r + 1) * L, :])
        for t in range(3):
            dz_ref[t] = acc[t].astype(BF)

        flip = (lax.broadcasted_iota(jnp.int32, (BLK, BLK), 0)
                + lax.broadcasted_iota(jnp.int32, (BLK, BLK), 1) == BLK - 1).astype(BF)
        dsum_ref[...] = jnp.zeros_like(dsum_ref)
        for g in range(3):
            for hh in range(2):
                halves = []
                for half in range(2):
                    tile = db_ref[g, hh * BLK:(hh + 1) * BLK, half * BLK:(half + 1) * BLK]
                    hi = tile.astype(BF)
                    lo = (tile - hi.astype(F32)).astype(BF)
                    rev = (jnp.dot(hi, flip, preferred_element_type=F32)
                           + jnp.dot(lo, flip, preferred_element_type=F32))
                    skew = pltpu.roll(rev, 0, 1, stride=1, stride_axis=0)
                    halves.append(jnp.sum(skew, axis=0, keepdims=True))
                dsum_ref[2 * g + hh:2 * g + hh + 1, :] = jnp.concatenate(halves, axis=1)

    col = pl.BlockSpec((S, W), lambda hp: (0, hp))
    return pl.pallas_call(
        _after(body, 7, deps), name=name,
        grid=(D // W,),
        in_specs=[pl.BlockSpec((None, S, W), lambda hp: (0, 0, hp)),
                  pl.BlockSpec((None, S, W), lambda hp: (1, 0, hp)),
                  pl.BlockSpec((None, S, W), lambda hp: (2, 0, hp)),
                  col, col, col,
                  pl.BlockSpec((3, 2 * BLK, 2 * BLK), lambda hp: (0, hp, 0))] + [ANY] * len(deps),
        out_specs=[pl.BlockSpec((3, S, W), lambda hp: (0, 0, hp)),
                   pl.BlockSpec((None, 8, 2 * BLK), lambda hp: (hp, 0, 0))],
        out_shape=[jax.ShapeDtypeStruct((3, S, D), BF),
                   jax.ShapeDtypeStruct((D // W, 8, 2 * BLK), F32)],
        scratch_shapes=[pltpu.VMEM((S, W), F32), pltpu.VMEM((S, W), F32),
                        pltpu.VMEM((2, S, W), BF), pltpu.VMEM((2, S, W), BF),
                        pltpu.VMEM((2, S, W), BF), pltpu.VMEM((2, S, W), BF),
                        pltpu.VMEM((2, S, W), F32), pltpu.VMEM((2, S, W), F32),
                        pltpu.VMEM((3, S, W), F32), pltpu.VMEM((3, S, W), F32),
                        pltpu.VMEM((3, 2 * BLK, 2 * BLK), F32)],
        compiler_params=_cparams(("parallel",)),
    )(z3, z3, z3, dob, ob, lse_b, bias3, *deps)


def _me():
    return lax.axis_index("x"), lax.axis_index("y"), lax.axis_index("c")


def _other_chips(x, y):
    return [(1 - x, y), (x, 1 - y), (1 - x, 1 - y)]


def _shard_window(ref, axis, t, shape):
    R, C = shape
    if axis == 0:
        return ref.at[pl.ds(pl.multiple_of(t * R, 128), R), :]
    return ref.at[:, pl.ds(pl.multiple_of(t * C, 128), C)]


def all_gather_weights(shards, axes, name):
    n = len(shards)
    shapes = [s.shape for s in shards]
    outs_shape = [jax.ShapeDtypeStruct((8 * s.shape[0], s.shape[1]) if ax == 0
                                       else (s.shape[0], 8 * s.shape[1]), s.dtype)
                  for s, ax in zip(shards, axes)]

    def body(*refs):
        ins, outs = refs[:n], refs[n:2 * n]
        send_sems, recv_sems, local_sems = refs[2 * n:]
        x, y, c = _me()
        me, sibling = (x, y, c), (x, y, 1 - c)
        xnb, ynb, diag = (1 - x, y), (x, 1 - y), (1 - x, 1 - y)
        south = c == 0
        relay_from = (jnp.where(south, x, 1 - x), jnp.where(south, 1 - y, y))
        relay_to = (jnp.where(south, 1 - x, x), jnp.where(south, y, 1 - y))
        barrier = pltpu.get_barrier_semaphore()
        for peer in [sibling, (*xnb, c), (*ynb, c)]:
            pl.semaphore_signal(barrier, inc=1, device_id=peer, device_id_type=MESH)
        pl.semaphore_wait(barrier, 3)

        def win(i, px, py, pc):
            return _shard_window(outs[i], axes[i], 4 * px + 2 * py + pc, shapes[i])

        def copy(i, k, block, to, src=None):
            return pltpu.make_async_remote_copy(
                src_ref=win(i, *block) if src is None else src, dst_ref=win(i, *block),
                send_sem=send_sems.at[i * 7 + k], recv_sem=recv_sems.at[i * 7 + k],
                device_id=to, device_id_type=MESH)

        mine = [pltpu.make_async_copy(ins[i], win(i, *me), local_sems.at[i]) for i in range(n)]
        for cp in mine:
            cp.start()
        sent = []
        for i in range(n):
            sent += [copy(i, 0, me, sibling, src=ins[i]), copy(i, 1, me, (*xnb, c), src=ins[i]),
                     copy(i, 2, me, (*ynb, c), src=ins[i])]
        for cp in sent:
            cp.start()
        for i in range(n):
            for k, chip in ((1, xnb), (2, ynb)):
                copy(i, k, (*chip, c), me).wait_recv()
                sent.append(copy(i, 3 + k, (*chip, c), sibling))
                sent[-1].start()
            sent.append(copy(i, 3, (*relay_from, c), (*relay_to, c)))
            sent[-1].start()
        for i in range(n):
            copy(i, 3, (*diag, c), me).wait_recv()
            sent.append(copy(i, 6, (*diag, c), sibling))
            sent[-1].start()
        for i in range(n):
            copy(i, 0, sibling, me).wait_recv()
            for k, chip in ((4, xnb), (5, ynb), (6, diag)):
                copy(i, k, (*chip, 1 - c), me).wait_recv()
        for cp in sent:
            cp.wait_send()
        for cp in mine:
            cp.wait()

    return pl.kernel(
        body, out_type=outs_shape, name=name,
        mesh=plsc.ScalarSubcoreMesh(axis_name="sequencer", num_cores=1),
        scratch_types=[pltpu.SemaphoreType.DMA((7 * n,)), pltpu.SemaphoreType.DMA((7 * n,)),
                       pltpu.SemaphoreType.DMA((n,))],
        compiler_params=pltpu.CompilerParams(collective_id=1),
    )(*shards)


def pair_exchange_grads(grads, axes, shapes, name):
    n = len(grads)

    def body(*refs):
        ins, outs = refs[:n], refs[n:2 * n]
        send_sems, recv_sems = refs[2 * n:]
        x, y, c = _me()
        sibling = (x, y, 1 - c)
        barrier = pltpu.get_barrier_semaphore()
        pl.semaphore_signal(barrier, inc=1, device_id=sibling, device_id_type=MESH)
        pl.semaphore_wait(barrier, 1)
        copies = []
        for i in range(n):
            for q in range(4):
                t = 2 * q + (1 - c)
                copies.append(pltpu.make_async_remote_copy(
                    src_ref=_shard_window(ins[i], axes[i], t, shapes[i]), dst_ref=outs[i].at[q],
                    send_sem=send_sems.at[i * 4 + q], recv_sem=recv_sems.at[i * 4 + q],
                    device_id=sibling, device_id_type=MESH))
        for cp in copies:
            cp.start()
        for cp in copies:
            cp.wait_recv()
        for cp in copies:
            cp.wait_send()

    return pl.kernel(
        body, out_type=[jax.ShapeDtypeStruct((4,) + tuple(sh), BF) for sh in shapes], name=name,
        mesh=plsc.ScalarSubcoreMesh(axis_name="sequencer", num_cores=1),
        scratch_types=[pltpu.SemaphoreType.DMA((4 * n,)), pltpu.SemaphoreType.DMA((4 * n,))],
        compiler_params=pltpu.CompilerParams(collective_id=2),
    )(*grads)


def pair_add(grads, landed, axes, shapes, c_idx, name, deps=()):
    n = len(grads)

    def body(c_ref, *refs):
        for t in range(n):
            refs[2 * n + t][...] = (refs[2 * t][...].astype(F32)
                                    + refs[2 * t + 1][...].astype(F32)).astype(BF)

    halves = 1
    in_specs, out_specs, out_shapes, operands = [], [], [], []
    for t in range(n):
        R, C = shapes[t]
        rh = R // halves
        if axes[t] == 0:
            in_specs.append(pl.BlockSpec(
                (rh, C), lambda q, h, c_ref: (halves * (2 * q + c_ref[0]) + h, 0)))
        else:
            in_specs.append(pl.BlockSpec((rh, C), lambda q, h, c_ref: (h, 2 * q + c_ref[0])))
        blk = pl.BlockSpec((None, rh, C), lambda q, h, c_ref: (q, h, 0))
        in_specs.append(blk)
        out_specs.append(blk)
        out_shapes.append(jax.ShapeDtypeStruct((4, R, C), BF))
        operands += [grads[t], landed[t]]
    return pl.pallas_call(
        _after(body, 1 + 2 * n, deps), name=name,
        grid_spec=pltpu.PrefetchScalarGridSpec(
            num_scalar_prefetch=1, grid=(4, halves), in_specs=in_specs + [ANY] * len(deps),
            out_specs=out_specs),
        out_shape=out_shapes,
        compiler_params=_cparams(("parallel", "parallel")),
    )(c_idx, *operands, *deps)


def chip_exchange_grads(parts, name):
    n = len(parts)

    def body(*refs):
        ins, outs, relay = refs[:n], refs[n:2 * n], refs[2 * n:3 * n]
        send_sems, recv_sems = refs[3 * n:]
        x, y, c = _me()
        xnb, ynb, diag = (1 - x, y), (x, 1 - y), (1 - x, 1 - y)
        south = c == 0
        via = (jnp.where(south, 1 - x, x), jnp.where(south, y, 1 - y))
        onward = (jnp.where(south, x, 1 - x), jnp.where(south, 1 - y, y))
        barrier = pltpu.get_barrier_semaphore()
        for peer in (xnb, ynb):
            pl.semaphore_signal(barrier, inc=1, device_id=(*peer, c), device_id_type=MESH)
        pl.semaphore_wait(barrier, 2)

        def copy(i, k, src, dst, to):
            return pltpu.make_async_remote_copy(
                src_ref=src, dst_ref=dst, send_sem=send_sems.at[i * 4 + k],
                recv_sem=recv_sems.at[i * 4 + k], device_id=(*to, c), device_id_type=MESH)

        sent = []
        for i in range(n):
            sent += [copy(i, 0, ins[i].at[2 * xnb[0] + xnb[1]], outs[i].at[0], xnb),
                     copy(i, 1, ins[i].at[2 * ynb[0] + ynb[1]], outs[i].at[1], ynb),
                     copy(i, 2, ins[i].at[2 * diag[0] + diag[1]], relay[i], via)]
        for cp in sent:
            cp.start()
        for i in range(n):
            copy(i, 2, relay[i], relay[i], via).wait_recv()
            sent.append(copy(i, 3, relay[i], outs[i].at[2], onward))
            sent[-1].start()
        for i in range(n):
            copy(i, 0, outs[i].at[0], outs[i].at[0], xnb).wait_recv()
            copy(i, 1, outs[i].at[1], outs[i].at[1], ynb).wait_recv()
            copy(i, 3, outs[i].at[2], outs[i].at[2], onward).wait_recv()
        for cp in sent:
            cp.wait_send()

    landing = [jax.ShapeDtypeStruct((3,) + tuple(p.shape[1:]), BF) for p in parts]
    staging = [jax.ShapeDtypeStruct(tuple(p.shape[1:]), BF) for p in parts]
    return pl.kernel(
        body, out_type=landing + staging, name=name,
        mesh=plsc.ScalarSubcoreMesh(axis_name="sequencer", num_cores=1),
        scratch_types=[pltpu.SemaphoreType.DMA((4 * n,)), pltpu.SemaphoreType.DMA((4 * n,))],
        compiler_params=pltpu.CompilerParams(collective_id=3),
    )(*parts)[:n]


def all_gather_small(v, name):
    R, C = v.shape

    def body(v_ref, out_ref, send_sems, recv_sems, local_sem):
        x, y, c = _me()
        me, sibling = (x, y, c), (x, y, 1 - c)
        chips = _other_chips(x, y)

        def slot(px, py, pc):
            return out_ref.at[4 * px + 2 * py + pc]

        def copy(k, block, to, src=None):
            return pltpu.make_async_remote_copy(
                src_ref=slot(*block) if src is None else src, dst_ref=slot(*block),
                send_sem=send_sems.at[k], recv_sem=recv_sems.at[k],
                device_id=to, device_id_type=MESH)

        mine = pltpu.make_async_copy(v_ref, slot(*me), local_sem)
        mine.start()
        first = [copy(0, me, sibling, src=v_ref)]
        first += [copy(1 + j, me, (*chip, c), src=v_ref) for j, chip in enumerate(chips)]
        for cp in first:
            cp.start()
        passed = [copy(4 + j, (*chip, c), sibling) for j, chip in enumerate(chips)]
        for j, chip in enumerate(chips):
            copy(1 + j, (*chip, c), me).wait_recv()
            passed[j].start()
        copy(0, sibling, me).wait_recv()
        for j, chip in enumerate(chips):
            copy(4 + j, (*chip, 1 - c), me).wait_recv()
        for cp in first + passed:
            cp.wait_send()
        mine.wait()

    return pl.pallas_call(
        body, name=name,
        in_specs=[pl.BlockSpec(memory_space=pltpu.VMEM)],
        out_specs=pl.BlockSpec(memory_space=pltpu.VMEM),
        out_shape=jax.ShapeDtypeStruct((NDEV, R, C), F32),
        scratch_shapes=[pltpu.SemaphoreType.DMA((7,)), pltpu.SemaphoreType.DMA((7,)),
                        pltpu.SemaphoreType.DMA],
    )(v)


def _adamw(w, g, m, v):
    m = ADAM_B1 * m + (1.0 - ADAM_B1) * g
    v = ADAM_B2 * v + (1.0 - ADAM_B2) * (g * g)
    m_hat = m / (1.0 - ADAM_B1 ** ADAM_STEP)
    v_hat = v / (1.0 - ADAM_B2 ** ADAM_STEP)
    delta = -ADAM_LR * (m_hat / (jnp.sqrt(v_hat) + ADAM_EPS) + ADAM_WD * w)
    return delta, m, v


def reduce_adamw(parts, landed, params, q_idx, name, prevs, deps=()):
    n = len(parts)
    halves = 2
    in_specs, out_specs, out_shapes, operands, extra, aliases = [], [], [], [], [], {}
    for t in range(n):
        R, C = parts[t].shape[1:]
        w, m, v, layer = params[t]
        r, c = w.shape[1:]
        tr = r // halves
        assert tr % 16 == 0 and c == C
        wspec = pl.BlockSpec((None, tr, c), lambda i, q_ref, layer=layer: (layer, i, 0))
        in_specs += [pl.BlockSpec((None, tr, C), lambda i, q_ref: (q_ref[0], i, 0)),
                     pl.BlockSpec((3, tr, C), lambda i, q_ref: (0, i, 0)), wspec, wspec, wspec]
        out_specs += [wspec] * 4
        out_shapes += [jax.ShapeDtypeStruct(w.shape, F32)] * 4
        operands += [parts[t], landed[t], w, m, v]
        for k, buf in enumerate(prevs[t]):
            aliases[1 + 5 * n + len(extra)] = 4 * t + k
            extra.append(buf)
    extra += list(deps)

    def body(q_ref, *refs):
        for t in range(n):
            p_ref, l_ref, w_ref, m_ref, v_ref = refs[5 * t:5 * t + 5]
            g = p_ref[...].astype(F32)
            for k in range(3):
                g = g + l_ref[k].astype(F32)
            d, mm, vv = _adamw(w_ref[...], g, m_ref[...], v_ref[...])
            outs = refs[5 * n + 4 * t:5 * n + 4 * t + 4]
            outs[0][...] = g
            outs[1][...] = d
            outs[2][...] = mm
            outs[3][...] = vv

    res = pl.pallas_call(
        _after(body, 1 + 5 * n, extra), name=name,
        grid_spec=pltpu.PrefetchScalarGridSpec(
            num_scalar_prefetch=1, grid=(halves,),
            in_specs=in_specs + [ANY] * len(extra), out_specs=out_specs),
        out_shape=out_shapes,
        input_output_aliases=aliases,
        compiler_params=_cparams(("parallel",)),
    )(q_idx, *operands, *extra)
    return [res[4 * t:4 * t + 4] for t in range(n)]


def small_reduce_adamw(gathered, w, m, v, name):
    R, C = w.shape

    def body(a_ref, w_ref, m_ref, v_ref, g_out, d_out, m_out, v_out):
        g = a_ref[0]
        for k in range(1, NDEV):
            g = g + a_ref[k]
        d, mm, vv = _adamw(w_ref[...], g, m_ref[...], v_ref[...])
        g_out[...] = g
        d_out[...] = d
        m_out[...] = mm
        v_out[...] = vv

    out = jax.ShapeDtypeStruct((R, C), F32)
    return pl.pallas_call(body, name=name, out_shape=[out] * 4,
                          compiler_params=_cparams())(gathered, w, m, v)


def _pad_rows(a, n):
    return jnp.pad(a, ((0, n - a.shape[0]), (0, 0)))


def _pack_small(mix, ffn, fin, taps_full, relb, scalar=0.0):
    n = NUM_BUCKETS * H
    last = jnp.concatenate([relb.reshape(1, n), jnp.reshape(scalar, (1, 1)).astype(F32),
                            jnp.zeros((1, D - n - 1), F32)], axis=1)
    return jnp.concatenate([mix, ffn, fin.reshape(1, D), taps_full.reshape(6, D), last], axis=0)


def kernel(x, mix_norm, ffn_norm, final_norm, conv_w_in, conv_kernel, conv_w_out, attn_w_qkv, attn_w_out, rel_bias, ffn_w_gate, ffn_w_up, ffn_w_down, loss_target, m_mix_norm, m_ffn_norm, m_final_norm, m_conv_w_in, m_conv_kernel, m_conv_w_out, m_attn_w_qkv, m_attn_w_out, m_rel_bias, m_ffn_w_gate, m_ffn_w_up, m_ffn_w_down, v_mix_norm, v_ffn_norm, v_final_norm, v_conv_w_in, v_conv_kernel, v_conv_w_out, v_attn_w_qkv, v_attn_w_out, v_rel_bias, v_ffn_w_gate, v_ffn_w_up, v_ffn_w_down):
    xi, yi, ci = _me()
    me = 4 * xi + 2 * yi + ci
    c_idx = jnp.reshape(ci, (1,)).astype(jnp.int32)
    q_idx = jnp.reshape(2 * xi + yi, (1,)).astype(jnp.int32)
    col0 = me * (D // NDEV)

    gate_t, up_t = jnp.swapaxes(ffn_w_gate, 1, 2), jnp.swapaxes(ffn_w_up, 1, 2)
    m_gate_t, m_up_t = jnp.swapaxes(m_ffn_w_gate, 1, 2), jnp.swapaxes(m_ffn_w_up, 1, 2)
    v_gate_t, v_up_t = jnp.swapaxes(v_ffn_w_gate, 1, 2), jnp.swapaxes(v_ffn_w_up, 1, 2)

    mixer_in = (conv_w_in, attn_w_qkv)
    mixer_out = (conv_w_out, attn_w_out)
    taps_shard = jnp.pad(conv_kernel.reshape(6, D // NDEV), ((0, 2), (0, 0)))
    wts = []
    for i in range(DEPTH):
        j = i // 2
        shards = [mixer_in[i % 2][j].astype(BF), mixer_out[i % 2][j].astype(BF),
                  _pad_rows(gate_t[i].astype(BF), FF_SHARD_PAD),
                  _pad_rows(up_t[i].astype(BF), FF_SHARD_PAD),
                  _pad_rows(ffn_w_down[i].astype(BF), FF_SHARD_PAD)]
        axes = (1, 0, 0, 0, 0)
        groups = ((0, 1), (1, 2), (2, 4), (4, 5)) if i == 0 else ((0, 2), (2, 5))
        layer = []
        for lo, hi in groups:
            extra = [taps_shard] if (i, lo) == (0, 1) else []
            got = list(all_gather_weights(shards[lo:hi] + extra, axes[lo:hi] + (1,) * len(extra),
                                          f"ag_l{i}_{lo}"))
            if extra:
                taps_all = got.pop()
            layer += got
        wts.append(layer)
    taps = [jnp.pad(taps_all[3 * j:3 * j + 3], ((0, 5), (0, 0))) for j in range(2)]

    onehot_t, band = _bucket_onehot_t()
    bias3 = bias_tables(rel_bias.T, onehot_t, band, "bias_tables").reshape(3, H * BLK, 2 * BLK)

    saved = []
    xc = x[0]
    for i in range(DEPTH):
        w_in, w_out, w_g, w_u, w_d = wts[i]
        j = i // 2
        x_mix = xc
        z3, h_mix = norm_matmul3(xc, mix_norm[i:i + 1], w_in, f"mix_in_l{i}")
        if i % 2 == 0:
            act = conv_fwd(z3, taps[j], f"conv_fwd_l{i}")
            lse_b = None
        else:
            act, lse_b = attention_fwd(z3, bias3, f"attn_fwd_l{i}")
        xc = matmul_residual(act, w_out, xc, f"mix_out_l{i}")
        x_ffn = xc
        g, u, a, h_ffn = norm_swiglu_up(xc, ffn_norm[i:i + 1], w_g, w_u, f"ffn_up_l{i}")
        xc = matmul_residual(a, w_d, xc, f"ffn_down_l{i}")
        saved.append((x_mix, h_mix, z3, act, lse_b, x_ffn, h_ffn, g, u, a))

    dx, dxb, dg_final, sq = loss_head(xc, final_norm.reshape(1, D), loss_target[0], "loss_head")
    loss_local = 0.5 * jnp.sum(sq[0]) / D

    dg_mix = [None] * DEPTH
    dg_ffn = [None] * DEPTH
    dtaps = [None, None]
    dbias_all = []
    shape_in, shape_out = (D, 3 * D // NDEV), (D // NDEV, D)
    ffn_axes, ffn_shapes = (0, 0, 0), ((FF_SHARD_PAD, D),) * 3
    stacked = {}

    def pair_stage(grads, landed1, axes, shapes, tag, tok):
        parts = pair_add(grads, landed1, axes, shapes, c_idx, f"rs_add_{tag}", deps=[tok])
        return parts, chip_exchange_grads(parts, f"rs_chip_{tag}"), parts[-1]

    def adamw_stage(parts, landed2, params, tag, tok):
        names = [p[0] for p in params]
        res = reduce_adamw(parts, landed2, [p[1:] for p in params], q_idx, f"adamw_{tag}",
                           [stacked.get(nm, ()) for nm in names], deps=[tok])
        for nm, r4 in zip(names, res):
            stacked[nm] = r4
        return res[-1][0]

    tok = dxb
    mix_wait = None
    mix_chip = None
    ffn_chip = []
    for i in reversed(range(DEPTH)):
        w_in, w_out, w_g, w_u, w_d = wts[i]
        j = i // 2
        x_mix, h_mix, z3, act, lse_b, x_ffn, h_ffn, g, u, a = saved[i]
        ffn_params = [("ffn_w_gate", gate_t, m_gate_t, v_gate_t, i),
                      ("ffn_w_up", up_t, m_up_t, v_up_t, i),
                      ("ffn_w_down", ffn_w_down, m_ffn_w_down, v_ffn_w_down, i)]
        if i % 2 == 0:
            mix_params = [("conv_w_in", conv_w_in, m_conv_w_in, v_conv_w_in, j),
                          ("conv_w_out", conv_w_out, m_conv_w_out, v_conv_w_out, j)]
        else:
            mix_params = [("attn_w_qkv", attn_w_qkv, m_attn_w_qkv, v_attn_w_qkv, j),
                          ("attn_w_out", attn_w_out, m_attn_w_out, v_attn_w_out, j)]
        dgate, dup = swiglu_bwd_da(dxb, w_d, g, u, f"ffn_da_l{i}", deps=[tok])
        tok = dgate
        for group in ffn_chip:
            tok = adamw_stage(*group, tok)
        ffn_chip = []
        if mix_wait is not None:
            grads_m, landed1_m, params_m, tag_m = mix_wait
            parts_m, landed2_m, tok = pair_stage(grads_m, landed1_m, (1, 0), (shape_in, shape_out),
                                                 tag_m, tok)
            mix_chip = (parts_m, landed2_m, params_m, tag_m)
            mix_wait = None
        products = [(dgate, h_ffn), (dup, h_ffn), (a, dxb)]
        ffn_wait = []
        for gi, idxs in enumerate(((0,), (1, 2)) if i == 0 else ((0, 1, 2),)):
            grads_f = matmul_tn_group([products[k] for k in idxs], f"ffn_dw_l{i}_{gi}", deps=[tok])
            landed1_f = pair_exchange_grads(grads_f, ffn_axes[:len(idxs)], ffn_shapes[:len(idxs)],
                                            f"rs_pair_f{i}_{gi}")
            ffn_wait.append((grads_f, landed1_f, [ffn_params[k] for k in idxs], f"f{i}_{gi}"))
            tok = grads_f[-1]

        def ffn_pair_stages(groups, tok):
            for grads_f, landed1_f, params_f, tag_f in groups:
                nf = len(grads_f)
                parts_f, landed2_f, tok = pair_stage(grads_f, landed1_f, ffn_axes[:nf],
                                                     ffn_shapes[:nf], tag_f, tok)
                ffn_chip.append((parts_f, landed2_f, params_f, tag_f))
            return tok

        if len(ffn_wait) > 1:
            if mix_chip is not None:
                tok = adamw_stage(*mix_chip, tok)
                mix_chip = None
            tok = ffn_pair_stages(ffn_wait[:-1], tok)
        dx, dxb, dg_ffn[i] = matmul_normbwd(
            [(dgate, w_g, False), (dup, w_u, False)], x_ffn, ffn_norm[i:i + 1], dx, f"ffn_dh_l{i}",
            deps=[tok])
        dxb_mix = dxb
        tok = dxb
        if mix_chip is not None:
            tok = adamw_stage(*mix_chip, tok)
            mix_chip = None
        tok = ffn_pair_stages(ffn_wait[-1:], tok)
        dact = matmul_nt(dxb, w_out, f"mix_dact_l{i}", out_dtype=F32 if i % 2 == 0 else BF,
                         deps=[tok])
        if i % 2 == 0:
            dz3, dtaps[j] = conv_bwd(dact, z3, taps[j], f"conv_bwd_l{i}")
        else:
            dz3, dsum = attention_bwd(z3, dact, act, lse_b, bias3, f"attn_bwd_l{i}")
            dbias_all.append(dsum[:, :6].reshape(H // 2, 3, 2, 2 * BLK).transpose(1, 0, 2, 3)
                             .reshape(3, H, 2 * BLK))
        grads_m = matmul_tn_group([(h_mix, dz3), (act, dxb_mix)], f"mix_dw_l{i}")
        landed1_m = pair_exchange_grads(grads_m, (1, 0), (shape_in, shape_out), f"rs_pair_m{i}")
        mix_wait = (grads_m, landed1_m, mix_params, f"m{i}")
        dx, dxb, dg_mix[i] = matmul_normbwd(
            [(dz3, w_in, True)], x_mix, mix_norm[i:i + 1], dx, f"mix_dh_l{i}", deps=[grads_m[-1]])
        tok = dxb
    for group in ffn_chip:
        tok = adamw_stage(*group, tok)
    grads_m, landed1_m, params_m, tag_m = mix_wait
    parts_m, landed2_m, tok = pair_stage(grads_m, landed1_m, (1, 0), (shape_in, shape_out), tag_m, tok)

    grad_relb_t = bias_grad(jnp.concatenate(dbias_all), _diagonal_onehot_t(), "bias_grad")
    dtaps_full = jnp.stack([dtaps[0][:3], dtaps[1][:3]])
    g_small = _pack_small(jnp.concatenate([d[0:1] for d in dg_mix], axis=0),
                          jnp.concatenate([d[0:1] for d in dg_ffn], axis=0),
                          dg_final[0], dtaps_full, grad_relb_t.T, loss_local)
    gathered = all_gather_small(g_small, "ag_small_grads")

    def taps_at_cols(k):
        return lax.dynamic_update_slice(jnp.zeros((2, 3, D), F32), k, (0, 0, col0))

    w_small = _pack_small(mix_norm, ffn_norm, final_norm, taps_at_cols(conv_kernel), rel_bias)
    m_small = _pack_small(m_mix_norm, m_ffn_norm, m_final_norm, taps_at_cols(m_conv_kernel), m_rel_bias)
    v_small = _pack_small(v_mix_norm, v_ffn_norm, v_final_norm, taps_at_cols(v_conv_kernel), v_rel_bias)
    small = small_reduce_adamw(gathered, w_small, m_small, v_small, "adamw_small")

    def unpack_small(p):
        taps_p = lax.dynamic_slice(p[9:15].reshape(2, 3, D), (0, 0, col0), (2, 3, D // NDEV))
        return {"mix_norm": p[0:4], "ffn_norm": p[4:8], "final_norm": p[8],
                "conv_kernel": taps_p, "rel_bias": p[15, :NUM_BUCKETS * H].reshape(NUM_BUCKETS, H)}

    small_out = [unpack_small(p) for p in small]
    loss = small[0][15, NUM_BUCKETS * H]
    adamw_stage(parts_m, landed2_m, params_m, tag_m, small[0])

    names = ["mix_norm", "ffn_norm", "final_norm", "conv_w_in", "conv_kernel", "conv_w_out",
             "attn_w_qkv", "attn_w_out", "rel_bias", "ffn_w_gate", "ffn_w_up", "ffn_w_down"]
    outs = [loss, dx.reshape(1, S, D)]
    for o in range(4):
        for nme in names:
            if nme in ("ffn_w_gate", "ffn_w_up"):
                outs.append(jnp.swapaxes(stacked[nme][o], 1, 2))
            else:
                outs.append(stacked[nme][o] if nme in stacked else small_out[o][nme])
    return tuple(outs)
```

```python
import math

import numpy as np
import jax
import jax.numpy as jnp
from jax import lax
from jax.experimental import pallas as pl
from jax.experimental.pallas import tpu as pltpu
from jax.experimental.pallas import tpu_sc as plsc

S = 2048
D = 1024
H = 16
DH = 64
DFF = 2816
NDEV = 8
DEPTH = 4
FF_SHARD = DFF // NDEV
FF_SHARD_PAD = 384
DFF_PAD = FF_SHARD_PAD * NDEV
BLK = 128
BRANCH_DILATIONS = (1, 4, 16)
NUM_BUCKETS = 32
MAX_DISTANCE = 2048
EPS = 1e-6
NEG_INF = -1e30
SCALE = DH ** -0.5

ADAM_LR = 0.001
ADAM_B1 = 0.9
ADAM_B2 = 0.999
ADAM_EPS = 1e-08
ADAM_WD = 0.01
ADAM_STEP = 10

BF = jnp.bfloat16
F32 = jnp.float32
VMEM_LIMIT_BYTES = 56 * 1024 * 1024
KSPLIT = 512
NORM_CHUNK = 256
MESH = pl.DeviceIdType.MESH
ANY = pl.BlockSpec(memory_space=pl.ANY)

_NT = (((1,), (1,)), ((), ()))
_TN = (((0,), (0,)), ((), ()))


def _cparams(sem=None):
    return pltpu.CompilerParams(dimension_semantics=sem, vmem_limit_bytes=VMEM_LIMIT_BYTES)


def _after(body, n, deps):
    nd = len(deps)
    if nd == 0:
        return body

    def ordered(*refs):
        body(*refs[:n], *refs[n + nd:])
    return ordered


def _rms(x):
    return lax.rsqrt(jnp.mean(x * x, axis=-1, keepdims=True) + EPS)


def norm_matmul3(x, gain, w, name, tm=1024, tn=1024):
    per = D // tn

    def body(x_ref, g_ref, w_ref, z_ref, h_ref, hs_ref):
        @pl.when(pl.program_id(1) == 0)
        def _():
            for c in range(tm // NORM_CHUNK):
                rows = slice(c * NORM_CHUNK, (c + 1) * NORM_CHUNK)
                xv = x_ref[rows, :]
                hv = (xv * _rms(xv) * g_ref[...]).astype(BF)
                hs_ref[rows, :] = hv
                h_ref[rows, :] = hv
                z_ref[rows, :] = jnp.dot(hv, w_ref[...], preferred_element_type=F32).astype(BF)

        @pl.when(pl.program_id(1) > 0)
        def _():
            z_ref[...] = jnp.dot(hs_ref[...], w_ref[...], preferred_element_type=F32).astype(BF)

    return pl.pallas_call(
        body, name=name,
        grid=(S // tm, 3 * D // tn),
        in_specs=[pl.BlockSpec((tm, D), lambda i, j: (i, 0)),
                  pl.BlockSpec((None, 1, D), lambda i, j: (gain[1], 0, 0)),
                  pl.BlockSpec((D, tn), lambda i, j: (0, j))],
        out_specs=[pl.BlockSpec((None, tm, tn), lambda i, j: (j // per, i, j % per)),
                   pl.BlockSpec((tm, D), lambda i, j: (i, 0))],
        out_shape=[jax.ShapeDtypeStruct((3, S, D), BF), jax.ShapeDtypeStruct((S, D), BF)],
        scratch_shapes=[pltpu.VMEM((tm, D), BF)],
        compiler_params=_cparams(("parallel", "arbitrary")),
    )(x, gain[0], w)


def norm_swiglu_up(x, gain, wg_t, wu_t, name, tm=1024, tn=768):
    def body(x_ref, g_ref, wg_ref, wu_ref, go_ref, uo_ref, ao_ref, h_ref, hs_ref):
        def gate_up(hv, rows):
            g = lax.dot_general(hv, wg_ref[...], _NT, preferred_element_type=F32)
            u = lax.dot_general(hv, wu_ref[...], _NT, preferred_element_type=F32)
            go_ref[rows, :] = g.astype(BF)
            uo_ref[rows, :] = u.astype(BF)
            ao_ref[rows, :] = (g * jax.nn.sigmoid(g) * u).astype(BF)

        @pl.when(pl.program_id(1) == 0)
        def _():
            for c in range(tm // NORM_CHUNK):
                rows = slice(c * NORM_CHUNK, (c + 1) * NORM_CHUNK)
                xv = x_ref[rows, :]
                hv = (xv * _rms(xv) * g_ref[...]).astype(BF)
                hs_ref[rows, :] = hv
                h_ref[rows, :] = hv
                gate_up(hv, rows)

        @pl.when(pl.program_id(1) > 0)
        def _():
            gate_up(hs_ref[...], slice(None))

    act = jax.ShapeDtypeStruct((S, DFF_PAD), BF)
    blk = pl.BlockSpec((tm, tn), lambda i, j: (i, j))
    return pl.pallas_call(
        body, name=name,
        grid=(S // tm, DFF_PAD // tn),
        in_specs=[pl.BlockSpec((tm, D), lambda i, j: (i, 0)),
                  pl.BlockSpec((None, 1, D), lambda i, j: (gain[1], 0, 0)),
                  pl.BlockSpec((tn, D), lambda i, j: (j, 0)),
                  pl.BlockSpec((tn, D), lambda i, j: (j, 0))],
        out_specs=[blk, blk, blk, pl.BlockSpec((tm, D), lambda i, j: (i, 0))],
        out_shape=[act, act, act, jax.ShapeDtypeStruct((S, D), BF)],
        scratch_shapes=[pltpu.VMEM((tm, D), BF)],
        compiler_params=_cparams(("parallel", "arbitrary")),
    )(x, gain[0], wg_t, wu_t)


def matmul_residual(a, w, x, name, tm=1024):
    K = a.shape[1]
    tn = D if K <= D else D // 2
    ns = K // KSPLIT
    kc = K // ns

    def body(*refs):
        x_ref, o_ref = refs[2 * ns:]
        acc = x_ref[...]
        for s in range(ns):
            acc = acc + jnp.dot(refs[s][...], refs[ns + s][...], preferred_element_type=F32)
        o_ref[...] = acc

    return pl.pallas_call(
        body, name=name,
        grid=(S // tm, D // tn),
        in_specs=[pl.BlockSpec((tm, kc), lambda i, j, s=s: (i, s)) for s in range(ns)]
        + [pl.BlockSpec((kc, tn), lambda i, j, s=s: (s, j)) for s in range(ns)]
        + [pl.BlockSpec((tm, tn), lambda i, j: (i, j))],
        out_specs=pl.BlockSpec((tm, tn), lambda i, j: (i, j)),
        out_shape=jax.ShapeDtypeStruct((S, D), F32),
        compiler_params=_cparams(("parallel", "parallel")),
    )(*([a] * ns), *([w] * ns), x)


def matmul_nt(a, w, name, out_dtype=BF, tm=1024, tn=1024, deps=()):
    K = a.shape[1]
    N = w.shape[0]

    def body(a_ref, w_ref, o_ref):
        o_ref[...] = lax.dot_general(a_ref[...], w_ref[...], _NT,
                                     preferred_element_type=F32).astype(o_ref.dtype)

    return pl.pallas_call(
        _after(body, 2, deps), name=name,
        grid=(S // tm, N // tn),
        in_specs=[pl.BlockSpec((tm, K), lambda i, j: (i, 0)),
                  pl.BlockSpec((tn, K), lambda i, j: (j, 0))] + [ANY] * len(deps),
        out_specs=pl.BlockSpec((tm, tn), lambda i, j: (i, j)),
        out_shape=jax.ShapeDtypeStruct((S, N), out_dtype),
        compiler_params=_cparams(("parallel", "parallel")),
    )(a, w, *deps)


def swiglu_bwd_da(dxb, wd, g, u, name, tm=1024, tn=768, deps=()):
    def body(dx_ref, w_ref, g_ref, u_ref, dg_ref, du_ref):
        da = lax.dot_general(dx_ref[...], w_ref[...], _NT, preferred_element_type=F32)
        gv = g_ref[...].astype(F32)
        uv = u_ref[...].astype(F32)
        sig = jax.nn.sigmoid(gv)
        dg_ref[...] = (da * uv * (sig * (1.0 + gv * (1.0 - sig)))).astype(BF)
        du_ref[...] = (da * (gv * sig)).astype(BF)

    act = jax.ShapeDtypeStruct((S, DFF_PAD), BF)
    blk = pl.BlockSpec((tm, tn), lambda i, j: (i, j))
    return pl.pallas_call(
        _after(body, 4, deps), name=name,
        grid=(S // tm, DFF_PAD // tn),
        in_specs=[pl.BlockSpec((tm, D), lambda i, j: (i, 0)),
                  pl.BlockSpec((tn, D), lambda i, j: (j, 0)),
                  blk, blk] + [ANY] * len(deps),
        out_specs=[blk, blk],
        out_shape=[act, act],
        compiler_params=_cparams(("parallel", "parallel")),
    )(dxb, wd, g, u, *deps)


def matmul_tn_group(pairs, name, tm=1024, deps=()):
    P = len(pairs)
    tn = D if P <= 2 else D // 2
    steps = []
    for p, (a, b) in enumerate(pairs):
        N = 3 * D if b.ndim == 3 else b.shape[1]
        steps += [(p, i, j) for i in range(a.shape[1] // tm) for j in range(N // tn)]
    T = len(steps)
    tab = np.zeros((T, 1 + 2 * P), np.int32)
    for p in range(P):
        cur = (0, 0)
        for s, (ph, i, j) in enumerate(steps):
            if ph == p:
                cur = (i, j)
            tab[s, 1 + 2 * p:3 + 2 * p] = cur
    tab[:, 0] = [ph for ph, _, _ in steps]

    in_specs, out_specs, out_shapes, operands = [], [], [], []
    per = D // tn
    for p, (a, b) in enumerate(pairs):
        ci, cj = 1 + 2 * p, 2 + 2 * p
        in_specs.append(pl.BlockSpec((S, tm), lambda s, t, ci=ci: (0, t[s, ci])))
        if b.ndim == 3:
            in_specs.append(pl.BlockSpec((None, S, tn),
                                         lambda s, t, cj=cj: (t[s, cj] // per, 0, t[s, cj] % per)))
            N = 3 * D
        else:
            in_specs.append(pl.BlockSpec((S, tn), lambda s, t, cj=cj: (0, t[s, cj])))
            N = b.shape[1]
        out_specs.append(pl.BlockSpec((tm, tn), lambda s, t, ci=ci, cj=cj: (t[s, ci], t[s, cj])))
        out_shapes.append(jax.ShapeDtypeStruct((a.shape[1], N), BF))
        operands += [a, b]

    def body(tab_ref, *refs):
        phase = tab_ref[pl.program_id(0), 0]
        for p in range(P):
            @pl.when(phase == p)
            def _(p=p):
                refs[2 * P + p][...] = lax.dot_general(
                    refs[2 * p][...], refs[2 * p + 1][...], _TN,
                    preferred_element_type=F32).astype(BF)

    return pl.pallas_call(
        _after(body, 1 + 2 * P, deps), name=name,
        grid_spec=pltpu.PrefetchScalarGridSpec(
            num_scalar_prefetch=1, grid=(T,), in_specs=in_specs + [ANY] * len(deps),
            out_specs=out_specs),
        out_shape=out_shapes,
        compiler_params=_cparams(("arbitrary",)),
    )(jnp.asarray(tab), *operands, *deps)


def matmul_normbwd(terms, x_in, gain, dx, name, tm=512, ch=256, deps=()):
    specs, operands = [], []
    for (a, w, stacked) in terms:
        if stacked:
            specs.append(pl.BlockSpec((3, tm, D), lambda i: (0, i, 0)))
        else:
            specs.append(pl.BlockSpec((tm, a.shape[1]), lambda i: (i, 0)))
        specs.append(pl.BlockSpec(w.shape, lambda i: (0, 0), pipeline_mode=pl.Buffered(1)))
        operands += [a, w]
    nt = len(terms)

    def body(*refs):
        aw = refs[:2 * nt]
        x_ref, g_ref, dx_ref, dxo_ref, dxb_ref, dg_ref = refs[2 * nt:]

        @pl.when(pl.program_id(0) == 0)
        def _():
            dg_ref[...] = jnp.zeros_like(dg_ref)

        dgain = None
        for c in range(tm // ch):
            rows = slice(c * ch, (c + 1) * ch)
            dh = None
            for t, (_, _, stacked) in enumerate(terms):
                a_ref, w_ref = aw[2 * t], aw[2 * t + 1]
                if stacked:
                    parts = [lax.dot_general(a_ref[k, rows, :], w_ref[:, k * D:(k + 1) * D], _NT,
                                             preferred_element_type=F32) for k in range(3)]
                else:
                    parts = [jnp.dot(a_ref[rows, :], w_ref[...], preferred_element_type=F32)]
                for p in parts:
                    dh = p if dh is None else dh + p
            xv = x_ref[rows, :]
            r = _rms(xv)
            xhat = xv * r
            part = jnp.sum(dh * xhat, axis=0, keepdims=True)
            dgain = part if dgain is None else dgain + part
            dxh = dh * g_ref[...]
            dxn = r * (dxh - xhat * jnp.mean(dxh * xhat, axis=-1, keepdims=True))
            out = dx_ref[rows, :] + dxn
            dxo_ref[rows, :] = out
            dxb_ref[rows, :] = out.astype(BF)
        dg_ref[0:1, :] += dgain

    row = pl.BlockSpec((tm, D), lambda i: (i, 0))
    return pl.pallas_call(
        _after(body, 2 * nt + 3, deps), name=name,
        grid=(S // tm,),
        in_specs=specs + [row, pl.BlockSpec((None, 1, D), lambda i: (gain[1], 0, 0)), row]
        + [ANY] * len(deps),
        out_specs=[row, row, pl.BlockSpec((8, D), lambda i: (0, 0))],
        out_shape=[jax.ShapeDtypeStruct((S, D), F32), jax.ShapeDtypeStruct((S, D), BF),
                   jax.ShapeDtypeStruct((8, D), F32)],
        compiler_params=_cparams(("arbitrary",)),
    )(*operands, x_in, gain[0], dx, *deps)


def loss_head(x, gain, target, name, tm=512):
    def body(x_ref, g_ref, t_ref, dxo_ref, dxb_ref, dg_ref, sq_ref):
        @pl.when(pl.program_id(0) == 0)
        def _():
            dg_ref[...] = jnp.zeros_like(dg_ref)
            sq_ref[...] = jnp.zeros_like(sq_ref)
        xv = x_ref[...]
        r = _rms(xv)
        xhat = xv * r
        err = xhat * g_ref[...] - t_ref[...]
        sq_ref[0:1, :] += jnp.sum(err * err, axis=0, keepdims=True)
        dy = err * (1.0 / D)
        dg_ref[0:1, :] += jnp.sum(dy * xhat, axis=0, keepdims=True)
        dxh = dy * g_ref[...]
        out = r * (dxh - xhat * jnp.mean(dxh * xhat, axis=-1, keepdims=True))
        dxo_ref[...] = out
        dxb_ref[...] = out.astype(BF)

    row = pl.BlockSpec((tm, D), lambda i: (i, 0))
    acc = pl.BlockSpec((8, D), lambda i: (0, 0))
    return pl.pallas_call(
        body, name=name,
        grid=(S // tm,),
        in_specs=[row, pl.BlockSpec((1, D), lambda i: (0, 0)), row],
        out_specs=[row, row, acc, acc],
        out_shape=[jax.ShapeDtypeStruct((S, D), F32), jax.ShapeDtypeStruct((S, D), BF),
                   jax.ShapeDtypeStruct((8, D), F32), jax.ShapeDtypeStruct((8, D), F32)],
        compiler_params=_cparams(("arbitrary",)),
    )(x, gain, target)


CONV_TM = 256
HALO = 16


def _halo_row(halo, r):
    hrow = lax.broadcasted_iota(jnp.int32, halo.shape, 0)
    return jnp.sum(jnp.where(hrow == r, halo, 0.0), axis=0, keepdims=True)


def _prev_rows(p, halo_p, n, row):
    out = pltpu.roll(p, n, axis=0)
    for k in range(n):
        out = jnp.where(row == k, _halo_row(halo_p, HALO - n + k), out)
    return out


def _next_rows(p, halo_p, n, row):
    tm = p.shape[0]
    out = pltpu.roll(p, tm - n, axis=0)
    for k in range(n):
        out = jnp.where(row == tm - n + k, _halo_row(halo_p, k), out)
    return out


def _conv_specs():
    per = CONV_TM // HALO
    main = pl.BlockSpec((3, CONV_TM, D), lambda i: (0, i, 0))
    prev = pl.BlockSpec((3, HALO, D), lambda i: (0, jnp.maximum(i * per - 1, 0), 0))
    nxt = pl.BlockSpec((3, HALO, D), lambda i: (0, jnp.minimum((i + 1) * per, S // HALO - 1), 0))
    return main, prev, nxt


def conv_fwd(z3, taps, name):
    def body(z_ref, zp_ref, k_ref, m_ref):
        i = pl.program_id(0)
        p = z_ref[1].astype(F32) * z_ref[2].astype(F32)
        halo = zp_ref[1].astype(F32) * zp_ref[2].astype(F32) * jnp.where(i > 0, 1.0, 0.0)
        row = lax.broadcasted_iota(jnp.int32, p.shape, 0)
        y = (k_ref[2:3, :] * p + k_ref[1:2, :] * _prev_rows(p, halo, 1, row)
             + k_ref[0:1, :] * _prev_rows(p, halo, 2, row))
        m_ref[...] = (z_ref[0].astype(F32) * y).astype(BF)

    main, prev, _ = _conv_specs()
    return pl.pallas_call(
        body, name=name,
        grid=(S // CONV_TM,),
        in_specs=[main, prev, pl.BlockSpec((8, D), lambda i: (0, 0))],
        out_specs=pl.BlockSpec((CONV_TM, D), lambda i: (i, 0)),
        out_shape=jax.ShapeDtypeStruct((S, D), BF),
        compiler_params=_cparams(("parallel",)),
    )(z3, z3, taps)


def conv_bwd(dm, z3, taps, name, deps=()):
    per = CONV_TM // HALO

    def body(dm_ref, dmn_ref, z_ref, zp_ref, zn_ref, k_ref, dz_ref, dk_ref):
        i = pl.program_id(0)

        @pl.when(i == 0)
        def _():
            dk_ref[...] = jnp.zeros_like(dk_ref)

        dmv = dm_ref[...]
        b = z_ref[0].astype(F32)
        c = z_ref[1].astype(F32)
        u = z_ref[2].astype(F32)
        p = c * u
        halo_p = zp_ref[1].astype(F32) * zp_ref[2].astype(F32) * jnp.where(i > 0, 1.0, 0.0)
        halo_dy = (dmn_ref[...] * zn_ref[0].astype(F32)
                   * jnp.where(i < S // CONV_TM - 1, 1.0, 0.0))
        row = lax.broadcasted_iota(jnp.int32, p.shape, 0)
        p1 = _prev_rows(p, halo_p, 1, row)
        p2 = _prev_rows(p, halo_p, 2, row)
        y = k_ref[2:3, :] * p + k_ref[1:2, :] * p1 + k_ref[0:1, :] * p2
        dy = dmv * b
        dz_ref[0] = (dmv * y).astype(BF)
        dp = (k_ref[2:3, :] * dy + k_ref[1:2, :] * _next_rows(dy, halo_dy, 1, row)
              + k_ref[0:1, :] * _next_rows(dy, halo_dy, 2, row))
        dz_ref[1] = (dp * u).astype(BF)
        dz_ref[2] = (dp * c).astype(BF)
        dk_ref[0:1, :] += jnp.sum(dy * p2, axis=0, keepdims=True)
        dk_ref[1:2, :] += jnp.sum(dy * p1, axis=0, keepdims=True)
        dk_ref[2:3, :] += jnp.sum(dy * p, axis=0, keepdims=True)

    main, prev, nxt = _conv_specs()
    return pl.pallas_call(
        _after(body, 6, deps), name=name,
        grid=(S // CONV_TM,),
        in_specs=[pl.BlockSpec((CONV_TM, D), lambda i: (i, 0)),
                  pl.BlockSpec((HALO, D), lambda i: (jnp.minimum((i + 1) * per, S // HALO - 1), 0)),
                  main, prev, nxt, pl.BlockSpec((8, D), lambda i: (0, 0))] + [ANY] * len(deps),
        out_specs=[main, pl.BlockSpec((8, D), lambda i: (0, 0))],
        out_shape=[jax.ShapeDtypeStruct((3, S, D), BF), jax.ShapeDtypeStruct((8, D), F32)],
        compiler_params=_cparams(("arbitrary",)),
    )(dm, dm, z3, z3, z3, taps, *deps)


def _t5_bucket(dist):
    exact = NUM_BUCKETS // 2
    df = jnp.maximum(dist, 1).astype(jnp.float32)
    large = exact + (jnp.log(df / exact) / math.log(MAX_DISTANCE / exact)
                     * (NUM_BUCKETS - exact)).astype(jnp.int32)
    large = jnp.minimum(large, NUM_BUCKETS - 1)
    return jnp.where(dist < exact, dist, large)


def _bucket_onehot_t():
    qi = jnp.arange(BLK)[:, None]
    ki = jnp.arange(2 * BLK)[None, :]
    rel = qi + BLK - ki
    band = ((rel >= 0) & (rel <= BLK)).reshape(1, -1).astype(F32)
    hots = []
    for d in BRANCH_DILATIONS:
        bucket = _t5_bucket(jnp.clip(rel, 0) * d).reshape(1, -1)
        hots.append((jnp.arange(NUM_BUCKETS)[:, None] == bucket).astype(F32))
    return jnp.stack(hots), band


def bias_tables(rel_bias_t, onehot_t, band, name):
    def body(rb_ref, oh_ref, band_ref, o_ref):
        b = jnp.dot(rb_ref[...], oh_ref[...], preferred_element_type=F32,
                    precision=lax.Precision.HIGHEST)
        o_ref[...] = jnp.where(band_ref[...] > 0.5, b, NEG_INF)

    n = BLK * 2 * BLK
    return pl.pallas_call(
        body, name=name,
        grid=(3,),
        in_specs=[pl.BlockSpec((H, NUM_BUCKETS), lambda g: (0, 0)),
                  pl.BlockSpec((None, NUM_BUCKETS, n), lambda g: (g, 0, 0)),
                  pl.BlockSpec((1, n), lambda g: (0, 0))],
        out_specs=pl.BlockSpec((None, H, n), lambda g: (g, 0, 0)),
        out_shape=jax.ShapeDtypeStruct((3, H, n), F32),
        compiler_params=_cparams(("parallel",)),
    )(rel_bias_t, onehot_t, band)


def _diagonal_onehot_t():
    c = jnp.arange(BLK)
    dist = jnp.concatenate([c + 1, (c + 1) % BLK])[None, :]
    hots = [(jnp.arange(NUM_BUCKETS)[:, None] == _t5_bucket(dist * d)).astype(F32)
            for d in BRANCH_DILATIONS]
    return jnp.stack(hots)


def bias_grad(dsums, onehot_t, name):
    def body(ds_ref, oh_ref, o_ref):
        @pl.when(pl.program_id(0) == 0)
        def _():
            o_ref[...] = jnp.zeros_like(o_ref)
        o_ref[...] += lax.dot_general(ds_ref[...], oh_ref[...], _NT, preferred_element_type=F32,
                                      precision=lax.Precision.HIGHEST)

    return pl.pallas_call(
        body, name=name,
        grid=(dsums.shape[0],),
        in_specs=[pl.BlockSpec((None, H, 2 * BLK), lambda g: (g, 0, 0)),
                  pl.BlockSpec((None, NUM_BUCKETS, 2 * BLK), lambda g: (g % 3, 0, 0))],
        out_specs=pl.BlockSpec((H, NUM_BUCKETS), lambda g: (0, 0)),
        out_shape=jax.ShapeDtypeStruct((H, NUM_BUCKETS), F32),
        compiler_params=_cparams(("arbitrary",)),
    )(dsums, onehot_t)


def _head_masks():
    lane = lax.broadcasted_iota(jnp.int32, (1, 2 * DH), 1)
    return (lane < DH, lane >= DH)


def _stack_heads(x, masks):
    zero = jnp.zeros_like(x)
    return jnp.concatenate([jnp.where(masks[0], x, zero), jnp.where(masks[1], x, zero)], axis=0)


def _deinterleave(src_ref, dst_ref, d, dtype):
    L = S // d
    for r in range(d):
        dst_ref[r * L:(r + 1) * L, :] = src_ref[pl.ds(r, L, stride=d), :].astype(dtype)


def _branch_loops(d, block):
    L = S // d
    for r in range(d):
        base = r * L
        block(base, base, BLK, True)
        for n in range(1, L // BLK):
            block(base + n * BLK, base + (n - 1) * BLK, 2 * BLK, False)


def attention_fwd(z3, bias3, name):
    W = 2 * DH
    CH = 256

    def body(q_ref, k_ref, v_ref, b_ref, o_ref, lse_ref, stage, qd, kd, vd, od, ld, on, ln):
        masks = _head_masks()
        for src, dst in ((q_ref, qd), (k_ref, kd), (v_ref, vd)):
            stage[...] = src[...].astype(F32)
            for gi, d in enumerate(BRANCH_DILATIONS[1:]):
                _deinterleave(stage, dst.at[gi], d, BF)

        for g, d in enumerate(BRANCH_DILATIONS):
            qs, ks, vs = (q_ref, k_ref, v_ref) if d == 1 else (qd.at[g - 1], kd.at[g - 1], vd.at[g - 1])
            o_dst, l_dst = (on.at[0], ln.at[0]) if d == 1 else (od, ld)

            def block(q0, k0, nk, first, g=g, qs=qs, ks=ks, vs=vs, o_dst=o_dst, l_dst=l_dst):
                q2 = _stack_heads(qs[pl.ds(q0, BLK), :], masks)
                kk = ks[pl.ds(k0, nk), :]
                vv = vs[pl.ds(k0, nk), :]
                bias = b_ref[g][:, BLK:] if first else b_ref[g]
                s = lax.dot_general(q2, kk, _NT, preferred_element_type=F32) * SCALE + bias
                mx = jnp.max(s, axis=1, keepdims=True)
                p = jnp.exp(s - mx)
                l = jnp.sum(p, axis=1, keepdims=True)
                o2 = jnp.dot(p.astype(BF), vv, preferred_element_type=F32) / l
                lse2 = mx + jnp.log(l)
                o_dst[pl.ds(q0, BLK), :] = jnp.where(masks[0], o2[:BLK], o2[BLK:])
                l_dst[pl.ds(q0, BLK), :] = jnp.where(masks[0], lse2[:BLK], lse2[BLK:])

            _branch_loops(d, block)
            if d > 1:
                L = S // d
                for r in range(d):
                    on[g, pl.ds(r, L, stride=d), :] = od[r * L:(r + 1) * L, :]
                    ln[g, pl.ds(r, L, stride=d), :] = ld[r * L:(r + 1) * L, :]

        def join(c, carry):
            rows = pl.ds(pl.multiple_of(c * CH, CH), CH)
            a, b, cc = ln[0, rows, :], ln[1, rows, :], ln[2, rows, :]
            mx = jnp.maximum(jnp.maximum(a, b), cc)
            ea, eb, ec = jnp.exp(a - mx), jnp.exp(b - mx), jnp.exp(cc - mx)
            tot = ea + eb + ec
            o_ref[rows, :] = ((ea * on[0, rows, :] + eb * on[1, rows, :] + ec * on[2, rows, :])
                              / tot).astype(BF)
            lse_ref[rows, :] = mx + jnp.log(tot)
            return carry
        lax.fori_loop(0, S // CH, join, 0)

    col = pl.BlockSpec((S, W), lambda hp: (0, hp))
    return pl.pallas_call(
        body, name=name,
        grid=(D // W,),
        in_specs=[pl.BlockSpec((None, S, W), lambda hp: (0, 0, hp)),
                  pl.BlockSpec((None, S, W), lambda hp: (1, 0, hp)),
                  pl.BlockSpec((None, S, W), lambda hp: (2, 0, hp)),
                  pl.BlockSpec((3, 2 * BLK, 2 * BLK), lambda hp: (0, hp, 0))],
        out_specs=[col, col],
        out_shape=[jax.ShapeDtypeStruct((S, D), BF), jax.ShapeDtypeStruct((S, D), F32)],
        scratch_shapes=[pltpu.VMEM((S, W), F32),
                        pltpu.VMEM((2, S, W), BF), pltpu.VMEM((2, S, W), BF), pltpu.VMEM((2, S, W), BF),
                        pltpu.VMEM((S, W), F32), pltpu.VMEM((S, W), F32),
                        pltpu.VMEM((3, S, W), F32), pltpu.VMEM((3, S, W), F32)],
        compiler_params=_cparams(("parallel",)),
    )(z3, z3, z3, bias3)


def attention_bwd(z3, dob, ob, lse_b, bias3, name, deps=()):
    W = 2 * DH
    CH = 256

    def body(q_ref, k_ref, v_ref, do_ref, o_ref, lse_ref, b_ref, dz_ref, dsum_ref,
             stage, delta, qd, kd, vd, dod, lsd, dld, res, acc, db_ref):
        masks = _head_masks()

        def rowsum(c, carry):
            rows = pl.ds(pl.multiple_of(c * CH, CH), CH)
            prod = do_ref[rows, :].astype(F32) * o_ref[rows, :].astype(F32)
            sa = jnp.sum(jnp.where(masks[0], prod, 0.0), axis=1, keepdims=True)
            sb = jnp.sum(jnp.where(masks[1], prod, 0.0), axis=1, keepdims=True)
            delta[rows, :] = jnp.where(masks[0], sa, sb)
            return carry
        lax.fori_loop(0, S // CH, rowsum, 0)

        for src, dst in ((q_ref, qd), (k_ref, kd), (v_ref, vd), (do_ref, dod)):
            stage[...] = src[...].astype(F32)
            for gi, d in enumerate(BRANCH_DILATIONS[1:]):
                _deinterleave(stage, dst.at[gi], d, BF)
        for gi, d in enumerate(BRANCH_DILATIONS[1:]):
            _deinterleave(lse_ref, lsd.at[gi], d, F32)
            _deinterleave(delta, dld.at[gi], d, F32)

        db_ref[...] = jnp.zeros_like(db_ref)
        for g, d in enumerate(BRANCH_DILATIONS):
            if d == 1:
                qs, ks, vs, dos, ls, dl = q_ref, k_ref, v_ref, do_ref, lse_ref, delta
            else:
                qs, ks, vs, dos = qd.at[g - 1], kd.at[g - 1], vd.at[g - 1], dod.at[g - 1]
                ls, dl = lsd.at[g - 1], dld.at[g - 1]
            res[1] = jnp.zeros((S, W), F32)
            res[2] = jnp.zeros((S, W), F32)

            def block(q0, k0, nk, first, g=g, qs=qs, ks=ks, vs=vs, dos=dos, ls=ls, dl=dl):
                kk = ks[pl.ds(k0, nk), :]
                vv = vs[pl.ds(k0, nk), :]
                q2 = _stack_heads(qs[pl.ds(q0, BLK), :], masks)
                do2 = _stack_heads(dos[pl.ds(q0, BLK), :], masks)
                lse_blk = ls[pl.ds(q0, BLK), :]
                del_blk = dl[pl.ds(q0, BLK), :]
                lse2 = jnp.concatenate([lse_blk[:, 0:1], lse_blk[:, DH:DH + 1]], axis=0)
                del2 = jnp.concatenate([del_blk[:, 0:1], del_blk[:, DH:DH + 1]], axis=0)
                bias = b_ref[g][:, BLK:] if first else b_ref[g]
                s = lax.dot_general(q2, kk, _NT, preferred_element_type=F32) * SCALE + bias
                p = jnp.exp(s - lse2)
                dp = lax.dot_general(do2, vv, _NT, preferred_element_type=F32)
                ds = p * (dp - del2)
                if first:
                    db_ref[g, :, BLK:] += ds
                else:
                    db_ref[g] += ds
                dsb = ds.astype(BF)
                dq2 = jnp.dot(dsb, kk, preferred_element_type=F32) * SCALE
                res[0, pl.ds(q0, BLK), :] = jnp.where(masks[0], dq2[:BLK], dq2[BLK:])
                res[1, pl.ds(k0, nk), :] += lax.dot_general(dsb, q2, _TN,
                                                            preferred_element_type=F32) * SCALE
                res[2, pl.ds(k0, nk), :] += lax.dot_general(p.astype(BF), do2, _TN,
                                                            preferred_element_type=F32)

            _branch_loops(d, block)
            L = S // d
            for t in range(3):
                if d == 1:
                    acc[t] = res[t]
                else:
                    for r in range(d):
                        acc[t, pl.ds(r, L, stride=d), :] = (acc[t, pl.ds(r, L, stride=d), :]
                                                            + res[t, r * L:(r + 1) * L, :])
        for t in range(3):
            dz_ref[t] = acc[t].astype(BF)

        flip = (lax.broadcasted_iota(jnp.int32, (BLK, BLK), 0)
                + lax.broadcasted_iota(jnp.int32, (BLK, BLK), 1) == BLK - 1).astype(BF)
        dsum_ref[...] = jnp.zeros_like(dsum_ref)
        for g in range(3):
            for hh in range(2):
                halves = []
                for half in range(2):
                    tile = db_ref[g, hh * BLK:(hh + 1) * BLK, half * BLK:(half + 1) * BLK]
                    hi = tile.astype(BF)
                    lo = (tile - hi.astype(F32)).astype(BF)
                    rev = (jnp.dot(hi, flip, preferred_element_type=F32)
                           + jnp.dot(lo, flip, preferred_element_type=F32))
                    skew = pltpu.roll(rev, 0, 1, stride=1, stride_axis=0)
                    halves.append(jnp.sum(skew, axis=0, keepdims=True))
                dsum_ref[2 * g + hh:2 * g + hh + 1, :] = jnp.concatenate(halves, axis=1)

    col = pl.BlockSpec((S, W), lambda hp: (0, hp))
    return pl.pallas_call(
        _after(body, 7, deps), name=name,
        grid=(D // W,),
        in_specs=[pl.BlockSpec((None, S, W), lambda hp: (0, 0, hp)),
                  pl.BlockSpec((None, S, W), lambda hp: (1, 0, hp)),
                  pl.BlockSpec((None, S, W), lambda hp: (2, 0, hp)),
                  col, col, col,
                  pl.BlockSpec((3, 2 * BLK, 2 * BLK), lambda hp: (0, hp, 0))] + [ANY] * len(deps),
        out_specs=[pl.BlockSpec((3, S, W), lambda hp: (0, 0, hp)),
                   pl.BlockSpec((None, 8, 2 * BLK), lambda hp: (hp, 0, 0))],
        out_shape=[jax.ShapeDtypeStruct((3, S, D), BF),
                   jax.ShapeDtypeStruct((D // W, 8, 2 * BLK), F32)],
        scratch_shapes=[pltpu.VMEM((S, W), F32), pltpu.VMEM((S, W), F32),
                        pltpu.VMEM((2, S, W), BF), pltpu.VMEM((2, S, W), BF),
                        pltpu.VMEM((2, S, W), BF), pltpu.VMEM((2, S, W), BF),
                        pltpu.VMEM((2, S, W), F32), pltpu.VMEM((2, S, W), F32),
                        pltpu.VMEM((3, S, W), F32), pltpu.VMEM((3, S, W), F32),
                        pltpu.VMEM((3, 2 * BLK, 2 * BLK), F32)],
        compiler_params=_cparams(("parallel",)),
    )(z3, z3, z3, dob, ob, lse_b, bias3, *deps)


def _me():
    return lax.axis_index("x"), lax.axis_index("y"), lax.axis_index("c")


def _other_chips(x, y):
    return [(1 - x, y), (x, 1 - y), (1 - x, 1 - y)]


def _shard_window(ref, axis, t, shape):
    R, C = shape
    if axis == 0:
        return ref.at[pl.ds(pl.multiple_of(t * R, 128), R), :]
    return ref.at[:, pl.ds(pl.multiple_of(t * C, 128), C)]


def all_gather_weights(shards, axes, name):
    n = len(shards)
    shapes = [s.shape for s in shards]
    outs_shape = [jax.ShapeDtypeStruct((8 * s.shape[0], s.shape[1]) if ax == 0
                                       else (s.shape[0], 8 * s.shape[1]), s.dtype)
                  for s, ax in zip(shards, axes)]

    def body(*refs):
        ins, outs = refs[:n], refs[n:2 * n]
        send_sems, recv_sems, local_sems = refs[2 * n:]
        x, y, c = _me()
        me, sibling = (x, y, c), (x, y, 1 - c)
        xnb, ynb, diag = (1 - x, y), (x, 1 - y), (1 - x, 1 - y)
        south = c == 0
        relay_from = (jnp.where(south, x, 1 - x), jnp.where(south, 1 - y, y))
        relay_to = (jnp.where(south, 1 - x, x), jnp.where(south, y, 1 - y))
        barrier = pltpu.get_barrier_semaphore()
        for peer in [sibling, (*xnb, c), (*ynb, c)]:
            pl.semaphore_signal(barrier, inc=1, device_id=peer, device_id_type=MESH)
        pl.semaphore_wait(barrier, 3)

        def win(i, px, py, pc):
            return _shard_window(outs[i], axes[i], 4 * px + 2 * py + pc, shapes[i])

        def copy(i, k, block, to, src=None):
            return pltpu.make_async_remote_copy(
                src_ref=win(i, *block) if src is None else src, dst_ref=win(i, *block),
                send_sem=send_sems.at[i * 7 + k], recv_sem=recv_sems.at[i * 7 + k],
                device_id=to, device_id_type=MESH)

        mine = [pltpu.make_async_copy(ins[i], win(i, *me), local_sems.at[i]) for i in range(n)]
        for cp in mine:
            cp.start()
        sent = []
        for i in range(n):
            sent += [copy(i, 0, me, sibling, src=ins[i]), copy(i, 1, me, (*xnb, c), src=ins[i]),
                     copy(i, 2, me, (*ynb, c), src=ins[i])]
        for cp in sent:
            cp.start()
        for i in range(n):
            for k, chip in ((1, xnb), (2, ynb)):
                copy(i, k, (*chip, c), me).wait_recv()
                sent.append(copy(i, 3 + k, (*chip, c), sibling))
                sent[-1].start()
            sent.append(copy(i, 3, (*relay_from, c), (*relay_to, c)))
            sent[-1].start()
        for i in range(n):
            copy(i, 3, (*diag, c), me).wait_recv()
            sent.append(copy(i, 6, (*diag, c), sibling))
            sent[-1].start()
        for i in range(n):
            copy(i, 0, sibling, me).wait_recv()
            for k, chip in ((4, xnb), (5, ynb), (6, diag)):
                copy(i, k, (*chip, 1 - c), me).wait_recv()
        for cp in sent:
            cp.wait_send()
        for cp in mine:
            cp.wait()

    return pl.kernel(
        body, out_type=outs_shape, name=name,
        mesh=plsc.ScalarSubcoreMesh(axis_name="sequencer", num_cores=1),
        scratch_types=[pltpu.SemaphoreType.DMA((7 * n,)), pltpu.SemaphoreType.DMA((7 * n,)),
                       pltpu.SemaphoreType.DMA((n,))],
        compiler_params=pltpu.CompilerParams(collective_id=1),
    )(*shards)


def pair_exchange_grads(grads, axes, shapes, name):
    n = len(grads)

    def body(*refs):
        ins, outs = refs[:n], refs[n:2 * n]
        send_sems, recv_sems = refs[2 * n:]
        x, y, c = _me()
        sibling = (x, y, 1 - c)
        barrier = pltpu.get_barrier_semaphore()
        pl.semaphore_signal(barrier, inc=1, device_id=sibling, device_id_type=MESH)
        pl.semaphore_wait(barrier, 1)
        copies = []
        for i in range(n):
            for q in range(4):
                t = 2 * q + (1 - c)
                copies.append(pltpu.make_async_remote_copy(
                    src_ref=_shard_window(ins[i], axes[i], t, shapes[i]), dst_ref=outs[i].at[q],
                    send_sem=send_sems.at[i * 4 + q], recv_sem=recv_sems.at[i * 4 + q],
                    device_id=sibling, device_id_type=MESH))
        for cp in copies:
            cp.start()
        for cp in copies:
            cp.wait_recv()
        for cp in copies:
            cp.wait_send()

    return pl.kernel(
        body, out_type=[jax.ShapeDtypeStruct((4,) + tuple(sh), BF) for sh in shapes], name=name,
        mesh=plsc.ScalarSubcoreMesh(axis_name="sequencer", num_cores=1),
        scratch_types=[pltpu.SemaphoreType.DMA((4 * n,)), pltpu.SemaphoreType.DMA((4 * n,))],
        compiler_params=pltpu.CompilerParams(collective_id=2),
    )(*grads)


def pair_add(grads, landed, axes, shapes, c_idx, name, deps=()):
    n = len(grads)

    def body(c_ref, *refs):
        for t in range(n):
            refs[2 * n + t][...] = (refs[2 * t][...].astype(F32)
                                    + refs[2 * t + 1][...].astype(F32)).astype(BF)

    halves = 1
    in_specs, out_specs, out_shapes, operands = [], [], [], []
    for t in range(n):
        R, C = shapes[t]
        rh = R // halves
        if axes[t] == 0:
            in_specs.append(pl.BlockSpec(
                (rh, C), lambda q, h, c_ref: (halves * (2 * q + c_ref[0]) + h, 0)))
        else:
            in_specs.append(pl.BlockSpec((rh, C), lambda q, h, c_ref: (h, 2 * q + c_ref[0])))
        blk = pl.BlockSpec((None, rh, C), lambda q, h, c_ref: (q, h, 0))
        in_specs.append(blk)
        out_specs.append(blk)
        out_shapes.append(jax.ShapeDtypeStruct((4, R, C), BF))
        operands += [grads[t], landed[t]]
    return pl.pallas_call(
        _after(body, 1 + 2 * n, deps), name=name,
        grid_spec=pltpu.PrefetchScalarGridSpec(
            num_scalar_prefetch=1, grid=(4, halves), in_specs=in_specs + [ANY] * len(deps),
            out_specs=out_specs),
        out_shape=out_shapes,
        compiler_params=_cparams(("parallel", "parallel")),
    )(c_idx, *operands, *deps)


def chip_exchange_grads(parts, name):
    n = len(parts)

    def body(*refs):
        ins, outs, relay = refs[:n], refs[n:2 * n], refs[2 * n:3 * n]
        send_sems, recv_sems = refs[3 * n:]
        x, y, c = _me()
        xnb, ynb, diag = (1 - x, y), (x, 1 - y), (1 - x, 1 - y)
        south = c == 0
        via = (jnp.where(south, 1 - x, x), jnp.where(south, y, 1 - y))
        onward = (jnp.where(south, x, 1 - x), jnp.where(south, 1 - y, y))
        barrier = pltpu.get_barrier_semaphore()
        for peer in (xnb, ynb):
            pl.semaphore_signal(barrier, inc=1, device_id=(*peer, c), device_id_type=MESH)
        pl.semaphore_wait(barrier, 2)

        def copy(i, k, src, dst, to):
            return pltpu.make_async_remote_copy(
                src_ref=src, dst_ref=dst, send_sem=send_sems.at[i * 4 + k],
                recv_sem=recv_sems.at[i * 4 + k], device_id=(*to, c), device_id_type=MESH)

        sent = []
        for i in range(n):
            sent += [copy(i, 0, ins[i].at[2 * xnb[0] + xnb[1]], outs[i].at[0], xnb),
                     copy(i, 1, ins[i].at[2 * ynb[0] + ynb[1]], outs[i].at[1], ynb),
                     copy(i, 2, ins[i].at[2 * diag[0] + diag[1]], relay[i], via)]
        for cp in sent:
            cp.start()
        for i in range(n):
            copy(i, 2, relay[i], relay[i], via).wait_recv()
            sent.append(copy(i, 3, relay[i], outs[i].at[2], onward))
            sent[-1].start()
        for i in range(n):
            copy(i, 0, outs[i].at[0], outs[i].at[0], xnb).wait_recv()
            copy(i, 1, outs[i].at[1], outs[i].at[1], ynb).wait_recv()
            copy(i, 3, outs[i].at[2], outs[i].at[2], onward).wait_recv()
        for cp in sent:
            cp.wait_send()

    landing = [jax.ShapeDtypeStruct((3,) + tuple(p.shape[1:]), BF) for p in parts]
    staging = [jax.ShapeDtypeStruct(tuple(p.shape[1:]), BF) for p in parts]
    return pl.kernel(
        body, out_type=landing + staging, name=name,
        mesh=plsc.ScalarSubcoreMesh(axis_name="sequencer", num_cores=1),
        scratch_types=[pltpu.SemaphoreType.DMA((4 * n,)), pltpu.SemaphoreType.DMA((4 * n,))],
        compiler_params=pltpu.CompilerParams(collective_id=3),
    )(*parts)[:n]


def all_gather_small(v, name):
    R, C = v.shape

    def body(v_ref, out_ref, send_sems, recv_sems, local_sem):
        x, y, c = _me()
        me, sibling = (x, y, c), (x, y, 1 - c)
        chips = _other_chips(x, y)

        def slot(px, py, pc):
            return out_ref.at[4 * px + 2 * py + pc]

        def copy(k, block, to, src=None):
            return pltpu.make_async_remote_copy(
                src_ref=slot(*block) if src is None else src, dst_ref=slot(*block),
                send_sem=send_sems.at[k], recv_sem=recv_sems.at[k],
                device_id=to, device_id_type=MESH)

        mine = pltpu.make_async_copy(v_ref, slot(*me), local_sem)
        mine.start()
        first = [copy(0, me, sibling, src=v_ref)]
        first += [copy(1 + j, me, (*chip, c), src=v_ref) for j, chip in enumerate(chips)]
        for cp in first:
            cp.start()
        passed = [copy(4 + j, (*chip, c), sibling) for j, chip in enumerate(chips)]
        for j, chip in enumerate(chips):
            copy(1 + j, (*chip, c), me).wait_recv()
            passed[j].start()
        copy(0, sibling, me).wait_recv()
        for j, chip in enumerate(chips):
            copy(4 + j, (*chip, 1 - c), me).wait_recv()
        for cp in first + passed:
            cp.wait_send()
        mine.wait()

    return pl.pallas_call(
        body, name=name,
        in_specs=[pl.BlockSpec(memory_space=pltpu.VMEM)],
        out_specs=pl.BlockSpec(memory_space=pltpu.VMEM),
        out_shape=jax.ShapeDtypeStruct((NDEV, R, C), F32),
        scratch_shapes=[pltpu.SemaphoreType.DMA((7,)), pltpu.SemaphoreType.DMA((7,)),
                        pltpu.SemaphoreType.DMA],
    )(v)


def _adamw(w, g, m, v):
    m = ADAM_B1 * m + (1.0 - ADAM_B1) * g
    v = ADAM_B2 * v + (1.0 - ADAM_B2) * (g * g)
    m_hat = m / (1.0 - ADAM_B1 ** ADAM_STEP)
    v_hat = v / (1.0 - ADAM_B2 ** ADAM_STEP)
    delta = -ADAM_LR * (m_hat / (jnp.sqrt(v_hat) + ADAM_EPS) + ADAM_WD * w)
    return delta, m, v


def reduce_adamw(parts, landed, params, q_idx, name, prevs, deps=()):
    n = len(parts)
    halves = 2
    in_specs, out_specs, out_shapes, operands, extra, aliases = [], [], [], [], [], {}
    for t in range(n):
        R, C = parts[t].shape[1:]
        w, m, v, layer = params[t]
        r, c = w.shape[1:]
        tr = r // halves
        assert tr % 16 == 0 and c == C
        wspec = pl.BlockSpec((None, tr, c), lambda i, q_ref, layer=layer: (layer, i, 0))
        in_specs += [pl.BlockSpec((None, tr, C), lambda i, q_ref: (q_ref[0], i, 0)),
                     pl.BlockSpec((3, tr, C), lambda i, q_ref: (0, i, 0)), wspec, wspec, wspec]
        out_specs += [wspec] * 4
        out_shapes += [jax.ShapeDtypeStruct(w.shape, F32)] * 4
        operands += [parts[t], landed[t], w, m, v]
        for k, buf in enumerate(prevs[t]):
            aliases[1 + 5 * n + len(extra)] = 4 * t + k
            extra.append(buf)
    extra += list(deps)

    def body(q_ref, *refs):
        for t in range(n):
            p_ref, l_ref, w_ref, m_ref, v_ref = refs[5 * t:5 * t + 5]
            g = p_ref[...].astype(F32)
            for k in range(3):
                g = g + l_ref[k].astype(F32)
            d, mm, vv = _adamw(w_ref[...], g, m_ref[...], v_ref[...])
            outs = refs[5 * n + 4 * t:5 * n + 4 * t + 4]
            outs[0][...] = g
            outs[1][...] = d
            outs[2][...] = mm
            outs[3][...] = vv

    res = pl.pallas_call(
        _after(body, 1 + 5 * n, extra), name=name,
        grid_spec=pltpu.PrefetchScalarGridSpec(
            num_scalar_prefetch=1, grid=(halves,),
            in_specs=in_specs + [ANY] * len(extra), out_specs=out_specs),
        out_shape=out_shapes,
        input_output_aliases=aliases,
        compiler_params=_cparams(("parallel",)),
    )(q_idx, *operands, *extra)
    return [res[4 * t:4 * t + 4] for t in range(n)]


def small_reduce_adamw(gathered, w, m, v, name):
    R, C = w.shape

    def body(a_ref, w_ref, m_ref, v_ref, g_out, d_out, m_out, v_out):
        g = a_ref[0]
        for k in range(1, NDEV):
            g = g + a_ref[k]
        d, mm, vv = _adamw(w_ref[...], g, m_ref[...], v_ref[...])
        g_out[...] = g
        d_out[...] = d
        m_out[...] = mm
        v_out[...] = vv

    out = jax.ShapeDtypeStruct((R, C), F32)
    return pl.pallas_call(body, name=name, out_shape=[out] * 4,
                          compiler_params=_cparams())(gathered, w, m, v)


def _pad_rows(a, n):
    return jnp.pad(a, ((0, n - a.shape[0]), (0, 0)))


def _pack_small(mix, ffn, fin, taps_full, relb, scalar=0.0):
    n = NUM_BUCKETS * H
    last = jnp.concatenate([relb.reshape(1, n), jnp.reshape(scalar, (1, 1)).astype(F32),
                            jnp.zeros((1, D - n - 1), F32)], axis=1)
    return jnp.concatenate([mix, ffn, fin.reshape(1, D), taps_full.reshape(6, D), last], axis=0)


def kernel(x, mix_norm, ffn_norm, final_norm, conv_w_in, conv_kernel, conv_w_out, attn_w_qkv, attn_w_out, rel_bias, ffn_w_gate, ffn_w_up, ffn_w_down, loss_target, m_mix_norm, m_ffn_norm, m_final_norm, m_conv_w_in, m_conv_kernel, m_conv_w_out, m_attn_w_qkv, m_attn_w_out, m_rel_bias, m_ffn_w_gate, m_ffn_w_up, m_ffn_w_down, v_mix_norm, v_ffn_norm, v_final_norm, v_conv_w_in, v_conv_kernel, v_conv_w_out, v_attn_w_qkv, v_attn_w_out, v_rel_bias, v_ffn_w_gate, v_ffn_w_up, v_ffn_w_down):
    xi, yi, ci = _me()
    me = 4 * xi + 2 * yi + ci
    c_idx = jnp.reshape(ci, (1,)).astype(jnp.int32)
    q_idx = jnp.reshape(2 * xi + yi, (1,)).astype(jnp.int32)
    col0 = me * (D // NDEV)

    gate_t, up_t = jnp.swapaxes(ffn_w_gate, 1, 2), jnp.swapaxes(ffn_w_up, 1, 2)
    m_gate_t, m_up_t = jnp.swapaxes(m_ffn_w_gate, 1, 2), jnp.swapaxes(m_ffn_w_up, 1, 2)
    v_gate_t, v_up_t = jnp.swapaxes(v_ffn_w_gate, 1, 2), jnp.swapaxes(v_ffn_w_up, 1, 2)

    mixer_in = (conv_w_in, attn_w_qkv)
    mixer_out = (conv_w_out, attn_w_out)
    taps_shard = jnp.pad(conv_kernel.reshape(6, D // NDEV), ((0, 2), (0, 0)))
    wts = []
    for i in range(DEPTH):
        j = i // 2
        shards = [mixer_in[i % 2][j].astype(BF), mixer_out[i % 2][j].astype(BF),
                  _pad_rows(gate_t[i].astype(BF), FF_SHARD_PAD),
                  _pad_rows(up_t[i].astype(BF), FF_SHARD_PAD),
                  _pad_rows(ffn_w_down[i].astype(BF), FF_SHARD_PAD)]
        axes = (1, 0, 0, 0, 0)
        groups = ((0, 1), (1, 2), (2, 4), (4, 5)) if i == 0 else ((0, 2), (2, 5))
        layer = []
        for lo, hi in groups:
            extra = [taps_shard] if (i, lo) == (0, 1) else []
            got = list(all_gather_weights(shards[lo:hi] + extra, axes[lo:hi] + (1,) * len(extra),
                                          f"ag_l{i}_{lo}"))
            if extra:
                taps_all = got.pop()
            layer += got
        wts.append(layer)
    taps = [jnp.pad(taps_all[3 * j:3 * j + 3], ((0, 5), (0, 0))) for j in range(2)]

    onehot_t, band = _bucket_onehot_t()
    bias3 = bias_tables(rel_bias.T, onehot_t, band, "bias_tables").reshape(3, H * BLK, 2 * BLK)

    mix_gains, ffn_gains = mix_norm.reshape(DEPTH, 1, D), ffn_norm.reshape(DEPTH, 1, D)
    saved = []
    xc = x[0]
    for i in range(DEPTH):
        w_in, w_out, w_g, w_u, w_d = wts[i]
        j = i // 2
        x_mix = xc
        z3, h_mix = norm_matmul3(xc, (mix_gains, i), w_in, f"mix_in_l{i}")
        if i % 2 == 0:
            act = conv_fwd(z3, taps[j], f"conv_fwd_l{i}")
            lse_b = None
        else:
            act, lse_b = attention_fwd(z3, bias3, f"attn_fwd_l{i}")
        xc = matmul_residual(act, w_out, xc, f"mix_out_l{i}")
        x_ffn = xc
        g, u, a, h_ffn = norm_swiglu_up(xc, (ffn_gains, i), w_g, w_u, f"ffn_up_l{i}")
        xc = matmul_residual(a, w_d, xc, f"ffn_down_l{i}")
        saved.append((x_mix, h_mix, z3, act, lse_b, x_ffn, h_ffn, g, u, a))

    dx, dxb, dg_final, sq = loss_head(xc, final_norm.reshape(1, D), loss_target[0], "loss_head")
    loss_local = 0.5 * jnp.sum(sq[0]) / D

    dg_mix = [None] * DEPTH
    dg_ffn = [None] * DEPTH
    dtaps = [None, None]
    dbias_all = []
    shape_in, shape_out = (D, 3 * D // NDEV), (D // NDEV, D)
    ffn_axes, ffn_shapes = (0, 0, 0), ((FF_SHARD_PAD, D),) * 3
    stacked = {}

    def pair_stage(grads, landed1, axes, shapes, tag, tok):
        parts = pair_add(grads, landed1, axes, shapes, c_idx, f"rs_add_{tag}", deps=[tok])
        return parts, chip_exchange_grads(parts, f"rs_chip_{tag}"), parts[-1]

    def adamw_stage(parts, landed2, params, tag, tok):
        names = [p[0] for p in params]
        res = reduce_adamw(parts, landed2, [p[1:] for p in params], q_idx, f"adamw_{tag}",
                           [stacked.get(nm, ()) for nm in names], deps=[tok])
        for nm, r4 in zip(names, res):
            stacked[nm] = r4
        return res[-1][0]

    tok = dxb
    mix_wait = None
    mix_chip = None
    ffn_chip = []
    for i in reversed(range(DEPTH)):
        w_in, w_out, w_g, w_u, w_d = wts[i]
        j = i // 2
        x_mix, h_mix, z3, act, lse_b, x_ffn, h_ffn, g, u, a = saved[i]
        ffn_params = [("ffn_w_gate", gate_t, m_gate_t, v_gate_t, i),
                      ("ffn_w_up", up_t, m_up_t, v_up_t, i),
                      ("ffn_w_down", ffn_w_down, m_ffn_w_down, v_ffn_w_down, i)]
        if i % 2 == 0:
            mix_params = [("conv_w_in", conv_w_in, m_conv_w_in, v_conv_w_in, j),
                          ("conv_w_out", conv_w_out, m_conv_w_out, v_conv_w_out, j)]
        else:
            mix_params = [("attn_w_qkv", attn_w_qkv, m_attn_w_qkv, v_attn_w_qkv, j),
                          ("attn_w_out", attn_w_out, m_attn_w_out, v_attn_w_out, j)]
        dgate, dup = swiglu_bwd_da(dxb, w_d, g, u, f"ffn_da_l{i}", deps=[tok])
        tok = dgate
        for group in ffn_chip:
            tok = adamw_stage(*group, tok)
        ffn_chip = []
        if mix_wait is not None:
            grads_m, landed1_m, params_m, tag_m = mix_wait
            parts_m, landed2_m, tok = pair_stage(grads_m, landed1_m, (1, 0), (shape_in, shape_out),
                                                 tag_m, tok)
            mix_chip = (parts_m, landed2_m, params_m, tag_m)
            mix_wait = None
        products = [(dgate, h_ffn), (dup, h_ffn), (a, dxb)]
        ffn_wait = []
        for gi, idxs in enumerate(((0,), (1, 2)) if i == 0 else ((0, 1, 2),)):
            grads_f = matmul_tn_group([products[k] for k in idxs], f"ffn_dw_l{i}_{gi}", deps=[tok])
            landed1_f = pair_exchange_grads(grads_f, ffn_axes[:len(idxs)], ffn_shapes[:len(idxs)],
                                            f"rs_pair_f{i}_{gi}")
            ffn_wait.append((grads_f, landed1_f, [ffn_params[k] for k in idxs], f"f{i}_{gi}"))
            tok = grads_f[-1]

        def ffn_pair_stages(groups, tok):
            for grads_f, landed1_f, params_f, tag_f in groups:
                nf = len(grads_f)
                parts_f, landed2_f, tok = pair_stage(grads_f, landed1_f, ffn_axes[:nf],
                                                     ffn_shapes[:nf], tag_f, tok)
                ffn_chip.append((parts_f, landed2_f, params_f, tag_f))
            return tok

        if len(ffn_wait) > 1:
            if mix_chip is not None:
                tok = adamw_stage(*mix_chip, tok)
                mix_chip = None
            tok = ffn_pair_stages(ffn_wait[:-1], tok)
        dx, dxb, dg_ffn[i] = matmul_normbwd(
            [(dgate, w_g, False), (dup, w_u, False)], x_ffn, (ffn_gains, i), dx, f"ffn_dh_l{i}",
            deps=[tok])
        dxb_mix = dxb
        tok = dxb
        if mix_chip is not None:
            tok = adamw_stage(*mix_chip, tok)
            mix_chip = None
        tok = ffn_pair_stages(ffn_wait[-1:], tok)
        dact = matmul_nt(dxb, w_out, f"mix_dact_l{i}", out_dtype=F32 if i % 2 == 0 else BF,
                         deps=[tok])
        if i % 2 == 0:
            dz3, dtaps[j] = conv_bwd(dact, z3, taps[j], f"conv_bwd_l{i}")
        else:
            dz3, dsum = attention_bwd(z3, dact, act, lse_b, bias3, f"attn_bwd_l{i}")
            dbias_all.append(dsum[:, :6].reshape(H // 2, 3, 2, 2 * BLK).transpose(1, 0, 2, 3)
                             .reshape(3, H, 2 * BLK))
        grads_m = matmul_tn_group([(h_mix, dz3), (act, dxb_mix)], f"mix_dw_l{i}")
        landed1_m = pair_exchange_grads(grads_m, (1, 0), (shape_in, shape_out), f"rs_pair_m{i}")
        mix_wait = (grads_m, landed1_m, mix_params, f"m{i}")
        dx, dxb, dg_mix[i] = matmul_normbwd(
            [(dz3, w_in, True)], x_mix, (mix_gains, i), dx, f"mix_dh_l{i}", deps=[grads_m[-1]])
        tok = dxb
    for group in ffn_chip:
        tok = adamw_stage(*group, tok)
    grads_m, landed1_m, params_m, tag_m = mix_wait
    parts_m, landed2_m, tok = pair_stage(grads_m, landed1_m, (1, 0), (shape_in, shape_out), tag_m, tok)

    grad_relb_t = bias_grad(jnp.concatenate(dbias_all), _diagonal_onehot_t(), "bias_grad")
    dtaps_full = jnp.stack([dtaps[0][:3], dtaps[1][:3]])
    g_small = _pack_small(jnp.concatenate([d[0:1] for d in dg_mix], axis=0),
                          jnp.concatenate([d[0:1] for d in dg_ffn], axis=0),
                          dg_final[0], dtaps_full, grad_relb_t.T, loss_local)
    gathered = all_gather_small(g_small, "ag_small_grads")

    def taps_at_cols(k):
        return lax.dynamic_update_slice(jnp.zeros((2, 3, D), F32), k, (0, 0, col0))

    w_small = _pack_small(mix_norm, ffn_norm, final_norm, taps_at_cols(conv_kernel), rel_bias)
    m_small = _pack_small(m_mix_norm, m_ffn_norm, m_final_norm, taps_at_cols(m_conv_kernel), m_rel_bias)
    v_small = _pack_small(v_mix_norm, v_ffn_norm, v_final_norm, taps_at_cols(v_conv_kernel), v_rel_bias)
    small = small_reduce_adamw(gathered, w_small, m_small, v_small, "adamw_small")

    def unpack_small(p):
        taps_p = lax.dynamic_slice(p[9:15].reshape(2, 3, D), (0, 0, col0), (2, 3, D // NDEV))
        return {"mix_norm": p[0:4], "ffn_norm": p[4:8], "final_norm": p[8],
                "conv_kernel": taps_p, "rel_bias": p[15, :NUM_BUCKETS * H].reshape(NUM_BUCKETS, H)}

    small_out = [unpack_small(p) for p in small]
    loss = small[0][15, NUM_BUCKETS * H]
    adamw_stage(parts_m, landed2_m, params_m, tag_m, small[0])

    names = ["mix_norm", "ffn_norm", "final_norm", "conv_w_in", "conv_kernel", "conv_w_out",
             "attn_w_qkv", "attn_w_out", "rel_bias", "ffn_w_gate", "ffn_w_up", "ffn_w_down"]
    outs = [loss, dx.reshape(1, S, D)]
    for o in range(4):
        for nme in names:
            if nme in ("ffn_w_gate", "ffn_w_up"):
                outs.append(jnp.swapaxes(stacked[nme][o], 1, 2))
            else:
                outs.append(stacked[nme][o] if nme in stacked else small_out[o][nme])
    return tuple(outs)
```

```python
import math

import numpy as np
import jax
import jax.numpy as jnp
from jax import lax
from jax.experimental import pallas as pl
from jax.experimental.pallas import tpu as pltpu
from jax.experimental.pallas import tpu_sc as plsc

S = 2048
D = 1024
H = 16
DH = 64
DFF = 2816
NDEV = 8
DEPTH = 4
FF_SHARD = DFF // NDEV
FF_SHARD_PAD = 384
DFF_PAD = FF_SHARD_PAD * NDEV
BLK = 128
BRANCH_DILATIONS = (1, 4, 16)
NUM_BUCKETS = 32
MAX_DISTANCE = 2048
EPS = 1e-6
NEG_INF = -1e30
SCALE = DH ** -0.5

ADAM_LR = 0.001
ADAM_B1 = 0.9
ADAM_B2 = 0.999
ADAM_EPS = 1e-08
ADAM_WD = 0.01
ADAM_STEP = 10

BF = jnp.bfloat16
F32 = jnp.float32
VMEM_LIMIT_BYTES = 56 * 1024 * 1024
KSPLIT = 512
NORM_CHUNK = 256
MESH = pl.DeviceIdType.MESH
ANY = pl.BlockSpec(memory_space=pl.ANY)

_NT = (((1,), (1,)), ((), ()))
_TN = (((0,), (0,)), ((), ()))


def _cparams(sem=None):
    return pltpu.CompilerParams(dimension_semantics=sem, vmem_limit_bytes=VMEM_LIMIT_BYTES)


def _after(body, n, deps):
    nd = len(deps)
    if nd == 0:
        return body

    def ordered(*refs):
        body(*refs[:n], *refs[n + nd:])
    return ordered


def _rms(x):
    return lax.rsqrt(jnp.mean(x * x, axis=-1, keepdims=True) + EPS)


def norm_matmul3(x, gain, w, name, tm=1024, tn=1024):
    per = D // tn

    def body(x_ref, g_ref, w_ref, z_ref, h_ref, hs_ref):
        @pl.when(pl.program_id(1) == 0)
        def _():
            for c in range(tm // NORM_CHUNK):
                rows = slice(c * NORM_CHUNK, (c + 1) * NORM_CHUNK)
                xv = x_ref[rows, :]
                hv = (xv * _rms(xv) * g_ref[...]).astype(BF)
                hs_ref[rows, :] = hv
                h_ref[rows, :] = hv
                z_ref[rows, :] = jnp.dot(hv, w_ref[...], preferred_element_type=F32).astype(BF)

        @pl.when(pl.program_id(1) > 0)
        def _():
            z_ref[...] = jnp.dot(hs_ref[...], w_ref[...], preferred_element_type=F32).astype(BF)

    return pl.pallas_call(
        body, name=name,
        grid=(S // tm, 3 * D // tn),
        in_specs=[pl.BlockSpec((tm, D), lambda i, j: (i, 0)),
                  pl.BlockSpec((None, 1, D), lambda i, j: (gain[1], 0, 0)),
                  pl.BlockSpec((D, tn), lambda i, j: (0, j))],
        out_specs=[pl.BlockSpec((None, tm, tn), lambda i, j: (j // per, i, j % per)),
                   pl.BlockSpec((tm, D), lambda i, j: (i, 0))],
        out_shape=[jax.ShapeDtypeStruct((3, S, D), BF), jax.ShapeDtypeStruct((S, D), BF)],
        scratch_shapes=[pltpu.VMEM((tm, D), BF)],
        compiler_params=_cparams(("parallel", "arbitrary")),
    )(x, gain[0], w)


def norm_swiglu_up(x, gain, wg_t, wu_t, name, tm=1024, tn=768):
    def body(x_ref, g_ref, wg_ref, wu_ref, go_ref, uo_ref, ao_ref, h_ref, hs_ref):
        def gate_up(hv, rows):
            g = lax.dot_general(hv, wg_ref[...], _NT, preferred_element_type=F32)
            u = lax.dot_general(hv, wu_ref[...], _NT, preferred_element_type=F32)
            go_ref[rows, :] = g.astype(BF)
            uo_ref[rows, :] = u.astype(BF)
            ao_ref[rows, :] = (g * jax.nn.sigmoid(g) * u).astype(BF)

        @pl.when(pl.program_id(1) == 0)
        def _():
            for c in range(tm // NORM_CHUNK):
                rows = slice(c * NORM_CHUNK, (c + 1) * NORM_CHUNK)
                xv = x_ref[rows, :]
                hv = (xv * _rms(xv) * g_ref[...]).astype(BF)
                hs_ref[rows, :] = hv
                h_ref[rows, :] = hv
                gate_up(hv, rows)

        @pl.when(pl.program_id(1) > 0)
        def _():
            gate_up(hs_ref[...], slice(None))

    act = jax.ShapeDtypeStruct((S, DFF_PAD), BF)
    blk = pl.BlockSpec((tm, tn), lambda i, j: (i, j))
    return pl.pallas_call(
        body, name=name,
        grid=(S // tm, DFF_PAD // tn),
        in_specs=[pl.BlockSpec((tm, D), lambda i, j: (i, 0)),
                  pl.BlockSpec((None, 1, D), lambda i, j: (gain[1], 0, 0)),
                  pl.BlockSpec((tn, D), lambda i, j: (j, 0)),
                  pl.BlockSpec((tn, D), lambda i, j: (j, 0))],
        out_specs=[blk, blk, blk, pl.BlockSpec((tm, D), lambda i, j: (i, 0))],
        out_shape=[act, act, act, jax.ShapeDtypeStruct((S, D), BF)],
        scratch_shapes=[pltpu.VMEM((tm, D), BF)],
        compiler_params=_cparams(("parallel", "arbitrary")),
    )(x, gain[0], wg_t, wu_t)


def matmul_residual(a, w, x, name, tm=1024):
    K = a.shape[1]
    tn = D if K <= D else D // 2
    ns = K // KSPLIT
    kc = K // ns

    def body(*refs):
        x_ref, o_ref = refs[2 * ns:]
        acc = x_ref[...]
        for s in range(ns):
            acc = acc + jnp.dot(refs[s][...], refs[ns + s][...], preferred_element_type=F32)
        o_ref[...] = acc

    return pl.pallas_call(
        body, name=name,
        grid=(S // tm, D // tn),
        in_specs=[pl.BlockSpec((tm, kc), lambda i, j, s=s: (i, s)) for s in range(ns)]
        + [pl.BlockSpec((kc, tn), lambda i, j, s=s: (s, j)) for s in range(ns)]
        + [pl.BlockSpec((tm, tn), lambda i, j: (i, j))],
        out_specs=pl.BlockSpec((tm, tn), lambda i, j: (i, j)),
        out_shape=jax.ShapeDtypeStruct((S, D), F32),
        compiler_params=_cparams(("parallel", "parallel")),
    )(*([a] * ns), *([w] * ns), x)


def matmul_nt(a, w, name, out_dtype=BF, tm=1024, tn=1024, deps=()):
    K = a.shape[1]
    N = w.shape[0]

    def body(a_ref, w_ref, o_ref):
        o_ref[...] = lax.dot_general(a_ref[...], w_ref[...], _NT,
                                     preferred_element_type=F32).astype(o_ref.dtype)

    return pl.pallas_call(
        _after(body, 2, deps), name=name,
        grid=(S // tm, N // tn),
        in_specs=[pl.BlockSpec((tm, K), lambda i, j: (i, 0)),
                  pl.BlockSpec((tn, K), lambda i, j: (j, 0))] + [ANY] * len(deps),
        out_specs=pl.BlockSpec((tm, tn), lambda i, j: (i, j)),
        out_shape=jax.ShapeDtypeStruct((S, N), out_dtype),
        compiler_params=_cparams(("parallel", "parallel")),
    )(a, w, *deps)


def swiglu_bwd_da(dxb, wd, g, u, name, tm=1024, tn=768, deps=()):
    def body(dx_ref, w_ref, g_ref, u_ref, dg_ref, du_ref):
        da = lax.dot_general(dx_ref[...], w_ref[...], _NT, preferred_element_type=F32)
        gv = g_ref[...].astype(F32)
        uv = u_ref[...].astype(F32)
        sig = jax.nn.sigmoid(gv)
        dg_ref[...] = (da * uv * (sig * (1.0 + gv * (1.0 - sig)))).astype(BF)
        du_ref[...] = (da * (gv * sig)).astype(BF)

    act = jax.ShapeDtypeStruct((S, DFF_PAD), BF)
    blk = pl.BlockSpec((tm, tn), lambda i, j: (i, j))
    return pl.pallas_call(
        _after(body, 4, deps), name=name,
        grid=(S // tm, DFF_PAD // tn),
        in_specs=[pl.BlockSpec((tm, D), lambda i, j: (i, 0)),
                  pl.BlockSpec((tn, D), lambda i, j: (j, 0)),
                  blk, blk] + [ANY] * len(deps),
        out_specs=[blk, blk],
        out_shape=[act, act],
        compiler_params=_cparams(("parallel", "parallel")),
    )(dxb, wd, g, u, *deps)


def matmul_tn_group(pairs, name, tm=1024, deps=()):
    P = len(pairs)
    tn = D if P <= 2 else D // 2
    steps = []
    for p, (a, b) in enumerate(pairs):
        N = 3 * D if b.ndim == 3 else b.shape[1]
        steps += [(p, i, j) for i in range(a.shape[1] // tm) for j in range(N // tn)]
    T = len(steps)
    tab = np.zeros((T, 1 + 2 * P), np.int32)
    for p in range(P):
        cur = (0, 0)
        for s, (ph, i, j) in enumerate(steps):
            if ph == p:
                cur = (i, j)
            tab[s, 1 + 2 * p:3 + 2 * p] = cur
    tab[:, 0] = [ph for ph, _, _ in steps]

    in_specs, out_specs, out_shapes, operands = [], [], [], []
    per = D // tn
    for p, (a, b) in enumerate(pairs):
        ci, cj = 1 + 2 * p, 2 + 2 * p
        in_specs.append(pl.BlockSpec((S, tm), lambda s, t, ci=ci: (0, t[s, ci])))
        if b.ndim == 3:
            in_specs.append(pl.BlockSpec((None, S, tn),
                                         lambda s, t, cj=cj: (t[s, cj] // per, 0, t[s, cj] % per)))
            N = 3 * D
        else:
            in_specs.append(pl.BlockSpec((S, tn), lambda s, t, cj=cj: (0, t[s, cj])))
            N = b.shape[1]
        out_specs.append(pl.BlockSpec((tm, tn), lambda s, t, ci=ci, cj=cj: (t[s, ci], t[s, cj])))
        out_shapes.append(jax.ShapeDtypeStruct((a.shape[1], N), BF))
        operands += [a, b]

    def body(tab_ref, *refs):
        phase = tab_ref[pl.program_id(0), 0]
        for p in range(P):
            @pl.when(phase == p)
            def _(p=p):
                refs[2 * P + p][...] = lax.dot_general(
                    refs[2 * p][...], refs[2 * p + 1][...], _TN,
                    preferred_element_type=F32).astype(BF)

    return pl.pallas_call(
        _after(body, 1 + 2 * P, deps), name=name,
        grid_spec=pltpu.PrefetchScalarGridSpec(
            num_scalar_prefetch=1, grid=(T,), in_specs=in_specs + [ANY] * len(deps),
            out_specs=out_specs),
        out_shape=out_shapes,
        compiler_params=_cparams(("arbitrary",)),
    )(jnp.asarray(tab), *operands, *deps)


def matmul_normbwd(terms, x_in, gain, dx, name, tm=512, ch=256, deps=()):
    specs, operands = [], []
    for (a, w, stacked) in terms:
        if stacked:
            specs.append(pl.BlockSpec((3, tm, D), lambda i: (0, i, 0)))
        else:
            specs.append(pl.BlockSpec((tm, a.shape[1]), lambda i: (i, 0)))
        specs.append(pl.BlockSpec(w.shape, lambda i: (0, 0), pipeline_mode=pl.Buffered(1)))
        operands += [a, w]
    nt = len(terms)

    def body(*refs):
        aw = refs[:2 * nt]
        x_ref, g_ref, dx_ref, dxo_ref, dxb_ref, dg_ref = refs[2 * nt:]

        @pl.when(pl.program_id(0) == 0)
        def _():
            dg_ref[...] = jnp.zeros_like(dg_ref)

        dgain = None
        for c in range(tm // ch):
            rows = slice(c * ch, (c + 1) * ch)
            dh = None
            for t, (_, _, stacked) in enumerate(terms):
                a_ref, w_ref = aw[2 * t], aw[2 * t + 1]
                if stacked:
                    parts = [lax.dot_general(a_ref[k, rows, :], w_ref[:, k * D:(k + 1) * D], _NT,
                                             preferred_element_type=F32) for k in range(3)]
                else:
                    parts = [jnp.dot(a_ref[rows, :], w_ref[...], preferred_element_type=F32)]
                for p in parts:
                    dh = p if dh is None else dh + p
            xv = x_ref[rows, :]
            r = _rms(xv)
            xhat = xv * r
            part = jnp.sum(dh * xhat, axis=0, keepdims=True)
            dgain = part if dgain is None else dgain + part
            dxh = dh * g_ref[...]
            dxn = r * (dxh - xhat * jnp.mean(dxh * xhat, axis=-1, keepdims=True))
            out = dx_ref[rows, :] + dxn
            dxo_ref[rows, :] = out
            dxb_ref[rows, :] = out.astype(BF)
        dg_ref[0:1, :] += dgain

    row = pl.BlockSpec((tm, D), lambda i: (i, 0))
    return pl.pallas_call(
        _after(body, 2 * nt + 3, deps), name=name,
        grid=(S // tm,),
        in_specs=specs + [row, pl.BlockSpec((None, 1, D), lambda i: (gain[1], 0, 0)), row]
        + [ANY] * len(deps),
        out_specs=[row, row, pl.BlockSpec((8, D), lambda i: (0, 0))],
        out_shape=[jax.ShapeDtypeStruct((S, D), F32), jax.ShapeDtypeStruct((S, D), BF),
                   jax.ShapeDtypeStruct((8, D), F32)],
        compiler_params=_cparams(("arbitrary",)),
    )(*operands, x_in, gain[0], dx, *deps)


def loss_head(x, gain, target, name, tm=512):
    def body(x_ref, g_ref, t_ref, dxo_ref, dxb_ref, dg_ref, sq_ref):
        @pl.when(pl.program_id(0) == 0)
        def _():
            dg_ref[...] = jnp.zeros_like(dg_ref)
            sq_ref[...] = jnp.zeros_like(sq_ref)
        xv = x_ref[...]
        r = _rms(xv)
        xhat = xv * r
        err = xhat * g_ref[...] - t_ref[...]
        sq_ref[0:1, :] += jnp.sum(err * err, axis=0, keepdims=True)
        dy = err * (1.0 / D)
        dg_ref[0:1, :] += jnp.sum(dy * xhat, axis=0, keepdims=True)
        dxh = dy * g_ref[...]
        out = r * (dxh - xhat * jnp.mean(dxh * xhat, axis=-1, keepdims=True))
        dxo_ref[...] = out
        dxb_ref[...] = out.astype(BF)

    row = pl.BlockSpec((tm, D), lambda i: (i, 0))
    acc = pl.BlockSpec((8, D), lambda i: (0, 0))
    return pl.pallas_call(
        body, name=name,
        grid=(S // tm,),
        in_specs=[row, pl.BlockSpec((1, D), lambda i: (0, 0)), row],
        out_specs=[row, row, acc, acc],
        out_shape=[jax.ShapeDtypeStruct((S, D), F32), jax.ShapeDtypeStruct((S, D), BF),
                   jax.ShapeDtypeStruct((8, D), F32), jax.ShapeDtypeStruct((8, D), F32)],
        compiler_params=_cparams(("arbitrary",)),
    )(x, gain, target)


CONV_TM = 256
HALO = 16


def _halo_row(halo, r):
    hrow = lax.broadcasted_iota(jnp.int32, halo.shape, 0)
    return jnp.sum(jnp.where(hrow == r, halo, 0.0), axis=0, keepdims=True)


def _prev_rows(p, halo_p, n, row):
    out = pltpu.roll(p, n, axis=0)
    for k in range(n):
        out = jnp.where(row == k, _halo_row(halo_p, HALO - n + k), out)
    return out


def _next_rows(p, halo_p, n, row):
    tm = p.shape[0]
    out = pltpu.roll(p, tm - n, axis=0)
    for k in range(n):
        out = jnp.where(row == tm - n + k, _halo_row(halo_p, k), out)
    return out


def _conv_specs():
    per = CONV_TM // HALO
    main = pl.BlockSpec((3, CONV_TM, D), lambda i: (0, i, 0))
    prev = pl.BlockSpec((3, HALO, D), lambda i: (0, jnp.maximum(i * per - 1, 0), 0))
    nxt = pl.BlockSpec((3, HALO, D), lambda i: (0, jnp.minimum((i + 1) * per, S // HALO - 1), 0))
    return main, prev, nxt


def conv_fwd(z3, taps, name):
    def body(z_ref, zp_ref, k_ref, m_ref):
        i = pl.program_id(0)
        p = z_ref[1].astype(F32) * z_ref[2].astype(F32)
        halo = zp_ref[1].astype(F32) * zp_ref[2].astype(F32) * jnp.where(i > 0, 1.0, 0.0)
        row = lax.broadcasted_iota(jnp.int32, p.shape, 0)
        y = (k_ref[2:3, :] * p + k_ref[1:2, :] * _prev_rows(p, halo, 1, row)
             + k_ref[0:1, :] * _prev_rows(p, halo, 2, row))
        m_ref[...] = (z_ref[0].astype(F32) * y).astype(BF)

    main, prev, _ = _conv_specs()
    return pl.pallas_call(
        body, name=name,
        grid=(S // CONV_TM,),
        in_specs=[main, prev, pl.BlockSpec((8, D), lambda i: (0, 0))],
        out_specs=pl.BlockSpec((CONV_TM, D), lambda i: (i, 0)),
        out_shape=jax.ShapeDtypeStruct((S, D), BF),
        compiler_params=_cparams(("parallel",)),
    )(z3, z3, taps)


def conv_bwd(dm, z3, taps, name, deps=()):
    per = CONV_TM // HALO

    def body(dm_ref, dmn_ref, z_ref, zp_ref, zn_ref, k_ref, dz_ref, dk_ref):
        i = pl.program_id(0)

        @pl.when(i == 0)
        def _():
            dk_ref[...] = jnp.zeros_like(dk_ref)

        dmv = dm_ref[...]
        b = z_ref[0].astype(F32)
        c = z_ref[1].astype(F32)
        u = z_ref[2].astype(F32)
        p = c * u
        halo_p = zp_ref[1].astype(F32) * zp_ref[2].astype(F32) * jnp.where(i > 0, 1.0, 0.0)
        halo_dy = (dmn_ref[...] * zn_ref[0].astype(F32)
                   * jnp.where(i < S // CONV_TM - 1, 1.0, 0.0))
        row = lax.broadcasted_iota(jnp.int32, p.shape, 0)
        p1 = _prev_rows(p, halo_p, 1, row)
        p2 = _prev_rows(p, halo_p, 2, row)
        y = k_ref[2:3, :] * p + k_ref[1:2, :] * p1 + k_ref[0:1, :] * p2
        dy = dmv * b
        dz_ref[0] = (dmv * y).astype(BF)
        dp = (k_ref[2:3, :] * dy + k_ref[1:2, :] * _next_rows(dy, halo_dy, 1, row)
              + k_ref[0:1, :] * _next_rows(dy, halo_dy, 2, row))
        dz_ref[1] = (dp * u).astype(BF)
        dz_ref[2] = (dp * c).astype(BF)
        dk_ref[0:1, :] += jnp.sum(dy * p2, axis=0, keepdims=True)
        dk_ref[1:2, :] += jnp.sum(dy * p1, axis=0, keepdims=True)
        dk_ref[2:3, :] += jnp.sum(dy * p, axis=0, keepdims=True)

    main, prev, nxt = _conv_specs()
    return pl.pallas_call(
        _after(body, 6, deps), name=name,
        grid=(S // CONV_TM,),
        in_specs=[pl.BlockSpec((CONV_TM, D), lambda i: (i, 0)),
                  pl.BlockSpec((HALO, D), lambda i: (jnp.minimum((i + 1) * per, S // HALO - 1), 0)),
                  main, prev, nxt, pl.BlockSpec((8, D), lambda i: (0, 0))] + [ANY] * len(deps),
        out_specs=[main, pl.BlockSpec((8, D), lambda i: (0, 0))],
        out_shape=[jax.ShapeDtypeStruct((3, S, D), BF), jax.ShapeDtypeStruct((8, D), F32)],
        compiler_params=_cparams(("arbitrary",)),
    )(dm, dm, z3, z3, z3, taps, *deps)


def _t5_bucket(dist):
    exact = NUM_BUCKETS // 2
    df = jnp.maximum(dist, 1).astype(jnp.float32)
    large = exact + (jnp.log(df / exact) / math.log(MAX_DISTANCE / exact)
                     * (NUM_BUCKETS - exact)).astype(jnp.int32)
    large = jnp.minimum(large, NUM_BUCKETS - 1)
    return jnp.where(dist < exact, dist, large)


def _bucket_onehot_t():
    qi = jnp.arange(BLK)[:, None]
    ki = jnp.arange(2 * BLK)[None, :]
    rel = qi + BLK - ki
    band = ((rel >= 0) & (rel <= BLK)).reshape(1, -1).astype(F32)
    hots = []
    for d in BRANCH_DILATIONS:
        bucket = _t5_bucket(jnp.clip(rel, 0) * d).reshape(1, -1)
        hots.append((jnp.arange(NUM_BUCKETS)[:, None] == bucket).astype(F32))
    return jnp.stack(hots), band


def bias_tables(rel_bias_t, onehot_t, band, name):
    def body(rb_ref, oh_ref, band_ref, o_ref):
        b = jnp.dot(rb_ref[...], oh_ref[...], preferred_element_type=F32,
                    precision=lax.Precision.HIGHEST)
        o_ref[...] = jnp.where(band_ref[...] > 0.5, b, NEG_INF)

    n = BLK * 2 * BLK
    return pl.pallas_call(
        body, name=name,
        grid=(3,),
        in_specs=[pl.BlockSpec((H, NUM_BUCKETS), lambda g: (0, 0)),
                  pl.BlockSpec((None, NUM_BUCKETS, n), lambda g: (g, 0, 0)),
                  pl.BlockSpec((1, n), lambda g: (0, 0))],
        out_specs=pl.BlockSpec((None, H, n), lambda g: (g, 0, 0)),
        out_shape=jax.ShapeDtypeStruct((3, H, n), F32),
        compiler_params=_cparams(("parallel",)),
    )(rel_bias_t, onehot_t, band)


def _diagonal_onehot_t():
    c = jnp.arange(BLK)
    dist = jnp.concatenate([c + 1, (c + 1) % BLK])[None, :]
    hots = [(jnp.arange(NUM_BUCKETS)[:, None] == _t5_bucket(dist * d)).astype(F32)
            for d in BRANCH_DILATIONS]
    return jnp.stack(hots)


def bias_grad(dsums, onehot_t, name):
    def body(ds_ref, oh_ref, o_ref):
        @pl.when(pl.program_id(0) == 0)
        def _():
            o_ref[...] = jnp.zeros_like(o_ref)
        o_ref[...] += lax.dot_general(ds_ref[...], oh_ref[...], _NT, preferred_element_type=F32,
                                      precision=lax.Precision.HIGHEST)

    return pl.pallas_call(
        body, name=name,
        grid=(dsums.shape[0],),
        in_specs=[pl.BlockSpec((None, H, 2 * BLK), lambda g: (g, 0, 0)),
                  pl.BlockSpec((None, NUM_BUCKETS, 2 * BLK), lambda g: (g % 3, 0, 0))],
        out_specs=pl.BlockSpec((H, NUM_BUCKETS), lambda g: (0, 0)),
        out_shape=jax.ShapeDtypeStruct((H, NUM_BUCKETS), F32),
        compiler_params=_cparams(("arbitrary",)),
    )(dsums, onehot_t)


def _head_masks():
    lane = lax.broadcasted_iota(jnp.int32, (1, 2 * DH), 1)
    return (lane < DH, lane >= DH)


def _stack_heads(x, masks):
    zero = jnp.zeros_like(x)
    return jnp.concatenate([jnp.where(masks[0], x, zero), jnp.where(masks[1], x, zero)], axis=0)


def _deinterleave(src_ref, dst_ref, d, dtype):
    dst_ref[...] = pltpu.einshape("(md)c->(dm)c", src_ref[...], d=d).astype(dtype)


def _branch_loops(d, block):
    L = S // d
    for r in range(d):
        base = r * L
        block(base, base, BLK, True)
        for n in range(1, L // BLK):
            block(base + n * BLK, base + (n - 1) * BLK, 2 * BLK, False)


def attention_fwd(z3, bias3, name):
    W = 2 * DH
    CH = 256

    def body(q_ref, k_ref, v_ref, b_ref, o_ref, lse_ref, stage, qd, kd, vd, od, ld, on, ln):
        masks = _head_masks()
        for src, dst in ((q_ref, qd), (k_ref, kd), (v_ref, vd)):
            stage[...] = src[...].astype(F32)
            for gi, d in enumerate(BRANCH_DILATIONS[1:]):
                _deinterleave(stage, dst.at[gi], d, BF)

        for g, d in enumerate(BRANCH_DILATIONS):
            qs, ks, vs = (q_ref, k_ref, v_ref) if d == 1 else (qd.at[g - 1], kd.at[g - 1], vd.at[g - 1])
            o_dst, l_dst = (on.at[0], ln.at[0]) if d == 1 else (od, ld)

            def block(q0, k0, nk, first, g=g, qs=qs, ks=ks, vs=vs, o_dst=o_dst, l_dst=l_dst):
                q2 = _stack_heads(qs[pl.ds(q0, BLK), :], masks)
                kk = ks[pl.ds(k0, nk), :]
                vv = vs[pl.ds(k0, nk), :]
                bias = b_ref[g][:, BLK:] if first else b_ref[g]
                s = lax.dot_general(q2, kk, _NT, preferred_element_type=F32) * SCALE + bias
                mx = jnp.max(s, axis=1, keepdims=True)
                p = jnp.exp(s - mx)
                l = jnp.sum(p, axis=1, keepdims=True)
                o2 = jnp.dot(p.astype(BF), vv, preferred_element_type=F32) / l
                lse2 = mx + jnp.log(l)
                o_dst[pl.ds(q0, BLK), :] = jnp.where(masks[0], o2[:BLK], o2[BLK:])
                l_dst[pl.ds(q0, BLK), :] = jnp.where(masks[0], lse2[:BLK], lse2[BLK:])

            _branch_loops(d, block)
            if d > 1:
                L = S // d
                for r in range(d):
                    on[g, pl.ds(r, L, stride=d), :] = od[r * L:(r + 1) * L, :]
                    ln[g, pl.ds(r, L, stride=d), :] = ld[r * L:(r + 1) * L, :]

        def join(c, carry):
            rows = pl.ds(pl.multiple_of(c * CH, CH), CH)
            a, b, cc = ln[0, rows, :], ln[1, rows, :], ln[2, rows, :]
            mx = jnp.maximum(jnp.maximum(a, b), cc)
            ea, eb, ec = jnp.exp(a - mx), jnp.exp(b - mx), jnp.exp(cc - mx)
            tot = ea + eb + ec
            o_ref[rows, :] = ((ea * on[0, rows, :] + eb * on[1, rows, :] + ec * on[2, rows, :])
                              / tot).astype(BF)
            lse_ref[rows, :] = mx + jnp.log(tot)
            return carry
        lax.fori_loop(0, S // CH, join, 0)

    col = pl.BlockSpec((S, W), lambda hp: (0, hp))
    return pl.pallas_call(
        body, name=name,
        grid=(D // W,),
        in_specs=[pl.BlockSpec((None, S, W), lambda hp: (0, 0, hp)),
                  pl.BlockSpec((None, S, W), lambda hp: (1, 0, hp)),
                  pl.BlockSpec((None, S, W), lambda hp: (2, 0, hp)),
                  pl.BlockSpec((3, 2 * BLK, 2 * BLK), lambda hp: (0, hp, 0))],
        out_specs=[col, col],
        out_shape=[jax.ShapeDtypeStruct((S, D), BF), jax.ShapeDtypeStruct((S, D), F32)],
        scratch_shapes=[pltpu.VMEM((S, W), F32),
                        pltpu.VMEM((2, S, W), BF), pltpu.VMEM((2, S, W), BF), pltpu.VMEM((2, S, W), BF),
                        pltpu.VMEM((S, W), F32), pltpu.VMEM((S, W), F32),
                        pltpu.VMEM((3, S, W), F32), pltpu.VMEM((3, S, W), F32)],
        compiler_params=_cparams(("parallel",)),
    )(z3, z3, z3, bias3)


def attention_bwd(z3, dob, ob, lse_b, bias3, name, deps=()):
    W = 2 * DH
    CH = 256

    def body(q_ref, k_ref, v_ref, do_ref, o_ref, lse_ref, b_ref, dz_ref, dsum_ref,
             stage, rowst, qd, kd, vd, dod, rsd, res, acc, db_ref):
        masks = _head_masks()
        lane = lax.broadcasted_iota(jnp.int32, (1, W), 1)
        first_half = (lane & (DH // 2)) == 0

        def rowsum(c, carry):
            rows = pl.ds(pl.multiple_of(c * CH, CH), CH)
            prod = do_ref[rows, :].astype(F32) * o_ref[rows, :].astype(F32)
            sa = jnp.sum(jnp.where(masks[0], prod, 0.0), axis=1, keepdims=True)
            sb = jnp.sum(jnp.where(masks[1], prod, 0.0), axis=1, keepdims=True)
            rowst[rows, :] = jnp.where(first_half, lse_ref[rows, :], jnp.where(masks[0], sa, sb))
            return carry
        lax.fori_loop(0, S // CH, rowsum, 0)

        for src, dst in ((q_ref, qd), (k_ref, kd), (v_ref, vd), (do_ref, dod)):
            stage[...] = src[...].astype(F32)
            for gi, d in enumerate(BRANCH_DILATIONS[1:]):
                _deinterleave(stage, dst.at[gi], d, BF)
        for gi, d in enumerate(BRANCH_DILATIONS[1:]):
            _deinterleave(rowst, rsd.at[gi], d, F32)

        db_ref[...] = jnp.zeros_like(db_ref)
        for g, d in enumerate(BRANCH_DILATIONS):
            if d == 1:
                qs, ks, vs, dos, rs = q_ref, k_ref, v_ref, do_ref, rowst
            else:
                qs, ks, vs, dos = qd.at[g - 1], kd.at[g - 1], vd.at[g - 1], dod.at[g - 1]
                rs = rsd.at[g - 1]
            res[1] = jnp.zeros((S, W), F32)
            res[2] = jnp.zeros((S, W), F32)

            def block(q0, k0, nk, first, g=g, qs=qs, ks=ks, vs=vs, dos=dos, rs=rs):
                kk = ks[pl.ds(k0, nk), :]
                vv = vs[pl.ds(k0, nk), :]
                q2 = _stack_heads(qs[pl.ds(q0, BLK), :], masks)
                do2 = _stack_heads(dos[pl.ds(q0, BLK), :], masks)
                st = rs[pl.ds(q0, BLK), :]
                lse2 = jnp.concatenate([st[:, 0:1], st[:, DH:DH + 1]], axis=0)
                del2 = jnp.concatenate([st[:, DH // 2:DH // 2 + 1],
                                        st[:, DH + DH // 2:DH + DH // 2 + 1]], axis=0)
                bias = b_ref[g][:, BLK:] if first else b_ref[g]
                s = lax.dot_general(q2, kk, _NT, preferred_element_type=F32) * SCALE + bias
                p = jnp.exp(s - lse2)
                dp = lax.dot_general(do2, vv, _NT, preferred_element_type=F32)
                ds = p * (dp - del2)
                if first:
                    db_ref[g, :, BLK:] += ds
                else:
                    db_ref[g] += ds
                dsb = ds.astype(BF)
                dq2 = jnp.dot(dsb, kk, preferred_element_type=F32) * SCALE
                res[0, pl.ds(q0, BLK), :] = jnp.where(masks[0], dq2[:BLK], dq2[BLK:])
                res[1, pl.ds(k0, nk), :] += lax.dot_general(dsb, q2, _TN,
                                                            preferred_element_type=F32) * SCALE
                res[2, pl.ds(k0, nk), :] += lax.dot_general(p.astype(BF), do2, _TN,
                                                            preferred_element_type=F32)

            _branch_loops(d, block)
            L = S // d
            for t in range(3):
                if d == 1:
                    acc[t] = res[t]
                else:
                    for r in range(d):
                        acc[t, pl.ds(r, L, stride=d), :] = (acc[t, pl.ds(r, L, stride=d), :]
                                                            + res[t, r * L:(r + 1) * L, :])
        for t in range(3):
            dz_ref[t] = acc[t].astype(BF)

        flip = (lax.broadcasted_iota(jnp.int32, (BLK, BLK), 0)
                + lax.broadcasted_iota(jnp.int32, (BLK, BLK), 1) == BLK - 1).astype(BF)
        dsum_ref[...] = jnp.zeros_like(dsum_ref)
        for g in range(3):
            for hh in range(2):
                halves = []
                for half in range(2):
                    tile = db_ref[g, hh * BLK:(hh + 1) * BLK, half * BLK:(half + 1) * BLK]
                    hi = tile.astype(BF)
                    lo = (tile - hi.astype(F32)).astype(BF)
                    rev = (jnp.dot(hi, flip, preferred_element_type=F32)
                           + jnp.dot(lo, flip, preferred_element_type=F32))
                    skew = pltpu.roll(rev, 0, 1, stride=1, stride_axis=0)
                    halves.append(jnp.sum(skew, axis=0, keepdims=True))
                dsum_ref[2 * g + hh:2 * g + hh + 1, :] = jnp.concatenate(halves, axis=1)

    col = pl.BlockSpec((S, W), lambda hp: (0, hp))
    return pl.pallas_call(
        _after(body, 7, deps), name=name,
        grid=(D // W,),
        in_specs=[pl.BlockSpec((None, S, W), lambda hp: (0, 0, hp)),
                  pl.BlockSpec((None, S, W), lambda hp: (1, 0, hp)),
                  pl.BlockSpec((None, S, W), lambda hp: (2, 0, hp)),
                  col, col, col,
                  pl.BlockSpec((3, 2 * BLK, 2 * BLK), lambda hp: (0, hp, 0))] + [ANY] * len(deps),
        out_specs=[pl.BlockSpec((3, S, W), lambda hp: (0, 0, hp)),
                   pl.BlockSpec((None, 8, 2 * BLK), lambda hp: (hp, 0, 0))],
        out_shape=[jax.ShapeDtypeStruct((3, S, D), BF),
                   jax.ShapeDtypeStruct((D // W, 8, 2 * BLK), F32)],
        scratch_shapes=[pltpu.VMEM((S, W), F32), pltpu.VMEM((S, W), F32),
                        pltpu.VMEM((2, S, W), BF), pltpu.VMEM((2, S, W), BF),
                        pltpu.VMEM((2, S, W), BF), pltpu.VMEM((2, S, W), BF),
                        pltpu.VMEM((2, S, W), F32),
                        pltpu.VMEM((3, S, W), F32), pltpu.VMEM((3, S, W), F32),
                        pltpu.VMEM((3, 2 * BLK, 2 * BLK), F32)],
        compiler_params=_cparams(("parallel",)),
    )(z3, z3, z3, dob, ob, lse_b, bias3, *deps)


def _me():
    return lax.axis_index("x"), lax.axis_index("y"), lax.axis_index("c")


def _other_chips(x, y):
    return [(1 - x, y), (x, 1 - y), (1 - x, 1 - y)]


def _shard_window(ref, axis, t, shape):
    R, C = shape
    if axis == 0:
        return ref.at[pl.ds(pl.multiple_of(t * R, 128), R), :]
    return ref.at[:, pl.ds(pl.multiple_of(t * C, 128), C)]


def all_gather_weights(shards, axes, name):
    n = len(shards)
    shapes = [s.shape for s in shards]
    outs_shape = [jax.ShapeDtypeStruct((8 * s.shape[0], s.shape[1]) if ax == 0
                                       else (s.shape[0], 8 * s.shape[1]), s.dtype)
                  for s, ax in zip(shards, axes)]

    def body(*refs):
        ins, outs = refs[:n], refs[n:2 * n]
        send_sems, recv_sems, local_sems = refs[2 * n:]
        x, y, c = _me()
        me, sibling = (x, y, c), (x, y, 1 - c)
        xnb, ynb, diag = (1 - x, y), (x, 1 - y), (1 - x, 1 - y)
        south = c == 0
        relay_from = (jnp.where(south, x, 1 - x), jnp.where(south, 1 - y, y))
        relay_to = (jnp.where(south, 1 - x, x), jnp.where(south, y, 1 - y))
        barrier = pltpu.get_barrier_semaphore()
        for peer in [sibling, (*xnb, c), (*ynb, c)]:
            pl.semaphore_signal(barrier, inc=1, device_id=peer, device_id_type=MESH)
        pl.semaphore_wait(barrier, 3)

        def win(i, px, py, pc):
            return _shard_window(outs[i], axes[i], 4 * px + 2 * py + pc, shapes[i])

        def copy(i, k, block, to, src=None):
            return pltpu.make_async_remote_copy(
                src_ref=win(i, *block) if src is None else src, dst_ref=win(i, *block),
                send_sem=send_sems.at[i * 7 + k], recv_sem=recv_sems.at[i * 7 + k],
                device_id=to, device_id_type=MESH)

        mine = [pltpu.make_async_copy(ins[i], win(i, *me), local_sems.at[i]) for i in range(n)]
        for cp in mine:
            cp.start()
        sent = []
        for i in range(n):
            sent += [copy(i, 0, me, sibling, src=ins[i]), copy(i, 1, me, (*xnb, c), src=ins[i]),
                     copy(i, 2, me, (*ynb, c), src=ins[i])]
        for cp in sent:
            cp.start()
        for i in range(n):
            for k, chip in ((1, xnb), (2, ynb)):
                copy(i, k, (*chip, c), me).wait_recv()
                sent.append(copy(i, 3 + k, (*chip, c), sibling))
                sent[-1].start()
            sent.append(copy(i, 3, (*relay_from, c), (*relay_to, c)))
            sent[-1].start()
        for i in range(n):
            copy(i, 3, (*diag, c), me).wait_recv()
            sent.append(copy(i, 6, (*diag, c), sibling))
            sent[-1].start()
        for i in range(n):
            copy(i, 0, sibling, me).wait_recv()
            for k, chip in ((4, xnb), (5, ynb), (6, diag)):
                copy(i, k, (*chip, 1 - c), me).wait_recv()
        for cp in sent:
            cp.wait_send()
        for cp in mine:
            cp.wait()

    return pl.kernel(
        body, out_type=outs_shape, name=name,
        mesh=plsc.ScalarSubcoreMesh(axis_name="sequencer", num_cores=1),
        scratch_types=[pltpu.SemaphoreType.DMA((7 * n,)), pltpu.SemaphoreType.DMA((7 * n,)),
                       pltpu.SemaphoreType.DMA((n,))],
        compiler_params=pltpu.CompilerParams(collective_id=1),
    )(*shards)


def pair_exchange_grads(grads, axes, shapes, name):
    n = len(grads)

    def body(*refs):
        ins, outs = refs[:n], refs[n:2 * n]
        send_sems, recv_sems = refs[2 * n:]
        x, y, c = _me()
        sibling = (x, y, 1 - c)
        barrier = pltpu.get_barrier_semaphore()
        pl.semaphore_signal(barrier, inc=1, device_id=sibling, device_id_type=MESH)
        pl.semaphore_wait(barrier, 1)
        copies = []
        for i in range(n):
            for q in range(4):
                t = 2 * q + (1 - c)
                copies.append(pltpu.make_async_remote_copy(
                    src_ref=_shard_window(ins[i], axes[i], t, shapes[i]), dst_ref=outs[i].at[q],
                    send_sem=send_sems.at[i * 4 + q], recv_sem=recv_sems.at[i * 4 + q],
                    device_id=sibling, device_id_type=MESH))
        for cp in copies:
            cp.start()
        for cp in copies:
            cp.wait_recv()
        for cp in copies:
            cp.wait_send()

    return pl.kernel(
        body, out_type=[jax.ShapeDtypeStruct((4,) + tuple(sh), BF) for sh in shapes], name=name,
        mesh=plsc.ScalarSubcoreMesh(axis_name="sequencer", num_cores=1),
        scratch_types=[pltpu.SemaphoreType.DMA((4 * n,)), pltpu.SemaphoreType.DMA((4 * n,))],
        compiler_params=pltpu.CompilerParams(collective_id=2),
    )(*grads)


def pair_add(grads, landed, axes, shapes, c_idx, name, deps=()):
    n = len(grads)

    def body(c_ref, *refs):
        for t in range(n):
            refs[2 * n + t][...] = (refs[2 * t][...].astype(F32)
                                    + refs[2 * t + 1][...].astype(F32)).astype(BF)

    halves = 1
    in_specs, out_specs, out_shapes, operands = [], [], [], []
    for t in range(n):
        R, C = shapes[t]
        rh = R // halves
        if axes[t] == 0:
            in_specs.append(pl.BlockSpec(
                (rh, C), lambda q, h, c_ref: (halves * (2 * q + c_ref[0]) + h, 0)))
        else:
            in_specs.append(pl.BlockSpec((rh, C), lambda q, h, c_ref: (h, 2 * q + c_ref[0])))
        blk = pl.BlockSpec((None, rh, C), lambda q, h, c_ref: (q, h, 0))
        in_specs.append(blk)
        out_specs.append(blk)
        out_shapes.append(jax.ShapeDtypeStruct((4, R, C), BF))
        operands += [grads[t], landed[t]]
    return pl.pallas_call(
        _after(body, 1 + 2 * n, deps), name=name,
        grid_spec=pltpu.PrefetchScalarGridSpec(
            num_scalar_prefetch=1, grid=(4, halves), in_specs=in_specs + [ANY] * len(deps),
            out_specs=out_specs),
        out_shape=out_shapes,
        compiler_params=_cparams(("parallel", "parallel")),
    )(c_idx, *operands, *deps)


def chip_exchange_grads(parts, name):
    n = len(parts)

    def body(*refs):
        ins, outs, relay = refs[:n], refs[n:2 * n], refs[2 * n:3 * n]
        send_sems, recv_sems = refs[3 * n:]
        x, y, c = _me()
        xnb, ynb, diag = (1 - x, y), (x, 1 - y), (1 - x, 1 - y)
        south = c == 0
        via = (jnp.where(south, 1 - x, x), jnp.where(south, y, 1 - y))
        onward = (jnp.where(south, x, 1 - x), jnp.where(south, 1 - y, y))
        barrier = pltpu.get_barrier_semaphore()
        for peer in (xnb, ynb):
            pl.semaphore_signal(barrier, inc=1, device_id=(*peer, c), device_id_type=MESH)
        pl.semaphore_wait(barrier, 2)

        def copy(i, k, src, dst, to):
            return pltpu.make_async_remote_copy(
                src_ref=src, dst_ref=dst, send_sem=send_sems.at[i * 4 + k],
                recv_sem=recv_sems.at[i * 4 + k], device_id=(*to, c), device_id_type=MESH)

        sent = []
        for i in range(n):
            sent += [copy(i, 0, ins[i].at[2 * xnb[0] + xnb[1]], outs[i].at[0], xnb),
                     copy(i, 1, ins[i].at[2 * ynb[0] + ynb[1]], outs[i].at[1], ynb),
                     copy(i, 2, ins[i].at[2 * diag[0] + diag[1]], relay[i], via)]
        for cp in sent:
            cp.start()
        for i in range(n):
            copy(i, 2, relay[i], relay[i], via).wait_recv()
            sent.append(copy(i, 3, relay[i], outs[i].at[2], onward))
            sent[-1].start()
        for i in range(n):
            copy(i, 0, outs[i].at[0], outs[i].at[0], xnb).wait_recv()
            copy(i, 1, outs[i].at[1], outs[i].at[1], ynb).wait_recv()
            copy(i, 3, outs[i].at[2], outs[i].at[2], onward).wait_recv()
        for cp in sent:
            cp.wait_send()

    landing = [jax.ShapeDtypeStruct((3,) + tuple(p.shape[1:]), BF) for p in parts]
    staging = [jax.ShapeDtypeStruct(tuple(p.shape[1:]), BF) for p in parts]
    return pl.kernel(
        body, out_type=landing + staging, name=name,
        mesh=plsc.ScalarSubcoreMesh(axis_name="sequencer", num_cores=1),
        scratch_types=[pltpu.SemaphoreType.DMA((4 * n,)), pltpu.SemaphoreType.DMA((4 * n,))],
        compiler_params=pltpu.CompilerParams(collective_id=3),
    )(*parts)[:n]


def all_gather_small(v, name):
    R, C = v.shape

    def body(v_ref, out_ref, send_sems, recv_sems, local_sem):
        x, y, c = _me()
        me, sibling = (x, y, c), (x, y, 1 - c)
        chips = _other_chips(x, y)

        def slot(px, py, pc):
            return out_ref.at[4 * px + 2 * py + pc]

        def copy(k, block, to, src=None):
            return pltpu.make_async_remote_copy(
                src_ref=slot(*block) if src is None else src, dst_ref=slot(*block),
                send_sem=send_sems.at[k], recv_sem=recv_sems.at[k],
                device_id=to, device_id_type=MESH)

        mine = pltpu.make_async_copy(v_ref, slot(*me), local_sem)
        mine.start()
        first = [copy(0, me, sibling, src=v_ref)]
        first += [copy(1 + j, me, (*chip, c), src=v_ref) for j, chip in enumerate(chips)]
        for cp in first:
            cp.start()
        passed = [copy(4 + j, (*chip, c), sibling) for j, chip in enumerate(chips)]
        for j, chip in enumerate(chips):
            copy(1 + j, (*chip, c), me).wait_recv()
            passed[j].start()
        copy(0, sibling, me).wait_recv()
        for j, chip in enumerate(chips):
            copy(4 + j, (*chip, 1 - c), me).wait_recv()
        for cp in first + passed:
            cp.wait_send()
        mine.wait()

    return pl.pallas_call(
        body, name=name,
        in_specs=[pl.BlockSpec(memory_space=pltpu.VMEM)],
        out_specs=pl.BlockSpec(memory_space=pltpu.VMEM),
        out_shape=jax.ShapeDtypeStruct((NDEV, R, C), F32),
        scratch_shapes=[pltpu.SemaphoreType.DMA((7,)), pltpu.SemaphoreType.DMA((7,)),
                        pltpu.SemaphoreType.DMA],
    )(v)


def _adamw(w, g, m, v):
    m = ADAM_B1 * m + (1.0 - ADAM_B1) * g
    v = ADAM_B2 * v + (1.0 - ADAM_B2) * (g * g)
    m_hat = m / (1.0 - ADAM_B1 ** ADAM_STEP)
    v_hat = v / (1.0 - ADAM_B2 ** ADAM_STEP)
    delta = -ADAM_LR * (m_hat / (jnp.sqrt(v_hat) + ADAM_EPS) + ADAM_WD * w)
    return delta, m, v


def reduce_adamw(parts, landed, params, q_idx, name, prevs, deps=()):
    n = len(parts)
    halves = 2
    in_specs, out_specs, out_shapes, operands, extra, aliases = [], [], [], [], [], {}
    for t in range(n):
        R, C = parts[t].shape[1:]
        w, m, v, layer = params[t]
        r, c = w.shape[1:]
        tr = r // halves
        assert tr % 16 == 0 and c == C
        wspec = pl.BlockSpec((None, tr, c), lambda i, q_ref, layer=layer: (layer, i, 0))
        in_specs += [pl.BlockSpec((None, tr, C), lambda i, q_ref: (q_ref[0], i, 0)),
                     pl.BlockSpec((3, tr, C), lambda i, q_ref: (0, i, 0)), wspec, wspec, wspec]
        out_specs += [wspec] * 4
        out_shapes += [jax.ShapeDtypeStruct(w.shape, F32)] * 4
        operands += [parts[t], landed[t], w, m, v]
        for k, buf in enumerate(prevs[t]):
            aliases[1 + 5 * n + len(extra)] = 4 * t + k
            extra.append(buf)
    extra += list(deps)

    def body(q_ref, *refs):
        for t in range(n):
            p_ref, l_ref, w_ref, m_ref, v_ref = refs[5 * t:5 * t + 5]
            g = p_ref[...].astype(F32)
            for k in range(3):
                g = g + l_ref[k].astype(F32)
            d, mm, vv = _adamw(w_ref[...], g, m_ref[...], v_ref[...])
            outs = refs[5 * n + 4 * t:5 * n + 4 * t + 4]
            outs[0][...] = g
            outs[1][...] = d
            outs[2][...] = mm
            outs[3][...] = vv

    res = pl.pallas_call(
        _after(body, 1 + 5 * n, extra), name=name,
        grid_spec=pltpu.PrefetchScalarGridSpec(
            num_scalar_prefetch=1, grid=(halves,),
            in_specs=in_specs + [ANY] * len(extra), out_specs=out_specs),
        out_shape=out_shapes,
        input_output_aliases=aliases,
        compiler_params=_cparams(("parallel",)),
    )(q_idx, *operands, *extra)
    return [res[4 * t:4 * t + 4] for t in range(n)]


def small_reduce_adamw(gathered, w, m, v, name):
    R, C = w.shape

    def body(a_ref, w_ref, m_ref, v_ref, g_out, d_out, m_out, v_out):
        g = a_ref[0]
        for k in range(1, NDEV):
            g = g + a_ref[k]
        d, mm, vv = _adamw(w_ref[...], g, m_ref[...], v_ref[...])
        g_out[...] = g
        d_out[...] = d
        m_out[...] = mm
        v_out[...] = vv

    out = jax.ShapeDtypeStruct((R, C), F32)
    return pl.pallas_call(body, name=name, out_shape=[out] * 4,
                          compiler_params=_cparams())(gathered, w, m, v)


def _pad_rows(a, n):
    return jnp.pad(a, ((0, n - a.shape[0]), (0, 0)))


def _pack_small(mix, ffn, fin, taps_full, relb, scalar=0.0):
    n = NUM_BUCKETS * H
    last = jnp.concatenate([relb.reshape(1, n), jnp.reshape(scalar, (1, 1)).astype(F32),
                            jnp.zeros((1, D - n - 1), F32)], axis=1)
    return jnp.concatenate([mix, ffn, fin.reshape(1, D), taps_full.reshape(6, D), last], axis=0)


def kernel(x, mix_norm, ffn_norm, final_norm, conv_w_in, conv_kernel, conv_w_out, attn_w_qkv, attn_w_out, rel_bias, ffn_w_gate, ffn_w_up, ffn_w_down, loss_target, m_mix_norm, m_ffn_norm, m_final_norm, m_conv_w_in, m_conv_kernel, m_conv_w_out, m_attn_w_qkv, m_attn_w_out, m_rel_bias, m_ffn_w_gate, m_ffn_w_up, m_ffn_w_down, v_mix_norm, v_ffn_norm, v_final_norm, v_conv_w_in, v_conv_kernel, v_conv_w_out, v_attn_w_qkv, v_attn_w_out, v_rel_bias, v_ffn_w_gate, v_ffn_w_up, v_ffn_w_down):
    xi, yi, ci = _me()
    me = 4 * xi + 2 * yi + ci
    c_idx = jnp.reshape(ci, (1,)).astype(jnp.int32)
    q_idx = jnp.reshape(2 * xi + yi, (1,)).astype(jnp.int32)
    col0 = me * (D // NDEV)

    gate_t, up_t = jnp.swapaxes(ffn_w_gate, 1, 2), jnp.swapaxes(ffn_w_up, 1, 2)
    m_gate_t, m_up_t = jnp.swapaxes(m_ffn_w_gate, 1, 2), jnp.swapaxes(m_ffn_w_up, 1, 2)
    v_gate_t, v_up_t = jnp.swapaxes(v_ffn_w_gate, 1, 2), jnp.swapaxes(v_ffn_w_up, 1, 2)

    mixer_in = (conv_w_in, attn_w_qkv)
    mixer_out = (conv_w_out, attn_w_out)
    taps_shard = jnp.pad(conv_kernel.reshape(6, D // NDEV), ((0, 2), (0, 0)))
    wts = []
    for i in range(DEPTH):
        j = i // 2
        shards = [mixer_in[i % 2][j].astype(BF), mixer_out[i % 2][j].astype(BF),
                  _pad_rows(gate_t[i].astype(BF), FF_SHARD_PAD),
                  _pad_rows(up_t[i].astype(BF), FF_SHARD_PAD),
                  _pad_rows(ffn_w_down[i].astype(BF), FF_SHARD_PAD)]
        axes = (1, 0, 0, 0, 0)
        groups = ((0, 1), (1, 2), (2, 4), (4, 5)) if i == 0 else ((0, 2), (2, 5))
        layer = []
        for lo, hi in groups:
            extra = [taps_shard] if (i, lo) == (0, 1) else []
            got = list(all_gather_weights(shards[lo:hi] + extra, axes[lo:hi] + (1,) * len(extra),
                                          f"ag_l{i}_{lo}"))
            if extra:
                taps_all = got.pop()
            layer += got
        wts.append(layer)
    taps = [jnp.pad(taps_all[3 * j:3 * j + 3], ((0, 5), (0, 0))) for j in range(2)]

    onehot_t, band = _bucket_onehot_t()
    bias3 = bias_tables(rel_bias.T, onehot_t, band, "bias_tables").reshape(3, H * BLK, 2 * BLK)

    mix_gains, ffn_gains = mix_norm.reshape(DEPTH, 1, D), ffn_norm.reshape(DEPTH, 1, D)
    saved = []
    xc = x[0]
    for i in range(DEPTH):
        w_in, w_out, w_g, w_u, w_d = wts[i]
        j = i // 2
        x_mix = xc
        z3, h_mix = norm_matmul3(xc, (mix_gains, i), w_in, f"mix_in_l{i}")
        if i % 2 == 0:
            act = conv_fwd(z3, taps[j], f"conv_fwd_l{i}")
            lse_b = None
        else:
            act, lse_b = attention_fwd(z3, bias3, f"attn_fwd_l{i}")
        xc = matmul_residual(act, w_out, xc, f"mix_out_l{i}")
        x_ffn = xc
        g, u, a, h_ffn = norm_swiglu_up(xc, (ffn_gains, i), w_g, w_u, f"ffn_up_l{i}")
        xc = matmul_residual(a, w_d, xc, f"ffn_down_l{i}")
        saved.append((x_mix, h_mix, z3, act, lse_b, x_ffn, h_ffn, g, u, a))

    dx, dxb, dg_final, sq = loss_head(xc, final_norm.reshape(1, D), loss_target[0], "loss_head")
    loss_local = 0.5 * jnp.sum(sq[0]) / D

    dg_mix = [None] * DEPTH
    dg_ffn = [None] * DEPTH
    dtaps = [None, None]
    dbias_all = []
    shape_in, shape_out = (D, 3 * D // NDEV), (D // NDEV, D)
    ffn_axes, ffn_shapes = (0, 0, 0), ((FF_SHARD_PAD, D),) * 3
    stacked = {}

    def pair_stage(grads, landed1, axes, shapes, tag, tok):
        parts = pair_add(grads, landed1, axes, shapes, c_idx, f"rs_add_{tag}", deps=[tok])
        return parts, chip_exchange_grads(parts, f"rs_chip_{tag}"), parts[-1]

    def adamw_stage(parts, landed2, params, tag, tok):
        names = [p[0] for p in params]
        res = reduce_adamw(parts, landed2, [p[1:] for p in params], q_idx, f"adamw_{tag}",
                           [stacked.get(nm, ()) for nm in names], deps=[tok])
        for nm, r4 in zip(names, res):
            stacked[nm] = r4
        return res[-1][0]

    tok = dxb
    mix_wait = None
    mix_chip = None
    ffn_chip = []
    for i in reversed(range(DEPTH)):
        w_in, w_out, w_g, w_u, w_d = wts[i]
        j = i // 2
        x_mix, h_mix, z3, act, lse_b, x_ffn, h_ffn, g, u, a = saved[i]
        ffn_params = [("ffn_w_gate", gate_t, m_gate_t, v_gate_t, i),
                      ("ffn_w_up", up_t, m_up_t, v_up_t, i),
                      ("ffn_w_down", ffn_w_down, m_ffn_w_down, v_ffn_w_down, i)]
        if i % 2 == 0:
            mix_params = [("conv_w_in", conv_w_in, m_conv_w_in, v_conv_w_in, j),
                          ("conv_w_out", conv_w_out, m_conv_w_out, v_conv_w_out, j)]
        else:
            mix_params = [("attn_w_qkv", attn_w_qkv, m_attn_w_qkv, v_attn_w_qkv, j),
                          ("attn_w_out", attn_w_out, m_attn_w_out, v_attn_w_out, j)]
        dgate, dup = swiglu_bwd_da(dxb, w_d, g, u, f"ffn_da_l{i}", deps=[tok])
        tok = dgate
        for group in ffn_chip:
            tok = adamw_stage(*group, tok)
        ffn_chip = []
        if mix_wait is not None:
            grads_m, landed1_m, params_m, tag_m = mix_wait
            parts_m, landed2_m, tok = pair_stage(grads_m, landed1_m, (1, 0), (shape_in, shape_out),
                                                 tag_m, tok)
            mix_chip = (parts_m, landed2_m, params_m, tag_m)
            mix_wait = None
        products = [(dgate, h_ffn), (dup, h_ffn), (a, dxb)]
        ffn_wait = []
        for gi, idxs in enumerate(((0,), (1, 2)) if i == 0 else ((0, 1, 2),)):
            grads_f = matmul_tn_group([products[k] for k in idxs], f"ffn_dw_l{i}_{gi}", deps=[tok])
            landed1_f = pair_exchange_grads(grads_f, ffn_axes[:len(idxs)], ffn_shapes[:len(idxs)],
                                            f"rs_pair_f{i}_{gi}")
            ffn_wait.append((grads_f, landed1_f, [ffn_params[k] for k in idxs], f"f{i}_{gi}"))
            tok = grads_f[-1]

        def ffn_pair_stages(groups, tok):
            for grads_f, landed1_f, params_f, tag_f in groups:
                nf = len(grads_f)
                parts_f, landed2_f, tok = pair_stage(grads_f, landed1_f, ffn_axes[:nf],
                                                     ffn_shapes[:nf], tag_f, tok)
                ffn_chip.append((parts_f, landed2_f, params_f, tag_f))
            return tok

        if len(ffn_wait) > 1:
            if mix_chip is not None:
                tok = adamw_stage(*mix_chip, tok)
                mix_chip = None
            tok = ffn_pair_stages(ffn_wait[:-1], tok)
        dx, dxb, dg_ffn[i] = matmul_normbwd(
            [(dgate, w_g, False), (dup, w_u, False)], x_ffn, (ffn_gains, i), dx, f"ffn_dh_l{i}",
            deps=[tok])
        dxb_mix = dxb
        tok = dxb
        if mix_chip is not None:
            tok = adamw_stage(*mix_chip, tok)
            mix_chip = None
        tok = ffn_pair_stages(ffn_wait[-1:], tok)
        dact = matmul_nt(dxb, w_out, f"mix_dact_l{i}", out_dtype=F32 if i % 2 == 0 else BF,
                         deps=[tok])
        if i % 2 == 0:
            dz3, dtaps[j] = conv_bwd(dact, z3, taps[j], f"conv_bwd_l{i}")
        else:
            dz3, dsum = attention_bwd(z3, dact, act, lse_b, bias3, f"attn_bwd_l{i}")
            dbias_all.append(dsum[:, :6].reshape(H // 2, 3, 2, 2 * BLK).transpose(1, 0, 2, 3)
                             .reshape(3, H, 2 * BLK))
        grads_m = matmul_tn_group([(h_mix, dz3), (act, dxb_mix)], f"mix_dw_l{i}")
        landed1_m = pair_exchange_grads(grads_m, (1, 0), (shape_in, shape_out), f"rs_pair_m{i}")
        mix_wait = (grads_m, landed1_m, mix_params, f"m{i}")
        dx, dxb, dg_mix[i] = matmul_normbwd(
            [(dz3, w_in, True)], x_mix, (mix_gains, i), dx, f"mix_dh_l{i}", deps=[grads_m[-1]])
        tok = dxb
    for group in ffn_chip:
        tok = adamw_stage(*group, tok)
    grads_m, landed1_m, params_m, tag_m = mix_wait
    parts_m, landed2_m, tok = pair_stage(grads_m, landed1_m, (1, 0), (shape_in, shape_out), tag_m, tok)

    grad_relb_t = bias_grad(jnp.concatenate(dbias_all), _diagonal_onehot_t(), "bias_grad")
    dtaps_full = jnp.stack([dtaps[0][:3], dtaps[1][:3]])
    g_small = _pack_small(jnp.concatenate([d[0:1] for d in dg_mix], axis=0),
                          jnp.concatenate([d[0:1] for d in dg_ffn], axis=0),
                          dg_final[0], dtaps_full, grad_relb_t.T, loss_local)
    gathered = all_gather_small(g_small, "ag_small_grads")

    def taps_at_cols(k):
        return lax.dynamic_update_slice(jnp.zeros((2, 3, D), F32), k, (0, 0, col0))

    w_small = _pack_small(mix_norm, ffn_norm, final_norm, taps_at_cols(conv_kernel), rel_bias)
    m_small = _pack_small(m_mix_norm, m_ffn_norm, m_final_norm, taps_at_cols(m_conv_kernel), m_rel_bias)
    v_small = _pack_small(v_mix_norm, v_ffn_norm, v_final_norm, taps_at_cols(v_conv_kernel), v_rel_bias)
    small = small_reduce_adamw(gathered, w_small, m_small, v_small, "adamw_small")

    def unpack_small(p):
        taps_p = lax.dynamic_slice(p[9:15].reshape(2, 3, D), (0, 0, col0), (2, 3, D // NDEV))
        return {"mix_norm": p[0:4], "ffn_norm": p[4:8], "final_norm": p[8],
                "conv_kernel": taps_p, "rel_bias": p[15, :NUM_BUCKETS * H].reshape(NUM_BUCKETS, H)}

    small_out = [unpack_small(p) for p in small]
    loss = small[0][15, NUM_BUCKETS * H]
    adamw_stage(parts_m, landed2_m, params_m, tag_m, small[0])

    names = ["mix_norm", "ffn_norm", "final_norm", "conv_w_in", "conv_kernel", "conv_w_out",
             "attn_w_qkv", "attn_w_out", "rel_bias", "ffn_w_gate", "ffn_w_up", "ffn_w_down"]
    outs = [loss, dx.reshape(1, S, D)]
    for o in range(4):
        for nme in names:
            if nme in ("ffn_w_gate", "ffn_w_up"):
                outs.append(jnp.swapaxes(stacked[nme][o], 1, 2))
            else:
                outs.append(stacked[nme][o] if nme in stacked else small_out[o][nme])
    return tuple(outs)
```

```python
import math

import numpy as np
import jax
import jax.numpy as jnp
from jax import lax
from jax.experimental import pallas as pl
from jax.experimental.pallas import tpu as pltpu
from jax.experimental.pallas import tpu_sc as plsc

S = 2048
D = 1024
H = 16
DH = 64
DFF = 2816
NDEV = 8
DEPTH = 4
FF_SHARD = DFF // NDEV
FF_SHARD_PAD = 384
DFF_PAD = FF_SHARD_PAD * NDEV
BLK = 128
BRANCH_DILATIONS = (1, 4, 16)
NUM_BUCKETS = 32
MAX_DISTANCE = 2048
EPS = 1e-6
NEG_INF = -1e30
SCALE = DH ** -0.5

ADAM_LR = 0.001
ADAM_B1 = 0.9
ADAM_B2 = 0.999
ADAM_EPS = 1e-08
ADAM_WD = 0.01
ADAM_STEP = 10

BF = jnp.bfloat16
F32 = jnp.float32
VMEM_LIMIT_BYTES = 56 * 1024 * 1024
KSPLIT = 512
NORM_CHUNK = 256
MESH = pl.DeviceIdType.MESH
ANY = pl.BlockSpec(memory_space=pl.ANY)

_NT = (((1,), (1,)), ((), ()))
_TN = (((0,), (0,)), ((), ()))


def _cparams(sem=None):
    return pltpu.CompilerParams(dimension_semantics=sem, vmem_limit_bytes=VMEM_LIMIT_BYTES)


def _after(body, n, deps):
    nd = len(deps)
    if nd == 0:
        return body

    def ordered(*refs):
        body(*refs[:n], *refs[n + nd:])
    return ordered


def _rms(x):
    return lax.rsqrt(jnp.mean(x * x, axis=-1, keepdims=True) + EPS)


def norm_matmul3(x, gain, w, name, tm=1024, tn=1024):
    per = D // tn

    def body(x_ref, g_ref, w_ref, z_ref, h_ref, hs_ref):
        @pl.when(pl.program_id(1) == 0)
        def _():
            for c in range(tm // NORM_CHUNK):
                rows = slice(c * NORM_CHUNK, (c + 1) * NORM_CHUNK)
                xv = x_ref[rows, :]
                hv = (xv * _rms(xv) * g_ref[...]).astype(BF)
                hs_ref[rows, :] = hv
                h_ref[rows, :] = hv
                z_ref[rows, :] = jnp.dot(hv, w_ref[...], preferred_element_type=F32).astype(BF)

        @pl.when(pl.program_id(1) > 0)
        def _():
            z_ref[...] = jnp.dot(hs_ref[...], w_ref[...], preferred_element_type=F32).astype(BF)

    return pl.pallas_call(
        body, name=name,
        grid=(S // tm, 3 * D // tn),
        in_specs=[pl.BlockSpec((tm, D), lambda i, j: (i, 0)),
                  pl.BlockSpec((None, 1, D), lambda i, j: (gain[1], 0, 0)),
                  pl.BlockSpec((D, tn), lambda i, j: (0, j))],
        out_specs=[pl.BlockSpec((None, tm, tn), lambda i, j: (j // per, i, j % per)),
                   pl.BlockSpec((tm, D), lambda i, j: (i, 0))],
        out_shape=[jax.ShapeDtypeStruct((3, S, D), BF), jax.ShapeDtypeStruct((S, D), BF)],
        scratch_shapes=[pltpu.VMEM((tm, D), BF)],
        compiler_params=_cparams(("parallel", "arbitrary")),
    )(x, gain[0], w)


def norm_swiglu_up(x, gain, wg_t, wu_t, name, tm=1024, tn=768):
    def body(x_ref, g_ref, wg_ref, wu_ref, go_ref, uo_ref, ao_ref, h_ref, hs_ref):
        def gate_up(hv, rows):
            g = lax.dot_general(hv, wg_ref[...], _NT, preferred_element_type=F32)
            u = lax.dot_general(hv, wu_ref[...], _NT, preferred_element_type=F32)
            go_ref[rows, :] = g.astype(BF)
            uo_ref[rows, :] = u.astype(BF)
            ao_ref[rows, :] = (g * jax.nn.sigmoid(g) * u).astype(BF)

        @pl.when(pl.program_id(1) == 0)
        def _():
            for c in range(tm // NORM_CHUNK):
                rows = slice(c * NORM_CHUNK, (c + 1) * NORM_CHUNK)
                xv = x_ref[rows, :]
                hv = (xv * _rms(xv) * g_ref[...]).astype(BF)
                hs_ref[rows, :] = hv
                h_ref[rows, :] = hv
                gate_up(hv, rows)

        @pl.when(pl.program_id(1) > 0)
        def _():
            gate_up(hs_ref[...], slice(None))

    act = jax.ShapeDtypeStruct((S, DFF_PAD), BF)
    blk = pl.BlockSpec((tm, tn), lambda i, j: (i, j))
    return pl.pallas_call(
        body, name=name,
        grid=(S // tm, DFF_PAD // tn),
        in_specs=[pl.BlockSpec((tm, D), lambda i, j: (i, 0)),
                  pl.BlockSpec((None, 1, D), lambda i, j: (gain[1], 0, 0)),
                  pl.BlockSpec((tn, D), lambda i, j: (j, 0)),
                  pl.BlockSpec((tn, D), lambda i, j: (j, 0))],
        out_specs=[blk, blk, blk, pl.BlockSpec((tm, D), lambda i, j: (i, 0))],
        out_shape=[act, act, act, jax.ShapeDtypeStruct((S, D), BF)],
        scratch_shapes=[pltpu.VMEM((tm, D), BF)],
        compiler_params=_cparams(("parallel", "arbitrary")),
    )(x, gain[0], wg_t, wu_t)


def matmul_residual(a, w, x, name, tm=1024):
    K = a.shape[1]
    tn = D if K <= D else D // 2
    ns = K // KSPLIT
    kc = K // ns

    def body(*refs):
        x_ref, o_ref = refs[2 * ns:]
        acc = x_ref[...]
        for s in range(ns):
            acc = acc + jnp.dot(refs[s][...], refs[ns + s][...], preferred_element_type=F32)
        o_ref[...] = acc

    return pl.pallas_call(
        body, name=name,
        grid=(S // tm, D // tn),
        in_specs=[pl.BlockSpec((tm, kc), lambda i, j, s=s: (i, s)) for s in range(ns)]
        + [pl.BlockSpec((kc, tn), lambda i, j, s=s: (s, j)) for s in range(ns)]
        + [pl.BlockSpec((tm, tn), lambda i, j: (i, j))],
        out_specs=pl.BlockSpec((tm, tn), lambda i, j: (i, j)),
        out_shape=jax.ShapeDtypeStruct((S, D), F32),
        compiler_params=_cparams(("parallel", "parallel")),
    )(*([a] * ns), *([w] * ns), x)


def matmul_nt(a, w, name, out_dtype=BF, tm=1024, tn=1024, deps=()):
    K = a.shape[1]
    N = w.shape[0]

    def body(a_ref, w_ref, o_ref):
        o_ref[...] = lax.dot_general(a_ref[...], w_ref[...], _NT,
                                     preferred_element_type=F32).astype(o_ref.dtype)

    return pl.pallas_call(
        _after(body, 2, deps), name=name,
        grid=(S // tm, N // tn),
        in_specs=[pl.BlockSpec((tm, K), lambda i, j: (i, 0)),
                  pl.BlockSpec((tn, K), lambda i, j: (j, 0))] + [ANY] * len(deps),
        out_specs=pl.BlockSpec((tm, tn), lambda i, j: (i, j)),
        out_shape=jax.ShapeDtypeStruct((S, N), out_dtype),
        compiler_params=_cparams(("parallel", "parallel")),
    )(a, w, *deps)


def swiglu_bwd_da(dxb, wd, g, u, name, tm=1024, tn=768, deps=()):
    def body(dx_ref, w_ref, g_ref, u_ref, dg_ref, du_ref):
        da = lax.dot_general(dx_ref[...], w_ref[...], _NT, preferred_element_type=F32)
        gv = g_ref[...].astype(F32)
        uv = u_ref[...].astype(F32)
        sig = jax.nn.sigmoid(gv)
        dg_ref[...] = (da * uv * (sig * (1.0 + gv * (1.0 - sig)))).astype(BF)
        du_ref[...] = (da * (gv * sig)).astype(BF)

    act = jax.ShapeDtypeStruct((S, DFF_PAD), BF)
    blk = pl.BlockSpec((tm, tn), lambda i, j: (i, j))
    return pl.pallas_call(
        _after(body, 4, deps), name=name,
        grid=(S // tm, DFF_PAD // tn),
        in_specs=[pl.BlockSpec((tm, D), lambda i, j: (i, 0)),
                  pl.BlockSpec((tn, D), lambda i, j: (j, 0)),
                  blk, blk] + [ANY] * len(deps),
        out_specs=[blk, blk],
        out_shape=[act, act],
        compiler_params=_cparams(("parallel", "parallel")),
    )(dxb, wd, g, u, *deps)


def matmul_tn_group(pairs, name, tm=1024, deps=()):
    P = len(pairs)
    tn = D
    if P > 2:
        tm = tm // 2
    steps = []
    for p, (a, b) in enumerate(pairs):
        N = 3 * D if b.ndim == 3 else b.shape[1]
        steps += [(p, i, j) for i in range(a.shape[1] // tm) for j in range(N // tn)]
    T = len(steps)
    tab = np.zeros((T, 1 + 2 * P), np.int32)
    for p in range(P):
        cur = (0, 0)
        for s, (ph, i, j) in enumerate(steps):
            if ph == p:
                cur = (i, j)
            tab[s, 1 + 2 * p:3 + 2 * p] = cur
    tab[:, 0] = [ph for ph, _, _ in steps]

    in_specs, out_specs, out_shapes, operands = [], [], [], []
    per = D // tn
    for p, (a, b) in enumerate(pairs):
        ci, cj = 1 + 2 * p, 2 + 2 * p
        in_specs.append(pl.BlockSpec((S, tm), lambda s, t, ci=ci: (0, t[s, ci])))
        if b.ndim == 3:
            in_specs.append(pl.BlockSpec((None, S, tn),
                                         lambda s, t, cj=cj: (t[s, cj] // per, 0, t[s, cj] % per)))
            N = 3 * D
        else:
            in_specs.append(pl.BlockSpec((S, tn), lambda s, t, cj=cj: (0, t[s, cj])))
            N = b.shape[1]
        out_specs.append(pl.BlockSpec((tm, tn), lambda s, t, ci=ci, cj=cj: (t[s, ci], t[s, cj])))
        out_shapes.append(jax.ShapeDtypeStruct((a.shape[1], N), BF))
        operands += [a, b]

    def body(tab_ref, *refs):
        phase = tab_ref[pl.program_id(0), 0]
        for p in range(P):
            @pl.when(phase == p)
            def _(p=p):
                refs[2 * P + p][...] = lax.dot_general(
                    refs[2 * p][...], refs[2 * p + 1][...], _TN,
                    preferred_element_type=F32).astype(BF)

    return pl.pallas_call(
        _after(body, 1 + 2 * P, deps), name=name,
        grid_spec=pltpu.PrefetchScalarGridSpec(
            num_scalar_prefetch=1, grid=(T,), in_specs=in_specs + [ANY] * len(deps),
            out_specs=out_specs),
        out_shape=out_shapes,
        compiler_params=_cparams(("arbitrary",)),
    )(jnp.asarray(tab), *operands, *deps)


def matmul_normbwd(terms, x_in, gain, dx, name, tm=512, ch=256, deps=()):
    specs, operands = [], []
    for (a, w, stacked) in terms:
        if stacked:
            specs.append(pl.BlockSpec((3, tm, D), lambda i: (0, i, 0)))
        else:
            specs.append(pl.BlockSpec((tm, a.shape[1]), lambda i: (i, 0)))
        specs.append(pl.BlockSpec(w.shape, lambda i: (0, 0), pipeline_mode=pl.Buffered(1)))
        operands += [a, w]
    nt = len(terms)

    def body(*refs):
        aw = refs[:2 * nt]
        x_ref, g_ref, dx_ref, dxo_ref, dxb_ref, dg_ref = refs[2 * nt:]

        @pl.when(pl.program_id(0) == 0)
        def _():
            dg_ref[...] = jnp.zeros_like(dg_ref)

        dgain = None
        for c in range(tm // ch):
            rows = slice(c * ch, (c + 1) * ch)
            dh = None
            for t, (_, _, stacked) in enumerate(terms):
                a_ref, w_ref = aw[2 * t], aw[2 * t + 1]
                if stacked:
                    parts = [lax.dot_general(a_ref[k, rows, :], w_ref[:, k * D:(k + 1) * D], _NT,
                                             preferred_element_type=F32) for k in range(3)]
                else:
                    parts = [jnp.dot(a_ref[rows, :], w_ref[...], preferred_element_type=F32)]
                for p in parts:
                    dh = p if dh is None else dh + p
            xv = x_ref[rows, :]
            r = _rms(xv)
            xhat = xv * r
            part = jnp.sum(dh * xhat, axis=0, keepdims=True)
            dgain = part if dgain is None else dgain + part
            dxh = dh * g_ref[...]
            dxn = r * (dxh - xhat * jnp.mean(dxh * xhat, axis=-1, keepdims=True))
            out = dx_ref[rows, :] + dxn
            dxo_ref[rows, :] = out
            dxb_ref[rows, :] = out.astype(BF)
        dg_ref[0:1, :] += dgain

    row = pl.BlockSpec((tm, D), lambda i: (i, 0))
    return pl.pallas_call(
        _after(body, 2 * nt + 3, deps), name=name,
        grid=(S // tm,),
        in_specs=specs + [row, pl.BlockSpec((None, 1, D), lambda i: (gain[1], 0, 0)), row]
        + [ANY] * len(deps),
        out_specs=[row, row, pl.BlockSpec((8, D), lambda i: (0, 0))],
        out_shape=[jax.ShapeDtypeStruct((S, D), F32), jax.ShapeDtypeStruct((S, D), BF),
                   jax.ShapeDtypeStruct((8, D), F32)],
        compiler_params=_cparams(("arbitrary",)),
    )(*operands, x_in, gain[0], dx, *deps)


def loss_head(x, gain, target, name, tm=512):
    def body(x_ref, g_ref, t_ref, dxo_ref, dxb_ref, dg_ref, sq_ref):
        @pl.when(pl.program_id(0) == 0)
        def _():
            dg_ref[...] = jnp.zeros_like(dg_ref)
            sq_ref[...] = jnp.zeros_like(sq_ref)
        xv = x_ref[...]
        r = _rms(xv)
        xhat = xv * r
        err = xhat * g_ref[...] - t_ref[...]
        sq_ref[0:1, :] += jnp.sum(err * err, axis=0, keepdims=True)
        dy = err * (1.0 / D)
        dg_ref[0:1, :] += jnp.sum(dy * xhat, axis=0, keepdims=True)
        dxh = dy * g_ref[...]
        out = r * (dxh - xhat * jnp.mean(dxh * xhat, axis=-1, keepdims=True))
        dxo_ref[...] = out
        dxb_ref[...] = out.astype(BF)

    row = pl.BlockSpec((tm, D), lambda i: (i, 0))
    acc = pl.BlockSpec((8, D), lambda i: (0, 0))
    return pl.pallas_call(
        body, name=name,
        grid=(S // tm,),
        in_specs=[row, pl.BlockSpec((1, D), lambda i: (0, 0)), row],
        out_specs=[row, row, acc, acc],
        out_shape=[jax.ShapeDtypeStruct((S, D), F32), jax.ShapeDtypeStruct((S, D), BF),
                   jax.ShapeDtypeStruct((8, D), F32), jax.ShapeDtypeStruct((8, D), F32)],
        compiler_params=_cparams(("arbitrary",)),
    )(x, gain, target)


CONV_TM = 256
HALO = 16


def _halo_row(halo, r):
    hrow = lax.broadcasted_iota(jnp.int32, halo.shape, 0)
    return jnp.sum(jnp.where(hrow == r, halo, 0.0), axis=0, keepdims=True)


def _prev_rows(p, halo_p, n, row):
    out = pltpu.roll(p, n, axis=0)
    for k in range(n):
        out = jnp.where(row == k, _halo_row(halo_p, HALO - n + k), out)
    return out


def _next_rows(p, halo_p, n, row):
    tm = p.shape[0]
    out = pltpu.roll(p, tm - n, axis=0)
    for k in range(n):
        out = jnp.where(row == tm - n + k, _halo_row(halo_p, k), out)
    return out


def _conv_specs():
    per = CONV_TM // HALO
    main = pl.BlockSpec((3, CONV_TM, D), lambda i: (0, i, 0))
    prev = pl.BlockSpec((3, HALO, D), lambda i: (0, jnp.maximum(i * per - 1, 0), 0))
    nxt = pl.BlockSpec((3, HALO, D), lambda i: (0, jnp.minimum((i + 1) * per, S // HALO - 1), 0))
    return main, prev, nxt


def conv_fwd(z3, taps, name):
    def body(z_ref, zp_ref, k_ref, m_ref):
        i = pl.program_id(0)
        p = z_ref[1].astype(F32) * z_ref[2].astype(F32)
        halo = zp_ref[1].astype(F32) * zp_ref[2].astype(F32) * jnp.where(i > 0, 1.0, 0.0)
        row = lax.broadcasted_iota(jnp.int32, p.shape, 0)
        y = (k_ref[2:3, :] * p + k_ref[1:2, :] * _prev_rows(p, halo, 1, row)
             + k_ref[0:1, :] * _prev_rows(p, halo, 2, row))
        m_ref[...] = (z_ref[0].astype(F32) * y).astype(BF)

    main, prev, _ = _conv_specs()
    return pl.pallas_call(
        body, name=name,
        grid=(S // CONV_TM,),
        in_specs=[main, prev, pl.BlockSpec((8, D), lambda i: (0, 0))],
        out_specs=pl.BlockSpec((CONV_TM, D), lambda i: (i, 0)),
        out_shape=jax.ShapeDtypeStruct((S, D), BF),
        compiler_params=_cparams(("parallel",)),
    )(z3, z3, taps)


def conv_bwd(dm, z3, taps, name, deps=()):
    per = CONV_TM // HALO

    def body(dm_ref, dmn_ref, z_ref, zp_ref, zn_ref, k_ref, dz_ref, dk_ref):
        i = pl.program_id(0)

        @pl.when(i == 0)
        def _():
            dk_ref[...] = jnp.zeros_like(dk_ref)

        dmv = dm_ref[...]
        b = z_ref[0].astype(F32)
        c = z_ref[1].astype(F32)
        u = z_ref[2].astype(F32)
        p = c * u
        halo_p = zp_ref[1].astype(F32) * zp_ref[2].astype(F32) * jnp.where(i > 0, 1.0, 0.0)
        halo_dy = (dmn_ref[...] * zn_ref[0].astype(F32)
                   * jnp.where(i < S // CONV_TM - 1, 1.0, 0.0))
        row = lax.broadcasted_iota(jnp.int32, p.shape, 0)
        p1 = _prev_rows(p, halo_p, 1, row)
        p2 = _prev_rows(p, halo_p, 2, row)
        y = k_ref[2:3, :] * p + k_ref[1:2, :] * p1 + k_ref[0:1, :] * p2
        dy = dmv * b
        dz_ref[0] = (dmv * y).astype(BF)
        dp = (k_ref[2:3, :] * dy + k_ref[1:2, :] * _next_rows(dy, halo_dy, 1, row)
              + k_ref[0:1, :] * _next_rows(dy, halo_dy, 2, row))
        dz_ref[1] = (dp * u).astype(BF)
        dz_ref[2] = (dp * c).astype(BF)
        dk_ref[0:1, :] += jnp.sum(dy * p2, axis=0, keepdims=True)
        dk_ref[1:2, :] += jnp.sum(dy * p1, axis=0, keepdims=True)
        dk_ref[2:3, :] += jnp.sum(dy * p, axis=0, keepdims=True)

    main, prev, nxt = _conv_specs()
    return pl.pallas_call(
        _after(body, 6, deps), name=name,
        grid=(S // CONV_TM,),
        in_specs=[pl.BlockSpec((CONV_TM, D), lambda i: (i, 0)),
                  pl.BlockSpec((HALO, D), lambda i: (jnp.minimum((i + 1) * per, S // HALO - 1), 0)),
                  main, prev, nxt, pl.BlockSpec((8, D), lambda i: (0, 0))] + [ANY] * len(deps),
        out_specs=[main, pl.BlockSpec((8, D), lambda i: (0, 0))],
        out_shape=[jax.ShapeDtypeStruct((3, S, D), BF), jax.ShapeDtypeStruct((8, D), F32)],
        compiler_params=_cparams(("arbitrary",)),
    )(dm, dm, z3, z3, z3, taps, *deps)


def _t5_bucket(dist):
    exact = NUM_BUCKETS // 2
    df = jnp.maximum(dist, 1).astype(jnp.float32)
    large = exact + (jnp.log(df / exact) / math.log(MAX_DISTANCE / exact)
                     * (NUM_BUCKETS - exact)).astype(jnp.int32)
    large = jnp.minimum(large, NUM_BUCKETS - 1)
    return jnp.where(dist < exact, dist, large)


def _bucket_onehot_t():
    qi = jnp.arange(BLK)[:, None]
    ki = jnp.arange(2 * BLK)[None, :]
    rel = qi + BLK - ki
    band = ((rel >= 0) & (rel <= BLK)).reshape(1, -1).astype(F32)
    hots = []
    for d in BRANCH_DILATIONS:
        bucket = _t5_bucket(jnp.clip(rel, 0) * d).reshape(1, -1)
        hots.append((jnp.arange(NUM_BUCKETS)[:, None] == bucket).astype(F32))
    return jnp.stack(hots), band


def bias_tables(rel_bias_t, onehot_t, band, name):
    def body(rb_ref, oh_ref, band_ref, o_ref):
        b = jnp.dot(rb_ref[...], oh_ref[...], preferred_element_type=F32,
                    precision=lax.Precision.HIGHEST)
        o_ref[...] = jnp.where(band_ref[...] > 0.5, b, NEG_INF)

    n = BLK * 2 * BLK
    return pl.pallas_call(
        body, name=name,
        grid=(3,),
        in_specs=[pl.BlockSpec((H, NUM_BUCKETS), lambda g: (0, 0)),
                  pl.BlockSpec((None, NUM_BUCKETS, n), lambda g: (g, 0, 0)),
                  pl.BlockSpec((1, n), lambda g: (0, 0))],
        out_specs=pl.BlockSpec((None, H, n), lambda g: (g, 0, 0)),
        out_shape=jax.ShapeDtypeStruct((3, H, n), F32),
        compiler_params=_cparams(("parallel",)),
    )(rel_bias_t, onehot_t, band)


def _diagonal_onehot_t():
    c = jnp.arange(BLK)
    dist = jnp.concatenate([c + 1, (c + 1) % BLK])[None, :]
    hots = [(jnp.arange(NUM_BUCKETS)[:, None] == _t5_bucket(dist * d)).astype(F32)
            for d in BRANCH_DILATIONS]
    return jnp.stack(hots)


def bias_grad(dsums, onehot_t, name):
    def body(ds_ref, oh_ref, o_ref):
        @pl.when(pl.program_id(0) == 0)
        def _():
            o_ref[...] = jnp.zeros_like(o_ref)
        o_ref[...] += lax.dot_general(ds_ref[...], oh_ref[...], _NT, preferred_element_type=F32,
                                      precision=lax.Precision.HIGHEST)

    return pl.pallas_call(
        body, name=name,
        grid=(dsums.shape[0],),
        in_specs=[pl.BlockSpec((None, H, 2 * BLK), lambda g: (g, 0, 0)),
                  pl.BlockSpec((None, NUM_BUCKETS, 2 * BLK), lambda g: (g % 3, 0, 0))],
        out_specs=pl.BlockSpec((H, NUM_BUCKETS), lambda g: (0, 0)),
        out_shape=jax.ShapeDtypeStruct((H, NUM_BUCKETS), F32),
        compiler_params=_cparams(("arbitrary",)),
    )(dsums, onehot_t)


def _head_masks():
    lane = lax.broadcasted_iota(jnp.int32, (1, 2 * DH), 1)
    return (lane < DH, lane >= DH)


def _stack_heads(x, masks):
    zero = jnp.zeros_like(x)
    return jnp.concatenate([jnp.where(masks[0], x, zero), jnp.where(masks[1], x, zero)], axis=0)


def _deinterleave(src_ref, dst_ref, d, dtype):
    dst_ref[...] = pltpu.einshape("(md)c->(dm)c", src_ref[...], d=d).astype(dtype)


def _branch_loops(d, block):
    L = S // d
    for r in range(d):
        base = r * L
        block(base, base, BLK, True)
        for n in range(1, L // BLK):
            block(base + n * BLK, base + (n - 1) * BLK, 2 * BLK, False)


def attention_fwd(z3, bias3, name):
    W = 2 * DH
    CH = 256

    def body(q_ref, k_ref, v_ref, b_ref, o_ref, lse_ref, stage, qd, kd, vd, od, ld, on, ln):
        masks = _head_masks()
        for src, dst in ((q_ref, qd), (k_ref, kd), (v_ref, vd)):
            stage[...] = src[...].astype(F32)
            for gi, d in enumerate(BRANCH_DILATIONS[1:]):
                _deinterleave(stage, dst.at[gi], d, BF)

        for g, d in enumerate(BRANCH_DILATIONS):
            qs, ks, vs = (q_ref, k_ref, v_ref) if d == 1 else (qd.at[g - 1], kd.at[g - 1], vd.at[g - 1])
            o_dst, l_dst = (on.at[0], ln.at[0]) if d == 1 else (od, ld)

            def block(q0, k0, nk, first, g=g, qs=qs, ks=ks, vs=vs, o_dst=o_dst, l_dst=l_dst):
                q2 = _stack_heads(qs[pl.ds(q0, BLK), :], masks)
                kk = ks[pl.ds(k0, nk), :]
                vv = vs[pl.ds(k0, nk), :]
                bias = b_ref[g][:, BLK:] if first else b_ref[g]
                s = lax.dot_general(q2, kk, _NT, preferred_element_type=F32) * SCALE + bias
                mx = jnp.max(s, axis=1, keepdims=True)
                p = jnp.exp(s - mx)
                l = jnp.sum(p, axis=1, keepdims=True)
                o2 = jnp.dot(p.astype(BF), vv, preferred_element_type=F32) / l
                lse2 = mx + jnp.log(l)
                o_dst[pl.ds(q0, BLK), :] = jnp.where(masks[0], o2[:BLK], o2[BLK:])
                l_dst[pl.ds(q0, BLK), :] = jnp.where(masks[0], lse2[:BLK], lse2[BLK:])

            _branch_loops(d, block)
            if d > 1:
                L = S // d
                for r in range(d):
                    on[g, pl.ds(r, L, stride=d), :] = od[r * L:(r + 1) * L, :]
                    ln[g, pl.ds(r, L, stride=d), :] = ld[r * L:(r + 1) * L, :]

        def join(c, carry):
            rows = pl.ds(pl.multiple_of(c * CH, CH), CH)
            a, b, cc = ln[0, rows, :], ln[1, rows, :], ln[2, rows, :]
            mx = jnp.maximum(jnp.maximum(a, b), cc)
            ea, eb, ec = jnp.exp(a - mx), jnp.exp(b - mx), jnp.exp(cc - mx)
            tot = ea + eb + ec
            o_ref[rows, :] = ((ea * on[0, rows, :] + eb * on[1, rows, :] + ec * on[2, rows, :])
                              / tot).astype(BF)
            lse_ref[rows, :] = mx + jnp.log(tot)
            return carry
        lax.fori_loop(0, S // CH, join, 0)

    col = pl.BlockSpec((S, W), lambda hp: (0, hp))
    return pl.pallas_call(
        body, name=name,
        grid=(D // W,),
        in_specs=[pl.BlockSpec((None, S, W), lambda hp: (0, 0, hp)),
                  pl.BlockSpec((None, S, W), lambda hp: (1, 0, hp)),
                  pl.BlockSpec((None, S, W), lambda hp: (2, 0, hp)),
                  pl.BlockSpec((3, 2 * BLK, 2 * BLK), lambda hp: (0, hp, 0))],
        out_specs=[col, col],
        out_shape=[jax.ShapeDtypeStruct((S, D), BF), jax.ShapeDtypeStruct((S, D), F32)],
        scratch_shapes=[pltpu.VMEM((S, W), F32),
                        pltpu.VMEM((2, S, W), BF), pltpu.VMEM((2, S, W), BF), pltpu.VMEM((2, S, W), BF),
                        pltpu.VMEM((S, W), F32), pltpu.VMEM((S, W), F32),
                        pltpu.VMEM((3, S, W), F32), pltpu.VMEM((3, S, W), F32)],
        compiler_params=_cparams(("parallel",)),
    )(z3, z3, z3, bias3)


def attention_bwd(z3, dob, ob, lse_b, bias3, name, deps=()):
    W = 2 * DH
    CH = 256

    def body(q_ref, k_ref, v_ref, do_ref, o_ref, lse_ref, b_ref, dz_ref, dsum_ref,
             stage, rowst, qd, kd, vd, dod, rsd, res, acc, db_ref):
        masks = _head_masks()
        lane = lax.broadcasted_iota(jnp.int32, (1, W), 1)
        first_half = (lane & (DH // 2)) == 0

        def rowsum(c, carry):
            rows = pl.ds(pl.multiple_of(c * CH, CH), CH)
            prod = do_ref[rows, :].astype(F32) * o_ref[rows, :].astype(F32)
            sa = jnp.sum(jnp.where(masks[0], prod, 0.0), axis=1, keepdims=True)
            sb = jnp.sum(jnp.where(masks[1], prod, 0.0), axis=1, keepdims=True)
            rowst[rows, :] = jnp.where(first_half, lse_ref[rows, :], jnp.where(masks[0], sa, sb))
            return carry
        lax.fori_loop(0, S // CH, rowsum, 0)

        for src, dst in ((q_ref, qd), (k_ref, kd), (v_ref, vd), (do_ref, dod)):
            stage[...] = src[...].astype(F32)
            for gi, d in enumerate(BRANCH_DILATIONS[1:]):
                _deinterleave(stage, dst.at[gi], d, BF)
        for gi, d in enumerate(BRANCH_DILATIONS[1:]):
            _deinterleave(rowst, rsd.at[gi], d, F32)

        db_ref[...] = jnp.zeros_like(db_ref)
        for g, d in enumerate(BRANCH_DILATIONS):
            if d == 1:
                qs, ks, vs, dos, rs = q_ref, k_ref, v_ref, do_ref, rowst
            else:
                qs, ks, vs, dos = qd.at[g - 1], kd.at[g - 1], vd.at[g - 1], dod.at[g - 1]
                rs = rsd.at[g - 1]
            res[1] = jnp.zeros((S, W), F32)
            res[2] = jnp.zeros((S, W), F32)

            def block(q0, k0, nk, first, g=g, qs=qs, ks=ks, vs=vs, dos=dos, rs=rs):
                kk = ks[pl.ds(k0, nk), :]
                vv = vs[pl.ds(k0, nk), :]
                q2 = _stack_heads(qs[pl.ds(q0, BLK), :], masks)
                do2 = _stack_heads(dos[pl.ds(q0, BLK), :], masks)
                st = rs[pl.ds(q0, BLK), :]
                lse2 = jnp.concatenate([st[:, 0:1], st[:, DH:DH + 1]], axis=0)
                del2 = jnp.concatenate([st[:, DH // 2:DH // 2 + 1],
                                        st[:, DH + DH // 2:DH + DH // 2 + 1]], axis=0)
                bias = b_ref[g][:, BLK:] if first else b_ref[g]
                s = lax.dot_general(q2, kk, _NT, preferred_element_type=F32) * SCALE + bias
                p = jnp.exp(s - lse2)
                dp = lax.dot_general(do2, vv, _NT, preferred_element_type=F32)
                ds = p * (dp - del2)
                if first:
                    db_ref[g, :, BLK:] += ds
                else:
                    db_ref[g] += ds
                dsb = ds.astype(BF)
                dq2 = jnp.dot(dsb, kk, preferred_element_type=F32) * SCALE
                res[0, pl.ds(q0, BLK), :] = jnp.where(masks[0], dq2[:BLK], dq2[BLK:])
                res[1, pl.ds(k0, nk), :] += lax.dot_general(dsb, q2, _TN,
                                                            preferred_element_type=F32) * SCALE
                res[2, pl.ds(k0, nk), :] += lax.dot_general(p.astype(BF), do2, _TN,
                                                            preferred_element_type=F32)

            _branch_loops(d, block)
            L = S // d
            for t in range(3):
                if d == 1:
                    acc[t] = res[t]
                else:
                    for r in range(d):
                        acc[t, pl.ds(r, L, stride=d), :] = (acc[t, pl.ds(r, L, stride=d), :]
                                                            + res[t, r * L:(r + 1) * L, :])
        for t in range(3):
            dz_ref[t] = acc[t].astype(BF)

        flip = (lax.broadcasted_iota(jnp.int32, (BLK, BLK), 0)
                + lax.broadcasted_iota(jnp.int32, (BLK, BLK), 1) == BLK - 1).astype(BF)
        dsum_ref[...] = jnp.zeros_like(dsum_ref)
        for g in range(3):
            for hh in range(2):
                halves = []
                for half in range(2):
                    tile = db_ref[g, hh * BLK:(hh + 1) * BLK, half * BLK:(half + 1) * BLK]
                    hi = tile.astype(BF)
                    lo = (tile - hi.astype(F32)).astype(BF)
                    rev = (jnp.dot(hi, flip, preferred_element_type=F32)
                           + jnp.dot(lo, flip, preferred_element_type=F32))
                    skew = pltpu.roll(rev, 0, 1, stride=1, stride_axis=0)
                    halves.append(jnp.sum(skew, axis=0, keepdims=True))
                dsum_ref[2 * g + hh:2 * g + hh + 1, :] = jnp.concatenate(halves, axis=1)

    col = pl.BlockSpec((S, W), lambda hp: (0, hp))
    return pl.pallas_call(
        _after(body, 7, deps), name=name,
        grid=(D // W,),
        in_specs=[pl.BlockSpec((None, S, W), lambda hp: (0, 0, hp)),
                  pl.BlockSpec((None, S, W), lambda hp: (1, 0, hp)),
                  pl.BlockSpec((None, S, W), lambda hp: (2, 0, hp)),
                  col, col, col,
                  pl.BlockSpec((3, 2 * BLK, 2 * BLK), lambda hp: (0, hp, 0))] + [ANY] * len(deps),
        out_specs=[pl.BlockSpec((3, S, W), lambda hp: (0, 0, hp)),
                   pl.BlockSpec((None, 8, 2 * BLK), lambda hp: (hp, 0, 0))],
        out_shape=[jax.ShapeDtypeStruct((3, S, D), BF),
                   jax.ShapeDtypeStruct((D // W, 8, 2 * BLK), F32)],
        scratch_shapes=[pltpu.VMEM((S, W), F32), pltpu.VMEM((S, W), F32),
                        pltpu.VMEM((2, S, W), BF), pltpu.VMEM((2, S, W), BF),
                        pltpu.VMEM((2, S, W), BF), pltpu.VMEM((2, S, W), BF),
                        pltpu.VMEM((2, S, W), F32),
                        pltpu.VMEM((3, S, W), F32), pltpu.VMEM((3, S, W), F32),
                        pltpu.VMEM((3, 2 * BLK, 2 * BLK), F32)],
        compiler_params=_cparams(("parallel",)),
    )(z3, z3, z3, dob, ob, lse_b, bias3, *deps)


def _me():
    return lax.axis_index("x"), lax.axis_index("y"), lax.axis_index("c")


def _other_chips(x, y):
    return [(1 - x, y), (x, 1 - y), (1 - x, 1 - y)]


def _shard_window(ref, axis, t, shape):
    R, C = shape
    if axis == 0:
        return ref.at[pl.ds(pl.multiple_of(t * R, 128), R), :]
    return ref.at[:, pl.ds(pl.multiple_of(t * C, 128), C)]


def all_gather_weights(shards, axes, name):
    n = len(shards)
    shapes = [s.shape for s in shards]
    outs_shape = [jax.ShapeDtypeStruct((8 * s.shape[0], s.shape[1]) if ax == 0
                                       else (s.shape[0], 8 * s.shape[1]), s.dtype)
                  for s, ax in zip(shards, axes)]

    def body(*refs):
        ins, outs = refs[:n], refs[n:2 * n]
        send_sems, recv_sems, local_sems = refs[2 * n:]
        x, y, c = _me()
        me, sibling = (x, y, c), (x, y, 1 - c)
        xnb, ynb, diag = (1 - x, y), (x, 1 - y), (1 - x, 1 - y)
        south = c == 0
        relay_from = (jnp.where(south, x, 1 - x), jnp.where(south, 1 - y, y))
        relay_to = (jnp.where(south, 1 - x, x), jnp.where(south, y, 1 - y))
        barrier = pltpu.get_barrier_semaphore()
        for peer in [sibling, (*xnb, c), (*ynb, c)]:
            pl.semaphore_signal(barrier, inc=1, device_id=peer, device_id_type=MESH)
        pl.semaphore_wait(barrier, 3)

        def win(i, px, py, pc):
            return _shard_window(outs[i], axes[i], 4 * px + 2 * py + pc, shapes[i])

        def copy(i, k, block, to, src=None):
            return pltpu.make_async_remote_copy(
                src_ref=win(i, *block) if src is None else src, dst_ref=win(i, *block),
                send_sem=send_sems.at[i * 7 + k], recv_sem=recv_sems.at[i * 7 + k],
                device_id=to, device_id_type=MESH)

        mine = [pltpu.make_async_copy(ins[i], win(i, *me), local_sems.at[i]) for i in range(n)]
        for cp in mine:
            cp.start()
        sent = []
        for i in range(n):
            sent += [copy(i, 0, me, sibling, src=ins[i]), copy(i, 1, me, (*xnb, c), src=ins[i]),
                     copy(i, 2, me, (*ynb, c), src=ins[i])]
        for cp in sent:
            cp.start()
        for i in range(n):
            for k, chip in ((1, xnb), (2, ynb)):
                copy(i, k, (*chip, c), me).wait_recv()
                sent.append(copy(i, 3 + k, (*chip, c), sibling))
                sent[-1].start()
            sent.append(copy(i, 3, (*relay_from, c), (*relay_to, c)))
            sent[-1].start()
        for i in range(n):
            copy(i, 3, (*diag, c), me).wait_recv()
            sent.append(copy(i, 6, (*diag, c), sibling))
            sent[-1].start()
        for i in range(n):
            copy(i, 0, sibling, me).wait_recv()
            for k, chip in ((4, xnb), (5, ynb), (6, diag)):
                copy(i, k, (*chip, 1 - c), me).wait_recv()
        for cp in sent:
            cp.wait_send()
        for cp in mine:
            cp.wait()

    return pl.kernel(
        body, out_type=outs_shape, name=name,
        mesh=plsc.ScalarSubcoreMesh(axis_name="sequencer", num_cores=1),
        scratch_types=[pltpu.SemaphoreType.DMA((7 * n,)), pltpu.SemaphoreType.DMA((7 * n,)),
                       pltpu.SemaphoreType.DMA((n,))],
        compiler_params=pltpu.CompilerParams(collective_id=1),
    )(*shards)


def pair_exchange_grads(grads, axes, shapes, name):
    n = len(grads)

    def body(*refs):
        ins, outs = refs[:n], refs[n:2 * n]
        send_sems, recv_sems = refs[2 * n:]
        x, y, c = _me()
        sibling = (x, y, 1 - c)
        barrier = pltpu.get_barrier_semaphore()
        pl.semaphore_signal(barrier, inc=1, device_id=sibling, device_id_type=MESH)
        pl.semaphore_wait(barrier, 1)
        copies = []
        for i in range(n):
            for q in range(4):
                t = 2 * q + (1 - c)
                copies.append(pltpu.make_async_remote_copy(
                    src_ref=_shard_window(ins[i], axes[i], t, shapes[i]), dst_ref=outs[i].at[q],
                    send_sem=send_sems.at[i * 4 + q], recv_sem=recv_sems.at[i * 4 + q],
                    device_id=sibling, device_id_type=MESH))
        for cp in copies:
            cp.start()
        for cp in copies:
            cp.wait_recv()
        for cp in copies:
            cp.wait_send()

    return pl.kernel(
        body, out_type=[jax.ShapeDtypeStruct((4,) + tuple(sh), BF) for sh in shapes], name=name,
        mesh=plsc.ScalarSubcoreMesh(axis_name="sequencer", num_cores=1),
        scratch_types=[pltpu.SemaphoreType.DMA((4 * n,)), pltpu.SemaphoreType.DMA((4 * n,))],
        compiler_params=pltpu.CompilerParams(collective_id=2),
    )(*grads)


def pair_add(grads, landed, axes, shapes, c_idx, name, deps=()):
    n = len(grads)

    def body(c_ref, *refs):
        for t in range(n):
            refs[2 * n + t][...] = (refs[2 * t][...].astype(F32)
                                    + refs[2 * t + 1][...].astype(F32)).astype(BF)

    halves = 1
    in_specs, out_specs, out_shapes, operands = [], [], [], []
    for t in range(n):
        R, C = shapes[t]
        rh = R // halves
        if axes[t] == 0:
            in_specs.append(pl.BlockSpec(
                (rh, C), lambda q, h, c_ref: (halves * (2 * q + c_ref[0]) + h, 0)))
        else:
            in_specs.append(pl.BlockSpec((rh, C), lambda q, h, c_ref: (h, 2 * q + c_ref[0])))
        blk = pl.BlockSpec((None, rh, C), lambda q, h, c_ref: (q, h, 0))
        in_specs.append(blk)
        out_specs.append(blk)
        out_shapes.append(jax.ShapeDtypeStruct((4, R, C), BF))
        operands += [grads[t], landed[t]]
    return pl.pallas_call(
        _after(body, 1 + 2 * n, deps), name=name,
        grid_spec=pltpu.PrefetchScalarGridSpec(
            num_scalar_prefetch=1, grid=(4, halves), in_specs=in_specs + [ANY] * len(deps),
            out_specs=out_specs),
        out_shape=out_shapes,
        compiler_params=_cparams(("parallel", "parallel")),
    )(c_idx, *operands, *deps)


def chip_exchange_grads(parts, name):
    n = len(parts)

    def body(*refs):
        ins, outs, relay = refs[:n], refs[n:2 * n], refs[2 * n:3 * n]
        send_sems, recv_sems = refs[3 * n:]
        x, y, c = _me()
        xnb, ynb, diag = (1 - x, y), (x, 1 - y), (1 - x, 1 - y)
        south = c == 0
        via = (jnp.where(south, 1 - x, x), jnp.where(south, y, 1 - y))
        onward = (jnp.where(south, x, 1 - x), jnp.where(south, 1 - y, y))
        barrier = pltpu.get_barrier_semaphore()
        for peer in (xnb, ynb):
            pl.semaphore_signal(barrier, inc=1, device_id=(*peer, c), device_id_type=MESH)
        pl.semaphore_wait(barrier, 2)

        def copy(i, k, src, dst, to):
            return pltpu.make_async_remote_copy(
                src_ref=src, dst_ref=dst, send_sem=send_sems.at[i * 4 + k],
                recv_sem=recv_sems.at[i * 4 + k], device_id=(*to, c), device_id_type=MESH)

        sent = []
        for i in range(n):
            sent += [copy(i, 0, ins[i].at[2 * xnb[0] + xnb[1]], outs[i].at[0], xnb),
                     copy(i, 1, ins[i].at[2 * ynb[0] + ynb[1]], outs[i].at[1], ynb),
                     copy(i, 2, ins[i].at[2 * diag[0] + diag[1]], relay[i], via)]
        for cp in sent:
            cp.start()
        for i in range(n):
            copy(i, 2, relay[i], relay[i], via).wait_recv()
            sent.append(copy(i, 3, relay[i], outs[i].at[2], onward))
            sent[-1].start()
        for i in range(n):
            copy(i, 0, outs[i].at[0], outs[i].at[0], xnb).wait_recv()
            copy(i, 1, outs[i].at[1], outs[i].at[1], ynb).wait_recv()
            copy(i, 3, outs[i].at[2], outs[i].at[2], onward).wait_recv()
        for cp in sent:
            cp.wait_send()

    landing = [jax.ShapeDtypeStruct((3,) + tuple(p.shape[1:]), BF) for p in parts]
    staging = [jax.ShapeDtypeStruct(tuple(p.shape[1:]), BF) for p in parts]
    return pl.kernel(
        body, out_type=landing + staging, name=name,
        mesh=plsc.ScalarSubcoreMesh(axis_name="sequencer", num_cores=1),
        scratch_types=[pltpu.SemaphoreType.DMA((4 * n,)), pltpu.SemaphoreType.DMA((4 * n,))],
        compiler_params=pltpu.CompilerParams(collective_id=3),
    )(*parts)[:n]


def all_gather_small(v, name):
    R, C = v.shape

    def body(v_ref, out_ref, send_sems, recv_sems, local_sem):
        x, y, c = _me()
        me, sibling = (x, y, c), (x, y, 1 - c)
        chips = _other_chips(x, y)

        def slot(px, py, pc):
            return out_ref.at[4 * px + 2 * py + pc]

        def copy(k, block, to, src=None):
            return pltpu.make_async_remote_copy(
                src_ref=slot(*block) if src is None else src, dst_ref=slot(*block),
                send_sem=send_sems.at[k], recv_sem=recv_sems.at[k],
                device_id=to, device_id_type=MESH)

        mine = pltpu.make_async_copy(v_ref, slot(*me), local_sem)
        mine.start()
        first = [copy(0, me, sibling, src=v_ref)]
        first += [copy(1 + j, me, (*chip, c), src=v_ref) for j, chip in enumerate(chips)]
        for cp in first:
            cp.start()
        passed = [copy(4 + j, (*chip, c), sibling) for j, chip in enumerate(chips)]
        for j, chip in enumerate(chips):
            copy(1 + j, (*chip, c), me).wait_recv()
            passed[j].start()
        copy(0, sibling, me).wait_recv()
        for j, chip in enumerate(chips):
            copy(4 + j, (*chip, 1 - c), me).wait_recv()
        for cp in first + passed:
            cp.wait_send()
        mine.wait()

    return pl.pallas_call(
        body, name=name,
        in_specs=[pl.BlockSpec(memory_space=pltpu.VMEM)],
        out_specs=pl.BlockSpec(memory_space=pltpu.VMEM),
        out_shape=jax.ShapeDtypeStruct((NDEV, R, C), F32),
        scratch_shapes=[pltpu.SemaphoreType.DMA((7,)), pltpu.SemaphoreType.DMA((7,)),
                        pltpu.SemaphoreType.DMA],
    )(v)


def _adamw(w, g, m, v):
    m = ADAM_B1 * m + (1.0 - ADAM_B1) * g
    v = ADAM_B2 * v + (1.0 - ADAM_B2) * (g * g)
    m_hat = m / (1.0 - ADAM_B1 ** ADAM_STEP)
    v_hat = v / (1.0 - ADAM_B2 ** ADAM_STEP)
    delta = -ADAM_LR * (m_hat / (jnp.sqrt(v_hat) + ADAM_EPS) + ADAM_WD * w)
    return delta, m, v


def reduce_adamw(parts, landed, params, q_idx, name, prevs, deps=()):
    n = len(parts)
    halves = 2
    in_specs, out_specs, out_shapes, operands, extra, aliases = [], [], [], [], [], {}
    for t in range(n):
        R, C = parts[t].shape[1:]
        w, m, v, layer = params[t]
        r, c = w.shape[1:]
        tr = r // halves
        assert tr % 16 == 0 and c == C
        wspec = pl.BlockSpec((None, tr, c), lambda i, q_ref, layer=layer: (layer, i, 0))
        in_specs += [pl.BlockSpec((None, tr, C), lambda i, q_ref: (q_ref[0], i, 0)),
                     pl.BlockSpec((3, tr, C), lambda i, q_ref: (0, i, 0)), wspec, wspec, wspec]
        out_specs += [wspec] * 4
        out_shapes += [jax.ShapeDtypeStruct(w.shape, F32)] * 4
        operands += [parts[t], landed[t], w, m, v]
        for k, buf in enumerate(prevs[t]):
            aliases[1 + 5 * n + len(extra)] = 4 * t + k
            extra.append(buf)
    extra += list(deps)

    def body(q_ref, *refs):
        for t in range(n):
            p_ref, l_ref, w_ref, m_ref, v_ref = refs[5 * t:5 * t + 5]
            g = p_ref[...].astype(F32)
            for k in range(3):
                g = g + l_ref[k].astype(F32)
            d, mm, vv = _adamw(w_ref[...], g, m_ref[...], v_ref[...])
            outs = refs[5 * n + 4 * t:5 * n + 4 * t + 4]
            outs[0][...] = g
            outs[1][...] = d
            outs[2][...] = mm
            outs[3][...] = vv

    res = pl.pallas_call(
        _after(body, 1 + 5 * n, extra), name=name,
        grid_spec=pltpu.PrefetchScalarGridSpec(
            num_scalar_prefetch=1, grid=(halves,),
            in_specs=in_specs + [ANY] * len(extra), out_specs=out_specs),
        out_shape=out_shapes,
        input_output_aliases=aliases,
        compiler_params=_cparams(("parallel",)),
    )(q_idx, *operands, *extra)
    return [res[4 * t:4 * t + 4] for t in range(n)]


def small_reduce_adamw(gathered, w, m, v, name):
    R, C = w.shape

    def body(a_ref, w_ref, m_ref, v_ref, g_out, d_out, m_out, v_out):
        g = a_ref[0]
        for k in range(1, NDEV):
            g = g + a_ref[k]
        d, mm, vv = _adamw(w_ref[...], g, m_ref[...], v_ref[...])
        g_out[...] = g
        d_out[...] = d
        m_out[...] = mm
        v_out[...] = vv

    out = jax.ShapeDtypeStruct((R, C), F32)
    return pl.pallas_call(body, name=name, out_shape=[out] * 4,
                          compiler_params=_cparams())(gathered, w, m, v)


def _pad_rows(a, n):
    return jnp.pad(a, ((0, n - a.shape[0]), (0, 0)))


def _pack_small(mix, ffn, fin, taps_full, relb, scalar=0.0):
    n = NUM_BUCKETS * H
    last = jnp.concatenate([relb.reshape(1, n), jnp.reshape(scalar, (1, 1)).astype(F32),
                            jnp.zeros((1, D - n - 1), F32)], axis=1)
    return jnp.concatenate([mix, ffn, fin.reshape(1, D), taps_full.reshape(6, D), last], axis=0)


def kernel(x, mix_norm, ffn_norm, final_norm, conv_w_in, conv_kernel, conv_w_out, attn_w_qkv, attn_w_out, rel_bias, ffn_w_gate, ffn_w_up, ffn_w_down, loss_target, m_mix_norm, m_ffn_norm, m_final_norm, m_conv_w_in, m_conv_kernel, m_conv_w_out, m_attn_w_qkv, m_attn_w_out, m_rel_bias, m_ffn_w_gate, m_ffn_w_up, m_ffn_w_down, v_mix_norm, v_ffn_norm, v_final_norm, v_conv_w_in, v_conv_kernel, v_conv_w_out, v_attn_w_qkv, v_attn_w_out, v_rel_bias, v_ffn_w_gate, v_ffn_w_up, v_ffn_w_down):
    xi, yi, ci = _me()
    me = 4 * xi + 2 * yi + ci
    c_idx = jnp.reshape(ci, (1,)).astype(jnp.int32)
    q_idx = jnp.reshape(2 * xi + yi, (1,)).astype(jnp.int32)
    col0 = me * (D // NDEV)

    gate_t, up_t = jnp.swapaxes(ffn_w_gate, 1, 2), jnp.swapaxes(ffn_w_up, 1, 2)
    m_gate_t, m_up_t = jnp.swapaxes(m_ffn_w_gate, 1, 2), jnp.swapaxes(m_ffn_w_up, 1, 2)
    v_gate_t, v_up_t = jnp.swapaxes(v_ffn_w_gate, 1, 2), jnp.swapaxes(v_ffn_w_up, 1, 2)

    mixer_in = (conv_w_in, attn_w_qkv)
    mixer_out = (conv_w_out, attn_w_out)
    taps_shard = jnp.pad(conv_kernel.reshape(6, D // NDEV), ((0, 2), (0, 0)))
    wts = []
    for i in range(DEPTH):
        j = i // 2
        shards = [mixer_in[i % 2][j].astype(BF), mixer_out[i % 2][j].astype(BF),
                  _pad_rows(gate_t[i].astype(BF), FF_SHARD_PAD),
                  _pad_rows(up_t[i].astype(BF), FF_SHARD_PAD),
                  _pad_rows(ffn_w_down[i].astype(BF), FF_SHARD_PAD)]
        axes = (1, 0, 0, 0, 0)
        groups = ((0, 1), (1, 2), (2, 4), (4, 5)) if i == 0 else ((0, 2), (2, 5))
        layer = []
        for lo, hi in groups:
            extra = [taps_shard] if (i, lo) == (0, 1) else []
            got = list(all_gather_weights(shards[lo:hi] + extra, axes[lo:hi] + (1,) * len(extra),
                                          f"ag_l{i}_{lo}"))
            if extra:
                taps_all = got.pop()
            layer += got
        wts.append(layer)
    taps = [jnp.pad(taps_all[3 * j:3 * j + 3], ((0, 5), (0, 0))) for j in range(2)]

    onehot_t, band = _bucket_onehot_t()
    bias3 = bias_tables(rel_bias.T, onehot_t, band, "bias_tables").reshape(3, H * BLK, 2 * BLK)

    mix_gains, ffn_gains = mix_norm.reshape(DEPTH, 1, D), ffn_norm.reshape(DEPTH, 1, D)
    saved = []
    xc = x[0]
    for i in range(DEPTH):
        w_in, w_out, w_g, w_u, w_d = wts[i]
        j = i // 2
        x_mix = xc
        z3, h_mix = norm_matmul3(xc, (mix_gains, i), w_in, f"mix_in_l{i}")
        if i % 2 == 0:
            act = conv_fwd(z3, taps[j], f"conv_fwd_l{i}")
            lse_b = None
        else:
            act, lse_b = attention_fwd(z3, bias3, f"attn_fwd_l{i}")
        xc = matmul_residual(act, w_out, xc, f"mix_out_l{i}")
        x_ffn = xc
        g, u, a, h_ffn = norm_swiglu_up(xc, (ffn_gains, i), w_g, w_u, f"ffn_up_l{i}")
        xc = matmul_residual(a, w_d, xc, f"ffn_down_l{i}")
        saved.append((x_mix, h_mix, z3, act, lse_b, x_ffn, h_ffn, g, u, a))

    dx, dxb, dg_final, sq = loss_head(xc, final_norm.reshape(1, D), loss_target[0], "loss_head")
    loss_local = 0.5 * jnp.sum(sq[0]) / D

    dg_mix = [None] * DEPTH
    dg_ffn = [None] * DEPTH
    dtaps = [None, None]
    dbias_all = []
    shape_in, shape_out = (D, 3 * D // NDEV), (D // NDEV, D)
    ffn_axes, ffn_shapes = (0, 0, 0), ((FF_SHARD_PAD, D),) * 3
    stacked = {}

    def pair_stage(grads, landed1, axes, shapes, tag, tok):
        parts = pair_add(grads, landed1, axes, shapes, c_idx, f"rs_add_{tag}", deps=[tok])
        return parts, chip_exchange_grads(parts, f"rs_chip_{tag}"), parts[-1]

    def adamw_stage(parts, landed2, params, tag, tok):
        names = [p[0] for p in params]
        res = reduce_adamw(parts, landed2, [p[1:] for p in params], q_idx, f"adamw_{tag}",
                           [stacked.get(nm, ()) for nm in names], deps=[tok])
        for nm, r4 in zip(names, res):
            stacked[nm] = r4
        return res[-1][0]

    tok = dxb
    mix_wait = None
    mix_chip = None
    ffn_chip = []
    for i in reversed(range(DEPTH)):
        w_in, w_out, w_g, w_u, w_d = wts[i]
        j = i // 2
        x_mix, h_mix, z3, act, lse_b, x_ffn, h_ffn, g, u, a = saved[i]
        ffn_params = [("ffn_w_gate", gate_t, m_gate_t, v_gate_t, i),
                      ("ffn_w_up", up_t, m_up_t, v_up_t, i),
                      ("ffn_w_down", ffn_w_down, m_ffn_w_down, v_ffn_w_down, i)]
        if i % 2 == 0:
            mix_params = [("conv_w_in", conv_w_in, m_conv_w_in, v_conv_w_in, j),
                          ("conv_w_out", conv_w_out, m_conv_w_out, v_conv_w_out, j)]
        else:
            mix_params = [("attn_w_qkv", attn_w_qkv, m_attn_w_qkv, v_attn_w_qkv, j),
                          ("attn_w_out", attn_w_out, m_attn_w_out, v_attn_w_out, j)]
        dgate, dup = swiglu_bwd_da(dxb, w_d, g, u, f"ffn_da_l{i}", deps=[tok])
        tok = dgate
        for group in ffn_chip:
            tok = adamw_stage(*group, tok)
        ffn_chip = []
        if mix_wait is not None:
            grads_m, landed1_m, params_m, tag_m = mix_wait
            parts_m, landed2_m, tok = pair_stage(grads_m, landed1_m, (1, 0), (shape_in, shape_out),
                                                 tag_m, tok)
            mix_chip = (parts_m, landed2_m, params_m, tag_m)
            mix_wait = None
        products = [(dgate, h_ffn), (dup, h_ffn), (a, dxb)]
        ffn_wait = []
        for gi, idxs in enumerate(((0,), (1, 2)) if i == 0 else ((0, 1, 2),)):
            grads_f = matmul_tn_group([products[k] for k in idxs], f"ffn_dw_l{i}_{gi}", deps=[tok])
            landed1_f = pair_exchange_grads(grads_f, ffn_axes[:len(idxs)], ffn_shapes[:len(idxs)],
                                            f"rs_pair_f{i}_{gi}")
            ffn_wait.append((grads_f, landed1_f, [ffn_params[k] for k in idxs], f"f{i}_{gi}"))
            tok = grads_f[-1]

        def ffn_pair_stages(groups, tok):
            for grads_f, landed1_f, params_f, tag_f in groups:
                nf = len(grads_f)
                parts_f, landed2_f, tok = pair_stage(grads_f, landed1_f, ffn_axes[:nf],
                                                     ffn_shapes[:nf], tag_f, tok)
                ffn_chip.append((parts_f, landed2_f, params_f, tag_f))
            return tok

        if len(ffn_wait) > 1:
            if mix_chip is not None:
                tok = adamw_stage(*mix_chip, tok)
                mix_chip = None
            tok = ffn_pair_stages(ffn_wait[:-1], tok)
        dx, dxb, dg_ffn[i] = matmul_normbwd(
            [(dgate, w_g, False), (dup, w_u, False)], x_ffn, (ffn_gains, i), dx, f"ffn_dh_l{i}",
            deps=[tok])
        dxb_mix = dxb
        tok = dxb
        if mix_chip is not None:
            tok = adamw_stage(*mix_chip, tok)
            mix_chip = None
        tok = ffn_pair_stages(ffn_wait[-1:], tok)
        dact = matmul_nt(dxb, w_out, f"mix_dact_l{i}", out_dtype=F32 if i % 2 == 0 else BF,
                         deps=[tok])
        if i % 2 == 0:
            dz3, dtaps[j] = conv_bwd(dact, z3, taps[j], f"conv_bwd_l{i}")
        else:
            dz3, dsum = attention_bwd(z3, dact, act, lse_b, bias3, f"attn_bwd_l{i}")
            dbias_all.append(dsum[:, :6].reshape(H // 2, 3, 2, 2 * BLK).transpose(1, 0, 2, 3)
                             .reshape(3, H, 2 * BLK))
        grads_m = matmul_tn_group([(h_mix, dz3), (act, dxb_mix)], f"mix_dw_l{i}")
        landed1_m = pair_exchange_grads(grads_m, (1, 0), (shape_in, shape_out), f"rs_pair_m{i}")
        mix_wait = (grads_m, landed1_m, mix_params, f"m{i}")
        dx, dxb, dg_mix[i] = matmul_normbwd(
            [(dz3, w_in, True)], x_mix, (mix_gains, i), dx, f"mix_dh_l{i}", deps=[grads_m[-1]])
        tok = dxb
    for group in ffn_chip:
        tok = adamw_stage(*group, tok)
    grads_m, landed1_m, params_m, tag_m = mix_wait
    parts_m, landed2_m, tok = pair_stage(grads_m, landed1_m, (1, 0), (shape_in, shape_out), tag_m, tok)

    grad_relb_t = bias_grad(jnp.concatenate(dbias_all), _diagonal_onehot_t(), "bias_grad")
    dtaps_full = jnp.stack([dtaps[0][:3], dtaps[1][:3]])
    g_small = _pack_small(jnp.concatenate([d[0:1] for d in dg_mix], axis=0),
                          jnp.concatenate([d[0:1] for d in dg_ffn], axis=0),
                          dg_final[0], dtaps_full, grad_relb_t.T, loss_local)
    gathered = all_gather_small(g_small, "ag_small_grads")

    def taps_at_cols(k):
        return lax.dynamic_update_slice(jnp.zeros((2, 3, D), F32), k, (0, 0, col0))

    w_small = _pack_small(mix_norm, ffn_norm, final_norm, taps_at_cols(conv_kernel), rel_bias)
    m_small = _pack_small(m_mix_norm, m_ffn_norm, m_final_norm, taps_at_cols(m_conv_kernel), m_rel_bias)
    v_small = _pack_small(v_mix_norm, v_ffn_norm, v_final_norm, taps_at_cols(v_conv_kernel), v_rel_bias)
    small = small_reduce_adamw(gathered, w_small, m_small, v_small, "adamw_small")

    def unpack_small(p):
        taps_p = lax.dynamic_slice(p[9:15].reshape(2, 3, D), (0, 0, col0), (2, 3, D // NDEV))
        return {"mix_norm": p[0:4], "ffn_norm": p[4:8], "final_norm": p[8],
                "conv_kernel": taps_p, "rel_bias": p[15, :NUM_BUCKETS * H].reshape(NUM_BUCKETS, H)}

    small_out = [unpack_small(p) for p in small]
    loss = small[0][15, NUM_BUCKETS * H]
    adamw_stage(parts_m, landed2_m, params_m, tag_m, small[0])

    names = ["mix_norm", "ffn_norm", "final_norm", "conv_w_in", "conv_kernel", "conv_w_out",
             "attn_w_qkv", "attn_w_out", "rel_bias", "ffn_w_gate", "ffn_w_up", "ffn_w_down"]
    outs = [loss, dx.reshape(1, S, D)]
    for o in range(4):
        for nme in names:
            if nme in ("ffn_w_gate", "ffn_w_up"):
                outs.append(jnp.swapaxes(stacked[nme][o], 1, 2))
            else:
                outs.append(stacked[nme][o] if nme in stacked else small_out[o][nme])
    return tuple(outs)
```
